```python
import jax
import jax.numpy as jnp
from jax import lax
import numpy as np

D_MODEL = 1024
BATCH = 4
SEQ = 4096
DEPTH = 2

CTX_LEN = 256
GRID_W = 64
HEAD_DIM = 64
GROUP_W = D_MODEL // 4
N_FOURIER_HEADS = GROUP_W // HEAD_DIM
N_Q_HEADS = GROUP_W // HEAD_DIM
N_KV_HEADS = 2
Q_PER_KV = N_Q_HEADS // N_KV_HEADS
KV_W = N_KV_HEADS * HEAD_DIM
WINDOW = 128
ATTN_BLOCK = 128
POOL_WINDOWS = (2, 4, 8, 16)
POOL_CH = GROUP_W // len(POOL_WINDOWS)
N_SGU_HEADS = 4
SGU_HEAD = GROUP_W // N_SGU_HEADS
SGU_CHUNK = 128
N_EXPERT_GROUPS = 4
EXPERTS_PER_GROUP = 8
N_EXPERTS = N_EXPERT_GROUPS * EXPERTS_PER_GROUP
TOP_K_FINE = 2
D_EXPERT = D_MODEL // 2
MOE_BLOCK = 128
ROPE_BASE = 10000.0
LN_EPS = 1e-6
NEG_INF = -1e30
RES_ALPHA = (2 * DEPTH) ** 0.25
INIT_BETA = (8 * DEPTH) ** -0.25

COL_FOURIER = 0
COL_Q = COL_FOURIER + GROUP_W
COL_K = COL_Q + N_Q_HEADS * HEAD_DIM
COL_V = COL_K + KV_W
COL_POOL = COL_V + KV_W
COL_U = COL_POOL + GROUP_W
COL_G = COL_U + GROUP_W
D_IN = COL_G + GROUP_W

kernel_name = 'hybrid_parallel_heads_flow_backbone'


def layer_norm(t, gain=None, bias=None):
    tf = t.astype(jnp.float32)
    mu = tf.mean(-1, keepdims=True)
    var = jnp.square(tf - mu).mean(-1, keepdims=True)
    y = (tf - mu) * lax.rsqrt(var + LN_EPS)
    if gain is not None:
        y = y * gain.astype(jnp.float32) + bias.astype(jnp.float32)
    return y.astype(t.dtype)


def modulate(t, shift, scale):
    return layer_norm(t) * (1 + scale) + shift


def axial_rope_tables(n_tokens):
    rows = n_tokens // GRID_W
    row = jnp.repeat(jnp.arange(rows), GRID_W).astype(jnp.float32)
    col = jnp.tile(jnp.arange(GRID_W), rows).astype(jnp.float32)
    n_freq = HEAD_DIM // 4
    freq = ROPE_BASE ** (-jnp.arange(n_freq, dtype=jnp.float32) / n_freq)
    ang = jnp.stack([row[:, None] * freq, col[:, None] * freq], axis=1)
    return jnp.cos(ang), jnp.sin(ang)


def apply_axial_rope(t, cos, sin):
    b, s, h, d = t.shape
    tf = t.astype(jnp.float32).reshape(b, s, h, 2, 2, d // 4)
    c = cos[None, :, None]
    sn = sin[None, :, None]
    t1, t2 = tf[..., 0, :], tf[..., 1, :]
    out = jnp.stack([t1 * c - t2 * sn, t1 * sn + t2 * c], axis=-2)
    return out.reshape(b, s, h, d).astype(t.dtype)


def to_heads(t, n_heads):
    return t.reshape(t.shape[0], t.shape[1], n_heads, HEAD_DIM)


def split_projection(z):
    return (z[..., COL_FOURIER:COL_Q],
            to_heads(z[..., COL_Q:COL_K], N_Q_HEADS),
            to_heads(z[..., COL_K:COL_V], N_KV_HEADS),
            to_heads(z[..., COL_V:COL_POOL], N_KV_HEADS),
            z[..., COL_POOL:COL_U],
            z[..., COL_U:COL_G],
            z[..., COL_G:D_IN])


def fourier_mix(a, w_fourier):
    b, n, _ = a.shape
    ah = a.astype(jnp.float32).reshape(b, n, N_FOURIER_HEADS, HEAD_DIM)
    f = jnp.fft.fft2(ah, axes=(1, 3), norm='ortho').real
    return f.reshape(b, n, GROUP_W).astype(a.dtype) @ w_fourier


def pool_mix(p, w_pool, pool_scale):
    b, n, _ = p.shape
    pf = p.astype(jnp.float32)
    cs = jnp.pad(jnp.cumsum(pf, axis=1), ((0, 0), (1, 0), (0, 0)))
    pos = jnp.arange(n)
    means = []
    for g, w in enumerate(POOL_WINDOWS):
        lo = jnp.clip(pos - w // 2, 0, n)
        hi = jnp.clip(pos + w // 2, 0, n)
        csg = cs[..., g * POOL_CH:(g + 1) * POOL_CH]
        means.append((csg[:, hi] - csg[:, lo]) / (hi - lo).astype(jnp.float32)[:, None])
    pooled = jnp.stack(means, axis=2) - pf.reshape(b, n, len(POOL_WINDOWS), POOL_CH)
    mixed = jnp.einsum('bngc,gcd->bngd', pooled.astype(p.dtype), w_pool)
    return mixed.reshape(b, n, GROUP_W) * pool_scale


def spatial_gating(u, g, w_sgu, b_sgu):
    b, n, _ = u.shape
    u = jax.nn.gelu(u, approximate=False)
    v = layer_norm(jax.nn.gelu(g, approximate=False))
    vh = v.reshape(b, n // SGU_CHUNK, SGU_CHUNK, N_SGU_HEADS, SGU_HEAD)
    mixed = jnp.einsum('hpq,bcqhd->bcphd', w_sgu, vh) + b_sgu.T[None, None, :, :, None]
    return u * mixed.reshape(b, n, GROUP_W)


def window_attention(q, k, v, kc, vc, sink):
    b, s, _, d = q.shape
    nb = s // ATTN_BLOCK
    n_ctx = kc.shape[1]
    scale = d ** -0.5
    qb = q.reshape(b, nb, ATTN_BLOCK, N_KV_HEADS, Q_PER_KV, d)

    def band(t):
        tp = jnp.pad(t, ((0, 0), (ATTN_BLOCK, ATTN_BLOCK), (0, 0), (0, 0)))
        tp = tp.reshape(b, nb + 2, ATTN_BLOCK, N_KV_HEADS, d)
        return jnp.concatenate([tp[:, :-2], tp[:, 1:-1], tp[:, 2:]], axis=2)

    kw, vw = band(k), band(v)
    s_loc = jnp.einsum('bnikgd,bnjkd->bnkgij', qb, kw).astype(jnp.float32) * scale
    s_ctx = jnp.einsum('bnikgd,bckd->bnkgic', qb, kc).astype(jnp.float32) * scale
    blk = jnp.arange(nb)[:, None, None]
    qpos = blk * ATTN_BLOCK + jnp.arange(ATTN_BLOCK)[None, :, None]
    kpos = (blk - 1) * ATTN_BLOCK + jnp.arange(3 * ATTN_BLOCK)[None, None, :]
    valid = (jnp.abs(qpos - kpos) <= WINDOW) & (kpos >= 0) & (kpos < s)
    s_loc = jnp.where(valid[None, :, None, None], s_loc, NEG_INF)
    sink_l = jnp.broadcast_to(
        sink.astype(jnp.float32).reshape(N_KV_HEADS, Q_PER_KV)[None, None, :, :, None, None],
        s_loc.shape[:-1] + (1,))
    probs = jax.nn.softmax(jnp.concatenate([s_loc, s_ctx, sink_l], axis=-1), axis=-1)
    n_loc = 3 * ATTN_BLOCK
    o = (jnp.einsum('bnkgij,bnjkd->bnikgd', probs[..., :n_loc].astype(v.dtype), vw)
         + jnp.einsum('bnkgic,bckd->bnikgd', probs[..., n_loc:n_loc + n_ctx].astype(vc.dtype), vc))
    return o.reshape(b, s, N_Q_HEADS * d)


def context_attention(qc, kc, vc, sink):
    b, n_ctx, _, d = qc.shape
    qg = qc.reshape(b, n_ctx, N_KV_HEADS, Q_PER_KV, d)
    sc = jnp.einsum('bikgd,bjkd->bkgij', qg, kc).astype(jnp.float32) * (d ** -0.5)
    sink_c = jnp.broadcast_to(
        sink.astype(jnp.float32).reshape(N_KV_HEADS, Q_PER_KV)[None, :, :, None, None],
        sc.shape[:-1] + (1,))
    probs = jax.nn.softmax(jnp.concatenate([sc, sink_c], axis=-1), axis=-1)
    o = jnp.einsum('bkgij,bjkd->bikgd', probs[..., :n_ctx].astype(vc.dtype), vc)
    return o.reshape(b, n_ctx, N_Q_HEADS * d)


def merge_groups(a, y_attn, p, u, g, w_fourier, w_pool, pool_scale, w_sgu, b_sgu, w_out):
    y_four = fourier_mix(a, w_fourier)
    y_pool = pool_mix(p, w_pool, pool_scale)
    y_sgu = spatial_gating(u, g, w_sgu, b_sgu)
    return jnp.concatenate([y_four, y_attn, y_pool, y_sgu], axis=-1) @ w_out


def parallel_mixers(h, hc, w_in, w_fourier, sink, w_pool, pool_scale, w_sgu, b_sgu, w_out,
                    rope_cos, rope_sin, ctx_out):
    a, q, k, v, p, u, g = split_projection(h @ w_in)
    q = apply_axial_rope(q, rope_cos, rope_sin)
    k = apply_axial_rope(k, rope_cos, rope_sin)
    if ctx_out:
        ac, qc, kc, vc, pc, uc, gc = split_projection(hc @ w_in)
    else:
        kvc = hc @ w_in[:, COL_K:COL_POOL]
        kc = to_heads(kvc[..., :KV_W], N_KV_HEADS)
        vc = to_heads(kvc[..., KV_W:], N_KV_HEADS)
    y_attn = window_attention(q, k, v, kc, vc, sink)
    y = merge_groups(a, y_attn, p, u, g, w_fourier, w_pool, pool_scale, w_sgu, b_sgu, w_out)
    if not ctx_out:
        return y, None
    yc_attn = context_attention(qc, kc, vc, sink)
    yc = merge_groups(ac, yc_attn, pc, uc, gc, w_fourier, w_pool, pool_scale, w_sgu, b_sgu, w_out)
    return y, yc


def grouped_experts(t, expert_idx, gates, w_gate, w_up, w_down):
    n_tok, d = t.shape
    k = expert_idx.shape[1]
    n_exp = w_gate.shape[0]
    flat_e = expert_idx.reshape(-1)
    order = jnp.argsort(flat_e)
    se = flat_e[order]
    counts = jnp.bincount(flat_e, length=n_exp)
    padded = (counts + MOE_BLOCK - 1) // MOE_BLOCK * MOE_BLOCK
    pad_end = jnp.cumsum(padded)
    pad_start = pad_end - padded
    start = jnp.cumsum(counts) - counts
    dest = pad_start[se] + jnp.arange(n_tok * k) - start[se]
    n_blocks = -(-(n_tok * k) // MOE_BLOCK) + n_exp
    rows = n_blocks * MOE_BLOCK
    slot_tok = jnp.full((rows,), n_tok, jnp.int32).at[dest].set((order // k).astype(jnp.int32))
    slot_gate = jnp.zeros((rows,), jnp.float32).at[dest].set(gates.reshape(-1)[order])
    block_e = jnp.minimum(jnp.searchsorted(pad_end, jnp.arange(n_blocks) * MOE_BLOCK, side='right'),
                          n_exp - 1)
    t_pad = jnp.concatenate([t, jnp.zeros((1, d), t.dtype)], axis=0)
    xb = t_pad[slot_tok].reshape(n_blocks, MOE_BLOCK, d)

    def expert_block(args):
        xblk, e = args
        hid = jax.nn.silu(xblk @ w_gate[e]) * (xblk @ w_up[e])
        return hid @ w_down[e]

    yb = lax.map(expert_block, (xb, block_e)).reshape(rows, d)
    out = jnp.zeros((n_tok + 1, d), t.dtype).at[slot_tok].add(yb * slot_gate[:, None].astype(yb.dtype))
    return out[:n_tok]


def hier_moe(t, w_router_group, w_router_expert, w_exp_gate, w_exp_up, w_exp_down):
    p_group = jax.nn.softmax((t @ w_router_group).astype(jnp.float32), axis=-1)
    grp = jnp.argmax(p_group, axis=-1).astype(jnp.int32)
    gate_group = jnp.take_along_axis(p_group, grp[:, None], axis=-1)
    logits_e = jnp.einsum('td,dge->tge', t, w_router_expert).astype(jnp.float32)
    logits_sel = jnp.take_along_axis(logits_e, grp[:, None, None], axis=1)[:, 0]
    top_p, top_j = lax.top_k(jax.nn.softmax(logits_sel, axis=-1), TOP_K_FINE)
    top_p = top_p / top_p.sum(-1, keepdims=True)
    expert_idx = grp[:, None] * EXPERTS_PER_GROUP + top_j.astype(jnp.int32)
    return grouped_experts(t, expert_idx, gate_group * top_p, w_exp_gate, w_exp_up, w_exp_down)


def setup_inputs(seed: int = 0) -> dict:
    key = jax.random.key(seed)
    ks = jax.random.split(key, 24)
    f32 = jnp.float32
    L, D = DEPTH, D_MODEL

    def nrm(k, shape, fan_in, gain=1.0):
        return jax.random.normal(k, shape, f32) * (gain * fan_in ** -0.5)

    def std(k, shape, s=1.0):
        return jax.random.normal(k, shape, f32) * s

    return {
        'x': std(ks[0], (BATCH, SEQ, D)),
        'c': std(ks[1], (BATCH, D)),
        'ctx': std(ks[2], (BATCH, CTX_LEN, D)),
        'c_ctx': std(ks[3], (D,)),
        'w_ada': nrm(ks[4], (L, D, 6 * D), D, 0.5),
        'b_ada': std(ks[5], (L, 6 * D), 0.02),
        'w_in': nrm(ks[6], (L, D, D_IN), D),
        'w_fourier': nrm(ks[7], (L, GROUP_W, GROUP_W), GROUP_W),
        'attn_sink': std(ks[8], (L, N_Q_HEADS)),
        'w_pool': nrm(ks[9], (L, len(POOL_WINDOWS), POOL_CH, POOL_CH), POOL_CH),
        'pool_scale': 1.0 + std(ks[10], (L, GROUP_W), 0.1),
        'w_sgu': nrm(ks[11], (L, N_SGU_HEADS, SGU_CHUNK, SGU_CHUNK), SGU_CHUNK),
        'b_sgu': 1.0 + std(ks[12], (L, N_SGU_HEADS, SGU_CHUNK), 0.1),
        'w_out': nrm(ks[13], (L, D, D), D, INIT_BETA),
        'ln1_g': 1.0 + std(ks[14], (L, D), 0.1),
        'ln1_b': std(ks[15], (L, D), 0.02),
        'w_router_group': nrm(ks[16], (L, D, N_EXPERT_GROUPS), D),
        'w_router_expert': nrm(ks[17], (L, D, N_EXPERT_GROUPS, EXPERTS_PER_GROUP), D),
        'w_exp_gate': nrm(ks[18], (L, N_EXPERTS, D, D_EXPERT), D),
        'w_exp_up': nrm(ks[19], (L, N_EXPERTS, D, D_EXPERT), D),
        'w_exp_down': nrm(ks[20], (L, N_EXPERTS, D_EXPERT, D), D_EXPERT, INIT_BETA),
        'ln2_g': 1.0 + std(ks[21], (L, D), 0.1),
        'ln2_b': std(ks[22], (L, D), 0.02),
    }


def reference(x, c, ctx, c_ctx, w_ada, b_ada, w_in, w_fourier, attn_sink, w_pool, pool_scale,
              w_sgu, b_sgu, w_out, ln1_g, ln1_b, w_router_group, w_router_expert,
              w_exp_gate, w_exp_up, w_exp_down, ln2_g, ln2_b):
    b, s, d = x.shape
    rope_cos, rope_sin = axial_rope_tables(s)
    cond_lat = jax.nn.silu(c)
    cond_ctx = jax.nn.silu(c_ctx)
    for layer in range(DEPTH):
        last = layer == DEPTH - 1
        m = (cond_lat @ w_ada[layer] + b_ada[layer])[:, None, :]
        mc = (cond_ctx @ w_ada[layer] + b_ada[layer])[None, None, :]
        sh1, sc1, g1, sh2, sc2, g2 = jnp.split(m, 6, axis=-1)
        csh1, csc1, cg1, csh2, csc2, cg2 = jnp.split(mc, 6, axis=-1)

        h = modulate(x, sh1, sc1)
        hc = modulate(ctx, csh1, csc1)
        y, yc = parallel_mixers(h, hc, w_in[layer], w_fourier[layer], attn_sink[layer], w_pool[layer],
                                pool_scale[layer], w_sgu[layer], b_sgu[layer], w_out[layer],
                                rope_cos, rope_sin, not last)
        x = layer_norm(RES_ALPHA * x + g1 * y, ln1_g[layer], ln1_b[layer])

        h2 = modulate(x, sh2, sc2)
        if last:
            f = hier_moe(h2.reshape(-1, d), w_router_group[layer], w_router_expert[layer],
                         w_exp_gate[layer], w_exp_up[layer], w_exp_down[layer]).reshape(b, s, d)
        else:
            ctx = layer_norm(RES_ALPHA * ctx + cg1 * yc, ln1_g[layer], ln1_b[layer])
            hc2 = modulate(ctx, csh2, csc2)
            tokens = jnp.concatenate([h2.reshape(-1, d), hc2.reshape(-1, d)], axis=0)
            f_all = hier_moe(tokens, w_router_group[layer], w_router_expert[layer],
                             w_exp_gate[layer], w_exp_up[layer], w_exp_down[layer])
            f = f_all[:b * s].reshape(b, s, d)
            fc = f_all[b * s:].reshape(ctx.shape)
            ctx = layer_norm(RES_ALPHA * ctx + cg2 * fc, ln2_g[layer], ln2_b[layer])
        x = layer_norm(RES_ALPHA * x + g2 * f, ln2_g[layer], ln2_b[layer])
    return x
```

```python
import functools
import math

import numpy as np
import jax
import jax.numpy as jnp
from jax import lax
from jax.experimental import pallas as pl
from jax.experimental.pallas import tpu as pltpu

GRID_W = 64
HEAD_DIM = 64
GROUP_W = 256
KV_W = 128
WINDOW = 128
POOL_WINDOWS = (2, 4, 8, 16)
SGU_CHUNK = 128
N_GROUPS = 4
EXPERTS_PER_GROUP = 8
N_EXPERTS = 32
ROPE_BASE = 10000.0
LN_EPS = 1e-6
NEG_INF = -1e30
DEPTH = 2
RES_ALPHA = (2 * DEPTH) ** 0.25

LANES = 128
SUBLANES = 8
VMEM_LIMIT = 48 * 1024 * 1024

ROW_TILE = 512
Q_BLOCK = 256
MOE_BLOCK = 256
FFT_R = 64

BF16 = jnp.bfloat16
F32 = jnp.float32


def _cparams(*sem):
    return pltpu.CompilerParams(dimension_semantics=sem, vmem_limit_bytes=VMEM_LIMIT)


def _dot(a, b):
    return jnp.dot(a, b, preferred_element_type=F32)


def _dot_nt(a, b):
    return lax.dot_general(a, b, (((1,), (1,)), ((), ())), preferred_element_type=F32)


def _layer_norm(t):
    mu = jnp.mean(t, axis=-1, keepdims=True)
    d = t - mu
    var = jnp.mean(d * d, axis=-1, keepdims=True)
    return d * lax.rsqrt(var + LN_EPS)


def _silu(t):
    return t * (1.0 / (1.0 + jnp.exp(-t)))


def _gelu(t):
    return 0.5 * t * (1.0 + lax.erf(t * (1.0 / math.sqrt(2.0))))


def _ada_kernel(c_ref, w_ref, b_ref, o_ref):
    s = _silu(c_ref[...]).astype(BF16)
    o_ref[...] = _dot(s, w_ref[...].astype(BF16)) + b_ref[...]


def _ada(cond, w_ada, b_ada):
    n_layers, d, n = w_ada.shape
    tn = n // 4
    return pl.pallas_call(
        _ada_kernel,
        grid=(n_layers, n // tn),
        in_specs=[
            pl.BlockSpec((SUBLANES, d), lambda l, j: (0, 0)),
            pl.BlockSpec((None, d, tn), lambda l, j: (l, 0, j)),
            pl.BlockSpec((None, 1, tn), lambda l, j: (l, 0, j)),
        ],
        out_specs=pl.BlockSpec((None, SUBLANES, tn), lambda l, j: (l, 0, j)),
        out_shape=jax.ShapeDtypeStruct((n_layers, SUBLANES, n), F32),
        compiler_params=_cparams("arbitrary", "arbitrary"),
        name="ada",
    )(cond, w_ada, b_ada)


def _rope(t, cos_t, sin_t):
    lane = lax.broadcasted_iota(jnp.int32, t.shape, 1)
    first = (lane % 32) < 16
    partner = jnp.where(first, pltpu.roll(t, LANES - 16, axis=1), pltpu.roll(t, 16, axis=1))
    return t * cos_t + partner * sin_t


def _proj_kernel(x_ref, mod_ref, w_ref, cos_ref, sin_ref,
                 a_ref, q_ref, ka_ref, kb_ref, va_ref, vb_ref, p_ref, ug_ref,
                 *, mod_row, rows_per_batch, rope, sh_col, sc_col):
    d = x_ref.shape[1]
    tm = x_ref.shape[0]
    if mod_row is None:
        row = (pl.program_id(0) * tm) // rows_per_batch
    else:
        row = mod_row
    m = mod_ref[pl.ds(row, 1), :]
    shift = m[:, sh_col * d:(sh_col + 1) * d]
    scale = m[:, sc_col * d:(sc_col + 1) * d]
    h = _layer_norm(x_ref[...]) * (1.0 + scale) + shift
    z = _dot(h.astype(BF16), w_ref[...])
    a_ref[0] = z[:, 0:128]
    a_ref[1] = z[:, 128:256]
    q0, q1 = z[:, 256:384], z[:, 384:512]
    k = z[:, 512:640]
    v = z[:, 640:768]
    if rope:
        cos_t, sin_t = cos_ref[...], sin_ref[...]
        q0, q1, k = _rope(q0, cos_t, sin_t), _rope(q1, cos_t, sin_t), _rope(k, cos_t, sin_t)
    q_ref[:, 0:128] = q0.astype(BF16)
    q_ref[:, 128:256] = q1.astype(BF16)
    ka_ref[...] = k.astype(BF16)
    kb_ref[...] = pltpu.roll(k, 64, axis=1).astype(BF16)
    va_ref[...] = v.astype(BF16)
    vb_ref[...] = pltpu.roll(v, 64, axis=1).astype(BF16)
    p_ref[...] = z[:, 768:1024].astype(BF16)
    ug_ref[...] = z[:, 1024:1536].astype(BF16)


def _proj(x2, mod, w_in, cos_t, sin_t, *, mod_row, rows_per_batch, rope, tm):
    rows, d = x2.shape
    n_in = w_in.shape[1]
    steps_per_seq = cos_t.shape[0] // tm
    kern = functools.partial(_proj_kernel, mod_row=mod_row, rows_per_batch=rows_per_batch,
                             rope=rope, sh_col=0, sc_col=1)
    row_spec = lambda w: pl.BlockSpec((tm, w), lambda i: (i, 0))
    out_w = (256, 128, 128, 128, 128, 256, 512)
    out_dt = (BF16,) * len(out_w)
    a_spec = pl.BlockSpec((2, tm, LANES), lambda i: (0, i, 0))
    a_shape = jax.ShapeDtypeStruct((2, rows, LANES), F32)
    return pl.pallas_call(
        kern,
        grid=(rows // tm,),
        in_specs=[
            row_spec(d),
            pl.BlockSpec(mod.shape, lambda i: (0, 0)),
            pl.BlockSpec((d, n_in), lambda i: (0, 0)),
            pl.BlockSpec((tm, LANES), lambda i: (i % steps_per_seq, 0)),
            pl.BlockSpec((tm, LANES), lambda i: (i % steps_per_seq, 0)),
        ],
        out_specs=[a_spec] + [row_spec(w) for w in out_w],
        out_shape=[a_shape] + [jax.ShapeDtypeStruct((rows, w), dt) for w, dt in zip(out_w, out_dt)],
        compiler_params=_cparams("arbitrary"),
        name="proj",
    )(x2, mod, w_in, cos_t, sin_t)


def _fft_tables(n_pos):
    r = FFT_R
    assert n_pos == r * r
    kb = np.arange(r)[None, :, None]
    na = np.arange(r)[:, None, None]
    nb = np.arange(r)[None, None, :]
    ang = 2.0 * np.pi * ((kb * (na + r * nb)) % n_pos) / n_pos
    m1 = np.concatenate([np.cos(ang), -np.sin(ang)], axis=1)
    ka = np.arange(r)[:, None]
    n2 = np.arange(r)[None, :]
    ang2 = 2.0 * np.pi * ((ka * n2) % r) / r
    c2, s2 = np.cos(ang2), np.sin(ang2)
    w2 = np.block([[c2, s2], [-s2, c2]])
    return m1, w2


def _channel_tables(n_pos):
    h = HEAD_DIM
    c = np.arange(h)
    ang = 2.0 * np.pi * ((c[:, None] * c[None, :]) % h) / h
    scale = 1.0 / math.sqrt(n_pos * h)
    eye = np.eye(GROUP_W // h)
    cc = np.kron(eye, np.cos(ang)) * scale
    ss = np.kron(eye, np.sin(ang)) * scale
    return np.concatenate([cc, ss], axis=0)


def _fourier_kernel(a_ref, m1_ref, w2_ref, ch_ref, wf_ref, o_ref, z_ref, y_ref):
    r = FFT_R

    def step1(na, c):
        rows = jnp.concatenate([a_ref[0, pl.ds(na, r, stride=r), :],
                                a_ref[1, pl.ds(na, r, stride=r), :]], axis=1)
        z = _dot(m1_ref[na], rows.astype(BF16))
        base = pl.multiple_of(na * r, r)
        z_ref[0, pl.ds(base, r), :] = z[0:r, 0:LANES]
        z_ref[1, pl.ds(base, r), :] = z[0:r, LANES:]
        z_ref[2, pl.ds(base, r), :] = z[r:, 0:LANES]
        z_ref[3, pl.ds(base, r), :] = z[r:, LANES:]
        return c

    lax.fori_loop(0, r, step1, 0)

    def step2(kb, c):
        q = [z_ref[j, pl.ds(kb, r, stride=r), :] for j in range(4)]
        zs = jnp.concatenate([jnp.concatenate(q[0:2], axis=1),
                              jnp.concatenate(q[2:4], axis=1)], axis=0)
        y = _dot(w2_ref[...], zs.astype(BF16))
        base = pl.multiple_of(kb * r, r)
        y_ref[0, pl.ds(base, r), :] = y[0:r, 0:LANES]
        y_ref[1, pl.ds(base, r), :] = y[0:r, LANES:]
        y_ref[2, pl.ds(base, r), :] = y[r:, 0:LANES]
        y_ref[3, pl.ds(base, r), :] = y[r:, LANES:]
        return c

    lax.fori_loop(0, r, step2, 0)

    yy = jnp.concatenate([y_ref[j] for j in range(4)], axis=1)
    f = _dot(yy.astype(BF16), ch_ref[...])
    g = _dot(f.astype(BF16), wf_ref[...])
    z_ref[0] = g[:, 0:LANES]
    z_ref[1] = g[:, LANES:]

    def step3(ka, c):
        base = pl.multiple_of(ka * r, r)
        o_ref[pl.ds(base, r), 0:LANES] = z_ref[0, pl.ds(ka, r, stride=r), :]
        o_ref[pl.ds(base, r), LANES:] = z_ref[1, pl.ds(ka, r, stride=r), :]
        return c

    lax.fori_loop(0, r, step3, 0)


def _fourier(a3, w_fourier, n_pos):
    _, rows, _ = a3.shape
    gw = GROUP_W
    m1, w2 = _fft_tables(n_pos)
    ch = _channel_tables(n_pos)
    const = lambda shape: pl.BlockSpec(shape, lambda b: (0,) * len(shape))
    return pl.pallas_call(
        _fourier_kernel,
        grid=(rows // n_pos,),
        in_specs=[
            pl.BlockSpec((2, n_pos, LANES), lambda b: (0, b, 0)),
            const(m1.shape), const(w2.shape), const(ch.shape), const(w_fourier.shape),
        ],
        out_specs=pl.BlockSpec((n_pos, gw), lambda b: (b, 0)),
        out_shape=jax.ShapeDtypeStruct((rows, gw), F32),
        scratch_shapes=[pltpu.VMEM((4, n_pos, LANES), F32), pltpu.VMEM((4, n_pos, LANES), F32)],
        compiler_params=_cparams("arbitrary"),
        name="fourier",
    )(a3, jnp.asarray(m1, BF16), jnp.asarray(w2, BF16), jnp.asarray(ch, BF16), w_fourier)


def _fourier_small_kernel(a_ref, cs_ref, ch_ref, wf_ref, o_ref):
    n = a_ref.shape[1]
    a = jnp.concatenate([a_ref[0], a_ref[1]], axis=1)
    pq = _dot(cs_ref[...], a.astype(BF16))
    y = jnp.concatenate([pq[0:n], pq[n:2 * n]], axis=1).astype(BF16)
    f = _dot(y, ch_ref[...])
    o_ref[...] = _dot(f.astype(BF16), wf_ref[...])


def _fourier_small(a3, w_fourier, n_pos):
    _, rows, _ = a3.shape
    gw = GROUP_W
    k = np.arange(n_pos)
    ang = 2.0 * np.pi * ((k[:, None] * k[None, :]) % n_pos) / n_pos
    cs = np.concatenate([np.cos(ang), -np.sin(ang)], axis=0)
    ch = _channel_tables(n_pos)
    const = lambda shape: pl.BlockSpec(shape, lambda b: (0,) * len(shape))
    return pl.pallas_call(
        _fourier_small_kernel,
        grid=(rows // n_pos,),
        in_specs=[pl.BlockSpec((2, n_pos, LANES), lambda b: (0, b, 0)),
                  const(cs.shape), const(ch.shape), const(w_fourier.shape)],
        out_specs=pl.BlockSpec((n_pos, gw), lambda b: (b, 0)),
        out_shape=jax.ShapeDtypeStruct((rows, gw), F32),
        compiler_params=_cparams("arbitrary"),
        name="fourier_ctx",
    )(a3, jnp.asarray(cs, BF16), jnp.asarray(ch, BF16), w_fourier)


def _attn_kernel(sink_ref, q_ref, ka_ref, kb_ref, va_ref, vb_ref, kca_ref, kcb_ref, vca_ref, vcb_ref,
                 o_ref, *, band, seq):
    qb = q_ref.shape[0]
    scale = HEAD_DIM ** -0.5
    lane = lax.broadcasted_iota(jnp.int32, (1, LANES), 1)
    half = [lane < HEAD_DIM, lane >= HEAD_DIM]
    if band:
        i = pl.program_id(1)
        kw = qb + 2 * WINDOW
        start = pl.multiple_of(jnp.clip(i * qb - WINDOW, 0, seq - kw), WINDOW)
        qpos = i * qb + lax.broadcasted_iota(jnp.int32, (qb, 1), 0)
        kpos = start + lax.broadcasted_iota(jnp.int32, (1, kw), 1)
        valid = jnp.abs(qpos - kpos) <= WINDOW
        k_src = [ka_ref[pl.ds(start, kw), :], kb_ref[pl.ds(start, kw), :]]
        v_src = [va_ref[pl.ds(start, kw), :], vb_ref[pl.ds(start, kw), :]]
    kc_src = [kca_ref[...], kcb_ref[...]]
    vc_src = [vca_ref[...], vcb_ref[...]]
    zero = jnp.zeros((), BF16)
    for pair in range(2):
        qp = q_ref[:, pair * LANES:(pair + 1) * LANES]
        outs = []
        for j in range(2):
            src = 0 if pair == j else 1
            sink = sink_ref[2 * pair + j]
            s_ctx = _dot_nt(qp, jnp.where(half[j], kc_src[src], zero)) * scale
            m = jnp.maximum(jnp.max(s_ctx, axis=1, keepdims=True), sink)
            if band:
                s_loc = _dot_nt(qp, jnp.where(half[j], k_src[src], zero)) * scale
                s_loc = jnp.where(valid, s_loc, NEG_INF)
                m = jnp.maximum(m, jnp.max(s_loc, axis=1, keepdims=True))
            p_ctx = jnp.exp(s_ctx - m)
            den = jnp.sum(p_ctx, axis=1, keepdims=True) + jnp.exp(sink - m)
            acc = _dot(p_ctx.astype(BF16), vc_src[src])
            if band:
                p_loc = jnp.exp(s_loc - m)
                den = den + jnp.sum(p_loc, axis=1, keepdims=True)
                acc = acc + _dot(p_loc.astype(BF16), v_src[src])
            outs.append(acc / den)
        o_ref[:, pair * LANES:(pair + 1) * LANES] = jnp.where(half[0], outs[0], outs[1]).astype(BF16)


def _attention(sink, q, k_a, k_b, v_a, v_b, kc_a, kc_b, vc_a, vc_b, *, seq, n_ctx, band):
    rows = q.shape[0]
    n_batch = rows // seq
    qb = Q_BLOCK if band else seq
    steps = seq // qb
    kern = functools.partial(_attn_kernel, band=band, seq=seq)
    seq_spec = pl.BlockSpec((seq, KV_W), lambda b, i: (b, 0))
    ctx_spec = pl.BlockSpec((n_ctx, KV_W), lambda b, i: (b, 0))
    q_spec = pl.BlockSpec((qb, GROUP_W), lambda b, i: (b * steps + i, 0))
    return pl.pallas_call(
        kern,
        grid=(n_batch, steps),
        in_specs=[pl.BlockSpec(memory_space=pltpu.SMEM), q_spec,
                  seq_spec, seq_spec, seq_spec, seq_spec, ctx_spec, ctx_spec, ctx_spec, ctx_spec],
        out_specs=q_spec,
        out_shape=jax.ShapeDtypeStruct((rows, GROUP_W), BF16),
        compiler_params=_cparams("arbitrary", "arbitrary"),
        name="attn" if band else "attn_ctx",
    )(sink, q, k_a, k_b, v_a, v_b, kc_a, kc_b, vc_a, vc_b)


POOL_HALO = max(POOL_WINDOWS) // 2


def _pool(p_ref, t0, tm, seq):
    halo = POOL_HALO
    pack = 2 * SUBLANES
    main = p_ref[pl.ds(t0, tm), :].astype(F32)
    lo = pl.multiple_of(jnp.maximum(t0 - pack, 0), pack)
    hi = pl.multiple_of(jnp.minimum(t0 + tm, seq - pack), pack)
    prev = p_ref[pl.ds(lo, pack), :].astype(F32)[pack - halo:, :]
    nxt = p_ref[pl.ds(hi, pack), :].astype(F32)[:halo, :]
    prev = jnp.where(t0 > 0, prev, 0.0)
    nxt = jnp.where(t0 + tm < seq, nxt, 0.0)
    full = jnp.concatenate([prev, main, nxt], axis=0)
    n = tm + 2 * halo
    lane = lax.broadcasted_iota(jnp.int32, (1, GROUP_W), 1)
    pos = t0 + lax.broadcasted_iota(jnp.int32, (tm, 1), 0)
    s = pltpu.roll(full, 1, axis=0) + full
    pooled = jnp.zeros((tm, GROUP_W), F32)
    for g, w in enumerate(POOL_WINDOWS):
        if g > 0:
            sh = w // 4
            s = pltpu.roll(s, sh, axis=0) + pltpu.roll(s, n - sh, axis=0)
        cnt = (jnp.minimum(pos + w // 2, seq) - jnp.maximum(pos - w // 2, 0)).astype(F32)
        mean = s[halo:halo + tm, :] / cnt
        pooled = jnp.where(lane // (GROUP_W // len(POOL_WINDOWS)) == g, mean, pooled)
    return pooled - main


def _route(logits):
    lane = lax.broadcasted_iota(jnp.int32, logits.shape, 1)
    big = jnp.int32(1 << 20)
    gl = jnp.where(lane < N_GROUPS, logits, -jnp.inf)
    gmax = jnp.max(gl, axis=1, keepdims=True)
    grp = jnp.min(jnp.where(gl == gmax, lane, big), axis=1, keepdims=True)
    gate_group = 1.0 / jnp.sum(jnp.exp(gl - gmax), axis=1, keepdims=True)
    lo = N_GROUPS + EXPERTS_PER_GROUP * grp
    el = jnp.where((lane >= lo) & (lane < lo + EXPERTS_PER_GROUP), logits, -jnp.inf)
    m1 = jnp.max(el, axis=1, keepdims=True)
    i1 = jnp.min(jnp.where(el == m1, lane, big), axis=1, keepdims=True)
    el2 = jnp.where(lane == i1, -jnp.inf, el)
    m2 = jnp.max(el2, axis=1, keepdims=True)
    i2 = jnp.min(jnp.where(el2 == m2, lane, big), axis=1, keepdims=True)
    r = jnp.exp(m2 - m1)
    g1 = gate_group / (1.0 + r)
    g2 = g1 * r
    e1 = (i1 - N_GROUPS).astype(F32)
    e2 = (i2 - N_GROUPS).astype(F32)
    return jnp.where(lane == 0, e1, jnp.where(lane == 1, e2, jnp.where(lane == 2, g1, jnp.where(lane == 3, g2, 0.0))))


def _merge_kernel(x_ref, p_ref, ug_ref, yf_ref, ya_ref, mod_ref, wpool_ref, pscale_ref, wsgu_ref, bsgu_ref,
                  wout_ref, lng_ref, lnb_ref, wrh_ref, wrl_ref, *rest,
                  mod_row, seq, n_alias):
    x1_ref, h2_ref, route_ref = rest[n_alias:]
    tm, d = x_ref.shape
    if mod_row is None:
        row = pl.program_id(0)
    else:
        row = mod_row
    t0 = pl.multiple_of(pl.program_id(1) * tm, tm)
    m = mod_ref[pl.ds(row, 1), :]
    gate1, shift2, scale2 = m[:, 2 * d:3 * d], m[:, 3 * d:4 * d], m[:, 4 * d:5 * d]

    pooled = _pool(p_ref, t0, tm, seq)
    y_pool = _dot(pooled.astype(BF16), wpool_ref[...]) * pscale_ref[...]

    ug = ug_ref[...].astype(F32)
    u = _gelu(ug[:, 0:GROUP_W])
    v = _layer_norm(_gelu(ug[:, GROUP_W:])).astype(BF16)
    lane = lax.broadcasted_iota(jnp.int32, (1, GROUP_W), 1)
    n_heads = wsgu_ref.shape[0] // SGU_CHUNK
    head = lane // (GROUP_W // n_heads)
    mixed = []
    for cidx in range(tm // SGU_CHUNK):
        vc = v[cidx * SGU_CHUNK:(cidx + 1) * SGU_CHUNK, :]
        full = _dot(wsgu_ref[...], vc)
        mc = bsgu_ref[...]
        for hd in range(n_heads):
            mc = mc + jnp.where(head == hd, full[hd * SGU_CHUNK:(hd + 1) * SGU_CHUNK, :], 0.0)
        mixed.append(mc)
    y_sgu = u * jnp.concatenate(mixed, axis=0)

    cat = jnp.concatenate([yf_ref[...].astype(BF16), ya_ref[...], y_pool.astype(BF16), y_sgu.astype(BF16)],
                          axis=1)
    y = _dot(cat, wout_ref[...])
    x1 = _layer_norm(RES_ALPHA * x_ref[...] + gate1 * y) * lng_ref[...] + lnb_ref[...]
    x1_ref[...] = x1
    h2 = _layer_norm(x1) * (1.0 + scale2) + shift2
    for j in range(d // LANES):
        h2_ref[pl.ds(j, tm, stride=SUBLANES), :] = h2[:, j * LANES:(j + 1) * LANES]
    hh = h2.astype(BF16)
    hl = (h2 - hh.astype(F32)).astype(BF16)
    logits = _dot(hh, wrh_ref[...]) + (_dot(hh, wrl_ref[...]) + _dot(hl, wrh_ref[...]))
    route_ref[...] = _route(logits)


def _merge(x2, p, ug, y_four, y_attn, mod, w_pool_bd, pool_scale, w_sgu_stack, b_sgu_exp, w_out,
           ln_g, ln_b, wr_hi, wr_lo, aliased, *, mod_row, seq, tm, row_off, total_rows):
    rows, d = x2.shape
    n_batch, steps = rows // seq, seq // tm
    off = row_off // tm
    kern = functools.partial(_merge_kernel, mod_row=mod_row, seq=seq, n_alias=len(aliased))
    row_spec = lambda w: pl.BlockSpec((tm, w), lambda b, i: (b * steps + i, 0))
    const = lambda a: pl.BlockSpec(a.shape, lambda b, i: (0,) * a.ndim)
    consts = (mod, w_pool_bd, pool_scale, w_sgu_stack, b_sgu_exp, w_out, ln_g, ln_b, wr_hi, wr_lo)
    n_in = 5 + len(consts)
    out_shapes = [jax.ShapeDtypeStruct((total_rows, d), F32),
                  jax.ShapeDtypeStruct((total_rows * SUBLANES, LANES), F32),
                  jax.ShapeDtypeStruct((total_rows, LANES), F32)]
    out_specs = [pl.BlockSpec((tm, d), lambda b, i: (off + b * steps + i, 0)),
                 pl.BlockSpec((tm * SUBLANES, LANES), lambda b, i: (off + b * steps + i, 0)),
                 pl.BlockSpec((tm, LANES), lambda b, i: (off + b * steps + i, 0))]
    return pl.pallas_call(
        kern,
        grid=(n_batch, steps),
        in_specs=[row_spec(d), pl.BlockSpec((seq, GROUP_W), lambda b, i: (b, 0)),
                  row_spec(2 * GROUP_W), row_spec(GROUP_W), row_spec(GROUP_W)]
                 + [const(a) for a in consts]
                 + [pl.BlockSpec(memory_space=pl.ANY)] * len(aliased),
        out_specs=out_specs,
        out_shape=out_shapes,
        input_output_aliases={n_in + k: k for k in range(len(aliased))},
        compiler_params=_cparams("arbitrary", "arbitrary"),
        name="merge",
    )(x2, p, ug, y_four, y_attn, *consts, *aliased)


def _exact_row_sums(vals):
    hi = jnp.floor(vals * (1.0 / 256.0))
    lo = vals - 256.0 * hi
    ones = jnp.ones((SUBLANES, LANES), BF16)
    return 256.0 * _dot_nt(ones, hi.astype(BF16)) + _dot_nt(ones, lo.astype(BF16))


def _plan_kernel(route_ref, dest_ref, cnt_out_ref, cnt_ref, start_ref, carry_ref):
    ph, t = pl.program_id(0), pl.program_id(1)
    tm = route_ref.shape[0]
    lane = lax.broadcasted_iota(jnp.int32, (1, LANES), 1)
    rt = route_ref[...]
    e1 = rt[:, 0:1].astype(jnp.int32)
    e2 = rt[:, 1:2].astype(jnp.int32)
    hit1, hit2 = lane == e1, lane == e2
    onehot = jnp.where(hit1 | hit2, 1.0, 0.0)
    colsum = jnp.sum(onehot, axis=0, keepdims=True)

    @pl.when((ph == 0) & (t == 0))
    def _():
        cnt_ref[...] = jnp.zeros_like(cnt_ref)

    @pl.when(ph == 0)
    def _():
        cnt_ref[...] += colsum

    @pl.when((ph == 1) & (t == 0))
    def _():
        cnt = jnp.broadcast_to(cnt_ref[...], (SUBLANES, LANES))
        padded = jnp.floor((cnt + (MOE_BLOCK - 1.0)) * (1.0 / MOE_BLOCK)) * MOE_BLOCK
        lane8 = lax.broadcasted_iota(jnp.int32, (SUBLANES, LANES), 1)
        incl = padded
        sh = 1
        while sh < LANES:
            incl = incl + jnp.where(lane8 >= sh, pltpu.roll(incl, sh, axis=1), 0.0)
            sh *= 2
        start_ref[...] = (incl - padded)[0:1, :]
        carry_ref[...] = jnp.zeros_like(carry_ref)
        cnt_out_ref[...] = cnt

    @pl.when(ph == 1)
    def _():
        r_i = lax.broadcasted_iota(jnp.int32, (tm, tm), 0)
        c_i = lax.broadcasted_iota(jnp.int32, (tm, tm), 1)
        tri = jnp.where(c_i < r_i, 1.0, 0.0).astype(BF16)
        rank = _dot(tri, onehot.astype(BF16))
        base = start_ref[...] + carry_ref[...] + rank
        d1 = _exact_row_sums(jnp.where(hit1, base, 0.0))
        d2 = _exact_row_sums(jnp.where(hit2, base, 0.0))
        sub = lax.broadcasted_iota(jnp.int32, (SUBLANES, tm), 0)
        dest_ref[...] = jnp.where(sub == 0, d1, d2).astype(jnp.int32)
        carry_ref[...] += colsum


def _plan(route, tm):
    rows = route.shape[0]
    n_t = rows // tm
    return pl.pallas_call(
        _plan_kernel,
        grid=(2, n_t),
        in_specs=[pl.BlockSpec((tm, LANES), lambda ph, t: (t, 0))],
        out_specs=[pl.BlockSpec((None, SUBLANES, tm), lambda ph, t: (t * ph, 0, 0)),
                   pl.BlockSpec((SUBLANES, LANES), lambda ph, t: (0, 0))],
        out_shape=[jax.ShapeDtypeStruct((n_t, SUBLANES, tm), jnp.int32),
                   jax.ShapeDtypeStruct((SUBLANES, LANES), F32)],
        scratch_shapes=[pltpu.VMEM((1, LANES), F32)] * 3,
        compiler_params=_cparams("arbitrary", "arbitrary"),
        name="moe_plan",
    )(route)


def _row_copy(src_ref, src_row, dst_ref, dst_row, sem):
    return pltpu.make_async_copy(
        src_ref.at[pl.ds(pl.multiple_of(src_row * SUBLANES, SUBLANES), SUBLANES)],
        dst_ref.at[pl.ds(pl.multiple_of(dst_row * SUBLANES, SUBLANES), SUBLANES)], sem)


def _dispatch_kernel(dest_ref, h2_ref, xs_ref, sem, *, tm):
    i = pl.program_id(0)

    def body(r, c):
        for k in range(2):
            _row_copy(h2_ref, i * tm + r, xs_ref, dest_ref[0, 0, k * tm + r], sem).start()
        return c

    lax.fori_loop(0, tm, body, 0)
    for k in range(2):
        pltpu.make_async_copy(h2_ref.at[pl.ds(0, tm * SUBLANES)], xs_ref.at[pl.ds(0, tm * SUBLANES)], sem).wait()


def _dispatch(dest, h2_tiles, n_slots, tm):
    n_t = dest.shape[0]
    return pl.pallas_call(
        functools.partial(_dispatch_kernel, tm=tm),
        grid=(n_t,),
        in_specs=[pl.BlockSpec((1, 1, 2 * tm), lambda i: (i, 0, 0), memory_space=pltpu.SMEM),
                  pl.BlockSpec(memory_space=pl.ANY)],
        out_specs=pl.BlockSpec(memory_space=pl.ANY),
        out_shape=jax.ShapeDtypeStruct((n_slots * SUBLANES, LANES), F32),
        scratch_shapes=[pltpu.SemaphoreType.DMA],
        compiler_params=_cparams("arbitrary"),
        name="moe_dispatch",
    )(dest, h2_tiles)


def _expert_kernel(be_ref, bn_ref, x_ref, wg_ref, wu_ref, wd_ref, y_ref):
    i = pl.program_id(0)
    mb = x_ref.shape[0] // SUBLANES
    d = wg_ref.shape[0]

    @pl.when(bn_ref[i] > 0)
    def _():
        live = lax.broadcasted_iota(jnp.int32, (mb, 1), 0) < bn_ref[i]
        x = jnp.concatenate([x_ref[pl.ds(j, mb, stride=SUBLANES), :] for j in range(d // LANES)], axis=1)
        x = jnp.where(live, x, 0.0).astype(BF16)
        g = _dot(x, wg_ref[...].astype(BF16))
        u = _dot(x, wu_ref[...].astype(BF16))
        hid = (_silu(g) * u).astype(BF16)
        y = _dot(hid, wd_ref[...].astype(BF16))
        for j in range(d // LANES):
            y_ref[pl.ds(j, mb, stride=SUBLANES), :] = y[:, j * LANES:(j + 1) * LANES]


def _experts(block_e, block_n, xs, w_gate, w_up, w_down, layer):
    n_blocks = block_e.shape[0]
    _, _, d, de = w_gate.shape
    blk = pl.BlockSpec((MOE_BLOCK * SUBLANES, LANES), lambda i, be, bn: (i, 0))
    return pl.pallas_call(
        _expert_kernel,
        grid_spec=pltpu.PrefetchScalarGridSpec(
            num_scalar_prefetch=2,
            grid=(n_blocks,),
            in_specs=[blk,
                      pl.BlockSpec((None, None, d, de), lambda i, be, bn: (layer, be[i], 0, 0)),
                      pl.BlockSpec((None, None, d, de), lambda i, be, bn: (layer, be[i], 0, 0)),
                      pl.BlockSpec((None, None, de, d), lambda i, be, bn: (layer, be[i], 0, 0))],
            out_specs=blk),
        out_shape=jax.ShapeDtypeStruct(xs.shape, F32),
        compiler_params=_cparams("arbitrary"),
        name="moe_experts",
    )(block_e, block_n, xs, w_gate, w_up, w_down)


def _combine_kernel(dest_ref, x1_ref, route_ref, mod_ref, lng_ref, lnb_ref, y_ref, o_ref, buf_ref, sem,
                    *, mod_row, rows_per_batch):
    tm, d = x1_ref.shape
    i = pl.program_id(0)

    def body(r, c):
        for k in range(2):
            _row_copy(y_ref, dest_ref[0, 0, k * tm + r], buf_ref, k * tm + r, sem).start()
        return c

    lax.fori_loop(0, tm, body, 0)
    for k in range(2):
        pltpu.make_async_copy(y_ref.at[pl.ds(0, tm * SUBLANES)], buf_ref.at[pl.ds(0, tm * SUBLANES)], sem).wait()

    if mod_row is None:
        row = (i * tm) // rows_per_batch
    else:
        row = mod_row
    gate2 = mod_ref[pl.ds(row, 1), :][:, 5 * d:6 * d]
    rt = route_ref[...]
    f = jnp.zeros((tm, d), F32)
    for k in range(2):
        rows = jnp.concatenate(
            [buf_ref[pl.ds(k * tm * SUBLANES + j, tm, stride=SUBLANES), :] for j in range(d // LANES)], axis=1)
        f = f + rows * rt[:, 2 + k:3 + k]
    o_ref[...] = _layer_norm(RES_ALPHA * x1_ref[...] + gate2 * f) * lng_ref[...] + lnb_ref[...]


def _combine(dest, x1, route, mod, ln_g, ln_b, y_tiles, *, tm, row_off, rows, mod_row, rows_per_batch):
    d = x1.shape[1]
    off = row_off // tm
    kern = functools.partial(_combine_kernel, mod_row=mod_row, rows_per_batch=rows_per_batch)
    const = lambda a: pl.BlockSpec(a.shape, lambda i: (0,) * a.ndim)
    return pl.pallas_call(
        kern,
        grid=(rows // tm,),
        in_specs=[pl.BlockSpec((1, 1, 2 * tm), lambda i: (off + i, 0, 0), memory_space=pltpu.SMEM),
                  pl.BlockSpec((tm, d), lambda i: (off + i, 0)),
                  pl.BlockSpec((tm, LANES), lambda i: (off + i, 0)),
                  const(mod), const(ln_g), const(ln_b),
                  pl.BlockSpec(memory_space=pl.ANY)],
        out_specs=pl.BlockSpec((tm, d), lambda i: (i, 0)),
        out_shape=jax.ShapeDtypeStruct((rows, d), F32),
        scratch_shapes=[pltpu.VMEM((2 * tm * SUBLANES, LANES), F32), pltpu.SemaphoreType.DMA],
        compiler_params=_cparams("arbitrary"),
        name="moe_combine",
    )(dest, x1, route, mod, ln_g, ln_b, y_tiles)


def _rope_tables(n_pos):
    rows = n_pos // GRID_W
    row = jnp.repeat(jnp.arange(rows), GRID_W).astype(F32)
    col = jnp.tile(jnp.arange(GRID_W), rows).astype(F32)
    n_freq = HEAD_DIM // 4
    freq = ROPE_BASE ** (-jnp.arange(n_freq, dtype=F32) / n_freq)
    ang_r, ang_c = row[:, None] * freq, col[:, None] * freq
    cos_h = jnp.concatenate([jnp.cos(ang_r)] * 2 + [jnp.cos(ang_c)] * 2, axis=1)
    sin_h = jnp.concatenate([-jnp.sin(ang_r), jnp.sin(ang_r), -jnp.sin(ang_c), jnp.sin(ang_c)], axis=1)
    return jnp.tile(cos_h, (1, 2)), jnp.tile(sin_h, (1, 2))


def _block_table(counts, n_blocks):
    cnt = counts.astype(jnp.int32)
    padded = (cnt + MOE_BLOCK - 1) // MOE_BLOCK * MOE_BLOCK
    pad_end = jnp.cumsum(padded)
    pad_start = pad_end - padded
    blk_start = jnp.arange(n_blocks, dtype=jnp.int32) * MOE_BLOCK
    be = jnp.minimum(jnp.searchsorted(pad_end, blk_start, side='right'), N_EXPERTS - 1).astype(jnp.int32)
    bn = jnp.clip(cnt[be] - (blk_start - pad_start[be]), 0, MOE_BLOCK).astype(jnp.int32)
    return be, bn


def _moe(route, h2_tiles, w_gate, w_up, w_down, layer):
    rows = route.shape[0]
    tm = ROW_TILE
    n_blocks = -(-(2 * rows) // MOE_BLOCK) + N_EXPERTS
    dest8, counts = _plan(route, tm)
    dest = dest8[:, 0:2, :].reshape(rows // tm, 1, 2 * tm)
    block_e, block_n = _block_table(counts[0, :N_EXPERTS], n_blocks)
    xs = _dispatch(dest, h2_tiles, n_blocks * MOE_BLOCK, tm)
    ys = _experts(block_e, block_n, xs, w_gate, w_up, w_down, layer)
    return dest, ys


def kernel(x, c, ctx, c_ctx, w_ada, b_ada, w_in, w_fourier, attn_sink, w_pool, pool_scale, w_sgu, b_sgu,
           w_out, ln1_g, ln1_b, w_router_group, w_router_expert, w_exp_gate, w_exp_up, w_exp_down,
           ln2_g, ln2_b):
    b, s, d = x.shape
    n_ctx = ctx.shape[1]
    n_layers = w_in.shape[0]
    tm = ROW_TILE
    cond = jnp.concatenate([c, c_ctx[None, :], jnp.zeros((SUBLANES - b - 1, d), F32)], axis=0)
    mod_all = _ada(cond, w_ada, b_ada[:, None, :])
    cos_t, sin_t = _rope_tables(s)
    x2 = x.reshape(b * s, d)
    c2 = ctx.reshape(b * n_ctx, d)
    n_sgu = w_sgu.shape[1]
    for layer in range(n_layers):
        last = layer == n_layers - 1
        mod = mod_all[layer]
        w_in_l = w_in[layer].astype(BF16)
        wf = w_fourier[layer].astype(BF16)
        w_pool_bd = jax.scipy.linalg.block_diag(*[w_pool[layer, g] for g in range(w_pool.shape[1])]).astype(BF16)
        w_sgu_stack = w_sgu[layer].reshape(n_sgu * SGU_CHUNK, SGU_CHUNK).astype(BF16)
        b_sgu_exp = jnp.repeat(b_sgu[layer].T, GROUP_W // n_sgu, axis=1)
        w_router = jnp.concatenate([w_router_group[layer], w_router_expert[layer].reshape(d, N_EXPERTS)], axis=1)
        w_router = jnp.pad(w_router, ((0, 0), (0, LANES - w_router.shape[1])))
        wr_hi = w_router.astype(BF16)
        wr_lo = (w_router - wr_hi.astype(F32)).astype(BF16)
        merge_consts = (mod, w_pool_bd, pool_scale[layer][None, :], w_sgu_stack, b_sgu_exp,
                        w_out[layer].astype(BF16), ln1_g[layer][None, :], ln1_b[layer][None, :], wr_hi, wr_lo)
        sink = attn_sink[layer]

        a, q, k_a, k_b, v_a, v_b, p, ug = _proj(x2, mod, w_in_l, cos_t, sin_t,
                                                mod_row=None, rows_per_batch=s, rope=True, tm=tm)
        ac, qc, kc_a, kc_b, vc_a, vc_b, pc, ugc = _proj(c2, mod, w_in_l, cos_t, sin_t,
                                                        mod_row=b, rows_per_batch=n_ctx, rope=False, tm=n_ctx)
        y_four = _fourier(a, wf, s)
        y_attn = _attention(sink, q, k_a, k_b, v_a, v_b, kc_a, kc_b, vc_a, vc_b, seq=s, n_ctx=n_ctx, band=True)
        total = b * s + (0 if last else b * n_ctx)
        merged = _merge(x2, p, ug, y_four, y_attn, *merge_consts, (),
                        mod_row=None, seq=s, tm=tm, row_off=0, total_rows=total)
        if not last:
            yc_four = _fourier_small(ac, wf, n_ctx)
            yc_attn = _attention(sink, qc, kc_a, kc_b, vc_a, vc_b, kc_a, kc_b, vc_a, vc_b,
                                 seq=n_ctx, n_ctx=n_ctx, band=False)
            merged = _merge(c2, pc, ugc, yc_four, yc_attn, *merge_consts, tuple(merged),
                            mod_row=b, seq=n_ctx, tm=n_ctx, row_off=b * s, total_rows=total)
        x1, h2_tiles, route = merged
        dest, ys = _moe(route, h2_tiles, w_exp_gate, w_exp_up, w_exp_down, layer)
        ln_g, ln_b = ln2_g[layer][None, :], ln2_b[layer][None, :]
        x2 = _combine(dest, x1, route, mod, ln_g, ln_b, ys, tm=tm, row_off=0, rows=b * s,
                      mod_row=None, rows_per_batch=s)
        if not last:
            c2 = _combine(dest, x1, route, mod, ln_g, ln_b, ys, tm=tm, row_off=b * s, rows=b * n_ctx,
                          mod_row=b, rows_per_batch=n_ctx)
    return x2.reshape(b, s, d)
```

```python
import functools
import math

import numpy as np
import jax
import jax.numpy as jnp
from jax import lax
from jax.experimental import pallas as pl
from jax.experimental.pallas import tpu as pltpu

GRID_W = 64
HEAD_DIM = 64
GROUP_W = 256
KV_W = 128
WINDOW = 128
POOL_WINDOWS = (2, 4, 8, 16)
SGU_CHUNK = 128
N_GROUPS = 4
EXPERTS_PER_GROUP = 8
N_EXPERTS = 32
ROPE_BASE = 10000.0
LN_EPS = 1e-6
NEG_INF = -1e30
DEPTH = 2
RES_ALPHA = (2 * DEPTH) ** 0.25

LANES = 128
SUBLANES = 8
VMEM_LIMIT = 48 * 1024 * 1024

ROW_TILE = 512
Q_BLOCK = 256
MOE_BLOCK = 256
FFT_R = 64

BF16 = jnp.bfloat16
F32 = jnp.float32


def _cparams(*sem):
    return pltpu.CompilerParams(dimension_semantics=sem, vmem_limit_bytes=VMEM_LIMIT)


def _dot(a, b):
    return jnp.dot(a, b, preferred_element_type=F32)


def _dot_nt(a, b):
    return lax.dot_general(a, b, (((1,), (1,)), ((), ())), preferred_element_type=F32)


def _layer_norm(t):
    mu = jnp.mean(t, axis=-1, keepdims=True)
    d = t - mu
    var = jnp.mean(d * d, axis=-1, keepdims=True)
    return d * lax.rsqrt(var + LN_EPS)


def _silu(t):
    return t * (1.0 / (1.0 + jnp.exp(-t)))


def _gelu(t):
    return 0.5 * t * (1.0 + lax.erf(t * (1.0 / math.sqrt(2.0))))


def _ada_kernel(c_ref, w_ref, b_ref, o_ref):
    s = _silu(c_ref[...]).astype(BF16)
    o_ref[...] = _dot(s, w_ref[...].astype(BF16)) + b_ref[...]


def _ada(cond, w_ada, b_ada):
    n_layers, d, n = w_ada.shape
    tn = n // 4
    return pl.pallas_call(
        _ada_kernel,
        grid=(n_layers, n // tn),
        in_specs=[
            pl.BlockSpec((SUBLANES, d), lambda l, j: (0, 0)),
            pl.BlockSpec((None, d, tn), lambda l, j: (l, 0, j)),
            pl.BlockSpec((None, 1, tn), lambda l, j: (l, 0, j)),
        ],
        out_specs=pl.BlockSpec((None, SUBLANES, tn), lambda l, j: (l, 0, j)),
        out_shape=jax.ShapeDtypeStruct((n_layers, SUBLANES, n), F32),
        compiler_params=_cparams("arbitrary", "arbitrary"),
        name="ada",
    )(cond, w_ada, b_ada)


def _rope(t, cos_t, sin_t):
    lane = lax.broadcasted_iota(jnp.int32, t.shape, 1)
    first = (lane % 32) < 16
    partner = jnp.where(first, pltpu.roll(t, LANES - 16, axis=1), pltpu.roll(t, 16, axis=1))
    return t * cos_t + partner * sin_t


def _proj_kernel(x_ref, mod_ref, w_ref, cos_ref, sin_ref,
                 a_ref, q_ref, ka_ref, kb_ref, va_ref, vb_ref, p_ref, ug_ref,
                 *, mod_row, rows_per_batch, rope, sh_col, sc_col):
    d = x_ref.shape[1]
    tm = x_ref.shape[0]
    if mod_row is None:
        row = (pl.program_id(0) * tm) // rows_per_batch
    else:
        row = mod_row
    m = mod_ref[pl.ds(row, 1), :]
    shift = m[:, sh_col * d:(sh_col + 1) * d]
    scale = m[:, sc_col * d:(sc_col + 1) * d]
    h = _layer_norm(x_ref[...]) * (1.0 + scale) + shift
    z = _dot(h.astype(BF16), w_ref[...])
    a_ref[0] = z[:, 0:128]
    a_ref[1] = z[:, 128:256]
    q0, q1 = z[:, 256:384], z[:, 384:512]
    k = z[:, 512:640]
    v = z[:, 640:768]
    if rope:
        cos_t, sin_t = cos_ref[...], sin_ref[...]
        q0, q1, k = _rope(q0, cos_t, sin_t), _rope(q1, cos_t, sin_t), _rope(k, cos_t, sin_t)
    q_ref[:, 0:128] = q0.astype(BF16)
    q_ref[:, 128:256] = q1.astype(BF16)
    ka_ref[...] = k.astype(BF16)
    kb_ref[...] = pltpu.roll(k, 64, axis=1).astype(BF16)
    va_ref[...] = v.astype(BF16)
    vb_ref[...] = pltpu.roll(v, 64, axis=1).astype(BF16)
    p_ref[...] = z[:, 768:1024].astype(BF16)
    ug_ref[...] = z[:, 1024:1536].astype(BF16)


def _proj(x2, mod, w_in, cos_t, sin_t, *, mod_row, rows_per_batch, rope, tm):
    rows, d = x2.shape
    n_in = w_in.shape[1]
    steps_per_seq = cos_t.shape[0] // tm
    kern = functools.partial(_proj_kernel, mod_row=mod_row, rows_per_batch=rows_per_batch,
                             rope=rope, sh_col=0, sc_col=1)
    row_spec = lambda w: pl.BlockSpec((tm, w), lambda i: (i, 0))
    out_w = (256, 128, 128, 128, 128, 256, 512)
    out_dt = (BF16,) * len(out_w)
    a_spec = pl.BlockSpec((2, tm, LANES), lambda i: (0, i, 0))
    a_shape = jax.ShapeDtypeStruct((2, rows, LANES), F32)
    return pl.pallas_call(
        kern,
        grid=(rows // tm,),
        in_specs=[
            row_spec(d),
            pl.BlockSpec(mod.shape, lambda i: (0, 0)),
            pl.BlockSpec((d, n_in), lambda i: (0, 0)),
            pl.BlockSpec((tm, LANES), lambda i: (i % steps_per_seq, 0)),
            pl.BlockSpec((tm, LANES), lambda i: (i % steps_per_seq, 0)),
        ],
        out_specs=[a_spec] + [row_spec(w) for w in out_w],
        out_shape=[a_shape] + [jax.ShapeDtypeStruct((rows, w), dt) for w, dt in zip(out_w, out_dt)],
        compiler_params=_cparams("arbitrary"),
        name="proj",
    )(x2, mod, w_in, cos_t, sin_t)


def _fft_tables(n_pos):
    r = FFT_R
    assert n_pos == r * r
    kb = np.arange(r)[None, :, None]
    na = np.arange(r)[:, None, None]
    nb = np.arange(r)[None, None, :]
    ang = 2.0 * np.pi * ((kb * (na + r * nb)) % n_pos) / n_pos
    m1 = np.concatenate([np.cos(ang), -np.sin(ang)], axis=1)
    ka = np.arange(r)[:, None]
    n2 = np.arange(r)[None, :]
    ang2 = 2.0 * np.pi * ((ka * n2) % r) / r
    c2, s2 = np.cos(ang2), np.sin(ang2)
    w2 = np.block([[c2, s2], [-s2, c2]])
    return m1, w2


def _channel_tables(n_pos):
    h = HEAD_DIM
    c = np.arange(h)
    ang = 2.0 * np.pi * ((c[:, None] * c[None, :]) % h) / h
    scale = 1.0 / math.sqrt(n_pos * h)
    eye = np.eye(GROUP_W // h)
    cc = np.kron(eye, np.cos(ang)) * scale
    ss = np.kron(eye, np.sin(ang)) * scale
    return np.concatenate([cc, ss], axis=0)


def _fourier_kernel(a_ref, m1_ref, w2_ref, ch_ref, wf_ref, o_ref, z_ref, y_ref):
    r = FFT_R

    def step1(na, c):
        rows = jnp.concatenate([a_ref[0, pl.ds(na, r, stride=r), :],
                                a_ref[1, pl.ds(na, r, stride=r), :]], axis=1)
        z = _dot(m1_ref[na], rows.astype(BF16))
        base = pl.multiple_of(na * r, r)
        z_ref[0, pl.ds(base, r), :] = z[0:r, 0:LANES]
        z_ref[1, pl.ds(base, r), :] = z[0:r, LANES:]
        z_ref[2, pl.ds(base, r), :] = z[r:, 0:LANES]
        z_ref[3, pl.ds(base, r), :] = z[r:, LANES:]
        return c

    lax.fori_loop(0, r, step1, 0)

    def step2(kb, c):
        q = [z_ref[j, pl.ds(kb, r, stride=r), :] for j in range(4)]
        zs = jnp.concatenate([jnp.concatenate(q[0:2], axis=1),
                              jnp.concatenate(q[2:4], axis=1)], axis=0)
        y = _dot(w2_ref[...], zs.astype(BF16))
        base = pl.multiple_of(kb * r, r)
        y_ref[0, pl.ds(base, r), :] = y[0:r, 0:LANES]
        y_ref[1, pl.ds(base, r), :] = y[0:r, LANES:]
        y_ref[2, pl.ds(base, r), :] = y[r:, 0:LANES]
        y_ref[3, pl.ds(base, r), :] = y[r:, LANES:]
        return c

    lax.fori_loop(0, r, step2, 0)

    yy = jnp.concatenate([y_ref[j] for j in range(4)], axis=1)
    f = _dot(yy.astype(BF16), ch_ref[...])
    g = _dot(f.astype(BF16), wf_ref[...])
    z_ref[0] = g[:, 0:LANES]
    z_ref[1] = g[:, LANES:]

    def step3(ka, c):
        base = pl.multiple_of(ka * r, r)
        o_ref[pl.ds(base, r), 0:LANES] = z_ref[0, pl.ds(ka, r, stride=r), :]
        o_ref[pl.ds(base, r), LANES:] = z_ref[1, pl.ds(ka, r, stride=r), :]
        return c

    lax.fori_loop(0, r, step3, 0)


def _fourier(a3, w_fourier, n_pos):
    _, rows, _ = a3.shape
    gw = GROUP_W
    m1, w2 = _fft_tables(n_pos)
    ch = _channel_tables(n_pos)
    const = lambda shape: pl.BlockSpec(shape, lambda b: (0,) * len(shape))
    return pl.pallas_call(
        _fourier_kernel,
        grid=(rows // n_pos,),
        in_specs=[
            pl.BlockSpec((2, n_pos, LANES), lambda b: (0, b, 0)),
            const(m1.shape), const(w2.shape), const(ch.shape), const(w_fourier.shape),
        ],
        out_specs=pl.BlockSpec((n_pos, gw), lambda b: (b, 0)),
        out_shape=jax.ShapeDtypeStruct((rows, gw), F32),
        scratch_shapes=[pltpu.VMEM((4, n_pos, LANES), F32), pltpu.VMEM((4, n_pos, LANES), F32)],
        compiler_params=_cparams("arbitrary"),
        name="fourier",
    )(a3, jnp.asarray(m1, BF16), jnp.asarray(w2, BF16), jnp.asarray(ch, BF16), w_fourier)


def _fourier_small_kernel(a_ref, cs_ref, ch_ref, wf_ref, o_ref):
    n = a_ref.shape[1]
    a = jnp.concatenate([a_ref[0], a_ref[1]], axis=1)
    pq = _dot(cs_ref[...], a.astype(BF16))
    y = jnp.concatenate([pq[0:n], pq[n:2 * n]], axis=1).astype(BF16)
    f = _dot(y, ch_ref[...])
    o_ref[...] = _dot(f.astype(BF16), wf_ref[...])


def _fourier_small(a3, w_fourier, n_pos):
    _, rows, _ = a3.shape
    gw = GROUP_W
    k = np.arange(n_pos)
    ang = 2.0 * np.pi * ((k[:, None] * k[None, :]) % n_pos) / n_pos
    cs = np.concatenate([np.cos(ang), -np.sin(ang)], axis=0)
    ch = _channel_tables(n_pos)
    const = lambda shape: pl.BlockSpec(shape, lambda b: (0,) * len(shape))
    return pl.pallas_call(
        _fourier_small_kernel,
        grid=(rows // n_pos,),
        in_specs=[pl.BlockSpec((2, n_pos, LANES), lambda b: (0, b, 0)),
                  const(cs.shape), const(ch.shape), const(w_fourier.shape)],
        out_specs=pl.BlockSpec((n_pos, gw), lambda b: (b, 0)),
        out_shape=jax.ShapeDtypeStruct((rows, gw), F32),
        compiler_params=_cparams("arbitrary"),
        name="fourier_ctx",
    )(a3, jnp.asarray(cs, BF16), jnp.asarray(ch, BF16), w_fourier)


def _attn_kernel(sink_ref, q_ref, ka_ref, kb_ref, va_ref, vb_ref, kca_ref, kcb_ref, vca_ref, vcb_ref,
                 o_ref, *, band, seq):
    qb = q_ref.shape[0]
    scale = HEAD_DIM ** -0.5
    lane = lax.broadcasted_iota(jnp.int32, (1, LANES), 1)
    half = [lane < HEAD_DIM, lane >= HEAD_DIM]
    if band:
        i = pl.program_id(1)
        kw = qb + 2 * WINDOW
        start = pl.multiple_of(jnp.clip(i * qb - WINDOW, 0, seq - kw), WINDOW)
        qpos = i * qb + lax.broadcasted_iota(jnp.int32, (qb, 1), 0)
        kpos = start + lax.broadcasted_iota(jnp.int32, (1, kw), 1)
        valid = jnp.abs(qpos - kpos) <= WINDOW
        k_src = [ka_ref[pl.ds(start, kw), :], kb_ref[pl.ds(start, kw), :]]
        v_src = [va_ref[pl.ds(start, kw), :], vb_ref[pl.ds(start, kw), :]]
    kc_src = [kca_ref[...], kcb_ref[...]]
    vc_src = [vca_ref[...], vcb_ref[...]]
    zero = jnp.zeros((), BF16)
    for pair in range(2):
        qp = q_ref[:, pair * LANES:(pair + 1) * LANES]
        outs = []
        for j in range(2):
            src = 0 if pair == j else 1
            sink = sink_ref[2 * pair + j]
            s_ctx = _dot_nt(qp, jnp.where(half[j], kc_src[src], zero)) * scale
            m = jnp.maximum(jnp.max(s_ctx, axis=1, keepdims=True), sink)
            if band:
                s_loc = _dot_nt(qp, jnp.where(half[j], k_src[src], zero)) * scale
                s_loc = jnp.where(valid, s_loc, NEG_INF)
                m = jnp.maximum(m, jnp.max(s_loc, axis=1, keepdims=True))
            p_ctx = jnp.exp(s_ctx - m)
            den = jnp.sum(p_ctx, axis=1, keepdims=True) + jnp.exp(sink - m)
            acc = _dot(p_ctx.astype(BF16), vc_src[src])
            if band:
                p_loc = jnp.exp(s_loc - m)
                den = den + jnp.sum(p_loc, axis=1, keepdims=True)
                acc = acc + _dot(p_loc.astype(BF16), v_src[src])
            outs.append(acc / den)
        o_ref[:, pair * LANES:(pair + 1) * LANES] = jnp.where(half[0], outs[0], outs[1]).astype(BF16)


def _attention(sink, q, k_a, k_b, v_a, v_b, kc_a, kc_b, vc_a, vc_b, *, seq, n_ctx, band):
    rows = q.shape[0]
    n_batch = rows // seq
    qb = Q_BLOCK if band else seq
    steps = seq // qb
    kern = functools.partial(_attn_kernel, band=band, seq=seq)
    seq_spec = pl.BlockSpec((seq, KV_W), lambda b, i: (b, 0))
    ctx_spec = pl.BlockSpec((n_ctx, KV_W), lambda b, i: (b, 0))
    q_spec = pl.BlockSpec((qb, GROUP_W), lambda b, i: (b * steps + i, 0))
    return pl.pallas_call(
        kern,
        grid=(n_batch, steps),
        in_specs=[pl.BlockSpec(memory_space=pltpu.SMEM), q_spec,
                  seq_spec, seq_spec, seq_spec, seq_spec, ctx_spec, ctx_spec, ctx_spec, ctx_spec],
        out_specs=q_spec,
        out_shape=jax.ShapeDtypeStruct((rows, GROUP_W), BF16),
        compiler_params=_cparams("arbitrary", "arbitrary"),
        name="attn" if band else "attn_ctx",
    )(sink, q, k_a, k_b, v_a, v_b, kc_a, kc_b, vc_a, vc_b)


POOL_HALO = max(POOL_WINDOWS) // 2


def _pool(p_ref, t0, tm, seq):
    halo = POOL_HALO
    pack = 2 * SUBLANES
    main = p_ref[pl.ds(t0, tm), :].astype(F32)
    lo = pl.multiple_of(jnp.maximum(t0 - pack, 0), pack)
    hi = pl.multiple_of(jnp.minimum(t0 + tm, seq - pack), pack)
    prev = p_ref[pl.ds(lo, pack), :].astype(F32)[pack - halo:, :]
    nxt = p_ref[pl.ds(hi, pack), :].astype(F32)[:halo, :]
    prev = jnp.where(t0 > 0, prev, 0.0)
    nxt = jnp.where(t0 + tm < seq, nxt, 0.0)
    full = jnp.concatenate([prev, main, nxt], axis=0)
    n = tm + 2 * halo
    lane = lax.broadcasted_iota(jnp.int32, (1, GROUP_W), 1)
    pos = t0 + lax.broadcasted_iota(jnp.int32, (tm, 1), 0)
    s = pltpu.roll(full, 1, axis=0) + full
    pooled = jnp.zeros((tm, GROUP_W), F32)
    for g, w in enumerate(POOL_WINDOWS):
        if g > 0:
            sh = w // 4
            s = pltpu.roll(s, sh, axis=0) + pltpu.roll(s, n - sh, axis=0)
        cnt = (jnp.minimum(pos + w // 2, seq) - jnp.maximum(pos - w // 2, 0)).astype(F32)
        mean = s[halo:halo + tm, :] / cnt
        pooled = jnp.where(lane // (GROUP_W // len(POOL_WINDOWS)) == g, mean, pooled)
    return pooled - main


def _route(logits):
    lane = lax.broadcasted_iota(jnp.int32, logits.shape, 1)
    big = jnp.int32(1 << 20)
    gl = jnp.where(lane < N_GROUPS, logits, -jnp.inf)
    gmax = jnp.max(gl, axis=1, keepdims=True)
    grp = jnp.min(jnp.where(gl == gmax, lane, big), axis=1, keepdims=True)
    gate_group = 1.0 / jnp.sum(jnp.exp(gl - gmax), axis=1, keepdims=True)
    lo = N_GROUPS + EXPERTS_PER_GROUP * grp
    el = jnp.where((lane >= lo) & (lane < lo + EXPERTS_PER_GROUP), logits, -jnp.inf)
    m1 = jnp.max(el, axis=1, keepdims=True)
    i1 = jnp.min(jnp.where(el == m1, lane, big), axis=1, keepdims=True)
    el2 = jnp.where(lane == i1, -jnp.inf, el)
    m2 = jnp.max(el2, axis=1, keepdims=True)
    i2 = jnp.min(jnp.where(el2 == m2, lane, big), axis=1, keepdims=True)
    r = jnp.exp(m2 - m1)
    g1 = gate_group / (1.0 + r)
    g2 = g1 * r
    e1 = (i1 - N_GROUPS).astype(F32)
    e2 = (i2 - N_GROUPS).astype(F32)
    return jnp.where(lane == 0, e1, jnp.where(lane == 1, e2, jnp.where(lane == 2, g1, jnp.where(lane == 3, g2, 0.0))))


def _merge_kernel(x_ref, p_ref, ug_ref, yf_ref, ya_ref, mod_ref, wpool_ref, pscale_ref, wsgu_ref, bsgu_ref,
                  wout_ref, lng_ref, lnb_ref, wrh_ref, wrl_ref, *rest,
                  mod_row, seq, n_alias):
    x1_ref, h2_ref, route_ref = rest[n_alias:]
    tm, d = x_ref.shape
    if mod_row is None:
        row = pl.program_id(0)
    else:
        row = mod_row
    t0 = pl.multiple_of(pl.program_id(1) * tm, tm)
    m = mod_ref[pl.ds(row, 1), :]
    gate1, shift2, scale2 = m[:, 2 * d:3 * d], m[:, 3 * d:4 * d], m[:, 4 * d:5 * d]

    pooled = _pool(p_ref, t0, tm, seq)
    y_pool = _dot(pooled.astype(BF16), wpool_ref[...]) * pscale_ref[...]

    ug = ug_ref[...].astype(F32)
    u = _gelu(ug[:, 0:GROUP_W])
    v = _layer_norm(_gelu(ug[:, GROUP_W:])).astype(BF16)
    lane = lax.broadcasted_iota(jnp.int32, (1, GROUP_W), 1)
    n_heads = wsgu_ref.shape[0] // SGU_CHUNK
    head = lane // (GROUP_W // n_heads)
    mixed = []
    for cidx in range(tm // SGU_CHUNK):
        vc = v[cidx * SGU_CHUNK:(cidx + 1) * SGU_CHUNK, :]
        full = _dot(wsgu_ref[...], vc)
        mc = bsgu_ref[...]
        for hd in range(n_heads):
            mc = mc + jnp.where(head == hd, full[hd * SGU_CHUNK:(hd + 1) * SGU_CHUNK, :], 0.0)
        mixed.append(mc)
    y_sgu = u * jnp.concatenate(mixed, axis=0)

    cat = jnp.concatenate([yf_ref[...].astype(BF16), ya_ref[...], y_pool.astype(BF16), y_sgu.astype(BF16)],
                          axis=1)
    y = _dot(cat, wout_ref[...])
    x1 = _layer_norm(RES_ALPHA * x_ref[...] + gate1 * y) * lng_ref[...] + lnb_ref[...]
    x1_ref[...] = x1
    h2 = _layer_norm(x1) * (1.0 + scale2) + shift2
    for j in range(d // LANES):
        h2_ref[pl.ds(j, tm, stride=SUBLANES), :] = h2[:, j * LANES:(j + 1) * LANES]
    hh = h2.astype(BF16)
    hl = (h2 - hh.astype(F32)).astype(BF16)
    logits = _dot(hh, wrh_ref[...]) + (_dot(hh, wrl_ref[...]) + _dot(hl, wrh_ref[...]))
    route_ref[...] = _route(logits)


def _merge(x2, p, ug, y_four, y_attn, mod, w_pool_bd, pool_scale, w_sgu_stack, b_sgu_exp, w_out,
           ln_g, ln_b, wr_hi, wr_lo, aliased, *, mod_row, seq, tm, row_off, total_rows):
    rows, d = x2.shape
    n_batch, steps = rows // seq, seq // tm
    off = row_off // tm
    kern = functools.partial(_merge_kernel, mod_row=mod_row, seq=seq, n_alias=len(aliased))
    row_spec = lambda w: pl.BlockSpec((tm, w), lambda b, i: (b * steps + i, 0))
    const = lambda a: pl.BlockSpec(a.shape, lambda b, i: (0,) * a.ndim)
    consts = (mod, w_pool_bd, pool_scale, w_sgu_stack, b_sgu_exp, w_out, ln_g, ln_b, wr_hi, wr_lo)
    n_in = 5 + len(consts)
    out_shapes = [jax.ShapeDtypeStruct((total_rows, d), F32),
                  jax.ShapeDtypeStruct((total_rows * SUBLANES, LANES), F32),
                  jax.ShapeDtypeStruct((total_rows, LANES), F32)]
    out_specs = [pl.BlockSpec((tm, d), lambda b, i: (off + b * steps + i, 0)),
                 pl.BlockSpec((tm * SUBLANES, LANES), lambda b, i: (off + b * steps + i, 0)),
                 pl.BlockSpec((tm, LANES), lambda b, i: (off + b * steps + i, 0))]
    return pl.pallas_call(
        kern,
        grid=(n_batch, steps),
        in_specs=[row_spec(d), pl.BlockSpec((seq, GROUP_W), lambda b, i: (b, 0)),
                  row_spec(2 * GROUP_W), row_spec(GROUP_W), row_spec(GROUP_W)]
                 + [const(a) for a in consts]
                 + [pl.BlockSpec(memory_space=pl.ANY)] * len(aliased),
        out_specs=out_specs,
        out_shape=out_shapes,
        input_output_aliases={n_in + k: k for k in range(len(aliased))},
        compiler_params=_cparams("arbitrary", "arbitrary"),
        name="merge",
    )(x2, p, ug, y_four, y_attn, *consts, *aliased)


def _exact_row_sums(vals):
    hi = jnp.floor(vals * (1.0 / 256.0))
    lo = vals - 256.0 * hi
    ones = jnp.ones((SUBLANES, LANES), BF16)
    return 256.0 * _dot_nt(ones, hi.astype(BF16)) + _dot_nt(ones, lo.astype(BF16))


def _plan_kernel(route_ref, dest_ref, cnt_out_ref, cnt_ref, start_ref, carry_ref):
    ph, t = pl.program_id(0), pl.program_id(1)
    tm = route_ref.shape[0]
    lane = lax.broadcasted_iota(jnp.int32, (1, LANES), 1)
    rt = route_ref[...]
    e1 = rt[:, 0:1].astype(jnp.int32)
    e2 = rt[:, 1:2].astype(jnp.int32)
    hit1, hit2 = lane == e1, lane == e2
    onehot = jnp.where(hit1 | hit2, 1.0, 0.0)
    colsum = jnp.sum(onehot, axis=0, keepdims=True)

    @pl.when((ph == 0) & (t == 0))
    def _():
        cnt_ref[...] = jnp.zeros_like(cnt_ref)

    @pl.when(ph == 0)
    def _():
        cnt_ref[...] += colsum

    @pl.when((ph == 1) & (t == 0))
    def _():
        cnt = jnp.broadcast_to(cnt_ref[...], (SUBLANES, LANES))
        padded = jnp.floor((cnt + (MOE_BLOCK - 1.0)) * (1.0 / MOE_BLOCK)) * MOE_BLOCK
        lane8 = lax.broadcasted_iota(jnp.int32, (SUBLANES, LANES), 1)
        incl = padded
        sh = 1
        while sh < LANES:
            incl = incl + jnp.where(lane8 >= sh, pltpu.roll(incl, sh, axis=1), 0.0)
            sh *= 2
        start_ref[...] = (incl - padded)[0:1, :]
        carry_ref[...] = jnp.zeros_like(carry_ref)
        cnt_out_ref[...] = cnt

    @pl.when(ph == 1)
    def _():
        r_i = lax.broadcasted_iota(jnp.int32, (tm, tm), 0)
        c_i = lax.broadcasted_iota(jnp.int32, (tm, tm), 1)
        tri = jnp.where(c_i < r_i, 1.0, 0.0).astype(BF16)
        rank = _dot(tri, onehot.astype(BF16))
        base = start_ref[...] + carry_ref[...] + rank
        d1 = _exact_row_sums(jnp.where(hit1, base, 0.0))
        d2 = _exact_row_sums(jnp.where(hit2, base, 0.0))
        sub = lax.broadcasted_iota(jnp.int32, (SUBLANES, tm), 0)
        dest_ref[...] = jnp.where(sub == 0, d1, d2).astype(jnp.int32)
        carry_ref[...] += colsum


def _plan(route, tm):
    rows = route.shape[0]
    n_t = rows // tm
    return pl.pallas_call(
        _plan_kernel,
        grid=(2, n_t),
        in_specs=[pl.BlockSpec((tm, LANES), lambda ph, t: (t, 0))],
        out_specs=[pl.BlockSpec((None, SUBLANES, tm), lambda ph, t: (t * ph, 0, 0)),
                   pl.BlockSpec((SUBLANES, LANES), lambda ph, t: (0, 0))],
        out_shape=[jax.ShapeDtypeStruct((n_t, SUBLANES, tm), jnp.int32),
                   jax.ShapeDtypeStruct((SUBLANES, LANES), F32)],
        scratch_shapes=[pltpu.VMEM((1, LANES), F32)] * 3,
        compiler_params=_cparams("arbitrary", "arbitrary"),
        name="moe_plan",
    )(route)


def _row_copy(src_ref, src_row, dst_ref, dst_row, sem):
    return pltpu.make_async_copy(
        src_ref.at[pl.ds(pl.multiple_of(src_row * SUBLANES, SUBLANES), SUBLANES)],
        dst_ref.at[pl.ds(pl.multiple_of(dst_row * SUBLANES, SUBLANES), SUBLANES)], sem)


def _dispatch_kernel(dest_ref, h2_ref, xs_ref, sem, *, tm):
    def body(r, c):
        for k in range(2):
            _row_copy(h2_ref, r, xs_ref, dest_ref[0, 0, k * tm + r], sem).start()
        return c

    lax.fori_loop(0, tm, body, 0)
    for k in range(2):
        pltpu.make_async_copy(h2_ref, xs_ref.at[pl.ds(0, tm * SUBLANES)], sem).wait()


def _dispatch(dest, h2_tiles, n_slots, tm):
    n_t = dest.shape[0]
    return pl.pallas_call(
        functools.partial(_dispatch_kernel, tm=tm),
        grid=(n_t,),
        in_specs=[pl.BlockSpec((1, 1, 2 * tm), lambda i: (i, 0, 0), memory_space=pltpu.SMEM),
                  pl.BlockSpec((tm * SUBLANES, LANES), lambda i: (i, 0))],
        out_specs=pl.BlockSpec(memory_space=pl.ANY),
        out_shape=jax.ShapeDtypeStruct((n_slots * SUBLANES, LANES), F32),
        scratch_shapes=[pltpu.SemaphoreType.DMA],
        compiler_params=_cparams("arbitrary"),
        name="moe_dispatch",
    )(dest, h2_tiles)


def _expert_kernel(be_ref, bn_ref, x_ref, wg_ref, wu_ref, wd_ref, y_ref):
    i = pl.program_id(0)
    mb = x_ref.shape[0] // SUBLANES
    d = wg_ref.shape[0]

    @pl.when(bn_ref[i] > 0)
    def _():
        live = lax.broadcasted_iota(jnp.int32, (mb, 1), 0) < bn_ref[i]
        x = jnp.concatenate([x_ref[pl.ds(j, mb, stride=SUBLANES), :] for j in range(d // LANES)], axis=1)
        x = jnp.where(live, x, 0.0).astype(BF16)
        g = _dot(x, wg_ref[...].astype(BF16))
        u = _dot(x, wu_ref[...].astype(BF16))
        hid = (_silu(g) * u).astype(BF16)
        y = _dot(hid, wd_ref[...].astype(BF16))
        for j in range(d // LANES):
            y_ref[pl.ds(j, mb, stride=SUBLANES), :] = y[:, j * LANES:(j + 1) * LANES]


def _experts(block_e, block_n, xs, w_gate, w_up, w_down, layer):
    n_blocks = block_e.shape[0]
    _, _, d, de = w_gate.shape
    blk = pl.BlockSpec((MOE_BLOCK * SUBLANES, LANES), lambda i, be, bn: (i, 0))
    return pl.pallas_call(
        _expert_kernel,
        grid_spec=pltpu.PrefetchScalarGridSpec(
            num_scalar_prefetch=2,
            grid=(n_blocks,),
            in_specs=[blk,
                      pl.BlockSpec((None, None, d, de), lambda i, be, bn: (layer, be[i], 0, 0)),
                      pl.BlockSpec((None, None, d, de), lambda i, be, bn: (layer, be[i], 0, 0)),
                      pl.BlockSpec((None, None, de, d), lambda i, be, bn: (layer, be[i], 0, 0))],
            out_specs=blk),
        out_shape=jax.ShapeDtypeStruct(xs.shape, F32),
        compiler_params=_cparams("arbitrary"),
        name="moe_experts",
    )(block_e, block_n, xs, w_gate, w_up, w_down)


def _combine_kernel(dest_ref, x1_ref, route_ref, mod_ref, lng_ref, lnb_ref, y_ref, o_ref, buf_ref, sem,
                    *, mod_row, rows_per_batch):
    tm, d = x1_ref.shape
    i = pl.program_id(0)

    def body(r, c):
        for k in range(2):
            _row_copy(y_ref, dest_ref[0, 0, k * tm + r], buf_ref, k * tm + r, sem).start()
        return c

    lax.fori_loop(0, tm, body, 0)
    for k in range(2):
        pltpu.make_async_copy(y_ref.at[pl.ds(0, tm * SUBLANES)], buf_ref.at[pl.ds(0, tm * SUBLANES)], sem).wait()

    if mod_row is None:
        row = (i * tm) // rows_per_batch
    else:
        row = mod_row
    gate2 = mod_ref[pl.ds(row, 1), :][:, 5 * d:6 * d]
    rt = route_ref[...]
    f = jnp.zeros((tm, d), F32)
    for k in range(2):
        rows = jnp.concatenate(
            [buf_ref[pl.ds(k * tm * SUBLANES + j, tm, stride=SUBLANES), :] for j in range(d // LANES)], axis=1)
        f = f + rows * rt[:, 2 + k:3 + k]
    o_ref[...] = _layer_norm(RES_ALPHA * x1_ref[...] + gate2 * f) * lng_ref[...] + lnb_ref[...]


def _combine(dest, x1, route, mod, ln_g, ln_b, y_tiles, *, tm, row_off, rows, mod_row, rows_per_batch):
    d = x1.shape[1]
    off = row_off // tm
    kern = functools.partial(_combine_kernel, mod_row=mod_row, rows_per_batch=rows_per_batch)
    const = lambda a: pl.BlockSpec(a.shape, lambda i: (0,) * a.ndim)
    return pl.pallas_call(
        kern,
        grid=(rows // tm,),
        in_specs=[pl.BlockSpec((1, 1, 2 * tm), lambda i: (off + i, 0, 0), memory_space=pltpu.SMEM),
                  pl.BlockSpec((tm, d), lambda i: (off + i, 0)),
                  pl.BlockSpec((tm, LANES), lambda i: (off + i, 0)),
                  const(mod), const(ln_g), const(ln_b),
                  pl.BlockSpec(memory_space=pl.ANY)],
        out_specs=pl.BlockSpec((tm, d), lambda i: (i, 0)),
        out_shape=jax.ShapeDtypeStruct((rows, d), F32),
        scratch_shapes=[pltpu.VMEM((2 * tm * SUBLANES, LANES), F32), pltpu.SemaphoreType.DMA],
        compiler_params=_cparams("arbitrary"),
        name="moe_combine",
    )(dest, x1, route, mod, ln_g, ln_b, y_tiles)


def _rope_tables(n_pos):
    rows = n_pos // GRID_W
    row = jnp.repeat(jnp.arange(rows), GRID_W).astype(F32)
    col = jnp.tile(jnp.arange(GRID_W), rows).astype(F32)
    n_freq = HEAD_DIM // 4
    freq = ROPE_BASE ** (-jnp.arange(n_freq, dtype=F32) / n_freq)
    ang_r, ang_c = row[:, None] * freq, col[:, None] * freq
    cos_h = jnp.concatenate([jnp.cos(ang_r)] * 2 + [jnp.cos(ang_c)] * 2, axis=1)
    sin_h = jnp.concatenate([-jnp.sin(ang_r), jnp.sin(ang_r), -jnp.sin(ang_c), jnp.sin(ang_c)], axis=1)
    return jnp.tile(cos_h, (1, 2)), jnp.tile(sin_h, (1, 2))


def _block_table(counts, n_blocks):
    cnt = counts.astype(jnp.int32)
    padded = (cnt + MOE_BLOCK - 1) // MOE_BLOCK * MOE_BLOCK
    pad_end = jnp.cumsum(padded)
    pad_start = pad_end - padded
    blk_start = jnp.arange(n_blocks, dtype=jnp.int32)[:, None] * MOE_BLOCK
    be = jnp.minimum(jnp.sum((pad_end[None, :] <= blk_start).astype(jnp.int32), axis=1), N_EXPERTS - 1)
    mine = be[:, None] == jnp.arange(N_EXPERTS, dtype=jnp.int32)[None, :]
    fill = jnp.sum(jnp.where(mine, cnt[None, :] + pad_start[None, :], 0), axis=1) - blk_start[:, 0]
    return be, jnp.clip(fill, 0, MOE_BLOCK)


def _moe(route, h2_tiles, w_gate, w_up, w_down, layer):
    rows = route.shape[0]
    tm = ROW_TILE
    n_blocks = -(-(2 * rows) // MOE_BLOCK) + N_EXPERTS
    dest8, counts = _plan(route, tm)
    dest = dest8[:, 0:2, :].reshape(rows // tm, 1, 2 * tm)
    block_e, block_n = _block_table(counts[0, :N_EXPERTS], n_blocks)
    xs = _dispatch(dest, h2_tiles, n_blocks * MOE_BLOCK, tm)
    ys = _experts(block_e, block_n, xs, w_gate, w_up, w_down, layer)
    return dest, ys


def kernel(x, c, ctx, c_ctx, w_ada, b_ada, w_in, w_fourier, attn_sink, w_pool, pool_scale, w_sgu, b_sgu,
           w_out, ln1_g, ln1_b, w_router_group, w_router_expert, w_exp_gate, w_exp_up, w_exp_down,
           ln2_g, ln2_b):
    b, s, d = x.shape
    n_ctx = ctx.shape[1]
    n_layers = w_in.shape[0]
    tm = ROW_TILE
    cond = jnp.concatenate([c, c_ctx[None, :], jnp.zeros((SUBLANES - b - 1, d), F32)], axis=0)
    mod_all = _ada(cond, w_ada, b_ada[:, None, :])
    cos_t, sin_t = _rope_tables(s)
    x2 = x.reshape(b * s, d)
    c2 = ctx.reshape(b * n_ctx, d)
    n_sgu = w_sgu.shape[1]
    for layer in range(n_layers):
        last = layer == n_layers - 1
        mod = mod_all[layer]
        w_in_l = w_in[layer].astype(BF16)
        wf = w_fourier[layer].astype(BF16)
        w_pool_bd = jax.scipy.linalg.block_diag(*[w_pool[layer, g] for g in range(w_pool.shape[1])]).astype(BF16)
        w_sgu_stack = w_sgu[layer].reshape(n_sgu * SGU_CHUNK, SGU_CHUNK).astype(BF16)
        b_sgu_exp = jnp.repeat(b_sgu[layer].T, GROUP_W // n_sgu, axis=1)
        w_router = jnp.concatenate([w_router_group[layer], w_router_expert[layer].reshape(d, N_EXPERTS)], axis=1)
        w_router = jnp.pad(w_router, ((0, 0), (0, LANES - w_router.shape[1])))
        wr_hi = w_router.astype(BF16)
        wr_lo = (w_router - wr_hi.astype(F32)).astype(BF16)
        merge_consts = (mod, w_pool_bd, pool_scale[layer][None, :], w_sgu_stack, b_sgu_exp,
                        w_out[layer].astype(BF16), ln1_g[layer][None, :], ln1_b[layer][None, :], wr_hi, wr_lo)
        sink = attn_sink[layer]

        a, q, k_a, k_b, v_a, v_b, p, ug = _proj(x2, mod, w_in_l, cos_t, sin_t,
                                                mod_row=None, rows_per_batch=s, rope=True, tm=tm)
        ac, qc, kc_a, kc_b, vc_a, vc_b, pc, ugc = _proj(c2, mod, w_in_l, cos_t, sin_t,
                                                        mod_row=b, rows_per_batch=n_ctx, rope=False, tm=n_ctx)
        y_four = _fourier(a, wf, s)
        y_attn = _attention(sink, q, k_a, k_b, v_a, v_b, kc_a, kc_b, vc_a, vc_b, seq=s, n_ctx=n_ctx, band=True)
        total = b * s + (0 if last else b * n_ctx)
        merged = _merge(x2, p, ug, y_four, y_attn, *merge_consts, (),
                        mod_row=None, seq=s, tm=tm, row_off=0, total_rows=total)
        if not last:
            yc_four = _fourier_small(ac, wf, n_ctx)
            yc_attn = _attention(sink, qc, kc_a, kc_b, vc_a, vc_b, kc_a, kc_b, vc_a, vc_b,
                                 seq=n_ctx, n_ctx=n_ctx, band=False)
            merged = _merge(c2, pc, ugc, yc_four, yc_attn, *merge_consts, tuple(merged),
                            mod_row=b, seq=n_ctx, tm=n_ctx, row_off=b * s, total_rows=total)
        x1, h2_tiles, route = merged
        dest, ys = _moe(route, h2_tiles, w_exp_gate, w_exp_up, w_exp_down, layer)
        ln_g, ln_b = ln2_g[layer][None, :], ln2_b[layer][None, :]
        x2 = _combine(dest, x1, route, mod, ln_g, ln_b, ys, tm=tm, row_off=0, rows=b * s,
                      mod_row=None, rows_per_batch=s)
        if not last:
            c2 = _combine(dest, x1, route, mod, ln_g, ln_b, ys, tm=tm, row_off=b * s, rows=b * n_ctx,
                          mod_row=b, rows_per_batch=n_ctx)
    return x2.reshape(b, s, d)
```

```python
import functools
import math

import numpy as np
import jax
import jax.numpy as jnp
from jax import lax
from jax.experimental import pallas as pl
from jax.experimental.pallas import tpu as pltpu

GRID_W = 64
HEAD_DIM = 64
GROUP_W = 256
KV_W = 128
WINDOW = 128
POOL_WINDOWS = (2, 4, 8, 16)
SGU_CHUNK = 128
N_GROUPS = 4
EXPERTS_PER_GROUP = 8
N_EXPERTS = 32
ROPE_BASE = 10000.0
LN_EPS = 1e-6
NEG_INF = -1e30
DEPTH = 2
RES_ALPHA = (2 * DEPTH) ** 0.25

LANES = 128
SUBLANES = 8
VMEM_LIMIT = 48 * 1024 * 1024

ROW_TILE = 512
Q_BLOCK = 256
MOE_BLOCK = 256
FFT_R = 64
FFT_PITCH = 72
FFT_UNROLL = 4
DMA_UNROLL = 4

BF16 = jnp.bfloat16
F32 = jnp.float32


def _cparams(*sem):
    return pltpu.CompilerParams(dimension_semantics=sem, vmem_limit_bytes=VMEM_LIMIT)


def _dot(a, b):
    return jnp.dot(a, b, preferred_element_type=F32)


def _dot_nt(a, b):
    return lax.dot_general(a, b, (((1,), (1,)), ((), ())), preferred_element_type=F32)


def _layer_norm(t):
    mu = jnp.mean(t, axis=-1, keepdims=True)
    d = t - mu
    var = jnp.mean(d * d, axis=-1, keepdims=True)
    return d * lax.rsqrt(var + LN_EPS)


def _silu(t):
    return t * (1.0 / (1.0 + jnp.exp(-t)))


def _gelu(t):
    return 0.5 * t * (1.0 + lax.erf(t * (1.0 / math.sqrt(2.0))))


def _ada_kernel(c_ref, w_ref, b_ref, o_ref):
    s = _silu(c_ref[...]).astype(BF16)
    o_ref[...] = _dot(s, w_ref[...].astype(BF16)) + b_ref[...]


def _ada(cond, w_ada, b_ada):
    n_layers, d, n = w_ada.shape
    tn = n // 4
    return pl.pallas_call(
        _ada_kernel,
        grid=(n_layers, n // tn),
        in_specs=[
            pl.BlockSpec((SUBLANES, d), lambda l, j: (0, 0)),
            pl.BlockSpec((None, d, tn), lambda l, j: (l, 0, j)),
            pl.BlockSpec((None, 1, tn), lambda l, j: (l, 0, j)),
        ],
        out_specs=pl.BlockSpec((None, SUBLANES, tn), lambda l, j: (l, 0, j)),
        out_shape=jax.ShapeDtypeStruct((n_layers, SUBLANES, n), F32),
        compiler_params=_cparams("arbitrary", "arbitrary"),
        name="ada",
    )(cond, w_ada, b_ada)


def _rope(t, cos_t, sin_t):
    lane = lax.broadcasted_iota(jnp.int32, t.shape, 1)
    first = (lane % 32) < 16
    partner = jnp.where(first, pltpu.roll(t, LANES - 16, axis=1), pltpu.roll(t, 16, axis=1))
    return t * cos_t + partner * sin_t


def _proj_kernel(x_ref, mod_ref, w_ref, cos_ref, sin_ref,
                 a_ref, q_ref, qs_ref, k_ref, v_ref, p_ref, ug_ref,
                 *, mod_row, rows_per_batch, rope, sh_col, sc_col, a_pitch):
    d = x_ref.shape[1]
    tm = x_ref.shape[0]
    if mod_row is None:
        row = (pl.program_id(0) * tm) // rows_per_batch
    else:
        row = mod_row
    m = mod_ref[pl.ds(row, 1), :]
    shift = m[:, sh_col * d:(sh_col + 1) * d]
    scale = m[:, sc_col * d:(sc_col + 1) * d]
    h = _layer_norm(x_ref[...]) * (1.0 + scale) + shift
    z = _dot(h.astype(BF16), w_ref[...])
    pad = jnp.zeros((a_pitch - FFT_R, LANES), F32)
    for g in range(tm // FFT_R):
        for hf in range(2):
            grp = z[g * FFT_R:(g + 1) * FFT_R, hf * LANES:(hf + 1) * LANES]
            if a_pitch > FFT_R:
                grp = jnp.concatenate([grp, pad], axis=0)
            a_ref[hf, g * a_pitch:(g + 1) * a_pitch, :] = grp
    q0, q1 = z[:, 256:384], z[:, 384:512]
    k = z[:, 512:640]
    if rope:
        cos_t, sin_t = cos_ref[...], sin_ref[...]
        q0, q1, k = _rope(q0, cos_t, sin_t), _rope(q1, cos_t, sin_t), _rope(k, cos_t, sin_t)
    q_ref[:, 0:128] = q0.astype(BF16)
    q_ref[:, 128:256] = q1.astype(BF16)
    qs_ref[:, 0:128] = pltpu.roll(q0, HEAD_DIM, axis=1).astype(BF16)
    qs_ref[:, 128:256] = pltpu.roll(q1, HEAD_DIM, axis=1).astype(BF16)
    k_ref[...] = k.astype(BF16)
    v_ref[...] = z[:, 640:768].astype(BF16)
    p_ref[...] = z[:, 768:1024].astype(BF16)
    ug_ref[...] = z[:, 1024:1536].astype(BF16)


def _proj(x2, mod, w_in, cos_t, sin_t, *, mod_row, rows_per_batch, rope, tm, a_pitch):
    rows, d = x2.shape
    n_in = w_in.shape[1]
    steps_per_seq = cos_t.shape[0] // tm
    kern = functools.partial(_proj_kernel, mod_row=mod_row, rows_per_batch=rows_per_batch,
                             rope=rope, sh_col=0, sc_col=1, a_pitch=a_pitch)
    row_spec = lambda w: pl.BlockSpec((tm, w), lambda i: (i, 0))
    out_w = (256, 256, 128, 128, 256, 512)
    out_dt = (BF16,) * len(out_w)
    ta = tm // FFT_R * a_pitch
    a_spec = pl.BlockSpec((2, ta, LANES), lambda i: (0, i, 0))
    a_shape = jax.ShapeDtypeStruct((2, rows // FFT_R * a_pitch, LANES), F32)
    return pl.pallas_call(
        kern,
        grid=(rows // tm,),
        in_specs=[
            row_spec(d),
            pl.BlockSpec(mod.shape, lambda i: (0, 0)),
            pl.BlockSpec((d, n_in), lambda i: (0, 0)),
            pl.BlockSpec((tm, LANES), lambda i: (i % steps_per_seq, 0)),
            pl.BlockSpec((tm, LANES), lambda i: (i % steps_per_seq, 0)),
        ],
        out_specs=[a_spec] + [row_spec(w) for w in out_w],
        out_shape=[a_shape] + [jax.ShapeDtypeStruct((rows, w), dt) for w, dt in zip(out_w, out_dt)],
        compiler_params=_cparams("arbitrary"),
        name="proj",
    )(x2, mod, w_in, cos_t, sin_t)


def _fft_tables(n_pos):
    r = FFT_R
    assert n_pos == r * r
    kb = np.arange(r)[None, :, None]
    na = np.arange(r)[:, None, None]
    nb = np.arange(r)[None, None, :]
    ang = 2.0 * np.pi * ((kb * (na + r * nb)) % n_pos) / n_pos
    m1 = np.concatenate([np.cos(ang), -np.sin(ang)], axis=1)
    ka = np.arange(r)[:, None]
    n2 = np.arange(r)[None, :]
    ang2 = 2.0 * np.pi * ((ka * n2) % r) / r
    c2, s2 = np.cos(ang2), np.sin(ang2)
    w2 = np.block([[c2, s2], [-s2, c2]])
    return m1, w2


def _channel_tables(n_pos):
    h = HEAD_DIM
    c = np.arange(h)
    ang = 2.0 * np.pi * ((c[:, None] * c[None, :]) % h) / h
    scale = 1.0 / math.sqrt(n_pos * h)
    eye = np.eye(GROUP_W // h)
    cc = np.kron(eye, np.cos(ang)) * scale
    ss = np.kron(eye, np.sin(ang)) * scale
    return np.concatenate([cc, ss], axis=0)


def _fourier_kernel(a_ref, m1_ref, w2_ref, ch_ref, wf_ref, o_ref, z_ref, y_ref):
    r, pt = FFT_R, FFT_PITCH

    def step1(i, c):
        for u in range(FFT_UNROLL):
            na = i * FFT_UNROLL + u
            rows = jnp.concatenate([a_ref[0, pl.ds(na, r, stride=pt), :],
                                    a_ref[1, pl.ds(na, r, stride=pt), :]], axis=1)
            z = _dot(m1_ref[na], rows.astype(BF16))
            base = pl.multiple_of(na * pt, SUBLANES)
            z_ref[0, pl.ds(base, r), :] = z[0:r, 0:LANES]
            z_ref[1, pl.ds(base, r), :] = z[0:r, LANES:]
            z_ref[2, pl.ds(base, r), :] = z[r:, 0:LANES]
            z_ref[3, pl.ds(base, r), :] = z[r:, LANES:]
        return c

    lax.fori_loop(0, r // FFT_UNROLL, step1, 0)

    def step2(i, c):
        for u in range(FFT_UNROLL):
            kb = i * FFT_UNROLL + u
            q = [z_ref[j, pl.ds(kb, r, stride=pt), :] for j in range(4)]
            zs = jnp.concatenate([jnp.concatenate(q[0:2], axis=1),
                                  jnp.concatenate(q[2:4], axis=1)], axis=0)
            y = _dot(w2_ref[...], zs.astype(BF16))
            base = pl.multiple_of(kb * r, r)
            y_ref[0, pl.ds(base, r), :] = y[0:r, 0:LANES]
            y_ref[1, pl.ds(base, r), :] = y[0:r, LANES:]
            y_ref[2, pl.ds(base, r), :] = y[r:, 0:LANES]
            y_ref[3, pl.ds(base, r), :] = y[r:, LANES:]
        return c

    lax.fori_loop(0, r // FFT_UNROLL, step2, 0)

    chunk = 8 * r
    for cidx in range(r * r // chunk):
        yy = jnp.concatenate([y_ref[j, cidx * chunk:(cidx + 1) * chunk, :] for j in range(4)], axis=1)
        f = _dot(yy.astype(BF16), ch_ref[...])
        g = _dot(f.astype(BF16), wf_ref[...])
        for gi in range(chunk // r):
            kb = cidx * (chunk // r) + gi
            z_ref[0, kb * pt:kb * pt + r, :] = g[gi * r:(gi + 1) * r, 0:LANES]
            z_ref[1, kb * pt:kb * pt + r, :] = g[gi * r:(gi + 1) * r, LANES:]

    def step3(i, c):
        for u in range(FFT_UNROLL):
            ka = i * FFT_UNROLL + u
            base = pl.multiple_of(ka * r, r)
            o_ref[pl.ds(base, r), 0:LANES] = z_ref[0, pl.ds(ka, r, stride=pt), :]
            o_ref[pl.ds(base, r), LANES:] = z_ref[1, pl.ds(ka, r, stride=pt), :]
        return c

    lax.fori_loop(0, r // FFT_UNROLL, step3, 0)


def _fourier(a3, w_fourier, n_pos):
    rows = a3.shape[1] // FFT_PITCH * FFT_R
    gw = GROUP_W
    m1, w2 = _fft_tables(n_pos)
    ch = _channel_tables(n_pos)
    const = lambda shape: pl.BlockSpec(shape, lambda b: (0,) * len(shape))
    return pl.pallas_call(
        _fourier_kernel,
        grid=(rows // n_pos,),
        in_specs=[
            pl.BlockSpec((2, FFT_R * FFT_PITCH, LANES), lambda b: (0, b, 0)),
            const(m1.shape), const(w2.shape), const(ch.shape), const(w_fourier.shape),
        ],
        out_specs=pl.BlockSpec((n_pos, gw), lambda b: (b, 0)),
        out_shape=jax.ShapeDtypeStruct((rows, gw), F32),
        scratch_shapes=[pltpu.VMEM((4, FFT_R * FFT_PITCH, LANES), F32), pltpu.VMEM((4, n_pos, LANES), F32)],
        compiler_params=_cparams("arbitrary"),
        name="fourier",
    )(a3, jnp.asarray(m1, BF16), jnp.asarray(w2, BF16), jnp.asarray(ch, BF16), w_fourier)


def _fourier_small_kernel(a_ref, cs_ref, ch_ref, wf_ref, o_ref):
    n = a_ref.shape[1]
    a = jnp.concatenate([a_ref[0], a_ref[1]], axis=1)
    pq = _dot(cs_ref[...], a.astype(BF16))
    y = jnp.concatenate([pq[0:n], pq[n:2 * n]], axis=1).astype(BF16)
    f = _dot(y, ch_ref[...])
    o_ref[...] = _dot(f.astype(BF16), wf_ref[...])


def _fourier_small(a3, w_fourier, n_pos):
    _, rows, _ = a3.shape
    gw = GROUP_W
    k = np.arange(n_pos)
    ang = 2.0 * np.pi * ((k[:, None] * k[None, :]) % n_pos) / n_pos
    cs = np.concatenate([np.cos(ang), -np.sin(ang)], axis=0)
    ch = _channel_tables(n_pos)
    const = lambda shape: pl.BlockSpec(shape, lambda b: (0,) * len(shape))
    return pl.pallas_call(
        _fourier_small_kernel,
        grid=(rows // n_pos,),
        in_specs=[pl.BlockSpec((2, n_pos, LANES), lambda b: (0, b, 0)),
                  const(cs.shape), const(ch.shape), const(w_fourier.shape)],
        out_specs=pl.BlockSpec((n_pos, gw), lambda b: (b, 0)),
        out_shape=jax.ShapeDtypeStruct((rows, gw), F32),
        compiler_params=_cparams("arbitrary"),
        name="fourier_ctx",
    )(a3, jnp.asarray(cs, BF16), jnp.asarray(ch, BF16), w_fourier)


ATTN_SUB = 128


def _attn_kernel(sink_ref, q_ref, qs_ref, k_ref, v_ref, kc_ref, vc_ref, o_ref, *, band, seq):
    qb = q_ref.shape[0]
    sub = ATTN_SUB
    lane = lax.broadcasted_iota(jnp.int32, (1, LANES), 1)
    lo_half = lane < HEAD_DIM
    zero = jnp.zeros((), BF16)
    scale = jnp.asarray(HEAD_DIM ** -0.5, BF16)
    kw = sub + 2 * WINDOW
    for sb in range(qb // sub):
        rows = slice(sb * sub, (sb + 1) * sub)
        qa0, qa1 = q_ref[rows, 0:LANES], q_ref[rows, LANES:]
        qs0, qs1 = qs_ref[rows, 0:LANES], qs_ref[rows, LANES:]
        q_all = jnp.concatenate([jnp.where(lo_half, qa0, zero), jnp.where(lo_half, qs0, zero),
                                 jnp.where(lo_half, zero, qs1), jnp.where(lo_half, zero, qa1)], axis=0) * scale
        if band:
            p0 = pl.program_id(1) * qb + sb * sub
            start = pl.multiple_of(jnp.clip(p0 - WINDOW, 0, seq - kw), WINDOW)
            qpos = p0 + lax.broadcasted_iota(jnp.int32, (sub, 1), 0)
            kpos = start + lax.broadcasted_iota(jnp.int32, (1, kw), 1)
            bias = jnp.where(jnp.abs(qpos - kpos) <= WINDOW, 0.0, NEG_INF)
            keys = jnp.concatenate([k_ref[pl.ds(start, kw), :], kc_ref[...]], axis=0)
            vals = jnp.concatenate([v_ref[pl.ds(start, kw), :], vc_ref[...]], axis=0)
        else:
            keys, vals = kc_ref[...], vc_ref[...]
        s_all = _dot_nt(q_all, keys)
        probs, dens = [], []
        for h in range(4):
            s = s_all[h * sub:(h + 1) * sub, :]
            sink = sink_ref[h]
            if band:
                s = jnp.concatenate([s[:, 0:kw] + bias, s[:, kw:]], axis=1)
            m = jnp.maximum(jnp.max(s, axis=1, keepdims=True), sink)
            p = jnp.exp(s - m)
            dens.append(jnp.sum(p, axis=1, keepdims=True) + jnp.exp(sink - m))
            probs.append(p.astype(BF16))
        o_all = _dot(jnp.concatenate(probs, axis=0), vals)
        o = [o_all[h * sub:(h + 1) * sub, :] / dens[h] for h in range(4)]
        o_ref[rows, 0:LANES] = jnp.where(lo_half, o[0], pltpu.roll(o[1], HEAD_DIM, axis=1)).astype(BF16)
        o_ref[rows, LANES:] = jnp.where(lo_half, pltpu.roll(o[2], HEAD_DIM, axis=1), o[3]).astype(BF16)


def _attention(sink, q, qs, k, v, kc, vc, *, seq, n_ctx, band):
    rows = q.shape[0]
    n_batch = rows // seq
    qb = Q_BLOCK if band else seq
    steps = seq // qb
    kern = functools.partial(_attn_kernel, band=band, seq=seq)
    seq_spec = pl.BlockSpec((seq, KV_W), lambda b, i: (b, 0))
    ctx_spec = pl.BlockSpec((n_ctx, KV_W), lambda b, i: (b, 0))
    q_spec = pl.BlockSpec((qb, GROUP_W), lambda b, i: (b * steps + i, 0))
    return pl.pallas_call(
        kern,
        grid=(n_batch, steps),
        in_specs=[pl.BlockSpec(memory_space=pltpu.SMEM), q_spec, q_spec,
                  seq_spec, seq_spec, ctx_spec, ctx_spec],
        out_specs=q_spec,
        out_shape=jax.ShapeDtypeStruct((rows, GROUP_W), BF16),
        compiler_params=_cparams("arbitrary", "arbitrary"),
        name="attn" if band else "attn_ctx",
    )(sink, q, qs, k, v, kc, vc)


POOL_HALO = max(POOL_WINDOWS) // 2


def _pool(p_ref, t0, tm, seq):
    halo = POOL_HALO
    pack = 2 * SUBLANES
    main = p_ref[pl.ds(t0, tm), :].astype(F32)
    lo = pl.multiple_of(jnp.maximum(t0 - pack, 0), pack)
    hi = pl.multiple_of(jnp.minimum(t0 + tm, seq - pack), pack)
    prev = p_ref[pl.ds(lo, pack), :].astype(F32)[pack - halo:, :]
    nxt = p_ref[pl.ds(hi, pack), :].astype(F32)[:halo, :]
    prev = jnp.where(t0 > 0, prev, 0.0)
    nxt = jnp.where(t0 + tm < seq, nxt, 0.0)
    full = jnp.concatenate([prev, main, nxt], axis=0)
    n = tm + 2 * halo
    lane = lax.broadcasted_iota(jnp.int32, (1, GROUP_W), 1)
    pos = t0 + lax.broadcasted_iota(jnp.int32, (tm, 1), 0)
    s = pltpu.roll(full, 1, axis=0) + full
    pooled = jnp.zeros((tm, GROUP_W), F32)
    for g, w in enumerate(POOL_WINDOWS):
        if g > 0:
            sh = w // 4
            s = pltpu.roll(s, sh, axis=0) + pltpu.roll(s, n - sh, axis=0)
        cnt = (jnp.minimum(pos + w // 2, seq) - jnp.maximum(pos - w // 2, 0)).astype(F32)
        mean = s[halo:halo + tm, :] / cnt
        pooled = jnp.where(lane // (GROUP_W // len(POOL_WINDOWS)) == g, mean, pooled)
    return pooled - main


def _route(logits):
    lane = lax.broadcasted_iota(jnp.int32, logits.shape, 1)
    big = jnp.int32(1 << 20)
    gl = jnp.where(lane < N_GROUPS, logits, -jnp.inf)
    gmax = jnp.max(gl, axis=1, keepdims=True)
    grp = jnp.min(jnp.where(gl == gmax, lane, big), axis=1, keepdims=True)
    gate_group = 1.0 / jnp.sum(jnp.exp(gl - gmax), axis=1, keepdims=True)
    lo = N_GROUPS + EXPERTS_PER_GROUP * grp
    el = jnp.where((lane >= lo) & (lane < lo + EXPERTS_PER_GROUP), logits, -jnp.inf)
    m1 = jnp.max(el, axis=1, keepdims=True)
    i1 = jnp.min(jnp.where(el == m1, lane, big), axis=1, keepdims=True)
    el2 = jnp.where(lane == i1, -jnp.inf, el)
    m2 = jnp.max(el2, axis=1, keepdims=True)
    i2 = jnp.min(jnp.where(el2 == m2, lane, big), axis=1, keepdims=True)
    r = jnp.exp(m2 - m1)
    g1 = gate_group / (1.0 + r)
    g2 = g1 * r
    e1 = (i1 - N_GROUPS).astype(F32)
    e2 = (i2 - N_GROUPS).astype(F32)
    return jnp.where(lane == 0, e1, jnp.where(lane == 1, e2, jnp.where(lane == 2, g1, jnp.where(lane == 3, g2, 0.0))))


def _merge_kernel(x_ref, p_ref, ug_ref, yf_ref, ya_ref, mod_ref, wpool_ref, pscale_ref, wsgu_ref, bsgu_ref,
                  wout_ref, lng_ref, lnb_ref, wrh_ref, wrl_ref, *rest,
                  mod_row, seq, n_alias):
    x1_ref, h2_ref, route_ref = rest[n_alias:]
    tm, d = x_ref.shape
    if mod_row is None:
        row = pl.program_id(0)
    else:
        row = mod_row
    t0 = pl.multiple_of(pl.program_id(1) * tm, tm)
    m = mod_ref[pl.ds(row, 1), :]
    gate1, shift2, scale2 = m[:, 2 * d:3 * d], m[:, 3 * d:4 * d], m[:, 4 * d:5 * d]

    pooled = _pool(p_ref, t0, tm, seq)
    y_pool = _dot(pooled.astype(BF16), wpool_ref[...]) * pscale_ref[...]

    ug = ug_ref[...].astype(F32)
    u = _gelu(ug[:, 0:GROUP_W])
    v = _layer_norm(_gelu(ug[:, GROUP_W:])).astype(BF16)
    lane = lax.broadcasted_iota(jnp.int32, (1, GROUP_W), 1)
    n_heads = wsgu_ref.shape[0] // SGU_CHUNK
    head = lane // (GROUP_W // n_heads)
    mixed = []
    for cidx in range(tm // SGU_CHUNK):
        vc = v[cidx * SGU_CHUNK:(cidx + 1) * SGU_CHUNK, :]
        full = _dot(wsgu_ref[...], vc)
        mc = bsgu_ref[...]
        for hd in range(n_heads):
            mc = mc + jnp.where(head == hd, full[hd * SGU_CHUNK:(hd + 1) * SGU_CHUNK, :], 0.0)
        mixed.append(mc)
    y_sgu = u * jnp.concatenate(mixed, axis=0)

    cat = jnp.concatenate([yf_ref[...].astype(BF16), ya_ref[...], y_pool.astype(BF16), y_sgu.astype(BF16)],
                          axis=1)
    y = _dot(cat, wout_ref[...])
    x1 = _layer_norm(RES_ALPHA * x_ref[...] + gate1 * y) * lng_ref[...] + lnb_ref[...]
    x1_ref[...] = x1
    h2 = _layer_norm(x1) * (1.0 + scale2) + shift2
    for j in range(d // LANES):
        h2_ref[pl.ds(j, tm, stride=SUBLANES), :] = h2[:, j * LANES:(j + 1) * LANES]
    hh = h2.astype(BF16)
    hl = (h2 - hh.astype(F32)).astype(BF16)
    logits = _dot(hh, wrh_ref[...]) + (_dot(hh, wrl_ref[...]) + _dot(hl, wrh_ref[...]))
    route_ref[...] = _route(logits)


def _merge(x2, p, ug, y_four, y_attn, mod, w_pool_bd, pool_scale, w_sgu_stack, b_sgu_exp, w_out,
           ln_g, ln_b, wr_hi, wr_lo, aliased, *, mod_row, seq, tm, row_off, total_rows):
    rows, d = x2.shape
    n_batch, steps = rows // seq, seq // tm
    off = row_off // tm
    kern = functools.partial(_merge_kernel, mod_row=mod_row, seq=seq, n_alias=len(aliased))
    row_spec = lambda w: pl.BlockSpec((tm, w), lambda b, i: (b * steps + i, 0))
    const = lambda a: pl.BlockSpec(a.shape, lambda b, i: (0,) * a.ndim)
    consts = (mod, w_pool_bd, pool_scale, w_sgu_stack, b_sgu_exp, w_out, ln_g, ln_b, wr_hi, wr_lo)
    n_in = 5 + len(consts)
    out_shapes = [jax.ShapeDtypeStruct((total_rows, d), F32),
                  jax.ShapeDtypeStruct((total_rows * SUBLANES, LANES), F32),
                  jax.ShapeDtypeStruct((total_rows, LANES), F32)]
    out_specs = [pl.BlockSpec((tm, d), lambda b, i: (off + b * steps + i, 0)),
                 pl.BlockSpec((tm * SUBLANES, LANES), lambda b, i: (off + b * steps + i, 0)),
                 pl.BlockSpec((tm, LANES), lambda b, i: (off + b * steps + i, 0))]
    return pl.pallas_call(
        kern,
        grid=(n_batch, steps),
        in_specs=[row_spec(d), pl.BlockSpec((seq, GROUP_W), lambda b, i: (b, 0)),
                  row_spec(2 * GROUP_W), row_spec(GROUP_W), row_spec(GROUP_W)]
                 + [const(a) for a in consts]
                 + [pl.BlockSpec(memory_space=pl.ANY)] * len(aliased),
        out_specs=out_specs,
        out_shape=out_shapes,
        input_output_aliases={n_in + k: k for k in range(len(aliased))},
        compiler_params=_cparams("arbitrary", "arbitrary"),
        name="merge",
    )(x2, p, ug, y_four, y_attn, *consts, *aliased)


def _exact_row_sums(vals):
    hi = jnp.floor(vals * (1.0 / 256.0))
    lo = vals - 256.0 * hi
    ones = jnp.ones((SUBLANES, LANES), BF16)
    return 256.0 * _dot_nt(ones, hi.astype(BF16)) + _dot_nt(ones, lo.astype(BF16))


def _plan_kernel(route_ref, dest_ref, cnt_out_ref, cnt_ref, start_ref, carry_ref):
    ph, t = pl.program_id(0), pl.program_id(1)
    tm = route_ref.shape[0]
    lane = lax.broadcasted_iota(jnp.int32, (1, LANES), 1)
    rt = route_ref[...]
    e1 = rt[:, 0:1].astype(jnp.int32)
    e2 = rt[:, 1:2].astype(jnp.int32)
    hit1, hit2 = lane == e1, lane == e2
    onehot = jnp.where(hit1 | hit2, 1.0, 0.0)
    colsum = jnp.sum(onehot, axis=0, keepdims=True)

    @pl.when((ph == 0) & (t == 0))
    def _():
        cnt_ref[...] = jnp.zeros_like(cnt_ref)

    @pl.when(ph == 0)
    def _():
        cnt_ref[...] += colsum

    @pl.when((ph == 1) & (t == 0))
    def _():
        cnt = jnp.broadcast_to(cnt_ref[...], (SUBLANES, LANES))
        padded = jnp.floor((cnt + (MOE_BLOCK - 1.0)) * (1.0 / MOE_BLOCK)) * MOE_BLOCK
        lane8 = lax.broadcasted_iota(jnp.int32, (SUBLANES, LANES), 1)
        incl = padded
        sh = 1
        while sh < LANES:
            incl = incl + jnp.where(lane8 >= sh, pltpu.roll(incl, sh, axis=1), 0.0)
            sh *= 2
        start_ref[...] = (incl - padded)[0:1, :]
        carry_ref[...] = jnp.zeros_like(carry_ref)
        cnt_out_ref[...] = cnt

    @pl.when(ph == 1)
    def _():
        r_i = lax.broadcasted_iota(jnp.int32, (tm, tm), 0)
        c_i = lax.broadcasted_iota(jnp.int32, (tm, tm), 1)
        tri = jnp.where(c_i < r_i, 1.0, 0.0).astype(BF16)
        rank = _dot(tri, onehot.astype(BF16))
        base = start_ref[...] + carry_ref[...] + rank
        d1 = _exact_row_sums(jnp.where(hit1, base, 0.0))
        d2 = _exact_row_sums(jnp.where(hit2, base, 0.0))
        sub = lax.broadcasted_iota(jnp.int32, (SUBLANES, tm), 0)
        dest_ref[...] = jnp.where(sub == 0, d1, d2).astype(jnp.int32)
        carry_ref[...] += colsum


def _plan(route, tm):
    rows = route.shape[0]
    n_t = rows // tm
    return pl.pallas_call(
        _plan_kernel,
        grid=(2, n_t),
        in_specs=[pl.BlockSpec((tm, LANES), lambda ph, t: (t, 0))],
        out_specs=[pl.BlockSpec((None, SUBLANES, tm), lambda ph, t: (t * ph, 0, 0)),
                   pl.BlockSpec((SUBLANES, LANES), lambda ph, t: (0, 0))],
        out_shape=[jax.ShapeDtypeStruct((n_t, SUBLANES, tm), jnp.int32),
                   jax.ShapeDtypeStruct((SUBLANES, LANES), F32)],
        scratch_shapes=[pltpu.VMEM((1, LANES), F32)] * 3,
        compiler_params=_cparams("arbitrary", "arbitrary"),
        name="moe_plan",
    )(route)


def _row_copy(src_ref, src_row, dst_ref, dst_row, sem):
    return pltpu.make_async_copy(
        src_ref.at[pl.ds(pl.multiple_of(src_row * SUBLANES, SUBLANES), SUBLANES)],
        dst_ref.at[pl.ds(pl.multiple_of(dst_row * SUBLANES, SUBLANES), SUBLANES)], sem)


def _dispatch_kernel(dest_ref, h2_ref, xs_ref, sem, *, tm):
    def body(r4, c):
        for u in range(DMA_UNROLL):
            r = r4 * DMA_UNROLL + u
            for k in range(2):
                _row_copy(h2_ref, r, xs_ref, dest_ref[0, 0, k * tm + r], sem).start()
        return c

    lax.fori_loop(0, tm // DMA_UNROLL, body, 0)
    for k in range(2):
        pltpu.make_async_copy(h2_ref, xs_ref.at[pl.ds(0, tm * SUBLANES)], sem).wait()


def _dispatch(dest, h2_tiles, n_slots, tm):
    n_t = dest.shape[0]
    return pl.pallas_call(
        functools.partial(_dispatch_kernel, tm=tm),
        grid=(n_t,),
        in_specs=[pl.BlockSpec((1, 1, 2 * tm), lambda i: (i, 0, 0), memory_space=pltpu.SMEM),
                  pl.BlockSpec((tm * SUBLANES, LANES), lambda i: (i, 0))],
        out_specs=pl.BlockSpec(memory_space=pl.ANY),
        out_shape=jax.ShapeDtypeStruct((n_slots * SUBLANES, LANES), F32),
        scratch_shapes=[pltpu.SemaphoreType.DMA],
        compiler_params=_cparams("arbitrary"),
        name="moe_dispatch",
    )(dest, h2_tiles)


def _expert_kernel(be_ref, bn_ref, x_ref, wg_ref, wu_ref, wd_ref, y_ref):
    i = pl.program_id(0)
    mb = x_ref.shape[0] // SUBLANES
    d = wg_ref.shape[0]

    @pl.when(bn_ref[i] > 0)
    def _():
        live = lax.broadcasted_iota(jnp.int32, (mb, 1), 0) < bn_ref[i]
        x = jnp.concatenate([x_ref[pl.ds(j, mb, stride=SUBLANES), :] for j in range(d // LANES)], axis=1)
        x = jnp.where(live, x, 0.0).astype(BF16)
        g = _dot(x, wg_ref[...].astype(BF16))
        u = _dot(x, wu_ref[...].astype(BF16))
        hid = (_silu(g) * u).astype(BF16)
        y = _dot(hid, wd_ref[...].astype(BF16))
        for j in range(d // LANES):
            y_ref[pl.ds(j, mb, stride=SUBLANES), :] = y[:, j * LANES:(j + 1) * LANES]


def _experts(block_e, block_n, xs, w_gate, w_up, w_down, layer):
    n_blocks = block_e.shape[0]
    _, _, d, de = w_gate.shape
    blk = pl.BlockSpec((MOE_BLOCK * SUBLANES, LANES), lambda i, be, bn: (i, 0))
    return pl.pallas_call(
        _expert_kernel,
        grid_spec=pltpu.PrefetchScalarGridSpec(
            num_scalar_prefetch=2,
            grid=(n_blocks,),
            in_specs=[blk,
                      pl.BlockSpec((None, None, d, de), lambda i, be, bn: (layer, be[i], 0, 0)),
                      pl.BlockSpec((None, None, d, de), lambda i, be, bn: (layer, be[i], 0, 0)),
                      pl.BlockSpec((None, None, de, d), lambda i, be, bn: (layer, be[i], 0, 0))],
            out_specs=blk),
        out_shape=jax.ShapeDtypeStruct(xs.shape, F32),
        compiler_params=_cparams("arbitrary"),
        name="moe_experts",
    )(block_e, block_n, xs, w_gate, w_up, w_down)


def _combine_kernel(dest_ref, dest_next_ref, x1_ref, route_ref, mod_ref, lng_ref, lnb_ref, y_ref, o_ref,
                    buf_ref, sem, *, mod_row, rows_per_batch):
    tm, d = x1_ref.shape
    i = pl.program_id(0)
    n = pl.num_programs(0)
    slot = i % 2

    def gather(idx_ref, s):
        def body(r4, c):
            for u in range(DMA_UNROLL):
                r = r4 * DMA_UNROLL + u
                for k in range(2):
                    _row_copy(y_ref, idx_ref[0, 0, k * tm + r], buf_ref.at[s], k * tm + r, sem.at[s]).start()
            return c

        lax.fori_loop(0, tm // DMA_UNROLL, body, 0)

    @pl.when(i == 0)
    def _():
        gather(dest_ref, 0)

    @pl.when(i + 1 < n)
    def _():
        gather(dest_next_ref, 1 - slot)

    for k in range(2):
        pltpu.make_async_copy(y_ref.at[pl.ds(0, tm * SUBLANES)],
                              buf_ref.at[slot, pl.ds(0, tm * SUBLANES)], sem.at[slot]).wait()

    if mod_row is None:
        row = (i * tm) // rows_per_batch
    else:
        row = mod_row
    gate2 = mod_ref[pl.ds(row, 1), :][:, 5 * d:6 * d]
    rt = route_ref[...]
    f = jnp.zeros((tm, d), F32)
    for k in range(2):
        rows = jnp.concatenate(
            [buf_ref[slot, pl.ds(k * tm * SUBLANES + j, tm, stride=SUBLANES), :] for j in range(d // LANES)],
            axis=1)
        f = f + rows * rt[:, 2 + k:3 + k]
    o_ref[...] = _layer_norm(RES_ALPHA * x1_ref[...] + gate2 * f) * lng_ref[...] + lnb_ref[...]


def _combine(dest, x1, route, mod, ln_g, ln_b, y_tiles, *, tm, row_off, rows, mod_row, rows_per_batch):
    d = x1.shape[1]
    off = row_off // tm
    steps = rows // tm
    kern = functools.partial(_combine_kernel, mod_row=mod_row, rows_per_batch=rows_per_batch)
    const = lambda a: pl.BlockSpec(a.shape, lambda i: (0,) * a.ndim)
    return pl.pallas_call(
        kern,
        grid=(steps,),
        in_specs=[pl.BlockSpec((1, 1, 2 * tm), lambda i: (off + i, 0, 0), memory_space=pltpu.SMEM),
                  pl.BlockSpec((1, 1, 2 * tm), lambda i: (off + jnp.minimum(i + 1, steps - 1), 0, 0),
                               memory_space=pltpu.SMEM),
                  pl.BlockSpec((tm, d), lambda i: (off + i, 0)),
                  pl.BlockSpec((tm, LANES), lambda i: (off + i, 0)),
                  const(mod), const(ln_g), const(ln_b),
                  pl.BlockSpec(memory_space=pl.ANY)],
        out_specs=pl.BlockSpec((tm, d), lambda i: (i, 0)),
        out_shape=jax.ShapeDtypeStruct((rows, d), F32),
        scratch_shapes=[pltpu.VMEM((2, 2 * tm * SUBLANES, LANES), F32), pltpu.SemaphoreType.DMA((2,))],
        compiler_params=_cparams("arbitrary"),
        name="moe_combine",
    )(dest, dest, x1, route, mod, ln_g, ln_b, y_tiles)


def _rope_tables(n_pos):
    rows = n_pos // GRID_W
    row = jnp.repeat(jnp.arange(rows), GRID_W).astype(F32)
    col = jnp.tile(jnp.arange(GRID_W), rows).astype(F32)
    n_freq = HEAD_DIM // 4
    freq = ROPE_BASE ** (-jnp.arange(n_freq, dtype=F32) / n_freq)
    ang_r, ang_c = row[:, None] * freq, col[:, None] * freq
    cos_h = jnp.concatenate([jnp.cos(ang_r)] * 2 + [jnp.cos(ang_c)] * 2, axis=1)
    sin_h = jnp.concatenate([-jnp.sin(ang_r), jnp.sin(ang_r), -jnp.sin(ang_c), jnp.sin(ang_c)], axis=1)
    return jnp.tile(cos_h, (1, 2)), jnp.tile(sin_h, (1, 2))


def _block_table(counts, n_blocks):
    cnt = counts.astype(jnp.int32)
    padded = (cnt + MOE_BLOCK - 1) // MOE_BLOCK * MOE_BLOCK
    pad_end = jnp.cumsum(padded)
    pad_start = pad_end - padded
    blk_start = jnp.arange(n_blocks, dtype=jnp.int32)[:, None] * MOE_BLOCK
    be = jnp.minimum(jnp.sum((pad_end[None, :] <= blk_start).astype(jnp.int32), axis=1), N_EXPERTS - 1)
    mine = be[:, None] == jnp.arange(N_EXPERTS, dtype=jnp.int32)[None, :]
    fill = jnp.sum(jnp.where(mine, cnt[None, :] + pad_start[None, :], 0), axis=1) - blk_start[:, 0]
    return be, jnp.clip(fill, 0, MOE_BLOCK)


def _moe(route, h2_tiles, w_gate, w_up, w_down, layer):
    rows = route.shape[0]
    tm = ROW_TILE
    n_blocks = -(-(2 * rows) // MOE_BLOCK) + N_EXPERTS
    dest8, counts = _plan(route, tm)
    dest = dest8[:, 0:2, :].reshape(rows // tm, 1, 2 * tm)
    block_e, block_n = _block_table(counts[0, :N_EXPERTS], n_blocks)
    xs = _dispatch(dest, h2_tiles, n_blocks * MOE_BLOCK, tm)
    ys = _experts(block_e, block_n, xs, w_gate, w_up, w_down, layer)
    return dest, ys


def kernel(x, c, ctx, c_ctx, w_ada, b_ada, w_in, w_fourier, attn_sink, w_pool, pool_scale, w_sgu, b_sgu,
           w_out, ln1_g, ln1_b, w_router_group, w_router_expert, w_exp_gate, w_exp_up, w_exp_down,
           ln2_g, ln2_b):
    b, s, d = x.shape
    n_ctx = ctx.shape[1]
    n_layers = w_in.shape[0]
    tm = ROW_TILE
    cond = jnp.concatenate([c, c_ctx[None, :], jnp.zeros((SUBLANES - b - 1, d), F32)], axis=0)
    mod_all = _ada(cond, w_ada, b_ada[:, None, :])
    cos_t, sin_t = _rope_tables(s)
    x2 = x.reshape(b * s, d)
    c2 = ctx.reshape(b * n_ctx, d)
    n_sgu = w_sgu.shape[1]
    for layer in range(n_layers):
        last = layer == n_layers - 1
        mod = mod_all[layer]
        w_in_l = w_in[layer].astype(BF16)
        wf = w_fourier[layer].astype(BF16)
        w_pool_bd = jax.scipy.linalg.block_diag(*[w_pool[layer, g] for g in range(w_pool.shape[1])]).astype(BF16)
        w_sgu_stack = w_sgu[layer].reshape(n_sgu * SGU_CHUNK, SGU_CHUNK).astype(BF16)
        b_sgu_exp = jnp.repeat(b_sgu[layer].T, GROUP_W // n_sgu, axis=1)
        w_router = jnp.concatenate([w_router_group[layer], w_router_expert[layer].reshape(d, N_EXPERTS)], axis=1)
        w_router = jnp.pad(w_router, ((0, 0), (0, LANES - w_router.shape[1])))
        wr_hi = w_router.astype(BF16)
        wr_lo = (w_router - wr_hi.astype(F32)).astype(BF16)
        merge_consts = (mod, w_pool_bd, pool_scale[layer][None, :], w_sgu_stack, b_sgu_exp,
                        w_out[layer].astype(BF16), ln1_g[layer][None, :], ln1_b[layer][None, :], wr_hi, wr_lo)
        sink = attn_sink[layer]

        a, q, qs, k, v, p, ug = _proj(x2, mod, w_in_l, cos_t, sin_t, mod_row=None, rows_per_batch=s,
                                      rope=True, tm=tm, a_pitch=FFT_PITCH)
        ac, qc, qsc, kc, vc, pc, ugc = _proj(c2, mod, w_in_l, cos_t, sin_t, mod_row=b, rows_per_batch=n_ctx,
                                             rope=False, tm=n_ctx, a_pitch=FFT_R)
        y_four = _fourier(a, wf, s)
        y_attn = _attention(sink, q, qs, k, v, kc, vc, seq=s, n_ctx=n_ctx, band=True)
        total = b * s + (0 if last else b * n_ctx)
        merged = _merge(x2, p, ug, y_four, y_attn, *merge_consts, (),
                        mod_row=None, seq=s, tm=tm, row_off=0, total_rows=total)
        if not last:
            yc_four = _fourier_small(ac, wf, n_ctx)
            yc_attn = _attention(sink, qc, qsc, kc, vc, kc, vc, seq=n_ctx, n_ctx=n_ctx, band=False)
            merged = _merge(c2, pc, ugc, yc_four, yc_attn, *merge_consts, tuple(merged),
                            mod_row=b, seq=n_ctx, tm=n_ctx, row_off=b * s, total_rows=total)
        x1, h2_tiles, route = merged
        dest, ys = _moe(route, h2_tiles, w_exp_gate, w_exp_up, w_exp_down, layer)
        ln_g, ln_b = ln2_g[layer][None, :], ln2_b[layer][None, :]
        x2 = _combine(dest, x1, route, mod, ln_g, ln_b, ys, tm=tm, row_off=0, rows=b * s,
                      mod_row=None, rows_per_batch=s)
        if not last:
            c2 = _combine(dest, x1, route, mod, ln_g, ln_b, ys, tm=tm, row_off=b * s, rows=b * n_ctx,
                          mod_row=b, rows_per_batch=n_ctx)
    return x2.reshape(b, s, d)
```

```python
import functools
import math

import numpy as np
import jax
import jax.numpy as jnp
from jax import lax
from jax.experimental import pallas as pl
from jax.experimental.pallas import tpu as pltpu

GRID_W = 64
HEAD_DIM = 64
GROUP_W = 256
KV_W = 128
WINDOW = 128
POOL_WINDOWS = (2, 4, 8, 16)
SGU_CHUNK = 128
N_GROUPS = 4
EXPERTS_PER_GROUP = 8
N_EXPERTS = 32
ROPE_BASE = 10000.0
LN_EPS = 1e-6
NEG_INF = -1e30
DEPTH = 2
RES_ALPHA = (2 * DEPTH) ** 0.25

LANES = 128
SUBLANES = 8
VMEM_LIMIT = 48 * 1024 * 1024

ROW_TILE = 512
Q_BLOCK = 256
MOE_BLOCK = 256
FFT_R = 64
FFT_PITCH = 72
FFT_UNROLL = 4
DMA_UNROLL = 4

BF16 = jnp.bfloat16
F32 = jnp.float32


def _cparams(*sem):
    return pltpu.CompilerParams(dimension_semantics=sem, vmem_limit_bytes=VMEM_LIMIT)


def _dot(a, b):
    return jnp.dot(a, b, preferred_element_type=F32)


def _dot_nt(a, b):
    return lax.dot_general(a, b, (((1,), (1,)), ((), ())), preferred_element_type=F32)


def _layer_norm(t):
    mu = jnp.mean(t, axis=-1, keepdims=True)
    d = t - mu
    var = jnp.mean(d * d, axis=-1, keepdims=True)
    return d * lax.rsqrt(var + LN_EPS)


def _silu(t):
    return t * (1.0 / (1.0 + jnp.exp(-t)))


def _gelu(t):
    return 0.5 * t * (1.0 + lax.erf(t * (1.0 / math.sqrt(2.0))))


ROW_WORDS = 4
HI_MASK = 0xFFFF0000


def _pack_rows(t):
    half = t.shape[1] // 2
    lo = lax.bitcast_convert_type(t[:, :half].astype(BF16).astype(F32), jnp.uint32)
    hi = lax.bitcast_convert_type(t[:, half:].astype(BF16).astype(F32), jnp.uint32)
    return (lo >> 16) | (hi & jnp.uint32(HI_MASK))


def _unpack_rows(w):
    return (lax.bitcast_convert_type(w << 16, F32),
            lax.bitcast_convert_type(w & jnp.uint32(HI_MASK), F32))


def _store_rows(ref, t):
    w = _pack_rows(t)
    for j in range(ROW_WORDS):
        ref[pl.ds(j, t.shape[0], stride=ROW_WORDS), :] = w[:, j * LANES:(j + 1) * LANES]


def _load_rows(ref, first, m):
    w = jnp.concatenate([ref[pl.ds(first * ROW_WORDS + j, m, stride=ROW_WORDS), :] for j in range(ROW_WORDS)],
                        axis=1)
    return _unpack_rows(w)


def _ada_kernel(c_ref, w_ref, b_ref, o_ref):
    s = _silu(c_ref[...]).astype(BF16)
    o_ref[...] = _dot(s, w_ref[...].astype(BF16)) + b_ref[...]


def _ada(cond, w_ada, b_ada):
    n_layers, d, n = w_ada.shape
    tn = n // 4
    return pl.pallas_call(
        _ada_kernel,
        grid=(n_layers, n // tn),
        in_specs=[
            pl.BlockSpec((SUBLANES, d), lambda l, j: (0, 0)),
            pl.BlockSpec((None, d, tn), lambda l, j: (l, 0, j)),
            pl.BlockSpec((None, 1, tn), lambda l, j: (l, 0, j)),
        ],
        out_specs=pl.BlockSpec((None, SUBLANES, tn), lambda l, j: (l, 0, j)),
        out_shape=jax.ShapeDtypeStruct((n_layers, SUBLANES, n), F32),
        compiler_params=_cparams("arbitrary", "arbitrary"),
        name="ada",
    )(cond, w_ada, b_ada)


def _rope(t, cos_t, sin_t):
    lane = lax.broadcasted_iota(jnp.int32, t.shape, 1)
    first = (lane % 32) < 16
    partner = jnp.where(first, pltpu.roll(t, LANES - 16, axis=1), pltpu.roll(t, 16, axis=1))
    return t * cos_t + partner * sin_t


def _proj_kernel(x_ref, mod_ref, w_ref, cos_ref, sin_ref,
                 a_ref, q_ref, qs_ref, k_ref, v_ref, p_ref, ug_ref,
                 *, mod_row, rows_per_batch, rope, sh_col, sc_col, a_pitch):
    d = x_ref.shape[1]
    tm = x_ref.shape[0]
    if mod_row is None:
        row = (pl.program_id(0) * tm) // rows_per_batch
    else:
        row = mod_row
    m = mod_ref[pl.ds(row, 1), :]
    shift = m[:, sh_col * d:(sh_col + 1) * d]
    scale = m[:, sc_col * d:(sc_col + 1) * d]
    h = _layer_norm(x_ref[...]) * (1.0 + scale) + shift
    z = _dot(h.astype(BF16), w_ref[...])
    pad = jnp.zeros((a_pitch - FFT_R, LANES), F32)
    for g in range(tm // FFT_R):
        for hf in range(2):
            grp = z[g * FFT_R:(g + 1) * FFT_R, hf * LANES:(hf + 1) * LANES]
            if a_pitch > FFT_R:
                grp = jnp.concatenate([grp, pad], axis=0)
            a_ref[hf, g * a_pitch:(g + 1) * a_pitch, :] = grp
    q0, q1 = z[:, 256:384], z[:, 384:512]
    k = z[:, 512:640]
    if rope:
        cos_t, sin_t = cos_ref[...], sin_ref[...]
        q0, q1, k = _rope(q0, cos_t, sin_t), _rope(q1, cos_t, sin_t), _rope(k, cos_t, sin_t)
    q_ref[:, 0:128] = q0.astype(BF16)
    q_ref[:, 128:256] = q1.astype(BF16)
    qs_ref[:, 0:128] = pltpu.roll(q0, HEAD_DIM, axis=1).astype(BF16)
    qs_ref[:, 128:256] = pltpu.roll(q1, HEAD_DIM, axis=1).astype(BF16)
    k_ref[...] = k.astype(BF16)
    v_ref[...] = z[:, 640:768].astype(BF16)
    p_ref[...] = z[:, 768:1024].astype(BF16)
    ug_ref[...] = z[:, 1024:1536].astype(BF16)


def _proj(x2, mod, w_in, cos_t, sin_t, *, mod_row, rows_per_batch, rope, tm, a_pitch):
    rows, d = x2.shape
    n_in = w_in.shape[1]
    steps_per_seq = cos_t.shape[0] // tm
    kern = functools.partial(_proj_kernel, mod_row=mod_row, rows_per_batch=rows_per_batch,
                             rope=rope, sh_col=0, sc_col=1, a_pitch=a_pitch)
    row_spec = lambda w: pl.BlockSpec((tm, w), lambda i: (i, 0))
    out_w = (256, 256, 128, 128, 256, 512)
    out_dt = (BF16,) * len(out_w)
    ta = tm // FFT_R * a_pitch
    a_spec = pl.BlockSpec((2, ta, LANES), lambda i: (0, i, 0))
    a_shape = jax.ShapeDtypeStruct((2, rows // FFT_R * a_pitch, LANES), F32)
    return pl.pallas_call(
        kern,
        grid=(rows // tm,),
        in_specs=[
            row_spec(d),
            pl.BlockSpec(mod.shape, lambda i: (0, 0)),
            pl.BlockSpec((d, n_in), lambda i: (0, 0)),
            pl.BlockSpec((tm, LANES), lambda i: (i % steps_per_seq, 0)),
            pl.BlockSpec((tm, LANES), lambda i: (i % steps_per_seq, 0)),
        ],
        out_specs=[a_spec] + [row_spec(w) for w in out_w],
        out_shape=[a_shape] + [jax.ShapeDtypeStruct((rows, w), dt) for w, dt in zip(out_w, out_dt)],
        compiler_params=_cparams("arbitrary"),
        name="proj",
    )(x2, mod, w_in, cos_t, sin_t)


def _fft_tables(n_pos):
    r = FFT_R
    assert n_pos == r * r
    kb = np.arange(r)[None, :, None]
    na = np.arange(r)[:, None, None]
    nb = np.arange(r)[None, None, :]
    ang = 2.0 * np.pi * ((kb * (na + r * nb)) % n_pos) / n_pos
    m1 = np.concatenate([np.cos(ang), -np.sin(ang)], axis=1)
    ka = np.arange(r)[:, None]
    n2 = np.arange(r)[None, :]
    ang2 = 2.0 * np.pi * ((ka * n2) % r) / r
    c2, s2 = np.cos(ang2), np.sin(ang2)
    w2 = np.block([[c2, s2], [-s2, c2]])
    return m1, w2


def _channel_tables(n_pos):
    h = HEAD_DIM
    c = np.arange(h)
    ang = 2.0 * np.pi * ((c[:, None] * c[None, :]) % h) / h
    scale = 1.0 / math.sqrt(n_pos * h)
    eye = np.eye(GROUP_W // h)
    cc = np.kron(eye, np.cos(ang)) * scale
    ss = np.kron(eye, np.sin(ang)) * scale
    return np.concatenate([cc, ss], axis=0)


def _fourier_kernel(a_ref, m1_ref, w2_ref, ch_ref, wf_ref, o_ref, z_ref, y_ref):
    r, pt = FFT_R, FFT_PITCH

    def step1(i, c):
        for u in range(FFT_UNROLL):
            na = i * FFT_UNROLL + u
            rows = jnp.concatenate([a_ref[0, pl.ds(na, r, stride=pt), :],
                                    a_ref[1, pl.ds(na, r, stride=pt), :]], axis=1)
            z = _dot(m1_ref[na], rows.astype(BF16))
            base = pl.multiple_of(na * pt, SUBLANES)
            z_ref[0, pl.ds(base, r), :] = z[0:r, 0:LANES]
            z_ref[1, pl.ds(base, r), :] = z[0:r, LANES:]
            z_ref[2, pl.ds(base, r), :] = z[r:, 0:LANES]
            z_ref[3, pl.ds(base, r), :] = z[r:, LANES:]
        return c

    lax.fori_loop(0, r // FFT_UNROLL, step1, 0)

    def step2(i, c):
        for u in range(FFT_UNROLL):
            kb = i * FFT_UNROLL + u
            q = [z_ref[j, pl.ds(kb, r, stride=pt), :] for j in range(4)]
            zs = jnp.concatenate([jnp.concatenate(q[0:2], axis=1),
                                  jnp.concatenate(q[2:4], axis=1)], axis=0)
            y = _dot(w2_ref[...], zs.astype(BF16))
            base = pl.multiple_of(kb * r, r)
            y_ref[0, pl.ds(base, r), :] = y[0:r, 0:LANES]
            y_ref[1, pl.ds(base, r), :] = y[0:r, LANES:]
            y_ref[2, pl.ds(base, r), :] = y[r:, 0:LANES]
            y_ref[3, pl.ds(base, r), :] = y[r:, LANES:]
        return c

    lax.fori_loop(0, r // FFT_UNROLL, step2, 0)

    chunk = 8 * r
    for cidx in range(r * r // chunk):
        yy = jnp.concatenate([y_ref[j, cidx * chunk:(cidx + 1) * chunk, :] for j in range(4)], axis=1)
        f = _dot(yy.astype(BF16), ch_ref[...])
        g = _dot(f.astype(BF16), wf_ref[...])
        for gi in range(chunk // r):
            kb = cidx * (chunk // r) + gi
            z_ref[0, kb * pt:kb * pt + r, :] = g[gi * r:(gi + 1) * r, 0:LANES]
            z_ref[1, kb * pt:kb * pt + r, :] = g[gi * r:(gi + 1) * r, LANES:]

    def step3(i, c):
        for u in range(FFT_UNROLL):
            ka = i * FFT_UNROLL + u
            base = pl.multiple_of(ka * r, r)
            o_ref[pl.ds(base, r), 0:LANES] = z_ref[0, pl.ds(ka, r, stride=pt), :]
            o_ref[pl.ds(base, r), LANES:] = z_ref[1, pl.ds(ka, r, stride=pt), :]
        return c

    lax.fori_loop(0, r // FFT_UNROLL, step3, 0)


def _fourier(a3, w_fourier, n_pos):
    rows = a3.shape[1] // FFT_PITCH * FFT_R
    gw = GROUP_W
    m1, w2 = _fft_tables(n_pos)
    ch = _channel_tables(n_pos)
    const = lambda shape: pl.BlockSpec(shape, lambda b: (0,) * len(shape))
    return pl.pallas_call(
        _fourier_kernel,
        grid=(rows // n_pos,),
        in_specs=[
            pl.BlockSpec((2, FFT_R * FFT_PITCH, LANES), lambda b: (0, b, 0)),
            const(m1.shape), const(w2.shape), const(ch.shape), const(w_fourier.shape),
        ],
        out_specs=pl.BlockSpec((n_pos, gw), lambda b: (b, 0)),
        out_shape=jax.ShapeDtypeStruct((rows, gw), F32),
        scratch_shapes=[pltpu.VMEM((4, FFT_R * FFT_PITCH, LANES), F32), pltpu.VMEM((4, n_pos, LANES), F32)],
        compiler_params=_cparams("arbitrary"),
        name="fourier",
    )(a3, jnp.asarray(m1, BF16), jnp.asarray(w2, BF16), jnp.asarray(ch, BF16), w_fourier)


def _fourier_small_kernel(a_ref, cs_ref, ch_ref, wf_ref, o_ref):
    n = a_ref.shape[1]
    a = jnp.concatenate([a_ref[0], a_ref[1]], axis=1)
    pq = _dot(cs_ref[...], a.astype(BF16))
    y = jnp.concatenate([pq[0:n], pq[n:2 * n]], axis=1).astype(BF16)
    f = _dot(y, ch_ref[...])
    o_ref[...] = _dot(f.astype(BF16), wf_ref[...])


def _fourier_small(a3, w_fourier, n_pos):
    _, rows, _ = a3.shape
    gw = GROUP_W
    k = np.arange(n_pos)
    ang = 2.0 * np.pi * ((k[:, None] * k[None, :]) % n_pos) / n_pos
    cs = np.concatenate([np.cos(ang), -np.sin(ang)], axis=0)
    ch = _channel_tables(n_pos)
    const = lambda shape: pl.BlockSpec(shape, lambda b: (0,) * len(shape))
    return pl.pallas_call(
        _fourier_small_kernel,
        grid=(rows // n_pos,),
        in_specs=[pl.BlockSpec((2, n_pos, LANES), lambda b: (0, b, 0)),
                  const(cs.shape), const(ch.shape), const(w_fourier.shape)],
        out_specs=pl.BlockSpec((n_pos, gw), lambda b: (b, 0)),
        out_shape=jax.ShapeDtypeStruct((rows, gw), F32),
        compiler_params=_cparams("arbitrary"),
        name="fourier_ctx",
    )(a3, jnp.asarray(cs, BF16), jnp.asarray(ch, BF16), w_fourier)


ATTN_SUB = 128


def _attn_kernel(sink_ref, q_ref, qs_ref, k_ref, v_ref, kc_ref, vc_ref, o_ref, *, band, seq):
    qb = q_ref.shape[0]
    sub = ATTN_SUB
    lane = lax.broadcasted_iota(jnp.int32, (1, LANES), 1)
    lo_half = lane < HEAD_DIM
    zero = jnp.zeros((), BF16)
    scale = jnp.asarray(HEAD_DIM ** -0.5, BF16)
    kw = sub + 2 * WINDOW
    for sb in range(qb // sub):
        rows = slice(sb * sub, (sb + 1) * sub)
        qa0, qa1 = q_ref[rows, 0:LANES], q_ref[rows, LANES:]
        qs0, qs1 = qs_ref[rows, 0:LANES], qs_ref[rows, LANES:]
        q_all = jnp.concatenate([jnp.where(lo_half, qa0, zero), jnp.where(lo_half, qs0, zero),
                                 jnp.where(lo_half, zero, qs1), jnp.where(lo_half, zero, qa1)], axis=0) * scale
        if band:
            p0 = pl.program_id(1) * qb + sb * sub
            start = pl.multiple_of(jnp.clip(p0 - WINDOW, 0, seq - kw), WINDOW)
            qpos = p0 + lax.broadcasted_iota(jnp.int32, (sub, 1), 0)
            kpos = start + lax.broadcasted_iota(jnp.int32, (1, kw), 1)
            bias = jnp.where(jnp.abs(qpos - kpos) <= WINDOW, 0.0, NEG_INF)
            keys = jnp.concatenate([k_ref[pl.ds(start, kw), :], kc_ref[...]], axis=0)
            vals = jnp.concatenate([v_ref[pl.ds(start, kw), :], vc_ref[...]], axis=0)
        else:
            keys, vals = kc_ref[...], vc_ref[...]
        s_all = _dot_nt(q_all, keys)
        probs, dens = [], []
        for h in range(4):
            s = s_all[h * sub:(h + 1) * sub, :]
            sink = sink_ref[h]
            if band:
                s = jnp.concatenate([s[:, 0:kw] + bias, s[:, kw:]], axis=1)
            m = jnp.maximum(jnp.max(s, axis=1, keepdims=True), sink)
            p = jnp.exp(s - m)
            dens.append(jnp.sum(p, axis=1, keepdims=True) + jnp.exp(sink - m))
            probs.append(p.astype(BF16))
        o_all = _dot(jnp.concatenate(probs, axis=0), vals)
        o = [o_all[h * sub:(h + 1) * sub, :] / dens[h] for h in range(4)]
        o_ref[rows, 0:LANES] = jnp.where(lo_half, o[0], pltpu.roll(o[1], HEAD_DIM, axis=1)).astype(BF16)
        o_ref[rows, LANES:] = jnp.where(lo_half, pltpu.roll(o[2], HEAD_DIM, axis=1), o[3]).astype(BF16)


def _attention(sink, q, qs, k, v, kc, vc, *, seq, n_ctx, band):
    rows = q.shape[0]
    n_batch = rows // seq
    qb = Q_BLOCK if band else seq
    steps = seq // qb
    kern = functools.partial(_attn_kernel, band=band, seq=seq)
    seq_spec = pl.BlockSpec((seq, KV_W), lambda b, i: (b, 0))
    ctx_spec = pl.BlockSpec((n_ctx, KV_W), lambda b, i: (b, 0))
    q_spec = pl.BlockSpec((qb, GROUP_W), lambda b, i: (b * steps + i, 0))
    return pl.pallas_call(
        kern,
        grid=(n_batch, steps),
        in_specs=[pl.BlockSpec(memory_space=pltpu.SMEM), q_spec, q_spec,
                  seq_spec, seq_spec, ctx_spec, ctx_spec],
        out_specs=q_spec,
        out_shape=jax.ShapeDtypeStruct((rows, GROUP_W), BF16),
        compiler_params=_cparams("arbitrary", "arbitrary"),
        name="attn" if band else "attn_ctx",
    )(sink, q, qs, k, v, kc, vc)


POOL_HALO = max(POOL_WINDOWS) // 2


def _pool(p_ref, t0, tm, seq):
    halo = POOL_HALO
    pack = 2 * SUBLANES
    main = p_ref[pl.ds(t0, tm), :].astype(F32)
    lo = pl.multiple_of(jnp.maximum(t0 - pack, 0), pack)
    hi = pl.multiple_of(jnp.minimum(t0 + tm, seq - pack), pack)
    prev = p_ref[pl.ds(lo, pack), :].astype(F32)[pack - halo:, :]
    nxt = p_ref[pl.ds(hi, pack), :].astype(F32)[:halo, :]
    prev = jnp.where(t0 > 0, prev, 0.0)
    nxt = jnp.where(t0 + tm < seq, nxt, 0.0)
    full = jnp.concatenate([prev, main, nxt], axis=0)
    n = tm + 2 * halo
    lane = lax.broadcasted_iota(jnp.int32, (1, GROUP_W), 1)
    pos = t0 + lax.broadcasted_iota(jnp.int32, (tm, 1), 0)
    s = pltpu.roll(full, 1, axis=0) + full
    pooled = jnp.zeros((tm, GROUP_W), F32)
    for g, w in enumerate(POOL_WINDOWS):
        if g > 0:
            sh = w // 4
            s = pltpu.roll(s, sh, axis=0) + pltpu.roll(s, n - sh, axis=0)
        cnt = (jnp.minimum(pos + w // 2, seq) - jnp.maximum(pos - w // 2, 0)).astype(F32)
        mean = s[halo:halo + tm, :] / cnt
        pooled = jnp.where(lane // (GROUP_W // len(POOL_WINDOWS)) == g, mean, pooled)
    return pooled - main


def _route(logits):
    lane = lax.broadcasted_iota(jnp.int32, logits.shape, 1)
    big = jnp.int32(1 << 20)
    gl = jnp.where(lane < N_GROUPS, logits, -jnp.inf)
    gmax = jnp.max(gl, axis=1, keepdims=True)
    grp = jnp.min(jnp.where(gl == gmax, lane, big), axis=1, keepdims=True)
    gate_group = 1.0 / jnp.sum(jnp.exp(gl - gmax), axis=1, keepdims=True)
    lo = N_GROUPS + EXPERTS_PER_GROUP * grp
    el = jnp.where((lane >= lo) & (lane < lo + EXPERTS_PER_GROUP), logits, -jnp.inf)
    m1 = jnp.max(el, axis=1, keepdims=True)
    i1 = jnp.min(jnp.where(el == m1, lane, big), axis=1, keepdims=True)
    el2 = jnp.where(lane == i1, -jnp.inf, el)
    m2 = jnp.max(el2, axis=1, keepdims=True)
    i2 = jnp.min(jnp.where(el2 == m2, lane, big), axis=1, keepdims=True)
    r = jnp.exp(m2 - m1)
    g1 = gate_group / (1.0 + r)
    g2 = g1 * r
    e1 = (i1 - N_GROUPS).astype(F32)
    e2 = (i2 - N_GROUPS).astype(F32)
    return jnp.where(lane == 0, e1, jnp.where(lane == 1, e2, jnp.where(lane == 2, g1, jnp.where(lane == 3, g2, 0.0))))


def _merge_kernel(x_ref, p_ref, ug_ref, yf_ref, ya_ref, mod_ref, wpool_ref, pscale_ref, wsgu_ref, bsgu_ref,
                  wout_ref, lng_ref, lnb_ref, wrh_ref, wrl_ref, *rest,
                  mod_row, seq, n_alias):
    x1_ref, h2_ref, route_ref = rest[n_alias:]
    tm, d = x_ref.shape
    if mod_row is None:
        row = pl.program_id(0)
    else:
        row = mod_row
    t0 = pl.multiple_of(pl.program_id(1) * tm, tm)
    m = mod_ref[pl.ds(row, 1), :]
    gate1, shift2, scale2 = m[:, 2 * d:3 * d], m[:, 3 * d:4 * d], m[:, 4 * d:5 * d]

    pooled = _pool(p_ref, t0, tm, seq)
    y_pool = _dot(pooled.astype(BF16), wpool_ref[...]) * pscale_ref[...]

    ug = ug_ref[...].astype(F32)
    u = _gelu(ug[:, 0:GROUP_W])
    v = _layer_norm(_gelu(ug[:, GROUP_W:])).astype(BF16)
    lane = lax.broadcasted_iota(jnp.int32, (1, GROUP_W), 1)
    n_heads = wsgu_ref.shape[0] // SGU_CHUNK
    head = lane // (GROUP_W // n_heads)
    mixed = []
    for cidx in range(tm // SGU_CHUNK):
        vc = v[cidx * SGU_CHUNK:(cidx + 1) * SGU_CHUNK, :]
        full = _dot(wsgu_ref[...], vc)
        mc = bsgu_ref[...]
        for hd in range(n_heads):
            mc = mc + jnp.where(head == hd, full[hd * SGU_CHUNK:(hd + 1) * SGU_CHUNK, :], 0.0)
        mixed.append(mc)
    y_sgu = u * jnp.concatenate(mixed, axis=0)

    cat = jnp.concatenate([yf_ref[...].astype(BF16), ya_ref[...], y_pool.astype(BF16), y_sgu.astype(BF16)],
                          axis=1)
    y = _dot(cat, wout_ref[...])
    x1 = _layer_norm(RES_ALPHA * x_ref[...] + gate1 * y) * lng_ref[...] + lnb_ref[...]
    x1_ref[...] = x1
    h2 = _layer_norm(x1) * (1.0 + scale2) + shift2
    _store_rows(h2_ref, h2)
    hh = h2.astype(BF16)
    hl = (h2 - hh.astype(F32)).astype(BF16)
    logits = _dot(hh, wrh_ref[...]) + (_dot(hh, wrl_ref[...]) + _dot(hl, wrh_ref[...]))
    route_ref[...] = _route(logits)


def _merge(x2, p, ug, y_four, y_attn, mod, w_pool_bd, pool_scale, w_sgu_stack, b_sgu_exp, w_out,
           ln_g, ln_b, wr_hi, wr_lo, aliased, *, mod_row, seq, tm, row_off, total_rows):
    rows, d = x2.shape
    n_batch, steps = rows // seq, seq // tm
    off = row_off // tm
    kern = functools.partial(_merge_kernel, mod_row=mod_row, seq=seq, n_alias=len(aliased))
    row_spec = lambda w: pl.BlockSpec((tm, w), lambda b, i: (b * steps + i, 0))
    const = lambda a: pl.BlockSpec(a.shape, lambda b, i: (0,) * a.ndim)
    consts = (mod, w_pool_bd, pool_scale, w_sgu_stack, b_sgu_exp, w_out, ln_g, ln_b, wr_hi, wr_lo)
    n_in = 5 + len(consts)
    out_shapes = [jax.ShapeDtypeStruct((total_rows, d), F32),
                  jax.ShapeDtypeStruct((total_rows * ROW_WORDS, LANES), jnp.uint32),
                  jax.ShapeDtypeStruct((total_rows, LANES), F32)]
    out_specs = [pl.BlockSpec((tm, d), lambda b, i: (off + b * steps + i, 0)),
                 pl.BlockSpec((tm * ROW_WORDS, LANES), lambda b, i: (off + b * steps + i, 0)),
                 pl.BlockSpec((tm, LANES), lambda b, i: (off + b * steps + i, 0))]
    return pl.pallas_call(
        kern,
        grid=(n_batch, steps),
        in_specs=[row_spec(d), pl.BlockSpec((seq, GROUP_W), lambda b, i: (b, 0)),
                  row_spec(2 * GROUP_W), row_spec(GROUP_W), row_spec(GROUP_W)]
                 + [const(a) for a in consts]
                 + [pl.BlockSpec(memory_space=pl.ANY)] * len(aliased),
        out_specs=out_specs,
        out_shape=out_shapes,
        input_output_aliases={n_in + k: k for k in range(len(aliased))},
        compiler_params=_cparams("arbitrary", "arbitrary"),
        name="merge",
    )(x2, p, ug, y_four, y_attn, *consts, *aliased)


def _exact_row_sums(vals):
    hi = jnp.floor(vals * (1.0 / 256.0))
    lo = vals - 256.0 * hi
    ones = jnp.ones((SUBLANES, LANES), BF16)
    return 256.0 * _dot_nt(ones, hi.astype(BF16)) + _dot_nt(ones, lo.astype(BF16))


def _plan_kernel(route_ref, dest_ref, cnt_out_ref, cnt_ref, start_ref, carry_ref):
    ph, t = pl.program_id(0), pl.program_id(1)
    tm = route_ref.shape[0]
    lane = lax.broadcasted_iota(jnp.int32, (1, LANES), 1)
    rt = route_ref[...]
    e1 = rt[:, 0:1].astype(jnp.int32)
    e2 = rt[:, 1:2].astype(jnp.int32)
    hit1, hit2 = lane == e1, lane == e2
    onehot = jnp.where(hit1 | hit2, 1.0, 0.0)
    colsum = jnp.sum(onehot, axis=0, keepdims=True)

    @pl.when((ph == 0) & (t == 0))
    def _():
        cnt_ref[...] = jnp.zeros_like(cnt_ref)

    @pl.when(ph == 0)
    def _():
        cnt_ref[...] += colsum

    @pl.when((ph == 1) & (t == 0))
    def _():
        cnt = jnp.broadcast_to(cnt_ref[...], (SUBLANES, LANES))
        padded = jnp.floor((cnt + (MOE_BLOCK - 1.0)) * (1.0 / MOE_BLOCK)) * MOE_BLOCK
        lane8 = lax.broadcasted_iota(jnp.int32, (SUBLANES, LANES), 1)
        incl = padded
        sh = 1
        while sh < LANES:
            incl = incl + jnp.where(lane8 >= sh, pltpu.roll(incl, sh, axis=1), 0.0)
            sh *= 2
        start_ref[...] = (incl - padded)[0:1, :]
        carry_ref[...] = jnp.zeros_like(carry_ref)
        cnt_out_ref[...] = cnt

    @pl.when(ph == 1)
    def _():
        r_i = lax.broadcasted_iota(jnp.int32, (tm, tm), 0)
        c_i = lax.broadcasted_iota(jnp.int32, (tm, tm), 1)
        tri = jnp.where(c_i < r_i, 1.0, 0.0).astype(BF16)
        rank = _dot(tri, onehot.astype(BF16))
        base = start_ref[...] + carry_ref[...] + rank
        d1 = _exact_row_sums(jnp.where(hit1, base, 0.0))
        d2 = _exact_row_sums(jnp.where(hit2, base, 0.0))
        sub = lax.broadcasted_iota(jnp.int32, (SUBLANES, tm), 0)
        dest_ref[...] = jnp.where(sub == 0, d1, d2).astype(jnp.int32)
        carry_ref[...] += colsum


def _plan(route, tm):
    rows = route.shape[0]
    n_t = rows // tm
    return pl.pallas_call(
        _plan_kernel,
        grid=(2, n_t),
        in_specs=[pl.BlockSpec((tm, LANES), lambda ph, t: (t, 0))],
        out_specs=[pl.BlockSpec((None, SUBLANES, tm), lambda ph, t: (t * ph, 0, 0)),
                   pl.BlockSpec((SUBLANES, LANES), lambda ph, t: (0, 0))],
        out_shape=[jax.ShapeDtypeStruct((n_t, SUBLANES, tm), jnp.int32),
                   jax.ShapeDtypeStruct((SUBLANES, LANES), F32)],
        scratch_shapes=[pltpu.VMEM((1, LANES), F32)] * 3,
        compiler_params=_cparams("arbitrary", "arbitrary"),
        name="moe_plan",
    )(route)


def _row_copy(src_ref, src_row, dst_ref, dst_row, sem):
    return pltpu.make_async_copy(
        src_ref.at[pl.ds(pl.multiple_of(src_row * ROW_WORDS, ROW_WORDS), ROW_WORDS)],
        dst_ref.at[pl.ds(pl.multiple_of(dst_row * ROW_WORDS, ROW_WORDS), ROW_WORDS)], sem)


def _dispatch_kernel(dest_ref, h2_ref, xs_ref, sem, *, tm):
    def body(r4, c):
        for u in range(DMA_UNROLL):
            r = r4 * DMA_UNROLL + u
            for k in range(2):
                _row_copy(h2_ref, r, xs_ref, dest_ref[0, 0, k * tm + r], sem).start(priority=k)
        return c

    lax.fori_loop(0, tm // DMA_UNROLL, body, 0)
    for k in range(2):
        pltpu.make_async_copy(h2_ref, xs_ref.at[pl.ds(0, tm * ROW_WORDS)], sem).wait()


def _dispatch(dest, h2_tiles, n_slots, tm):
    n_t = dest.shape[0]
    return pl.pallas_call(
        functools.partial(_dispatch_kernel, tm=tm),
        grid=(n_t,),
        in_specs=[pl.BlockSpec((1, 1, 2 * tm), lambda i: (i, 0, 0), memory_space=pltpu.SMEM),
                  pl.BlockSpec((tm * ROW_WORDS, LANES), lambda i: (i, 0))],
        out_specs=pl.BlockSpec(memory_space=pl.ANY),
        out_shape=jax.ShapeDtypeStruct((n_slots * ROW_WORDS, LANES), jnp.uint32),
        scratch_shapes=[pltpu.SemaphoreType.DMA],
        compiler_params=_cparams("arbitrary"),
        name="moe_dispatch",
    )(dest, h2_tiles)


def _expert_kernel(be_ref, bn_ref, x_ref, wg_ref, wu_ref, wd_ref, y_ref):
    i = pl.program_id(0)
    mb = x_ref.shape[0] // ROW_WORDS
    half = wg_ref.shape[0] // 2

    @pl.when(bn_ref[i] > 0)
    def _():
        live = lax.broadcasted_iota(jnp.int32, (mb, 1), 0) < bn_ref[i]
        x_lo, x_hi = _load_rows(x_ref, 0, mb)
        x_lo = jnp.where(live, x_lo, 0.0).astype(BF16)
        x_hi = jnp.where(live, x_hi, 0.0).astype(BF16)
        g = _dot(x_lo, wg_ref[0:half, :].astype(BF16)) + _dot(x_hi, wg_ref[half:, :].astype(BF16))
        u = _dot(x_lo, wu_ref[0:half, :].astype(BF16)) + _dot(x_hi, wu_ref[half:, :].astype(BF16))
        hid = (_silu(g) * u).astype(BF16)
        _store_rows(y_ref, _dot(hid, wd_ref[...].astype(BF16)))


def _experts(block_e, block_n, xs, w_gate, w_up, w_down, layer):
    n_blocks = block_e.shape[0]
    _, _, d, de = w_gate.shape
    blk = pl.BlockSpec((MOE_BLOCK * ROW_WORDS, LANES), lambda i, be, bn: (i, 0))
    return pl.pallas_call(
        _expert_kernel,
        grid_spec=pltpu.PrefetchScalarGridSpec(
            num_scalar_prefetch=2,
            grid=(n_blocks,),
            in_specs=[blk,
                      pl.BlockSpec((None, None, d, de), lambda i, be, bn: (layer, be[i], 0, 0)),
                      pl.BlockSpec((None, None, d, de), lambda i, be, bn: (layer, be[i], 0, 0)),
                      pl.BlockSpec((None, None, de, d), lambda i, be, bn: (layer, be[i], 0, 0))],
            out_specs=blk),
        out_shape=jax.ShapeDtypeStruct(xs.shape, jnp.uint32),
        compiler_params=_cparams("arbitrary"),
        name="moe_experts",
    )(block_e, block_n, xs, w_gate, w_up, w_down)


def _combine_kernel(dest_ref, dest_next_ref, x1_ref, route_ref, mod_ref, lng_ref, lnb_ref, y_ref, o_ref,
                    buf_ref, sem, *, mod_row, rows_per_batch):
    tm, d = x1_ref.shape
    i = pl.program_id(0)
    n = pl.num_programs(0)
    slot = i % 2

    def gather(idx_ref, s):
        def body(r4, c):
            for u in range(DMA_UNROLL):
                r = r4 * DMA_UNROLL + u
                for k in range(2):
                    _row_copy(y_ref, idx_ref[0, 0, k * tm + r], buf_ref.at[s], k * tm + r,
                              sem.at[s]).start(priority=k)
            return c

        lax.fori_loop(0, tm // DMA_UNROLL, body, 0)

    @pl.when(i == 0)
    def _():
        gather(dest_ref, 0)

    @pl.when(i + 1 < n)
    def _():
        gather(dest_next_ref, 1 - slot)

    for k in range(2):
        pltpu.make_async_copy(y_ref.at[pl.ds(0, tm * ROW_WORDS)],
                              buf_ref.at[slot, pl.ds(0, tm * ROW_WORDS)], sem.at[slot]).wait()

    if mod_row is None:
        row = (i * tm) // rows_per_batch
    else:
        row = mod_row
    gate2 = mod_ref[pl.ds(row, 1), :][:, 5 * d:6 * d]
    rt = route_ref[...]
    f = jnp.zeros((tm, d), F32)
    for k in range(2):
        lo, hi = _load_rows(buf_ref.at[slot], k * tm, tm)
        f = f + jnp.concatenate([lo, hi], axis=1) * rt[:, 2 + k:3 + k]
    o_ref[...] = _layer_norm(RES_ALPHA * x1_ref[...] + gate2 * f) * lng_ref[...] + lnb_ref[...]


def _combine(dest, x1, route, mod, ln_g, ln_b, y_tiles, *, tm, row_off, rows, mod_row, rows_per_batch):
    d = x1.shape[1]
    off = row_off // tm
    steps = rows // tm
    kern = functools.partial(_combine_kernel, mod_row=mod_row, rows_per_batch=rows_per_batch)
    const = lambda a: pl.BlockSpec(a.shape, lambda i: (0,) * a.ndim)
    return pl.pallas_call(
        kern,
        grid=(steps,),
        in_specs=[pl.BlockSpec((1, 1, 2 * tm), lambda i: (off + i, 0, 0), memory_space=pltpu.SMEM),
                  pl.BlockSpec((1, 1, 2 * tm), lambda i: (off + jnp.minimum(i + 1, steps - 1), 0, 0),
                               memory_space=pltpu.SMEM),
                  pl.BlockSpec((tm, d), lambda i: (off + i, 0)),
                  pl.BlockSpec((tm, LANES), lambda i: (off + i, 0)),
                  const(mod), const(ln_g), const(ln_b),
                  pl.BlockSpec(memory_space=pl.ANY)],
        out_specs=pl.BlockSpec((tm, d), lambda i: (i, 0)),
        out_shape=jax.ShapeDtypeStruct((rows, d), F32),
        scratch_shapes=[pltpu.VMEM((2, 2 * tm * ROW_WORDS, LANES), jnp.uint32), pltpu.SemaphoreType.DMA((2,))],
        compiler_params=_cparams("arbitrary"),
        name="moe_combine",
    )(dest, dest, x1, route, mod, ln_g, ln_b, y_tiles)


def _rope_tables(n_pos):
    rows = n_pos // GRID_W
    row = jnp.repeat(jnp.arange(rows), GRID_W).astype(F32)
    col = jnp.tile(jnp.arange(GRID_W), rows).astype(F32)
    n_freq = HEAD_DIM // 4
    freq = ROPE_BASE ** (-jnp.arange(n_freq, dtype=F32) / n_freq)
    ang_r, ang_c = row[:, None] * freq, col[:, None] * freq
    cos_h = jnp.concatenate([jnp.cos(ang_r)] * 2 + [jnp.cos(ang_c)] * 2, axis=1)
    sin_h = jnp.concatenate([-jnp.sin(ang_r), jnp.sin(ang_r), -jnp.sin(ang_c), jnp.sin(ang_c)], axis=1)
    return jnp.tile(cos_h, (1, 2)), jnp.tile(sin_h, (1, 2))


def _block_table(counts, n_blocks):
    cnt = counts.astype(jnp.int32)
    padded = (cnt + MOE_BLOCK - 1) // MOE_BLOCK * MOE_BLOCK
    pad_end = jnp.cumsum(padded)
    pad_start = pad_end - padded
    blk_start = jnp.arange(n_blocks, dtype=jnp.int32)[:, None] * MOE_BLOCK
    be = jnp.minimum(jnp.sum((pad_end[None, :] <= blk_start).astype(jnp.int32), axis=1), N_EXPERTS - 1)
    mine = be[:, None] == jnp.arange(N_EXPERTS, dtype=jnp.int32)[None, :]
    fill = jnp.sum(jnp.where(mine, cnt[None, :] + pad_start[None, :], 0), axis=1) - blk_start[:, 0]
    return be, jnp.clip(fill, 0, MOE_BLOCK)


def _moe(route, h2_tiles, w_gate, w_up, w_down, layer):
    rows = route.shape[0]
    tm = ROW_TILE
    n_blocks = -(-(2 * rows) // MOE_BLOCK) + N_EXPERTS
    dest8, counts = _plan(route, tm)
    dest = dest8[:, 0:2, :].reshape(rows // tm, 1, 2 * tm)
    block_e, block_n = _block_table(counts[0, :N_EXPERTS], n_blocks)
    xs = _dispatch(dest, h2_tiles, n_blocks * MOE_BLOCK, tm)
    ys = _experts(block_e, block_n, xs, w_gate, w_up, w_down, layer)
    return dest, ys


def kernel(x, c, ctx, c_ctx, w_ada, b_ada, w_in, w_fourier, attn_sink, w_pool, pool_scale, w_sgu, b_sgu,
           w_out, ln1_g, ln1_b, w_router_group, w_router_expert, w_exp_gate, w_exp_up, w_exp_down,
           ln2_g, ln2_b):
    b, s, d = x.shape
    n_ctx = ctx.shape[1]
    n_layers = w_in.shape[0]
    tm = ROW_TILE
    cond = jnp.concatenate([c, c_ctx[None, :], jnp.zeros((SUBLANES - b - 1, d), F32)], axis=0)
    mod_all = _ada(cond, w_ada, b_ada[:, None, :])
    cos_t, sin_t = _rope_tables(s)
    x2 = x.reshape(b * s, d)
    c2 = ctx.reshape(b * n_ctx, d)
    n_sgu = w_sgu.shape[1]
    for layer in range(n_layers):
        last = layer == n_layers - 1
        mod = mod_all[layer]
        w_in_l = w_in[layer].astype(BF16)
        wf = w_fourier[layer].astype(BF16)
        w_pool_bd = jax.scipy.linalg.block_diag(*[w_pool[layer, g] for g in range(w_pool.shape[1])]).astype(BF16)
        w_sgu_stack = w_sgu[layer].reshape(n_sgu * SGU_CHUNK, SGU_CHUNK).astype(BF16)
        b_sgu_exp = jnp.repeat(b_sgu[layer].T, GROUP_W // n_sgu, axis=1)
        w_router = jnp.concatenate([w_router_group[layer], w_router_expert[layer].reshape(d, N_EXPERTS)], axis=1)
        w_router = jnp.pad(w_router, ((0, 0), (0, LANES - w_router.shape[1])))
        wr_hi = w_router.astype(BF16)
        wr_lo = (w_router - wr_hi.astype(F32)).astype(BF16)
        merge_consts = (mod, w_pool_bd, pool_scale[layer][None, :], w_sgu_stack, b_sgu_exp,
                        w_out[layer].astype(BF16), ln1_g[layer][None, :], ln1_b[layer][None, :], wr_hi, wr_lo)
        sink = attn_sink[layer]

        a, q, qs, k, v, p, ug = _proj(x2, mod, w_in_l, cos_t, sin_t, mod_row=None, rows_per_batch=s,
                                      rope=True, tm=tm, a_pitch=FFT_PITCH)
        ac, qc, qsc, kc, vc, pc, ugc = _proj(c2, mod, w_in_l, cos_t, sin_t, mod_row=b, rows_per_batch=n_ctx,
                                             rope=False, tm=n_ctx, a_pitch=FFT_R)
        y_four = _fourier(a, wf, s)
        y_attn = _attention(sink, q, qs, k, v, kc, vc, seq=s, n_ctx=n_ctx, band=True)
        total = b * s + (0 if last else b * n_ctx)
        merged = _merge(x2, p, ug, y_four, y_attn, *merge_consts, (),
                        mod_row=None, seq=s, tm=tm, row_off=0, total_rows=total)
        if not last:
            yc_four = _fourier_small(ac, wf, n_ctx)
            yc_attn = _attention(sink, qc, qsc, kc, vc, kc, vc, seq=n_ctx, n_ctx=n_ctx, band=False)
            merged = _merge(c2, pc, ugc, yc_four, yc_attn, *merge_consts, tuple(merged),
                            mod_row=b, seq=n_ctx, tm=n_ctx, row_off=b * s, total_rows=total)
        x1, h2_tiles, route = merged
        dest, ys = _moe(route, h2_tiles, w_exp_gate, w_exp_up, w_exp_down, layer)
        ln_g, ln_b = ln2_g[layer][None, :], ln2_b[layer][None, :]
        x2 = _combine(dest, x1, route, mod, ln_g, ln_b, ys, tm=tm, row_off=0, rows=b * s,
                      mod_row=None, rows_per_batch=s)
        if not last:
            c2 = _combine(dest, x1, route, mod, ln_g, ln_b, ys, tm=tm, row_off=b * s, rows=b * n_ctx,
                          mod_row=b, rows_per_batch=n_ctx)
    return x2.reshape(b, s, d)
```

```python
import functools
import math

import numpy as np
import jax
import jax.numpy as jnp
from jax import lax
from jax.experimental import pallas as pl
from jax.experimental.pallas import tpu as pltpu

GRID_W = 64
HEAD_DIM = 64
GROUP_W = 256
KV_W = 128
WINDOW = 128
POOL_WINDOWS = (2, 4, 8, 16)
SGU_CHUNK = 128
N_GROUPS = 4
EXPERTS_PER_GROUP = 8
N_EXPERTS = 32
ROPE_BASE = 10000.0
LN_EPS = 1e-6
NEG_INF = -1e30
DEPTH = 2
RES_ALPHA = (2 * DEPTH) ** 0.25

LANES = 128
SUBLANES = 8
VMEM_LIMIT = 48 * 1024 * 1024

ROW_TILE = 512
Q_BLOCK = 256
MOE_BLOCK = 256
FFT_R = 64
FFT_PITCH = 72
FFT_UNROLL = 4
DMA_UNROLL = 4

BF16 = jnp.bfloat16
F32 = jnp.float32


def _cparams(*sem):
    return pltpu.CompilerParams(dimension_semantics=sem, vmem_limit_bytes=VMEM_LIMIT)


def _dot(a, b):
    return jnp.dot(a, b, preferred_element_type=F32)


def _dot_nt(a, b):
    return lax.dot_general(a, b, (((1,), (1,)), ((), ())), preferred_element_type=F32)


def _layer_norm(t):
    mu = jnp.mean(t, axis=-1, keepdims=True)
    d = t - mu
    var = jnp.mean(d * d, axis=-1, keepdims=True)
    return d * lax.rsqrt(var + LN_EPS)


def _silu(t):
    return t * (1.0 / (1.0 + jnp.exp(-t)))


def _gelu(t):
    return 0.5 * t * (1.0 + lax.erf(t * (1.0 / math.sqrt(2.0))))


ROW_WORDS = 4
HI_MASK = 0xFFFF0000


def _pack_rows(t):
    half = t.shape[1] // 2
    lo = lax.bitcast_convert_type(t[:, :half].astype(BF16).astype(F32), jnp.uint32)
    hi = lax.bitcast_convert_type(t[:, half:].astype(BF16).astype(F32), jnp.uint32)
    return (lo >> 16) | (hi & jnp.uint32(HI_MASK))


def _unpack_rows(w):
    return (lax.bitcast_convert_type(w << 16, F32),
            lax.bitcast_convert_type(w & jnp.uint32(HI_MASK), F32))


def _store_rows(ref, t):
    w = _pack_rows(t)
    for j in range(ROW_WORDS):
        ref[pl.ds(j, t.shape[0], stride=ROW_WORDS), :] = w[:, j * LANES:(j + 1) * LANES]


def _load_rows(ref, first, m):
    w = jnp.concatenate([ref[pl.ds(first * ROW_WORDS + j, m, stride=ROW_WORDS), :] for j in range(ROW_WORDS)],
                        axis=1)
    return _unpack_rows(w)


def _ada_kernel(c_ref, w_ref, b_ref, o_ref):
    s = _silu(c_ref[...]).astype(BF16)
    o_ref[...] = _dot(s, w_ref[...].astype(BF16)) + b_ref[...]


def _ada(cond, w_ada, b_ada):
    n_layers, d, n = w_ada.shape
    tn = n // 4
    return pl.pallas_call(
        _ada_kernel,
        grid=(n_layers, n // tn),
        in_specs=[
            pl.BlockSpec((SUBLANES, d), lambda l, j: (0, 0)),
            pl.BlockSpec((None, d, tn), lambda l, j: (l, 0, j)),
            pl.BlockSpec((None, 1, tn), lambda l, j: (l, 0, j)),
        ],
        out_specs=pl.BlockSpec((None, SUBLANES, tn), lambda l, j: (l, 0, j)),
        out_shape=jax.ShapeDtypeStruct((n_layers, SUBLANES, n), F32),
        compiler_params=_cparams("arbitrary", "arbitrary"),
        name="ada",
    )(cond, w_ada, b_ada)


def _rope(t, cos_t, sin_t):
    lane = lax.broadcasted_iota(jnp.int32, t.shape, 1)
    first = (lane % 32) < 16
    partner = jnp.where(first, pltpu.roll(t, LANES - 16, axis=1), pltpu.roll(t, 16, axis=1))
    return t * cos_t + partner * sin_t


def _proj_kernel(x_ref, mod_ref, w_ref, cos_ref, sin_ref,
                 a_ref, q_ref, qs_ref, k_ref, v_ref, p_ref, ug_ref,
                 *, mod_row, rows_per_batch, rope, sh_col, sc_col, a_pitch):
    d = x_ref.shape[1]
    tm = x_ref.shape[0]
    if mod_row is None:
        row = (pl.program_id(0) * tm) // rows_per_batch
    else:
        row = mod_row
    m = mod_ref[pl.ds(row, 1), :]
    shift = m[:, sh_col * d:(sh_col + 1) * d]
    scale = m[:, sc_col * d:(sc_col + 1) * d]
    h = _layer_norm(x_ref[...]) * (1.0 + scale) + shift
    z = _dot(h.astype(BF16), w_ref[...])
    pad = jnp.zeros((a_pitch - FFT_R, LANES), F32)
    for g in range(tm // FFT_R):
        for hf in range(2):
            grp = z[g * FFT_R:(g + 1) * FFT_R, hf * LANES:(hf + 1) * LANES]
            if a_pitch > FFT_R:
                grp = jnp.concatenate([grp, pad], axis=0)
            a_ref[hf, g * a_pitch:(g + 1) * a_pitch, :] = grp
    q0, q1 = z[:, 256:384], z[:, 384:512]
    k = z[:, 512:640]
    if rope:
        cos_t, sin_t = cos_ref[...], sin_ref[...]
        q0, q1, k = _rope(q0, cos_t, sin_t), _rope(q1, cos_t, sin_t), _rope(k, cos_t, sin_t)
    q_ref[:, 0:128] = q0.astype(BF16)
    q_ref[:, 128:256] = q1.astype(BF16)
    qs_ref[:, 0:128] = pltpu.roll(q0, HEAD_DIM, axis=1).astype(BF16)
    qs_ref[:, 128:256] = pltpu.roll(q1, HEAD_DIM, axis=1).astype(BF16)
    k_ref[...] = k.astype(BF16)
    v_ref[...] = z[:, 640:768].astype(BF16)
    p_ref[...] = z[:, 768:1024].astype(BF16)
    ug_ref[...] = z[:, 1024:1536].astype(BF16)


def _proj(x2, mod, w_in, cos_t, sin_t, *, mod_row, rows_per_batch, rope, tm, a_pitch):
    rows, d = x2.shape
    n_in = w_in.shape[1]
    steps_per_seq = cos_t.shape[0] // tm
    kern = functools.partial(_proj_kernel, mod_row=mod_row, rows_per_batch=rows_per_batch,
                             rope=rope, sh_col=0, sc_col=1, a_pitch=a_pitch)
    row_spec = lambda w: pl.BlockSpec((tm, w), lambda i: (i, 0))
    out_w = (256, 256, 128, 128, 256, 512)
    out_dt = (BF16,) * len(out_w)
    ta = tm // FFT_R * a_pitch
    a_spec = pl.BlockSpec((2, ta, LANES), lambda i: (0, i, 0))
    a_shape = jax.ShapeDtypeStruct((2, rows // FFT_R * a_pitch, LANES), F32)
    return pl.pallas_call(
        kern,
        grid=(rows // tm,),
        in_specs=[
            row_spec(d),
            pl.BlockSpec(mod.shape, lambda i: (0, 0)),
            pl.BlockSpec((d, n_in), lambda i: (0, 0)),
            pl.BlockSpec((tm, LANES), lambda i: (i % steps_per_seq, 0)),
            pl.BlockSpec((tm, LANES), lambda i: (i % steps_per_seq, 0)),
        ],
        out_specs=[a_spec] + [row_spec(w) for w in out_w],
        out_shape=[a_shape] + [jax.ShapeDtypeStruct((rows, w), dt) for w, dt in zip(out_w, out_dt)],
        compiler_params=_cparams("arbitrary"),
        name="proj",
    )(x2, mod, w_in, cos_t, sin_t)


def _fft_tables(n_pos):
    r = FFT_R
    assert n_pos == r * r
    kb = np.arange(r)[None, :, None]
    na = np.arange(r)[:, None, None]
    nb = np.arange(r)[None, None, :]
    ang = 2.0 * np.pi * ((kb * (na + r * nb)) % n_pos) / n_pos
    m1 = np.concatenate([np.cos(ang), -np.sin(ang)], axis=1)
    ka = np.arange(r)[:, None]
    n2 = np.arange(r)[None, :]
    ang2 = 2.0 * np.pi * ((ka * n2) % r) / r
    c2, s2 = np.cos(ang2), np.sin(ang2)
    w2 = np.block([[c2, s2], [-s2, c2]])
    return m1, w2


def _channel_tables(n_pos):
    h = HEAD_DIM
    c = np.arange(h)
    ang = 2.0 * np.pi * ((c[:, None] * c[None, :]) % h) / h
    scale = 1.0 / math.sqrt(n_pos * h)
    eye = np.eye(GROUP_W // h)
    cc = np.kron(eye, np.cos(ang)) * scale
    ss = np.kron(eye, np.sin(ang)) * scale
    return np.concatenate([cc, ss], axis=0)


def _fourier_kernel(a_ref, m1_ref, w2_ref, ch_ref, wf_ref, o_ref, z_ref, y_ref):
    r, pt = FFT_R, FFT_PITCH

    def step1(i, c):
        for u in range(FFT_UNROLL):
            na = i * FFT_UNROLL + u
            rows = jnp.concatenate([a_ref[0, pl.ds(na, r, stride=pt), :],
                                    a_ref[1, pl.ds(na, r, stride=pt), :]], axis=1)
            z = _dot(m1_ref[na], rows.astype(BF16))
            base = pl.multiple_of(na * pt, SUBLANES)
            z_ref[0, pl.ds(base, r), :] = z[0:r, 0:LANES]
            z_ref[1, pl.ds(base, r), :] = z[0:r, LANES:]
            z_ref[2, pl.ds(base, r), :] = z[r:, 0:LANES]
            z_ref[3, pl.ds(base, r), :] = z[r:, LANES:]
        return c

    lax.fori_loop(0, r // FFT_UNROLL, step1, 0)

    def step2(i, c):
        for u in range(FFT_UNROLL):
            kb = i * FFT_UNROLL + u
            q = [z_ref[j, pl.ds(kb, r, stride=pt), :] for j in range(4)]
            zs = jnp.concatenate([jnp.concatenate(q[0:2], axis=1),
                                  jnp.concatenate(q[2:4], axis=1)], axis=0)
            y = _dot(w2_ref[...], zs.astype(BF16))
            base = pl.multiple_of(kb * r, r)
            y_ref[0, pl.ds(base, r), :] = y[0:r, 0:LANES]
            y_ref[1, pl.ds(base, r), :] = y[0:r, LANES:]
            y_ref[2, pl.ds(base, r), :] = y[r:, 0:LANES]
            y_ref[3, pl.ds(base, r), :] = y[r:, LANES:]
        return c

    lax.fori_loop(0, r // FFT_UNROLL, step2, 0)

    chunk = 8 * r
    for cidx in range(r * r // chunk):
        yy = jnp.concatenate([y_ref[j, cidx * chunk:(cidx + 1) * chunk, :] for j in range(4)], axis=1)
        f = _dot(yy.astype(BF16), ch_ref[...])
        g = _dot(f.astype(BF16), wf_ref[...])
        for gi in range(chunk // r):
            kb = cidx * (chunk // r) + gi
            z_ref[0, kb * pt:kb * pt + r, :] = g[gi * r:(gi + 1) * r, 0:LANES]
            z_ref[1, kb * pt:kb * pt + r, :] = g[gi * r:(gi + 1) * r, LANES:]

    def step3(i, c):
        for u in range(FFT_UNROLL):
            ka = i * FFT_UNROLL + u
            base = pl.multiple_of(ka * r, r)
            o_ref[pl.ds(base, r), 0:LANES] = z_ref[0, pl.ds(ka, r, stride=pt), :]
            o_ref[pl.ds(base, r), LANES:] = z_ref[1, pl.ds(ka, r, stride=pt), :]
        return c

    lax.fori_loop(0, r // FFT_UNROLL, step3, 0)


def _fourier(a3, w_fourier, n_pos):
    rows = a3.shape[1] // FFT_PITCH * FFT_R
    gw = GROUP_W
    m1, w2 = _fft_tables(n_pos)
    ch = _channel_tables(n_pos)
    const = lambda shape: pl.BlockSpec(shape, lambda b: (0,) * len(shape))
    return pl.pallas_call(
        _fourier_kernel,
        grid=(rows // n_pos,),
        in_specs=[
            pl.BlockSpec((2, FFT_R * FFT_PITCH, LANES), lambda b: (0, b, 0)),
            const(m1.shape), const(w2.shape), const(ch.shape), const(w_fourier.shape),
        ],
        out_specs=pl.BlockSpec((n_pos, gw), lambda b: (b, 0)),
        out_shape=jax.ShapeDtypeStruct((rows, gw), F32),
        scratch_shapes=[pltpu.VMEM((4, FFT_R * FFT_PITCH, LANES), F32), pltpu.VMEM((4, n_pos, LANES), F32)],
        compiler_params=_cparams("arbitrary"),
        name="fourier",
    )(a3, jnp.asarray(m1, BF16), jnp.asarray(w2, BF16), jnp.asarray(ch, BF16), w_fourier)


def _fourier_small_kernel(a_ref, cs_ref, ch_ref, wf_ref, o_ref):
    n = a_ref.shape[1]
    a = jnp.concatenate([a_ref[0], a_ref[1]], axis=1)
    pq = _dot(cs_ref[...], a.astype(BF16))
    y = jnp.concatenate([pq[0:n], pq[n:2 * n]], axis=1).astype(BF16)
    f = _dot(y, ch_ref[...])
    o_ref[...] = _dot(f.astype(BF16), wf_ref[...])


def _fourier_small(a3, w_fourier, n_pos):
    _, rows, _ = a3.shape
    gw = GROUP_W
    k = np.arange(n_pos)
    ang = 2.0 * np.pi * ((k[:, None] * k[None, :]) % n_pos) / n_pos
    cs = np.concatenate([np.cos(ang), -np.sin(ang)], axis=0)
    ch = _channel_tables(n_pos)
    const = lambda shape: pl.BlockSpec(shape, lambda b: (0,) * len(shape))
    return pl.pallas_call(
        _fourier_small_kernel,
        grid=(rows // n_pos,),
        in_specs=[pl.BlockSpec((2, n_pos, LANES), lambda b: (0, b, 0)),
                  const(cs.shape), const(ch.shape), const(w_fourier.shape)],
        out_specs=pl.BlockSpec((n_pos, gw), lambda b: (b, 0)),
        out_shape=jax.ShapeDtypeStruct((rows, gw), F32),
        compiler_params=_cparams("arbitrary"),
        name="fourier_ctx",
    )(a3, jnp.asarray(cs, BF16), jnp.asarray(ch, BF16), w_fourier)


ATTN_SUB = 128


def _attn_kernel(sink_ref, q_ref, qs_ref, k_ref, v_ref, kc_ref, vc_ref, o_ref, *, band, seq):
    qb = q_ref.shape[0]
    sub = ATTN_SUB
    lane = lax.broadcasted_iota(jnp.int32, (1, LANES), 1)
    lo_half = lane < HEAD_DIM
    zero = jnp.zeros((), BF16)
    scale = jnp.asarray(HEAD_DIM ** -0.5, BF16)
    kw = sub + 2 * WINDOW
    for sb in range(qb // sub):
        rows = slice(sb * sub, (sb + 1) * sub)
        qa0, qa1 = q_ref[rows, 0:LANES], q_ref[rows, LANES:]
        qs0, qs1 = qs_ref[rows, 0:LANES], qs_ref[rows, LANES:]
        q_all = jnp.concatenate([jnp.where(lo_half, qa0, zero), jnp.where(lo_half, qs0, zero),
                                 jnp.where(lo_half, zero, qs1), jnp.where(lo_half, zero, qa1)], axis=0) * scale
        if band:
            p0 = pl.program_id(1) * qb + sb * sub
            start = pl.multiple_of(jnp.clip(p0 - WINDOW, 0, seq - kw), WINDOW)
            qpos = p0 + lax.broadcasted_iota(jnp.int32, (sub, 1), 0)
            kpos = start + lax.broadcasted_iota(jnp.int32, (1, kw), 1)
            bias = jnp.where(jnp.abs(qpos - kpos) <= WINDOW, 0.0, NEG_INF)
            keys = jnp.concatenate([k_ref[pl.ds(start, kw), :], kc_ref[...]], axis=0)
            vals = jnp.concatenate([v_ref[pl.ds(start, kw), :], vc_ref[...]], axis=0)
        else:
            keys, vals = kc_ref[...], vc_ref[...]
        s_all = _dot_nt(q_all, keys)
        probs, dens = [], []
        for h in range(4):
            s = s_all[h * sub:(h + 1) * sub, :]
            sink = sink_ref[h]
            if band:
                s = jnp.concatenate([s[:, 0:kw] + bias, s[:, kw:]], axis=1)
            m = jnp.maximum(jnp.max(s, axis=1, keepdims=True), sink)
            p = jnp.exp(s - m)
            dens.append(jnp.sum(p, axis=1, keepdims=True) + jnp.exp(sink - m))
            probs.append(p.astype(BF16))
        o_all = _dot(jnp.concatenate(probs, axis=0), vals)
        o = [o_all[h * sub:(h + 1) * sub, :] / dens[h] for h in range(4)]
        o_ref[rows, 0:LANES] = jnp.where(lo_half, o[0], pltpu.roll(o[1], HEAD_DIM, axis=1)).astype(BF16)
        o_ref[rows, LANES:] = jnp.where(lo_half, pltpu.roll(o[2], HEAD_DIM, axis=1), o[3]).astype(BF16)


def _attention(sink, q, qs, k, v, kc, vc, *, seq, n_ctx, band):
    rows = q.shape[0]
    n_batch = rows // seq
    qb = Q_BLOCK if band else seq
    steps = seq // qb
    kern = functools.partial(_attn_kernel, band=band, seq=seq)
    seq_spec = pl.BlockSpec((seq, KV_W), lambda b, i: (b, 0))
    ctx_spec = pl.BlockSpec((n_ctx, KV_W), lambda b, i: (b, 0))
    q_spec = pl.BlockSpec((qb, GROUP_W), lambda b, i: (b * steps + i, 0))
    return pl.pallas_call(
        kern,
        grid=(n_batch, steps),
        in_specs=[pl.BlockSpec(memory_space=pltpu.SMEM), q_spec, q_spec,
                  seq_spec, seq_spec, ctx_spec, ctx_spec],
        out_specs=q_spec,
        out_shape=jax.ShapeDtypeStruct((rows, GROUP_W), BF16),
        compiler_params=_cparams("arbitrary", "arbitrary"),
        name="attn" if band else "attn_ctx",
    )(sink, q, qs, k, v, kc, vc)


POOL_HALO = max(POOL_WINDOWS) // 2


def _pool(p_ref, t0, tm, seq):
    halo = POOL_HALO
    pack = 2 * SUBLANES
    main = p_ref[pl.ds(t0, tm), :].astype(F32)
    lo = pl.multiple_of(jnp.maximum(t0 - pack, 0), pack)
    hi = pl.multiple_of(jnp.minimum(t0 + tm, seq - pack), pack)
    prev = p_ref[pl.ds(lo, pack), :].astype(F32)[pack - halo:, :]
    nxt = p_ref[pl.ds(hi, pack), :].astype(F32)[:halo, :]
    prev = jnp.where(t0 > 0, prev, 0.0)
    nxt = jnp.where(t0 + tm < seq, nxt, 0.0)
    full = jnp.concatenate([prev, main, nxt], axis=0)
    n = tm + 2 * halo
    gch = GROUP_W // len(POOL_WINDOWS)
    first = lax.broadcasted_iota(jnp.int32, (1, LANES), 1) < gch
    means = []
    for hf in range(GROUP_W // LANES):
        wa, wb = POOL_WINDOWS[2 * hf], POOL_WINDOWS[2 * hf + 1]
        x = full[:, hf * LANES:(hf + 1) * LANES]
        sums, w, s = {}, 2, pltpu.roll(x, 1, axis=0) + x
        sums[w] = s
        while w < wb:
            s = pltpu.roll(s, w // 2, axis=0) + pltpu.roll(s, n - w // 2, axis=0)
            w *= 2
            sums[w] = s
        means.append(jnp.where(first, sums[wa] * (1.0 / wa), sums[wb] * (1.0 / wb))[halo:halo + tm, :])
    mean = jnp.concatenate(means, axis=1)
    win = jnp.concatenate([jnp.full((1, gch), w, jnp.int32) for w in POOL_WINDOWS], axis=1)

    def rescale(rows, first_pos):
        pos = first_pos + lax.broadcasted_iota(jnp.int32, (halo, 1), 0)
        cnt = jnp.minimum(pos + win // 2, seq) - jnp.maximum(pos - win // 2, 0)
        return rows * (win.astype(F32) / cnt.astype(F32))

    mean = jnp.concatenate([rescale(mean[:halo], t0), mean[halo:tm - halo],
                            rescale(mean[tm - halo:], t0 + tm - halo)], axis=0)
    return mean - main


def _route(logits):
    tm = logits.shape[0]
    lt = logits.T
    gl = lt[N_EXPERTS:N_EXPERTS + N_GROUPS]
    sub_g = lax.broadcasted_iota(jnp.int32, gl.shape, 0)
    gmax = jnp.max(gl, axis=0, keepdims=True)
    grp = jnp.min(jnp.where(gl == gmax, sub_g, N_GROUPS), axis=0, keepdims=True)
    gate_group = 1.0 / jnp.sum(jnp.exp(gl - gmax), axis=0, keepdims=True)
    el = lt[0:EXPERTS_PER_GROUP]
    for g in range(1, N_GROUPS):
        el = jnp.where(grp == g, lt[g * EXPERTS_PER_GROUP:(g + 1) * EXPERTS_PER_GROUP], el)
    sub = lax.broadcasted_iota(jnp.int32, el.shape, 0)
    m1 = jnp.max(el, axis=0, keepdims=True)
    i1 = jnp.min(jnp.where(el == m1, sub, EXPERTS_PER_GROUP), axis=0, keepdims=True)
    el2 = jnp.where(sub == i1, -jnp.inf, el)
    m2 = jnp.max(el2, axis=0, keepdims=True)
    i2 = jnp.min(jnp.where(el2 == m2, sub, EXPERTS_PER_GROUP), axis=0, keepdims=True)
    r = jnp.exp(m2 - m1)
    g1 = gate_group / (1.0 + r)
    g2 = g1 * r
    e1 = (grp * EXPERTS_PER_GROUP + i1).astype(F32)
    e2 = (grp * EXPERTS_PER_GROUP + i2).astype(F32)
    rows = jnp.where(sub == 0, e1, jnp.where(sub == 1, e2, jnp.where(sub == 2, g1, jnp.where(sub == 3, g2, 0.0))))
    cols = jnp.concatenate([rows, jnp.zeros((LANES - rows.shape[0], tm), F32)], axis=0).T
    return cols, rows


def _merge_kernel(x_ref, p_ref, ug_ref, yf_ref, ya_ref, mod_ref, wpool_ref, pscale_ref, wsgu_ref, bsgu_ref,
                  wout_ref, lng_ref, lnb_ref, wrh_ref, wrl_ref, *rest,
                  mod_row, seq, n_alias):
    x1_ref, h2_ref, route_ref, route_t_ref = rest[n_alias:]
    tm, d = x_ref.shape
    if mod_row is None:
        row = pl.program_id(0)
    else:
        row = mod_row
    t0 = pl.multiple_of(pl.program_id(1) * tm, tm)
    m = mod_ref[pl.ds(row, 1), :]
    gate1, shift2, scale2 = m[:, 2 * d:3 * d], m[:, 3 * d:4 * d], m[:, 4 * d:5 * d]

    pooled = _pool(p_ref, t0, tm, seq)
    y_pool = _dot(pooled.astype(BF16), wpool_ref[...]) * pscale_ref[...]

    ug = ug_ref[...].astype(F32)
    u = _gelu(ug[:, 0:GROUP_W])
    v = _layer_norm(_gelu(ug[:, GROUP_W:])).astype(BF16)
    lane = lax.broadcasted_iota(jnp.int32, (1, GROUP_W), 1)
    n_heads = wsgu_ref.shape[0] // SGU_CHUNK
    head = lane // (GROUP_W // n_heads)
    mixed = []
    for cidx in range(tm // SGU_CHUNK):
        vc = v[cidx * SGU_CHUNK:(cidx + 1) * SGU_CHUNK, :]
        full = _dot(wsgu_ref[...], vc)
        mc = bsgu_ref[...]
        for hd in range(n_heads):
            mc = mc + jnp.where(head == hd, full[hd * SGU_CHUNK:(hd + 1) * SGU_CHUNK, :], 0.0)
        mixed.append(mc)
    y_sgu = u * jnp.concatenate(mixed, axis=0)

    cat = jnp.concatenate([yf_ref[...].astype(BF16), ya_ref[...], y_pool.astype(BF16), y_sgu.astype(BF16)],
                          axis=1)
    y = _dot(cat, wout_ref[...])
    x1 = _layer_norm(RES_ALPHA * x_ref[...] + gate1 * y) * lng_ref[...] + lnb_ref[...]
    x1_ref[...] = x1
    h2 = _layer_norm(x1) * (1.0 + scale2) + shift2
    _store_rows(h2_ref, h2)
    hh = h2.astype(BF16)
    hl = (h2 - hh.astype(F32)).astype(BF16)
    logits = _dot(hh, wrh_ref[...]) + (_dot(hh, wrl_ref[...]) + _dot(hl, wrh_ref[...]))
    route_ref[...], route_t_ref[...] = _route(logits)


def _merge(x2, p, ug, y_four, y_attn, mod, w_pool_bd, pool_scale, w_sgu_stack, b_sgu_exp, w_out,
           ln_g, ln_b, wr_hi, wr_lo, aliased, *, mod_row, seq, tm, row_off, total_rows):
    rows, d = x2.shape
    n_batch, steps = rows // seq, seq // tm
    off = row_off // tm
    kern = functools.partial(_merge_kernel, mod_row=mod_row, seq=seq, n_alias=len(aliased))
    row_spec = lambda w: pl.BlockSpec((tm, w), lambda b, i: (b * steps + i, 0))
    const = lambda a: pl.BlockSpec(a.shape, lambda b, i: (0,) * a.ndim)
    consts = (mod, w_pool_bd, pool_scale, w_sgu_stack, b_sgu_exp, w_out, ln_g, ln_b, wr_hi, wr_lo)
    n_in = 5 + len(consts)
    out_shapes = [jax.ShapeDtypeStruct((total_rows, d), F32),
                  jax.ShapeDtypeStruct((total_rows * ROW_WORDS, LANES), jnp.uint32),
                  jax.ShapeDtypeStruct((total_rows, LANES), F32),
                  jax.ShapeDtypeStruct((SUBLANES, total_rows), F32)]
    out_specs = [pl.BlockSpec((tm, d), lambda b, i: (off + b * steps + i, 0)),
                 pl.BlockSpec((tm * ROW_WORDS, LANES), lambda b, i: (off + b * steps + i, 0)),
                 pl.BlockSpec((tm, LANES), lambda b, i: (off + b * steps + i, 0)),
                 pl.BlockSpec((SUBLANES, tm), lambda b, i: (0, off + b * steps + i))]
    return pl.pallas_call(
        kern,
        grid=(n_batch, steps),
        in_specs=[row_spec(d), pl.BlockSpec((seq, GROUP_W), lambda b, i: (b, 0)),
                  row_spec(2 * GROUP_W), row_spec(GROUP_W), row_spec(GROUP_W)]
                 + [const(a) for a in consts]
                 + [pl.BlockSpec(memory_space=pl.ANY)] * len(aliased),
        out_specs=out_specs,
        out_shape=out_shapes,
        input_output_aliases={n_in + k: k for k in range(len(aliased))},
        compiler_params=_cparams("arbitrary", "arbitrary"),
        name="merge",
    )(x2, p, ug, y_four, y_attn, *consts, *aliased)


def _plan_kernel(route_ref, dest_ref, cnt_out_ref, cnt_ref, start_ref, carry_ref):
    ph, t = pl.program_id(0), pl.program_id(1)
    tm = route_ref.shape[1]
    rt = route_ref[...]
    e1 = rt[0:1, :].astype(jnp.int32)
    e2 = rt[1:2, :].astype(jnp.int32)
    sub = lax.broadcasted_iota(jnp.int32, (N_EXPERTS, tm), 0)
    hit1, hit2 = sub == e1, sub == e2
    onehot = jnp.where(hit1 | hit2, 1.0, 0.0)
    tile_cnt = jnp.sum(onehot, axis=1, keepdims=True)

    @pl.when((ph == 0) & (t == 0))
    def _():
        cnt_ref[...] = jnp.zeros_like(cnt_ref)

    @pl.when(ph == 0)
    def _():
        cnt_ref[...] += tile_cnt

    @pl.when((ph == 1) & (t == 0))
    def _():
        cnt = cnt_ref[...]
        padded = jnp.floor((cnt + (MOE_BLOCK - 1.0)) * (1.0 / MOE_BLOCK)) * MOE_BLOCK
        row = lax.broadcasted_iota(jnp.int32, cnt.shape, 0)
        incl = padded
        sh = 1
        while sh < N_EXPERTS:
            incl = incl + jnp.where(row >= sh, pltpu.roll(incl, sh, axis=0), 0.0)
            sh *= 2
        start_ref[...] = incl - padded
        carry_ref[...] = jnp.zeros_like(carry_ref)
        cnt_out_ref[...] = cnt

    @pl.when(ph == 1)
    def _():
        r_i = lax.broadcasted_iota(jnp.int32, (tm, tm), 0)
        c_i = lax.broadcasted_iota(jnp.int32, (tm, tm), 1)
        before = jnp.where(r_i < c_i, 1.0, 0.0).astype(BF16)
        rank = _dot(onehot.astype(BF16), before)
        base = start_ref[:, 0:1] + carry_ref[:, 0:1] + rank
        d1 = jnp.sum(jnp.where(hit1, base, 0.0), axis=0, keepdims=True)
        d2 = jnp.sum(jnp.where(hit2, base, 0.0), axis=0, keepdims=True)
        sub8 = lax.broadcasted_iota(jnp.int32, (SUBLANES, tm), 0)
        dest_ref[...] = jnp.where(sub8 == 0, d1, d2).astype(jnp.int32)
        carry_ref[...] += tile_cnt


def _plan(route_t, tm):
    rows = route_t.shape[1]
    n_t = rows // tm
    return pl.pallas_call(
        _plan_kernel,
        grid=(2, n_t),
        in_specs=[pl.BlockSpec((SUBLANES, tm), lambda ph, t: (0, t))],
        out_specs=[pl.BlockSpec((None, SUBLANES, tm), lambda ph, t: (t * ph, 0, 0)),
                   pl.BlockSpec((N_EXPERTS, LANES), lambda ph, t: (0, 0))],
        out_shape=[jax.ShapeDtypeStruct((n_t, SUBLANES, tm), jnp.int32),
                   jax.ShapeDtypeStruct((N_EXPERTS, LANES), F32)],
        scratch_shapes=[pltpu.VMEM((N_EXPERTS, LANES), F32)] * 3,
        compiler_params=_cparams("arbitrary", "arbitrary"),
        name="moe_plan",
    )(route_t)


def _row_copy(src_ref, src_row, dst_ref, dst_row, sem):
    return pltpu.make_async_copy(
        src_ref.at[pl.ds(pl.multiple_of(src_row * ROW_WORDS, ROW_WORDS), ROW_WORDS)],
        dst_ref.at[pl.ds(pl.multiple_of(dst_row * ROW_WORDS, ROW_WORDS), ROW_WORDS)], sem)


def _dispatch_kernel(dest_ref, h2_ref, xs_ref, sem, *, tm):
    def body(r4, c):
        for u in range(DMA_UNROLL):
            r = r4 * DMA_UNROLL + u
            for k in range(2):
                _row_copy(h2_ref, r, xs_ref, dest_ref[0, 0, k * tm + r], sem).start(priority=k)
        return c

    lax.fori_loop(0, tm // DMA_UNROLL, body, 0)
    for k in range(2):
        pltpu.make_async_copy(h2_ref, xs_ref.at[pl.ds(0, tm * ROW_WORDS)], sem).wait()


def _dispatch(dest, h2_tiles, n_slots, tm):
    n_t = dest.shape[0]
    return pl.pallas_call(
        functools.partial(_dispatch_kernel, tm=tm),
        grid=(n_t,),
        in_specs=[pl.BlockSpec((1, 1, 2 * tm), lambda i: (i, 0, 0), memory_space=pltpu.SMEM),
                  pl.BlockSpec((tm * ROW_WORDS, LANES), lambda i: (i, 0))],
        out_specs=pl.BlockSpec(memory_space=pl.ANY),
        out_shape=jax.ShapeDtypeStruct((n_slots * ROW_WORDS, LANES), jnp.uint32),
        scratch_shapes=[pltpu.SemaphoreType.DMA],
        compiler_params=_cparams("arbitrary"),
        name="moe_dispatch",
    )(dest, h2_tiles)


def _expert_kernel(be_ref, bn_ref, first_ref, slot_ref, nxt_ref, x_ref, wg_hbm, wu_hbm, wd_hbm, y_ref,
                   wg_buf, wu_buf, wd_buf, wg_bf, wu_bf, wd_bf, sem, *, layer):
    i = pl.program_id(0)
    mb = x_ref.shape[0] // ROW_WORDS
    half = wg_bf.shape[0] // 2

    def weight_copies(e, s):
        return [pltpu.make_async_copy(wg_hbm.at[layer, e], wg_buf.at[s], sem.at[s]),
                pltpu.make_async_copy(wu_hbm.at[layer, e], wu_buf.at[s], sem.at[s]),
                pltpu.make_async_copy(wd_hbm.at[layer, e], wd_buf.at[s], sem.at[s])]

    @pl.when(i == 0)
    def _():
        for cp in weight_copies(be_ref[0], 0):
            cp.start()

    @pl.when(first_ref[i] == 1)
    def _():
        s = slot_ref[i]
        for cp in weight_copies(be_ref[i], s):
            cp.wait()

        @pl.when(nxt_ref[i] >= 0)
        def _():
            for cp in weight_copies(nxt_ref[i], 1 - s):
                cp.start()

        wg_bf[...] = wg_buf[s].astype(BF16)
        wu_bf[...] = wu_buf[s].astype(BF16)
        wd_bf[...] = wd_buf[s].astype(BF16)

    @pl.when(bn_ref[i] > 0)
    def _():
        live = lax.broadcasted_iota(jnp.int32, (mb, 1), 0) < bn_ref[i]
        x_lo, x_hi = _load_rows(x_ref, 0, mb)
        x_lo = jnp.where(live, x_lo, 0.0).astype(BF16)
        x_hi = jnp.where(live, x_hi, 0.0).astype(BF16)
        g = _dot(x_lo, wg_bf[0:half, :]) + _dot(x_hi, wg_bf[half:, :])
        u = _dot(x_lo, wu_bf[0:half, :]) + _dot(x_hi, wu_bf[half:, :])
        hid = (_silu(g) * u).astype(BF16)
        _store_rows(y_ref, _dot(hid, wd_bf[...]))


def _experts(table, xs, w_gate, w_up, w_down, layer):
    n_blocks = table[0].shape[0]
    _, _, d, de = w_gate.shape
    blk = pl.BlockSpec((MOE_BLOCK * ROW_WORDS, LANES), lambda i, *_: (i, 0))
    hbm = pl.BlockSpec(memory_space=pl.ANY)
    return pl.pallas_call(
        functools.partial(_expert_kernel, layer=layer),
        grid_spec=pltpu.PrefetchScalarGridSpec(
            num_scalar_prefetch=len(table),
            grid=(n_blocks,),
            in_specs=[blk, hbm, hbm, hbm],
            out_specs=blk,
            scratch_shapes=[pltpu.VMEM((2, d, de), F32), pltpu.VMEM((2, d, de), F32), pltpu.VMEM((2, de, d), F32),
                            pltpu.VMEM((d, de), BF16), pltpu.VMEM((d, de), BF16), pltpu.VMEM((de, d), BF16),
                            pltpu.SemaphoreType.DMA((2,))]),
        out_shape=jax.ShapeDtypeStruct(xs.shape, jnp.uint32),
        compiler_params=_cparams("arbitrary"),
        name="moe_experts",
    )(*table, xs, w_gate, w_up, w_down)


def _combine_kernel(dest_ref, dest_next_ref, x1_ref, route_ref, mod_ref, lng_ref, lnb_ref, y_ref, o_ref,
                    buf_ref, sem, *, mod_row, rows_per_batch):
    tm, d = x1_ref.shape
    i = pl.program_id(0)
    n = pl.num_programs(0)
    slot = i % 2

    def gather(idx_ref, s):
        def body(r4, c):
            for u in range(DMA_UNROLL):
                r = r4 * DMA_UNROLL + u
                for k in range(2):
                    _row_copy(y_ref, idx_ref[0, 0, k * tm + r], buf_ref.at[s], k * tm + r,
                              sem.at[s]).start(priority=k)
            return c

        lax.fori_loop(0, tm // DMA_UNROLL, body, 0)

    @pl.when(i == 0)
    def _():
        gather(dest_ref, 0)

    @pl.when(i + 1 < n)
    def _():
        gather(dest_next_ref, 1 - slot)

    for k in range(2):
        pltpu.make_async_copy(y_ref.at[pl.ds(0, tm * ROW_WORDS)],
                              buf_ref.at[slot, pl.ds(0, tm * ROW_WORDS)], sem.at[slot]).wait()

    if mod_row is None:
        row = (i * tm) // rows_per_batch
    else:
        row = mod_row
    gate2 = mod_ref[pl.ds(row, 1), :][:, 5 * d:6 * d]
    rt = route_ref[...]
    f = jnp.zeros((tm, d), F32)
    for k in range(2):
        lo, hi = _load_rows(buf_ref.at[slot], k * tm, tm)
        f = f + jnp.concatenate([lo, hi], axis=1) * rt[:, 2 + k:3 + k]
    o_ref[...] = _layer_norm(RES_ALPHA * x1_ref[...] + gate2 * f) * lng_ref[...] + lnb_ref[...]


def _combine(dest, x1, route, mod, ln_g, ln_b, y_tiles, *, tm, row_off, rows, mod_row, rows_per_batch):
    d = x1.shape[1]
    off = row_off // tm
    steps = rows // tm
    kern = functools.partial(_combine_kernel, mod_row=mod_row, rows_per_batch=rows_per_batch)
    const = lambda a: pl.BlockSpec(a.shape, lambda i: (0,) * a.ndim)
    return pl.pallas_call(
        kern,
        grid=(steps,),
        in_specs=[pl.BlockSpec((1, 1, 2 * tm), lambda i: (off + i, 0, 0), memory_space=pltpu.SMEM),
                  pl.BlockSpec((1, 1, 2 * tm), lambda i: (off + jnp.minimum(i + 1, steps - 1), 0, 0),
                               memory_space=pltpu.SMEM),
                  pl.BlockSpec((tm, d), lambda i: (off + i, 0)),
                  pl.BlockSpec((tm, LANES), lambda i: (off + i, 0)),
                  const(mod), const(ln_g), const(ln_b),
                  pl.BlockSpec(memory_space=pl.ANY)],
        out_specs=pl.BlockSpec((tm, d), lambda i: (i, 0)),
        out_shape=jax.ShapeDtypeStruct((rows, d), F32),
        scratch_shapes=[pltpu.VMEM((2, 2 * tm * ROW_WORDS, LANES), jnp.uint32), pltpu.SemaphoreType.DMA((2,))],
        compiler_params=_cparams("arbitrary"),
        name="moe_combine",
    )(dest, dest, x1, route, mod, ln_g, ln_b, y_tiles)


def _rope_tables(n_pos):
    rows = n_pos // GRID_W
    row = jnp.repeat(jnp.arange(rows), GRID_W).astype(F32)
    col = jnp.tile(jnp.arange(GRID_W), rows).astype(F32)
    n_freq = HEAD_DIM // 4
    freq = ROPE_BASE ** (-jnp.arange(n_freq, dtype=F32) / n_freq)
    ang_r, ang_c = row[:, None] * freq, col[:, None] * freq
    cos_h = jnp.concatenate([jnp.cos(ang_r)] * 2 + [jnp.cos(ang_c)] * 2, axis=1)
    sin_h = jnp.concatenate([-jnp.sin(ang_r), jnp.sin(ang_r), -jnp.sin(ang_c), jnp.sin(ang_c)], axis=1)
    return jnp.tile(cos_h, (1, 2)), jnp.tile(sin_h, (1, 2))


def _block_table(counts, n_blocks):
    cnt = counts.astype(jnp.int32)
    padded = (cnt + MOE_BLOCK - 1) // MOE_BLOCK * MOE_BLOCK
    pad_end = jnp.cumsum(padded)
    pad_start = pad_end - padded
    blk_start = jnp.arange(n_blocks, dtype=jnp.int32)[:, None] * MOE_BLOCK
    be = jnp.minimum(jnp.sum((pad_end[None, :] <= blk_start).astype(jnp.int32), axis=1), N_EXPERTS - 1)
    ids = jnp.arange(N_EXPERTS, dtype=jnp.int32)
    mine = be[:, None] == ids[None, :]
    fill = jnp.sum(jnp.where(mine, cnt[None, :] + pad_start[None, :], 0), axis=1) - blk_start[:, 0]
    bn = jnp.clip(fill, 0, MOE_BLOCK)
    prev = jnp.concatenate([jnp.full((1,), -1, jnp.int32), be[:-1]])
    first = ((bn > 0) & (be != prev)).astype(jnp.int32)
    slot = (jnp.cumsum(first) - 1) % 2
    later = (ids[None, :] > ids[:, None]) & (cnt[None, :] > 0)
    nxt_e = jnp.min(jnp.where(later, ids[None, :], N_EXPERTS), axis=1)
    nxt_e = jnp.where(nxt_e == N_EXPERTS, -1, nxt_e)
    nxt = jnp.sum(jnp.where(mine, nxt_e[None, :], 0), axis=1)
    return be, bn, first, slot.astype(jnp.int32), nxt.astype(jnp.int32)


def _moe(route_t, h2_tiles, w_gate, w_up, w_down, layer):
    rows = route_t.shape[1]
    tm = ROW_TILE
    n_blocks = -(-(2 * rows) // MOE_BLOCK) + N_EXPERTS
    dest8, counts = _plan(route_t, tm)
    dest = dest8[:, 0:2, :].reshape(rows // tm, 1, 2 * tm)
    table = _block_table(counts[:, 0], n_blocks)
    xs = _dispatch(dest, h2_tiles, n_blocks * MOE_BLOCK, tm)
    ys = _experts(table, xs, w_gate, w_up, w_down, layer)
    return dest, ys


def kernel(x, c, ctx, c_ctx, w_ada, b_ada, w_in, w_fourier, attn_sink, w_pool, pool_scale, w_sgu, b_sgu,
           w_out, ln1_g, ln1_b, w_router_group, w_router_expert, w_exp_gate, w_exp_up, w_exp_down,
           ln2_g, ln2_b):
    b, s, d = x.shape
    n_ctx = ctx.shape[1]
    n_layers = w_in.shape[0]
    tm = ROW_TILE
    cond = jnp.concatenate([c, c_ctx[None, :], jnp.zeros((SUBLANES - b - 1, d), F32)], axis=0)
    mod_all = _ada(cond, w_ada, b_ada[:, None, :])
    cos_t, sin_t = _rope_tables(s)
    x2 = x.reshape(b * s, d)
    c2 = ctx.reshape(b * n_ctx, d)
    n_sgu = w_sgu.shape[1]
    for layer in range(n_layers):
        last = layer == n_layers - 1
        mod = mod_all[layer]
        w_in_l = w_in[layer].astype(BF16)
        wf = w_fourier[layer].astype(BF16)
        w_pool_bd = jax.scipy.linalg.block_diag(*[w_pool[layer, g] for g in range(w_pool.shape[1])]).astype(BF16)
        w_sgu_stack = w_sgu[layer].reshape(n_sgu * SGU_CHUNK, SGU_CHUNK).astype(BF16)
        b_sgu_exp = jnp.repeat(b_sgu[layer].T, GROUP_W // n_sgu, axis=1)
        w_router = jnp.concatenate([w_router_expert[layer].reshape(d, N_EXPERTS), w_router_group[layer]], axis=1)
        w_router = jnp.pad(w_router, ((0, 0), (0, LANES - w_router.shape[1])))
        wr_hi = w_router.astype(BF16)
        wr_lo = (w_router - wr_hi.astype(F32)).astype(BF16)
        merge_consts = (mod, w_pool_bd, pool_scale[layer][None, :], w_sgu_stack, b_sgu_exp,
                        w_out[layer].astype(BF16), ln1_g[layer][None, :], ln1_b[layer][None, :], wr_hi, wr_lo)
        sink = attn_sink[layer]

        a, q, qs, k, v, p, ug = _proj(x2, mod, w_in_l, cos_t, sin_t, mod_row=None, rows_per_batch=s,
                                      rope=True, tm=tm, a_pitch=FFT_PITCH)
        ac, qc, qsc, kc, vc, pc, ugc = _proj(c2, mod, w_in_l, cos_t, sin_t, mod_row=b, rows_per_batch=n_ctx,
                                             rope=False, tm=n_ctx, a_pitch=FFT_R)
        y_four = _fourier(a, wf, s)
        y_attn = _attention(sink, q, qs, k, v, kc, vc, seq=s, n_ctx=n_ctx, band=True)
        total = b * s + (0 if last else b * n_ctx)
        merged = _merge(x2, p, ug, y_four, y_attn, *merge_consts, (),
                        mod_row=None, seq=s, tm=tm, row_off=0, total_rows=total)
        if not last:
            yc_four = _fourier_small(ac, wf, n_ctx)
            yc_attn = _attention(sink, qc, qsc, kc, vc, kc, vc, seq=n_ctx, n_ctx=n_ctx, band=False)
            merged = _merge(c2, pc, ugc, yc_four, yc_attn, *merge_consts, tuple(merged),
                            mod_row=b, seq=n_ctx, tm=n_ctx, row_off=b * s, total_rows=total)
        x1, h2_tiles, route, route_t = merged
        dest, ys = _moe(route_t, h2_tiles, w_exp_gate, w_exp_up, w_exp_down, layer)
        ln_g, ln_b = ln2_g[layer][None, :], ln2_b[layer][None, :]
        x2 = _combine(dest, x1, route, mod, ln_g, ln_b, ys, tm=tm, row_off=0, rows=b * s,
                      mod_row=None, rows_per_batch=s)
        if not last:
            c2 = _combine(dest, x1, route, mod, ln_g, ln_b, ys, tm=tm, row_off=b * s, rows=b * n_ctx,
                          mod_row=b, rows_per_batch=n_ctx)
    return x2.reshape(b, s, d)
```

```python
import functools
import math

import numpy as np
import jax
import jax.numpy as jnp
from jax import lax
from jax.experimental import pallas as pl
from jax.experimental.pallas import tpu as pltpu

GRID_W = 64
HEAD_DIM = 64
GROUP_W = 256
KV_W = 128
WINDOW = 128
POOL_WINDOWS = (2, 4, 8, 16)
SGU_CHUNK = 128
N_GROUPS = 4
EXPERTS_PER_GROUP = 8
N_EXPERTS = 32
ROPE_BASE = 10000.0
LN_EPS = 1e-6
NEG_INF = -1e30
DEPTH = 2
RES_ALPHA = (2 * DEPTH) ** 0.25

LANES = 128
SUBLANES = 8
VMEM_LIMIT = 48 * 1024 * 1024

ROW_TILE = 512
Q_BLOCK = 256
MOE_BLOCK = 256
FFT_R = 64
FFT_PITCH = 72
FFT_UNROLL = 4
DMA_UNROLL = 4
MERGE_PART = 256

BF16 = jnp.bfloat16
F32 = jnp.float32


def _cparams(*sem):
    return pltpu.CompilerParams(dimension_semantics=sem, vmem_limit_bytes=VMEM_LIMIT)


def _dot(a, b):
    return jnp.dot(a, b, preferred_element_type=F32)


def _dot_nt(a, b):
    return lax.dot_general(a, b, (((1,), (1,)), ((), ())), preferred_element_type=F32)


def _layer_norm(t):
    mu = jnp.mean(t, axis=-1, keepdims=True)
    d = t - mu
    var = jnp.mean(d * d, axis=-1, keepdims=True)
    return d * lax.rsqrt(var + LN_EPS)


def _silu(t):
    return t * (1.0 / (1.0 + jnp.exp(-t)))


def _gelu(t):
    return 0.5 * t * (1.0 + lax.erf(t * (1.0 / math.sqrt(2.0))))


ROW_WORDS = 4
HI_MASK = 0xFFFF0000


def _pack_rows(t):
    half = t.shape[1] // 2
    lo = lax.bitcast_convert_type(t[:, :half].astype(BF16).astype(F32), jnp.uint32)
    hi = lax.bitcast_convert_type(t[:, half:].astype(BF16).astype(F32), jnp.uint32)
    return (lo >> 16) | (hi & jnp.uint32(HI_MASK))


def _unpack_rows(w):
    return (lax.bitcast_convert_type(w << 16, F32),
            lax.bitcast_convert_type(w & jnp.uint32(HI_MASK), F32))


def _store_rows(ref, t, first=0):
    w = _pack_rows(t)
    for j in range(ROW_WORDS):
        ref[pl.ds(first * ROW_WORDS + j, t.shape[0], stride=ROW_WORDS), :] = w[:, j * LANES:(j + 1) * LANES]


def _load_rows(ref, first, m):
    w = jnp.concatenate([ref[pl.ds(first * ROW_WORDS + j, m, stride=ROW_WORDS), :] for j in range(ROW_WORDS)],
                        axis=1)
    return _unpack_rows(w)


def _ada_kernel(c_ref, w_ref, b_ref, o_ref):
    s = _silu(c_ref[...]).astype(BF16)
    o_ref[...] = _dot(s, w_ref[...].astype(BF16)) + b_ref[...]


def _ada(cond, w_ada, b_ada):
    n_layers, d, n = w_ada.shape
    tn = n // 4
    return pl.pallas_call(
        _ada_kernel,
        grid=(n_layers, n // tn),
        in_specs=[
            pl.BlockSpec((SUBLANES, d), lambda l, j: (0, 0)),
            pl.BlockSpec((None, d, tn), lambda l, j: (l, 0, j)),
            pl.BlockSpec((None, 1, tn), lambda l, j: (l, 0, j)),
        ],
        out_specs=pl.BlockSpec((None, SUBLANES, tn), lambda l, j: (l, 0, j)),
        out_shape=jax.ShapeDtypeStruct((n_layers, SUBLANES, n), F32),
        compiler_params=_cparams("arbitrary", "arbitrary"),
        name="ada",
    )(cond, w_ada, b_ada)


def _rope(t, cos_t, sin_t):
    lane = lax.broadcasted_iota(jnp.int32, t.shape, 1)
    first = (lane % 32) < 16
    partner = jnp.where(first, pltpu.roll(t, LANES - 16, axis=1), pltpu.roll(t, 16, axis=1))
    return t * cos_t + partner * sin_t


def _proj_body(x, m, w_ref, cos_ref, sin_ref, outs, *, rope, a_pitch):
    a_ref, q_ref, qs_ref, k_ref, v_ref, p_ref, ug_ref = outs
    tm, d = x.shape
    shift, scale = m[:, 0:d], m[:, d:2 * d]
    h = _layer_norm(x) * (1.0 + scale) + shift
    z = _dot(h.astype(BF16), w_ref[...])
    pad = jnp.zeros((a_pitch - FFT_R, LANES), F32)
    for g in range(tm // FFT_R):
        for hf in range(2):
            grp = z[g * FFT_R:(g + 1) * FFT_R, hf * LANES:(hf + 1) * LANES]
            if a_pitch > FFT_R:
                grp = jnp.concatenate([grp, pad], axis=0)
            a_ref[hf, g * a_pitch:(g + 1) * a_pitch, :] = grp
    q0, q1 = z[:, 256:384], z[:, 384:512]
    k = z[:, 512:640]
    if rope:
        cos_t, sin_t = cos_ref[...], sin_ref[...]
        q0, q1, k = _rope(q0, cos_t, sin_t), _rope(q1, cos_t, sin_t), _rope(k, cos_t, sin_t)
    q_ref[:, 0:128] = q0.astype(BF16)
    q_ref[:, 128:256] = q1.astype(BF16)
    qs_ref[:, 0:128] = pltpu.roll(q0, HEAD_DIM, axis=1).astype(BF16)
    qs_ref[:, 128:256] = pltpu.roll(q1, HEAD_DIM, axis=1).astype(BF16)
    k_ref[...] = k.astype(BF16)
    v_ref[...] = z[:, 640:768].astype(BF16)
    p_ref[...] = z[:, 768:1024].astype(BF16)
    ug_ref[...] = z[:, 1024:1536].astype(BF16)


def _mod_row(mod_ref, mod_row, tm, rows_per_batch):
    row = (pl.program_id(0) * tm) // rows_per_batch if mod_row is None else mod_row
    return mod_ref[pl.ds(row, 1), :]


def _proj_kernel(x_ref, mod_ref, w_ref, cos_ref, sin_ref, *outs, mod_row, rows_per_batch, rope, a_pitch):
    m = _mod_row(mod_ref, mod_row, x_ref.shape[0], rows_per_batch)
    _proj_body(x_ref[...], m, w_ref, cos_ref, sin_ref, outs, rope=rope, a_pitch=a_pitch)


def _proj_specs(rows, tm, a_pitch, seq_steps):
    row_spec = lambda w: pl.BlockSpec((tm, w), lambda i: (i, 0))
    out_w = (256, 256, 128, 128, 256, 512)
    ta = tm // FFT_R * a_pitch
    out_specs = [pl.BlockSpec((2, ta, LANES), lambda i: (0, i, 0))] + [row_spec(w) for w in out_w]
    out_shape = ([jax.ShapeDtypeStruct((2, rows // FFT_R * a_pitch, LANES), F32)]
                 + [jax.ShapeDtypeStruct((rows, w), BF16) for w in out_w])
    table_spec = pl.BlockSpec((tm, LANES), lambda i: (i % seq_steps, 0))
    return table_spec, out_specs, out_shape


def _proj(x2, mod, w_in, cos_t, sin_t, *, mod_row, rows_per_batch, rope, tm, a_pitch):
    rows, d = x2.shape
    kern = functools.partial(_proj_kernel, mod_row=mod_row, rows_per_batch=rows_per_batch,
                             rope=rope, a_pitch=a_pitch)
    table_spec, out_specs, out_shape = _proj_specs(rows, tm, a_pitch, cos_t.shape[0] // tm)
    return pl.pallas_call(
        kern,
        grid=(rows // tm,),
        in_specs=[
            pl.BlockSpec((tm, d), lambda i: (i, 0)),
            pl.BlockSpec(mod.shape, lambda i: (0, 0)),
            pl.BlockSpec(w_in.shape, lambda i: (0, 0)),
            table_spec, table_spec,
        ],
        out_specs=out_specs,
        out_shape=out_shape,
        compiler_params=_cparams("arbitrary"),
        name="proj",
    )(x2, mod, w_in, cos_t, sin_t)


def _fft_tables(n_pos):
    r = FFT_R
    assert n_pos == r * r
    kb = np.arange(r)[None, :, None]
    na = np.arange(r)[:, None, None]
    nb = np.arange(r)[None, None, :]
    ang = 2.0 * np.pi * ((kb * (na + r * nb)) % n_pos) / n_pos
    m1 = np.concatenate([np.cos(ang), -np.sin(ang)], axis=1)
    ka = np.arange(r)[:, None]
    n2 = np.arange(r)[None, :]
    ang2 = 2.0 * np.pi * ((ka * n2) % r) / r
    c2, s2 = np.cos(ang2), np.sin(ang2)
    w2 = np.block([[c2, s2], [-s2, c2]])
    return m1, w2


def _channel_tables(n_pos):
    h = HEAD_DIM
    c = np.arange(h)
    ang = 2.0 * np.pi * ((c[:, None] * c[None, :]) % h) / h
    scale = 1.0 / math.sqrt(n_pos * h)
    eye = np.eye(GROUP_W // h)
    cc = np.kron(eye, np.cos(ang)) * scale
    ss = np.kron(eye, np.sin(ang)) * scale
    return np.concatenate([cc, ss], axis=0)


def _fourier_kernel(a_ref, m1_ref, w2_ref, ch_ref, wf_ref, o_ref, z_ref, y_ref):
    r, pt = FFT_R, FFT_PITCH

    def step1(i, c):
        for u in range(FFT_UNROLL):
            na = i * FFT_UNROLL + u
            rows = jnp.concatenate([a_ref[0, pl.ds(na, r, stride=pt), :],
                                    a_ref[1, pl.ds(na, r, stride=pt), :]], axis=1)
            z = _dot(m1_ref[na], rows.astype(BF16))
            base = pl.multiple_of(na * pt, SUBLANES)
            z_ref[0, pl.ds(base, r), :] = z[0:r, 0:LANES]
            z_ref[1, pl.ds(base, r), :] = z[0:r, LANES:]
            z_ref[2, pl.ds(base, r), :] = z[r:, 0:LANES]
            z_ref[3, pl.ds(base, r), :] = z[r:, LANES:]
        return c

    lax.fori_loop(0, r // FFT_UNROLL, step1, 0)

    def step2(i, c):
        for u in range(FFT_UNROLL):
            kb = i * FFT_UNROLL + u
            q = [z_ref[j, pl.ds(kb, r, stride=pt), :] for j in range(4)]
            zs = jnp.concatenate([jnp.concatenate(q[0:2], axis=1),
                                  jnp.concatenate(q[2:4], axis=1)], axis=0)
            y = _dot(w2_ref[...], zs.astype(BF16))
            base = pl.multiple_of(kb * r, r)
            y_ref[0, pl.ds(base, r), :] = y[0:r, 0:LANES]
            y_ref[1, pl.ds(base, r), :] = y[0:r, LANES:]
            y_ref[2, pl.ds(base, r), :] = y[r:, 0:LANES]
            y_ref[3, pl.ds(base, r), :] = y[r:, LANES:]
        return c

    lax.fori_loop(0, r // FFT_UNROLL, step2, 0)

    chunk = 8 * r
    for cidx in range(r * r // chunk):
        yy = jnp.concatenate([y_ref[j, cidx * chunk:(cidx + 1) * chunk, :] for j in range(4)], axis=1)
        f = _dot(yy.astype(BF16), ch_ref[...])
        g = _dot(f.astype(BF16), wf_ref[...])
        for gi in range(chunk // r):
            kb = cidx * (chunk // r) + gi
            z_ref[0, kb * pt:kb * pt + r, :] = g[gi * r:(gi + 1) * r, 0:LANES]
            z_ref[1, kb * pt:kb * pt + r, :] = g[gi * r:(gi + 1) * r, LANES:]

    def step3(i, c):
        for u in range(FFT_UNROLL):
            ka = i * FFT_UNROLL + u
            base = pl.multiple_of(ka * r, r)
            o_ref[pl.ds(base, r), 0:LANES] = z_ref[0, pl.ds(ka, r, stride=pt), :]
            o_ref[pl.ds(base, r), LANES:] = z_ref[1, pl.ds(ka, r, stride=pt), :]
        return c

    lax.fori_loop(0, r // FFT_UNROLL, step3, 0)


def _fourier(a3, w_fourier, n_pos):
    rows = a3.shape[1] // FFT_PITCH * FFT_R
    gw = GROUP_W
    m1, w2 = _fft_tables(n_pos)
    ch = _channel_tables(n_pos)
    const = lambda shape: pl.BlockSpec(shape, lambda b: (0,) * len(shape))
    return pl.pallas_call(
        _fourier_kernel,
        grid=(rows // n_pos,),
        in_specs=[
            pl.BlockSpec((2, FFT_R * FFT_PITCH, LANES), lambda b: (0, b, 0)),
            const(m1.shape), const(w2.shape), const(ch.shape), const(w_fourier.shape),
        ],
        out_specs=pl.BlockSpec((n_pos, gw), lambda b: (b, 0)),
        out_shape=jax.ShapeDtypeStruct((rows, gw), F32),
        scratch_shapes=[pltpu.VMEM((4, FFT_R * FFT_PITCH, LANES), F32), pltpu.VMEM((4, n_pos, LANES), F32)],
        compiler_params=_cparams("arbitrary"),
        name="fourier",
    )(a3, jnp.asarray(m1, BF16), jnp.asarray(w2, BF16), jnp.asarray(ch, BF16), w_fourier)


def _fourier_small_kernel(a_ref, cs_ref, ch_ref, wf_ref, o_ref):
    n = a_ref.shape[1]
    a = jnp.concatenate([a_ref[0], a_ref[1]], axis=1)
    pq = _dot(cs_ref[...], a.astype(BF16))
    y = jnp.concatenate([pq[0:n], pq[n:2 * n]], axis=1).astype(BF16)
    f = _dot(y, ch_ref[...])
    o_ref[...] = _dot(f.astype(BF16), wf_ref[...])


def _fourier_small(a3, w_fourier, n_pos):
    _, rows, _ = a3.shape
    gw = GROUP_W
    k = np.arange(n_pos)
    ang = 2.0 * np.pi * ((k[:, None] * k[None, :]) % n_pos) / n_pos
    cs = np.concatenate([np.cos(ang), -np.sin(ang)], axis=0)
    ch = _channel_tables(n_pos)
    const = lambda shape: pl.BlockSpec(shape, lambda b: (0,) * len(shape))
    return pl.pallas_call(
        _fourier_small_kernel,
        grid=(rows // n_pos,),
        in_specs=[pl.BlockSpec((2, n_pos, LANES), lambda b: (0, b, 0)),
                  const(cs.shape), const(ch.shape), const(w_fourier.shape)],
        out_specs=pl.BlockSpec((n_pos, gw), lambda b: (b, 0)),
        out_shape=jax.ShapeDtypeStruct((rows, gw), F32),
        compiler_params=_cparams("arbitrary"),
        name="fourier_ctx",
    )(a3, jnp.asarray(cs, BF16), jnp.asarray(ch, BF16), w_fourier)


ATTN_SUB = 128


def _attn_kernel(sink_ref, q_ref, qs_ref, k_ref, v_ref, kc_ref, vc_ref, o_ref, *, band, seq):
    qb = q_ref.shape[0]
    sub = ATTN_SUB
    lane = lax.broadcasted_iota(jnp.int32, (1, LANES), 1)
    lo_half = lane < HEAD_DIM
    zero = jnp.zeros((), BF16)
    scale = jnp.asarray(HEAD_DIM ** -0.5, BF16)
    kw = sub + 2 * WINDOW
    for sb in range(qb // sub):
        rows = slice(sb * sub, (sb + 1) * sub)
        qa0, qa1 = q_ref[rows, 0:LANES], q_ref[rows, LANES:]
        qs0, qs1 = qs_ref[rows, 0:LANES], qs_ref[rows, LANES:]
        q_all = jnp.concatenate([jnp.where(lo_half, qa0, zero), jnp.where(lo_half, qs0, zero),
                                 jnp.where(lo_half, zero, qs1), jnp.where(lo_half, zero, qa1)], axis=0) * scale
        if band:
            p0 = pl.program_id(1) * qb + sb * sub
            start = pl.multiple_of(jnp.clip(p0 - WINDOW, 0, seq - kw), WINDOW)
            qpos = p0 + lax.broadcasted_iota(jnp.int32, (sub, 1), 0)
            kpos = start + lax.broadcasted_iota(jnp.int32, (1, kw), 1)
            bias = jnp.where(jnp.abs(qpos - kpos) <= WINDOW, 0.0, NEG_INF)
            keys = jnp.concatenate([k_ref[pl.ds(start, kw), :], kc_ref[...]], axis=0)
            vals = jnp.concatenate([v_ref[pl.ds(start, kw), :], vc_ref[...]], axis=0)
        else:
            keys, vals = kc_ref[...], vc_ref[...]
        s_all = _dot_nt(q_all, keys)
        probs, dens = [], []
        for h in range(4):
            s = s_all[h * sub:(h + 1) * sub, :]
            sink = sink_ref[h]
            if band:
                s = jnp.concatenate([s[:, 0:kw] + bias, s[:, kw:]], axis=1)
            m = jnp.maximum(jnp.max(s, axis=1, keepdims=True), sink)
            p = jnp.exp(s - m)
            dens.append(jnp.sum(p, axis=1, keepdims=True) + jnp.exp(sink - m))
            probs.append(p.astype(BF16))
        o_all = _dot(jnp.concatenate(probs, axis=0), vals)
        o = [o_all[h * sub:(h + 1) * sub, :] / dens[h] for h in range(4)]
        o_ref[rows, 0:LANES] = jnp.where(lo_half, o[0], pltpu.roll(o[1], HEAD_DIM, axis=1)).astype(BF16)
        o_ref[rows, LANES:] = jnp.where(lo_half, pltpu.roll(o[2], HEAD_DIM, axis=1), o[3]).astype(BF16)


def _attention(sink, q, qs, k, v, kc, vc, *, seq, n_ctx, band):
    rows = q.shape[0]
    n_batch = rows // seq
    qb = Q_BLOCK if band else seq
    steps = seq // qb
    kern = functools.partial(_attn_kernel, band=band, seq=seq)
    seq_spec = pl.BlockSpec((seq, KV_W), lambda b, i: (b, 0))
    ctx_spec = pl.BlockSpec((n_ctx, KV_W), lambda b, i: (b, 0))
    q_spec = pl.BlockSpec((qb, GROUP_W), lambda b, i: (b * steps + i, 0))
    return pl.pallas_call(
        kern,
        grid=(n_batch, steps),
        in_specs=[pl.BlockSpec(memory_space=pltpu.SMEM), q_spec, q_spec,
                  seq_spec, seq_spec, ctx_spec, ctx_spec],
        out_specs=q_spec,
        out_shape=jax.ShapeDtypeStruct((rows, GROUP_W), BF16),
        compiler_params=_cparams("arbitrary", "arbitrary"),
        name="attn" if band else "attn_ctx",
    )(sink, q, qs, k, v, kc, vc)


POOL_HALO = max(POOL_WINDOWS) // 2


def _pool(p_ref, t0, tm, seq):
    halo = POOL_HALO
    pack = 2 * SUBLANES
    t0 = pl.multiple_of(t0, pack)
    main = p_ref[pl.ds(t0, tm), :].astype(F32)
    lo = pl.multiple_of(jnp.maximum(t0 - pack, 0), pack)
    hi = pl.multiple_of(jnp.minimum(t0 + tm, seq - pack), pack)
    prev = p_ref[pl.ds(lo, pack), :].astype(F32)[pack - halo:, :]
    nxt = p_ref[pl.ds(hi, pack), :].astype(F32)[:halo, :]
    prev = jnp.where(t0 > 0, prev, 0.0)
    nxt = jnp.where(t0 + tm < seq, nxt, 0.0)
    full = jnp.concatenate([prev, main, nxt], axis=0)
    n = tm + 2 * halo
    gch = GROUP_W // len(POOL_WINDOWS)
    first = lax.broadcasted_iota(jnp.int32, (1, LANES), 1) < gch
    means = []
    for hf in range(GROUP_W // LANES):
        wa, wb = POOL_WINDOWS[2 * hf], POOL_WINDOWS[2 * hf + 1]
        x = full[:, hf * LANES:(hf + 1) * LANES]
        sums, w, s = {}, 2, pltpu.roll(x, 1, axis=0) + x
        sums[w] = s
        while w < wb:
            s = pltpu.roll(s, w // 2, axis=0) + pltpu.roll(s, n - w // 2, axis=0)
            w *= 2
            sums[w] = s
        means.append(jnp.where(first, sums[wa] * (1.0 / wa), sums[wb] * (1.0 / wb))[halo:halo + tm, :])
    mean = jnp.concatenate(means, axis=1)
    win = jnp.concatenate([jnp.full((1, gch), w, jnp.int32) for w in POOL_WINDOWS], axis=1)

    def rescale(rows, first_pos):
        pos = first_pos + lax.broadcasted_iota(jnp.int32, (halo, 1), 0)
        cnt = jnp.minimum(pos + win // 2, seq) - jnp.maximum(pos - win // 2, 0)
        return rows * (win.astype(F32) / cnt.astype(F32))

    mean = jnp.concatenate([rescale(mean[:halo], t0), mean[halo:tm - halo],
                            rescale(mean[tm - halo:], t0 + tm - halo)], axis=0)
    return mean - main


def _route(logits):
    tm = logits.shape[0]
    lt = logits.T
    gl = lt[N_EXPERTS:N_EXPERTS + N_GROUPS]
    sub_g = lax.broadcasted_iota(jnp.int32, gl.shape, 0)
    gmax = jnp.max(gl, axis=0, keepdims=True)
    grp = jnp.min(jnp.where(gl == gmax, sub_g, N_GROUPS), axis=0, keepdims=True)
    gate_group = 1.0 / jnp.sum(jnp.exp(gl - gmax), axis=0, keepdims=True)
    el = lt[0:EXPERTS_PER_GROUP]
    for g in range(1, N_GROUPS):
        el = jnp.where(grp == g, lt[g * EXPERTS_PER_GROUP:(g + 1) * EXPERTS_PER_GROUP], el)
    sub = lax.broadcasted_iota(jnp.int32, el.shape, 0)
    m1 = jnp.max(el, axis=0, keepdims=True)
    i1 = jnp.min(jnp.where(el == m1, sub, EXPERTS_PER_GROUP), axis=0, keepdims=True)
    el2 = jnp.where(sub == i1, -jnp.inf, el)
    m2 = jnp.max(el2, axis=0, keepdims=True)
    i2 = jnp.min(jnp.where(el2 == m2, sub, EXPERTS_PER_GROUP), axis=0, keepdims=True)
    r = jnp.exp(m2 - m1)
    g1 = gate_group / (1.0 + r)
    g2 = g1 * r
    e1 = (grp * EXPERTS_PER_GROUP + i1).astype(F32)
    e2 = (grp * EXPERTS_PER_GROUP + i2).astype(F32)
    rows = jnp.where(sub == 0, e1, jnp.where(sub == 1, e2, jnp.where(sub == 2, g1, jnp.where(sub == 3, g2, 0.0))))
    cols = jnp.concatenate([rows, jnp.zeros((LANES - rows.shape[0], tm), F32)], axis=0).T
    return cols, rows


def _merge_kernel(x_ref, p_ref, ug_ref, yf_ref, ya_ref, mod_ref, wpool_ref, pscale_ref, wsgu_ref, bsgu_ref,
                  wout_ref, lng_ref, lnb_ref, wrh_ref, wrl_ref, *rest,
                  mod_row, seq, n_alias):
    x1_ref, h2_ref, route_ref, route_t_ref = rest[n_alias:]
    tm, d = x_ref.shape
    if mod_row is None:
        row = pl.program_id(0)
    else:
        row = mod_row
    t0 = pl.multiple_of(pl.program_id(1) * tm, tm)
    m = mod_ref[pl.ds(row, 1), :]
    gate1, shift2, scale2 = m[:, 2 * d:3 * d], m[:, 3 * d:4 * d], m[:, 4 * d:5 * d]
    lane = lax.broadcasted_iota(jnp.int32, (1, GROUP_W), 1)
    n_heads = wsgu_ref.shape[0] // SGU_CHUNK
    head = lane // (GROUP_W // n_heads)

    pm = min(tm, MERGE_PART)
    for part in range(tm // pm):
        r0 = part * pm
        rows = slice(r0, r0 + pm)
        pooled = _pool(p_ref, t0 + r0, pm, seq)
        y_pool = _dot(pooled.astype(BF16), wpool_ref[...]) * pscale_ref[...]

        ug = ug_ref[rows, :].astype(F32)
        u = _gelu(ug[:, 0:GROUP_W])
        v = _layer_norm(_gelu(ug[:, GROUP_W:])).astype(BF16)
        mixed = []
        for cidx in range(pm // SGU_CHUNK):
            vc = v[cidx * SGU_CHUNK:(cidx + 1) * SGU_CHUNK, :]
            full = _dot(wsgu_ref[...], vc)
            mc = bsgu_ref[...]
            for hd in range(n_heads):
                mc = mc + jnp.where(head == hd, full[hd * SGU_CHUNK:(hd + 1) * SGU_CHUNK, :], 0.0)
            mixed.append(mc)
        y_sgu = u * jnp.concatenate(mixed, axis=0)

        cat = jnp.concatenate([yf_ref[rows, :].astype(BF16), ya_ref[rows, :], y_pool.astype(BF16),
                               y_sgu.astype(BF16)], axis=1)
        y = _dot(cat, wout_ref[...])
        x1 = _layer_norm(RES_ALPHA * x_ref[rows, :] + gate1 * y) * lng_ref[...] + lnb_ref[...]
        x1_ref[rows, :] = x1
        h2 = _layer_norm(x1) * (1.0 + scale2) + shift2
        _store_rows(h2_ref, h2, first=r0)
        hh = h2.astype(BF16)
        hl = (h2 - hh.astype(F32)).astype(BF16)
        logits = _dot(hh, wrh_ref[...]) + (_dot(hh, wrl_ref[...]) + _dot(hl, wrh_ref[...]))
        route_ref[rows, :], route_t_ref[:, rows] = _route(logits)


def _merge(x2, p, ug, y_four, y_attn, mod, w_pool_bd, pool_scale, w_sgu_stack, b_sgu_exp, w_out,
           ln_g, ln_b, wr_hi, wr_lo, aliased, *, mod_row, seq, tm, row_off, total_rows):
    rows, d = x2.shape
    n_batch, steps = rows // seq, seq // tm
    off = row_off // tm
    kern = functools.partial(_merge_kernel, mod_row=mod_row, seq=seq, n_alias=len(aliased))
    row_spec = lambda w: pl.BlockSpec((tm, w), lambda b, i: (b * steps + i, 0))
    const = lambda a: pl.BlockSpec(a.shape, lambda b, i: (0,) * a.ndim)
    consts = (mod, w_pool_bd, pool_scale, w_sgu_stack, b_sgu_exp, w_out, ln_g, ln_b, wr_hi, wr_lo)
    n_in = 5 + len(consts)
    out_shapes = [jax.ShapeDtypeStruct((total_rows, d), F32),
                  jax.ShapeDtypeStruct((total_rows * ROW_WORDS, LANES), jnp.uint32),
                  jax.ShapeDtypeStruct((total_rows, LANES), F32),
                  jax.ShapeDtypeStruct((SUBLANES, total_rows), F32)]
    out_specs = [pl.BlockSpec((tm, d), lambda b, i: (off + b * steps + i, 0)),
                 pl.BlockSpec((tm * ROW_WORDS, LANES), lambda b, i: (off + b * steps + i, 0)),
                 pl.BlockSpec((tm, LANES), lambda b, i: (off + b * steps + i, 0)),
                 pl.BlockSpec((SUBLANES, tm), lambda b, i: (0, off + b * steps + i))]
    return pl.pallas_call(
        kern,
        grid=(n_batch, steps),
        in_specs=[row_spec(d), pl.BlockSpec((seq, GROUP_W), lambda b, i: (b, 0)),
                  row_spec(2 * GROUP_W), row_spec(GROUP_W), row_spec(GROUP_W)]
                 + [const(a) for a in consts]
                 + [pl.BlockSpec(memory_space=pl.ANY)] * len(aliased),
        out_specs=out_specs,
        out_shape=out_shapes,
        input_output_aliases={n_in + k: k for k in range(len(aliased))},
        compiler_params=_cparams("arbitrary", "arbitrary"),
        name="merge",
    )(x2, p, ug, y_four, y_attn, *consts, *aliased)


def _plan_kernel(route_ref, dest_ref, cnt_out_ref, cnt_ref, start_ref, carry_ref):
    ph, t = pl.program_id(0), pl.program_id(1)
    tm = route_ref.shape[1]
    rt = route_ref[...]
    e1 = rt[0:1, :].astype(jnp.int32)
    e2 = rt[1:2, :].astype(jnp.int32)
    sub = lax.broadcasted_iota(jnp.int32, (N_EXPERTS, tm), 0)
    hit1, hit2 = sub == e1, sub == e2
    onehot = jnp.where(hit1 | hit2, 1.0, 0.0)
    tile_cnt = jnp.sum(onehot, axis=1, keepdims=True)

    @pl.when((ph == 0) & (t == 0))
    def _():
        cnt_ref[...] = jnp.zeros_like(cnt_ref)

    @pl.when(ph == 0)
    def _():
        cnt_ref[...] += tile_cnt

    @pl.when((ph == 1) & (t == 0))
    def _():
        cnt = cnt_ref[...]
        padded = jnp.floor((cnt + (MOE_BLOCK - 1.0)) * (1.0 / MOE_BLOCK)) * MOE_BLOCK
        row = lax.broadcasted_iota(jnp.int32, cnt.shape, 0)
        incl = padded
        sh = 1
        while sh < N_EXPERTS:
            incl = incl + jnp.where(row >= sh, pltpu.roll(incl, sh, axis=0), 0.0)
            sh *= 2
        start_ref[...] = incl - padded
        carry_ref[...] = jnp.zeros_like(carry_ref)
        cnt_out_ref[...] = cnt

    @pl.when(ph == 1)
    def _():
        r_i = lax.broadcasted_iota(jnp.int32, (tm, tm), 0)
        c_i = lax.broadcasted_iota(jnp.int32, (tm, tm), 1)
        before = jnp.where(r_i < c_i, 1.0, 0.0).astype(BF16)
        rank = _dot(onehot.astype(BF16), before)
        base = start_ref[:, 0:1] + carry_ref[:, 0:1] + rank
        d1 = jnp.sum(jnp.where(hit1, base, 0.0), axis=0, keepdims=True)
        d2 = jnp.sum(jnp.where(hit2, base, 0.0), axis=0, keepdims=True)
        sub8 = lax.broadcasted_iota(jnp.int32, (SUBLANES, tm), 0)
        dest_ref[...] = jnp.where(sub8 == 0, d1, d2).astype(jnp.int32)
        carry_ref[...] += tile_cnt


def _plan(route_t, tm):
    rows = route_t.shape[1]
    n_t = rows // tm
    return pl.pallas_call(
        _plan_kernel,
        grid=(2, n_t),
        in_specs=[pl.BlockSpec((SUBLANES, tm), lambda ph, t: (0, t))],
        out_specs=[pl.BlockSpec((None, SUBLANES, tm), lambda ph, t: (t * ph, 0, 0)),
                   pl.BlockSpec((N_EXPERTS, LANES), lambda ph, t: (0, 0))],
        out_shape=[jax.ShapeDtypeStruct((n_t, SUBLANES, tm), jnp.int32),
                   jax.ShapeDtypeStruct((N_EXPERTS, LANES), F32)],
        scratch_shapes=[pltpu.VMEM((N_EXPERTS, LANES), F32)] * 3,
        compiler_params=_cparams("arbitrary", "arbitrary"),
        name="moe_plan",
    )(route_t)


def _row_copy(src_ref, src_row, dst_ref, dst_row, sem):
    return pltpu.make_async_copy(
        src_ref.at[pl.ds(pl.multiple_of(src_row * ROW_WORDS, ROW_WORDS), ROW_WORDS)],
        dst_ref.at[pl.ds(pl.multiple_of(dst_row * ROW_WORDS, ROW_WORDS), ROW_WORDS)], sem)


def _dispatch_kernel(dest_ref, h2_ref, xs_ref, sem, *, tm):
    def body(r4, c):
        for u in range(DMA_UNROLL):
            r = r4 * DMA_UNROLL + u
            for k in range(2):
                _row_copy(h2_ref, r, xs_ref, dest_ref[0, 0, k * tm + r], sem).start(priority=k)
        return c

    lax.fori_loop(0, tm // DMA_UNROLL, body, 0)
    for k in range(2):
        pltpu.make_async_copy(h2_ref, xs_ref.at[pl.ds(0, tm * ROW_WORDS)], sem).wait()


def _dispatch(dest, h2_tiles, n_slots, tm):
    n_t = dest.shape[0]
    return pl.pallas_call(
        functools.partial(_dispatch_kernel, tm=tm),
        grid=(n_t,),
        in_specs=[pl.BlockSpec((1, 1, 2 * tm), lambda i: (i, 0, 0), memory_space=pltpu.SMEM),
                  pl.BlockSpec((tm * ROW_WORDS, LANES), lambda i: (i, 0))],
        out_specs=pl.BlockSpec(memory_space=pl.ANY),
        out_shape=jax.ShapeDtypeStruct((n_slots * ROW_WORDS, LANES), jnp.uint32),
        scratch_shapes=[pltpu.SemaphoreType.DMA],
        compiler_params=_cparams("arbitrary"),
        name="moe_dispatch",
    )(dest, h2_tiles)


def _expert_kernel(be_ref, bn_ref, first_ref, slot_ref, nxt_ref, x_ref, wg_hbm, wu_hbm, wd_hbm, y_ref,
                   wg_buf, wu_buf, wd_buf, wg_bf, wu_bf, wd_bf, sem, *, layer):
    i = pl.program_id(0)
    mb = x_ref.shape[0] // ROW_WORDS
    half = wg_bf.shape[0] // 2

    def weight_copies(e, s):
        return [pltpu.make_async_copy(wg_hbm.at[layer, e], wg_buf.at[s], sem.at[s]),
                pltpu.make_async_copy(wu_hbm.at[layer, e], wu_buf.at[s], sem.at[s]),
                pltpu.make_async_copy(wd_hbm.at[layer, e], wd_buf.at[s], sem.at[s])]

    @pl.when(i == 0)
    def _():
        for cp in weight_copies(be_ref[0], 0):
            cp.start()

    @pl.when(first_ref[i] == 1)
    def _():
        s = slot_ref[i]
        for cp in weight_copies(be_ref[i], s):
            cp.wait()

        @pl.when(nxt_ref[i] >= 0)
        def _():
            for cp in weight_copies(nxt_ref[i], 1 - s):
                cp.start()

        wg_bf[...] = wg_buf[s].astype(BF16)
        wu_bf[...] = wu_buf[s].astype(BF16)
        wd_bf[...] = wd_buf[s].astype(BF16)

    @pl.when(bn_ref[i] > 0)
    def _():
        live = lax.broadcasted_iota(jnp.int32, (mb, 1), 0) < bn_ref[i]
        x_lo, x_hi = _load_rows(x_ref, 0, mb)
        x_lo = jnp.where(live, x_lo, 0.0).astype(BF16)
        x_hi = jnp.where(live, x_hi, 0.0).astype(BF16)
        g = _dot(x_lo, wg_bf[0:half, :]) + _dot(x_hi, wg_bf[half:, :])
        u = _dot(x_lo, wu_bf[0:half, :]) + _dot(x_hi, wu_bf[half:, :])
        hid = (_silu(g) * u).astype(BF16)
        _store_rows(y_ref, _dot(hid, wd_bf[...]))


def _experts(table, xs, w_gate, w_up, w_down, layer):
    n_blocks = table[0].shape[0]
    _, _, d, de = w_gate.shape
    blk = pl.BlockSpec((MOE_BLOCK * ROW_WORDS, LANES), lambda i, *_: (i, 0))
    hbm = pl.BlockSpec(memory_space=pl.ANY)
    return pl.pallas_call(
        functools.partial(_expert_kernel, layer=layer),
        grid_spec=pltpu.PrefetchScalarGridSpec(
            num_scalar_prefetch=len(table),
            grid=(n_blocks,),
            in_specs=[blk, hbm, hbm, hbm],
            out_specs=blk,
            scratch_shapes=[pltpu.VMEM((2, d, de), F32), pltpu.VMEM((2, d, de), F32), pltpu.VMEM((2, de, d), F32),
                            pltpu.VMEM((d, de), BF16), pltpu.VMEM((d, de), BF16), pltpu.VMEM((de, d), BF16),
                            pltpu.SemaphoreType.DMA((2,))]),
        out_shape=jax.ShapeDtypeStruct(xs.shape, jnp.uint32),
        compiler_params=_cparams("arbitrary"),
        name="moe_experts",
    )(*table, xs, w_gate, w_up, w_down)


def _combine_body(dest_ref, dest_next_ref, x1_ref, route_ref, m, lng_ref, lnb_ref, y_ref, buf_ref, sem):
    tm, d = x1_ref.shape
    i = pl.program_id(0)
    n = pl.num_programs(0)
    slot = i % 2

    def gather(idx_ref, s):
        def body(r4, c):
            for u in range(DMA_UNROLL):
                r = r4 * DMA_UNROLL + u
                for k in range(2):
                    _row_copy(y_ref, idx_ref[0, 0, k * tm + r], buf_ref.at[s], k * tm + r,
                              sem.at[s]).start(priority=k)
            return c

        lax.fori_loop(0, tm // DMA_UNROLL, body, 0)

    @pl.when(i == 0)
    def _():
        gather(dest_ref, 0)

    @pl.when(i + 1 < n)
    def _():
        gather(dest_next_ref, 1 - slot)

    for k in range(2):
        pltpu.make_async_copy(y_ref.at[pl.ds(0, tm * ROW_WORDS)],
                              buf_ref.at[slot, pl.ds(0, tm * ROW_WORDS)], sem.at[slot]).wait()

    gate2 = m[:, 5 * d:6 * d]
    rt = route_ref[...]
    f = jnp.zeros((tm, d), F32)
    for k in range(2):
        lo, hi = _load_rows(buf_ref.at[slot], k * tm, tm)
        f = f + jnp.concatenate([lo, hi], axis=1) * rt[:, 2 + k:3 + k]
    return _layer_norm(RES_ALPHA * x1_ref[...] + gate2 * f) * lng_ref[...] + lnb_ref[...]


def _combine_kernel(dest_ref, dest_next_ref, x1_ref, route_ref, mod_ref, lng_ref, lnb_ref, y_ref, o_ref,
                    buf_ref, sem, *, mod_row, rows_per_batch):
    m = _mod_row(mod_ref, mod_row, x1_ref.shape[0], rows_per_batch)
    o_ref[...] = _combine_body(dest_ref, dest_next_ref, x1_ref, route_ref, m, lng_ref, lnb_ref, y_ref,
                               buf_ref, sem)


def _combine_proj_kernel(dest_ref, dest_next_ref, x1_ref, route_ref, mod_ref, lng_ref, lnb_ref,
                         mod_next_ref, w_ref, cos_ref, sin_ref, y_ref, o_ref, *rest,
                         rows_per_batch, a_pitch):
    outs, (buf_ref, sem) = rest[:-2], rest[-2:]
    tm = x1_ref.shape[0]
    x = _combine_body(dest_ref, dest_next_ref, x1_ref, route_ref, _mod_row(mod_ref, None, tm, rows_per_batch),
                      lng_ref, lnb_ref, y_ref, buf_ref, sem)
    o_ref[...] = x
    _proj_body(x, _mod_row(mod_next_ref, None, tm, rows_per_batch), w_ref, cos_ref, sin_ref, outs,
               rope=True, a_pitch=a_pitch)


def _combine_specs(tm, d, off, steps):
    return [pl.BlockSpec((1, 1, 2 * tm), lambda i: (off + i, 0, 0), memory_space=pltpu.SMEM),
            pl.BlockSpec((1, 1, 2 * tm), lambda i: (off + jnp.minimum(i + 1, steps - 1), 0, 0),
                         memory_space=pltpu.SMEM),
            pl.BlockSpec((tm, d), lambda i: (off + i, 0)),
            pl.BlockSpec((tm, LANES), lambda i: (off + i, 0))]


def _combine_scratch(tm):
    return [pltpu.VMEM((2, 2 * tm * ROW_WORDS, LANES), jnp.uint32), pltpu.SemaphoreType.DMA((2,))]


def _combine(dest, x1, route, mod, ln_g, ln_b, y_tiles, *, tm, row_off, rows, mod_row, rows_per_batch):
    d = x1.shape[1]
    steps = rows // tm
    kern = functools.partial(_combine_kernel, mod_row=mod_row, rows_per_batch=rows_per_batch)
    const = lambda a: pl.BlockSpec(a.shape, lambda i: (0,) * a.ndim)
    return pl.pallas_call(
        kern,
        grid=(steps,),
        in_specs=_combine_specs(tm, d, row_off // tm, steps)
                 + [const(mod), const(ln_g), const(ln_b), pl.BlockSpec(memory_space=pl.ANY)],
        out_specs=pl.BlockSpec((tm, d), lambda i: (i, 0)),
        out_shape=jax.ShapeDtypeStruct((rows, d), F32),
        scratch_shapes=_combine_scratch(tm),
        compiler_params=_cparams("arbitrary"),
        name="moe_combine",
    )(dest, dest, x1, route, mod, ln_g, ln_b, y_tiles)


def _combine_proj(dest, x1, route, mod, ln_g, ln_b, y_tiles, mod_next, w_in, cos_t, sin_t, *,
                  tm, rows, rows_per_batch, a_pitch):
    d = x1.shape[1]
    steps = rows // tm
    kern = functools.partial(_combine_proj_kernel, rows_per_batch=rows_per_batch, a_pitch=a_pitch)
    const = lambda a: pl.BlockSpec(a.shape, lambda i: (0,) * a.ndim)
    table_spec, proj_specs, proj_shapes = _proj_specs(rows, tm, a_pitch, cos_t.shape[0] // tm)
    outs = pl.pallas_call(
        kern,
        grid=(steps,),
        in_specs=_combine_specs(tm, d, 0, steps)
                 + [const(mod), const(ln_g), const(ln_b), const(mod_next), const(w_in), table_spec, table_spec,
                    pl.BlockSpec(memory_space=pl.ANY)],
        out_specs=[pl.BlockSpec((tm, d), lambda i: (i, 0))] + proj_specs,
        out_shape=[jax.ShapeDtypeStruct((rows, d), F32)] + proj_shapes,
        scratch_shapes=_combine_scratch(tm),
        compiler_params=_cparams("arbitrary"),
        name="moe_combine_proj",
    )(dest, dest, x1, route, mod, ln_g, ln_b, mod_next, w_in, cos_t, sin_t, y_tiles)
    return outs[0], outs[1:]


def _rope_tables(n_pos):
    rows = n_pos // GRID_W
    row = jnp.repeat(jnp.arange(rows), GRID_W).astype(F32)
    col = jnp.tile(jnp.arange(GRID_W), rows).astype(F32)
    n_freq = HEAD_DIM // 4
    freq = ROPE_BASE ** (-jnp.arange(n_freq, dtype=F32) / n_freq)
    ang_r, ang_c = row[:, None] * freq, col[:, None] * freq
    cos_h = jnp.concatenate([jnp.cos(ang_r)] * 2 + [jnp.cos(ang_c)] * 2, axis=1)
    sin_h = jnp.concatenate([-jnp.sin(ang_r), jnp.sin(ang_r), -jnp.sin(ang_c), jnp.sin(ang_c)], axis=1)
    return jnp.tile(cos_h, (1, 2)), jnp.tile(sin_h, (1, 2))


def _block_table(counts, n_blocks):
    cnt = counts.astype(jnp.int32)
    padded = (cnt + MOE_BLOCK - 1) // MOE_BLOCK * MOE_BLOCK
    pad_end = jnp.cumsum(padded)
    pad_start = pad_end - padded
    blk_start = jnp.arange(n_blocks, dtype=jnp.int32)[:, None] * MOE_BLOCK
    be = jnp.minimum(jnp.sum((pad_end[None, :] <= blk_start).astype(jnp.int32), axis=1), N_EXPERTS - 1)
    ids = jnp.arange(N_EXPERTS, dtype=jnp.int32)
    mine = be[:, None] == ids[None, :]
    fill = jnp.sum(jnp.where(mine, cnt[None, :] + pad_start[None, :], 0), axis=1) - blk_start[:, 0]
    bn = jnp.clip(fill, 0, MOE_BLOCK)
    prev = jnp.concatenate([jnp.full((1,), -1, jnp.int32), be[:-1]])
    first = ((bn > 0) & (be != prev)).astype(jnp.int32)
    slot = (jnp.cumsum(first) - 1) % 2
    later = (ids[None, :] > ids[:, None]) & (cnt[None, :] > 0)
    nxt_e = jnp.min(jnp.where(later, ids[None, :], N_EXPERTS), axis=1)
    nxt_e = jnp.where(nxt_e == N_EXPERTS, -1, nxt_e)
    nxt = jnp.sum(jnp.where(mine, nxt_e[None, :], 0), axis=1)
    return be, bn, first, slot.astype(jnp.int32), nxt.astype(jnp.int32)


def _moe(route_t, h2_tiles, w_gate, w_up, w_down, layer):
    rows = route_t.shape[1]
    tm = ROW_TILE
    n_blocks = -(-(2 * rows) // MOE_BLOCK) + N_EXPERTS
    dest8, counts = _plan(route_t, tm)
    dest = dest8[:, 0:2, :].reshape(rows // tm, 1, 2 * tm)
    table = _block_table(counts[:, 0], n_blocks)
    xs = _dispatch(dest, h2_tiles, n_blocks * MOE_BLOCK, tm)
    ys = _experts(table, xs, w_gate, w_up, w_down, layer)
    return dest, ys


def kernel(x, c, ctx, c_ctx, w_ada, b_ada, w_in, w_fourier, attn_sink, w_pool, pool_scale, w_sgu, b_sgu,
           w_out, ln1_g, ln1_b, w_router_group, w_router_expert, w_exp_gate, w_exp_up, w_exp_down,
           ln2_g, ln2_b):
    b, s, d = x.shape
    n_ctx = ctx.shape[1]
    n_layers = w_in.shape[0]
    tm = ROW_TILE
    cond = jnp.concatenate([c, c_ctx[None, :], jnp.zeros((SUBLANES - b - 1, d), F32)], axis=0)
    mod_all = _ada(cond, w_ada, b_ada[:, None, :])
    cos_t, sin_t = _rope_tables(s)
    x2 = x.reshape(b * s, d)
    c2 = ctx.reshape(b * n_ctx, d)
    n_sgu = w_sgu.shape[1]
    lat = None
    for layer in range(n_layers):
        last = layer == n_layers - 1
        mod = mod_all[layer]
        w_in_l = w_in[layer].astype(BF16)
        wf = w_fourier[layer].astype(BF16)
        w_pool_bd = jax.scipy.linalg.block_diag(*[w_pool[layer, g] for g in range(w_pool.shape[1])]).astype(BF16)
        w_sgu_stack = w_sgu[layer].reshape(n_sgu * SGU_CHUNK, SGU_CHUNK).astype(BF16)
        b_sgu_exp = jnp.repeat(b_sgu[layer].T, GROUP_W // n_sgu, axis=1)
        w_router = jnp.concatenate([w_router_expert[layer].reshape(d, N_EXPERTS), w_router_group[layer]], axis=1)
        w_router = jnp.pad(w_router, ((0, 0), (0, LANES - w_router.shape[1])))
        wr_hi = w_router.astype(BF16)
        wr_lo = (w_router - wr_hi.astype(F32)).astype(BF16)
        merge_consts = (mod, w_pool_bd, pool_scale[layer][None, :], w_sgu_stack, b_sgu_exp,
                        w_out[layer].astype(BF16), ln1_g[layer][None, :], ln1_b[layer][None, :], wr_hi, wr_lo)
        sink = attn_sink[layer]

        if lat is None:
            lat = _proj(x2, mod, w_in_l, cos_t, sin_t, mod_row=None, rows_per_batch=s,
                        rope=True, tm=tm, a_pitch=FFT_PITCH)
        a, q, qs, k, v, p, ug = lat
        ac, qc, qsc, kc, vc, pc, ugc = _proj(c2, mod, w_in_l, cos_t, sin_t, mod_row=b, rows_per_batch=n_ctx,
                                             rope=False, tm=n_ctx, a_pitch=FFT_R)
        y_four = _fourier(a, wf, s)
        y_attn = _attention(sink, q, qs, k, v, kc, vc, seq=s, n_ctx=n_ctx, band=True)
        total = b * s + (0 if last else b * n_ctx)
        merged = _merge(x2, p, ug, y_four, y_attn, *merge_consts, (),
                        mod_row=None, seq=s, tm=tm, row_off=0, total_rows=total)
        if not last:
            yc_four = _fourier_small(ac, wf, n_ctx)
            yc_attn = _attention(sink, qc, qsc, kc, vc, kc, vc, seq=n_ctx, n_ctx=n_ctx, band=False)
            merged = _merge(c2, pc, ugc, yc_four, yc_attn, *merge_consts, tuple(merged),
                            mod_row=b, seq=n_ctx, tm=n_ctx, row_off=b * s, total_rows=total)
        x1, h2_tiles, route, route_t = merged
        dest, ys = _moe(route_t, h2_tiles, w_exp_gate, w_exp_up, w_exp_down, layer)
        ln_g, ln_b = ln2_g[layer][None, :], ln2_b[layer][None, :]
        if last:
            x2 = _combine(dest, x1, route, mod, ln_g, ln_b, ys, tm=tm, row_off=0, rows=b * s,
                          mod_row=None, rows_per_batch=s)
        else:
            x2, lat = _combine_proj(dest, x1, route, mod, ln_g, ln_b, ys, mod_all[layer + 1],
                                    w_in[layer + 1].astype(BF16), cos_t, sin_t,
                                    tm=tm, rows=b * s, rows_per_batch=s, a_pitch=FFT_PITCH)
            c2 = _combine(dest, x1, route, mod, ln_g, ln_b, ys, tm=tm, row_off=b * s, rows=b * n_ctx,
                          mod_row=b, rows_per_batch=n_ctx)
    return x2.reshape(b, s, d)
```

```python
import functools
import math

import numpy as np
import jax
import jax.numpy as jnp
from jax import lax
from jax.experimental import pallas as pl
from jax.experimental.pallas import tpu as pltpu

GRID_W = 64
HEAD_DIM = 64
GROUP_W = 256
KV_W = 128
WINDOW = 128
POOL_WINDOWS = (2, 4, 8, 16)
SGU_CHUNK = 128
N_GROUPS = 4
EXPERTS_PER_GROUP = 8
N_EXPERTS = 32
ROPE_BASE = 10000.0
LN_EPS = 1e-6
NEG_INF = -1e30
DEPTH = 2
RES_ALPHA = (2 * DEPTH) ** 0.25

LANES = 128
SUBLANES = 8
VMEM_LIMIT = 48 * 1024 * 1024

ROW_TILE = 512
Q_BLOCK = 512
MOE_BLOCK = 512
FFT_R = 64
FFT_PITCH = 72
FFT_UNROLL = 8
DMA_UNROLL = 8
MERGE_PART = 256

BF16 = jnp.bfloat16
F32 = jnp.float32


def _cparams(*sem):
    return pltpu.CompilerParams(dimension_semantics=sem, vmem_limit_bytes=VMEM_LIMIT)


def _dot(a, b):
    return jnp.dot(a, b, preferred_element_type=F32)


def _dot_nt(a, b):
    return lax.dot_general(a, b, (((1,), (1,)), ((), ())), preferred_element_type=F32)


def _layer_norm(t):
    mu = jnp.mean(t, axis=-1, keepdims=True)
    d = t - mu
    var = jnp.mean(d * d, axis=-1, keepdims=True)
    return d * lax.rsqrt(var + LN_EPS)


def _silu(t):
    return t * (1.0 / (1.0 + jnp.exp(-t)))


def _gelu(t):
    return 0.5 * t * (1.0 + lax.erf(t * (1.0 / math.sqrt(2.0))))


ROW_WORDS = 4
HI_MASK = 0xFFFF0000


def _pack_rows(t):
    half = t.shape[1] // 2
    lo = lax.bitcast_convert_type(t[:, :half].astype(BF16).astype(F32), jnp.uint32)
    hi = lax.bitcast_convert_type(t[:, half:].astype(BF16).astype(F32), jnp.uint32)
    return (lo >> 16) | (hi & jnp.uint32(HI_MASK))


def _unpack_rows(w):
    return (lax.bitcast_convert_type(w << 16, F32),
            lax.bitcast_convert_type(w & jnp.uint32(HI_MASK), F32))


def _store_rows(ref, t, first=0):
    w = _pack_rows(t)
    for j in range(ROW_WORDS):
        ref[pl.ds(first * ROW_WORDS + j, t.shape[0], stride=ROW_WORDS), :] = w[:, j * LANES:(j + 1) * LANES]


def _load_rows(ref, first, m):
    w = jnp.concatenate([ref[pl.ds(first * ROW_WORDS + j, m, stride=ROW_WORDS), :] for j in range(ROW_WORDS)],
                        axis=1)
    return _unpack_rows(w)


def _ada_kernel(c_ref, w_ref, b_ref, o_ref):
    s = _silu(c_ref[...]).astype(BF16)
    o_ref[...] = _dot(s, w_ref[...].astype(BF16)) + b_ref[...]


def _ada(cond, w_ada, b_ada):
    n_layers, d, n = w_ada.shape
    tn = n // 4
    return pl.pallas_call(
        _ada_kernel,
        grid=(n_layers, n // tn),
        in_specs=[
            pl.BlockSpec((SUBLANES, d), lambda l, j: (0, 0)),
            pl.BlockSpec((None, d, tn), lambda l, j: (l, 0, j)),
            pl.BlockSpec((None, 1, tn), lambda l, j: (l, 0, j)),
        ],
        out_specs=pl.BlockSpec((None, SUBLANES, tn), lambda l, j: (l, 0, j)),
        out_shape=jax.ShapeDtypeStruct((n_layers, SUBLANES, n), F32),
        compiler_params=_cparams("arbitrary", "arbitrary"),
        name="ada",
    )(cond, w_ada, b_ada)


def _rope(t, cos_t, sin_t):
    lane = lax.broadcasted_iota(jnp.int32, t.shape, 1)
    first = (lane % 32) < 16
    partner = jnp.where(first, pltpu.roll(t, LANES - 16, axis=1), pltpu.roll(t, 16, axis=1))
    return t * cos_t + partner * sin_t


def _proj_body(x, m, w_ref, cos_ref, sin_ref, outs, *, rope, a_pitch):
    a_ref, q_ref, qs_ref, k_ref, v_ref, p_ref, ug_ref = outs
    tm, d = x.shape
    shift, scale = m[:, 0:d], m[:, d:2 * d]
    h = _layer_norm(x) * (1.0 + scale) + shift
    z = _dot(h.astype(BF16), w_ref[...])
    pad = jnp.zeros((a_pitch - FFT_R, LANES), F32)
    for g in range(tm // FFT_R):
        for hf in range(2):
            grp = z[g * FFT_R:(g + 1) * FFT_R, hf * LANES:(hf + 1) * LANES]
            if a_pitch > FFT_R:
                grp = jnp.concatenate([grp, pad], axis=0)
            a_ref[hf, g * a_pitch:(g + 1) * a_pitch, :] = grp
    q0, q1 = z[:, 256:384], z[:, 384:512]
    k = z[:, 512:640]
    if rope:
        cos_t, sin_t = cos_ref[...], sin_ref[...]
        q0, q1, k = _rope(q0, cos_t, sin_t), _rope(q1, cos_t, sin_t), _rope(k, cos_t, sin_t)
    q_ref[:, 0:128] = q0.astype(BF16)
    q_ref[:, 128:256] = q1.astype(BF16)
    qs_ref[:, 0:128] = pltpu.roll(q0, HEAD_DIM, axis=1).astype(BF16)
    qs_ref[:, 128:256] = pltpu.roll(q1, HEAD_DIM, axis=1).astype(BF16)
    k_ref[...] = k.astype(BF16)
    v_ref[...] = z[:, 640:768].astype(BF16)
    p_ref[...] = z[:, 768:1024].astype(BF16)
    ug_ref[...] = z[:, 1024:1536].astype(BF16)


def _mod_row(mod_ref, mod_row, tm, rows_per_batch):
    row = (pl.program_id(0) * tm) // rows_per_batch if mod_row is None else mod_row
    return mod_ref[pl.ds(row, 1), :]


def _proj_kernel(x_ref, mod_ref, w_ref, cos_ref, sin_ref, *outs, mod_row, rows_per_batch, rope, a_pitch):
    m = _mod_row(mod_ref, mod_row, x_ref.shape[0], rows_per_batch)
    _proj_body(x_ref[...], m, w_ref, cos_ref, sin_ref, outs, rope=rope, a_pitch=a_pitch)


def _proj_specs(rows, tm, a_pitch, seq_steps):
    row_spec = lambda w: pl.BlockSpec((tm, w), lambda i: (i, 0))
    out_w = (256, 256, 128, 128, 256, 512)
    ta = tm // FFT_R * a_pitch
    out_specs = [pl.BlockSpec((2, ta, LANES), lambda i: (0, i, 0))] + [row_spec(w) for w in out_w]
    out_shape = ([jax.ShapeDtypeStruct((2, rows // FFT_R * a_pitch, LANES), F32)]
                 + [jax.ShapeDtypeStruct((rows, w), BF16) for w in out_w])
    table_spec = pl.BlockSpec((tm, LANES), lambda i: (i % seq_steps, 0))
    return table_spec, out_specs, out_shape


def _proj(x2, mod, w_in, cos_t, sin_t, *, mod_row, rows_per_batch, rope, tm, a_pitch):
    rows, d = x2.shape
    kern = functools.partial(_proj_kernel, mod_row=mod_row, rows_per_batch=rows_per_batch,
                             rope=rope, a_pitch=a_pitch)
    table_spec, out_specs, out_shape = _proj_specs(rows, tm, a_pitch, cos_t.shape[0] // tm)
    return pl.pallas_call(
        kern,
        grid=(rows // tm,),
        in_specs=[
            pl.BlockSpec((tm, d), lambda i: (i, 0)),
            pl.BlockSpec(mod.shape, lambda i: (0, 0)),
            pl.BlockSpec(w_in.shape, lambda i: (0, 0)),
            table_spec, table_spec,
        ],
        out_specs=out_specs,
        out_shape=out_shape,
        compiler_params=_cparams("arbitrary"),
        name="proj",
    )(x2, mod, w_in, cos_t, sin_t)


def _fft_tables(n_pos):
    r = FFT_R
    assert n_pos == r * r
    kb = np.arange(r)[None, :, None]
    na = np.arange(r)[:, None, None]
    nb = np.arange(r)[None, None, :]
    ang = 2.0 * np.pi * ((kb * (na + r * nb)) % n_pos) / n_pos
    m1 = np.concatenate([np.cos(ang), -np.sin(ang)], axis=1)
    ka = np.arange(r)[:, None]
    n2 = np.arange(r)[None, :]
    ang2 = 2.0 * np.pi * ((ka * n2) % r) / r
    c2, s2 = np.cos(ang2), np.sin(ang2)
    w2 = np.block([[c2, s2], [-s2, c2]])
    return m1, w2


def _channel_tables(n_pos):
    h = HEAD_DIM
    c = np.arange(h)
    ang = 2.0 * np.pi * ((c[:, None] * c[None, :]) % h) / h
    scale = 1.0 / math.sqrt(n_pos * h)
    eye = np.eye(GROUP_W // h)
    cc = np.kron(eye, np.cos(ang)) * scale
    ss = np.kron(eye, np.sin(ang)) * scale
    return np.concatenate([cc, ss], axis=0)


def _fourier_kernel(a_ref, m1_ref, w2_ref, ch_ref, wf_ref, o_ref, z_ref, y_ref):
    r, pt = FFT_R, FFT_PITCH

    def step1(i, c):
        for u in range(FFT_UNROLL):
            na = i * FFT_UNROLL + u
            rows = jnp.concatenate([a_ref[0, pl.ds(na, r, stride=pt), :],
                                    a_ref[1, pl.ds(na, r, stride=pt), :]], axis=1)
            z = _dot(m1_ref[na], rows.astype(BF16))
            base = pl.multiple_of(na * pt, SUBLANES)
            z_ref[0, pl.ds(base, r), :] = z[0:r, 0:LANES]
            z_ref[1, pl.ds(base, r), :] = z[0:r, LANES:]
            z_ref[2, pl.ds(base, r), :] = z[r:, 0:LANES]
            z_ref[3, pl.ds(base, r), :] = z[r:, LANES:]
        return c

    lax.fori_loop(0, r // FFT_UNROLL, step1, 0)

    def step2(i, c):
        for u in range(FFT_UNROLL):
            kb = i * FFT_UNROLL + u
            q = [z_ref[j, pl.ds(kb, r, stride=pt), :] for j in range(4)]
            zs = jnp.concatenate([jnp.concatenate(q[0:2], axis=1),
                                  jnp.concatenate(q[2:4], axis=1)], axis=0)
            y = _dot(w2_ref[...], zs.astype(BF16))
            base = pl.multiple_of(kb * r, r)
            y_ref[0, pl.ds(base, r), :] = y[0:r, 0:LANES]
            y_ref[1, pl.ds(base, r), :] = y[0:r, LANES:]
            y_ref[2, pl.ds(base, r), :] = y[r:, 0:LANES]
            y_ref[3, pl.ds(base, r), :] = y[r:, LANES:]
        return c

    lax.fori_loop(0, r // FFT_UNROLL, step2, 0)

    chunk = 8 * r
    for cidx in range(r * r // chunk):
        yy = jnp.concatenate([y_ref[j, cidx * chunk:(cidx + 1) * chunk, :] for j in range(4)], axis=1)
        f = _dot(yy.astype(BF16), ch_ref[...])
        g = _dot(f.astype(BF16), wf_ref[...])
        for gi in range(chunk // r):
            kb = cidx * (chunk // r) + gi
            z_ref[0, kb * pt:kb * pt + r, :] = g[gi * r:(gi + 1) * r, 0:LANES]
            z_ref[1, kb * pt:kb * pt + r, :] = g[gi * r:(gi + 1) * r, LANES:]

    def step3(i, c):
        for u in range(FFT_UNROLL):
            ka = i * FFT_UNROLL + u
            base = pl.multiple_of(ka * r, r)
            o_ref[pl.ds(base, r), 0:LANES] = z_ref[0, pl.ds(ka, r, stride=pt), :]
            o_ref[pl.ds(base, r), LANES:] = z_ref[1, pl.ds(ka, r, stride=pt), :]
        return c

    lax.fori_loop(0, r // FFT_UNROLL, step3, 0)


def _fourier(a3, w_fourier, n_pos):
    rows = a3.shape[1] // FFT_PITCH * FFT_R
    gw = GROUP_W
    m1, w2 = _fft_tables(n_pos)
    ch = _channel_tables(n_pos)
    const = lambda shape: pl.BlockSpec(shape, lambda b: (0,) * len(shape))
    return pl.pallas_call(
        _fourier_kernel,
        grid=(rows // n_pos,),
        in_specs=[
            pl.BlockSpec((2, FFT_R * FFT_PITCH, LANES), lambda b: (0, b, 0)),
            const(m1.shape), const(w2.shape), const(ch.shape), const(w_fourier.shape),
        ],
        out_specs=pl.BlockSpec((n_pos, gw), lambda b: (b, 0)),
        out_shape=jax.ShapeDtypeStruct((rows, gw), F32),
        scratch_shapes=[pltpu.VMEM((4, FFT_R * FFT_PITCH, LANES), F32), pltpu.VMEM((4, n_pos, LANES), F32)],
        compiler_params=_cparams("arbitrary"),
        name="fourier",
    )(a3, jnp.asarray(m1, BF16), jnp.asarray(w2, BF16), jnp.asarray(ch, BF16), w_fourier)


def _fourier_small_kernel(a_ref, cs_ref, ch_ref, wf_ref, o_ref):
    n = a_ref.shape[1]
    a = jnp.concatenate([a_ref[0], a_ref[1]], axis=1)
    pq = _dot(cs_ref[...], a.astype(BF16))
    y = jnp.concatenate([pq[0:n], pq[n:2 * n]], axis=1).astype(BF16)
    f = _dot(y, ch_ref[...])
    o_ref[...] = _dot(f.astype(BF16), wf_ref[...])


def _fourier_small(a3, w_fourier, n_pos):
    _, rows, _ = a3.shape
    gw = GROUP_W
    k = np.arange(n_pos)
    ang = 2.0 * np.pi * ((k[:, None] * k[None, :]) % n_pos) / n_pos
    cs = np.concatenate([np.cos(ang), -np.sin(ang)], axis=0)
    ch = _channel_tables(n_pos)
    const = lambda shape: pl.BlockSpec(shape, lambda b: (0,) * len(shape))
    return pl.pallas_call(
        _fourier_small_kernel,
        grid=(rows // n_pos,),
        in_specs=[pl.BlockSpec((2, n_pos, LANES), lambda b: (0, b, 0)),
                  const(cs.shape), const(ch.shape), const(w_fourier.shape)],
        out_specs=pl.BlockSpec((n_pos, gw), lambda b: (b, 0)),
        out_shape=jax.ShapeDtypeStruct((rows, gw), F32),
        compiler_params=_cparams("arbitrary"),
        name="fourier_ctx",
    )(a3, jnp.asarray(cs, BF16), jnp.asarray(ch, BF16), w_fourier)


ATTN_SUB = 128


def _attn_kernel(sink_ref, q_ref, qs_ref, k_ref, v_ref, kc_ref, vc_ref, o_ref, *, band, seq):
    qb = q_ref.shape[0]
    sub = ATTN_SUB
    lane = lax.broadcasted_iota(jnp.int32, (1, LANES), 1)
    lo_half = lane < HEAD_DIM
    zero = jnp.zeros((), BF16)
    scale = jnp.asarray(HEAD_DIM ** -0.5, BF16)
    kw = sub + 2 * WINDOW
    for sb in range(qb // sub):
        rows = slice(sb * sub, (sb + 1) * sub)
        qa0, qa1 = q_ref[rows, 0:LANES], q_ref[rows, LANES:]
        qs0, qs1 = qs_ref[rows, 0:LANES], qs_ref[rows, LANES:]
        q_all = jnp.concatenate([jnp.where(lo_half, qa0, zero), jnp.where(lo_half, qs0, zero),
                                 jnp.where(lo_half, zero, qs1), jnp.where(lo_half, zero, qa1)], axis=0) * scale
        if band:
            p0 = pl.program_id(1) * qb + sb * sub
            start = pl.multiple_of(jnp.clip(p0 - WINDOW, 0, seq - kw), WINDOW)
            qpos = p0 + lax.broadcasted_iota(jnp.int32, (sub, 1), 0)
            kpos = start + lax.broadcasted_iota(jnp.int32, (1, kw), 1)
            bias = jnp.where(jnp.abs(qpos - kpos) <= WINDOW, 0.0, NEG_INF)
            keys = jnp.concatenate([k_ref[pl.ds(start, kw), :], kc_ref[...]], axis=0)
            vals = jnp.concatenate([v_ref[pl.ds(start, kw), :], vc_ref[...]], axis=0)
        else:
            keys, vals = kc_ref[...], vc_ref[...]
        s_all = _dot_nt(q_all, keys)
        probs, dens = [], []
        for h in range(4):
            s = s_all[h * sub:(h + 1) * sub, :]
            sink = sink_ref[h]
            if band:
                s = jnp.concatenate([s[:, 0:kw] + bias, s[:, kw:]], axis=1)
            m = jnp.maximum(jnp.max(s, axis=1, keepdims=True), sink)
            p = jnp.exp(s - m)
            dens.append(jnp.sum(p, axis=1, keepdims=True) + jnp.exp(sink - m))
            probs.append(p.astype(BF16))
        o_all = _dot(jnp.concatenate(probs, axis=0), vals)
        o = [o_all[h * sub:(h + 1) * sub, :] / dens[h] for h in range(4)]
        o_ref[rows, 0:LANES] = jnp.where(lo_half, o[0], pltpu.roll(o[1], HEAD_DIM, axis=1)).astype(BF16)
        o_ref[rows, LANES:] = jnp.where(lo_half, pltpu.roll(o[2], HEAD_DIM, axis=1), o[3]).astype(BF16)


def _attention(sink, q, qs, k, v, kc, vc, *, seq, n_ctx, band):
    rows = q.shape[0]
    n_batch = rows // seq
    qb = Q_BLOCK if band else seq
    steps = seq // qb
    kern = functools.partial(_attn_kernel, band=band, seq=seq)
    seq_spec = pl.BlockSpec((seq, KV_W), lambda b, i: (b, 0))
    ctx_spec = pl.BlockSpec((n_ctx, KV_W), lambda b, i: (b, 0))
    q_spec = pl.BlockSpec((qb, GROUP_W), lambda b, i: (b * steps + i, 0))
    return pl.pallas_call(
        kern,
        grid=(n_batch, steps),
        in_specs=[pl.BlockSpec(memory_space=pltpu.SMEM), q_spec, q_spec,
                  seq_spec, seq_spec, ctx_spec, ctx_spec],
        out_specs=q_spec,
        out_shape=jax.ShapeDtypeStruct((rows, GROUP_W), BF16),
        compiler_params=_cparams("arbitrary", "arbitrary"),
        name="attn" if band else "attn_ctx",
    )(sink, q, qs, k, v, kc, vc)


POOL_HALO = max(POOL_WINDOWS) // 2


def _pool(p_ref, t0, tm, seq):
    halo = POOL_HALO
    pack = 2 * SUBLANES
    t0 = pl.multiple_of(t0, pack)
    main = p_ref[pl.ds(t0, tm), :].astype(F32)
    lo = pl.multiple_of(jnp.maximum(t0 - pack, 0), pack)
    hi = pl.multiple_of(jnp.minimum(t0 + tm, seq - pack), pack)
    prev = p_ref[pl.ds(lo, pack), :].astype(F32)[pack - halo:, :]
    nxt = p_ref[pl.ds(hi, pack), :].astype(F32)[:halo, :]
    prev = jnp.where(t0 > 0, prev, 0.0)
    nxt = jnp.where(t0 + tm < seq, nxt, 0.0)
    full = jnp.concatenate([prev, main, nxt], axis=0)
    n = tm + 2 * halo
    gch = GROUP_W // len(POOL_WINDOWS)
    first = lax.broadcasted_iota(jnp.int32, (1, LANES), 1) < gch
    means = []
    for hf in range(GROUP_W // LANES):
        wa, wb = POOL_WINDOWS[2 * hf], POOL_WINDOWS[2 * hf + 1]
        x = full[:, hf * LANES:(hf + 1) * LANES]
        sums, w, s = {}, 2, pltpu.roll(x, 1, axis=0) + x
        sums[w] = s
        while w < wb:
            s = pltpu.roll(s, w // 2, axis=0) + pltpu.roll(s, n - w // 2, axis=0)
            w *= 2
            sums[w] = s
        means.append(jnp.where(first, sums[wa] * (1.0 / wa), sums[wb] * (1.0 / wb))[halo:halo + tm, :])
    mean = jnp.concatenate(means, axis=1)
    win = jnp.concatenate([jnp.full((1, gch), w, jnp.int32) for w in POOL_WINDOWS], axis=1)

    def rescale(rows, first_pos):
        pos = first_pos + lax.broadcasted_iota(jnp.int32, (halo, 1), 0)
        cnt = jnp.minimum(pos + win // 2, seq) - jnp.maximum(pos - win // 2, 0)
        return rows * (win.astype(F32) / cnt.astype(F32))

    mean = jnp.concatenate([rescale(mean[:halo], t0), mean[halo:tm - halo],
                            rescale(mean[tm - halo:], t0 + tm - halo)], axis=0)
    return mean - main


def _route(logits):
    tm = logits.shape[0]
    lt = logits.T
    gl = lt[N_EXPERTS:N_EXPERTS + N_GROUPS]
    sub_g = lax.broadcasted_iota(jnp.int32, gl.shape, 0)
    gmax = jnp.max(gl, axis=0, keepdims=True)
    grp = jnp.min(jnp.where(gl == gmax, sub_g, N_GROUPS), axis=0, keepdims=True)
    gate_group = 1.0 / jnp.sum(jnp.exp(gl - gmax), axis=0, keepdims=True)
    el = lt[0:EXPERTS_PER_GROUP]
    for g in range(1, N_GROUPS):
        el = jnp.where(grp == g, lt[g * EXPERTS_PER_GROUP:(g + 1) * EXPERTS_PER_GROUP], el)
    sub = lax.broadcasted_iota(jnp.int32, el.shape, 0)
    m1 = jnp.max(el, axis=0, keepdims=True)
    i1 = jnp.min(jnp.where(el == m1, sub, EXPERTS_PER_GROUP), axis=0, keepdims=True)
    el2 = jnp.where(sub == i1, -jnp.inf, el)
    m2 = jnp.max(el2, axis=0, keepdims=True)
    i2 = jnp.min(jnp.where(el2 == m2, sub, EXPERTS_PER_GROUP), axis=0, keepdims=True)
    r = jnp.exp(m2 - m1)
    g1 = gate_group / (1.0 + r)
    g2 = g1 * r
    e1 = (grp * EXPERTS_PER_GROUP + i1).astype(F32)
    e2 = (grp * EXPERTS_PER_GROUP + i2).astype(F32)
    rows = jnp.where(sub == 0, e1, jnp.where(sub == 1, e2, jnp.where(sub == 2, g1, jnp.where(sub == 3, g2, 0.0))))
    cols = jnp.concatenate([rows, jnp.zeros((LANES - rows.shape[0], tm), F32)], axis=0).T
    return cols, rows


def _merge_kernel(x_ref, p_ref, ug_ref, yf_ref, ya_ref, mod_ref, wpool_ref, pscale_ref, wsgu_ref, bsgu_ref,
                  wout_ref, lng_ref, lnb_ref, wrh_ref, wrl_ref, *rest,
                  mod_row, seq, n_alias):
    x1_ref, h2_ref, route_ref, route_t_ref = rest[n_alias:]
    tm, d = x_ref.shape
    if mod_row is None:
        row = pl.program_id(0)
    else:
        row = mod_row
    t0 = pl.multiple_of(pl.program_id(1) * tm, tm)
    m = mod_ref[pl.ds(row, 1), :]
    gate1, shift2, scale2 = m[:, 2 * d:3 * d], m[:, 3 * d:4 * d], m[:, 4 * d:5 * d]
    lane = lax.broadcasted_iota(jnp.int32, (1, GROUP_W), 1)
    n_heads = wsgu_ref.shape[0] // SGU_CHUNK
    head = lane // (GROUP_W // n_heads)

    pm = min(tm, MERGE_PART)
    for part in range(tm // pm):
        r0 = part * pm
        rows = slice(r0, r0 + pm)
        pooled = _pool(p_ref, t0 + r0, pm, seq)
        y_pool = _dot(pooled.astype(BF16), wpool_ref[...]) * pscale_ref[...]

        ug = ug_ref[rows, :].astype(F32)
        u = _gelu(ug[:, 0:GROUP_W])
        v = _layer_norm(_gelu(ug[:, GROUP_W:])).astype(BF16)
        mixed = []
        for cidx in range(pm // SGU_CHUNK):
            vc = v[cidx * SGU_CHUNK:(cidx + 1) * SGU_CHUNK, :]
            full = _dot(wsgu_ref[...], vc)
            mc = bsgu_ref[...]
            for hd in range(n_heads):
                mc = mc + jnp.where(head == hd, full[hd * SGU_CHUNK:(hd + 1) * SGU_CHUNK, :], 0.0)
            mixed.append(mc)
        y_sgu = u * jnp.concatenate(mixed, axis=0)

        cat = jnp.concatenate([yf_ref[rows, :].astype(BF16), ya_ref[rows, :], y_pool.astype(BF16),
                               y_sgu.astype(BF16)], axis=1)
        y = _dot(cat, wout_ref[...])
        x1 = _layer_norm(RES_ALPHA * x_ref[rows, :] + gate1 * y) * lng_ref[...] + lnb_ref[...]
        x1_ref[rows, :] = x1
        h2 = _layer_norm(x1) * (1.0 + scale2) + shift2
        _store_rows(h2_ref, h2, first=r0)
        hh = h2.astype(BF16)
        hl = (h2 - hh.astype(F32)).astype(BF16)
        logits = _dot(hh, wrh_ref[...]) + (_dot(hh, wrl_ref[...]) + _dot(hl, wrh_ref[...]))
        route_ref[rows, :], route_t_ref[:, rows] = _route(logits)


def _merge(x2, p, ug, y_four, y_attn, mod, w_pool_bd, pool_scale, w_sgu_stack, b_sgu_exp, w_out,
           ln_g, ln_b, wr_hi, wr_lo, aliased, *, mod_row, seq, tm, row_off, total_rows):
    rows, d = x2.shape
    n_batch, steps = rows // seq, seq // tm
    off = row_off // tm
    kern = functools.partial(_merge_kernel, mod_row=mod_row, seq=seq, n_alias=len(aliased))
    row_spec = lambda w: pl.BlockSpec((tm, w), lambda b, i: (b * steps + i, 0))
    const = lambda a: pl.BlockSpec(a.shape, lambda b, i: (0,) * a.ndim)
    consts = (mod, w_pool_bd, pool_scale, w_sgu_stack, b_sgu_exp, w_out, ln_g, ln_b, wr_hi, wr_lo)
    n_in = 5 + len(consts)
    out_shapes = [jax.ShapeDtypeStruct((total_rows, d), F32),
                  jax.ShapeDtypeStruct((total_rows * ROW_WORDS, LANES), jnp.uint32),
                  jax.ShapeDtypeStruct((total_rows, LANES), F32),
                  jax.ShapeDtypeStruct((SUBLANES, total_rows), F32)]
    out_specs = [pl.BlockSpec((tm, d), lambda b, i: (off + b * steps + i, 0)),
                 pl.BlockSpec((tm * ROW_WORDS, LANES), lambda b, i: (off + b * steps + i, 0)),
                 pl.BlockSpec((tm, LANES), lambda b, i: (off + b * steps + i, 0)),
                 pl.BlockSpec((SUBLANES, tm), lambda b, i: (0, off + b * steps + i))]
    return pl.pallas_call(
        kern,
        grid=(n_batch, steps),
        in_specs=[row_spec(d), pl.BlockSpec((seq, GROUP_W), lambda b, i: (b, 0)),
                  row_spec(2 * GROUP_W), row_spec(GROUP_W), row_spec(GROUP_W)]
                 + [const(a) for a in consts]
                 + [pl.BlockSpec(memory_space=pl.ANY)] * len(aliased),
        out_specs=out_specs,
        out_shape=out_shapes,
        input_output_aliases={n_in + k: k for k in range(len(aliased))},
        compiler_params=_cparams("arbitrary", "arbitrary"),
        name="merge",
    )(x2, p, ug, y_four, y_attn, *consts, *aliased)


def _plan_kernel(route_ref, dest_ref, cnt_out_ref, cnt_ref, start_ref, carry_ref):
    ph, t = pl.program_id(0), pl.program_id(1)
    tm = route_ref.shape[1]
    rt = route_ref[...]
    e1 = rt[0:1, :].astype(jnp.int32)
    e2 = rt[1:2, :].astype(jnp.int32)
    sub = lax.broadcasted_iota(jnp.int32, (N_EXPERTS, tm), 0)
    hit1, hit2 = sub == e1, sub == e2
    onehot = jnp.where(hit1 | hit2, 1.0, 0.0)
    tile_cnt = jnp.sum(onehot, axis=1, keepdims=True)

    @pl.when((ph == 0) & (t == 0))
    def _():
        cnt_ref[...] = jnp.zeros_like(cnt_ref)

    @pl.when(ph == 0)
    def _():
        cnt_ref[...] += tile_cnt

    @pl.when((ph == 1) & (t == 0))
    def _():
        cnt = cnt_ref[...]
        padded = jnp.floor((cnt + (MOE_BLOCK - 1.0)) * (1.0 / MOE_BLOCK)) * MOE_BLOCK
        row = lax.broadcasted_iota(jnp.int32, cnt.shape, 0)
        incl = padded
        sh = 1
        while sh < N_EXPERTS:
            incl = incl + jnp.where(row >= sh, pltpu.roll(incl, sh, axis=0), 0.0)
            sh *= 2
        start_ref[...] = incl - padded
        carry_ref[...] = jnp.zeros_like(carry_ref)
        cnt_out_ref[...] = cnt

    @pl.when(ph == 1)
    def _():
        r_i = lax.broadcasted_iota(jnp.int32, (tm, tm), 0)
        c_i = lax.broadcasted_iota(jnp.int32, (tm, tm), 1)
        before = jnp.where(r_i < c_i, 1.0, 0.0).astype(BF16)
        rank = _dot(onehot.astype(BF16), before)
        base = start_ref[:, 0:1] + carry_ref[:, 0:1] + rank
        d1 = jnp.sum(jnp.where(hit1, base, 0.0), axis=0, keepdims=True)
        d2 = jnp.sum(jnp.where(hit2, base, 0.0), axis=0, keepdims=True)
        sub8 = lax.broadcasted_iota(jnp.int32, (SUBLANES, tm), 0)
        dest_ref[...] = jnp.where(sub8 == 0, d1, d2).astype(jnp.int32)
        carry_ref[...] += tile_cnt


def _plan(route_t, tm):
    rows = route_t.shape[1]
    n_t = rows // tm
    return pl.pallas_call(
        _plan_kernel,
        grid=(2, n_t),
        in_specs=[pl.BlockSpec((SUBLANES, tm), lambda ph, t: (0, t))],
        out_specs=[pl.BlockSpec((None, SUBLANES, tm), lambda ph, t: (t * ph, 0, 0)),
                   pl.BlockSpec((N_EXPERTS, LANES), lambda ph, t: (0, 0))],
        out_shape=[jax.ShapeDtypeStruct((n_t, SUBLANES, tm), jnp.int32),
                   jax.ShapeDtypeStruct((N_EXPERTS, LANES), F32)],
        scratch_shapes=[pltpu.VMEM((N_EXPERTS, LANES), F32)] * 3,
        compiler_params=_cparams("arbitrary", "arbitrary"),
        name="moe_plan",
    )(route_t)


def _row_copy(src_ref, src_row, dst_ref, dst_row, sem):
    return pltpu.make_async_copy(
        src_ref.at[pl.ds(pl.multiple_of(src_row * ROW_WORDS, ROW_WORDS), ROW_WORDS)],
        dst_ref.at[pl.ds(pl.multiple_of(dst_row * ROW_WORDS, ROW_WORDS), ROW_WORDS)], sem)


def _dispatch_kernel(dest_ref, h2_ref, xs_ref, sem, *, tm):
    def body(r4, c):
        for u in range(DMA_UNROLL):
            r = r4 * DMA_UNROLL + u
            for k in range(2):
                _row_copy(h2_ref, r, xs_ref, dest_ref[0, 0, k * tm + r], sem).start(priority=k)
        return c

    lax.fori_loop(0, tm // DMA_UNROLL, body, 0)
    for k in range(2):
        pltpu.make_async_copy(h2_ref, xs_ref.at[pl.ds(0, tm * ROW_WORDS)], sem).wait()


def _dispatch(dest, h2_tiles, n_slots, tm):
    n_t = dest.shape[0]
    return pl.pallas_call(
        functools.partial(_dispatch_kernel, tm=tm),
        grid=(n_t,),
        in_specs=[pl.BlockSpec((1, 1, 2 * tm), lambda i: (i, 0, 0), memory_space=pltpu.SMEM),
                  pl.BlockSpec((tm * ROW_WORDS, LANES), lambda i: (i, 0))],
        out_specs=pl.BlockSpec(memory_space=pl.ANY),
        out_shape=jax.ShapeDtypeStruct((n_slots * ROW_WORDS, LANES), jnp.uint32),
        scratch_shapes=[pltpu.SemaphoreType.DMA],
        compiler_params=_cparams("arbitrary"),
        name="moe_dispatch",
    )(dest, h2_tiles)


def _expert_kernel(be_ref, bn_ref, first_ref, slot_ref, nxt_ref, x_ref, wg_hbm, wu_hbm, wd_hbm, y_ref,
                   wg_buf, wu_buf, wd_buf, wg_bf, wu_bf, wd_bf, sem, *, layer):
    i = pl.program_id(0)
    mb = x_ref.shape[0] // ROW_WORDS
    half = wg_bf.shape[0] // 2

    def weight_copies(e, s):
        return [pltpu.make_async_copy(wg_hbm.at[layer, e], wg_buf.at[s], sem.at[s]),
                pltpu.make_async_copy(wu_hbm.at[layer, e], wu_buf.at[s], sem.at[s]),
                pltpu.make_async_copy(wd_hbm.at[layer, e], wd_buf.at[s], sem.at[s])]

    @pl.when(i == 0)
    def _():
        for cp in weight_copies(be_ref[0], 0):
            cp.start()

    @pl.when(first_ref[i] == 1)
    def _():
        s = slot_ref[i]
        for cp in weight_copies(be_ref[i], s):
            cp.wait()

        @pl.when(nxt_ref[i] >= 0)
        def _():
            for cp in weight_copies(nxt_ref[i], 1 - s):
                cp.start()

        wg_bf[...] = wg_buf[s].astype(BF16)
        wu_bf[...] = wu_buf[s].astype(BF16)
        wd_bf[...] = wd_buf[s].astype(BF16)

    @pl.when(bn_ref[i] > 0)
    def _():
        live = lax.broadcasted_iota(jnp.int32, (mb, 1), 0) < bn_ref[i]
        x_lo, x_hi = _load_rows(x_ref, 0, mb)
        x_lo = jnp.where(live, x_lo, 0.0).astype(BF16)
        x_hi = jnp.where(live, x_hi, 0.0).astype(BF16)
        g = _dot(x_lo, wg_bf[0:half, :]) + _dot(x_hi, wg_bf[half:, :])
        u = _dot(x_lo, wu_bf[0:half, :]) + _dot(x_hi, wu_bf[half:, :])
        hid = (_silu(g) * u).astype(BF16)
        _store_rows(y_ref, _dot(hid, wd_bf[...]))


def _experts(table, xs, w_gate, w_up, w_down, layer):
    n_blocks = table[0].shape[0]
    _, _, d, de = w_gate.shape
    blk = pl.BlockSpec((MOE_BLOCK * ROW_WORDS, LANES), lambda i, *_: (i, 0))
    hbm = pl.BlockSpec(memory_space=pl.ANY)
    return pl.pallas_call(
        functools.partial(_expert_kernel, layer=layer),
        grid_spec=pltpu.PrefetchScalarGridSpec(
            num_scalar_prefetch=len(table),
            grid=(n_blocks,),
            in_specs=[blk, hbm, hbm, hbm],
            out_specs=blk,
            scratch_shapes=[pltpu.VMEM((2, d, de), F32), pltpu.VMEM((2, d, de), F32), pltpu.VMEM((2, de, d), F32),
                            pltpu.VMEM((d, de), BF16), pltpu.VMEM((d, de), BF16), pltpu.VMEM((de, d), BF16),
                            pltpu.SemaphoreType.DMA((2,))]),
        out_shape=jax.ShapeDtypeStruct(xs.shape, jnp.uint32),
        compiler_params=_cparams("arbitrary"),
        name="moe_experts",
    )(*table, xs, w_gate, w_up, w_down)


def _combine_body(dest_ref, dest_next_ref, x1_ref, route_ref, m, lng_ref, lnb_ref, y_ref, buf_ref, sem):
    tm, d = x1_ref.shape
    i = pl.program_id(0)
    n = pl.num_programs(0)
    slot = i % 2

    def gather(idx_ref, s):
        def body(r4, c):
            for u in range(DMA_UNROLL):
                r = r4 * DMA_UNROLL + u
                for k in range(2):
                    _row_copy(y_ref, idx_ref[0, 0, k * tm + r], buf_ref.at[s], k * tm + r,
                              sem.at[s]).start(priority=k)
            return c

        lax.fori_loop(0, tm // DMA_UNROLL, body, 0)

    @pl.when(i == 0)
    def _():
        gather(dest_ref, 0)

    @pl.when(i + 1 < n)
    def _():
        gather(dest_next_ref, 1 - slot)

    for k in range(2):
        pltpu.make_async_copy(y_ref.at[pl.ds(0, tm * ROW_WORDS)],
                              buf_ref.at[slot, pl.ds(0, tm * ROW_WORDS)], sem.at[slot]).wait()

    gate2 = m[:, 5 * d:6 * d]
    rt = route_ref[...]
    f = jnp.zeros((tm, d), F32)
    for k in range(2):
        lo, hi = _load_rows(buf_ref.at[slot], k * tm, tm)
        f = f + jnp.concatenate([lo, hi], axis=1) * rt[:, 2 + k:3 + k]
    return _layer_norm(RES_ALPHA * x1_ref[...] + gate2 * f) * lng_ref[...] + lnb_ref[...]


def _combine_kernel(dest_ref, dest_next_ref, x1_ref, route_ref, mod_ref, lng_ref, lnb_ref, y_ref, o_ref,
                    buf_ref, sem, *, mod_row, rows_per_batch):
    m = _mod_row(mod_ref, mod_row, x1_ref.shape[0], rows_per_batch)
    o_ref[...] = _combine_body(dest_ref, dest_next_ref, x1_ref, route_ref, m, lng_ref, lnb_ref, y_ref,
                               buf_ref, sem)


def _combine_proj_kernel(dest_ref, dest_next_ref, x1_ref, route_ref, mod_ref, lng_ref, lnb_ref,
                         mod_next_ref, w_ref, cos_ref, sin_ref, y_ref, o_ref, *rest,
                         rows_per_batch, a_pitch):
    outs, (buf_ref, sem) = rest[:-2], rest[-2:]
    tm = x1_ref.shape[0]
    x = _combine_body(dest_ref, dest_next_ref, x1_ref, route_ref, _mod_row(mod_ref, None, tm, rows_per_batch),
                      lng_ref, lnb_ref, y_ref, buf_ref, sem)
    o_ref[...] = x
    _proj_body(x, _mod_row(mod_next_ref, None, tm, rows_per_batch), w_ref, cos_ref, sin_ref, outs,
               rope=True, a_pitch=a_pitch)


def _combine_specs(tm, d, off, steps):
    return [pl.BlockSpec((1, 1, 2 * tm), lambda i: (off + i, 0, 0), memory_space=pltpu.SMEM),
            pl.BlockSpec((1, 1, 2 * tm), lambda i: (off + jnp.minimum(i + 1, steps - 1), 0, 0),
                         memory_space=pltpu.SMEM),
            pl.BlockSpec((tm, d), lambda i: (off + i, 0)),
            pl.BlockSpec((tm, LANES), lambda i: (off + i, 0))]


def _combine_scratch(tm):
    return [pltpu.VMEM((2, 2 * tm * ROW_WORDS, LANES), jnp.uint32), pltpu.SemaphoreType.DMA((2,))]


def _combine(dest, x1, route, mod, ln_g, ln_b, y_tiles, *, tm, row_off, rows, mod_row, rows_per_batch):
    d = x1.shape[1]
    steps = rows // tm
    kern = functools.partial(_combine_kernel, mod_row=mod_row, rows_per_batch=rows_per_batch)
    const = lambda a: pl.BlockSpec(a.shape, lambda i: (0,) * a.ndim)
    return pl.pallas_call(
        kern,
        grid=(steps,),
        in_specs=_combine_specs(tm, d, row_off // tm, steps)
                 + [const(mod), const(ln_g), const(ln_b), pl.BlockSpec(memory_space=pl.ANY)],
        out_specs=pl.BlockSpec((tm, d), lambda i: (i, 0)),
        out_shape=jax.ShapeDtypeStruct((rows, d), F32),
        scratch_shapes=_combine_scratch(tm),
        compiler_params=_cparams("arbitrary"),
        name="moe_combine",
    )(dest, dest, x1, route, mod, ln_g, ln_b, y_tiles)


def _combine_proj(dest, x1, route, mod, ln_g, ln_b, y_tiles, mod_next, w_in, cos_t, sin_t, *,
                  tm, rows, rows_per_batch, a_pitch):
    d = x1.shape[1]
    steps = rows // tm
    kern = functools.partial(_combine_proj_kernel, rows_per_batch=rows_per_batch, a_pitch=a_pitch)
    const = lambda a: pl.BlockSpec(a.shape, lambda i: (0,) * a.ndim)
    table_spec, proj_specs, proj_shapes = _proj_specs(rows, tm, a_pitch, cos_t.shape[0] // tm)
    outs = pl.pallas_call(
        kern,
        grid=(steps,),
        in_specs=_combine_specs(tm, d, 0, steps)
                 + [const(mod), const(ln_g), const(ln_b), const(mod_next), const(w_in), table_spec, table_spec,
                    pl.BlockSpec(memory_space=pl.ANY)],
        out_specs=[pl.BlockSpec((tm, d), lambda i: (i, 0))] + proj_specs,
        out_shape=[jax.ShapeDtypeStruct((rows, d), F32)] + proj_shapes,
        scratch_shapes=_combine_scratch(tm),
        compiler_params=_cparams("arbitrary"),
        name="moe_combine_proj",
    )(dest, dest, x1, route, mod, ln_g, ln_b, mod_next, w_in, cos_t, sin_t, y_tiles)
    return outs[0], outs[1:]


def _rope_tables(n_pos):
    rows = n_pos // GRID_W
    row = jnp.repeat(jnp.arange(rows), GRID_W).astype(F32)
    col = jnp.tile(jnp.arange(GRID_W), rows).astype(F32)
    n_freq = HEAD_DIM // 4
    freq = ROPE_BASE ** (-jnp.arange(n_freq, dtype=F32) / n_freq)
    ang_r, ang_c = row[:, None] * freq, col[:, None] * freq
    cos_h = jnp.concatenate([jnp.cos(ang_r)] * 2 + [jnp.cos(ang_c)] * 2, axis=1)
    sin_h = jnp.concatenate([-jnp.sin(ang_r), jnp.sin(ang_r), -jnp.sin(ang_c), jnp.sin(ang_c)], axis=1)
    return jnp.tile(cos_h, (1, 2)), jnp.tile(sin_h, (1, 2))


def _block_table(counts, n_blocks):
    cnt = counts.astype(jnp.int32)
    padded = (cnt + MOE_BLOCK - 1) // MOE_BLOCK * MOE_BLOCK
    pad_end = jnp.cumsum(padded)
    pad_start = pad_end - padded
    blk_start = jnp.arange(n_blocks, dtype=jnp.int32)[:, None] * MOE_BLOCK
    be = jnp.minimum(jnp.sum((pad_end[None, :] <= blk_start).astype(jnp.int32), axis=1), N_EXPERTS - 1)
    ids = jnp.arange(N_EXPERTS, dtype=jnp.int32)
    mine = be[:, None] == ids[None, :]
    fill = jnp.sum(jnp.where(mine, cnt[None, :] + pad_start[None, :], 0), axis=1) - blk_start[:, 0]
    bn = jnp.clip(fill, 0, MOE_BLOCK)
    prev = jnp.concatenate([jnp.full((1,), -1, jnp.int32), be[:-1]])
    first = ((bn > 0) & (be != prev)).astype(jnp.int32)
    slot = (jnp.cumsum(first) - 1) % 2
    later = (ids[None, :] > ids[:, None]) & (cnt[None, :] > 0)
    nxt_e = jnp.min(jnp.where(later, ids[None, :], N_EXPERTS), axis=1)
    nxt_e = jnp.where(nxt_e == N_EXPERTS, -1, nxt_e)
    nxt = jnp.sum(jnp.where(mine, nxt_e[None, :], 0), axis=1)
    return be, bn, first, slot.astype(jnp.int32), nxt.astype(jnp.int32)


def _moe(route_t, h2_tiles, w_gate, w_up, w_down, layer):
    rows = route_t.shape[1]
    tm = ROW_TILE
    n_blocks = -(-(2 * rows) // MOE_BLOCK) + N_EXPERTS
    dest8, counts = _plan(route_t, tm)
    dest = dest8[:, 0:2, :].reshape(rows // tm, 1, 2 * tm)
    table = _block_table(counts[:, 0], n_blocks)
    xs = _dispatch(dest, h2_tiles, n_blocks * MOE_BLOCK, tm)
    ys = _experts(table, xs, w_gate, w_up, w_down, layer)
    return dest, ys


def kernel(x, c, ctx, c_ctx, w_ada, b_ada, w_in, w_fourier, attn_sink, w_pool, pool_scale, w_sgu, b_sgu,
           w_out, ln1_g, ln1_b, w_router_group, w_router_expert, w_exp_gate, w_exp_up, w_exp_down,
           ln2_g, ln2_b):
    b, s, d = x.shape
    n_ctx = ctx.shape[1]
    n_layers = w_in.shape[0]
    tm = ROW_TILE
    cond = jnp.concatenate([c, c_ctx[None, :], jnp.zeros((SUBLANES - b - 1, d), F32)], axis=0)
    mod_all = _ada(cond, w_ada, b_ada[:, None, :])
    cos_t, sin_t = _rope_tables(s)
    x2 = x.reshape(b * s, d)
    c2 = ctx.reshape(b * n_ctx, d)
    n_sgu = w_sgu.shape[1]
    lat = None
    for layer in range(n_layers):
        last = layer == n_layers - 1
        mod = mod_all[layer]
        w_in_l = w_in[layer].astype(BF16)
        wf = w_fourier[layer].astype(BF16)
        w_pool_bd = jax.scipy.linalg.block_diag(*[w_pool[layer, g] for g in range(w_pool.shape[1])]).astype(BF16)
        w_sgu_stack = w_sgu[layer].reshape(n_sgu * SGU_CHUNK, SGU_CHUNK).astype(BF16)
        b_sgu_exp = jnp.repeat(b_sgu[layer].T, GROUP_W // n_sgu, axis=1)
        w_router = jnp.concatenate([w_router_expert[layer].reshape(d, N_EXPERTS), w_router_group[layer]], axis=1)
        w_router = jnp.pad(w_router, ((0, 0), (0, LANES - w_router.shape[1])))
        wr_hi = w_router.astype(BF16)
        wr_lo = (w_router - wr_hi.astype(F32)).astype(BF16)
        merge_consts = (mod, w_pool_bd, pool_scale[layer][None, :], w_sgu_stack, b_sgu_exp,
                        w_out[layer].astype(BF16), ln1_g[layer][None, :], ln1_b[layer][None, :], wr_hi, wr_lo)
        sink = attn_sink[layer]

        if lat is None:
            lat = _proj(x2, mod, w_in_l, cos_t, sin_t, mod_row=None, rows_per_batch=s,
                        rope=True, tm=tm, a_pitch=FFT_PITCH)
        a, q, qs, k, v, p, ug = lat
        ac, qc, qsc, kc, vc, pc, ugc = _proj(c2, mod, w_in_l, cos_t, sin_t, mod_row=b, rows_per_batch=n_ctx,
                                             rope=False, tm=n_ctx, a_pitch=FFT_R)
        y_four = _fourier(a, wf, s)
        y_attn = _attention(sink, q, qs, k, v, kc, vc, seq=s, n_ctx=n_ctx, band=True)
        total = b * s + (0 if last else b * n_ctx)
        merged = _merge(x2, p, ug, y_four, y_attn, *merge_consts, (),
                        mod_row=None, seq=s, tm=tm, row_off=0, total_rows=total)
        if not last:
            yc_four = _fourier_small(ac, wf, n_ctx)
            yc_attn = _attention(sink, qc, qsc, kc, vc, kc, vc, seq=n_ctx, n_ctx=n_ctx, band=False)
            merged = _merge(c2, pc, ugc, yc_four, yc_attn, *merge_consts, tuple(merged),
                            mod_row=b, seq=n_ctx, tm=n_ctx, row_off=b * s, total_rows=total)
        x1, h2_tiles, route, route_t = merged
        dest, ys = _moe(route_t, h2_tiles, w_exp_gate, w_exp_up, w_exp_down, layer)
        ln_g, ln_b = ln2_g[layer][None, :], ln2_b[layer][None, :]
        if last:
            x2 = _combine(dest, x1, route, mod, ln_g, ln_b, ys, tm=tm, row_off=0, rows=b * s,
                          mod_row=None, rows_per_batch=s)
        else:
            x2, lat = _combine_proj(dest, x1, route, mod, ln_g, ln_b, ys, mod_all[layer + 1],
                                    w_in[layer + 1].astype(BF16), cos_t, sin_t,
                                    tm=tm, rows=b * s, rows_per_batch=s, a_pitch=FFT_PITCH)
            c2 = _combine(dest, x1, route, mod, ln_g, ln_b, ys, tm=tm, row_off=b * s, rows=b * n_ctx,
                          mod_row=b, rows_per_batch=n_ctx)
    return x2.reshape(b, s, d)
```

```python
import functools
import math

import numpy as np
import jax
import jax.numpy as jnp
from jax import lax
from jax.experimental import pallas as pl
from jax.experimental.pallas import tpu as pltpu

GRID_W = 64
HEAD_DIM = 64
GROUP_W = 256
KV_W = 128
WINDOW = 128
POOL_WINDOWS = (2, 4, 8, 16)
SGU_CHUNK = 128
N_GROUPS = 4
EXPERTS_PER_GROUP = 8
N_EXPERTS = 32
ROPE_BASE = 10000.0
LN_EPS = 1e-6
NEG_INF = -1e30
DEPTH = 2
RES_ALPHA = (2 * DEPTH) ** 0.25

LANES = 128
SUBLANES = 8
VMEM_LIMIT = 48 * 1024 * 1024

ROW_TILE = 512
Q_BLOCK = 512
MOE_BLOCK = 512
FFT_R = 64
FFT_PITCH = 72
FFT_UNROLL = 8
DMA_UNROLL = 8
MERGE_PART = 256

BF16 = jnp.bfloat16
F32 = jnp.float32


def _cparams(*sem):
    return pltpu.CompilerParams(dimension_semantics=sem, vmem_limit_bytes=VMEM_LIMIT)


def _dot(a, b):
    return jnp.dot(a, b, preferred_element_type=F32)


def _dot_nt(a, b):
    return lax.dot_general(a, b, (((1,), (1,)), ((), ())), preferred_element_type=F32)


def _layer_norm(t):
    mu = jnp.mean(t, axis=-1, keepdims=True)
    d = t - mu
    var = jnp.mean(d * d, axis=-1, keepdims=True)
    return d * lax.rsqrt(var + LN_EPS)


def _silu(t):
    return t * (1.0 / (1.0 + jnp.exp(-t)))


def _gelu(t):
    return 0.5 * t * (1.0 + lax.erf(t * (1.0 / math.sqrt(2.0))))


ROW_WORDS = 4
HI_MASK = 0xFFFF0000


def _pack_rows(t):
    half = t.shape[1] // 2
    lo = lax.bitcast_convert_type(t[:, :half].astype(BF16).astype(F32), jnp.uint32)
    hi = lax.bitcast_convert_type(t[:, half:].astype(BF16).astype(F32), jnp.uint32)
    return (lo >> 16) | (hi & jnp.uint32(HI_MASK))


def _unpack_rows(w):
    return (lax.bitcast_convert_type(w << 16, F32),
            lax.bitcast_convert_type(w & jnp.uint32(HI_MASK), F32))


def _store_rows(ref, t, first=0):
    w = _pack_rows(t)
    for j in range(ROW_WORDS):
        ref[pl.ds(first * ROW_WORDS + j, t.shape[0], stride=ROW_WORDS), :] = w[:, j * LANES:(j + 1) * LANES]


def _load_rows(ref, first, m):
    w = jnp.concatenate([ref[pl.ds(first * ROW_WORDS + j, m, stride=ROW_WORDS), :] for j in range(ROW_WORDS)],
                        axis=1)
    return _unpack_rows(w)


def _ada_kernel(c_ref, w_ref, b_ref, o_ref):
    s = _silu(c_ref[...]).astype(BF16)
    o_ref[...] = _dot(s, w_ref[...].astype(BF16)) + b_ref[...]


def _ada(cond, w_ada, b_ada):
    n_layers, d, n = w_ada.shape
    tn = n // 4
    return pl.pallas_call(
        _ada_kernel,
        grid=(n_layers, n // tn),
        in_specs=[
            pl.BlockSpec((SUBLANES, d), lambda l, j: (0, 0)),
            pl.BlockSpec((None, d, tn), lambda l, j: (l, 0, j)),
            pl.BlockSpec((None, 1, tn), lambda l, j: (l, 0, j)),
        ],
        out_specs=pl.BlockSpec((None, SUBLANES, tn), lambda l, j: (l, 0, j)),
        out_shape=jax.ShapeDtypeStruct((n_layers, SUBLANES, n), F32),
        compiler_params=_cparams("arbitrary", "arbitrary"),
        name="ada",
    )(cond, w_ada, b_ada)


def _rope(t, cos_t, sin_t):
    lane = lax.broadcasted_iota(jnp.int32, t.shape, 1)
    first = (lane % 32) < 16
    partner = jnp.where(first, pltpu.roll(t, LANES - 16, axis=1), pltpu.roll(t, 16, axis=1))
    return t * cos_t + partner * sin_t


def _proj_body(x, m, w_ref, cos_ref, sin_ref, outs, *, rope, a_pitch):
    a_ref, q_ref, qs_ref, k_ref, v_ref, p_ref, ug_ref = outs
    tm, d = x.shape
    shift, scale = m[:, 0:d], m[:, d:2 * d]
    h = _layer_norm(x) * (1.0 + scale) + shift
    z = _dot(h.astype(BF16), w_ref[...])
    pad = jnp.zeros((a_pitch - FFT_R, LANES), F32)
    for g in range(tm // FFT_R):
        for hf in range(2):
            grp = z[g * FFT_R:(g + 1) * FFT_R, hf * LANES:(hf + 1) * LANES]
            if a_pitch > FFT_R:
                grp = jnp.concatenate([grp, pad], axis=0)
            a_ref[hf, g * a_pitch:(g + 1) * a_pitch, :] = grp
    q0, q1 = z[:, 256:384], z[:, 384:512]
    k = z[:, 512:640]
    if rope:
        cos_t, sin_t = cos_ref[...], sin_ref[...]
        q0, q1, k = _rope(q0, cos_t, sin_t), _rope(q1, cos_t, sin_t), _rope(k, cos_t, sin_t)
    q_ref[:, 0:128] = q0.astype(BF16)
    q_ref[:, 128:256] = q1.astype(BF16)
    qs_ref[:, 0:128] = pltpu.roll(q0, HEAD_DIM, axis=1).astype(BF16)
    qs_ref[:, 128:256] = pltpu.roll(q1, HEAD_DIM, axis=1).astype(BF16)
    k_ref[...] = k.astype(BF16)
    v_ref[...] = z[:, 640:768].astype(BF16)
    p_ref[...] = z[:, 768:1024].astype(BF16)
    ug_ref[...] = z[:, 1024:1536].astype(BF16)


def _mod_row(mod_ref, mod_row, tm, rows_per_batch):
    row = (pl.program_id(0) * tm) // rows_per_batch if mod_row is None else mod_row
    return mod_ref[pl.ds(row, 1), :]


def _proj_kernel(x_ref, mod_ref, w_ref, cos_ref, sin_ref, *outs, mod_row, rows_per_batch, rope, a_pitch):
    m = _mod_row(mod_ref, mod_row, x_ref.shape[0], rows_per_batch)
    _proj_body(x_ref[...], m, w_ref, cos_ref, sin_ref, outs, rope=rope, a_pitch=a_pitch)


def _proj_specs(rows, tm, a_pitch, seq_steps):
    row_spec = lambda w: pl.BlockSpec((tm, w), lambda i: (i, 0))
    out_w = (256, 256, 128, 128, 256, 512)
    ta = tm // FFT_R * a_pitch
    out_specs = [pl.BlockSpec((2, ta, LANES), lambda i: (0, i, 0))] + [row_spec(w) for w in out_w]
    out_shape = ([jax.ShapeDtypeStruct((2, rows // FFT_R * a_pitch, LANES), F32)]
                 + [jax.ShapeDtypeStruct((rows, w), BF16) for w in out_w])
    table_spec = pl.BlockSpec((tm, LANES), lambda i: (i % seq_steps, 0))
    return table_spec, out_specs, out_shape


def _proj(x2, mod, w_in, cos_t, sin_t, *, mod_row, rows_per_batch, rope, tm, a_pitch):
    rows, d = x2.shape
    kern = functools.partial(_proj_kernel, mod_row=mod_row, rows_per_batch=rows_per_batch,
                             rope=rope, a_pitch=a_pitch)
    table_spec, out_specs, out_shape = _proj_specs(rows, tm, a_pitch, cos_t.shape[0] // tm)
    return pl.pallas_call(
        kern,
        grid=(rows // tm,),
        in_specs=[
            pl.BlockSpec((tm, d), lambda i: (i, 0)),
            pl.BlockSpec(mod.shape, lambda i: (0, 0)),
            pl.BlockSpec(w_in.shape, lambda i: (0, 0)),
            table_spec, table_spec,
        ],
        out_specs=out_specs,
        out_shape=out_shape,
        compiler_params=_cparams("arbitrary"),
        name="proj",
    )(x2, mod, w_in, cos_t, sin_t)


def _fft_tables(n_pos):
    r = FFT_R
    assert n_pos == r * r
    kb = np.arange(r)[None, :, None]
    na = np.arange(r)[:, None, None]
    nb = np.arange(r)[None, None, :]
    ang = 2.0 * np.pi * ((kb * (na + r * nb)) % n_pos) / n_pos
    m1 = np.concatenate([np.cos(ang), -np.sin(ang)], axis=1)
    ka = np.arange(r)[:, None]
    n2 = np.arange(r)[None, :]
    ang2 = 2.0 * np.pi * ((ka * n2) % r) / r
    c2, s2 = np.cos(ang2), np.sin(ang2)
    w2 = np.block([[c2, s2], [-s2, c2]])
    return m1, w2


def _channel_tables(n_pos):
    h = HEAD_DIM
    c = np.arange(h)
    ang = 2.0 * np.pi * ((c[:, None] * c[None, :]) % h) / h
    scale = 1.0 / math.sqrt(n_pos * h)
    eye = np.eye(GROUP_W // h)
    cc = np.kron(eye, np.cos(ang)) * scale
    ss = np.kron(eye, np.sin(ang)) * scale
    return np.concatenate([cc, ss], axis=0)


def _fourier_kernel(a_ref, m1_ref, w2_ref, ch_ref, wf_ref, o_ref, z_ref, y_ref):
    r, pt = FFT_R, FFT_PITCH

    def step1(i, c):
        for u in range(FFT_UNROLL):
            na = i * FFT_UNROLL + u
            rows = jnp.concatenate([a_ref[0, pl.ds(na, r, stride=pt), :],
                                    a_ref[1, pl.ds(na, r, stride=pt), :]], axis=1)
            z = _dot(m1_ref[na], rows.astype(BF16))
            base = pl.multiple_of(na * pt, SUBLANES)
            z_ref[0, pl.ds(base, r), :] = z[0:r, 0:LANES]
            z_ref[1, pl.ds(base, r), :] = z[0:r, LANES:]
            z_ref[2, pl.ds(base, r), :] = z[r:, 0:LANES]
            z_ref[3, pl.ds(base, r), :] = z[r:, LANES:]
        return c

    lax.fori_loop(0, r // FFT_UNROLL, step1, 0)

    def step2(i, c):
        for u in range(FFT_UNROLL):
            kb = i * FFT_UNROLL + u
            q = [z_ref[j, pl.ds(kb, r, stride=pt), :] for j in range(4)]
            zs = jnp.concatenate([jnp.concatenate(q[0:2], axis=1),
                                  jnp.concatenate(q[2:4], axis=1)], axis=0)
            y = _dot(w2_ref[...], zs.astype(BF16))
            base = pl.multiple_of(kb * r, r)
            y_ref[0, pl.ds(base, r), :] = y[0:r, 0:LANES]
            y_ref[1, pl.ds(base, r), :] = y[0:r, LANES:]
            y_ref[2, pl.ds(base, r), :] = y[r:, 0:LANES]
            y_ref[3, pl.ds(base, r), :] = y[r:, LANES:]
        return c

    lax.fori_loop(0, r // FFT_UNROLL, step2, 0)

    chunk = 8 * r
    for cidx in range(r * r // chunk):
        yy = jnp.concatenate([y_ref[j, cidx * chunk:(cidx + 1) * chunk, :] for j in range(4)], axis=1)
        f = _dot(yy.astype(BF16), ch_ref[...])
        g = _dot(f.astype(BF16), wf_ref[...])
        for gi in range(chunk // r):
            kb = cidx * (chunk // r) + gi
            z_ref[0, kb * pt:kb * pt + r, :] = g[gi * r:(gi + 1) * r, 0:LANES]
            z_ref[1, kb * pt:kb * pt + r, :] = g[gi * r:(gi + 1) * r, LANES:]

    def step3(i, c):
        for u in range(FFT_UNROLL):
            ka = i * FFT_UNROLL + u
            base = pl.multiple_of(ka * r, r)
            o_ref[pl.ds(base, r), 0:LANES] = z_ref[0, pl.ds(ka, r, stride=pt), :]
            o_ref[pl.ds(base, r), LANES:] = z_ref[1, pl.ds(ka, r, stride=pt), :]
        return c

    lax.fori_loop(0, r // FFT_UNROLL, step3, 0)


def _fourier(a3, w_fourier, n_pos):
    rows = a3.shape[1] // FFT_PITCH * FFT_R
    gw = GROUP_W
    m1, w2 = _fft_tables(n_pos)
    ch = _channel_tables(n_pos)
    const = lambda shape: pl.BlockSpec(shape, lambda b: (0,) * len(shape))
    return pl.pallas_call(
        _fourier_kernel,
        grid=(rows // n_pos,),
        in_specs=[
            pl.BlockSpec((2, FFT_R * FFT_PITCH, LANES), lambda b: (0, b, 0)),
            const(m1.shape), const(w2.shape), const(ch.shape), const(w_fourier.shape),
        ],
        out_specs=pl.BlockSpec((n_pos, gw), lambda b: (b, 0)),
        out_shape=jax.ShapeDtypeStruct((rows, gw), F32),
        scratch_shapes=[pltpu.VMEM((4, FFT_R * FFT_PITCH, LANES), F32), pltpu.VMEM((4, n_pos, LANES), F32)],
        compiler_params=_cparams("arbitrary"),
        name="fourier",
    )(a3, jnp.asarray(m1, BF16), jnp.asarray(w2, BF16), jnp.asarray(ch, BF16), w_fourier)


def _fourier_small_kernel(a_ref, cs_ref, ch_ref, wf_ref, o_ref):
    n = a_ref.shape[1]
    a = jnp.concatenate([a_ref[0], a_ref[1]], axis=1)
    pq = _dot(cs_ref[...], a.astype(BF16))
    y = jnp.concatenate([pq[0:n], pq[n:2 * n]], axis=1).astype(BF16)
    f = _dot(y, ch_ref[...])
    o_ref[...] = _dot(f.astype(BF16), wf_ref[...])


def _fourier_small(a3, w_fourier, n_pos):
    _, rows, _ = a3.shape
    gw = GROUP_W
    k = np.arange(n_pos)
    ang = 2.0 * np.pi * ((k[:, None] * k[None, :]) % n_pos) / n_pos
    cs = np.concatenate([np.cos(ang), -np.sin(ang)], axis=0)
    ch = _channel_tables(n_pos)
    const = lambda shape: pl.BlockSpec(shape, lambda b: (0,) * len(shape))
    return pl.pallas_call(
        _fourier_small_kernel,
        grid=(rows // n_pos,),
        in_specs=[pl.BlockSpec((2, n_pos, LANES), lambda b: (0, b, 0)),
                  const(cs.shape), const(ch.shape), const(w_fourier.shape)],
        out_specs=pl.BlockSpec((n_pos, gw), lambda b: (b, 0)),
        out_shape=jax.ShapeDtypeStruct((rows, gw), F32),
        compiler_params=_cparams("arbitrary"),
        name="fourier_ctx",
    )(a3, jnp.asarray(cs, BF16), jnp.asarray(ch, BF16), w_fourier)


ATTN_SUB = 128


def _attn_kernel(sink_ref, q_ref, qs_ref, k_ref, v_ref, kc_ref, vc_ref, o_ref, *, band, seq):
    qb = q_ref.shape[0]
    sub = ATTN_SUB
    lane = lax.broadcasted_iota(jnp.int32, (1, LANES), 1)
    lo_half = lane < HEAD_DIM
    zero = jnp.zeros((), BF16)
    scale = jnp.asarray(HEAD_DIM ** -0.5, BF16)
    kw = sub + 2 * WINDOW
    for sb in range(qb // sub):
        rows = slice(sb * sub, (sb + 1) * sub)
        qa0, qa1 = q_ref[rows, 0:LANES], q_ref[rows, LANES:]
        qs0, qs1 = qs_ref[rows, 0:LANES], qs_ref[rows, LANES:]
        q_all = jnp.concatenate([jnp.where(lo_half, qa0, zero), jnp.where(lo_half, qs0, zero),
                                 jnp.where(lo_half, zero, qs1), jnp.where(lo_half, zero, qa1)], axis=0) * scale
        if band:
            p0 = pl.program_id(1) * qb + sb * sub
            start = pl.multiple_of(jnp.clip(p0 - WINDOW, 0, seq - kw), WINDOW)
            qpos = p0 + lax.broadcasted_iota(jnp.int32, (sub, 1), 0)
            kpos = start + lax.broadcasted_iota(jnp.int32, (1, kw), 1)
            bias = jnp.where(jnp.abs(qpos - kpos) <= WINDOW, 0.0, NEG_INF)
            keys = jnp.concatenate([k_ref[pl.ds(start, kw), :], kc_ref[...]], axis=0)
            vals = jnp.concatenate([v_ref[pl.ds(start, kw), :], vc_ref[...]], axis=0)
        else:
            keys, vals = kc_ref[...], vc_ref[...]
        s_all = _dot_nt(q_all, keys)
        probs, dens = [], []
        for h in range(4):
            s = s_all[h * sub:(h + 1) * sub, :]
            sink = sink_ref[h]
            if band:
                s = jnp.concatenate([s[:, 0:kw] + bias, s[:, kw:]], axis=1)
            m = jnp.maximum(jnp.max(s, axis=1, keepdims=True), sink)
            p = jnp.exp(s - m)
            dens.append(jnp.sum(p, axis=1, keepdims=True) + jnp.exp(sink - m))
            probs.append(p.astype(BF16))
        o_all = _dot(jnp.concatenate(probs, axis=0), vals)
        o = [o_all[h * sub:(h + 1) * sub, :] / dens[h] for h in range(4)]
        o_ref[rows, 0:LANES] = jnp.where(lo_half, o[0], pltpu.roll(o[1], HEAD_DIM, axis=1)).astype(BF16)
        o_ref[rows, LANES:] = jnp.where(lo_half, pltpu.roll(o[2], HEAD_DIM, axis=1), o[3]).astype(BF16)


def _attention(sink, q, qs, k, v, kc, vc, *, seq, n_ctx, band):
    rows = q.shape[0]
    n_batch = rows // seq
    qb = Q_BLOCK if band else seq
    steps = seq // qb
    kern = functools.partial(_attn_kernel, band=band, seq=seq)
    seq_spec = pl.BlockSpec((seq, KV_W), lambda b, i: (b, 0))
    ctx_spec = pl.BlockSpec((n_ctx, KV_W), lambda b, i: (b, 0))
    q_spec = pl.BlockSpec((qb, GROUP_W), lambda b, i: (b * steps + i, 0))
    return pl.pallas_call(
        kern,
        grid=(n_batch, steps),
        in_specs=[pl.BlockSpec(memory_space=pltpu.SMEM), q_spec, q_spec,
                  seq_spec, seq_spec, ctx_spec, ctx_spec],
        out_specs=q_spec,
        out_shape=jax.ShapeDtypeStruct((rows, GROUP_W), BF16),
        compiler_params=_cparams("arbitrary", "arbitrary"),
        name="attn" if band else "attn_ctx",
    )(sink, q, qs, k, v, kc, vc)


POOL_HALO = max(POOL_WINDOWS) // 2


def _pool(p_ref, t0, tm, seq):
    halo = POOL_HALO
    pack = 2 * SUBLANES
    t0 = pl.multiple_of(t0, pack)
    main = p_ref[pl.ds(t0, tm), :].astype(F32)
    lo = pl.multiple_of(jnp.maximum(t0 - pack, 0), pack)
    hi = pl.multiple_of(jnp.minimum(t0 + tm, seq - pack), pack)
    prev = p_ref[pl.ds(lo, pack), :].astype(F32)[pack - halo:, :]
    nxt = p_ref[pl.ds(hi, pack), :].astype(F32)[:halo, :]
    prev = jnp.where(t0 > 0, prev, 0.0)
    nxt = jnp.where(t0 + tm < seq, nxt, 0.0)
    full = jnp.concatenate([prev, main, nxt], axis=0)
    n = tm + 2 * halo
    gch = GROUP_W // len(POOL_WINDOWS)
    first = lax.broadcasted_iota(jnp.int32, (1, LANES), 1) < gch
    means = []
    for hf in range(GROUP_W // LANES):
        wa, wb = POOL_WINDOWS[2 * hf], POOL_WINDOWS[2 * hf + 1]
        x = full[:, hf * LANES:(hf + 1) * LANES]
        sums, w, s = {}, 2, pltpu.roll(x, 1, axis=0) + x
        sums[w] = s
        while w < wb:
            s = pltpu.roll(s, w // 2, axis=0) + pltpu.roll(s, n - w // 2, axis=0)
            w *= 2
            sums[w] = s
        means.append(jnp.where(first, sums[wa] * (1.0 / wa), sums[wb] * (1.0 / wb))[halo:halo + tm, :])
    mean = jnp.concatenate(means, axis=1)
    win = jnp.concatenate([jnp.full((1, gch), w, jnp.int32) for w in POOL_WINDOWS], axis=1)

    def rescale(rows, first_pos):
        pos = first_pos + lax.broadcasted_iota(jnp.int32, (halo, 1), 0)
        cnt = jnp.minimum(pos + win // 2, seq) - jnp.maximum(pos - win // 2, 0)
        return rows * (win.astype(F32) / cnt.astype(F32))

    mean = jnp.concatenate([rescale(mean[:halo], t0), mean[halo:tm - halo],
                            rescale(mean[tm - halo:], t0 + tm - halo)], axis=0)
    return mean - main


def _route(logits):
    tm = logits.shape[0]
    lt = logits.T
    gl = lt[N_EXPERTS:N_EXPERTS + N_GROUPS]
    sub_g = lax.broadcasted_iota(jnp.int32, gl.shape, 0)
    gmax = jnp.max(gl, axis=0, keepdims=True)
    grp = jnp.min(jnp.where(gl == gmax, sub_g, N_GROUPS), axis=0, keepdims=True)
    gate_group = 1.0 / jnp.sum(jnp.exp(gl - gmax), axis=0, keepdims=True)
    el = lt[0:EXPERTS_PER_GROUP]
    for g in range(1, N_GROUPS):
        el = jnp.where(grp == g, lt[g * EXPERTS_PER_GROUP:(g + 1) * EXPERTS_PER_GROUP], el)
    sub = lax.broadcasted_iota(jnp.int32, el.shape, 0)
    m1 = jnp.max(el, axis=0, keepdims=True)
    i1 = jnp.min(jnp.where(el == m1, sub, EXPERTS_PER_GROUP), axis=0, keepdims=True)
    el2 = jnp.where(sub == i1, -jnp.inf, el)
    m2 = jnp.max(el2, axis=0, keepdims=True)
    i2 = jnp.min(jnp.where(el2 == m2, sub, EXPERTS_PER_GROUP), axis=0, keepdims=True)
    r = jnp.exp(m2 - m1)
    g1 = gate_group / (1.0 + r)
    g2 = g1 * r
    e1 = (grp * EXPERTS_PER_GROUP + i1).astype(F32)
    e2 = (grp * EXPERTS_PER_GROUP + i2).astype(F32)
    rows = jnp.where(sub == 0, e1, jnp.where(sub == 1, e2, jnp.where(sub == 2, g1, jnp.where(sub == 3, g2, 0.0))))
    cols = jnp.concatenate([rows, jnp.zeros((LANES - rows.shape[0], tm), F32)], axis=0).T
    return cols, rows


def _merge_kernel(x_ref, p_ref, ug_ref, yf_ref, ya_ref, mod_ref, wpool_ref, pscale_ref, wsgu_ref, bsgu_ref,
                  wout_ref, lng_ref, lnb_ref, wr_ref, *rest,
                  mod_row, seq, n_alias):
    x1_ref, h2_ref, route_ref, route_t_ref = rest[n_alias:]
    tm, d = x_ref.shape
    if mod_row is None:
        row = pl.program_id(0)
    else:
        row = mod_row
    t0 = pl.multiple_of(pl.program_id(1) * tm, tm)
    m = mod_ref[pl.ds(row, 1), :]
    gate1, shift2, scale2 = m[:, 2 * d:3 * d], m[:, 3 * d:4 * d], m[:, 4 * d:5 * d]
    lane = lax.broadcasted_iota(jnp.int32, (1, GROUP_W), 1)
    n_heads = wsgu_ref.shape[0] // SGU_CHUNK
    head = lane // (GROUP_W // n_heads)

    pm = min(tm, MERGE_PART)
    for part in range(tm // pm):
        r0 = part * pm
        rows = slice(r0, r0 + pm)
        pooled = _pool(p_ref, t0 + r0, pm, seq)
        y_pool = _dot(pooled.astype(BF16), wpool_ref[...]) * pscale_ref[...]

        ug = ug_ref[rows, :].astype(F32)
        u = _gelu(ug[:, 0:GROUP_W])
        v = _layer_norm(_gelu(ug[:, GROUP_W:])).astype(BF16)
        mixed = []
        for cidx in range(pm // SGU_CHUNK):
            vc = v[cidx * SGU_CHUNK:(cidx + 1) * SGU_CHUNK, :]
            full = _dot(wsgu_ref[...], vc)
            mc = bsgu_ref[...]
            for hd in range(n_heads):
                mc = mc + jnp.where(head == hd, full[hd * SGU_CHUNK:(hd + 1) * SGU_CHUNK, :], 0.0)
            mixed.append(mc)
        y_sgu = u * jnp.concatenate(mixed, axis=0)

        cat = jnp.concatenate([yf_ref[rows, :].astype(BF16), ya_ref[rows, :], y_pool.astype(BF16),
                               y_sgu.astype(BF16)], axis=1)
        y = _dot(cat, wout_ref[...])
        x1 = _layer_norm(RES_ALPHA * x_ref[rows, :] + gate1 * y) * lng_ref[...] + lnb_ref[...]
        x1_ref[rows, :] = x1
        h2 = _layer_norm(x1) * (1.0 + scale2) + shift2
        _store_rows(h2_ref, h2, first=r0)
        lg = _dot(h2.astype(BF16), wr_ref[...])
        route_ref[rows, :], route_t_ref[:, rows] = _route(lg[:, 0:LANES] + lg[:, LANES:])


def _merge(x2, p, ug, y_four, y_attn, mod, w_pool_bd, pool_scale, w_sgu_stack, b_sgu_exp, w_out,
           ln_g, ln_b, w_route, aliased, *, mod_row, seq, tm, row_off, total_rows):
    rows, d = x2.shape
    n_batch, steps = rows // seq, seq // tm
    off = row_off // tm
    kern = functools.partial(_merge_kernel, mod_row=mod_row, seq=seq, n_alias=len(aliased))
    row_spec = lambda w: pl.BlockSpec((tm, w), lambda b, i: (b * steps + i, 0))
    const = lambda a: pl.BlockSpec(a.shape, lambda b, i: (0,) * a.ndim)
    consts = (mod, w_pool_bd, pool_scale, w_sgu_stack, b_sgu_exp, w_out, ln_g, ln_b, w_route)
    n_in = 5 + len(consts)
    out_shapes = [jax.ShapeDtypeStruct((total_rows, d), F32),
                  jax.ShapeDtypeStruct((total_rows * ROW_WORDS, LANES), jnp.uint32),
                  jax.ShapeDtypeStruct((total_rows, LANES), F32),
                  jax.ShapeDtypeStruct((SUBLANES, total_rows), F32)]
    out_specs = [pl.BlockSpec((tm, d), lambda b, i: (off + b * steps + i, 0)),
                 pl.BlockSpec((tm * ROW_WORDS, LANES), lambda b, i: (off + b * steps + i, 0)),
                 pl.BlockSpec((tm, LANES), lambda b, i: (off + b * steps + i, 0)),
                 pl.BlockSpec((SUBLANES, tm), lambda b, i: (0, off + b * steps + i))]
    return pl.pallas_call(
        kern,
        grid=(n_batch, steps),
        in_specs=[row_spec(d), pl.BlockSpec((seq, GROUP_W), lambda b, i: (b, 0)),
                  row_spec(2 * GROUP_W), row_spec(GROUP_W), row_spec(GROUP_W)]
                 + [const(a) for a in consts]
                 + [pl.BlockSpec(memory_space=pl.ANY)] * len(aliased),
        out_specs=out_specs,
        out_shape=out_shapes,
        input_output_aliases={n_in + k: k for k in range(len(aliased))},
        compiler_params=_cparams("arbitrary", "arbitrary"),
        name="merge",
    )(x2, p, ug, y_four, y_attn, *consts, *aliased)


def _plan_kernel(route_ref, dest_ref, cnt_out_ref, cnt_ref, start_ref, carry_ref):
    ph, t = pl.program_id(0), pl.program_id(1)
    tm = route_ref.shape[1]
    rt = route_ref[...]
    e1 = rt[0:1, :].astype(jnp.int32)
    e2 = rt[1:2, :].astype(jnp.int32)
    sub = lax.broadcasted_iota(jnp.int32, (N_EXPERTS, tm), 0)
    hit1, hit2 = sub == e1, sub == e2
    onehot = jnp.where(hit1 | hit2, 1.0, 0.0)
    tile_cnt = jnp.sum(onehot, axis=1, keepdims=True)

    @pl.when((ph == 0) & (t == 0))
    def _():
        cnt_ref[...] = jnp.zeros_like(cnt_ref)

    @pl.when(ph == 0)
    def _():
        cnt_ref[...] += tile_cnt

    @pl.when((ph == 1) & (t == 0))
    def _():
        cnt = cnt_ref[...]
        padded = jnp.floor((cnt + (MOE_BLOCK - 1.0)) * (1.0 / MOE_BLOCK)) * MOE_BLOCK
        row = lax.broadcasted_iota(jnp.int32, cnt.shape, 0)
        incl = padded
        sh = 1
        while sh < N_EXPERTS:
            incl = incl + jnp.where(row >= sh, pltpu.roll(incl, sh, axis=0), 0.0)
            sh *= 2
        start_ref[...] = incl - padded
        carry_ref[...] = jnp.zeros_like(carry_ref)
        cnt_out_ref[...] = cnt

    @pl.when(ph == 1)
    def _():
        r_i = lax.broadcasted_iota(jnp.int32, (tm, tm), 0)
        c_i = lax.broadcasted_iota(jnp.int32, (tm, tm), 1)
        before = jnp.where(r_i < c_i, 1.0, 0.0).astype(BF16)
        rank = _dot(onehot.astype(BF16), before)
        base = start_ref[:, 0:1] + carry_ref[:, 0:1] + rank
        d1 = jnp.sum(jnp.where(hit1, base, 0.0), axis=0, keepdims=True)
        d2 = jnp.sum(jnp.where(hit2, base, 0.0), axis=0, keepdims=True)
        sub8 = lax.broadcasted_iota(jnp.int32, (SUBLANES, tm), 0)
        dest_ref[...] = jnp.where(sub8 == 0, d1, d2).astype(jnp.int32)
        carry_ref[...] += tile_cnt


def _plan(route_t, tm):
    rows = route_t.shape[1]
    n_t = rows // tm
    return pl.pallas_call(
        _plan_kernel,
        grid=(2, n_t),
        in_specs=[pl.BlockSpec((SUBLANES, tm), lambda ph, t: (0, t))],
        out_specs=[pl.BlockSpec((None, SUBLANES, tm), lambda ph, t: (t * ph, 0, 0)),
                   pl.BlockSpec((N_EXPERTS, LANES), lambda ph, t: (0, 0))],
        out_shape=[jax.ShapeDtypeStruct((n_t, SUBLANES, tm), jnp.int32),
                   jax.ShapeDtypeStruct((N_EXPERTS, LANES), F32)],
        scratch_shapes=[pltpu.VMEM((N_EXPERTS, LANES), F32)] * 3,
        compiler_params=_cparams("arbitrary", "arbitrary"),
        name="moe_plan",
    )(route_t)


def _row_copy(src_ref, src_row, dst_ref, dst_row, sem):
    return pltpu.make_async_copy(
        src_ref.at[pl.ds(pl.multiple_of(src_row * ROW_WORDS, ROW_WORDS), ROW_WORDS)],
        dst_ref.at[pl.ds(pl.multiple_of(dst_row * ROW_WORDS, ROW_WORDS), ROW_WORDS)], sem)


def _dispatch_kernel(dest_ref, h2_ref, xs_ref, sem, *, tm):
    def body(r4, c):
        for u in range(DMA_UNROLL):
            r = r4 * DMA_UNROLL + u
            for k in range(2):
                _row_copy(h2_ref, r, xs_ref, dest_ref[0, 0, k * tm + r], sem).start(priority=k)
        return c

    lax.fori_loop(0, tm // DMA_UNROLL, body, 0)
    for k in range(2):
        pltpu.make_async_copy(h2_ref, xs_ref.at[pl.ds(0, tm * ROW_WORDS)], sem).wait()


def _dispatch(dest, h2_tiles, n_slots, tm):
    n_t = dest.shape[0]
    return pl.pallas_call(
        functools.partial(_dispatch_kernel, tm=tm),
        grid=(n_t,),
        in_specs=[pl.BlockSpec((1, 1, 2 * tm), lambda i: (i, 0, 0), memory_space=pltpu.SMEM),
                  pl.BlockSpec((tm * ROW_WORDS, LANES), lambda i: (i, 0))],
        out_specs=pl.BlockSpec(memory_space=pl.ANY),
        out_shape=jax.ShapeDtypeStruct((n_slots * ROW_WORDS, LANES), jnp.uint32),
        scratch_shapes=[pltpu.SemaphoreType.DMA],
        compiler_params=_cparams("arbitrary"),
        name="moe_dispatch",
    )(dest, h2_tiles)


def _expert_kernel(be_ref, bn_ref, first_ref, slot_ref, nxt_ref, x_ref, wg_hbm, wu_hbm, wd_hbm, y_ref,
                   wg_buf, wu_buf, wd_buf, wg_bf, wu_bf, wd_bf, sem, *, layer):
    i = pl.program_id(0)
    mb = x_ref.shape[0] // ROW_WORDS
    half = wg_bf.shape[0] // 2

    def weight_copies(e, s):
        return [pltpu.make_async_copy(wg_hbm.at[layer, e], wg_buf.at[s], sem.at[s]),
                pltpu.make_async_copy(wu_hbm.at[layer, e], wu_buf.at[s], sem.at[s]),
                pltpu.make_async_copy(wd_hbm.at[layer, e], wd_buf.at[s], sem.at[s])]

    @pl.when(i == 0)
    def _():
        for cp in weight_copies(be_ref[0], 0):
            cp.start()

    @pl.when(first_ref[i] == 1)
    def _():
        s = slot_ref[i]
        for cp in weight_copies(be_ref[i], s):
            cp.wait()

        @pl.when(nxt_ref[i] >= 0)
        def _():
            for cp in weight_copies(nxt_ref[i], 1 - s):
                cp.start()

        wg_bf[...] = wg_buf[s].astype(BF16)
        wu_bf[...] = wu_buf[s].astype(BF16)
        wd_bf[...] = wd_buf[s].astype(BF16)

    def compute(m):
        live = lax.broadcasted_iota(jnp.int32, (m, 1), 0) < bn_ref[i]
        x_lo, x_hi = _load_rows(x_ref, 0, m)
        x_lo = jnp.where(live, x_lo, 0.0).astype(BF16)
        x_hi = jnp.where(live, x_hi, 0.0).astype(BF16)
        g = _dot(x_lo, wg_bf[0:half, :]) + _dot(x_hi, wg_bf[half:, :])
        u = _dot(x_lo, wu_bf[0:half, :]) + _dot(x_hi, wu_bf[half:, :])
        hid = (_silu(g) * u).astype(BF16)
        _store_rows(y_ref, _dot(hid, wd_bf[...]))

    @pl.when(bn_ref[i] > mb // 2)
    def _():
        compute(mb)

    @pl.when((bn_ref[i] > 0) & (bn_ref[i] <= mb // 2))
    def _():
        compute(mb // 2)


def _experts(table, xs, w_gate, w_up, w_down, layer):
    n_blocks = table[0].shape[0]
    _, _, d, de = w_gate.shape
    blk = pl.BlockSpec((MOE_BLOCK * ROW_WORDS, LANES), lambda i, *_: (i, 0))
    hbm = pl.BlockSpec(memory_space=pl.ANY)
    return pl.pallas_call(
        functools.partial(_expert_kernel, layer=layer),
        grid_spec=pltpu.PrefetchScalarGridSpec(
            num_scalar_prefetch=len(table),
            grid=(n_blocks,),
            in_specs=[blk, hbm, hbm, hbm],
            out_specs=blk,
            scratch_shapes=[pltpu.VMEM((2, d, de), F32), pltpu.VMEM((2, d, de), F32), pltpu.VMEM((2, de, d), F32),
                            pltpu.VMEM((d, de), BF16), pltpu.VMEM((d, de), BF16), pltpu.VMEM((de, d), BF16),
                            pltpu.SemaphoreType.DMA((2,))]),
        out_shape=jax.ShapeDtypeStruct(xs.shape, jnp.uint32),
        compiler_params=_cparams("arbitrary"),
        name="moe_experts",
    )(*table, xs, w_gate, w_up, w_down)


def _combine_body(dest_ref, dest_next_ref, x1_ref, route_ref, m, lng_ref, lnb_ref, y_ref, buf_ref, sem):
    tm, d = x1_ref.shape
    i = pl.program_id(0)
    n = pl.num_programs(0)
    slot = i % 2

    def gather(idx_ref, s):
        def body(r4, c):
            for u in range(DMA_UNROLL):
                r = r4 * DMA_UNROLL + u
                for k in range(2):
                    _row_copy(y_ref, idx_ref[0, 0, k * tm + r], buf_ref.at[s], k * tm + r,
                              sem.at[s]).start(priority=k)
            return c

        lax.fori_loop(0, tm // DMA_UNROLL, body, 0)

    @pl.when(i == 0)
    def _():
        gather(dest_ref, 0)

    @pl.when(i + 1 < n)
    def _():
        gather(dest_next_ref, 1 - slot)

    for k in range(2):
        pltpu.make_async_copy(y_ref.at[pl.ds(0, tm * ROW_WORDS)],
                              buf_ref.at[slot, pl.ds(0, tm * ROW_WORDS)], sem.at[slot]).wait()

    gate2 = m[:, 5 * d:6 * d]
    rt = route_ref[...]
    f = jnp.zeros((tm, d), F32)
    for k in range(2):
        lo, hi = _load_rows(buf_ref.at[slot], k * tm, tm)
        f = f + jnp.concatenate([lo, hi], axis=1) * rt[:, 2 + k:3 + k]
    return _layer_norm(RES_ALPHA * x1_ref[...] + gate2 * f) * lng_ref[...] + lnb_ref[...]


def _combine_kernel(dest_ref, dest_next_ref, x1_ref, route_ref, mod_ref, lng_ref, lnb_ref, y_ref, o_ref,
                    buf_ref, sem, *, mod_row, rows_per_batch):
    m = _mod_row(mod_ref, mod_row, x1_ref.shape[0], rows_per_batch)
    o_ref[...] = _combine_body(dest_ref, dest_next_ref, x1_ref, route_ref, m, lng_ref, lnb_ref, y_ref,
                               buf_ref, sem)


def _combine_proj_kernel(dest_ref, dest_next_ref, x1_ref, route_ref, mod_ref, lng_ref, lnb_ref,
                         mod_next_ref, w_ref, cos_ref, sin_ref, y_ref, o_ref, *rest,
                         rows_per_batch, a_pitch):
    outs, (buf_ref, sem) = rest[:-2], rest[-2:]
    tm = x1_ref.shape[0]
    x = _combine_body(dest_ref, dest_next_ref, x1_ref, route_ref, _mod_row(mod_ref, None, tm, rows_per_batch),
                      lng_ref, lnb_ref, y_ref, buf_ref, sem)
    o_ref[...] = x
    _proj_body(x, _mod_row(mod_next_ref, None, tm, rows_per_batch), w_ref, cos_ref, sin_ref, outs,
               rope=True, a_pitch=a_pitch)


def _combine_specs(tm, d, off, steps):
    return [pl.BlockSpec((1, 1, 2 * tm), lambda i: (off + i, 0, 0), memory_space=pltpu.SMEM),
            pl.BlockSpec((1, 1, 2 * tm), lambda i: (off + jnp.minimum(i + 1, steps - 1), 0, 0),
                         memory_space=pltpu.SMEM),
            pl.BlockSpec((tm, d), lambda i: (off + i, 0)),
            pl.BlockSpec((tm, LANES), lambda i: (off + i, 0))]


def _combine_scratch(tm):
    return [pltpu.VMEM((2, 2 * tm * ROW_WORDS, LANES), jnp.uint32), pltpu.SemaphoreType.DMA((2,))]


def _combine(dest, x1, route, mod, ln_g, ln_b, y_tiles, *, tm, row_off, rows, mod_row, rows_per_batch):
    d = x1.shape[1]
    steps = rows // tm
    kern = functools.partial(_combine_kernel, mod_row=mod_row, rows_per_batch=rows_per_batch)
    const = lambda a: pl.BlockSpec(a.shape, lambda i: (0,) * a.ndim)
    return pl.pallas_call(
        kern,
        grid=(steps,),
        in_specs=_combine_specs(tm, d, row_off // tm, steps)
                 + [const(mod), const(ln_g), const(ln_b), pl.BlockSpec(memory_space=pl.ANY)],
        out_specs=pl.BlockSpec((tm, d), lambda i: (i, 0)),
        out_shape=jax.ShapeDtypeStruct((rows, d), F32),
        scratch_shapes=_combine_scratch(tm),
        compiler_params=_cparams("arbitrary"),
        name="moe_combine",
    )(dest, dest, x1, route, mod, ln_g, ln_b, y_tiles)


def _combine_proj(dest, x1, route, mod, ln_g, ln_b, y_tiles, mod_next, w_in, cos_t, sin_t, *,
                  tm, rows, rows_per_batch, a_pitch):
    d = x1.shape[1]
    steps = rows // tm
    kern = functools.partial(_combine_proj_kernel, rows_per_batch=rows_per_batch, a_pitch=a_pitch)
    const = lambda a: pl.BlockSpec(a.shape, lambda i: (0,) * a.ndim)
    table_spec, proj_specs, proj_shapes = _proj_specs(rows, tm, a_pitch, cos_t.shape[0] // tm)
    outs = pl.pallas_call(
        kern,
        grid=(steps,),
        in_specs=_combine_specs(tm, d, 0, steps)
                 + [const(mod), const(ln_g), const(ln_b), const(mod_next), const(w_in), table_spec, table_spec,
                    pl.BlockSpec(memory_space=pl.ANY)],
        out_specs=[pl.BlockSpec((tm, d), lambda i: (i, 0))] + proj_specs,
        out_shape=[jax.ShapeDtypeStruct((rows, d), F32)] + proj_shapes,
        scratch_shapes=_combine_scratch(tm),
        compiler_params=_cparams("arbitrary"),
        name="moe_combine_proj",
    )(dest, dest, x1, route, mod, ln_g, ln_b, mod_next, w_in, cos_t, sin_t, y_tiles)
    return outs[0], outs[1:]


def _rope_tables(n_pos):
    rows = n_pos // GRID_W
    row = jnp.repeat(jnp.arange(rows), GRID_W).astype(F32)
    col = jnp.tile(jnp.arange(GRID_W), rows).astype(F32)
    n_freq = HEAD_DIM // 4
    freq = ROPE_BASE ** (-jnp.arange(n_freq, dtype=F32) / n_freq)
    ang_r, ang_c = row[:, None] * freq, col[:, None] * freq
    cos_h = jnp.concatenate([jnp.cos(ang_r)] * 2 + [jnp.cos(ang_c)] * 2, axis=1)
    sin_h = jnp.concatenate([-jnp.sin(ang_r), jnp.sin(ang_r), -jnp.sin(ang_c), jnp.sin(ang_c)], axis=1)
    return jnp.tile(cos_h, (1, 2)), jnp.tile(sin_h, (1, 2))


def _block_table(counts, n_blocks):
    cnt = counts.astype(jnp.int32)
    padded = (cnt + MOE_BLOCK - 1) // MOE_BLOCK * MOE_BLOCK
    pad_end = jnp.cumsum(padded)
    pad_start = pad_end - padded
    blk_start = jnp.arange(n_blocks, dtype=jnp.int32)[:, None] * MOE_BLOCK
    be = jnp.minimum(jnp.sum((pad_end[None, :] <= blk_start).astype(jnp.int32), axis=1), N_EXPERTS - 1)
    ids = jnp.arange(N_EXPERTS, dtype=jnp.int32)
    mine = be[:, None] == ids[None, :]
    fill = jnp.sum(jnp.where(mine, cnt[None, :] + pad_start[None, :], 0), axis=1) - blk_start[:, 0]
    bn = jnp.clip(fill, 0, MOE_BLOCK)
    prev = jnp.concatenate([jnp.full((1,), -1, jnp.int32), be[:-1]])
    first = ((bn > 0) & (be != prev)).astype(jnp.int32)
    slot = (jnp.cumsum(first) - 1) % 2
    later = (ids[None, :] > ids[:, None]) & (cnt[None, :] > 0)
    nxt_e = jnp.min(jnp.where(later, ids[None, :], N_EXPERTS), axis=1)
    nxt_e = jnp.where(nxt_e == N_EXPERTS, -1, nxt_e)
    nxt = jnp.sum(jnp.where(mine, nxt_e[None, :], 0), axis=1)
    return be, bn, first, slot.astype(jnp.int32), nxt.astype(jnp.int32)


def _moe(route_t, h2_tiles, w_gate, w_up, w_down, layer):
    rows = route_t.shape[1]
    tm = ROW_TILE
    n_blocks = -(-(2 * rows) // MOE_BLOCK) + N_EXPERTS
    dest8, counts = _plan(route_t, tm)
    dest = dest8[:, 0:2, :].reshape(rows // tm, 1, 2 * tm)
    table = _block_table(counts[:, 0], n_blocks)
    xs = _dispatch(dest, h2_tiles, n_blocks * MOE_BLOCK, tm)
    ys = _experts(table, xs, w_gate, w_up, w_down, layer)
    return dest, ys


def kernel(x, c, ctx, c_ctx, w_ada, b_ada, w_in, w_fourier, attn_sink, w_pool, pool_scale, w_sgu, b_sgu,
           w_out, ln1_g, ln1_b, w_router_group, w_router_expert, w_exp_gate, w_exp_up, w_exp_down,
           ln2_g, ln2_b):
    b, s, d = x.shape
    n_ctx = ctx.shape[1]
    n_layers = w_in.shape[0]
    tm = ROW_TILE
    cond = jnp.concatenate([c, c_ctx[None, :], jnp.zeros((SUBLANES - b - 1, d), F32)], axis=0)
    mod_all = _ada(cond, w_ada, b_ada[:, None, :])
    cos_t, sin_t = _rope_tables(s)
    x2 = x.reshape(b * s, d)
    c2 = ctx.reshape(b * n_ctx, d)
    n_sgu = w_sgu.shape[1]
    lat = None
    for layer in range(n_layers):
        last = layer == n_layers - 1
        mod = mod_all[layer]
        w_in_l = w_in[layer].astype(BF16)
        wf = w_fourier[layer].astype(BF16)
        w_pool_bd = jax.scipy.linalg.block_diag(*[w_pool[layer, g] for g in range(w_pool.shape[1])]).astype(BF16)
        w_sgu_stack = w_sgu[layer].reshape(n_sgu * SGU_CHUNK, SGU_CHUNK).astype(BF16)
        b_sgu_exp = jnp.repeat(b_sgu[layer].T, GROUP_W // n_sgu, axis=1)
        w_router = jnp.concatenate([w_router_expert[layer].reshape(d, N_EXPERTS), w_router_group[layer]], axis=1)
        w_router = jnp.pad(w_router, ((0, 0), (0, LANES - w_router.shape[1])))
        wr_hi = w_router.astype(BF16)
        w_route = jnp.concatenate([wr_hi, (w_router - wr_hi.astype(F32)).astype(BF16)], axis=1)
        merge_consts = (mod, w_pool_bd, pool_scale[layer][None, :], w_sgu_stack, b_sgu_exp,
                        w_out[layer].astype(BF16), ln1_g[layer][None, :], ln1_b[layer][None, :], w_route)
        sink = attn_sink[layer]

        if lat is None:
            lat = _proj(x2, mod, w_in_l, cos_t, sin_t, mod_row=None, rows_per_batch=s,
                        rope=True, tm=tm, a_pitch=FFT_PITCH)
        a, q, qs, k, v, p, ug = lat
        ac, qc, qsc, kc, vc, pc, ugc = _proj(c2, mod, w_in_l, cos_t, sin_t, mod_row=b, rows_per_batch=n_ctx,
                                             rope=False, tm=n_ctx, a_pitch=FFT_R)
        y_four = _fourier(a, wf, s)
        y_attn = _attention(sink, q, qs, k, v, kc, vc, seq=s, n_ctx=n_ctx, band=True)
        total = b * s + (0 if last else b * n_ctx)
        merged = _merge(x2, p, ug, y_four, y_attn, *merge_consts, (),
                        mod_row=None, seq=s, tm=tm, row_off=0, total_rows=total)
        if not last:
            yc_four = _fourier_small(ac, wf, n_ctx)
            yc_attn = _attention(sink, qc, qsc, kc, vc, kc, vc, seq=n_ctx, n_ctx=n_ctx, band=False)
            merged = _merge(c2, pc, ugc, yc_four, yc_attn, *merge_consts, tuple(merged),
                            mod_row=b, seq=n_ctx, tm=n_ctx, row_off=b * s, total_rows=total)
        x1, h2_tiles, route, route_t = merged
        dest, ys = _moe(route_t, h2_tiles, w_exp_gate, w_exp_up, w_exp_down, layer)
        ln_g, ln_b = ln2_g[layer][None, :], ln2_b[layer][None, :]
        if last:
            x2 = _combine(dest, x1, route, mod, ln_g, ln_b, ys, tm=tm, row_off=0, rows=b * s,
                          mod_row=None, rows_per_batch=s)
        else:
            x2, lat = _combine_proj(dest, x1, route, mod, ln_g, ln_b, ys, mod_all[layer + 1],
                                    w_in[layer + 1].astype(BF16), cos_t, sin_t,
                                    tm=tm, rows=b * s, rows_per_batch=s, a_pitch=FFT_PITCH)
            c2 = _combine(dest, x1, route, mod, ln_g, ln_b, ys, tm=tm, row_off=b * s, rows=b * n_ctx,
                          mod_row=b, rows_per_batch=n_ctx)
    return x2.reshape(b, s, d)
```

```python
import functools
import math

import numpy as np
import jax
import jax.numpy as jnp
from jax import lax
from jax.experimental import pallas as pl
from jax.experimental.pallas import tpu as pltpu

GRID_W = 64
HEAD_DIM = 64
GROUP_W = 256
KV_W = 128
WINDOW = 128
POOL_WINDOWS = (2, 4, 8, 16)
SGU_CHUNK = 128
N_GROUPS = 4
EXPERTS_PER_GROUP = 8
N_EXPERTS = 32
ROPE_BASE = 10000.0
LN_EPS = 1e-6
NEG_INF = -1e30
DEPTH = 2
RES_ALPHA = (2 * DEPTH) ** 0.25

LANES = 128
SUBLANES = 8
VMEM_LIMIT = 48 * 1024 * 1024

ROW_TILE = 1024
Q_BLOCK = 1024
MOE_BLOCK = 1024
MOE_PART = 256
FFT_R = 64
FFT_PITCH = 72
FFT_UNROLL = 8
DMA_UNROLL = 8
MERGE_PART = 256

BF16 = jnp.bfloat16
F32 = jnp.float32


def _cparams(*sem):
    return pltpu.CompilerParams(dimension_semantics=sem, vmem_limit_bytes=VMEM_LIMIT)


def _dot(a, b):
    return jnp.dot(a, b, preferred_element_type=F32)


def _dot_nt(a, b):
    return lax.dot_general(a, b, (((1,), (1,)), ((), ())), preferred_element_type=F32)


def _layer_norm(t):
    mu = jnp.mean(t, axis=-1, keepdims=True)
    d = t - mu
    var = jnp.mean(d * d, axis=-1, keepdims=True)
    return d * lax.rsqrt(var + LN_EPS)


def _silu(t):
    return t * (1.0 / (1.0 + jnp.exp(-t)))


def _gelu(t):
    return 0.5 * t * (1.0 + lax.erf(t * (1.0 / math.sqrt(2.0))))


ROW_WORDS = 4
HI_MASK = 0xFFFF0000


def _pack_rows(t):
    half = t.shape[1] // 2
    lo = lax.bitcast_convert_type(t[:, :half].astype(BF16).astype(F32), jnp.uint32)
    hi = lax.bitcast_convert_type(t[:, half:].astype(BF16).astype(F32), jnp.uint32)
    return (lo >> 16) | (hi & jnp.uint32(HI_MASK))


def _unpack_rows(w):
    return (lax.bitcast_convert_type(w << 16, F32),
            lax.bitcast_convert_type(w & jnp.uint32(HI_MASK), F32))


def _store_rows(ref, t, first=0):
    w = _pack_rows(t)
    for j in range(ROW_WORDS):
        ref[pl.ds(first * ROW_WORDS + j, t.shape[0], stride=ROW_WORDS), :] = w[:, j * LANES:(j + 1) * LANES]


def _load_rows(ref, first, m):
    w = jnp.concatenate([ref[pl.ds(first * ROW_WORDS + j, m, stride=ROW_WORDS), :] for j in range(ROW_WORDS)],
                        axis=1)
    return _unpack_rows(w)


def _ada_kernel(c_ref, w_ref, b_ref, o_ref):
    s = _silu(c_ref[...]).astype(BF16)
    o_ref[...] = _dot(s, w_ref[...].astype(BF16)) + b_ref[...]


def _ada(cond, w_ada, b_ada):
    n_layers, d, n = w_ada.shape
    tn = n // 4
    return pl.pallas_call(
        _ada_kernel,
        grid=(n_layers, n // tn),
        in_specs=[
            pl.BlockSpec((SUBLANES, d), lambda l, j: (0, 0)),
            pl.BlockSpec((None, d, tn), lambda l, j: (l, 0, j)),
            pl.BlockSpec((None, 1, tn), lambda l, j: (l, 0, j)),
        ],
        out_specs=pl.BlockSpec((None, SUBLANES, tn), lambda l, j: (l, 0, j)),
        out_shape=jax.ShapeDtypeStruct((n_layers, SUBLANES, n), F32),
        compiler_params=_cparams("arbitrary", "arbitrary"),
        name="ada",
    )(cond, w_ada, b_ada)


def _rope(t, cos_t, sin_t):
    lane = lax.broadcasted_iota(jnp.int32, t.shape, 1)
    first = (lane % 32) < 16
    partner = jnp.where(first, pltpu.roll(t, LANES - 16, axis=1), pltpu.roll(t, 16, axis=1))
    return t * cos_t + partner * sin_t


def _proj_body(x, m, w_ref, cos_ref, sin_ref, outs, *, rope, a_pitch):
    a_ref, q_ref, qs_ref, k_ref, v_ref, p_ref, ug_ref = outs
    tm, d = x.shape
    shift, scale = m[:, 0:d], m[:, d:2 * d]
    h = _layer_norm(x) * (1.0 + scale) + shift
    z = _dot(h.astype(BF16), w_ref[...])
    pad = jnp.zeros((a_pitch - FFT_R, LANES), F32)
    for g in range(tm // FFT_R):
        for hf in range(2):
            grp = z[g * FFT_R:(g + 1) * FFT_R, hf * LANES:(hf + 1) * LANES]
            if a_pitch > FFT_R:
                grp = jnp.concatenate([grp, pad], axis=0)
            a_ref[hf, g * a_pitch:(g + 1) * a_pitch, :] = grp
    q0, q1 = z[:, 256:384], z[:, 384:512]
    k = z[:, 512:640]
    if rope:
        cos_t, sin_t = cos_ref[...], sin_ref[...]
        q0, q1, k = _rope(q0, cos_t, sin_t), _rope(q1, cos_t, sin_t), _rope(k, cos_t, sin_t)
    q_ref[:, 0:128] = q0.astype(BF16)
    q_ref[:, 128:256] = q1.astype(BF16)
    qs_ref[:, 0:128] = pltpu.roll(q0, HEAD_DIM, axis=1).astype(BF16)
    qs_ref[:, 128:256] = pltpu.roll(q1, HEAD_DIM, axis=1).astype(BF16)
    k_ref[...] = k.astype(BF16)
    v_ref[...] = z[:, 640:768].astype(BF16)
    p_ref[...] = z[:, 768:1024].astype(BF16)
    ug_ref[...] = z[:, 1024:1536].astype(BF16)


def _mod_row(mod_ref, mod_row, tm, rows_per_batch):
    row = (pl.program_id(0) * tm) // rows_per_batch if mod_row is None else mod_row
    return mod_ref[pl.ds(row, 1), :]


def _proj_kernel(x_ref, mod_ref, w_ref, cos_ref, sin_ref, *outs, mod_row, rows_per_batch, rope, a_pitch):
    m = _mod_row(mod_ref, mod_row, x_ref.shape[0], rows_per_batch)
    _proj_body(x_ref[...], m, w_ref, cos_ref, sin_ref, outs, rope=rope, a_pitch=a_pitch)


def _proj_specs(rows, tm, a_pitch, seq_steps):
    row_spec = lambda w: pl.BlockSpec((tm, w), lambda i: (i, 0))
    out_w = (256, 256, 128, 128, 256, 512)
    ta = tm // FFT_R * a_pitch
    out_specs = [pl.BlockSpec((2, ta, LANES), lambda i: (0, i, 0))] + [row_spec(w) for w in out_w]
    out_shape = ([jax.ShapeDtypeStruct((2, rows // FFT_R * a_pitch, LANES), F32)]
                 + [jax.ShapeDtypeStruct((rows, w), BF16) for w in out_w])
    table_spec = pl.BlockSpec((tm, LANES), lambda i: (i % seq_steps, 0))
    return table_spec, out_specs, out_shape


def _proj(x2, mod, w_in, cos_t, sin_t, *, mod_row, rows_per_batch, rope, tm, a_pitch):
    rows, d = x2.shape
    kern = functools.partial(_proj_kernel, mod_row=mod_row, rows_per_batch=rows_per_batch,
                             rope=rope, a_pitch=a_pitch)
    table_spec, out_specs, out_shape = _proj_specs(rows, tm, a_pitch, cos_t.shape[0] // tm)
    return pl.pallas_call(
        kern,
        grid=(rows // tm,),
        in_specs=[
            pl.BlockSpec((tm, d), lambda i: (i, 0)),
            pl.BlockSpec(mod.shape, lambda i: (0, 0)),
            pl.BlockSpec(w_in.shape, lambda i: (0, 0)),
            table_spec, table_spec,
        ],
        out_specs=out_specs,
        out_shape=out_shape,
        compiler_params=_cparams("arbitrary"),
        name="proj",
    )(x2, mod, w_in, cos_t, sin_t)


def _fft_tables(n_pos):
    r = FFT_R
    assert n_pos == r * r
    kb = np.arange(r)[None, :, None]
    na = np.arange(r)[:, None, None]
    nb = np.arange(r)[None, None, :]
    ang = 2.0 * np.pi * ((kb * (na + r * nb)) % n_pos) / n_pos
    m1 = np.concatenate([np.cos(ang), -np.sin(ang)], axis=1)
    ka = np.arange(r)[:, None]
    n2 = np.arange(r)[None, :]
    ang2 = 2.0 * np.pi * ((ka * n2) % r) / r
    c2, s2 = np.cos(ang2), np.sin(ang2)
    w2 = np.block([[c2, s2], [-s2, c2]])
    return m1, w2


def _channel_tables(n_pos):
    h = HEAD_DIM
    c = np.arange(h)
    ang = 2.0 * np.pi * ((c[:, None] * c[None, :]) % h) / h
    scale = 1.0 / math.sqrt(n_pos * h)
    eye = np.eye(GROUP_W // h)
    cc = np.kron(eye, np.cos(ang)) * scale
    ss = np.kron(eye, np.sin(ang)) * scale
    return np.concatenate([cc, ss], axis=0)


def _fourier_kernel(a_ref, m1_ref, w2_ref, ch_ref, wf_ref, o_ref, z_ref, y_ref):
    r, pt = FFT_R, FFT_PITCH

    def step1(i, c):
        for u in range(FFT_UNROLL):
            na = i * FFT_UNROLL + u
            rows = jnp.concatenate([a_ref[0, pl.ds(na, r, stride=pt), :],
                                    a_ref[1, pl.ds(na, r, stride=pt), :]], axis=1)
            z = _dot(m1_ref[na], rows.astype(BF16))
            base = pl.multiple_of(na * pt, SUBLANES)
            z_ref[0, pl.ds(base, r), :] = z[0:r, 0:LANES]
            z_ref[1, pl.ds(base, r), :] = z[0:r, LANES:]
            z_ref[2, pl.ds(base, r), :] = z[r:, 0:LANES]
            z_ref[3, pl.ds(base, r), :] = z[r:, LANES:]
        return c

    lax.fori_loop(0, r // FFT_UNROLL, step1, 0)

    def step2(i, c):
        for u in range(FFT_UNROLL):
            kb = i * FFT_UNROLL + u
            q = [z_ref[j, pl.ds(kb, r, stride=pt), :] for j in range(4)]
            zs = jnp.concatenate([jnp.concatenate(q[0:2], axis=1),
                                  jnp.concatenate(q[2:4], axis=1)], axis=0)
            y = _dot(w2_ref[...], zs.astype(BF16))
            base = pl.multiple_of(kb * r, r)
            y_ref[0, pl.ds(base, r), :] = y[0:r, 0:LANES]
            y_ref[1, pl.ds(base, r), :] = y[0:r, LANES:]
            y_ref[2, pl.ds(base, r), :] = y[r:, 0:LANES]
            y_ref[3, pl.ds(base, r), :] = y[r:, LANES:]
        return c

    lax.fori_loop(0, r // FFT_UNROLL, step2, 0)

    chunk = 8 * r
    for cidx in range(r * r // chunk):
        yy = jnp.concatenate([y_ref[j, cidx * chunk:(cidx + 1) * chunk, :] for j in range(4)], axis=1)
        f = _dot(yy.astype(BF16), ch_ref[...])
        g = _dot(f.astype(BF16), wf_ref[...])
        for gi in range(chunk // r):
            kb = cidx * (chunk // r) + gi
            z_ref[0, kb * pt:kb * pt + r, :] = g[gi * r:(gi + 1) * r, 0:LANES]
            z_ref[1, kb * pt:kb * pt + r, :] = g[gi * r:(gi + 1) * r, LANES:]

    def step3(i, c):
        for u in range(FFT_UNROLL):
            ka = i * FFT_UNROLL + u
            base = pl.multiple_of(ka * r, r)
            o_ref[pl.ds(base, r), 0:LANES] = z_ref[0, pl.ds(ka, r, stride=pt), :]
            o_ref[pl.ds(base, r), LANES:] = z_ref[1, pl.ds(ka, r, stride=pt), :]
        return c

    lax.fori_loop(0, r // FFT_UNROLL, step3, 0)


def _fourier(a3, w_fourier, n_pos):
    rows = a3.shape[1] // FFT_PITCH * FFT_R
    gw = GROUP_W
    m1, w2 = _fft_tables(n_pos)
    ch = _channel_tables(n_pos)
    const = lambda shape: pl.BlockSpec(shape, lambda b: (0,) * len(shape))
    return pl.pallas_call(
        _fourier_kernel,
        grid=(rows // n_pos,),
        in_specs=[
            pl.BlockSpec((2, FFT_R * FFT_PITCH, LANES), lambda b: (0, b, 0)),
            const(m1.shape), const(w2.shape), const(ch.shape), const(w_fourier.shape),
        ],
        out_specs=pl.BlockSpec((n_pos, gw), lambda b: (b, 0)),
        out_shape=jax.ShapeDtypeStruct((rows, gw), F32),
        scratch_shapes=[pltpu.VMEM((4, FFT_R * FFT_PITCH, LANES), F32), pltpu.VMEM((4, n_pos, LANES), F32)],
        compiler_params=_cparams("arbitrary"),
        name="fourier",
    )(a3, jnp.asarray(m1, BF16), jnp.asarray(w2, BF16), jnp.asarray(ch, BF16), w_fourier)


def _fourier_small_kernel(a_ref, cs_ref, ch_ref, wf_ref, o_ref):
    n = a_ref.shape[1]
    a = jnp.concatenate([a_ref[0], a_ref[1]], axis=1)
    pq = _dot(cs_ref[...], a.astype(BF16))
    y = jnp.concatenate([pq[0:n], pq[n:2 * n]], axis=1).astype(BF16)
    f = _dot(y, ch_ref[...])
    o_ref[...] = _dot(f.astype(BF16), wf_ref[...])


def _fourier_small(a3, w_fourier, n_pos):
    _, rows, _ = a3.shape
    gw = GROUP_W
    k = np.arange(n_pos)
    ang = 2.0 * np.pi * ((k[:, None] * k[None, :]) % n_pos) / n_pos
    cs = np.concatenate([np.cos(ang), -np.sin(ang)], axis=0)
    ch = _channel_tables(n_pos)
    const = lambda shape: pl.BlockSpec(shape, lambda b: (0,) * len(shape))
    return pl.pallas_call(
        _fourier_small_kernel,
        grid=(rows // n_pos,),
        in_specs=[pl.BlockSpec((2, n_pos, LANES), lambda b: (0, b, 0)),
                  const(cs.shape), const(ch.shape), const(w_fourier.shape)],
        out_specs=pl.BlockSpec((n_pos, gw), lambda b: (b, 0)),
        out_shape=jax.ShapeDtypeStruct((rows, gw), F32),
        compiler_params=_cparams("arbitrary"),
        name="fourier_ctx",
    )(a3, jnp.asarray(cs, BF16), jnp.asarray(ch, BF16), w_fourier)


ATTN_SUB = 128


def _attn_kernel(sink_ref, q_ref, qs_ref, k_ref, v_ref, kc_ref, vc_ref, o_ref, *, band, seq):
    qb = q_ref.shape[0]
    sub = ATTN_SUB
    lane = lax.broadcasted_iota(jnp.int32, (1, LANES), 1)
    lo_half = lane < HEAD_DIM
    zero = jnp.zeros((), BF16)
    scale = jnp.asarray(HEAD_DIM ** -0.5, BF16)
    kw = sub + 2 * WINDOW
    for sb in range(qb // sub):
        rows = slice(sb * sub, (sb + 1) * sub)
        qa0, qa1 = q_ref[rows, 0:LANES], q_ref[rows, LANES:]
        qs0, qs1 = qs_ref[rows, 0:LANES], qs_ref[rows, LANES:]
        q_all = jnp.concatenate([jnp.where(lo_half, qa0, zero), jnp.where(lo_half, qs0, zero),
                                 jnp.where(lo_half, zero, qs1), jnp.where(lo_half, zero, qa1)], axis=0) * scale
        if band:
            p0 = pl.program_id(1) * qb + sb * sub
            start = pl.multiple_of(jnp.clip(p0 - WINDOW, 0, seq - kw), WINDOW)
            qpos = p0 + lax.broadcasted_iota(jnp.int32, (sub, 1), 0)
            kpos = start + lax.broadcasted_iota(jnp.int32, (1, kw), 1)
            bias = jnp.where(jnp.abs(qpos - kpos) <= WINDOW, 0.0, NEG_INF)
            keys = jnp.concatenate([k_ref[pl.ds(start, kw), :], kc_ref[...]], axis=0)
            vals = jnp.concatenate([v_ref[pl.ds(start, kw), :], vc_ref[...]], axis=0)
        else:
            keys, vals = kc_ref[...], vc_ref[...]
        s_all = _dot_nt(q_all, keys)
        probs, dens = [], []
        for h in range(4):
            s = s_all[h * sub:(h + 1) * sub, :]
            sink = sink_ref[h]
            if band:
                s = jnp.concatenate([s[:, 0:kw] + bias, s[:, kw:]], axis=1)
            m = jnp.maximum(jnp.max(s, axis=1, keepdims=True), sink)
            p = jnp.exp(s - m)
            dens.append(jnp.sum(p, axis=1, keepdims=True) + jnp.exp(sink - m))
            probs.append(p.astype(BF16))
        o_all = _dot(jnp.concatenate(probs, axis=0), vals)
        o = [o_all[h * sub:(h + 1) * sub, :] / dens[h] for h in range(4)]
        o_ref[rows, 0:LANES] = jnp.where(lo_half, o[0], pltpu.roll(o[1], HEAD_DIM, axis=1)).astype(BF16)
        o_ref[rows, LANES:] = jnp.where(lo_half, pltpu.roll(o[2], HEAD_DIM, axis=1), o[3]).astype(BF16)


def _attention(sink, q, qs, k, v, kc, vc, *, seq, n_ctx, band):
    rows = q.shape[0]
    n_batch = rows // seq
    qb = Q_BLOCK if band else seq
    steps = seq // qb
    kern = functools.partial(_attn_kernel, band=band, seq=seq)
    seq_spec = pl.BlockSpec((seq, KV_W), lambda b, i: (b, 0))
    ctx_spec = pl.BlockSpec((n_ctx, KV_W), lambda b, i: (b, 0))
    q_spec = pl.BlockSpec((qb, GROUP_W), lambda b, i: (b * steps + i, 0))
    return pl.pallas_call(
        kern,
        grid=(n_batch, steps),
        in_specs=[pl.BlockSpec(memory_space=pltpu.SMEM), q_spec, q_spec,
                  seq_spec, seq_spec, ctx_spec, ctx_spec],
        out_specs=q_spec,
        out_shape=jax.ShapeDtypeStruct((rows, GROUP_W), BF16),
        compiler_params=_cparams("arbitrary", "arbitrary"),
        name="attn" if band else "attn_ctx",
    )(sink, q, qs, k, v, kc, vc)


POOL_HALO = max(POOL_WINDOWS) // 2


def _pool(p_ref, t0, tm, seq):
    halo = POOL_HALO
    pack = 2 * SUBLANES
    t0 = pl.multiple_of(t0, pack)
    main = p_ref[pl.ds(t0, tm), :].astype(F32)
    lo = pl.multiple_of(jnp.maximum(t0 - pack, 0), pack)
    hi = pl.multiple_of(jnp.minimum(t0 + tm, seq - pack), pack)
    prev = p_ref[pl.ds(lo, pack), :].astype(F32)[pack - halo:, :]
    nxt = p_ref[pl.ds(hi, pack), :].astype(F32)[:halo, :]
    prev = jnp.where(t0 > 0, prev, 0.0)
    nxt = jnp.where(t0 + tm < seq, nxt, 0.0)
    full = jnp.concatenate([prev, main, nxt], axis=0)
    n = tm + 2 * halo
    gch = GROUP_W // len(POOL_WINDOWS)
    first = lax.broadcasted_iota(jnp.int32, (1, LANES), 1) < gch
    means = []
    for hf in range(GROUP_W // LANES):
        wa, wb = POOL_WINDOWS[2 * hf], POOL_WINDOWS[2 * hf + 1]
        x = full[:, hf * LANES:(hf + 1) * LANES]
        sums, w, s = {}, 2, pltpu.roll(x, 1, axis=0) + x
        sums[w] = s
        while w < wb:
            s = pltpu.roll(s, w // 2, axis=0) + pltpu.roll(s, n - w // 2, axis=0)
            w *= 2
            sums[w] = s
        means.append(jnp.where(first, sums[wa] * (1.0 / wa), sums[wb] * (1.0 / wb))[halo:halo + tm, :])
    mean = jnp.concatenate(means, axis=1)
    win = jnp.concatenate([jnp.full((1, gch), w, jnp.int32) for w in POOL_WINDOWS], axis=1)

    def rescale(rows, first_pos):
        pos = first_pos + lax.broadcasted_iota(jnp.int32, (halo, 1), 0)
        cnt = jnp.minimum(pos + win // 2, seq) - jnp.maximum(pos - win // 2, 0)
        return rows * (win.astype(F32) / cnt.astype(F32))

    mean = jnp.concatenate([rescale(mean[:halo], t0), mean[halo:tm - halo],
                            rescale(mean[tm - halo:], t0 + tm - halo)], axis=0)
    return mean - main


def _route(logits):
    tm = logits.shape[0]
    lt = logits.T
    gl = lt[N_EXPERTS:N_EXPERTS + N_GROUPS]
    sub_g = lax.broadcasted_iota(jnp.int32, gl.shape, 0)
    gmax = jnp.max(gl, axis=0, keepdims=True)
    grp = jnp.min(jnp.where(gl == gmax, sub_g, N_GROUPS), axis=0, keepdims=True)
    gate_group = 1.0 / jnp.sum(jnp.exp(gl - gmax), axis=0, keepdims=True)
    el = lt[0:EXPERTS_PER_GROUP]
    for g in range(1, N_GROUPS):
        el = jnp.where(grp == g, lt[g * EXPERTS_PER_GROUP:(g + 1) * EXPERTS_PER_GROUP], el)
    sub = lax.broadcasted_iota(jnp.int32, el.shape, 0)
    m1 = jnp.max(el, axis=0, keepdims=True)
    i1 = jnp.min(jnp.where(el == m1, sub, EXPERTS_PER_GROUP), axis=0, keepdims=True)
    el2 = jnp.where(sub == i1, -jnp.inf, el)
    m2 = jnp.max(el2, axis=0, keepdims=True)
    i2 = jnp.min(jnp.where(el2 == m2, sub, EXPERTS_PER_GROUP), axis=0, keepdims=True)
    r = jnp.exp(m2 - m1)
    g1 = gate_group / (1.0 + r)
    g2 = g1 * r
    e1 = (grp * EXPERTS_PER_GROUP + i1).astype(F32)
    e2 = (grp * EXPERTS_PER_GROUP + i2).astype(F32)
    rows = jnp.where(sub == 0, e1, jnp.where(sub == 1, e2, jnp.where(sub == 2, g1, jnp.where(sub == 3, g2, 0.0))))
    cols = jnp.concatenate([rows, jnp.zeros((LANES - rows.shape[0], tm), F32)], axis=0).T
    return cols, rows


def _merge_kernel(x_ref, p_ref, ug_ref, yf_ref, ya_ref, mod_ref, wpool_ref, pscale_ref, wsgu_ref, bsgu_ref,
                  wout_ref, lng_ref, lnb_ref, wr_ref, *rest,
                  mod_row, seq, n_alias):
    x1_ref, h2_ref, route_ref, route_t_ref = rest[n_alias:]
    tm, d = x_ref.shape
    if mod_row is None:
        row = pl.program_id(0)
    else:
        row = mod_row
    t0 = pl.multiple_of(pl.program_id(1) * tm, tm)
    m = mod_ref[pl.ds(row, 1), :]
    gate1, shift2, scale2 = m[:, 2 * d:3 * d], m[:, 3 * d:4 * d], m[:, 4 * d:5 * d]
    lane = lax.broadcasted_iota(jnp.int32, (1, GROUP_W), 1)
    n_heads = wsgu_ref.shape[0] // SGU_CHUNK
    head = lane // (GROUP_W // n_heads)

    pm = min(tm, MERGE_PART)
    for part in range(tm // pm):
        r0 = part * pm
        rows = slice(r0, r0 + pm)
        pooled = _pool(p_ref, t0 + r0, pm, seq)
        y_pool = _dot(pooled.astype(BF16), wpool_ref[...]) * pscale_ref[...]

        ug = ug_ref[rows, :].astype(F32)
        u = _gelu(ug[:, 0:GROUP_W])
        v = _layer_norm(_gelu(ug[:, GROUP_W:])).astype(BF16)
        mixed = []
        for cidx in range(pm // SGU_CHUNK):
            vc = v[cidx * SGU_CHUNK:(cidx + 1) * SGU_CHUNK, :]
            full = _dot(wsgu_ref[...], vc)
            mc = bsgu_ref[...]
            for hd in range(n_heads):
                mc = mc + jnp.where(head == hd, full[hd * SGU_CHUNK:(hd + 1) * SGU_CHUNK, :], 0.0)
            mixed.append(mc)
        y_sgu = u * jnp.concatenate(mixed, axis=0)

        cat = jnp.concatenate([yf_ref[rows, :].astype(BF16), ya_ref[rows, :], y_pool.astype(BF16),
                               y_sgu.astype(BF16)], axis=1)
        y = _dot(cat, wout_ref[...])
        x1 = _layer_norm(RES_ALPHA * x_ref[rows, :] + gate1 * y) * lng_ref[...] + lnb_ref[...]
        x1_ref[rows, :] = x1
        h2 = _layer_norm(x1) * (1.0 + scale2) + shift2
        _store_rows(h2_ref, h2, first=r0)
        lg = _dot(h2.astype(BF16), wr_ref[...])
        route_ref[rows, :], route_t_ref[:, rows] = _route(lg[:, 0:LANES] + lg[:, LANES:])


def _merge(x2, p, ug, y_four, y_attn, mod, w_pool_bd, pool_scale, w_sgu_stack, b_sgu_exp, w_out,
           ln_g, ln_b, w_route, aliased, *, mod_row, seq, tm, row_off, total_rows):
    rows, d = x2.shape
    n_batch, steps = rows // seq, seq // tm
    off = row_off // tm
    kern = functools.partial(_merge_kernel, mod_row=mod_row, seq=seq, n_alias=len(aliased))
    row_spec = lambda w: pl.BlockSpec((tm, w), lambda b, i: (b * steps + i, 0))
    const = lambda a: pl.BlockSpec(a.shape, lambda b, i: (0,) * a.ndim)
    consts = (mod, w_pool_bd, pool_scale, w_sgu_stack, b_sgu_exp, w_out, ln_g, ln_b, w_route)
    n_in = 5 + len(consts)
    out_shapes = [jax.ShapeDtypeStruct((total_rows, d), F32),
                  jax.ShapeDtypeStruct((total_rows * ROW_WORDS, LANES), jnp.uint32),
                  jax.ShapeDtypeStruct((total_rows, LANES), F32),
                  jax.ShapeDtypeStruct((SUBLANES, total_rows), F32)]
    out_specs = [pl.BlockSpec((tm, d), lambda b, i: (off + b * steps + i, 0)),
                 pl.BlockSpec((tm * ROW_WORDS, LANES), lambda b, i: (off + b * steps + i, 0)),
                 pl.BlockSpec((tm, LANES), lambda b, i: (off + b * steps + i, 0)),
                 pl.BlockSpec((SUBLANES, tm), lambda b, i: (0, off + b * steps + i))]
    return pl.pallas_call(
        kern,
        grid=(n_batch, steps),
        in_specs=[row_spec(d), pl.BlockSpec((seq, GROUP_W), lambda b, i: (b, 0)),
                  row_spec(2 * GROUP_W), row_spec(GROUP_W), row_spec(GROUP_W)]
                 + [const(a) for a in consts]
                 + [pl.BlockSpec(memory_space=pl.ANY)] * len(aliased),
        out_specs=out_specs,
        out_shape=out_shapes,
        input_output_aliases={n_in + k: k for k in range(len(aliased))},
        compiler_params=_cparams("arbitrary", "arbitrary"),
        name="merge",
    )(x2, p, ug, y_four, y_attn, *consts, *aliased)


def _plan_kernel(route_ref, dest_ref, cnt_out_ref, cnt_ref, start_ref, carry_ref):
    ph, t = pl.program_id(0), pl.program_id(1)
    tm = route_ref.shape[1]
    rt = route_ref[...]
    e1 = rt[0:1, :].astype(jnp.int32)
    e2 = rt[1:2, :].astype(jnp.int32)
    sub = lax.broadcasted_iota(jnp.int32, (N_EXPERTS, tm), 0)
    hit1, hit2 = sub == e1, sub == e2
    onehot = jnp.where(hit1 | hit2, 1.0, 0.0)
    tile_cnt = jnp.sum(onehot, axis=1, keepdims=True)

    @pl.when((ph == 0) & (t == 0))
    def _():
        cnt_ref[...] = jnp.zeros_like(cnt_ref)

    @pl.when(ph == 0)
    def _():
        cnt_ref[...] += tile_cnt

    @pl.when((ph == 1) & (t == 0))
    def _():
        cnt = cnt_ref[...]
        padded = jnp.floor((cnt + (MOE_BLOCK - 1.0)) * (1.0 / MOE_BLOCK)) * MOE_BLOCK
        row = lax.broadcasted_iota(jnp.int32, cnt.shape, 0)
        incl = padded
        sh = 1
        while sh < N_EXPERTS:
            incl = incl + jnp.where(row >= sh, pltpu.roll(incl, sh, axis=0), 0.0)
            sh *= 2
        start_ref[...] = incl - padded
        carry_ref[...] = jnp.zeros_like(carry_ref)
        cnt_out_ref[...] = cnt

    @pl.when(ph == 1)
    def _():
        r_i = lax.broadcasted_iota(jnp.int32, (tm, tm), 0)
        c_i = lax.broadcasted_iota(jnp.int32, (tm, tm), 1)
        before = jnp.where(r_i < c_i, 1.0, 0.0).astype(BF16)
        rank = _dot(onehot.astype(BF16), before)
        base = start_ref[:, 0:1] + carry_ref[:, 0:1] + rank
        d1 = jnp.sum(jnp.where(hit1, base, 0.0), axis=0, keepdims=True)
        d2 = jnp.sum(jnp.where(hit2, base, 0.0), axis=0, keepdims=True)
        sub8 = lax.broadcasted_iota(jnp.int32, (SUBLANES, tm), 0)
        dest_ref[...] = jnp.where(sub8 == 0, d1, d2).astype(jnp.int32)
        carry_ref[...] += tile_cnt


def _plan(route_t, tm):
    rows = route_t.shape[1]
    n_t = rows // tm
    return pl.pallas_call(
        _plan_kernel,
        grid=(2, n_t),
        in_specs=[pl.BlockSpec((SUBLANES, tm), lambda ph, t: (0, t))],
        out_specs=[pl.BlockSpec((None, SUBLANES, tm), lambda ph, t: (t * ph, 0, 0)),
                   pl.BlockSpec((N_EXPERTS, LANES), lambda ph, t: (0, 0))],
        out_shape=[jax.ShapeDtypeStruct((n_t, SUBLANES, tm), jnp.int32),
                   jax.ShapeDtypeStruct((N_EXPERTS, LANES), F32)],
        scratch_shapes=[pltpu.VMEM((N_EXPERTS, LANES), F32)] * 3,
        compiler_params=_cparams("arbitrary", "arbitrary"),
        name="moe_plan",
    )(route_t)


def _row_copy(src_ref, src_row, dst_ref, dst_row, sem):
    return pltpu.make_async_copy(
        src_ref.at[pl.ds(pl.multiple_of(src_row * ROW_WORDS, ROW_WORDS), ROW_WORDS)],
        dst_ref.at[pl.ds(pl.multiple_of(dst_row * ROW_WORDS, ROW_WORDS), ROW_WORDS)], sem)


def _dispatch_kernel(dest_ref, h2_ref, xs_ref, sem, *, tm):
    def body(r4, c):
        for u in range(DMA_UNROLL):
            r = r4 * DMA_UNROLL + u
            for k in range(2):
                _row_copy(h2_ref, r, xs_ref, dest_ref[0, 0, k * tm + r], sem).start(priority=k)
        return c

    lax.fori_loop(0, tm // DMA_UNROLL, body, 0)
    for k in range(2):
        pltpu.make_async_copy(h2_ref, xs_ref.at[pl.ds(0, tm * ROW_WORDS)], sem).wait()


def _dispatch(dest, h2_tiles, n_slots, tm):
    n_t = dest.shape[0]
    return pl.pallas_call(
        functools.partial(_dispatch_kernel, tm=tm),
        grid=(n_t,),
        in_specs=[pl.BlockSpec((1, 1, 2 * tm), lambda i: (i, 0, 0), memory_space=pltpu.SMEM),
                  pl.BlockSpec((tm * ROW_WORDS, LANES), lambda i: (i, 0))],
        out_specs=pl.BlockSpec(memory_space=pl.ANY),
        out_shape=jax.ShapeDtypeStruct((n_slots * ROW_WORDS, LANES), jnp.uint32),
        scratch_shapes=[pltpu.SemaphoreType.DMA],
        compiler_params=_cparams("arbitrary"),
        name="moe_dispatch",
    )(dest, h2_tiles)


def _expert_kernel(be_ref, bn_ref, first_ref, slot_ref, nxt_ref, x_ref, wg_hbm, wu_hbm, wd_hbm, y_ref,
                   wg_buf, wu_buf, wd_buf, wg_bf, wu_bf, wd_bf, sem, *, layer):
    i = pl.program_id(0)
    mb = x_ref.shape[0] // ROW_WORDS
    half = wg_bf.shape[0] // 2

    def weight_copies(e, s):
        return [pltpu.make_async_copy(wg_hbm.at[layer, e], wg_buf.at[s], sem.at[s]),
                pltpu.make_async_copy(wu_hbm.at[layer, e], wu_buf.at[s], sem.at[s]),
                pltpu.make_async_copy(wd_hbm.at[layer, e], wd_buf.at[s], sem.at[s])]

    @pl.when(i == 0)
    def _():
        for cp in weight_copies(be_ref[0], 0):
            cp.start()

    @pl.when(first_ref[i] == 1)
    def _():
        s = slot_ref[i]
        for cp in weight_copies(be_ref[i], s):
            cp.wait()

        @pl.when(nxt_ref[i] >= 0)
        def _():
            for cp in weight_copies(nxt_ref[i], 1 - s):
                cp.start()

        wg_bf[...] = wg_buf[s].astype(BF16)
        wu_bf[...] = wu_buf[s].astype(BF16)
        wd_bf[...] = wd_buf[s].astype(BF16)

    def compute(m):
        live = lax.broadcasted_iota(jnp.int32, (m, 1), 0) < bn_ref[i]
        x_lo, x_hi = _load_rows(x_ref, 0, m)
        x_lo = jnp.where(live, x_lo, 0.0).astype(BF16)
        x_hi = jnp.where(live, x_hi, 0.0).astype(BF16)
        g = _dot(x_lo, wg_bf[0:half, :]) + _dot(x_hi, wg_bf[half:, :])
        u = _dot(x_lo, wu_bf[0:half, :]) + _dot(x_hi, wu_bf[half:, :])
        hid = (_silu(g) * u).astype(BF16)
        _store_rows(y_ref, _dot(hid, wd_bf[...]))

    for parts in range(1, mb // MOE_PART + 1):
        @pl.when((bn_ref[i] > (parts - 1) * MOE_PART) & (bn_ref[i] <= parts * MOE_PART))
        def _(parts=parts):
            compute(parts * MOE_PART)


def _experts(table, xs, w_gate, w_up, w_down, layer):
    n_blocks = table[0].shape[0]
    _, _, d, de = w_gate.shape
    blk = pl.BlockSpec((MOE_BLOCK * ROW_WORDS, LANES), lambda i, *_: (i, 0))
    hbm = pl.BlockSpec(memory_space=pl.ANY)
    return pl.pallas_call(
        functools.partial(_expert_kernel, layer=layer),
        grid_spec=pltpu.PrefetchScalarGridSpec(
            num_scalar_prefetch=len(table),
            grid=(n_blocks,),
            in_specs=[blk, hbm, hbm, hbm],
            out_specs=blk,
            scratch_shapes=[pltpu.VMEM((2, d, de), F32), pltpu.VMEM((2, d, de), F32), pltpu.VMEM((2, de, d), F32),
                            pltpu.VMEM((d, de), BF16), pltpu.VMEM((d, de), BF16), pltpu.VMEM((de, d), BF16),
                            pltpu.SemaphoreType.DMA((2,))]),
        out_shape=jax.ShapeDtypeStruct(xs.shape, jnp.uint32),
        compiler_params=_cparams("arbitrary"),
        name="moe_experts",
    )(*table, xs, w_gate, w_up, w_down)


def _combine_body(dest_ref, dest_next_ref, x1_ref, route_ref, m, lng_ref, lnb_ref, y_ref, buf_ref, sem):
    tm, d = x1_ref.shape
    i = pl.program_id(0)
    n = pl.num_programs(0)
    slot = i % 2

    def gather(idx_ref, s):
        def body(r4, c):
            for u in range(DMA_UNROLL):
                r = r4 * DMA_UNROLL + u
                for k in range(2):
                    _row_copy(y_ref, idx_ref[0, 0, k * tm + r], buf_ref.at[s], k * tm + r,
                              sem.at[s]).start(priority=k)
            return c

        lax.fori_loop(0, tm // DMA_UNROLL, body, 0)

    @pl.when(i == 0)
    def _():
        gather(dest_ref, 0)

    @pl.when(i + 1 < n)
    def _():
        gather(dest_next_ref, 1 - slot)

    for k in range(2):
        pltpu.make_async_copy(y_ref.at[pl.ds(0, tm * ROW_WORDS)],
                              buf_ref.at[slot, pl.ds(0, tm * ROW_WORDS)], sem.at[slot]).wait()

    gate2 = m[:, 5 * d:6 * d]
    rt = route_ref[...]
    f = jnp.zeros((tm, d), F32)
    for k in range(2):
        lo, hi = _load_rows(buf_ref.at[slot], k * tm, tm)
        f = f + jnp.concatenate([lo, hi], axis=1) * rt[:, 2 + k:3 + k]
    return _layer_norm(RES_ALPHA * x1_ref[...] + gate2 * f) * lng_ref[...] + lnb_ref[...]


def _combine_kernel(dest_ref, dest_next_ref, x1_ref, route_ref, mod_ref, lng_ref, lnb_ref, y_ref, o_ref,
                    buf_ref, sem, *, mod_row, rows_per_batch):
    m = _mod_row(mod_ref, mod_row, x1_ref.shape[0], rows_per_batch)
    o_ref[...] = _combine_body(dest_ref, dest_next_ref, x1_ref, route_ref, m, lng_ref, lnb_ref, y_ref,
                               buf_ref, sem)


def _combine_proj_kernel(dest_ref, dest_next_ref, x1_ref, route_ref, mod_ref, lng_ref, lnb_ref,
                         mod_next_ref, w_ref, cos_ref, sin_ref, y_ref, o_ref, *rest,
                         rows_per_batch, a_pitch):
    outs, (buf_ref, sem) = rest[:-2], rest[-2:]
    tm = x1_ref.shape[0]
    x = _combine_body(dest_ref, dest_next_ref, x1_ref, route_ref, _mod_row(mod_ref, None, tm, rows_per_batch),
                      lng_ref, lnb_ref, y_ref, buf_ref, sem)
    o_ref[...] = x
    _proj_body(x, _mod_row(mod_next_ref, None, tm, rows_per_batch), w_ref, cos_ref, sin_ref, outs,
               rope=True, a_pitch=a_pitch)


def _combine_specs(tm, d, off, steps):
    return [pl.BlockSpec((1, 1, 2 * tm), lambda i: (off + i, 0, 0), memory_space=pltpu.SMEM),
            pl.BlockSpec((1, 1, 2 * tm), lambda i: (off + jnp.minimum(i + 1, steps - 1), 0, 0),
                         memory_space=pltpu.SMEM),
            pl.BlockSpec((tm, d), lambda i: (off + i, 0)),
            pl.BlockSpec((tm, LANES), lambda i: (off + i, 0))]


def _combine_scratch(tm):
    return [pltpu.VMEM((2, 2 * tm * ROW_WORDS, LANES), jnp.uint32), pltpu.SemaphoreType.DMA((2,))]


def _combine(dest, x1, route, mod, ln_g, ln_b, y_tiles, *, tm, row_off, rows, mod_row, rows_per_batch):
    d = x1.shape[1]
    steps = rows // tm
    kern = functools.partial(_combine_kernel, mod_row=mod_row, rows_per_batch=rows_per_batch)
    const = lambda a: pl.BlockSpec(a.shape, lambda i: (0,) * a.ndim)
    return pl.pallas_call(
        kern,
        grid=(steps,),
        in_specs=_combine_specs(tm, d, row_off // tm, steps)
                 + [const(mod), const(ln_g), const(ln_b), pl.BlockSpec(memory_space=pl.ANY)],
        out_specs=pl.BlockSpec((tm, d), lambda i: (i, 0)),
        out_shape=jax.ShapeDtypeStruct((rows, d), F32),
        scratch_shapes=_combine_scratch(tm),
        compiler_params=_cparams("arbitrary"),
        name="moe_combine",
    )(dest, dest, x1, route, mod, ln_g, ln_b, y_tiles)


def _combine_proj(dest, x1, route, mod, ln_g, ln_b, y_tiles, mod_next, w_in, cos_t, sin_t, *,
                  tm, rows, rows_per_batch, a_pitch):
    d = x1.shape[1]
    steps = rows // tm
    kern = functools.partial(_combine_proj_kernel, rows_per_batch=rows_per_batch, a_pitch=a_pitch)
    const = lambda a: pl.BlockSpec(a.shape, lambda i: (0,) * a.ndim)
    table_spec, proj_specs, proj_shapes = _proj_specs(rows, tm, a_pitch, cos_t.shape[0] // tm)
    outs = pl.pallas_call(
        kern,
        grid=(steps,),
        in_specs=_combine_specs(tm, d, 0, steps)
                 + [const(mod), const(ln_g), const(ln_b), const(mod_next), const(w_in), table_spec, table_spec,
                    pl.BlockSpec(memory_space=pl.ANY)],
        out_specs=[pl.BlockSpec((tm, d), lambda i: (i, 0))] + proj_specs,
        out_shape=[jax.ShapeDtypeStruct((rows, d), F32)] + proj_shapes,
        scratch_shapes=_combine_scratch(tm),
        compiler_params=_cparams("arbitrary"),
        name="moe_combine_proj",
    )(dest, dest, x1, route, mod, ln_g, ln_b, mod_next, w_in, cos_t, sin_t, y_tiles)
    return outs[0], outs[1:]


def _rope_tables(n_pos):
    rows = n_pos // GRID_W
    row = jnp.repeat(jnp.arange(rows), GRID_W).astype(F32)
    col = jnp.tile(jnp.arange(GRID_W), rows).astype(F32)
    n_freq = HEAD_DIM // 4
    freq = ROPE_BASE ** (-jnp.arange(n_freq, dtype=F32) / n_freq)
    ang_r, ang_c = row[:, None] * freq, col[:, None] * freq
    cos_h = jnp.concatenate([jnp.cos(ang_r)] * 2 + [jnp.cos(ang_c)] * 2, axis=1)
    sin_h = jnp.concatenate([-jnp.sin(ang_r), jnp.sin(ang_r), -jnp.sin(ang_c), jnp.sin(ang_c)], axis=1)
    return jnp.tile(cos_h, (1, 2)), jnp.tile(sin_h, (1, 2))


def _block_table(counts, n_blocks):
    cnt = counts.astype(jnp.int32)
    padded = (cnt + MOE_BLOCK - 1) // MOE_BLOCK * MOE_BLOCK
    pad_end = jnp.cumsum(padded)
    pad_start = pad_end - padded
    blk_start = jnp.arange(n_blocks, dtype=jnp.int32)[:, None] * MOE_BLOCK
    be = jnp.minimum(jnp.sum((pad_end[None, :] <= blk_start).astype(jnp.int32), axis=1), N_EXPERTS - 1)
    ids = jnp.arange(N_EXPERTS, dtype=jnp.int32)
    mine = be[:, None] == ids[None, :]
    fill = jnp.sum(jnp.where(mine, cnt[None, :] + pad_start[None, :], 0), axis=1) - blk_start[:, 0]
    bn = jnp.clip(fill, 0, MOE_BLOCK)
    prev = jnp.concatenate([jnp.full((1,), -1, jnp.int32), be[:-1]])
    first = ((bn > 0) & (be != prev)).astype(jnp.int32)
    slot = (jnp.cumsum(first) - 1) % 2
    later = (ids[None, :] > ids[:, None]) & (cnt[None, :] > 0)
    nxt_e = jnp.min(jnp.where(later, ids[None, :], N_EXPERTS), axis=1)
    nxt_e = jnp.where(nxt_e == N_EXPERTS, -1, nxt_e)
    nxt = jnp.sum(jnp.where(mine, nxt_e[None, :], 0), axis=1)
    return be, bn, first, slot.astype(jnp.int32), nxt.astype(jnp.int32)


def _moe(route_t, h2_tiles, w_gate, w_up, w_down, layer):
    rows = route_t.shape[1]
    tm = ROW_TILE
    n_blocks = -(-(2 * rows) // MOE_BLOCK) + N_EXPERTS
    dest8, counts = _plan(route_t, tm)
    dest = dest8[:, 0:2, :].reshape(rows // tm, 1, 2 * tm)
    table = _block_table(counts[:, 0], n_blocks)
    xs = _dispatch(dest, h2_tiles, n_blocks * MOE_BLOCK, tm)
    ys = _experts(table, xs, w_gate, w_up, w_down, layer)
    return dest, ys


def kernel(x, c, ctx, c_ctx, w_ada, b_ada, w_in, w_fourier, attn_sink, w_pool, pool_scale, w_sgu, b_sgu,
           w_out, ln1_g, ln1_b, w_router_group, w_router_expert, w_exp_gate, w_exp_up, w_exp_down,
           ln2_g, ln2_b):
    b, s, d = x.shape
    n_ctx = ctx.shape[1]
    n_layers = w_in.shape[0]
    tm = ROW_TILE
    cond = jnp.concatenate([c, c_ctx[None, :], jnp.zeros((SUBLANES - b - 1, d), F32)], axis=0)
    mod_all = _ada(cond, w_ada, b_ada[:, None, :])
    cos_t, sin_t = _rope_tables(s)
    x2 = x.reshape(b * s, d)
    c2 = ctx.reshape(b * n_ctx, d)
    n_sgu = w_sgu.shape[1]
    lat = None
    for layer in range(n_layers):
        last = layer == n_layers - 1
        mod = mod_all[layer]
        w_in_l = w_in[layer].astype(BF16)
        wf = w_fourier[layer].astype(BF16)
        w_pool_bd = jax.scipy.linalg.block_diag(*[w_pool[layer, g] for g in range(w_pool.shape[1])]).astype(BF16)
        w_sgu_stack = w_sgu[layer].reshape(n_sgu * SGU_CHUNK, SGU_CHUNK).astype(BF16)
        b_sgu_exp = jnp.repeat(b_sgu[layer].T, GROUP_W // n_sgu, axis=1)
        w_router = jnp.concatenate([w_router_expert[layer].reshape(d, N_EXPERTS), w_router_group[layer]], axis=1)
        w_router = jnp.pad(w_router, ((0, 0), (0, LANES - w_router.shape[1])))
        wr_hi = w_router.astype(BF16)
        w_route = jnp.concatenate([wr_hi, (w_router - wr_hi.astype(F32)).astype(BF16)], axis=1)
        merge_consts = (mod, w_pool_bd, pool_scale[layer][None, :], w_sgu_stack, b_sgu_exp,
                        w_out[layer].astype(BF16), ln1_g[layer][None, :], ln1_b[layer][None, :], w_route)
        sink = attn_sink[layer]

        if lat is None:
            lat = _proj(x2, mod, w_in_l, cos_t, sin_t, mod_row=None, rows_per_batch=s,
                        rope=True, tm=tm, a_pitch=FFT_PITCH)
        a, q, qs, k, v, p, ug = lat
        ac, qc, qsc, kc, vc, pc, ugc = _proj(c2, mod, w_in_l, cos_t, sin_t, mod_row=b, rows_per_batch=n_ctx,
                                             rope=False, tm=n_ctx, a_pitch=FFT_R)
        y_four = _fourier(a, wf, s)
        y_attn = _attention(sink, q, qs, k, v, kc, vc, seq=s, n_ctx=n_ctx, band=True)
        total = b * s + (0 if last else b * n_ctx)
        merged = _merge(x2, p, ug, y_four, y_attn, *merge_consts, (),
                        mod_row=None, seq=s, tm=tm, row_off=0, total_rows=total)
        if not last:
            yc_four = _fourier_small(ac, wf, n_ctx)
            yc_attn = _attention(sink, qc, qsc, kc, vc, kc, vc, seq=n_ctx, n_ctx=n_ctx, band=False)
            merged = _merge(c2, pc, ugc, yc_four, yc_attn, *merge_consts, tuple(merged),
                            mod_row=b, seq=n_ctx, tm=n_ctx, row_off=b * s, total_rows=total)
        x1, h2_tiles, route, route_t = merged
        dest, ys = _moe(route_t, h2_tiles, w_exp_gate, w_exp_up, w_exp_down, layer)
        ln_g, ln_b = ln2_g[layer][None, :], ln2_b[layer][None, :]
        if last:
            x2 = _combine(dest, x1, route, mod, ln_g, ln_b, ys, tm=tm, row_off=0, rows=b * s,
                          mod_row=None, rows_per_batch=s)
        else:
            x2, lat = _combine_proj(dest, x1, route, mod, ln_g, ln_b, ys, mod_all[layer + 1],
                                    w_in[layer + 1].astype(BF16), cos_t, sin_t,
                                    tm=tm, rows=b * s, rows_per_batch=s, a_pitch=FFT_PITCH)
            c2 = _combine(dest, x1, route, mod, ln_g, ln_b, ys, tm=tm, row_off=b * s, rows=b * n_ctx,
                          mod_row=b, rows_per_batch=n_ctx)
    return x2.reshape(b, s, d)
```

```python
import functools
import math

import numpy as np
import jax
import jax.numpy as jnp
from jax import lax
from jax.experimental import pallas as pl
from jax.experimental.pallas import tpu as pltpu

GRID_W = 64
HEAD_DIM = 64
GROUP_W = 256
KV_W = 128
WINDOW = 128
POOL_WINDOWS = (2, 4, 8, 16)
SGU_CHUNK = 128
N_GROUPS = 4
EXPERTS_PER_GROUP = 8
N_EXPERTS = 32
ROPE_BASE = 10000.0
LN_EPS = 1e-6
NEG_INF = -1e30
DEPTH = 2
RES_ALPHA = (2 * DEPTH) ** 0.25

LANES = 128
SUBLANES = 8
VMEM_LIMIT = 48 * 1024 * 1024

ROW_TILE = 1024
Q_BLOCK = 1024
COMBINE_TILE = 512
COMBINE_PARTS = 4
MOE_BLOCK = 512
MOE_PART = 256
FFT_R = 64
FFT_PITCH = 72
FFT_UNROLL = 8
DMA_UNROLL = 8
MERGE_PART = 256

BF16 = jnp.bfloat16
F32 = jnp.float32


def _cparams(*sem):
    return pltpu.CompilerParams(dimension_semantics=sem, vmem_limit_bytes=VMEM_LIMIT)


def _dot(a, b):
    return jnp.dot(a, b, preferred_element_type=F32)


def _dot_nt(a, b):
    return lax.dot_general(a, b, (((1,), (1,)), ((), ())), preferred_element_type=F32)


def _layer_norm(t):
    mu = jnp.mean(t, axis=-1, keepdims=True)
    d = t - mu
    var = jnp.mean(d * d, axis=-1, keepdims=True)
    return d * lax.rsqrt(var + LN_EPS)


def _silu(t):
    return t * (1.0 / (1.0 + jnp.exp(-t)))


def _gelu(t):
    return 0.5 * t * (1.0 + lax.erf(t * (1.0 / math.sqrt(2.0))))


ROW_WORDS = 4
HI_MASK = 0xFFFF0000


def _pack_rows(t):
    half = t.shape[1] // 2
    lo = lax.bitcast_convert_type(t[:, :half].astype(BF16).astype(F32), jnp.uint32)
    hi = lax.bitcast_convert_type(t[:, half:].astype(BF16).astype(F32), jnp.uint32)
    return (lo >> 16) | (hi & jnp.uint32(HI_MASK))


def _unpack_rows(w):
    return (lax.bitcast_convert_type(w << 16, F32),
            lax.bitcast_convert_type(w & jnp.uint32(HI_MASK), F32))


def _store_rows(ref, t, first=0):
    w = _pack_rows(t)
    for j in range(ROW_WORDS):
        ref[pl.ds(first * ROW_WORDS + j, t.shape[0], stride=ROW_WORDS), :] = w[:, j * LANES:(j + 1) * LANES]


def _load_rows(ref, first, m):
    w = jnp.concatenate([ref[pl.ds(first * ROW_WORDS + j, m, stride=ROW_WORDS), :] for j in range(ROW_WORDS)],
                        axis=1)
    return _unpack_rows(w)


def _ada_kernel(c_ref, w_ref, b_ref, o_ref):
    s = _silu(c_ref[...]).astype(BF16)
    o_ref[...] = _dot(s, w_ref[...].astype(BF16)) + b_ref[...]


def _ada(cond, w_ada, b_ada):
    n_layers, d, n = w_ada.shape
    tn = n // 4
    return pl.pallas_call(
        _ada_kernel,
        grid=(n_layers, n // tn),
        in_specs=[
            pl.BlockSpec((SUBLANES, d), lambda l, j: (0, 0)),
            pl.BlockSpec((None, d, tn), lambda l, j: (l, 0, j)),
            pl.BlockSpec((None, 1, tn), lambda l, j: (l, 0, j)),
        ],
        out_specs=pl.BlockSpec((None, SUBLANES, tn), lambda l, j: (l, 0, j)),
        out_shape=jax.ShapeDtypeStruct((n_layers, SUBLANES, n), F32),
        compiler_params=_cparams("arbitrary", "arbitrary"),
        name="ada",
    )(cond, w_ada, b_ada)


def _rope(t, cos_t, sin_t):
    lane = lax.broadcasted_iota(jnp.int32, t.shape, 1)
    first = (lane % 32) < 16
    partner = jnp.where(first, pltpu.roll(t, LANES - 16, axis=1), pltpu.roll(t, 16, axis=1))
    return t * cos_t + partner * sin_t


def _proj_body(x, m, w_ref, cos_ref, sin_ref, outs, *, rope, a_pitch):
    a_ref, q_ref, qs_ref, k_ref, v_ref, p_ref, ug_ref = outs
    tm, d = x.shape
    shift, scale = m[:, 0:d], m[:, d:2 * d]
    h = _layer_norm(x) * (1.0 + scale) + shift
    z = _dot(h.astype(BF16), w_ref[...])
    pad = jnp.zeros((a_pitch - FFT_R, LANES), F32)
    for g in range(tm // FFT_R):
        for hf in range(2):
            grp = z[g * FFT_R:(g + 1) * FFT_R, hf * LANES:(hf + 1) * LANES]
            if a_pitch > FFT_R:
                grp = jnp.concatenate([grp, pad], axis=0)
            a_ref[hf, g * a_pitch:(g + 1) * a_pitch, :] = grp
    q0, q1 = z[:, 256:384], z[:, 384:512]
    k = z[:, 512:640]
    if rope:
        cos_t, sin_t = cos_ref[...], sin_ref[...]
        q0, q1, k = _rope(q0, cos_t, sin_t), _rope(q1, cos_t, sin_t), _rope(k, cos_t, sin_t)
    q_ref[:, 0:128] = q0.astype(BF16)
    q_ref[:, 128:256] = q1.astype(BF16)
    qs_ref[:, 0:128] = pltpu.roll(q0, HEAD_DIM, axis=1).astype(BF16)
    qs_ref[:, 128:256] = pltpu.roll(q1, HEAD_DIM, axis=1).astype(BF16)
    k_ref[...] = k.astype(BF16)
    v_ref[...] = z[:, 640:768].astype(BF16)
    p_ref[...] = z[:, 768:1024].astype(BF16)
    ug_ref[...] = z[:, 1024:1536].astype(BF16)


def _mod_row(mod_ref, mod_row, tm, rows_per_batch):
    row = (pl.program_id(0) * tm) // rows_per_batch if mod_row is None else mod_row
    return mod_ref[pl.ds(row, 1), :]


def _proj_kernel(x_ref, mod_ref, w_ref, cos_ref, sin_ref, *outs, mod_row, rows_per_batch, rope, a_pitch):
    m = _mod_row(mod_ref, mod_row, x_ref.shape[0], rows_per_batch)
    _proj_body(x_ref[...], m, w_ref, cos_ref, sin_ref, outs, rope=rope, a_pitch=a_pitch)


def _proj_specs(rows, tm, a_pitch, seq_steps):
    row_spec = lambda w: pl.BlockSpec((tm, w), lambda i: (i, 0))
    out_w = (256, 256, 128, 128, 256, 512)
    ta = tm // FFT_R * a_pitch
    out_specs = [pl.BlockSpec((2, ta, LANES), lambda i: (0, i, 0))] + [row_spec(w) for w in out_w]
    out_shape = ([jax.ShapeDtypeStruct((2, rows // FFT_R * a_pitch, LANES), F32)]
                 + [jax.ShapeDtypeStruct((rows, w), BF16) for w in out_w])
    table_spec = pl.BlockSpec((tm, LANES), lambda i: (i % seq_steps, 0))
    return table_spec, out_specs, out_shape


def _proj(x2, mod, w_in, cos_t, sin_t, *, mod_row, rows_per_batch, rope, tm, a_pitch):
    rows, d = x2.shape
    kern = functools.partial(_proj_kernel, mod_row=mod_row, rows_per_batch=rows_per_batch,
                             rope=rope, a_pitch=a_pitch)
    table_spec, out_specs, out_shape = _proj_specs(rows, tm, a_pitch, cos_t.shape[0] // tm)
    return pl.pallas_call(
        kern,
        grid=(rows // tm,),
        in_specs=[
            pl.BlockSpec((tm, d), lambda i: (i, 0)),
            pl.BlockSpec(mod.shape, lambda i: (0, 0)),
            pl.BlockSpec(w_in.shape, lambda i: (0, 0)),
            table_spec, table_spec,
        ],
        out_specs=out_specs,
        out_shape=out_shape,
        compiler_params=_cparams("arbitrary"),
        name="proj",
    )(x2, mod, w_in, cos_t, sin_t)


def _fft_tables(n_pos):
    r = FFT_R
    assert n_pos == r * r
    kb = np.arange(r)[None, :, None]
    na = np.arange(r)[:, None, None]
    nb = np.arange(r)[None, None, :]
    ang = 2.0 * np.pi * ((kb * (na + r * nb)) % n_pos) / n_pos
    m1 = np.concatenate([np.cos(ang), -np.sin(ang)], axis=1)
    ka = np.arange(r)[:, None]
    n2 = np.arange(r)[None, :]
    ang2 = 2.0 * np.pi * ((ka * n2) % r) / r
    c2, s2 = np.cos(ang2), np.sin(ang2)
    w2 = np.block([[c2, s2], [-s2, c2]])
    return m1, w2


def _channel_tables(n_pos):
    h = HEAD_DIM
    c = np.arange(h)
    ang = 2.0 * np.pi * ((c[:, None] * c[None, :]) % h) / h
    scale = 1.0 / math.sqrt(n_pos * h)
    eye = np.eye(GROUP_W // h)
    cc = np.kron(eye, np.cos(ang)) * scale
    ss = np.kron(eye, np.sin(ang)) * scale
    return np.concatenate([cc, ss], axis=0)


def _fourier_kernel(a_ref, m1_ref, w2_ref, ch_ref, wf_ref, o_ref, z_ref, y_ref):
    r, pt = FFT_R, FFT_PITCH

    def step1(i, c):
        for u in range(FFT_UNROLL):
            na = i * FFT_UNROLL + u
            rows = jnp.concatenate([a_ref[0, pl.ds(na, r, stride=pt), :],
                                    a_ref[1, pl.ds(na, r, stride=pt), :]], axis=1)
            z = _dot(m1_ref[na], rows.astype(BF16))
            base = pl.multiple_of(na * pt, SUBLANES)
            z_ref[0, pl.ds(base, r), :] = z[0:r, 0:LANES]
            z_ref[1, pl.ds(base, r), :] = z[0:r, LANES:]
            z_ref[2, pl.ds(base, r), :] = z[r:, 0:LANES]
            z_ref[3, pl.ds(base, r), :] = z[r:, LANES:]
        return c

    lax.fori_loop(0, r // FFT_UNROLL, step1, 0)

    def step2(i, c):
        for u in range(FFT_UNROLL):
            kb = i * FFT_UNROLL + u
            q = [z_ref[j, pl.ds(kb, r, stride=pt), :] for j in range(4)]
            zs = jnp.concatenate([jnp.concatenate(q[0:2], axis=1),
                                  jnp.concatenate(q[2:4], axis=1)], axis=0)
            y = _dot(w2_ref[...], zs.astype(BF16))
            base = pl.multiple_of(kb * r, r)
            y_ref[0, pl.ds(base, r), :] = y[0:r, 0:LANES]
            y_ref[1, pl.ds(base, r), :] = y[0:r, LANES:]
            y_ref[2, pl.ds(base, r), :] = y[r:, 0:LANES]
            y_ref[3, pl.ds(base, r), :] = y[r:, LANES:]
        return c

    lax.fori_loop(0, r // FFT_UNROLL, step2, 0)

    chunk = 8 * r
    for cidx in range(r * r // chunk):
        yy = jnp.concatenate([y_ref[j, cidx * chunk:(cidx + 1) * chunk, :] for j in range(4)], axis=1)
        f = _dot(yy.astype(BF16), ch_ref[...])
        g = _dot(f.astype(BF16), wf_ref[...])
        for gi in range(chunk // r):
            kb = cidx * (chunk // r) + gi
            z_ref[0, kb * pt:kb * pt + r, :] = g[gi * r:(gi + 1) * r, 0:LANES]
            z_ref[1, kb * pt:kb * pt + r, :] = g[gi * r:(gi + 1) * r, LANES:]

    def step3(i, c):
        for u in range(FFT_UNROLL):
            ka = i * FFT_UNROLL + u
            base = pl.multiple_of(ka * r, r)
            o_ref[pl.ds(base, r), 0:LANES] = z_ref[0, pl.ds(ka, r, stride=pt), :]
            o_ref[pl.ds(base, r), LANES:] = z_ref[1, pl.ds(ka, r, stride=pt), :]
        return c

    lax.fori_loop(0, r // FFT_UNROLL, step3, 0)


def _fourier(a3, w_fourier, n_pos):
    rows = a3.shape[1] // FFT_PITCH * FFT_R
    gw = GROUP_W
    m1, w2 = _fft_tables(n_pos)
    ch = _channel_tables(n_pos)
    const = lambda shape: pl.BlockSpec(shape, lambda b: (0,) * len(shape))
    return pl.pallas_call(
        _fourier_kernel,
        grid=(rows // n_pos,),
        in_specs=[
            pl.BlockSpec((2, FFT_R * FFT_PITCH, LANES), lambda b: (0, b, 0)),
            const(m1.shape), const(w2.shape), const(ch.shape), const(w_fourier.shape),
        ],
        out_specs=pl.BlockSpec((n_pos, gw), lambda b: (b, 0)),
        out_shape=jax.ShapeDtypeStruct((rows, gw), F32),
        scratch_shapes=[pltpu.VMEM((4, FFT_R * FFT_PITCH, LANES), F32), pltpu.VMEM((4, n_pos, LANES), F32)],
        compiler_params=_cparams("arbitrary"),
        name="fourier",
    )(a3, jnp.asarray(m1, BF16), jnp.asarray(w2, BF16), jnp.asarray(ch, BF16), w_fourier)


def _fourier_small_kernel(a_ref, cs_ref, ch_ref, wf_ref, o_ref):
    n = a_ref.shape[1]
    a = jnp.concatenate([a_ref[0], a_ref[1]], axis=1)
    pq = _dot(cs_ref[...], a.astype(BF16))
    y = jnp.concatenate([pq[0:n], pq[n:2 * n]], axis=1).astype(BF16)
    f = _dot(y, ch_ref[...])
    o_ref[...] = _dot(f.astype(BF16), wf_ref[...])


def _fourier_small(a3, w_fourier, n_pos):
    _, rows, _ = a3.shape
    gw = GROUP_W
    k = np.arange(n_pos)
    ang = 2.0 * np.pi * ((k[:, None] * k[None, :]) % n_pos) / n_pos
    cs = np.concatenate([np.cos(ang), -np.sin(ang)], axis=0)
    ch = _channel_tables(n_pos)
    const = lambda shape: pl.BlockSpec(shape, lambda b: (0,) * len(shape))
    return pl.pallas_call(
        _fourier_small_kernel,
        grid=(rows // n_pos,),
        in_specs=[pl.BlockSpec((2, n_pos, LANES), lambda b: (0, b, 0)),
                  const(cs.shape), const(ch.shape), const(w_fourier.shape)],
        out_specs=pl.BlockSpec((n_pos, gw), lambda b: (b, 0)),
        out_shape=jax.ShapeDtypeStruct((rows, gw), F32),
        compiler_params=_cparams("arbitrary"),
        name="fourier_ctx",
    )(a3, jnp.asarray(cs, BF16), jnp.asarray(ch, BF16), w_fourier)


ATTN_SUB = 128


def _attn_kernel(sink_ref, q_ref, qs_ref, k_ref, v_ref, kc_ref, vc_ref, o_ref, *, band, seq):
    qb = q_ref.shape[0]
    sub = ATTN_SUB
    lane = lax.broadcasted_iota(jnp.int32, (1, LANES), 1)
    lo_half = lane < HEAD_DIM
    zero = jnp.zeros((), BF16)
    scale = jnp.asarray(HEAD_DIM ** -0.5, BF16)
    kw = sub + 2 * WINDOW
    for sb in range(qb // sub):
        rows = slice(sb * sub, (sb + 1) * sub)
        qa0, qa1 = q_ref[rows, 0:LANES], q_ref[rows, LANES:]
        qs0, qs1 = qs_ref[rows, 0:LANES], qs_ref[rows, LANES:]
        q_all = jnp.concatenate([jnp.where(lo_half, qa0, zero), jnp.where(lo_half, qs0, zero),
                                 jnp.where(lo_half, zero, qs1), jnp.where(lo_half, zero, qa1)], axis=0) * scale
        if band:
            p0 = pl.program_id(1) * qb + sb * sub
            start = pl.multiple_of(jnp.clip(p0 - WINDOW, 0, seq - kw), WINDOW)
            qpos = p0 + lax.broadcasted_iota(jnp.int32, (sub, 1), 0)
            kpos = start + lax.broadcasted_iota(jnp.int32, (1, kw), 1)
            bias = jnp.where(jnp.abs(qpos - kpos) <= WINDOW, 0.0, NEG_INF)
            keys = jnp.concatenate([k_ref[pl.ds(start, kw), :], kc_ref[...]], axis=0)
            vals = jnp.concatenate([v_ref[pl.ds(start, kw), :], vc_ref[...]], axis=0)
        else:
            keys, vals = kc_ref[...], vc_ref[...]
        s_all = _dot_nt(q_all, keys)
        probs, dens = [], []
        for h in range(4):
            s = s_all[h * sub:(h + 1) * sub, :]
            sink = sink_ref[h]
            if band:
                s = jnp.concatenate([s[:, 0:kw] + bias, s[:, kw:]], axis=1)
            m = jnp.maximum(jnp.max(s, axis=1, keepdims=True), sink)
            p = jnp.exp(s - m)
            dens.append(jnp.sum(p, axis=1, keepdims=True) + jnp.exp(sink - m))
            probs.append(p.astype(BF16))
        o_all = _dot(jnp.concatenate(probs, axis=0), vals)
        o = [o_all[h * sub:(h + 1) * sub, :] / dens[h] for h in range(4)]
        o_ref[rows, 0:LANES] = jnp.where(lo_half, o[0], pltpu.roll(o[1], HEAD_DIM, axis=1)).astype(BF16)
        o_ref[rows, LANES:] = jnp.where(lo_half, pltpu.roll(o[2], HEAD_DIM, axis=1), o[3]).astype(BF16)


def _attention(sink, q, qs, k, v, kc, vc, *, seq, n_ctx, band):
    rows = q.shape[0]
    n_batch = rows // seq
    qb = Q_BLOCK if band else seq
    steps = seq // qb
    kern = functools.partial(_attn_kernel, band=band, seq=seq)
    seq_spec = pl.BlockSpec((seq, KV_W), lambda b, i: (b, 0))
    ctx_spec = pl.BlockSpec((n_ctx, KV_W), lambda b, i: (b, 0))
    q_spec = pl.BlockSpec((qb, GROUP_W), lambda b, i: (b * steps + i, 0))
    return pl.pallas_call(
        kern,
        grid=(n_batch, steps),
        in_specs=[pl.BlockSpec(memory_space=pltpu.SMEM), q_spec, q_spec,
                  seq_spec, seq_spec, ctx_spec, ctx_spec],
        out_specs=q_spec,
        out_shape=jax.ShapeDtypeStruct((rows, GROUP_W), BF16),
        compiler_params=_cparams("arbitrary", "arbitrary"),
        name="attn" if band else "attn_ctx",
    )(sink, q, qs, k, v, kc, vc)


POOL_HALO = max(POOL_WINDOWS) // 2


def _pool(p_ref, t0, tm, seq):
    halo = POOL_HALO
    pack = 2 * SUBLANES
    t0 = pl.multiple_of(t0, pack)
    main = p_ref[pl.ds(t0, tm), :].astype(F32)
    lo = pl.multiple_of(jnp.maximum(t0 - pack, 0), pack)
    hi = pl.multiple_of(jnp.minimum(t0 + tm, seq - pack), pack)
    prev = p_ref[pl.ds(lo, pack), :].astype(F32)[pack - halo:, :]
    nxt = p_ref[pl.ds(hi, pack), :].astype(F32)[:halo, :]
    prev = jnp.where(t0 > 0, prev, 0.0)
    nxt = jnp.where(t0 + tm < seq, nxt, 0.0)
    full = jnp.concatenate([prev, main, nxt], axis=0)
    n = tm + 2 * halo
    gch = GROUP_W // len(POOL_WINDOWS)
    first = lax.broadcasted_iota(jnp.int32, (1, LANES), 1) < gch
    means = []
    for hf in range(GROUP_W // LANES):
        wa, wb = POOL_WINDOWS[2 * hf], POOL_WINDOWS[2 * hf + 1]
        x = full[:, hf * LANES:(hf + 1) * LANES]
        sums, w, s = {}, 2, pltpu.roll(x, 1, axis=0) + x
        sums[w] = s
        while w < wb:
            s = pltpu.roll(s, w // 2, axis=0) + pltpu.roll(s, n - w // 2, axis=0)
            w *= 2
            sums[w] = s
        means.append(jnp.where(first, sums[wa] * (1.0 / wa), sums[wb] * (1.0 / wb))[halo:halo + tm, :])
    mean = jnp.concatenate(means, axis=1)
    win = jnp.concatenate([jnp.full((1, gch), w, jnp.int32) for w in POOL_WINDOWS], axis=1)

    def rescale(rows, first_pos):
        pos = first_pos + lax.broadcasted_iota(jnp.int32, (halo, 1), 0)
        cnt = jnp.minimum(pos + win // 2, seq) - jnp.maximum(pos - win // 2, 0)
        return rows * (win.astype(F32) / cnt.astype(F32))

    mean = jnp.concatenate([rescale(mean[:halo], t0), mean[halo:tm - halo],
                            rescale(mean[tm - halo:], t0 + tm - halo)], axis=0)
    return mean - main


def _route(logits):
    tm = logits.shape[0]
    lt = logits.T
    gl = lt[N_EXPERTS:N_EXPERTS + N_GROUPS]
    sub_g = lax.broadcasted_iota(jnp.int32, gl.shape, 0)
    gmax = jnp.max(gl, axis=0, keepdims=True)
    grp = jnp.min(jnp.where(gl == gmax, sub_g, N_GROUPS), axis=0, keepdims=True)
    gate_group = 1.0 / jnp.sum(jnp.exp(gl - gmax), axis=0, keepdims=True)
    el = lt[0:EXPERTS_PER_GROUP]
    for g in range(1, N_GROUPS):
        el = jnp.where(grp == g, lt[g * EXPERTS_PER_GROUP:(g + 1) * EXPERTS_PER_GROUP], el)
    sub = lax.broadcasted_iota(jnp.int32, el.shape, 0)
    m1 = jnp.max(el, axis=0, keepdims=True)
    i1 = jnp.min(jnp.where(el == m1, sub, EXPERTS_PER_GROUP), axis=0, keepdims=True)
    el2 = jnp.where(sub == i1, -jnp.inf, el)
    m2 = jnp.max(el2, axis=0, keepdims=True)
    i2 = jnp.min(jnp.where(el2 == m2, sub, EXPERTS_PER_GROUP), axis=0, keepdims=True)
    r = jnp.exp(m2 - m1)
    g1 = gate_group / (1.0 + r)
    g2 = g1 * r
    e1 = (grp * EXPERTS_PER_GROUP + i1).astype(F32)
    e2 = (grp * EXPERTS_PER_GROUP + i2).astype(F32)
    rows = jnp.where(sub == 0, e1, jnp.where(sub == 1, e2, jnp.where(sub == 2, g1, jnp.where(sub == 3, g2, 0.0))))
    cols = jnp.concatenate([rows, jnp.zeros((LANES - rows.shape[0], tm), F32)], axis=0).T
    return cols, rows


def _merge_kernel(x_ref, p_ref, ug_ref, yf_ref, ya_ref, mod_ref, wpool_ref, pscale_ref, wsgu_ref, bsgu_ref,
                  wout_ref, lng_ref, lnb_ref, wr_ref, *rest,
                  mod_row, seq, n_alias):
    x1_ref, h2_ref, route_ref, route_t_ref = rest[n_alias:]
    tm, d = x_ref.shape
    if mod_row is None:
        row = pl.program_id(0)
    else:
        row = mod_row
    t0 = pl.multiple_of(pl.program_id(1) * tm, tm)
    m = mod_ref[pl.ds(row, 1), :]
    gate1, shift2, scale2 = m[:, 2 * d:3 * d], m[:, 3 * d:4 * d], m[:, 4 * d:5 * d]
    lane = lax.broadcasted_iota(jnp.int32, (1, GROUP_W), 1)
    n_heads = wsgu_ref.shape[0] // SGU_CHUNK
    head = lane // (GROUP_W // n_heads)

    pm = min(tm, MERGE_PART)
    for part in range(tm // pm):
        r0 = part * pm
        rows = slice(r0, r0 + pm)
        pooled = _pool(p_ref, t0 + r0, pm, seq)
        y_pool = _dot(pooled.astype(BF16), wpool_ref[...]) * pscale_ref[...]

        ug = ug_ref[rows, :].astype(F32)
        u = _gelu(ug[:, 0:GROUP_W])
        v = _layer_norm(_gelu(ug[:, GROUP_W:])).astype(BF16)
        mixed = []
        for cidx in range(pm // SGU_CHUNK):
            vc = v[cidx * SGU_CHUNK:(cidx + 1) * SGU_CHUNK, :]
            full = _dot(wsgu_ref[...], vc)
            mc = bsgu_ref[...]
            for hd in range(n_heads):
                mc = mc + jnp.where(head == hd, full[hd * SGU_CHUNK:(hd + 1) * SGU_CHUNK, :], 0.0)
            mixed.append(mc)
        y_sgu = u * jnp.concatenate(mixed, axis=0)

        cat = jnp.concatenate([yf_ref[rows, :].astype(BF16), ya_ref[rows, :], y_pool.astype(BF16),
                               y_sgu.astype(BF16)], axis=1)
        y = _dot(cat, wout_ref[...])
        x1 = _layer_norm(RES_ALPHA * x_ref[rows, :] + gate1 * y) * lng_ref[...] + lnb_ref[...]
        x1_ref[rows, :] = x1
        h2 = _layer_norm(x1) * (1.0 + scale2) + shift2
        _store_rows(h2_ref, h2, first=r0)
        lg = _dot(h2.astype(BF16), wr_ref[...])
        route_ref[rows, :], route_t_ref[:, rows] = _route(lg[:, 0:LANES] + lg[:, LANES:])


def _merge(x2, p, ug, y_four, y_attn, mod, w_pool_bd, pool_scale, w_sgu_stack, b_sgu_exp, w_out,
           ln_g, ln_b, w_route, aliased, *, mod_row, seq, tm, row_off, total_rows):
    rows, d = x2.shape
    n_batch, steps = rows // seq, seq // tm
    off = row_off // tm
    kern = functools.partial(_merge_kernel, mod_row=mod_row, seq=seq, n_alias=len(aliased))
    row_spec = lambda w: pl.BlockSpec((tm, w), lambda b, i: (b * steps + i, 0))
    const = lambda a: pl.BlockSpec(a.shape, lambda b, i: (0,) * a.ndim)
    consts = (mod, w_pool_bd, pool_scale, w_sgu_stack, b_sgu_exp, w_out, ln_g, ln_b, w_route)
    n_in = 5 + len(consts)
    out_shapes = [jax.ShapeDtypeStruct((total_rows, d), F32),
                  jax.ShapeDtypeStruct((total_rows * ROW_WORDS, LANES), jnp.uint32),
                  jax.ShapeDtypeStruct((total_rows, LANES), F32),
                  jax.ShapeDtypeStruct((SUBLANES, total_rows), F32)]
    out_specs = [pl.BlockSpec((tm, d), lambda b, i: (off + b * steps + i, 0)),
                 pl.BlockSpec((tm * ROW_WORDS, LANES), lambda b, i: (off + b * steps + i, 0)),
                 pl.BlockSpec((tm, LANES), lambda b, i: (off + b * steps + i, 0)),
                 pl.BlockSpec((SUBLANES, tm), lambda b, i: (0, off + b * steps + i))]
    return pl.pallas_call(
        kern,
        grid=(n_batch, steps),
        in_specs=[row_spec(d), pl.BlockSpec((seq, GROUP_W), lambda b, i: (b, 0)),
                  row_spec(2 * GROUP_W), row_spec(GROUP_W), row_spec(GROUP_W)]
                 + [const(a) for a in consts]
                 + [pl.BlockSpec(memory_space=pl.ANY)] * len(aliased),
        out_specs=out_specs,
        out_shape=out_shapes,
        input_output_aliases={n_in + k: k for k in range(len(aliased))},
        compiler_params=_cparams("arbitrary", "arbitrary"),
        name="merge",
    )(x2, p, ug, y_four, y_attn, *consts, *aliased)


def _plan_kernel(route_ref, dest_ref, cnt_out_ref, cnt_ref, start_ref, carry_ref):
    ph, t = pl.program_id(0), pl.program_id(1)
    tm = route_ref.shape[1]
    rt = route_ref[...]
    e1 = rt[0:1, :].astype(jnp.int32)
    e2 = rt[1:2, :].astype(jnp.int32)
    sub = lax.broadcasted_iota(jnp.int32, (N_EXPERTS, tm), 0)
    hit1, hit2 = sub == e1, sub == e2
    onehot = jnp.where(hit1 | hit2, 1.0, 0.0)
    tile_cnt = jnp.sum(onehot, axis=1, keepdims=True)

    @pl.when((ph == 0) & (t == 0))
    def _():
        cnt_ref[...] = jnp.zeros_like(cnt_ref)

    @pl.when(ph == 0)
    def _():
        cnt_ref[...] += tile_cnt

    @pl.when((ph == 1) & (t == 0))
    def _():
        cnt = cnt_ref[...]
        padded = jnp.floor((cnt + (MOE_BLOCK - 1.0)) * (1.0 / MOE_BLOCK)) * MOE_BLOCK
        row = lax.broadcasted_iota(jnp.int32, cnt.shape, 0)
        incl = padded
        sh = 1
        while sh < N_EXPERTS:
            incl = incl + jnp.where(row >= sh, pltpu.roll(incl, sh, axis=0), 0.0)
            sh *= 2
        start_ref[...] = incl - padded
        carry_ref[...] = jnp.zeros_like(carry_ref)
        cnt_out_ref[...] = cnt

    @pl.when(ph == 1)
    def _():
        r_i = lax.broadcasted_iota(jnp.int32, (tm, tm), 0)
        c_i = lax.broadcasted_iota(jnp.int32, (tm, tm), 1)
        before = jnp.where(r_i < c_i, 1.0, 0.0).astype(BF16)
        rank = _dot(onehot.astype(BF16), before)
        base = start_ref[:, 0:1] + carry_ref[:, 0:1] + rank
        d1 = jnp.sum(jnp.where(hit1, base, 0.0), axis=0, keepdims=True)
        d2 = jnp.sum(jnp.where(hit2, base, 0.0), axis=0, keepdims=True)
        sub8 = lax.broadcasted_iota(jnp.int32, (SUBLANES, tm), 0)
        dest_ref[...] = jnp.where(sub8 == 0, d1, d2).astype(jnp.int32)
        carry_ref[...] += tile_cnt


def _plan(route_t, tm):
    rows = route_t.shape[1]
    n_t = rows // tm
    return pl.pallas_call(
        _plan_kernel,
        grid=(2, n_t),
        in_specs=[pl.BlockSpec((SUBLANES, tm), lambda ph, t: (0, t))],
        out_specs=[pl.BlockSpec((None, SUBLANES, tm), lambda ph, t: (t * ph, 0, 0)),
                   pl.BlockSpec((N_EXPERTS, LANES), lambda ph, t: (0, 0))],
        out_shape=[jax.ShapeDtypeStruct((n_t, SUBLANES, tm), jnp.int32),
                   jax.ShapeDtypeStruct((N_EXPERTS, LANES), F32)],
        scratch_shapes=[pltpu.VMEM((N_EXPERTS, LANES), F32)] * 3,
        compiler_params=_cparams("arbitrary", "arbitrary"),
        name="moe_plan",
    )(route_t)


def _row_copy(src_ref, src_row, dst_ref, dst_row, sem):
    return pltpu.make_async_copy(
        src_ref.at[pl.ds(pl.multiple_of(src_row * ROW_WORDS, ROW_WORDS), ROW_WORDS)],
        dst_ref.at[pl.ds(pl.multiple_of(dst_row * ROW_WORDS, ROW_WORDS), ROW_WORDS)], sem)


def _dispatch_kernel(dest_ref, h2_ref, xs_ref, sem, *, tm):
    def body(r4, c):
        for u in range(DMA_UNROLL):
            r = r4 * DMA_UNROLL + u
            for k in range(2):
                _row_copy(h2_ref, r, xs_ref, dest_ref[0, 0, k * tm + r], sem).start(priority=k)
        return c

    lax.fori_loop(0, tm // DMA_UNROLL, body, 0)
    for k in range(2):
        pltpu.make_async_copy(h2_ref, xs_ref.at[pl.ds(0, tm * ROW_WORDS)], sem).wait()


def _dispatch(dest, h2_tiles, n_slots, tm):
    n_t = dest.shape[0]
    return pl.pallas_call(
        functools.partial(_dispatch_kernel, tm=tm),
        grid=(n_t,),
        in_specs=[pl.BlockSpec((1, 1, 2 * tm), lambda i: (i, 0, 0), memory_space=pltpu.SMEM),
                  pl.BlockSpec((tm * ROW_WORDS, LANES), lambda i: (i, 0))],
        out_specs=pl.BlockSpec(memory_space=pl.ANY),
        out_shape=jax.ShapeDtypeStruct((n_slots * ROW_WORDS, LANES), jnp.uint32),
        scratch_shapes=[pltpu.SemaphoreType.DMA],
        compiler_params=_cparams("arbitrary"),
        name="moe_dispatch",
    )(dest, h2_tiles)


def _expert_kernel(be_ref, bn_ref, first_ref, slot_ref, nxt_ref, x_ref, wg_hbm, wu_hbm, wd_hbm, y_ref,
                   wg_buf, wu_buf, wd_buf, wg_bf, wu_bf, wd_bf, sem, *, layer):
    i = pl.program_id(0)
    mb = x_ref.shape[0] // ROW_WORDS
    half = wg_bf.shape[0] // 2

    def weight_copies(e, s):
        return [pltpu.make_async_copy(wg_hbm.at[layer, e], wg_buf.at[s], sem.at[s]),
                pltpu.make_async_copy(wu_hbm.at[layer, e], wu_buf.at[s], sem.at[s]),
                pltpu.make_async_copy(wd_hbm.at[layer, e], wd_buf.at[s], sem.at[s])]

    @pl.when(i == 0)
    def _():
        for cp in weight_copies(be_ref[0], 0):
            cp.start()

    @pl.when(first_ref[i] == 1)
    def _():
        s = slot_ref[i]
        for cp in weight_copies(be_ref[i], s):
            cp.wait()

        @pl.when(nxt_ref[i] >= 0)
        def _():
            for cp in weight_copies(nxt_ref[i], 1 - s):
                cp.start()

        wg_bf[...] = wg_buf[s].astype(BF16)
        wu_bf[...] = wu_buf[s].astype(BF16)
        wd_bf[...] = wd_buf[s].astype(BF16)

    def compute(m):
        live = lax.broadcasted_iota(jnp.int32, (m, 1), 0) < bn_ref[i]
        x_lo, x_hi = _load_rows(x_ref, 0, m)
        x_lo = jnp.where(live, x_lo, 0.0).astype(BF16)
        x_hi = jnp.where(live, x_hi, 0.0).astype(BF16)
        g = _dot(x_lo, wg_bf[0:half, :]) + _dot(x_hi, wg_bf[half:, :])
        u = _dot(x_lo, wu_bf[0:half, :]) + _dot(x_hi, wu_bf[half:, :])
        hid = (_silu(g) * u).astype(BF16)
        _store_rows(y_ref, _dot(hid, wd_bf[...]))

    for parts in range(1, mb // MOE_PART + 1):
        @pl.when((bn_ref[i] > (parts - 1) * MOE_PART) & (bn_ref[i] <= parts * MOE_PART))
        def _(parts=parts):
            compute(parts * MOE_PART)


def _experts(table, xs, w_gate, w_up, w_down, layer):
    n_blocks = table[0].shape[0]
    _, _, d, de = w_gate.shape
    blk = pl.BlockSpec((MOE_BLOCK * ROW_WORDS, LANES), lambda i, *_: (i, 0))
    hbm = pl.BlockSpec(memory_space=pl.ANY)
    return pl.pallas_call(
        functools.partial(_expert_kernel, layer=layer),
        grid_spec=pltpu.PrefetchScalarGridSpec(
            num_scalar_prefetch=len(table),
            grid=(n_blocks,),
            in_specs=[blk, hbm, hbm, hbm],
            out_specs=blk,
            scratch_shapes=[pltpu.VMEM((2, d, de), F32), pltpu.VMEM((2, d, de), F32), pltpu.VMEM((2, de, d), F32),
                            pltpu.VMEM((d, de), BF16), pltpu.VMEM((d, de), BF16), pltpu.VMEM((de, d), BF16),
                            pltpu.SemaphoreType.DMA((2,))]),
        out_shape=jax.ShapeDtypeStruct(xs.shape, jnp.uint32),
        compiler_params=_cparams("arbitrary"),
        name="moe_experts",
    )(*table, xs, w_gate, w_up, w_down)


def _combine_body(dest_ref, dest_next_ref, x1_ref, route_ref, m, lng_ref, lnb_ref, y_ref, o_ref, buf_ref, sem):
    tm, d = x1_ref.shape
    i = pl.program_id(0)
    n = pl.num_programs(0)
    slot = i % 2
    pr = tm // COMBINE_PARTS

    def gather(idx_ref, s, first, count):
        def body(r4, c):
            for u in range(DMA_UNROLL):
                r = first + r4 * DMA_UNROLL + u
                for k in range(2):
                    _row_copy(y_ref, idx_ref[0, 0, k * tm + r], buf_ref.at[s], k * tm + r,
                              sem.at[s]).start(priority=k)
            return c

        lax.fori_loop(0, count // DMA_UNROLL, body, 0)

    @pl.when(i == 0)
    def _():
        gather(dest_ref, 0, 0, tm)

    for k in range(2):
        pltpu.make_async_copy(y_ref.at[pl.ds(0, tm * ROW_WORDS)],
                              buf_ref.at[slot, pl.ds(0, tm * ROW_WORDS)], sem.at[slot]).wait()

    gate2 = m[:, 5 * d:6 * d]
    for part in range(COMBINE_PARTS):
        @pl.when(i + 1 < n)
        def _(part=part):
            gather(dest_next_ref, 1 - slot, part * pr, pr)

        rows = slice(part * pr, (part + 1) * pr)
        rt = route_ref[rows, :]
        f = jnp.zeros((pr, d), F32)
        for k in range(2):
            lo, hi = _load_rows(buf_ref.at[slot], k * tm + part * pr, pr)
            f = f + jnp.concatenate([lo, hi], axis=1) * rt[:, 2 + k:3 + k]
        o_ref[rows, :] = _layer_norm(RES_ALPHA * x1_ref[rows, :] + gate2 * f) * lng_ref[...] + lnb_ref[...]


def _combine_kernel(dest_ref, dest_next_ref, x1_ref, route_ref, mod_ref, lng_ref, lnb_ref, y_ref, o_ref,
                    buf_ref, sem, *, mod_row, rows_per_batch):
    m = _mod_row(mod_ref, mod_row, x1_ref.shape[0], rows_per_batch)
    _combine_body(dest_ref, dest_next_ref, x1_ref, route_ref, m, lng_ref, lnb_ref, y_ref, o_ref, buf_ref, sem)


def _combine_proj_kernel(dest_ref, dest_next_ref, x1_ref, route_ref, mod_ref, lng_ref, lnb_ref,
                         mod_next_ref, w_ref, cos_ref, sin_ref, y_ref, o_ref, *rest,
                         rows_per_batch, a_pitch):
    outs, (buf_ref, sem) = rest[:-2], rest[-2:]
    tm = x1_ref.shape[0]
    _combine_body(dest_ref, dest_next_ref, x1_ref, route_ref, _mod_row(mod_ref, None, tm, rows_per_batch),
                  lng_ref, lnb_ref, y_ref, o_ref, buf_ref, sem)
    _proj_body(o_ref[...], _mod_row(mod_next_ref, None, tm, rows_per_batch), w_ref, cos_ref, sin_ref, outs,
               rope=True, a_pitch=a_pitch)


def _combine_specs(tm, d, off, steps):
    return [pl.BlockSpec((1, 1, 2 * tm), lambda i: (off + i, 0, 0), memory_space=pltpu.SMEM),
            pl.BlockSpec((1, 1, 2 * tm), lambda i: (off + jnp.minimum(i + 1, steps - 1), 0, 0),
                         memory_space=pltpu.SMEM),
            pl.BlockSpec((tm, d), lambda i: (off + i, 0)),
            pl.BlockSpec((tm, LANES), lambda i: (off + i, 0))]


def _combine_scratch(tm):
    return [pltpu.VMEM((2, 2 * tm * ROW_WORDS, LANES), jnp.uint32), pltpu.SemaphoreType.DMA((2,))]


def _combine(dest, x1, route, mod, ln_g, ln_b, y_tiles, *, tm, row_off, rows, mod_row, rows_per_batch):
    d = x1.shape[1]
    steps = rows // tm
    kern = functools.partial(_combine_kernel, mod_row=mod_row, rows_per_batch=rows_per_batch)
    const = lambda a: pl.BlockSpec(a.shape, lambda i: (0,) * a.ndim)
    return pl.pallas_call(
        kern,
        grid=(steps,),
        in_specs=_combine_specs(tm, d, row_off // tm, steps)
                 + [const(mod), const(ln_g), const(ln_b), pl.BlockSpec(memory_space=pl.ANY)],
        out_specs=pl.BlockSpec((tm, d), lambda i: (i, 0)),
        out_shape=jax.ShapeDtypeStruct((rows, d), F32),
        scratch_shapes=_combine_scratch(tm),
        compiler_params=_cparams("arbitrary"),
        name="moe_combine",
    )(dest, dest, x1, route, mod, ln_g, ln_b, y_tiles)


def _combine_proj(dest, x1, route, mod, ln_g, ln_b, y_tiles, mod_next, w_in, cos_t, sin_t, *,
                  tm, rows, rows_per_batch, a_pitch):
    d = x1.shape[1]
    steps = rows // tm
    kern = functools.partial(_combine_proj_kernel, rows_per_batch=rows_per_batch, a_pitch=a_pitch)
    const = lambda a: pl.BlockSpec(a.shape, lambda i: (0,) * a.ndim)
    table_spec, proj_specs, proj_shapes = _proj_specs(rows, tm, a_pitch, cos_t.shape[0] // tm)
    outs = pl.pallas_call(
        kern,
        grid=(steps,),
        in_specs=_combine_specs(tm, d, 0, steps)
                 + [const(mod), const(ln_g), const(ln_b), const(mod_next), const(w_in), table_spec, table_spec,
                    pl.BlockSpec(memory_space=pl.ANY)],
        out_specs=[pl.BlockSpec((tm, d), lambda i: (i, 0))] + proj_specs,
        out_shape=[jax.ShapeDtypeStruct((rows, d), F32)] + proj_shapes,
        scratch_shapes=_combine_scratch(tm),
        compiler_params=_cparams("arbitrary"),
        name="moe_combine_proj",
    )(dest, dest, x1, route, mod, ln_g, ln_b, mod_next, w_in, cos_t, sin_t, y_tiles)
    return outs[0], outs[1:]


def _rope_tables(n_pos):
    rows = n_pos // GRID_W
    row = jnp.repeat(jnp.arange(rows), GRID_W).astype(F32)
    col = jnp.tile(jnp.arange(GRID_W), rows).astype(F32)
    n_freq = HEAD_DIM // 4
    freq = ROPE_BASE ** (-jnp.arange(n_freq, dtype=F32) / n_freq)
    ang_r, ang_c = row[:, None] * freq, col[:, None] * freq
    cos_h = jnp.concatenate([jnp.cos(ang_r)] * 2 + [jnp.cos(ang_c)] * 2, axis=1)
    sin_h = jnp.concatenate([-jnp.sin(ang_r), jnp.sin(ang_r), -jnp.sin(ang_c), jnp.sin(ang_c)], axis=1)
    return jnp.tile(cos_h, (1, 2)), jnp.tile(sin_h, (1, 2))


def _block_table(counts, n_blocks):
    cnt = counts.astype(jnp.int32)
    padded = (cnt + MOE_BLOCK - 1) // MOE_BLOCK * MOE_BLOCK
    pad_end = jnp.cumsum(padded)
    pad_start = pad_end - padded
    blk_start = jnp.arange(n_blocks, dtype=jnp.int32)[:, None] * MOE_BLOCK
    be = jnp.minimum(jnp.sum((pad_end[None, :] <= blk_start).astype(jnp.int32), axis=1), N_EXPERTS - 1)
    ids = jnp.arange(N_EXPERTS, dtype=jnp.int32)
    mine = be[:, None] == ids[None, :]
    fill = jnp.sum(jnp.where(mine, cnt[None, :] + pad_start[None, :], 0), axis=1) - blk_start[:, 0]
    bn = jnp.clip(fill, 0, MOE_BLOCK)
    prev = jnp.concatenate([jnp.full((1,), -1, jnp.int32), be[:-1]])
    first = ((bn > 0) & (be != prev)).astype(jnp.int32)
    slot = (jnp.cumsum(first) - 1) % 2
    later = (ids[None, :] > ids[:, None]) & (cnt[None, :] > 0)
    nxt_e = jnp.min(jnp.where(later, ids[None, :], N_EXPERTS), axis=1)
    nxt_e = jnp.where(nxt_e == N_EXPERTS, -1, nxt_e)
    nxt = jnp.sum(jnp.where(mine, nxt_e[None, :], 0), axis=1)
    return be, bn, first, slot.astype(jnp.int32), nxt.astype(jnp.int32)


def _moe(route_t, h2_tiles, w_gate, w_up, w_down, layer):
    rows = route_t.shape[1]
    tm = ROW_TILE
    n_blocks = -(-(2 * rows) // MOE_BLOCK) + N_EXPERTS
    dest8, counts = _plan(route_t, tm)
    dest = dest8[:, 0:2, :].reshape(rows // tm, 1, 2 * tm)
    table = _block_table(counts[:, 0], n_blocks)
    xs = _dispatch(dest, h2_tiles, n_blocks * MOE_BLOCK, tm)
    ys = _experts(table, xs, w_gate, w_up, w_down, layer)
    tc = COMBINE_TILE
    dest_c = dest8[:, 0:2, :].reshape(rows // tm, 2, tm // tc, tc).transpose(0, 2, 1, 3).reshape(rows // tc, 1, 2 * tc)
    return dest_c, ys


def kernel(x, c, ctx, c_ctx, w_ada, b_ada, w_in, w_fourier, attn_sink, w_pool, pool_scale, w_sgu, b_sgu,
           w_out, ln1_g, ln1_b, w_router_group, w_router_expert, w_exp_gate, w_exp_up, w_exp_down,
           ln2_g, ln2_b):
    b, s, d = x.shape
    n_ctx = ctx.shape[1]
    n_layers = w_in.shape[0]
    tm = ROW_TILE
    cond = jnp.concatenate([c, c_ctx[None, :], jnp.zeros((SUBLANES - b - 1, d), F32)], axis=0)
    mod_all = _ada(cond, w_ada, b_ada[:, None, :])
    cos_t, sin_t = _rope_tables(s)
    x2 = x.reshape(b * s, d)
    c2 = ctx.reshape(b * n_ctx, d)
    n_sgu = w_sgu.shape[1]
    lat = None
    for layer in range(n_layers):
        last = layer == n_layers - 1
        mod = mod_all[layer]
        w_in_l = w_in[layer].astype(BF16)
        wf = w_fourier[layer].astype(BF16)
        w_pool_bd = jax.scipy.linalg.block_diag(*[w_pool[layer, g] for g in range(w_pool.shape[1])]).astype(BF16)
        w_sgu_stack = w_sgu[layer].reshape(n_sgu * SGU_CHUNK, SGU_CHUNK).astype(BF16)
        b_sgu_exp = jnp.repeat(b_sgu[layer].T, GROUP_W // n_sgu, axis=1)
        w_router = jnp.concatenate([w_router_expert[layer].reshape(d, N_EXPERTS), w_router_group[layer]], axis=1)
        w_router = jnp.pad(w_router, ((0, 0), (0, LANES - w_router.shape[1])))
        wr_hi = w_router.astype(BF16)
        w_route = jnp.concatenate([wr_hi, (w_router - wr_hi.astype(F32)).astype(BF16)], axis=1)
        merge_consts = (mod, w_pool_bd, pool_scale[layer][None, :], w_sgu_stack, b_sgu_exp,
                        w_out[layer].astype(BF16), ln1_g[layer][None, :], ln1_b[layer][None, :], w_route)
        sink = attn_sink[layer]

        if lat is None:
            lat = _proj(x2, mod, w_in_l, cos_t, sin_t, mod_row=None, rows_per_batch=s,
                        rope=True, tm=tm, a_pitch=FFT_PITCH)
        a, q, qs, k, v, p, ug = lat
        ac, qc, qsc, kc, vc, pc, ugc = _proj(c2, mod, w_in_l, cos_t, sin_t, mod_row=b, rows_per_batch=n_ctx,
                                             rope=False, tm=n_ctx, a_pitch=FFT_R)
        y_four = _fourier(a, wf, s)
        y_attn = _attention(sink, q, qs, k, v, kc, vc, seq=s, n_ctx=n_ctx, band=True)
        total = b * s + (0 if last else b * n_ctx)
        merged = _merge(x2, p, ug, y_four, y_attn, *merge_consts, (),
                        mod_row=None, seq=s, tm=tm, row_off=0, total_rows=total)
        if not last:
            yc_four = _fourier_small(ac, wf, n_ctx)
            yc_attn = _attention(sink, qc, qsc, kc, vc, kc, vc, seq=n_ctx, n_ctx=n_ctx, band=False)
            merged = _merge(c2, pc, ugc, yc_four, yc_attn, *merge_consts, tuple(merged),
                            mod_row=b, seq=n_ctx, tm=n_ctx, row_off=b * s, total_rows=total)
        x1, h2_tiles, route, route_t = merged
        dest, ys = _moe(route_t, h2_tiles, w_exp_gate, w_exp_up, w_exp_down, layer)
        ln_g, ln_b = ln2_g[layer][None, :], ln2_b[layer][None, :]
        if last:
            x2 = _combine(dest, x1, route, mod, ln_g, ln_b, ys, tm=COMBINE_TILE, row_off=0, rows=b * s,
                          mod_row=None, rows_per_batch=s)
        else:
            x2, lat = _combine_proj(dest, x1, route, mod, ln_g, ln_b, ys, mod_all[layer + 1],
                                    w_in[layer + 1].astype(BF16), cos_t, sin_t,
                                    tm=COMBINE_TILE, rows=b * s, rows_per_batch=s, a_pitch=FFT_PITCH)
            c2 = _combine(dest, x1, route, mod, ln_g, ln_b, ys, tm=COMBINE_TILE, row_off=b * s,
                          rows=b * n_ctx, mod_row=b, rows_per_batch=n_ctx)
    return x2.reshape(b, s, d)
```

```python
import functools
import math

import numpy as np
import jax
import jax.numpy as jnp
from jax import lax
from jax.experimental import pallas as pl
from jax.experimental.pallas import tpu as pltpu

GRID_W = 64
HEAD_DIM = 64
GROUP_W = 256
KV_W = 128
WINDOW = 128
POOL_WINDOWS = (2, 4, 8, 16)
SGU_CHUNK = 128
N_GROUPS = 4
EXPERTS_PER_GROUP = 8
N_EXPERTS = 32
ROPE_BASE = 10000.0
LN_EPS = 1e-6
NEG_INF = -1e30
DEPTH = 2
RES_ALPHA = (2 * DEPTH) ** 0.25

LANES = 128
SUBLANES = 8
VMEM_LIMIT = 48 * 1024 * 1024

ROW_TILE = 1024
Q_BLOCK = 1024
COMBINE_TILE = 512
COMBINE_PARTS = 1
WEIGHT_SLOTS = 3
MOE_BLOCK = 512
MOE_PART = 256
FFT_R = 64
FFT_PITCH = 72
FFT_UNROLL = 8
DMA_UNROLL = 8
MERGE_PART = 256

BF16 = jnp.bfloat16
F32 = jnp.float32


def _cparams(*sem):
    return pltpu.CompilerParams(dimension_semantics=sem, vmem_limit_bytes=VMEM_LIMIT)


def _dot(a, b):
    return jnp.dot(a, b, preferred_element_type=F32)


def _dot_nt(a, b):
    return lax.dot_general(a, b, (((1,), (1,)), ((), ())), preferred_element_type=F32)


def _layer_norm(t):
    mu = jnp.mean(t, axis=-1, keepdims=True)
    d = t - mu
    var = jnp.mean(d * d, axis=-1, keepdims=True)
    return d * lax.rsqrt(var + LN_EPS)


def _silu(t):
    return t * (1.0 / (1.0 + jnp.exp(-t)))


def _gelu(t):
    return 0.5 * t * (1.0 + lax.erf(t * (1.0 / math.sqrt(2.0))))


ROW_WORDS = 4
HI_MASK = 0xFFFF0000


def _pack_rows(t):
    half = t.shape[1] // 2
    lo = lax.bitcast_convert_type(t[:, :half].astype(BF16).astype(F32), jnp.uint32)
    hi = lax.bitcast_convert_type(t[:, half:].astype(BF16).astype(F32), jnp.uint32)
    return (lo >> 16) | (hi & jnp.uint32(HI_MASK))


def _unpack_rows(w):
    return (lax.bitcast_convert_type(w << 16, F32),
            lax.bitcast_convert_type(w & jnp.uint32(HI_MASK), F32))


def _store_rows(ref, t, first=0):
    w = _pack_rows(t)
    for j in range(ROW_WORDS):
        ref[pl.ds(first * ROW_WORDS + j, t.shape[0], stride=ROW_WORDS), :] = w[:, j * LANES:(j + 1) * LANES]


def _load_rows(ref, first, m):
    w = jnp.concatenate([ref[pl.ds(first * ROW_WORDS + j, m, stride=ROW_WORDS), :] for j in range(ROW_WORDS)],
                        axis=1)
    return _unpack_rows(w)


def _ada_kernel(c_ref, w_ref, b_ref, o_ref):
    s = _silu(c_ref[...]).astype(BF16)
    o_ref[...] = _dot(s, w_ref[...].astype(BF16)) + b_ref[...]


def _ada(cond, w_ada, b_ada):
    n_layers, d, n = w_ada.shape
    tn = n // 4
    return pl.pallas_call(
        _ada_kernel,
        grid=(n_layers, n // tn),
        in_specs=[
            pl.BlockSpec((SUBLANES, d), lambda l, j: (0, 0)),
            pl.BlockSpec((None, d, tn), lambda l, j: (l, 0, j)),
            pl.BlockSpec((None, 1, tn), lambda l, j: (l, 0, j)),
        ],
        out_specs=pl.BlockSpec((None, SUBLANES, tn), lambda l, j: (l, 0, j)),
        out_shape=jax.ShapeDtypeStruct((n_layers, SUBLANES, n), F32),
        compiler_params=_cparams("arbitrary", "arbitrary"),
        name="ada",
    )(cond, w_ada, b_ada)


def _rope(t, cos_t, sin_t):
    lane = lax.broadcasted_iota(jnp.int32, t.shape, 1)
    first = (lane % 32) < 16
    partner = jnp.where(first, pltpu.roll(t, LANES - 16, axis=1), pltpu.roll(t, 16, axis=1))
    return t * cos_t + partner * sin_t


def _proj_body(x, m, w_ref, cos_ref, sin_ref, outs, *, rope, a_pitch):
    a_ref, q_ref, qs_ref, k_ref, v_ref, p_ref, ug_ref = outs
    tm, d = x.shape
    shift, scale = m[:, 0:d], m[:, d:2 * d]
    h = _layer_norm(x) * (1.0 + scale) + shift
    z = _dot(h.astype(BF16), w_ref[...])
    pad = jnp.zeros((a_pitch - FFT_R, LANES), F32)
    for g in range(tm // FFT_R):
        for hf in range(2):
            grp = z[g * FFT_R:(g + 1) * FFT_R, hf * LANES:(hf + 1) * LANES]
            if a_pitch > FFT_R:
                grp = jnp.concatenate([grp, pad], axis=0)
            a_ref[hf, g * a_pitch:(g + 1) * a_pitch, :] = grp
    q0, q1 = z[:, 256:384], z[:, 384:512]
    k = z[:, 512:640]
    if rope:
        cos_t, sin_t = cos_ref[...], sin_ref[...]
        q0, q1, k = _rope(q0, cos_t, sin_t), _rope(q1, cos_t, sin_t), _rope(k, cos_t, sin_t)
    q_ref[:, 0:128] = q0.astype(BF16)
    q_ref[:, 128:256] = q1.astype(BF16)
    qs_ref[:, 0:128] = pltpu.roll(q0, HEAD_DIM, axis=1).astype(BF16)
    qs_ref[:, 128:256] = pltpu.roll(q1, HEAD_DIM, axis=1).astype(BF16)
    k_ref[...] = k.astype(BF16)
    v_ref[...] = z[:, 640:768].astype(BF16)
    p_ref[...] = z[:, 768:1024].astype(BF16)
    ug_ref[...] = z[:, 1024:1536].astype(BF16)


def _mod_row(mod_ref, mod_row, tm, rows_per_batch):
    row = (pl.program_id(0) * tm) // rows_per_batch if mod_row is None else mod_row
    return mod_ref[pl.ds(row, 1), :]


def _proj_kernel(x_ref, mod_ref, w_ref, cos_ref, sin_ref, *outs, mod_row, rows_per_batch, rope, a_pitch):
    m = _mod_row(mod_ref, mod_row, x_ref.shape[0], rows_per_batch)
    _proj_body(x_ref[...], m, w_ref, cos_ref, sin_ref, outs, rope=rope, a_pitch=a_pitch)


def _proj_specs(rows, tm, a_pitch, seq_steps):
    row_spec = lambda w: pl.BlockSpec((tm, w), lambda i: (i, 0))
    out_w = (256, 256, 128, 128, 256, 512)
    ta = tm // FFT_R * a_pitch
    out_specs = [pl.BlockSpec((2, ta, LANES), lambda i: (0, i, 0))] + [row_spec(w) for w in out_w]
    out_shape = ([jax.ShapeDtypeStruct((2, rows // FFT_R * a_pitch, LANES), F32)]
                 + [jax.ShapeDtypeStruct((rows, w), BF16) for w in out_w])
    table_spec = pl.BlockSpec((tm, LANES), lambda i: (i % seq_steps, 0))
    return table_spec, out_specs, out_shape


def _proj(x2, mod, w_in, cos_t, sin_t, *, mod_row, rows_per_batch, rope, tm, a_pitch):
    rows, d = x2.shape
    kern = functools.partial(_proj_kernel, mod_row=mod_row, rows_per_batch=rows_per_batch,
                             rope=rope, a_pitch=a_pitch)
    table_spec, out_specs, out_shape = _proj_specs(rows, tm, a_pitch, cos_t.shape[0] // tm)
    return pl.pallas_call(
        kern,
        grid=(rows // tm,),
        in_specs=[
            pl.BlockSpec((tm, d), lambda i: (i, 0)),
            pl.BlockSpec(mod.shape, lambda i: (0, 0)),
            pl.BlockSpec(w_in.shape, lambda i: (0, 0)),
            table_spec, table_spec,
        ],
        out_specs=out_specs,
        out_shape=out_shape,
        compiler_params=_cparams("arbitrary"),
        name="proj",
    )(x2, mod, w_in, cos_t, sin_t)


def _fft_tables(n_pos):
    r = FFT_R
    assert n_pos == r * r
    kb = np.arange(r)[None, :, None]
    na = np.arange(r)[:, None, None]
    nb = np.arange(r)[None, None, :]
    ang = 2.0 * np.pi * ((kb * (na + r * nb)) % n_pos) / n_pos
    m1 = np.concatenate([np.cos(ang), -np.sin(ang)], axis=1)
    ka = np.arange(r)[:, None]
    n2 = np.arange(r)[None, :]
    ang2 = 2.0 * np.pi * ((ka * n2) % r) / r
    c2, s2 = np.cos(ang2), np.sin(ang2)
    w2 = np.block([[c2, s2], [-s2, c2]])
    return m1, w2


def _channel_tables(n_pos):
    h = HEAD_DIM
    c = np.arange(h)
    ang = 2.0 * np.pi * ((c[:, None] * c[None, :]) % h) / h
    scale = 1.0 / math.sqrt(n_pos * h)
    eye = np.eye(GROUP_W // h)
    cc = np.kron(eye, np.cos(ang)) * scale
    ss = np.kron(eye, np.sin(ang)) * scale
    return np.concatenate([cc, ss], axis=0)


def _fourier_kernel(a_ref, m1_ref, w2_ref, ch_ref, wf_ref, o_ref, z_ref, y_ref):
    r, pt = FFT_R, FFT_PITCH

    def step1(i, c):
        for u in range(FFT_UNROLL):
            na = i * FFT_UNROLL + u
            rows = jnp.concatenate([a_ref[0, pl.ds(na, r, stride=pt), :],
                                    a_ref[1, pl.ds(na, r, stride=pt), :]], axis=1)
            z = _dot(m1_ref[na], rows.astype(BF16))
            base = pl.multiple_of(na * pt, SUBLANES)
            z_ref[0, pl.ds(base, r), :] = z[0:r, 0:LANES]
            z_ref[1, pl.ds(base, r), :] = z[0:r, LANES:]
            z_ref[2, pl.ds(base, r), :] = z[r:, 0:LANES]
            z_ref[3, pl.ds(base, r), :] = z[r:, LANES:]
        return c

    lax.fori_loop(0, r // FFT_UNROLL, step1, 0)

    def step2(i, c):
        for u in range(FFT_UNROLL):
            kb = i * FFT_UNROLL + u
            q = [z_ref[j, pl.ds(kb, r, stride=pt), :] for j in range(4)]
            zs = jnp.concatenate([jnp.concatenate(q[0:2], axis=1),
                                  jnp.concatenate(q[2:4], axis=1)], axis=0)
            y = _dot(w2_ref[...], zs.astype(BF16))
            base = pl.multiple_of(kb * r, r)
            y_ref[0, pl.ds(base, r), :] = y[0:r, 0:LANES]
            y_ref[1, pl.ds(base, r), :] = y[0:r, LANES:]
            y_ref[2, pl.ds(base, r), :] = y[r:, 0:LANES]
            y_ref[3, pl.ds(base, r), :] = y[r:, LANES:]
        return c

    lax.fori_loop(0, r // FFT_UNROLL, step2, 0)

    chunk = 8 * r
    for cidx in range(r * r // chunk):
        yy = jnp.concatenate([y_ref[j, cidx * chunk:(cidx + 1) * chunk, :] for j in range(4)], axis=1)
        f = _dot(yy.astype(BF16), ch_ref[...])
        g = _dot(f.astype(BF16), wf_ref[...])
        for gi in range(chunk // r):
            kb = cidx * (chunk // r) + gi
            z_ref[0, kb * pt:kb * pt + r, :] = g[gi * r:(gi + 1) * r, 0:LANES]
            z_ref[1, kb * pt:kb * pt + r, :] = g[gi * r:(gi + 1) * r, LANES:]

    def step3(i, c):
        for u in range(FFT_UNROLL):
            ka = i * FFT_UNROLL + u
            base = pl.multiple_of(ka * r, r)
            o_ref[pl.ds(base, r), 0:LANES] = z_ref[0, pl.ds(ka, r, stride=pt), :]
            o_ref[pl.ds(base, r), LANES:] = z_ref[1, pl.ds(ka, r, stride=pt), :]
        return c

    lax.fori_loop(0, r // FFT_UNROLL, step3, 0)


def _fourier(a3, w_fourier, n_pos):
    rows = a3.shape[1] // FFT_PITCH * FFT_R
    gw = GROUP_W
    m1, w2 = _fft_tables(n_pos)
    ch = _channel_tables(n_pos)
    const = lambda shape: pl.BlockSpec(shape, lambda b: (0,) * len(shape))
    return pl.pallas_call(
        _fourier_kernel,
        grid=(rows // n_pos,),
        in_specs=[
            pl.BlockSpec((2, FFT_R * FFT_PITCH, LANES), lambda b: (0, b, 0)),
            const(m1.shape), const(w2.shape), const(ch.shape), const(w_fourier.shape),
        ],
        out_specs=pl.BlockSpec((n_pos, gw), lambda b: (b, 0)),
        out_shape=jax.ShapeDtypeStruct((rows, gw), F32),
        scratch_shapes=[pltpu.VMEM((4, FFT_R * FFT_PITCH, LANES), F32), pltpu.VMEM((4, n_pos, LANES), F32)],
        compiler_params=_cparams("arbitrary"),
        name="fourier",
    )(a3, jnp.asarray(m1, BF16), jnp.asarray(w2, BF16), jnp.asarray(ch, BF16), w_fourier)


def _fourier_small_kernel(a_ref, cs_ref, ch_ref, wf_ref, o_ref):
    n = a_ref.shape[1]
    a = jnp.concatenate([a_ref[0], a_ref[1]], axis=1)
    pq = _dot(cs_ref[...], a.astype(BF16))
    y = jnp.concatenate([pq[0:n], pq[n:2 * n]], axis=1).astype(BF16)
    f = _dot(y, ch_ref[...])
    o_ref[...] = _dot(f.astype(BF16), wf_ref[...])


def _fourier_small(a3, w_fourier, n_pos):
    _, rows, _ = a3.shape
    gw = GROUP_W
    k = np.arange(n_pos)
    ang = 2.0 * np.pi * ((k[:, None] * k[None, :]) % n_pos) / n_pos
    cs = np.concatenate([np.cos(ang), -np.sin(ang)], axis=0)
    ch = _channel_tables(n_pos)
    const = lambda shape: pl.BlockSpec(shape, lambda b: (0,) * len(shape))
    return pl.pallas_call(
        _fourier_small_kernel,
        grid=(rows // n_pos,),
        in_specs=[pl.BlockSpec((2, n_pos, LANES), lambda b: (0, b, 0)),
                  const(cs.shape), const(ch.shape), const(w_fourier.shape)],
        out_specs=pl.BlockSpec((n_pos, gw), lambda b: (b, 0)),
        out_shape=jax.ShapeDtypeStruct((rows, gw), F32),
        compiler_params=_cparams("arbitrary"),
        name="fourier_ctx",
    )(a3, jnp.asarray(cs, BF16), jnp.asarray(ch, BF16), w_fourier)


ATTN_SUB = 128


def _attn_kernel(sink_ref, q_ref, qs_ref, k_ref, v_ref, kc_ref, vc_ref, o_ref, *, band, seq):
    qb = q_ref.shape[0]
    sub = ATTN_SUB
    lane = lax.broadcasted_iota(jnp.int32, (1, LANES), 1)
    lo_half = lane < HEAD_DIM
    zero = jnp.zeros((), BF16)
    scale = jnp.asarray(HEAD_DIM ** -0.5, BF16)
    kw = sub + 2 * WINDOW
    for sb in range(qb // sub):
        rows = slice(sb * sub, (sb + 1) * sub)
        qa0, qa1 = q_ref[rows, 0:LANES], q_ref[rows, LANES:]
        qs0, qs1 = qs_ref[rows, 0:LANES], qs_ref[rows, LANES:]
        q_all = jnp.concatenate([jnp.where(lo_half, qa0, zero), jnp.where(lo_half, qs0, zero),
                                 jnp.where(lo_half, zero, qs1), jnp.where(lo_half, zero, qa1)], axis=0) * scale
        if band:
            p0 = pl.program_id(1) * qb + sb * sub
            start = pl.multiple_of(jnp.clip(p0 - WINDOW, 0, seq - kw), WINDOW)
            qpos = p0 + lax.broadcasted_iota(jnp.int32, (sub, 1), 0)
            kpos = start + lax.broadcasted_iota(jnp.int32, (1, kw), 1)
            bias = jnp.where(jnp.abs(qpos - kpos) <= WINDOW, 0.0, NEG_INF)
            keys = jnp.concatenate([k_ref[pl.ds(start, kw), :], kc_ref[...]], axis=0)
            vals = jnp.concatenate([v_ref[pl.ds(start, kw), :], vc_ref[...]], axis=0)
        else:
            keys, vals = kc_ref[...], vc_ref[...]
        s_all = _dot_nt(q_all, keys)
        probs, dens = [], []
        for h in range(4):
            s = s_all[h * sub:(h + 1) * sub, :]
            sink = sink_ref[h]
            if band:
                s = jnp.concatenate([s[:, 0:kw] + bias, s[:, kw:]], axis=1)
            m = jnp.maximum(jnp.max(s, axis=1, keepdims=True), sink)
            p = jnp.exp(s - m)
            dens.append(jnp.sum(p, axis=1, keepdims=True) + jnp.exp(sink - m))
            probs.append(p.astype(BF16))
        o_all = _dot(jnp.concatenate(probs, axis=0), vals)
        o = [o_all[h * sub:(h + 1) * sub, :] / dens[h] for h in range(4)]
        o_ref[rows, 0:LANES] = jnp.where(lo_half, o[0], pltpu.roll(o[1], HEAD_DIM, axis=1)).astype(BF16)
        o_ref[rows, LANES:] = jnp.where(lo_half, pltpu.roll(o[2], HEAD_DIM, axis=1), o[3]).astype(BF16)


def _attention(sink, q, qs, k, v, kc, vc, *, seq, n_ctx, band):
    rows = q.shape[0]
    n_batch = rows // seq
    qb = Q_BLOCK if band else seq
    steps = seq // qb
    kern = functools.partial(_attn_kernel, band=band, seq=seq)
    seq_spec = pl.BlockSpec((seq, KV_W), lambda b, i: (b, 0))
    ctx_spec = pl.BlockSpec((n_ctx, KV_W), lambda b, i: (b, 0))
    q_spec = pl.BlockSpec((qb, GROUP_W), lambda b, i: (b * steps + i, 0))
    return pl.pallas_call(
        kern,
        grid=(n_batch, steps),
        in_specs=[pl.BlockSpec(memory_space=pltpu.SMEM), q_spec, q_spec,
                  seq_spec, seq_spec, ctx_spec, ctx_spec],
        out_specs=q_spec,
        out_shape=jax.ShapeDtypeStruct((rows, GROUP_W), BF16),
        compiler_params=_cparams("arbitrary", "arbitrary"),
        name="attn" if band else "attn_ctx",
    )(sink, q, qs, k, v, kc, vc)


POOL_HALO = max(POOL_WINDOWS) // 2


def _pool(p_ref, t0, tm, seq):
    halo = POOL_HALO
    pack = 2 * SUBLANES
    t0 = pl.multiple_of(t0, pack)
    main = p_ref[pl.ds(t0, tm), :].astype(F32)
    lo = pl.multiple_of(jnp.maximum(t0 - pack, 0), pack)
    hi = pl.multiple_of(jnp.minimum(t0 + tm, seq - pack), pack)
    prev = p_ref[pl.ds(lo, pack), :].astype(F32)[pack - halo:, :]
    nxt = p_ref[pl.ds(hi, pack), :].astype(F32)[:halo, :]
    prev = jnp.where(t0 > 0, prev, 0.0)
    nxt = jnp.where(t0 + tm < seq, nxt, 0.0)
    full = jnp.concatenate([prev, main, nxt], axis=0)
    n = tm + 2 * halo
    gch = GROUP_W // len(POOL_WINDOWS)
    first = lax.broadcasted_iota(jnp.int32, (1, LANES), 1) < gch
    means = []
    for hf in range(GROUP_W // LANES):
        wa, wb = POOL_WINDOWS[2 * hf], POOL_WINDOWS[2 * hf + 1]
        x = full[:, hf * LANES:(hf + 1) * LANES]
        sums, w, s = {}, 2, pltpu.roll(x, 1, axis=0) + x
        sums[w] = s
        while w < wb:
            s = pltpu.roll(s, w // 2, axis=0) + pltpu.roll(s, n - w // 2, axis=0)
            w *= 2
            sums[w] = s
        means.append(jnp.where(first, sums[wa] * (1.0 / wa), sums[wb] * (1.0 / wb))[halo:halo + tm, :])
    mean = jnp.concatenate(means, axis=1)
    win = jnp.concatenate([jnp.full((1, gch), w, jnp.int32) for w in POOL_WINDOWS], axis=1)

    def rescale(rows, first_pos):
        pos = first_pos + lax.broadcasted_iota(jnp.int32, (halo, 1), 0)
        cnt = jnp.minimum(pos + win // 2, seq) - jnp.maximum(pos - win // 2, 0)
        return rows * (win.astype(F32) / cnt.astype(F32))

    mean = jnp.concatenate([rescale(mean[:halo], t0), mean[halo:tm - halo],
                            rescale(mean[tm - halo:], t0 + tm - halo)], axis=0)
    return mean - main


def _route(logits):
    tm = logits.shape[0]
    lt = logits.T
    gl = lt[N_EXPERTS:N_EXPERTS + N_GROUPS]
    sub_g = lax.broadcasted_iota(jnp.int32, gl.shape, 0)
    gmax = jnp.max(gl, axis=0, keepdims=True)
    grp = jnp.min(jnp.where(gl == gmax, sub_g, N_GROUPS), axis=0, keepdims=True)
    gate_group = 1.0 / jnp.sum(jnp.exp(gl - gmax), axis=0, keepdims=True)
    el = lt[0:EXPERTS_PER_GROUP]
    for g in range(1, N_GROUPS):
        el = jnp.where(grp == g, lt[g * EXPERTS_PER_GROUP:(g + 1) * EXPERTS_PER_GROUP], el)
    sub = lax.broadcasted_iota(jnp.int32, el.shape, 0)
    m1 = jnp.max(el, axis=0, keepdims=True)
    i1 = jnp.min(jnp.where(el == m1, sub, EXPERTS_PER_GROUP), axis=0, keepdims=True)
    el2 = jnp.where(sub == i1, -jnp.inf, el)
    m2 = jnp.max(el2, axis=0, keepdims=True)
    i2 = jnp.min(jnp.where(el2 == m2, sub, EXPERTS_PER_GROUP), axis=0, keepdims=True)
    r = jnp.exp(m2 - m1)
    g1 = gate_group / (1.0 + r)
    g2 = g1 * r
    e1 = (grp * EXPERTS_PER_GROUP + i1).astype(F32)
    e2 = (grp * EXPERTS_PER_GROUP + i2).astype(F32)
    rows = jnp.where(sub == 0, e1, jnp.where(sub == 1, e2, jnp.where(sub == 2, g1, jnp.where(sub == 3, g2, 0.0))))
    cols = jnp.concatenate([rows, jnp.zeros((LANES - rows.shape[0], tm), F32)], axis=0).T
    return cols, rows


def _merge_kernel(x_ref, p_ref, ug_ref, yf_ref, ya_ref, mod_ref, wpool_ref, pscale_ref, wsgu_ref, bsgu_ref,
                  wout_ref, lng_ref, lnb_ref, wr_ref, *rest,
                  mod_row, seq, n_alias):
    x1_ref, h2_ref, route_ref, route_t_ref = rest[n_alias:]
    tm, d = x_ref.shape
    if mod_row is None:
        row = pl.program_id(0)
    else:
        row = mod_row
    t0 = pl.multiple_of(pl.program_id(1) * tm, tm)
    m = mod_ref[pl.ds(row, 1), :]
    gate1, shift2, scale2 = m[:, 2 * d:3 * d], m[:, 3 * d:4 * d], m[:, 4 * d:5 * d]
    lane = lax.broadcasted_iota(jnp.int32, (1, GROUP_W), 1)
    n_heads = wsgu_ref.shape[0] // SGU_CHUNK
    head = lane // (GROUP_W // n_heads)

    pm = min(tm, MERGE_PART)
    for part in range(tm // pm):
        r0 = part * pm
        rows = slice(r0, r0 + pm)
        pooled = _pool(p_ref, t0 + r0, pm, seq)
        y_pool = _dot(pooled.astype(BF16), wpool_ref[...]) * pscale_ref[...]

        ug = ug_ref[rows, :].astype(F32)
        u = _gelu(ug[:, 0:GROUP_W])
        v = _layer_norm(_gelu(ug[:, GROUP_W:])).astype(BF16)
        mixed = []
        for cidx in range(pm // SGU_CHUNK):
            vc = v[cidx * SGU_CHUNK:(cidx + 1) * SGU_CHUNK, :]
            full = _dot(wsgu_ref[...], vc)
            mc = bsgu_ref[...]
            for hd in range(n_heads):
                mc = mc + jnp.where(head == hd, full[hd * SGU_CHUNK:(hd + 1) * SGU_CHUNK, :], 0.0)
            mixed.append(mc)
        y_sgu = u * jnp.concatenate(mixed, axis=0)

        cat = jnp.concatenate([yf_ref[rows, :].astype(BF16), ya_ref[rows, :], y_pool.astype(BF16),
                               y_sgu.astype(BF16)], axis=1)
        y = _dot(cat, wout_ref[...])
        x1 = _layer_norm(RES_ALPHA * x_ref[rows, :] + gate1 * y) * lng_ref[...] + lnb_ref[...]
        x1_ref[rows, :] = x1
        h2 = _layer_norm(x1) * (1.0 + scale2) + shift2
        _store_rows(h2_ref, h2, first=r0)
        lg = _dot(h2.astype(BF16), wr_ref[...])
        route_ref[rows, :], route_t_ref[:, rows] = _route(lg[:, 0:LANES] + lg[:, LANES:])


def _merge(x2, p, ug, y_four, y_attn, mod, w_pool_bd, pool_scale, w_sgu_stack, b_sgu_exp, w_out,
           ln_g, ln_b, w_route, aliased, *, mod_row, seq, tm, row_off, total_rows):
    rows, d = x2.shape
    n_batch, steps = rows // seq, seq // tm
    off = row_off // tm
    kern = functools.partial(_merge_kernel, mod_row=mod_row, seq=seq, n_alias=len(aliased))
    row_spec = lambda w: pl.BlockSpec((tm, w), lambda b, i: (b * steps + i, 0))
    const = lambda a: pl.BlockSpec(a.shape, lambda b, i: (0,) * a.ndim)
    consts = (mod, w_pool_bd, pool_scale, w_sgu_stack, b_sgu_exp, w_out, ln_g, ln_b, w_route)
    n_in = 5 + len(consts)
    out_shapes = [jax.ShapeDtypeStruct((total_rows, d), F32),
                  jax.ShapeDtypeStruct((total_rows * ROW_WORDS, LANES), jnp.uint32),
                  jax.ShapeDtypeStruct((total_rows, LANES), F32),
                  jax.ShapeDtypeStruct((SUBLANES, total_rows), F32)]
    out_specs = [pl.BlockSpec((tm, d), lambda b, i: (off + b * steps + i, 0)),
                 pl.BlockSpec((tm * ROW_WORDS, LANES), lambda b, i: (off + b * steps + i, 0)),
                 pl.BlockSpec((tm, LANES), lambda b, i: (off + b * steps + i, 0)),
                 pl.BlockSpec((SUBLANES, tm), lambda b, i: (0, off + b * steps + i))]
    return pl.pallas_call(
        kern,
        grid=(n_batch, steps),
        in_specs=[row_spec(d), pl.BlockSpec((seq, GROUP_W), lambda b, i: (b, 0)),
                  row_spec(2 * GROUP_W), row_spec(GROUP_W), row_spec(GROUP_W)]
                 + [const(a) for a in consts]
                 + [pl.BlockSpec(memory_space=pl.ANY)] * len(aliased),
        out_specs=out_specs,
        out_shape=out_shapes,
        input_output_aliases={n_in + k: k for k in range(len(aliased))},
        compiler_params=_cparams("arbitrary", "arbitrary"),
        name="merge",
    )(x2, p, ug, y_four, y_attn, *consts, *aliased)


def _plan_kernel(route_ref, dest_ref, cnt_out_ref, cnt_ref, start_ref, carry_ref):
    ph, t = pl.program_id(0), pl.program_id(1)
    tm = route_ref.shape[1]
    rt = route_ref[...]
    e1 = rt[0:1, :].astype(jnp.int32)
    e2 = rt[1:2, :].astype(jnp.int32)
    sub = lax.broadcasted_iota(jnp.int32, (N_EXPERTS, tm), 0)
    hit1, hit2 = sub == e1, sub == e2
    onehot = jnp.where(hit1 | hit2, 1.0, 0.0)
    tile_cnt = jnp.sum(onehot, axis=1, keepdims=True)

    @pl.when((ph == 0) & (t == 0))
    def _():
        cnt_ref[...] = jnp.zeros_like(cnt_ref)

    @pl.when(ph == 0)
    def _():
        cnt_ref[...] += tile_cnt

    @pl.when((ph == 1) & (t == 0))
    def _():
        cnt = cnt_ref[...]
        padded = jnp.floor((cnt + (MOE_BLOCK - 1.0)) * (1.0 / MOE_BLOCK)) * MOE_BLOCK
        row = lax.broadcasted_iota(jnp.int32, cnt.shape, 0)
        incl = padded
        sh = 1
        while sh < N_EXPERTS:
            incl = incl + jnp.where(row >= sh, pltpu.roll(incl, sh, axis=0), 0.0)
            sh *= 2
        start_ref[...] = incl - padded
        carry_ref[...] = jnp.zeros_like(carry_ref)
        cnt_out_ref[...] = cnt

    @pl.when(ph == 1)
    def _():
        r_i = lax.broadcasted_iota(jnp.int32, (tm, tm), 0)
        c_i = lax.broadcasted_iota(jnp.int32, (tm, tm), 1)
        before = jnp.where(r_i < c_i, 1.0, 0.0).astype(BF16)
        rank = _dot(onehot.astype(BF16), before)
        base = start_ref[:, 0:1] + carry_ref[:, 0:1] + rank
        d1 = jnp.sum(jnp.where(hit1, base, 0.0), axis=0, keepdims=True)
        d2 = jnp.sum(jnp.where(hit2, base, 0.0), axis=0, keepdims=True)
        sub8 = lax.broadcasted_iota(jnp.int32, (SUBLANES, tm), 0)
        dest_ref[...] = jnp.where(sub8 == 0, d1, d2).astype(jnp.int32)
        carry_ref[...] += tile_cnt


def _plan(route_t, tm):
    rows = route_t.shape[1]
    n_t = rows // tm
    return pl.pallas_call(
        _plan_kernel,
        grid=(2, n_t),
        in_specs=[pl.BlockSpec((SUBLANES, tm), lambda ph, t: (0, t))],
        out_specs=[pl.BlockSpec((None, SUBLANES, tm), lambda ph, t: (t * ph, 0, 0)),
                   pl.BlockSpec((N_EXPERTS, LANES), lambda ph, t: (0, 0))],
        out_shape=[jax.ShapeDtypeStruct((n_t, SUBLANES, tm), jnp.int32),
                   jax.ShapeDtypeStruct((N_EXPERTS, LANES), F32)],
        scratch_shapes=[pltpu.VMEM((N_EXPERTS, LANES), F32)] * 3,
        compiler_params=_cparams("arbitrary", "arbitrary"),
        name="moe_plan",
    )(route_t)


def _row_copy(src_ref, src_row, dst_ref, dst_row, sem):
    return pltpu.make_async_copy(
        src_ref.at[pl.ds(pl.multiple_of(src_row * ROW_WORDS, ROW_WORDS), ROW_WORDS)],
        dst_ref.at[pl.ds(pl.multiple_of(dst_row * ROW_WORDS, ROW_WORDS), ROW_WORDS)], sem)


def _dispatch_kernel(dest_ref, h2_ref, xs_ref, sem, *, tm):
    def body(r4, c):
        for u in range(DMA_UNROLL):
            r = r4 * DMA_UNROLL + u
            for k in range(2):
                _row_copy(h2_ref, r, xs_ref, dest_ref[0, 0, k * tm + r], sem).start(priority=k)
        return c

    lax.fori_loop(0, tm // DMA_UNROLL, body, 0)
    for k in range(2):
        pltpu.make_async_copy(h2_ref, xs_ref.at[pl.ds(0, tm * ROW_WORDS)], sem).wait()


def _dispatch(dest, h2_tiles, n_slots, tm):
    n_t = dest.shape[0]
    return pl.pallas_call(
        functools.partial(_dispatch_kernel, tm=tm),
        grid=(n_t,),
        in_specs=[pl.BlockSpec((1, 1, 2 * tm), lambda i: (i, 0, 0), memory_space=pltpu.SMEM),
                  pl.BlockSpec((tm * ROW_WORDS, LANES), lambda i: (i, 0))],
        out_specs=pl.BlockSpec(memory_space=pl.ANY),
        out_shape=jax.ShapeDtypeStruct((n_slots * ROW_WORDS, LANES), jnp.uint32),
        scratch_shapes=[pltpu.SemaphoreType.DMA],
        compiler_params=_cparams("arbitrary"),
        name="moe_dispatch",
    )(dest, h2_tiles)


def _expert_kernel(be_ref, bn_ref, first_ref, slot_ref, nxt_ref, nxt2_ref, x_ref, wg_hbm, wu_hbm, wd_hbm, y_ref,
                   wg_buf, wu_buf, wd_buf, wg_bf, wu_bf, wd_bf, sem, *, layer):
    i = pl.program_id(0)
    mb = x_ref.shape[0] // ROW_WORDS
    half = wg_bf.shape[0] // 2

    def weight_copies(e, s):
        return [pltpu.make_async_copy(wg_hbm.at[layer, e], wg_buf.at[s], sem.at[s]),
                pltpu.make_async_copy(wu_hbm.at[layer, e], wu_buf.at[s], sem.at[s]),
                pltpu.make_async_copy(wd_hbm.at[layer, e], wd_buf.at[s], sem.at[s])]

    @pl.when(i == 0)
    def _():
        for cp in weight_copies(be_ref[0], 0):
            cp.start()

        @pl.when(nxt_ref[0] >= 0)
        def _():
            for cp in weight_copies(nxt_ref[0], 1):
                cp.start()

    @pl.when(first_ref[i] == 1)
    def _():
        s = slot_ref[i]
        for cp in weight_copies(be_ref[i], s):
            cp.wait()

        @pl.when(nxt2_ref[i] >= 0)
        def _():
            s2 = jnp.where(s == 0, WEIGHT_SLOTS - 1, s - 1)
            for cp in weight_copies(nxt2_ref[i], s2):
                cp.start()

        wg_bf[...] = wg_buf[s].astype(BF16)
        wu_bf[...] = wu_buf[s].astype(BF16)
        wd_bf[...] = wd_buf[s].astype(BF16)

    def compute(m):
        live = lax.broadcasted_iota(jnp.int32, (m, 1), 0) < bn_ref[i]
        x_lo, x_hi = _load_rows(x_ref, 0, m)
        x_lo = jnp.where(live, x_lo, 0.0).astype(BF16)
        x_hi = jnp.where(live, x_hi, 0.0).astype(BF16)
        g = _dot(x_lo, wg_bf[0:half, :]) + _dot(x_hi, wg_bf[half:, :])
        u = _dot(x_lo, wu_bf[0:half, :]) + _dot(x_hi, wu_bf[half:, :])
        hid = (_silu(g) * u).astype(BF16)
        _store_rows(y_ref, _dot(hid, wd_bf[...]))

    for parts in range(1, mb // MOE_PART + 1):
        @pl.when((bn_ref[i] > (parts - 1) * MOE_PART) & (bn_ref[i] <= parts * MOE_PART))
        def _(parts=parts):
            compute(parts * MOE_PART)


def _experts(table, xs, w_gate, w_up, w_down, layer):
    n_blocks = table[0].shape[0]
    _, _, d, de = w_gate.shape
    blk = pl.BlockSpec((MOE_BLOCK * ROW_WORDS, LANES), lambda i, *_: (i, 0))
    hbm = pl.BlockSpec(memory_space=pl.ANY)
    return pl.pallas_call(
        functools.partial(_expert_kernel, layer=layer),
        grid_spec=pltpu.PrefetchScalarGridSpec(
            num_scalar_prefetch=len(table),
            grid=(n_blocks,),
            in_specs=[blk, hbm, hbm, hbm],
            out_specs=blk,
            scratch_shapes=[pltpu.VMEM((WEIGHT_SLOTS, d, de), F32), pltpu.VMEM((WEIGHT_SLOTS, d, de), F32),
                            pltpu.VMEM((WEIGHT_SLOTS, de, d), F32),
                            pltpu.VMEM((d, de), BF16), pltpu.VMEM((d, de), BF16), pltpu.VMEM((de, d), BF16),
                            pltpu.SemaphoreType.DMA((WEIGHT_SLOTS,))]),
        out_shape=jax.ShapeDtypeStruct(xs.shape, jnp.uint32),
        compiler_params=_cparams("arbitrary"),
        name="moe_experts",
    )(*table, xs, w_gate, w_up, w_down)


def _combine_body(dest_ref, dest_next_ref, x1_ref, route_ref, m, lng_ref, lnb_ref, y_ref, o_ref, buf_ref, sem):
    tm, d = x1_ref.shape
    i = pl.program_id(0)
    n = pl.num_programs(0)
    slot = i % 2
    pr = tm // COMBINE_PARTS

    def gather(idx_ref, s, first, count):
        def body(r4, c):
            for u in range(DMA_UNROLL):
                r = first + r4 * DMA_UNROLL + u
                for k in range(2):
                    _row_copy(y_ref, idx_ref[0, 0, k * tm + r], buf_ref.at[s], k * tm + r,
                              sem.at[s]).start(priority=k)
            return c

        lax.fori_loop(0, count // DMA_UNROLL, body, 0)

    @pl.when(i == 0)
    def _():
        gather(dest_ref, 0, 0, tm)

    for k in range(2):
        pltpu.make_async_copy(y_ref.at[pl.ds(0, tm * ROW_WORDS)],
                              buf_ref.at[slot, pl.ds(0, tm * ROW_WORDS)], sem.at[slot]).wait()

    gate2 = m[:, 5 * d:6 * d]
    for part in range(COMBINE_PARTS):
        @pl.when(i + 1 < n)
        def _(part=part):
            gather(dest_next_ref, 1 - slot, part * pr, pr)

        rows = slice(part * pr, (part + 1) * pr)
        rt = route_ref[rows, :]
        f = jnp.zeros((pr, d), F32)
        for k in range(2):
            lo, hi = _load_rows(buf_ref.at[slot], k * tm + part * pr, pr)
            f = f + jnp.concatenate([lo, hi], axis=1) * rt[:, 2 + k:3 + k]
        o_ref[rows, :] = _layer_norm(RES_ALPHA * x1_ref[rows, :] + gate2 * f) * lng_ref[...] + lnb_ref[...]


def _combine_kernel(dest_ref, dest_next_ref, x1_ref, route_ref, mod_ref, lng_ref, lnb_ref, y_ref, o_ref,
                    buf_ref, sem, *, mod_row, rows_per_batch):
    m = _mod_row(mod_ref, mod_row, x1_ref.shape[0], rows_per_batch)
    _combine_body(dest_ref, dest_next_ref, x1_ref, route_ref, m, lng_ref, lnb_ref, y_ref, o_ref, buf_ref, sem)


def _combine_proj_kernel(dest_ref, dest_next_ref, x1_ref, route_ref, mod_ref, lng_ref, lnb_ref,
                         mod_next_ref, w_ref, cos_ref, sin_ref, y_ref, o_ref, *rest,
                         rows_per_batch, a_pitch):
    outs, (buf_ref, sem) = rest[:-2], rest[-2:]
    tm = x1_ref.shape[0]
    _combine_body(dest_ref, dest_next_ref, x1_ref, route_ref, _mod_row(mod_ref, None, tm, rows_per_batch),
                  lng_ref, lnb_ref, y_ref, o_ref, buf_ref, sem)
    _proj_body(o_ref[...], _mod_row(mod_next_ref, None, tm, rows_per_batch), w_ref, cos_ref, sin_ref, outs,
               rope=True, a_pitch=a_pitch)


def _combine_specs(tm, d, off, steps):
    return [pl.BlockSpec((1, 1, 2 * tm), lambda i: (off + i, 0, 0), memory_space=pltpu.SMEM),
            pl.BlockSpec((1, 1, 2 * tm), lambda i: (off + jnp.minimum(i + 1, steps - 1), 0, 0),
                         memory_space=pltpu.SMEM),
            pl.BlockSpec((tm, d), lambda i: (off + i, 0)),
            pl.BlockSpec((tm, LANES), lambda i: (off + i, 0))]


def _combine_scratch(tm):
    return [pltpu.VMEM((2, 2 * tm * ROW_WORDS, LANES), jnp.uint32), pltpu.SemaphoreType.DMA((2,))]


def _combine(dest, x1, route, mod, ln_g, ln_b, y_tiles, *, tm, row_off, rows, mod_row, rows_per_batch):
    d = x1.shape[1]
    steps = rows // tm
    kern = functools.partial(_combine_kernel, mod_row=mod_row, rows_per_batch=rows_per_batch)
    const = lambda a: pl.BlockSpec(a.shape, lambda i: (0,) * a.ndim)
    return pl.pallas_call(
        kern,
        grid=(steps,),
        in_specs=_combine_specs(tm, d, row_off // tm, steps)
                 + [const(mod), const(ln_g), const(ln_b), pl.BlockSpec(memory_space=pl.ANY)],
        out_specs=pl.BlockSpec((tm, d), lambda i: (i, 0)),
        out_shape=jax.ShapeDtypeStruct((rows, d), F32),
        scratch_shapes=_combine_scratch(tm),
        compiler_params=_cparams("arbitrary"),
        name="moe_combine",
    )(dest, dest, x1, route, mod, ln_g, ln_b, y_tiles)


def _combine_proj(dest, x1, route, mod, ln_g, ln_b, y_tiles, mod_next, w_in, cos_t, sin_t, *,
                  tm, rows, rows_per_batch, a_pitch):
    d = x1.shape[1]
    steps = rows // tm
    kern = functools.partial(_combine_proj_kernel, rows_per_batch=rows_per_batch, a_pitch=a_pitch)
    const = lambda a: pl.BlockSpec(a.shape, lambda i: (0,) * a.ndim)
    table_spec, proj_specs, proj_shapes = _proj_specs(rows, tm, a_pitch, cos_t.shape[0] // tm)
    outs = pl.pallas_call(
        kern,
        grid=(steps,),
        in_specs=_combine_specs(tm, d, 0, steps)
                 + [const(mod), const(ln_g), const(ln_b), const(mod_next), const(w_in), table_spec, table_spec,
                    pl.BlockSpec(memory_space=pl.ANY)],
        out_specs=[pl.BlockSpec((tm, d), lambda i: (i, 0))] + proj_specs,
        out_shape=[jax.ShapeDtypeStruct((rows, d), F32)] + proj_shapes,
        scratch_shapes=_combine_scratch(tm),
        compiler_params=_cparams("arbitrary"),
        name="moe_combine_proj",
    )(dest, dest, x1, route, mod, ln_g, ln_b, mod_next, w_in, cos_t, sin_t, y_tiles)
    return outs[0], outs[1:]


def _rope_tables(n_pos):
    rows = n_pos // GRID_W
    row = jnp.repeat(jnp.arange(rows), GRID_W).astype(F32)
    col = jnp.tile(jnp.arange(GRID_W), rows).astype(F32)
    n_freq = HEAD_DIM // 4
    freq = ROPE_BASE ** (-jnp.arange(n_freq, dtype=F32) / n_freq)
    ang_r, ang_c = row[:, None] * freq, col[:, None] * freq
    cos_h = jnp.concatenate([jnp.cos(ang_r)] * 2 + [jnp.cos(ang_c)] * 2, axis=1)
    sin_h = jnp.concatenate([-jnp.sin(ang_r), jnp.sin(ang_r), -jnp.sin(ang_c), jnp.sin(ang_c)], axis=1)
    return jnp.tile(cos_h, (1, 2)), jnp.tile(sin_h, (1, 2))


def _block_table(counts, n_blocks):
    cnt = counts.astype(jnp.int32)
    padded = (cnt + MOE_BLOCK - 1) // MOE_BLOCK * MOE_BLOCK
    pad_end = jnp.cumsum(padded)
    pad_start = pad_end - padded
    blk_start = jnp.arange(n_blocks, dtype=jnp.int32)[:, None] * MOE_BLOCK
    be = jnp.minimum(jnp.sum((pad_end[None, :] <= blk_start).astype(jnp.int32), axis=1), N_EXPERTS - 1)
    ids = jnp.arange(N_EXPERTS, dtype=jnp.int32)
    mine = be[:, None] == ids[None, :]
    fill = jnp.sum(jnp.where(mine, cnt[None, :] + pad_start[None, :], 0), axis=1) - blk_start[:, 0]
    bn = jnp.clip(fill, 0, MOE_BLOCK)
    prev = jnp.concatenate([jnp.full((1,), -1, jnp.int32), be[:-1]])
    first = ((bn > 0) & (be != prev)).astype(jnp.int32)
    slot = (jnp.cumsum(first) - 1) % WEIGHT_SLOTS
    later = (ids[None, :] > ids[:, None]) & (cnt[None, :] > 0)
    nxt_e = jnp.min(jnp.where(later, ids[None, :], N_EXPERTS), axis=1)
    hop = nxt_e[:, None] == ids[None, :]
    nxt2_e = jnp.sum(jnp.where(hop, nxt_e[None, :], 0), axis=1) + jnp.where(nxt_e == N_EXPERTS, N_EXPERTS, 0)
    lookup = lambda tab: jnp.sum(jnp.where(mine, jnp.where(tab >= N_EXPERTS, -1, tab)[None, :], 0), axis=1)
    return (be, bn, first, slot.astype(jnp.int32), lookup(nxt_e).astype(jnp.int32),
            lookup(nxt2_e).astype(jnp.int32))


def _moe(route_t, h2_tiles, w_gate, w_up, w_down, layer):
    rows = route_t.shape[1]
    tm = ROW_TILE
    n_blocks = -(-(2 * rows) // MOE_BLOCK) + N_EXPERTS
    dest8, counts = _plan(route_t, tm)
    dest = dest8[:, 0:2, :].reshape(rows // tm, 1, 2 * tm)
    table = _block_table(counts[:, 0], n_blocks)
    xs = _dispatch(dest, h2_tiles, n_blocks * MOE_BLOCK, tm)
    ys = _experts(table, xs, w_gate, w_up, w_down, layer)
    tc = COMBINE_TILE
    dest_c = dest8[:, 0:2, :].reshape(rows // tm, 2, tm // tc, tc).transpose(0, 2, 1, 3).reshape(rows // tc, 1, 2 * tc)
    return dest_c, ys


def kernel(x, c, ctx, c_ctx, w_ada, b_ada, w_in, w_fourier, attn_sink, w_pool, pool_scale, w_sgu, b_sgu,
           w_out, ln1_g, ln1_b, w_router_group, w_router_expert, w_exp_gate, w_exp_up, w_exp_down,
           ln2_g, ln2_b):
    b, s, d = x.shape
    n_ctx = ctx.shape[1]
    n_layers = w_in.shape[0]
    tm = ROW_TILE
    cond = jnp.concatenate([c, c_ctx[None, :], jnp.zeros((SUBLANES - b - 1, d), F32)], axis=0)
    mod_all = _ada(cond, w_ada, b_ada[:, None, :])
    cos_t, sin_t = _rope_tables(s)
    x2 = x.reshape(b * s, d)
    c2 = ctx.reshape(b * n_ctx, d)
    n_sgu = w_sgu.shape[1]
    lat = None
    for layer in range(n_layers):
        last = layer == n_layers - 1
        mod = mod_all[layer]
        w_in_l = w_in[layer].astype(BF16)
        wf = w_fourier[layer].astype(BF16)
        w_pool_bd = jax.scipy.linalg.block_diag(*[w_pool[layer, g] for g in range(w_pool.shape[1])]).astype(BF16)
        w_sgu_stack = w_sgu[layer].reshape(n_sgu * SGU_CHUNK, SGU_CHUNK).astype(BF16)
        b_sgu_exp = jnp.repeat(b_sgu[layer].T, GROUP_W // n_sgu, axis=1)
        w_router = jnp.concatenate([w_router_expert[layer].reshape(d, N_EXPERTS), w_router_group[layer]], axis=1)
        w_router = jnp.pad(w_router, ((0, 0), (0, LANES - w_router.shape[1])))
        wr_hi = w_router.astype(BF16)
        w_route = jnp.concatenate([wr_hi, (w_router - wr_hi.astype(F32)).astype(BF16)], axis=1)
        merge_consts = (mod, w_pool_bd, pool_scale[layer][None, :], w_sgu_stack, b_sgu_exp,
                        w_out[layer].astype(BF16), ln1_g[layer][None, :], ln1_b[layer][None, :], w_route)
        sink = attn_sink[layer]

        if lat is None:
            lat = _proj(x2, mod, w_in_l, cos_t, sin_t, mod_row=None, rows_per_batch=s,
                        rope=True, tm=tm, a_pitch=FFT_PITCH)
        a, q, qs, k, v, p, ug = lat
        ac, qc, qsc, kc, vc, pc, ugc = _proj(c2, mod, w_in_l, cos_t, sin_t, mod_row=b, rows_per_batch=n_ctx,
                                             rope=False, tm=n_ctx, a_pitch=FFT_R)
        y_four = _fourier(a, wf, s)
        y_attn = _attention(sink, q, qs, k, v, kc, vc, seq=s, n_ctx=n_ctx, band=True)
        total = b * s + (0 if last else b * n_ctx)
        merged = _merge(x2, p, ug, y_four, y_attn, *merge_consts, (),
                        mod_row=None, seq=s, tm=tm, row_off=0, total_rows=total)
        if not last:
            yc_four = _fourier_small(ac, wf, n_ctx)
            yc_attn = _attention(sink, qc, qsc, kc, vc, kc, vc, seq=n_ctx, n_ctx=n_ctx, band=False)
            merged = _merge(c2, pc, ugc, yc_four, yc_attn, *merge_consts, tuple(merged),
                            mod_row=b, seq=n_ctx, tm=n_ctx, row_off=b * s, total_rows=total)
        x1, h2_tiles, route, route_t = merged
        dest, ys = _moe(route_t, h2_tiles, w_exp_gate, w_exp_up, w_exp_down, layer)
        ln_g, ln_b = ln2_g[layer][None, :], ln2_b[layer][None, :]
        if last:
            x2 = _combine(dest, x1, route, mod, ln_g, ln_b, ys, tm=COMBINE_TILE, row_off=0, rows=b * s,
                          mod_row=None, rows_per_batch=s)
        else:
            x2, lat = _combine_proj(dest, x1, route, mod, ln_g, ln_b, ys, mod_all[layer + 1],
                                    w_in[layer + 1].astype(BF16), cos_t, sin_t,
                                    tm=COMBINE_TILE, rows=b * s, rows_per_batch=s, a_pitch=FFT_PITCH)
            c2 = _combine(dest, x1, route, mod, ln_g, ln_b, ys, tm=COMBINE_TILE, row_off=b * s,
                          rows=b * n_ctx, mod_row=b, rows_per_batch=n_ctx)
    return x2.reshape(b, s, d)
```

```python
import functools
import math

import numpy as np
import jax
import jax.numpy as jnp
from jax import lax
from jax.experimental import pallas as pl
from jax.experimental.pallas import tpu as pltpu

GRID_W = 64
HEAD_DIM = 64
GROUP_W = 256
KV_W = 128
WINDOW = 128
POOL_WINDOWS = (2, 4, 8, 16)
SGU_CHUNK = 128
N_GROUPS = 4
EXPERTS_PER_GROUP = 8
N_EXPERTS = 32
ROPE_BASE = 10000.0
LN_EPS = 1e-6
NEG_INF = -1e30
DEPTH = 2
RES_ALPHA = (2 * DEPTH) ** 0.25

LANES = 128
SUBLANES = 8
VMEM_LIMIT = 48 * 1024 * 1024

ROW_TILE = 1024
Q_BLOCK = 1024
COMBINE_TILE = 512
COMBINE_PARTS = 1
WEIGHT_SLOTS = 3
MOE_BLOCK = 512
MOE_PART = 256
FFT_R = 64
FFT_PITCH = 72
FFT_UNROLL = 8
DMA_UNROLL = 8
MERGE_PART = 512

BF16 = jnp.bfloat16
F32 = jnp.float32


def _cparams(*sem):
    return pltpu.CompilerParams(dimension_semantics=sem, vmem_limit_bytes=VMEM_LIMIT)


def _dot(a, b):
    return jnp.dot(a, b, preferred_element_type=F32)


def _dot_nt(a, b):
    return lax.dot_general(a, b, (((1,), (1,)), ((), ())), preferred_element_type=F32)


def _layer_norm(t):
    mu = jnp.mean(t, axis=-1, keepdims=True)
    d = t - mu
    var = jnp.mean(d * d, axis=-1, keepdims=True)
    return d * lax.rsqrt(var + LN_EPS)


def _silu(t):
    return t * (1.0 / (1.0 + jnp.exp(-t)))


def _gelu(t):
    return 0.5 * t * (1.0 + lax.erf(t * (1.0 / math.sqrt(2.0))))


ROW_WORDS = 4
HI_MASK = 0xFFFF0000


def _pack_rows(t):
    half = t.shape[1] // 2
    lo = lax.bitcast_convert_type(t[:, :half].astype(BF16).astype(F32), jnp.uint32)
    hi = lax.bitcast_convert_type(t[:, half:].astype(BF16).astype(F32), jnp.uint32)
    return (lo >> 16) | (hi & jnp.uint32(HI_MASK))


def _unpack_rows(w):
    return (lax.bitcast_convert_type(w << 16, F32),
            lax.bitcast_convert_type(w & jnp.uint32(HI_MASK), F32))


def _store_rows(ref, t, first=0):
    w = _pack_rows(t)
    for j in range(ROW_WORDS):
        ref[pl.ds(first * ROW_WORDS + j, t.shape[0], stride=ROW_WORDS), :] = w[:, j * LANES:(j + 1) * LANES]


def _load_rows(ref, first, m):
    w = jnp.concatenate([ref[pl.ds(first * ROW_WORDS + j, m, stride=ROW_WORDS), :] for j in range(ROW_WORDS)],
                        axis=1)
    return _unpack_rows(w)


def _ada_kernel(c_ref, w_ref, b_ref, o_ref):
    s = _silu(c_ref[...]).astype(BF16)
    o_ref[...] = _dot(s, w_ref[...].astype(BF16)) + b_ref[...]


def _ada(cond, w_ada, b_ada):
    n_layers, d, n = w_ada.shape
    tn = n // 4
    return pl.pallas_call(
        _ada_kernel,
        grid=(n_layers, n // tn),
        in_specs=[
            pl.BlockSpec((SUBLANES, d), lambda l, j: (0, 0)),
            pl.BlockSpec((None, d, tn), lambda l, j: (l, 0, j)),
            pl.BlockSpec((None, 1, tn), lambda l, j: (l, 0, j)),
        ],
        out_specs=pl.BlockSpec((None, SUBLANES, tn), lambda l, j: (l, 0, j)),
        out_shape=jax.ShapeDtypeStruct((n_layers, SUBLANES, n), F32),
        compiler_params=_cparams("arbitrary", "arbitrary"),
        name="ada",
    )(cond, w_ada, b_ada)


def _rope(t, cos_t, sin_t):
    lane = lax.broadcasted_iota(jnp.int32, t.shape, 1)
    first = (lane % 32) < 16
    partner = jnp.where(first, pltpu.roll(t, LANES - 16, axis=1), pltpu.roll(t, 16, axis=1))
    return t * cos_t + partner * sin_t


def _proj_body(x, m, w_ref, cos_ref, sin_ref, outs, *, rope, a_pitch):
    a_ref, q_ref, qs_ref, k_ref, v_ref, p_ref, ug_ref = outs
    tm, d = x.shape
    shift, scale = m[:, 0:d], m[:, d:2 * d]
    h = _layer_norm(x) * (1.0 + scale) + shift
    z = _dot(h.astype(BF16), w_ref[...])
    pad = jnp.zeros((a_pitch - FFT_R, LANES), F32)
    for g in range(tm // FFT_R):
        for hf in range(2):
            grp = z[g * FFT_R:(g + 1) * FFT_R, hf * LANES:(hf + 1) * LANES]
            if a_pitch > FFT_R:
                grp = jnp.concatenate([grp, pad], axis=0)
            a_ref[hf, g * a_pitch:(g + 1) * a_pitch, :] = grp
    q0, q1 = z[:, 256:384], z[:, 384:512]
    k = z[:, 512:640]
    if rope:
        cos_t, sin_t = cos_ref[...], sin_ref[...]
        q0, q1, k = _rope(q0, cos_t, sin_t), _rope(q1, cos_t, sin_t), _rope(k, cos_t, sin_t)
    q_ref[:, 0:128] = q0.astype(BF16)
    q_ref[:, 128:256] = q1.astype(BF16)
    qs_ref[:, 0:128] = pltpu.roll(q0, HEAD_DIM, axis=1).astype(BF16)
    qs_ref[:, 128:256] = pltpu.roll(q1, HEAD_DIM, axis=1).astype(BF16)
    k_ref[...] = k.astype(BF16)
    v_ref[...] = z[:, 640:768].astype(BF16)
    p_ref[...] = z[:, 768:1024].astype(BF16)
    ug_ref[...] = z[:, 1024:1536].astype(BF16)


def _mod_row(mod_ref, mod_row, tm, rows_per_batch):
    row = (pl.program_id(0) * tm) // rows_per_batch if mod_row is None else mod_row
    return mod_ref[pl.ds(row, 1), :]


def _proj_kernel(x_ref, mod_ref, w_ref, cos_ref, sin_ref, *outs, mod_row, rows_per_batch, rope, a_pitch):
    m = _mod_row(mod_ref, mod_row, x_ref.shape[0], rows_per_batch)
    _proj_body(x_ref[...], m, w_ref, cos_ref, sin_ref, outs, rope=rope, a_pitch=a_pitch)


def _proj_specs(rows, tm, a_pitch, seq_steps):
    row_spec = lambda w: pl.BlockSpec((tm, w), lambda i: (i, 0))
    out_w = (256, 256, 128, 128, 256, 512)
    ta = tm // FFT_R * a_pitch
    out_specs = [pl.BlockSpec((2, ta, LANES), lambda i: (0, i, 0))] + [row_spec(w) for w in out_w]
    out_shape = ([jax.ShapeDtypeStruct((2, rows // FFT_R * a_pitch, LANES), F32)]
                 + [jax.ShapeDtypeStruct((rows, w), BF16) for w in out_w])
    table_spec = pl.BlockSpec((tm, LANES), lambda i: (i % seq_steps, 0))
    return table_spec, out_specs, out_shape


def _proj(x2, mod, w_in, cos_t, sin_t, *, mod_row, rows_per_batch, rope, tm, a_pitch):
    rows, d = x2.shape
    kern = functools.partial(_proj_kernel, mod_row=mod_row, rows_per_batch=rows_per_batch,
                             rope=rope, a_pitch=a_pitch)
    table_spec, out_specs, out_shape = _proj_specs(rows, tm, a_pitch, cos_t.shape[0] // tm)
    return pl.pallas_call(
        kern,
        grid=(rows // tm,),
        in_specs=[
            pl.BlockSpec((tm, d), lambda i: (i, 0)),
            pl.BlockSpec(mod.shape, lambda i: (0, 0)),
            pl.BlockSpec(w_in.shape, lambda i: (0, 0)),
            table_spec, table_spec,
        ],
        out_specs=out_specs,
        out_shape=out_shape,
        compiler_params=_cparams("arbitrary"),
        name="proj",
    )(x2, mod, w_in, cos_t, sin_t)


def _fft_tables(n_pos):
    r = FFT_R
    assert n_pos == r * r
    kb = np.arange(r)[None, :, None]
    na = np.arange(r)[:, None, None]
    nb = np.arange(r)[None, None, :]
    ang = 2.0 * np.pi * ((kb * (na + r * nb)) % n_pos) / n_pos
    m1 = np.concatenate([np.cos(ang), -np.sin(ang)], axis=1)
    ka = np.arange(r)[:, None]
    n2 = np.arange(r)[None, :]
    ang2 = 2.0 * np.pi * ((ka * n2) % r) / r
    c2, s2 = np.cos(ang2), np.sin(ang2)
    w2 = np.block([[c2, s2], [-s2, c2]])
    return m1, w2


def _channel_tables(n_pos):
    h = HEAD_DIM
    c = np.arange(h)
    ang = 2.0 * np.pi * ((c[:, None] * c[None, :]) % h) / h
    scale = 1.0 / math.sqrt(n_pos * h)
    eye = np.eye(GROUP_W // h)
    cc = np.kron(eye, np.cos(ang)) * scale
    ss = np.kron(eye, np.sin(ang)) * scale
    return np.concatenate([cc, ss], axis=0)


def _fourier_kernel(a_ref, m1_ref, w2_ref, ch_ref, wf_ref, o_ref, z_ref, y_ref):
    r, pt = FFT_R, FFT_PITCH

    def step1(i, c):
        for u in range(FFT_UNROLL):
            na = i * FFT_UNROLL + u
            rows = jnp.concatenate([a_ref[0, pl.ds(na, r, stride=pt), :],
                                    a_ref[1, pl.ds(na, r, stride=pt), :]], axis=1)
            z = _dot(m1_ref[na], rows.astype(BF16))
            base = pl.multiple_of(na * pt, SUBLANES)
            z_ref[0, pl.ds(base, r), :] = z[0:r, 0:LANES]
            z_ref[1, pl.ds(base, r), :] = z[0:r, LANES:]
            z_ref[2, pl.ds(base, r), :] = z[r:, 0:LANES]
            z_ref[3, pl.ds(base, r), :] = z[r:, LANES:]
        return c

    lax.fori_loop(0, r // FFT_UNROLL, step1, 0)

    def step2(i, c):
        for u in range(FFT_UNROLL):
            kb = i * FFT_UNROLL + u
            q = [z_ref[j, pl.ds(kb, r, stride=pt), :] for j in range(4)]
            zs = jnp.concatenate([jnp.concatenate(q[0:2], axis=1),
                                  jnp.concatenate(q[2:4], axis=1)], axis=0)
            y = _dot(w2_ref[...], zs.astype(BF16))
            base = pl.multiple_of(kb * r, r)
            y_ref[0, pl.ds(base, r), :] = y[0:r, 0:LANES]
            y_ref[1, pl.ds(base, r), :] = y[0:r, LANES:]
            y_ref[2, pl.ds(base, r), :] = y[r:, 0:LANES]
            y_ref[3, pl.ds(base, r), :] = y[r:, LANES:]
        return c

    lax.fori_loop(0, r // FFT_UNROLL, step2, 0)

    chunk = 8 * r
    for cidx in range(r * r // chunk):
        yy = jnp.concatenate([y_ref[j, cidx * chunk:(cidx + 1) * chunk, :] for j in range(4)], axis=1)
        f = _dot(yy.astype(BF16), ch_ref[...])
        g = _dot(f.astype(BF16), wf_ref[...])
        for gi in range(chunk // r):
            kb = cidx * (chunk // r) + gi
            z_ref[0, kb * pt:kb * pt + r, :] = g[gi * r:(gi + 1) * r, 0:LANES]
            z_ref[1, kb * pt:kb * pt + r, :] = g[gi * r:(gi + 1) * r, LANES:]

    def step3(i, c):
        for u in range(FFT_UNROLL):
            ka = i * FFT_UNROLL + u
            base = pl.multiple_of(ka * r, r)
            o_ref[pl.ds(base, r), 0:LANES] = z_ref[0, pl.ds(ka, r, stride=pt), :]
            o_ref[pl.ds(base, r), LANES:] = z_ref[1, pl.ds(ka, r, stride=pt), :]
        return c

    lax.fori_loop(0, r // FFT_UNROLL, step3, 0)


def _fourier(a3, w_fourier, n_pos):
    rows = a3.shape[1] // FFT_PITCH * FFT_R
    gw = GROUP_W
    m1, w2 = _fft_tables(n_pos)
    ch = _channel_tables(n_pos)
    const = lambda shape: pl.BlockSpec(shape, lambda b: (0,) * len(shape))
    return pl.pallas_call(
        _fourier_kernel,
        grid=(rows // n_pos,),
        in_specs=[
            pl.BlockSpec((2, FFT_R * FFT_PITCH, LANES), lambda b: (0, b, 0)),
            const(m1.shape), const(w2.shape), const(ch.shape), const(w_fourier.shape),
        ],
        out_specs=pl.BlockSpec((n_pos, gw), lambda b: (b, 0)),
        out_shape=jax.ShapeDtypeStruct((rows, gw), F32),
        scratch_shapes=[pltpu.VMEM((4, FFT_R * FFT_PITCH, LANES), F32), pltpu.VMEM((4, n_pos, LANES), F32)],
        compiler_params=_cparams("arbitrary"),
        name="fourier",
    )(a3, jnp.asarray(m1, BF16), jnp.asarray(w2, BF16), jnp.asarray(ch, BF16), w_fourier)


def _fourier_small_kernel(a_ref, cs_ref, ch_ref, wf_ref, o_ref):
    n = a_ref.shape[1]
    a = jnp.concatenate([a_ref[0], a_ref[1]], axis=1)
    pq = _dot(cs_ref[...], a.astype(BF16))
    y = jnp.concatenate([pq[0:n], pq[n:2 * n]], axis=1).astype(BF16)
    f = _dot(y, ch_ref[...])
    o_ref[...] = _dot(f.astype(BF16), wf_ref[...])


def _fourier_small(a3, w_fourier, n_pos):
    _, rows, _ = a3.shape
    gw = GROUP_W
    k = np.arange(n_pos)
    ang = 2.0 * np.pi * ((k[:, None] * k[None, :]) % n_pos) / n_pos
    cs = np.concatenate([np.cos(ang), -np.sin(ang)], axis=0)
    ch = _channel_tables(n_pos)
    const = lambda shape: pl.BlockSpec(shape, lambda b: (0,) * len(shape))
    return pl.pallas_call(
        _fourier_small_kernel,
        grid=(rows // n_pos,),
        in_specs=[pl.BlockSpec((2, n_pos, LANES), lambda b: (0, b, 0)),
                  const(cs.shape), const(ch.shape), const(w_fourier.shape)],
        out_specs=pl.BlockSpec((n_pos, gw), lambda b: (b, 0)),
        out_shape=jax.ShapeDtypeStruct((rows, gw), F32),
        compiler_params=_cparams("arbitrary"),
        name="fourier_ctx",
    )(a3, jnp.asarray(cs, BF16), jnp.asarray(ch, BF16), w_fourier)


ATTN_SUB = 128


def _attn_kernel(sink_ref, q_ref, qs_ref, k_ref, v_ref, kc_ref, vc_ref, o_ref, *, band, seq):
    qb = q_ref.shape[0]
    sub = ATTN_SUB
    lane = lax.broadcasted_iota(jnp.int32, (1, LANES), 1)
    lo_half = lane < HEAD_DIM
    zero = jnp.zeros((), BF16)
    scale = jnp.asarray(HEAD_DIM ** -0.5, BF16)
    kw = sub + 2 * WINDOW
    for sb in range(qb // sub):
        rows = slice(sb * sub, (sb + 1) * sub)
        qa0, qa1 = q_ref[rows, 0:LANES], q_ref[rows, LANES:]
        qs0, qs1 = qs_ref[rows, 0:LANES], qs_ref[rows, LANES:]
        q_all = jnp.concatenate([jnp.where(lo_half, qa0, zero), jnp.where(lo_half, qs0, zero),
                                 jnp.where(lo_half, zero, qs1), jnp.where(lo_half, zero, qa1)], axis=0) * scale
        if band:
            p0 = pl.program_id(1) * qb + sb * sub
            start = pl.multiple_of(jnp.clip(p0 - WINDOW, 0, seq - kw), WINDOW)
            qpos = p0 + lax.broadcasted_iota(jnp.int32, (sub, 1), 0)
            kpos = start + lax.broadcasted_iota(jnp.int32, (1, kw), 1)
            bias = jnp.where(jnp.abs(qpos - kpos) <= WINDOW, 0.0, NEG_INF)
            keys = jnp.concatenate([k_ref[pl.ds(start, kw), :], kc_ref[...]], axis=0)
            vals = jnp.concatenate([v_ref[pl.ds(start, kw), :], vc_ref[...]], axis=0)
        else:
            keys, vals = kc_ref[...], vc_ref[...]
        s_all = _dot_nt(q_all, keys)
        probs, dens = [], []
        for h in range(4):
            s = s_all[h * sub:(h + 1) * sub, :]
            sink = sink_ref[h]
            if band:
                s = jnp.concatenate([s[:, 0:kw] + bias, s[:, kw:]], axis=1)
            m = jnp.maximum(jnp.max(s, axis=1, keepdims=True), sink)
            p = jnp.exp(s - m)
            dens.append(jnp.sum(p, axis=1, keepdims=True) + jnp.exp(sink - m))
            probs.append(p.astype(BF16))
        o_all = _dot(jnp.concatenate(probs, axis=0), vals)
        o = [o_all[h * sub:(h + 1) * sub, :] / dens[h] for h in range(4)]
        o_ref[rows, 0:LANES] = jnp.where(lo_half, o[0], pltpu.roll(o[1], HEAD_DIM, axis=1)).astype(BF16)
        o_ref[rows, LANES:] = jnp.where(lo_half, pltpu.roll(o[2], HEAD_DIM, axis=1), o[3]).astype(BF16)


def _attention(sink, q, qs, k, v, kc, vc, *, seq, n_ctx, band):
    rows = q.shape[0]
    n_batch = rows // seq
    qb = Q_BLOCK if band else seq
    steps = seq // qb
    kern = functools.partial(_attn_kernel, band=band, seq=seq)
    seq_spec = pl.BlockSpec((seq, KV_W), lambda b, i: (b, 0))
    ctx_spec = pl.BlockSpec((n_ctx, KV_W), lambda b, i: (b, 0))
    q_spec = pl.BlockSpec((qb, GROUP_W), lambda b, i: (b * steps + i, 0))
    return pl.pallas_call(
        kern,
        grid=(n_batch, steps),
        in_specs=[pl.BlockSpec(memory_space=pltpu.SMEM), q_spec, q_spec,
                  seq_spec, seq_spec, ctx_spec, ctx_spec],
        out_specs=q_spec,
        out_shape=jax.ShapeDtypeStruct((rows, GROUP_W), BF16),
        compiler_params=_cparams("arbitrary", "arbitrary"),
        name="attn" if band else "attn_ctx",
    )(sink, q, qs, k, v, kc, vc)


POOL_HALO = max(POOL_WINDOWS) // 2


def _pool(p_ref, t0, tm, seq):
    halo = POOL_HALO
    pack = 2 * SUBLANES
    t0 = pl.multiple_of(t0, pack)
    main = p_ref[pl.ds(t0, tm), :].astype(F32)
    lo = pl.multiple_of(jnp.maximum(t0 - pack, 0), pack)
    hi = pl.multiple_of(jnp.minimum(t0 + tm, seq - pack), pack)
    prev = p_ref[pl.ds(lo, pack), :].astype(F32)[pack - halo:, :]
    nxt = p_ref[pl.ds(hi, pack), :].astype(F32)[:halo, :]
    prev = jnp.where(t0 > 0, prev, 0.0)
    nxt = jnp.where(t0 + tm < seq, nxt, 0.0)
    full = jnp.concatenate([prev, main, nxt], axis=0)
    n = tm + 2 * halo
    gch = GROUP_W // len(POOL_WINDOWS)
    first = lax.broadcasted_iota(jnp.int32, (1, LANES), 1) < gch
    means = []
    for hf in range(GROUP_W // LANES):
        wa, wb = POOL_WINDOWS[2 * hf], POOL_WINDOWS[2 * hf + 1]
        x = full[:, hf * LANES:(hf + 1) * LANES]
        sums, w, s = {}, 2, pltpu.roll(x, 1, axis=0) + x
        sums[w] = s
        while w < wb:
            s = pltpu.roll(s, w // 2, axis=0) + pltpu.roll(s, n - w // 2, axis=0)
            w *= 2
            sums[w] = s
        means.append(jnp.where(first, sums[wa] * (1.0 / wa), sums[wb] * (1.0 / wb))[halo:halo + tm, :])
    mean = jnp.concatenate(means, axis=1)
    win = jnp.concatenate([jnp.full((1, gch), w, jnp.int32) for w in POOL_WINDOWS], axis=1)

    def rescale(rows, first_pos):
        pos = first_pos + lax.broadcasted_iota(jnp.int32, (halo, 1), 0)
        cnt = jnp.minimum(pos + win // 2, seq) - jnp.maximum(pos - win // 2, 0)
        return rows * (win.astype(F32) / cnt.astype(F32))

    mean = jnp.concatenate([rescale(mean[:halo], t0), mean[halo:tm - halo],
                            rescale(mean[tm - halo:], t0 + tm - halo)], axis=0)
    return mean - main


def _route(logits):
    tm = logits.shape[0]
    lt = logits.T
    gl = lt[N_EXPERTS:N_EXPERTS + N_GROUPS]
    sub_g = lax.broadcasted_iota(jnp.int32, gl.shape, 0)
    gmax = jnp.max(gl, axis=0, keepdims=True)
    grp = jnp.min(jnp.where(gl == gmax, sub_g, N_GROUPS), axis=0, keepdims=True)
    gate_group = 1.0 / jnp.sum(jnp.exp(gl - gmax), axis=0, keepdims=True)
    el = lt[0:EXPERTS_PER_GROUP]
    for g in range(1, N_GROUPS):
        el = jnp.where(grp == g, lt[g * EXPERTS_PER_GROUP:(g + 1) * EXPERTS_PER_GROUP], el)
    sub = lax.broadcasted_iota(jnp.int32, el.shape, 0)
    m1 = jnp.max(el, axis=0, keepdims=True)
    i1 = jnp.min(jnp.where(el == m1, sub, EXPERTS_PER_GROUP), axis=0, keepdims=True)
    el2 = jnp.where(sub == i1, -jnp.inf, el)
    m2 = jnp.max(el2, axis=0, keepdims=True)
    i2 = jnp.min(jnp.where(el2 == m2, sub, EXPERTS_PER_GROUP), axis=0, keepdims=True)
    r = jnp.exp(m2 - m1)
    g1 = gate_group / (1.0 + r)
    g2 = g1 * r
    e1 = (grp * EXPERTS_PER_GROUP + i1).astype(F32)
    e2 = (grp * EXPERTS_PER_GROUP + i2).astype(F32)
    rows = jnp.where(sub == 0, e1, jnp.where(sub == 1, e2, jnp.where(sub == 2, g1, jnp.where(sub == 3, g2, 0.0))))
    cols = jnp.concatenate([rows, jnp.zeros((LANES - rows.shape[0], tm), F32)], axis=0).T
    return cols, rows


def _merge_kernel(x_ref, p_ref, ug_ref, yf_ref, ya_ref, mod_ref, wpool_ref, pscale_ref, wsgu_ref, bsgu_ref,
                  wout_ref, lng_ref, lnb_ref, wr_ref, *rest,
                  mod_row, seq, n_alias):
    x1_ref, h2_ref, route_ref, route_t_ref = rest[n_alias:]
    tm, d = x_ref.shape
    if mod_row is None:
        row = pl.program_id(0)
    else:
        row = mod_row
    t0 = pl.multiple_of(pl.program_id(1) * tm, tm)
    m = mod_ref[pl.ds(row, 1), :]
    gate1, shift2, scale2 = m[:, 2 * d:3 * d], m[:, 3 * d:4 * d], m[:, 4 * d:5 * d]
    lane = lax.broadcasted_iota(jnp.int32, (1, GROUP_W), 1)
    n_heads = wsgu_ref.shape[0] // SGU_CHUNK
    head = lane // (GROUP_W // n_heads)

    pm = min(tm, MERGE_PART)
    for part in range(tm // pm):
        r0 = part * pm
        rows = slice(r0, r0 + pm)
        pooled = _pool(p_ref, t0 + r0, pm, seq)
        y_pool = _dot(pooled.astype(BF16), wpool_ref[...]) * pscale_ref[...]

        ug = ug_ref[rows, :].astype(F32)
        u = _gelu(ug[:, 0:GROUP_W])
        v = _layer_norm(_gelu(ug[:, GROUP_W:])).astype(BF16)
        mixed = []
        for cidx in range(pm // SGU_CHUNK):
            vc = v[cidx * SGU_CHUNK:(cidx + 1) * SGU_CHUNK, :]
            full = _dot(wsgu_ref[...], vc)
            mc = bsgu_ref[...]
            for hd in range(n_heads):
                mc = mc + jnp.where(head == hd, full[hd * SGU_CHUNK:(hd + 1) * SGU_CHUNK, :], 0.0)
            mixed.append(mc)
        y_sgu = u * jnp.concatenate(mixed, axis=0)

        cat = jnp.concatenate([yf_ref[rows, :].astype(BF16), ya_ref[rows, :], y_pool.astype(BF16),
                               y_sgu.astype(BF16)], axis=1)
        y = _dot(cat, wout_ref[...])
        x1 = _layer_norm(RES_ALPHA * x_ref[rows, :] + gate1 * y) * lng_ref[...] + lnb_ref[...]
        x1_ref[rows, :] = x1
        h2 = _layer_norm(x1) * (1.0 + scale2) + shift2
        _store_rows(h2_ref, h2, first=r0)
        lg = _dot(h2.astype(BF16), wr_ref[...])
        route_ref[rows, :], route_t_ref[:, rows] = _route(lg[:, 0:LANES] + lg[:, LANES:])


def _merge(x2, p, ug, y_four, y_attn, mod, w_pool_bd, pool_scale, w_sgu_stack, b_sgu_exp, w_out,
           ln_g, ln_b, w_route, aliased, *, mod_row, seq, tm, row_off, total_rows):
    rows, d = x2.shape
    n_batch, steps = rows // seq, seq // tm
    off = row_off // tm
    kern = functools.partial(_merge_kernel, mod_row=mod_row, seq=seq, n_alias=len(aliased))
    row_spec = lambda w: pl.BlockSpec((tm, w), lambda b, i: (b * steps + i, 0))
    const = lambda a: pl.BlockSpec(a.shape, lambda b, i: (0,) * a.ndim)
    consts = (mod, w_pool_bd, pool_scale, w_sgu_stack, b_sgu_exp, w_out, ln_g, ln_b, w_route)
    n_in = 5 + len(consts)
    out_shapes = [jax.ShapeDtypeStruct((total_rows, d), F32),
                  jax.ShapeDtypeStruct((total_rows * ROW_WORDS, LANES), jnp.uint32),
                  jax.ShapeDtypeStruct((total_rows, LANES), F32),
                  jax.ShapeDtypeStruct((SUBLANES, total_rows), F32)]
    out_specs = [pl.BlockSpec((tm, d), lambda b, i: (off + b * steps + i, 0)),
                 pl.BlockSpec((tm * ROW_WORDS, LANES), lambda b, i: (off + b * steps + i, 0)),
                 pl.BlockSpec((tm, LANES), lambda b, i: (off + b * steps + i, 0)),
                 pl.BlockSpec((SUBLANES, tm), lambda b, i: (0, off + b * steps + i))]
    return pl.pallas_call(
        kern,
        grid=(n_batch, steps),
        in_specs=[row_spec(d), pl.BlockSpec((seq, GROUP_W), lambda b, i: (b, 0)),
                  row_spec(2 * GROUP_W), row_spec(GROUP_W), row_spec(GROUP_W)]
                 + [const(a) for a in consts]
                 + [pl.BlockSpec(memory_space=pl.ANY)] * len(aliased),
        out_specs=out_specs,
        out_shape=out_shapes,
        input_output_aliases={n_in + k: k for k in range(len(aliased))},
        compiler_params=_cparams("arbitrary", "arbitrary"),
        name="merge",
    )(x2, p, ug, y_four, y_attn, *consts, *aliased)


def _plan_kernel(route_ref, dest_ref, cnt_out_ref, cnt_ref, start_ref, carry_ref):
    ph, t = pl.program_id(0), pl.program_id(1)
    tm = route_ref.shape[1]
    rt = route_ref[...]
    e1 = rt[0:1, :].astype(jnp.int32)
    e2 = rt[1:2, :].astype(jnp.int32)
    sub = lax.broadcasted_iota(jnp.int32, (N_EXPERTS, tm), 0)
    hit1, hit2 = sub == e1, sub == e2
    onehot = jnp.where(hit1 | hit2, 1.0, 0.0)
    tile_cnt = jnp.sum(onehot, axis=1, keepdims=True)

    @pl.when((ph == 0) & (t == 0))
    def _():
        cnt_ref[...] = jnp.zeros_like(cnt_ref)

    @pl.when(ph == 0)
    def _():
        cnt_ref[...] += tile_cnt

    @pl.when((ph == 1) & (t == 0))
    def _():
        cnt = cnt_ref[...]
        padded = jnp.floor((cnt + (MOE_BLOCK - 1.0)) * (1.0 / MOE_BLOCK)) * MOE_BLOCK
        row = lax.broadcasted_iota(jnp.int32, cnt.shape, 0)
        incl = padded
        sh = 1
        while sh < N_EXPERTS:
            incl = incl + jnp.where(row >= sh, pltpu.roll(incl, sh, axis=0), 0.0)
            sh *= 2
        start_ref[...] = incl - padded
        carry_ref[...] = jnp.zeros_like(carry_ref)
        cnt_out_ref[...] = cnt

    @pl.when(ph == 1)
    def _():
        r_i = lax.broadcasted_iota(jnp.int32, (tm, tm), 0)
        c_i = lax.broadcasted_iota(jnp.int32, (tm, tm), 1)
        before = jnp.where(r_i < c_i, 1.0, 0.0).astype(BF16)
        rank = _dot(onehot.astype(BF16), before)
        base = start_ref[:, 0:1] + carry_ref[:, 0:1] + rank
        d1 = jnp.sum(jnp.where(hit1, base, 0.0), axis=0, keepdims=True)
        d2 = jnp.sum(jnp.where(hit2, base, 0.0), axis=0, keepdims=True)
        sub8 = lax.broadcasted_iota(jnp.int32, (SUBLANES, tm), 0)
        dest_ref[...] = jnp.where(sub8 == 0, d1, d2).astype(jnp.int32)
        carry_ref[...] += tile_cnt


def _plan(route_t, tm):
    rows = route_t.shape[1]
    n_t = rows // tm
    return pl.pallas_call(
        _plan_kernel,
        grid=(2, n_t),
        in_specs=[pl.BlockSpec((SUBLANES, tm), lambda ph, t: (0, t))],
        out_specs=[pl.BlockSpec((None, SUBLANES, tm), lambda ph, t: (t * ph, 0, 0)),
                   pl.BlockSpec((N_EXPERTS, LANES), lambda ph, t: (0, 0))],
        out_shape=[jax.ShapeDtypeStruct((n_t, SUBLANES, tm), jnp.int32),
                   jax.ShapeDtypeStruct((N_EXPERTS, LANES), F32)],
        scratch_shapes=[pltpu.VMEM((N_EXPERTS, LANES), F32)] * 3,
        compiler_params=_cparams("arbitrary", "arbitrary"),
        name="moe_plan",
    )(route_t)


def _row_copy(src_ref, src_row, dst_ref, dst_row, sem):
    return pltpu.make_async_copy(
        src_ref.at[pl.ds(pl.multiple_of(src_row * ROW_WORDS, ROW_WORDS), ROW_WORDS)],
        dst_ref.at[pl.ds(pl.multiple_of(dst_row * ROW_WORDS, ROW_WORDS), ROW_WORDS)], sem)


def _dispatch_kernel(dest_ref, h2_ref, xs_ref, sem, *, tm):
    def body(r4, c):
        for u in range(DMA_UNROLL):
            r = r4 * DMA_UNROLL + u
            for k in range(2):
                _row_copy(h2_ref, r, xs_ref, dest_ref[0, 0, k * tm + r], sem).start(priority=k)
        return c

    lax.fori_loop(0, tm // DMA_UNROLL, body, 0)
    for k in range(2):
        pltpu.make_async_copy(h2_ref, xs_ref.at[pl.ds(0, tm * ROW_WORDS)], sem).wait()


def _dispatch(dest, h2_tiles, n_slots, tm):
    n_t = dest.shape[0]
    return pl.pallas_call(
        functools.partial(_dispatch_kernel, tm=tm),
        grid=(n_t,),
        in_specs=[pl.BlockSpec((1, 1, 2 * tm), lambda i: (i, 0, 0), memory_space=pltpu.SMEM),
                  pl.BlockSpec((tm * ROW_WORDS, LANES), lambda i: (i, 0))],
        out_specs=pl.BlockSpec(memory_space=pl.ANY),
        out_shape=jax.ShapeDtypeStruct((n_slots * ROW_WORDS, LANES), jnp.uint32),
        scratch_shapes=[pltpu.SemaphoreType.DMA],
        compiler_params=_cparams("arbitrary"),
        name="moe_dispatch",
    )(dest, h2_tiles)


def _expert_kernel(be_ref, bn_ref, first_ref, slot_ref, nxt_ref, nxt2_ref, _blk_ref,
                   x_ref, wg_hbm, wu_hbm, wd_hbm, y_ref,
                   wg_buf, wu_buf, wd_buf, wg_bf, wu_bf, wd_bf, sem, *, layer):
    i = pl.program_id(0)
    mb = x_ref.shape[0] // ROW_WORDS
    half = wg_bf.shape[0] // 2

    def weight_copies(e, s):
        return [pltpu.make_async_copy(wg_hbm.at[layer, e], wg_buf.at[s], sem.at[s]),
                pltpu.make_async_copy(wu_hbm.at[layer, e], wu_buf.at[s], sem.at[s]),
                pltpu.make_async_copy(wd_hbm.at[layer, e], wd_buf.at[s], sem.at[s])]

    @pl.when(i == 0)
    def _():
        for cp in weight_copies(be_ref[0], 0):
            cp.start()

        @pl.when(nxt_ref[0] >= 0)
        def _():
            for cp in weight_copies(nxt_ref[0], 1):
                cp.start()

    @pl.when(first_ref[i] == 1)
    def _():
        s = slot_ref[i]
        for cp in weight_copies(be_ref[i], s):
            cp.wait()

        @pl.when(nxt2_ref[i] >= 0)
        def _():
            s2 = jnp.where(s == 0, WEIGHT_SLOTS - 1, s - 1)
            for cp in weight_copies(nxt2_ref[i], s2):
                cp.start()

        wg_bf[...] = wg_buf[s].astype(BF16)
        wu_bf[...] = wu_buf[s].astype(BF16)
        wd_bf[...] = wd_buf[s].astype(BF16)

    def compute(m):
        live = lax.broadcasted_iota(jnp.int32, (m, 1), 0) < bn_ref[i]
        x_lo, x_hi = _load_rows(x_ref, 0, m)
        x_lo = jnp.where(live, x_lo, 0.0).astype(BF16)
        x_hi = jnp.where(live, x_hi, 0.0).astype(BF16)
        g = _dot(x_lo, wg_bf[0:half, :]) + _dot(x_hi, wg_bf[half:, :])
        u = _dot(x_lo, wu_bf[0:half, :]) + _dot(x_hi, wu_bf[half:, :])
        hid = (_silu(g) * u).astype(BF16)
        _store_rows(y_ref, _dot(hid, wd_bf[...]))

    for parts in range(1, mb // MOE_PART + 1):
        @pl.when((bn_ref[i] > (parts - 1) * MOE_PART) & (bn_ref[i] <= parts * MOE_PART))
        def _(parts=parts):
            compute(parts * MOE_PART)


def _experts(table, xs, w_gate, w_up, w_down, layer):
    n_blocks = table[0].shape[0]
    _, _, d, de = w_gate.shape
    shape = (MOE_BLOCK * ROW_WORDS, LANES)
    hbm = pl.BlockSpec(memory_space=pl.ANY)
    return pl.pallas_call(
        functools.partial(_expert_kernel, layer=layer),
        grid_spec=pltpu.PrefetchScalarGridSpec(
            num_scalar_prefetch=len(table),
            grid=(n_blocks,),
            in_specs=[pl.BlockSpec(shape, lambda i, *t: (t[-1][i], 0)), hbm, hbm, hbm],
            out_specs=pl.BlockSpec(shape, lambda i, *t: (t[-1][i], 0)),
            scratch_shapes=[pltpu.VMEM((WEIGHT_SLOTS, d, de), F32), pltpu.VMEM((WEIGHT_SLOTS, d, de), F32),
                            pltpu.VMEM((WEIGHT_SLOTS, de, d), F32),
                            pltpu.VMEM((d, de), BF16), pltpu.VMEM((d, de), BF16), pltpu.VMEM((de, d), BF16),
                            pltpu.SemaphoreType.DMA((WEIGHT_SLOTS,))]),
        out_shape=jax.ShapeDtypeStruct(xs.shape, jnp.uint32),
        compiler_params=_cparams("arbitrary"),
        name="moe_experts",
    )(*table, xs, w_gate, w_up, w_down)


def _combine_body(dest_ref, dest_next_ref, x1_ref, route_ref, m, lng_ref, lnb_ref, y_ref, o_ref, buf_ref, sem):
    tm, d = x1_ref.shape
    i = pl.program_id(0)
    n = pl.num_programs(0)
    slot = i % 2
    pr = tm // COMBINE_PARTS

    def gather(idx_ref, s, first, count):
        def body(r4, c):
            for u in range(DMA_UNROLL):
                r = first + r4 * DMA_UNROLL + u
                for k in range(2):
                    _row_copy(y_ref, idx_ref[0, 0, k * tm + r], buf_ref.at[s], k * tm + r,
                              sem.at[s]).start(priority=k)
            return c

        lax.fori_loop(0, count // DMA_UNROLL, body, 0)

    @pl.when(i == 0)
    def _():
        gather(dest_ref, 0, 0, tm)

    for k in range(2):
        pltpu.make_async_copy(y_ref.at[pl.ds(0, tm * ROW_WORDS)],
                              buf_ref.at[slot, pl.ds(0, tm * ROW_WORDS)], sem.at[slot]).wait()

    gate2 = m[:, 5 * d:6 * d]
    for part in range(COMBINE_PARTS):
        @pl.when(i + 1 < n)
        def _(part=part):
            gather(dest_next_ref, 1 - slot, part * pr, pr)

        rows = slice(part * pr, (part + 1) * pr)
        rt = route_ref[rows, :]
        f = jnp.zeros((pr, d), F32)
        for k in range(2):
            lo, hi = _load_rows(buf_ref.at[slot], k * tm + part * pr, pr)
            f = f + jnp.concatenate([lo, hi], axis=1) * rt[:, 2 + k:3 + k]
        o_ref[rows, :] = _layer_norm(RES_ALPHA * x1_ref[rows, :] + gate2 * f) * lng_ref[...] + lnb_ref[...]


def _combine_kernel(dest_ref, dest_next_ref, x1_ref, route_ref, mod_ref, lng_ref, lnb_ref, y_ref, o_ref,
                    buf_ref, sem, *, mod_row, rows_per_batch):
    m = _mod_row(mod_ref, mod_row, x1_ref.shape[0], rows_per_batch)
    _combine_body(dest_ref, dest_next_ref, x1_ref, route_ref, m, lng_ref, lnb_ref, y_ref, o_ref, buf_ref, sem)


def _combine_proj_kernel(dest_ref, dest_next_ref, x1_ref, route_ref, mod_ref, lng_ref, lnb_ref,
                         mod_next_ref, w_ref, cos_ref, sin_ref, y_ref, o_ref, *rest,
                         rows_per_batch, a_pitch):
    outs, (buf_ref, sem) = rest[:-2], rest[-2:]
    tm = x1_ref.shape[0]
    _combine_body(dest_ref, dest_next_ref, x1_ref, route_ref, _mod_row(mod_ref, None, tm, rows_per_batch),
                  lng_ref, lnb_ref, y_ref, o_ref, buf_ref, sem)
    _proj_body(o_ref[...], _mod_row(mod_next_ref, None, tm, rows_per_batch), w_ref, cos_ref, sin_ref, outs,
               rope=True, a_pitch=a_pitch)


def _combine_specs(tm, d, off, steps):
    return [pl.BlockSpec((1, 1, 2 * tm), lambda i: (off + i, 0, 0), memory_space=pltpu.SMEM),
            pl.BlockSpec((1, 1, 2 * tm), lambda i: (off + jnp.minimum(i + 1, steps - 1), 0, 0),
                         memory_space=pltpu.SMEM),
            pl.BlockSpec((tm, d), lambda i: (off + i, 0)),
            pl.BlockSpec((tm, LANES), lambda i: (off + i, 0))]


def _combine_scratch(tm):
    return [pltpu.VMEM((2, 2 * tm * ROW_WORDS, LANES), jnp.uint32), pltpu.SemaphoreType.DMA((2,))]


def _combine(dest, x1, route, mod, ln_g, ln_b, y_tiles, *, tm, row_off, rows, mod_row, rows_per_batch):
    d = x1.shape[1]
    steps = rows // tm
    kern = functools.partial(_combine_kernel, mod_row=mod_row, rows_per_batch=rows_per_batch)
    const = lambda a: pl.BlockSpec(a.shape, lambda i: (0,) * a.ndim)
    return pl.pallas_call(
        kern,
        grid=(steps,),
        in_specs=_combine_specs(tm, d, row_off // tm, steps)
                 + [const(mod), const(ln_g), const(ln_b), pl.BlockSpec(memory_space=pl.ANY)],
        out_specs=pl.BlockSpec((tm, d), lambda i: (i, 0)),
        out_shape=jax.ShapeDtypeStruct((rows, d), F32),
        scratch_shapes=_combine_scratch(tm),
        compiler_params=_cparams("arbitrary"),
        name="moe_combine",
    )(dest, dest, x1, route, mod, ln_g, ln_b, y_tiles)


def _combine_proj(dest, x1, route, mod, ln_g, ln_b, y_tiles, mod_next, w_in, cos_t, sin_t, *,
                  tm, rows, rows_per_batch, a_pitch):
    d = x1.shape[1]
    steps = rows // tm
    kern = functools.partial(_combine_proj_kernel, rows_per_batch=rows_per_batch, a_pitch=a_pitch)
    const = lambda a: pl.BlockSpec(a.shape, lambda i: (0,) * a.ndim)
    table_spec, proj_specs, proj_shapes = _proj_specs(rows, tm, a_pitch, cos_t.shape[0] // tm)
    outs = pl.pallas_call(
        kern,
        grid=(steps,),
        in_specs=_combine_specs(tm, d, 0, steps)
                 + [const(mod), const(ln_g), const(ln_b), const(mod_next), const(w_in), table_spec, table_spec,
                    pl.BlockSpec(memory_space=pl.ANY)],
        out_specs=[pl.BlockSpec((tm, d), lambda i: (i, 0))] + proj_specs,
        out_shape=[jax.ShapeDtypeStruct((rows, d), F32)] + proj_shapes,
        scratch_shapes=_combine_scratch(tm),
        compiler_params=_cparams("arbitrary"),
        name="moe_combine_proj",
    )(dest, dest, x1, route, mod, ln_g, ln_b, mod_next, w_in, cos_t, sin_t, y_tiles)
    return outs[0], outs[1:]


def _rope_tables(n_pos):
    rows = n_pos // GRID_W
    row = jnp.repeat(jnp.arange(rows), GRID_W).astype(F32)
    col = jnp.tile(jnp.arange(GRID_W), rows).astype(F32)
    n_freq = HEAD_DIM // 4
    freq = ROPE_BASE ** (-jnp.arange(n_freq, dtype=F32) / n_freq)
    ang_r, ang_c = row[:, None] * freq, col[:, None] * freq
    cos_h = jnp.concatenate([jnp.cos(ang_r)] * 2 + [jnp.cos(ang_c)] * 2, axis=1)
    sin_h = jnp.concatenate([-jnp.sin(ang_r), jnp.sin(ang_r), -jnp.sin(ang_c), jnp.sin(ang_c)], axis=1)
    return jnp.tile(cos_h, (1, 2)), jnp.tile(sin_h, (1, 2))


def _block_table(counts, n_blocks):
    cnt = counts.astype(jnp.int32)
    padded = (cnt + MOE_BLOCK - 1) // MOE_BLOCK * MOE_BLOCK
    pad_end = jnp.cumsum(padded)
    pad_start = pad_end - padded
    blk_start = jnp.arange(n_blocks, dtype=jnp.int32)[:, None] * MOE_BLOCK
    be = jnp.minimum(jnp.sum((pad_end[None, :] <= blk_start).astype(jnp.int32), axis=1), N_EXPERTS - 1)
    ids = jnp.arange(N_EXPERTS, dtype=jnp.int32)
    mine = be[:, None] == ids[None, :]
    fill = jnp.sum(jnp.where(mine, cnt[None, :] + pad_start[None, :], 0), axis=1) - blk_start[:, 0]
    bn = jnp.clip(fill, 0, MOE_BLOCK)
    prev = jnp.concatenate([jnp.full((1,), -1, jnp.int32), be[:-1]])
    first = ((bn > 0) & (be != prev)).astype(jnp.int32)
    slot = (jnp.cumsum(first) - 1) % WEIGHT_SLOTS
    later = (ids[None, :] > ids[:, None]) & (cnt[None, :] > 0)
    nxt_e = jnp.min(jnp.where(later, ids[None, :], N_EXPERTS), axis=1)
    hop = nxt_e[:, None] == ids[None, :]
    nxt2_e = jnp.sum(jnp.where(hop, nxt_e[None, :], 0), axis=1) + jnp.where(nxt_e == N_EXPERTS, N_EXPERTS, 0)
    lookup = lambda tab: jnp.sum(jnp.where(mine, jnp.where(tab >= N_EXPERTS, -1, tab)[None, :], 0), axis=1)
    blk_idx = jnp.minimum(blk_start[:, 0] // MOE_BLOCK, jnp.maximum(jnp.sum((bn > 0).astype(jnp.int32)) - 1, 0))
    return (be, bn, first, slot.astype(jnp.int32), lookup(nxt_e).astype(jnp.int32),
            lookup(nxt2_e).astype(jnp.int32), blk_idx.astype(jnp.int32))


def _moe(route_t, h2_tiles, w_gate, w_up, w_down, layer):
    rows = route_t.shape[1]
    tm = ROW_TILE
    n_blocks = -(-(2 * rows) // MOE_BLOCK) + N_EXPERTS
    dest8, counts = _plan(route_t, tm)
    dest = dest8[:, 0:2, :].reshape(rows // tm, 1, 2 * tm)
    table = _block_table(counts[:, 0], n_blocks)
    xs = _dispatch(dest, h2_tiles, n_blocks * MOE_BLOCK, tm)
    ys = _experts(table, xs, w_gate, w_up, w_down, layer)
    tc = COMBINE_TILE
    dest_c = dest8[:, 0:2, :].reshape(rows // tm, 2, tm // tc, tc).transpose(0, 2, 1, 3).reshape(rows // tc, 1, 2 * tc)
    return dest_c, ys


def kernel(x, c, ctx, c_ctx, w_ada, b_ada, w_in, w_fourier, attn_sink, w_pool, pool_scale, w_sgu, b_sgu,
           w_out, ln1_g, ln1_b, w_router_group, w_router_expert, w_exp_gate, w_exp_up, w_exp_down,
           ln2_g, ln2_b):
    b, s, d = x.shape
    n_ctx = ctx.shape[1]
    n_layers = w_in.shape[0]
    tm = ROW_TILE
    cond = jnp.concatenate([c, c_ctx[None, :], jnp.zeros((SUBLANES - b - 1, d), F32)], axis=0)
    mod_all = _ada(cond, w_ada, b_ada[:, None, :])
    cos_t, sin_t = _rope_tables(s)
    x2 = x.reshape(b * s, d)
    c2 = ctx.reshape(b * n_ctx, d)
    n_sgu = w_sgu.shape[1]
    lat = None
    for layer in range(n_layers):
        last = layer == n_layers - 1
        mod = mod_all[layer]
        w_in_l = w_in[layer].astype(BF16)
        wf = w_fourier[layer].astype(BF16)
        w_pool_bd = jax.scipy.linalg.block_diag(*[w_pool[layer, g] for g in range(w_pool.shape[1])]).astype(BF16)
        w_sgu_stack = w_sgu[layer].reshape(n_sgu * SGU_CHUNK, SGU_CHUNK).astype(BF16)
        b_sgu_exp = jnp.repeat(b_sgu[layer].T, GROUP_W // n_sgu, axis=1)
        w_router = jnp.concatenate([w_router_expert[layer].reshape(d, N_EXPERTS), w_router_group[layer]], axis=1)
        w_router = jnp.pad(w_router, ((0, 0), (0, LANES - w_router.shape[1])))
        wr_hi = w_router.astype(BF16)
        w_route = jnp.concatenate([wr_hi, (w_router - wr_hi.astype(F32)).astype(BF16)], axis=1)
        merge_consts = (mod, w_pool_bd, pool_scale[layer][None, :], w_sgu_stack, b_sgu_exp,
                        w_out[layer].astype(BF16), ln1_g[layer][None, :], ln1_b[layer][None, :], w_route)
        sink = attn_sink[layer]

        if lat is None:
            lat = _proj(x2, mod, w_in_l, cos_t, sin_t, mod_row=None, rows_per_batch=s,
                        rope=True, tm=tm, a_pitch=FFT_PITCH)
        a, q, qs, k, v, p, ug = lat
        ac, qc, qsc, kc, vc, pc, ugc = _proj(c2, mod, w_in_l, cos_t, sin_t, mod_row=b, rows_per_batch=n_ctx,
                                             rope=False, tm=n_ctx, a_pitch=FFT_R)
        y_four = _fourier(a, wf, s)
        y_attn = _attention(sink, q, qs, k, v, kc, vc, seq=s, n_ctx=n_ctx, band=True)
        total = b * s + (0 if last else b * n_ctx)
        merged = _merge(x2, p, ug, y_four, y_attn, *merge_consts, (),
                        mod_row=None, seq=s, tm=tm, row_off=0, total_rows=total)
        if not last:
            yc_four = _fourier_small(ac, wf, n_ctx)
            yc_attn = _attention(sink, qc, qsc, kc, vc, kc, vc, seq=n_ctx, n_ctx=n_ctx, band=False)
            merged = _merge(c2, pc, ugc, yc_four, yc_attn, *merge_consts, tuple(merged),
                            mod_row=b, seq=n_ctx, tm=n_ctx, row_off=b * s, total_rows=total)
        x1, h2_tiles, route, route_t = merged
        dest, ys = _moe(route_t, h2_tiles, w_exp_gate, w_exp_up, w_exp_down, layer)
        ln_g, ln_b = ln2_g[layer][None, :], ln2_b[layer][None, :]
        if last:
            x2 = _combine(dest, x1, route, mod, ln_g, ln_b, ys, tm=COMBINE_TILE, row_off=0, rows=b * s,
                          mod_row=None, rows_per_batch=s)
        else:
            x2, lat = _combine_proj(dest, x1, route, mod, ln_g, ln_b, ys, mod_all[layer + 1],
                                    w_in[layer + 1].astype(BF16), cos_t, sin_t,
                                    tm=COMBINE_TILE, rows=b * s, rows_per_batch=s, a_pitch=FFT_PITCH)
            c2 = _combine(dest, x1, route, mod, ln_g, ln_b, ys, tm=COMBINE_TILE, row_off=b * s,
                          rows=b * n_ctx, mod_row=b, rows_per_batch=n_ctx)
    return x2.reshape(b, s, d)
```

```python
import functools
import math

import numpy as np
import jax
import jax.numpy as jnp
from jax import lax
from jax.experimental import pallas as pl
from jax.experimental.pallas import tpu as pltpu

GRID_W = 64
HEAD_DIM = 64
GROUP_W = 256
KV_W = 128
WINDOW = 128
POOL_WINDOWS = (2, 4, 8, 16)
SGU_CHUNK = 128
N_GROUPS = 4
EXPERTS_PER_GROUP = 8
N_EXPERTS = 32
ROPE_BASE = 10000.0
LN_EPS = 1e-6
NEG_INF = -1e30
DEPTH = 2
RES_ALPHA = (2 * DEPTH) ** 0.25

LANES = 128
SUBLANES = 8
VMEM_LIMIT = 48 * 1024 * 1024

ROW_TILE = 1024
Q_BLOCK = 1024
COMBINE_TILE = 512
WEIGHT_SLOTS = 3
MOE_BLOCK = 512
MOE_PART = 256
FFT_R = 64
FFT_PITCH = 72
FFT_UNROLL = 16
DMA_UNROLL = 8
MERGE_PART = 512

BF16 = jnp.bfloat16
F32 = jnp.float32


def _cparams(*sem):
    return pltpu.CompilerParams(dimension_semantics=sem, vmem_limit_bytes=VMEM_LIMIT)


def _dot(a, b):
    return jnp.dot(a, b, preferred_element_type=F32)


def _dot_nt(a, b):
    return lax.dot_general(a, b, (((1,), (1,)), ((), ())), preferred_element_type=F32)


def _layer_norm(t):
    mu = jnp.mean(t, axis=-1, keepdims=True)
    d = t - mu
    var = jnp.mean(d * d, axis=-1, keepdims=True)
    return d * lax.rsqrt(var + LN_EPS)


def _silu(t):
    return t * (1.0 / (1.0 + jnp.exp(-t)))


def _gelu(t):
    return 0.5 * t * (1.0 + lax.erf(t * (1.0 / math.sqrt(2.0))))


ROW_WORDS = 4
HI_MASK = 0xFFFF0000


def _pack_rows(t):
    half = t.shape[1] // 2
    lo = lax.bitcast_convert_type(t[:, :half].astype(BF16).astype(F32), jnp.uint32)
    hi = lax.bitcast_convert_type(t[:, half:].astype(BF16).astype(F32), jnp.uint32)
    return (lo >> 16) | (hi & jnp.uint32(HI_MASK))


def _unpack_rows(w):
    return (lax.bitcast_convert_type(w << 16, F32),
            lax.bitcast_convert_type(w & jnp.uint32(HI_MASK), F32))


def _store_rows(ref, t, first=0):
    w = _pack_rows(t)
    for j in range(ROW_WORDS):
        ref[pl.ds(first * ROW_WORDS + j, t.shape[0], stride=ROW_WORDS), :] = w[:, j * LANES:(j + 1) * LANES]


def _load_rows(ref, first, m):
    w = jnp.concatenate([ref[pl.ds(first * ROW_WORDS + j, m, stride=ROW_WORDS), :] for j in range(ROW_WORDS)],
                        axis=1)
    return _unpack_rows(w)


def _ada_kernel(c_ref, w_ref, b_ref, o_ref):
    s = _silu(c_ref[...]).astype(BF16)
    o_ref[...] = _dot(s, w_ref[...].astype(BF16)) + b_ref[...]


def _ada(cond, w_ada, b_ada):
    n_layers, d, n = w_ada.shape
    tn = n // 4
    return pl.pallas_call(
        _ada_kernel,
        grid=(n_layers, n // tn),
        in_specs=[
            pl.BlockSpec((SUBLANES, d), lambda l, j: (0, 0)),
            pl.BlockSpec((None, d, tn), lambda l, j: (l, 0, j)),
            pl.BlockSpec((None, 1, tn), lambda l, j: (l, 0, j)),
        ],
        out_specs=pl.BlockSpec((None, SUBLANES, tn), lambda l, j: (l, 0, j)),
        out_shape=jax.ShapeDtypeStruct((n_layers, SUBLANES, n), F32),
        compiler_params=_cparams("arbitrary", "arbitrary"),
        name="ada",
    )(cond, w_ada, b_ada)


def _rope(t, cos_t, sin_t):
    lane = lax.broadcasted_iota(jnp.int32, t.shape, 1)
    first = (lane % 32) < 16
    partner = jnp.where(first, pltpu.roll(t, LANES - 16, axis=1), pltpu.roll(t, 16, axis=1))
    return t * cos_t + partner * sin_t


def _proj_body(x, m, w_ref, cos_ref, sin_ref, outs, *, rope, a_pitch):
    a_ref, q_ref, qs_ref, k_ref, v_ref, p_ref, ug_ref = outs
    tm, d = x.shape
    shift, scale = m[:, 0:d], m[:, d:2 * d]
    h = _layer_norm(x) * (1.0 + scale) + shift
    z = _dot(h.astype(BF16), w_ref[...])
    pad = jnp.zeros((a_pitch - FFT_R, LANES), F32)
    for g in range(tm // FFT_R):
        for hf in range(2):
            grp = z[g * FFT_R:(g + 1) * FFT_R, hf * LANES:(hf + 1) * LANES]
            if a_pitch > FFT_R:
                grp = jnp.concatenate([grp, pad], axis=0)
            a_ref[hf, g * a_pitch:(g + 1) * a_pitch, :] = grp
    q0, q1 = z[:, 256:384], z[:, 384:512]
    k = z[:, 512:640]
    if rope:
        cos_t, sin_t = cos_ref[...], sin_ref[...]
        q0, q1, k = _rope(q0, cos_t, sin_t), _rope(q1, cos_t, sin_t), _rope(k, cos_t, sin_t)
    q_ref[:, 0:128] = q0.astype(BF16)
    q_ref[:, 128:256] = q1.astype(BF16)
    qs_ref[:, 0:128] = pltpu.roll(q0, HEAD_DIM, axis=1).astype(BF16)
    qs_ref[:, 128:256] = pltpu.roll(q1, HEAD_DIM, axis=1).astype(BF16)
    k_ref[...] = k.astype(BF16)
    v_ref[...] = z[:, 640:768].astype(BF16)
    p_ref[...] = z[:, 768:1024].astype(BF16)
    ug_ref[...] = z[:, 1024:1536].astype(BF16)


def _mod_row(mod_ref, mod_row, tm, rows_per_batch):
    row = (pl.program_id(0) * tm) // rows_per_batch if mod_row is None else mod_row
    return mod_ref[pl.ds(row, 1), :]


def _proj_kernel(x_ref, mod_ref, w_ref, cos_ref, sin_ref, *outs, mod_row, rows_per_batch, rope, a_pitch):
    m = _mod_row(mod_ref, mod_row, x_ref.shape[0], rows_per_batch)
    _proj_body(x_ref[...], m, w_ref, cos_ref, sin_ref, outs, rope=rope, a_pitch=a_pitch)


def _proj_specs(rows, tm, a_pitch, seq_steps):
    row_spec = lambda w: pl.BlockSpec((tm, w), lambda i: (i, 0))
    out_w = (256, 256, 128, 128, 256, 512)
    ta = tm // FFT_R * a_pitch
    out_specs = [pl.BlockSpec((2, ta, LANES), lambda i: (0, i, 0))] + [row_spec(w) for w in out_w]
    out_shape = ([jax.ShapeDtypeStruct((2, rows // FFT_R * a_pitch, LANES), F32)]
                 + [jax.ShapeDtypeStruct((rows, w), BF16) for w in out_w])
    table_spec = pl.BlockSpec((tm, LANES), lambda i: (i % seq_steps, 0))
    return table_spec, out_specs, out_shape


def _proj(x2, mod, w_in, cos_t, sin_t, *, mod_row, rows_per_batch, rope, tm, a_pitch):
    rows, d = x2.shape
    kern = functools.partial(_proj_kernel, mod_row=mod_row, rows_per_batch=rows_per_batch,
                             rope=rope, a_pitch=a_pitch)
    table_spec, out_specs, out_shape = _proj_specs(rows, tm, a_pitch, cos_t.shape[0] // tm)
    return pl.pallas_call(
        kern,
        grid=(rows // tm,),
        in_specs=[
            pl.BlockSpec((tm, d), lambda i: (i, 0)),
            pl.BlockSpec(mod.shape, lambda i: (0, 0)),
            pl.BlockSpec(w_in.shape, lambda i: (0, 0)),
            table_spec, table_spec,
        ],
        out_specs=out_specs,
        out_shape=out_shape,
        compiler_params=_cparams("arbitrary"),
        name="proj",
    )(x2, mod, w_in, cos_t, sin_t)


def _fft_tables(n_pos):
    r = FFT_R
    assert n_pos == r * r
    kb = np.arange(r)[None, :, None]
    na = np.arange(r)[:, None, None]
    nb = np.arange(r)[None, None, :]
    ang = 2.0 * np.pi * ((kb * (na + r * nb)) % n_pos) / n_pos
    m1 = np.concatenate([np.cos(ang), -np.sin(ang)], axis=1)
    ka = np.arange(r)[:, None]
    n2 = np.arange(r)[None, :]
    ang2 = 2.0 * np.pi * ((ka * n2) % r) / r
    c2, s2 = np.cos(ang2), np.sin(ang2)
    w2 = np.block([[c2, s2], [-s2, c2]])
    return m1, w2


def _channel_tables(n_pos):
    h = HEAD_DIM
    c = np.arange(h)
    ang = 2.0 * np.pi * ((c[:, None] * c[None, :]) % h) / h
    scale = 1.0 / math.sqrt(n_pos * h)
    eye = np.eye(GROUP_W // h)
    cc = np.kron(eye, np.cos(ang)) * scale
    ss = np.kron(eye, np.sin(ang)) * scale
    return np.concatenate([cc, ss], axis=0)


def _fourier_kernel(a_ref, m1_ref, w2_ref, ch_ref, wf_ref, o_ref, z_ref, y_ref):
    r, pt = FFT_R, FFT_PITCH

    def step1(i, c):
        for u in range(FFT_UNROLL):
            na = i * FFT_UNROLL + u
            rows = jnp.concatenate([a_ref[0, pl.ds(na, r, stride=pt), :],
                                    a_ref[1, pl.ds(na, r, stride=pt), :]], axis=1)
            z = _dot(m1_ref[na], rows.astype(BF16))
            base = pl.multiple_of(na * pt, SUBLANES)
            z_ref[0, pl.ds(base, r), :] = z[0:r, 0:LANES]
            z_ref[1, pl.ds(base, r), :] = z[0:r, LANES:]
            z_ref[2, pl.ds(base, r), :] = z[r:, 0:LANES]
            z_ref[3, pl.ds(base, r), :] = z[r:, LANES:]
        return c

    lax.fori_loop(0, r // FFT_UNROLL, step1, 0)

    def step2(i, c):
        for u in range(FFT_UNROLL):
            kb = i * FFT_UNROLL + u
            q = [z_ref[j, pl.ds(kb, r, stride=pt), :] for j in range(4)]
            zs = jnp.concatenate([jnp.concatenate(q[0:2], axis=1),
                                  jnp.concatenate(q[2:4], axis=1)], axis=0)
            y = _dot(w2_ref[...], zs.astype(BF16))
            base = pl.multiple_of(kb * r, r)
            y_ref[0, pl.ds(base, r), :] = y[0:r, 0:LANES]
            y_ref[1, pl.ds(base, r), :] = y[0:r, LANES:]
            y_ref[2, pl.ds(base, r), :] = y[r:, 0:LANES]
            y_ref[3, pl.ds(base, r), :] = y[r:, LANES:]
        return c

    lax.fori_loop(0, r // FFT_UNROLL, step2, 0)

    chunk = 8 * r
    for cidx in range(r * r // chunk):
        yy = jnp.concatenate([y_ref[j, cidx * chunk:(cidx + 1) * chunk, :] for j in range(4)], axis=1)
        f = _dot(yy.astype(BF16), ch_ref[...])
        g = _dot(f.astype(BF16), wf_ref[...])
        for gi in range(chunk // r):
            kb = cidx * (chunk // r) + gi
            z_ref[0, kb * pt:kb * pt + r, :] = g[gi * r:(gi + 1) * r, 0:LANES]
            z_ref[1, kb * pt:kb * pt + r, :] = g[gi * r:(gi + 1) * r, LANES:]

    def step3(i, c):
        for u in range(FFT_UNROLL):
            ka = i * FFT_UNROLL + u
            base = pl.multiple_of(ka * r, r)
            o_ref[pl.ds(base, r), 0:LANES] = z_ref[0, pl.ds(ka, r, stride=pt), :]
            o_ref[pl.ds(base, r), LANES:] = z_ref[1, pl.ds(ka, r, stride=pt), :]
        return c

    lax.fori_loop(0, r // FFT_UNROLL, step3, 0)


def _fourier(a3, w_fourier, n_pos):
    rows = a3.shape[1] // FFT_PITCH * FFT_R
    gw = GROUP_W
    m1, w2 = _fft_tables(n_pos)
    ch = _channel_tables(n_pos)
    const = lambda shape: pl.BlockSpec(shape, lambda b: (0,) * len(shape))
    return pl.pallas_call(
        _fourier_kernel,
        grid=(rows // n_pos,),
        in_specs=[
            pl.BlockSpec((2, FFT_R * FFT_PITCH, LANES), lambda b: (0, b, 0)),
            const(m1.shape), const(w2.shape), const(ch.shape), const(w_fourier.shape),
        ],
        out_specs=pl.BlockSpec((n_pos, gw), lambda b: (b, 0)),
        out_shape=jax.ShapeDtypeStruct((rows, gw), F32),
        scratch_shapes=[pltpu.VMEM((4, FFT_R * FFT_PITCH, LANES), F32), pltpu.VMEM((4, n_pos, LANES), F32)],
        compiler_params=_cparams("arbitrary"),
        name="fourier",
    )(a3, jnp.asarray(m1, BF16), jnp.asarray(w2, BF16), jnp.asarray(ch, BF16), w_fourier)


def _fourier_small_kernel(a_ref, cs_ref, ch_ref, wf_ref, o_ref):
    n = a_ref.shape[1]
    a = jnp.concatenate([a_ref[0], a_ref[1]], axis=1)
    pq = _dot(cs_ref[...], a.astype(BF16))
    y = jnp.concatenate([pq[0:n], pq[n:2 * n]], axis=1).astype(BF16)
    f = _dot(y, ch_ref[...])
    o_ref[...] = _dot(f.astype(BF16), wf_ref[...])


def _fourier_small(a3, w_fourier, n_pos):
    _, rows, _ = a3.shape
    gw = GROUP_W
    k = np.arange(n_pos)
    ang = 2.0 * np.pi * ((k[:, None] * k[None, :]) % n_pos) / n_pos
    cs = np.concatenate([np.cos(ang), -np.sin(ang)], axis=0)
    ch = _channel_tables(n_pos)
    const = lambda shape: pl.BlockSpec(shape, lambda b: (0,) * len(shape))
    return pl.pallas_call(
        _fourier_small_kernel,
        grid=(rows // n_pos,),
        in_specs=[pl.BlockSpec((2, n_pos, LANES), lambda b: (0, b, 0)),
                  const(cs.shape), const(ch.shape), const(w_fourier.shape)],
        out_specs=pl.BlockSpec((n_pos, gw), lambda b: (b, 0)),
        out_shape=jax.ShapeDtypeStruct((rows, gw), F32),
        compiler_params=_cparams("arbitrary"),
        name="fourier_ctx",
    )(a3, jnp.asarray(cs, BF16), jnp.asarray(ch, BF16), w_fourier)


ATTN_SUB = 128


def _attn_kernel(sink_ref, q_ref, qs_ref, k_ref, v_ref, kc_ref, vc_ref, o_ref, *, band, seq):
    qb = q_ref.shape[0]
    sub = ATTN_SUB
    lane = lax.broadcasted_iota(jnp.int32, (1, LANES), 1)
    lo_half = lane < HEAD_DIM
    zero = jnp.zeros((), BF16)
    scale = jnp.asarray(HEAD_DIM ** -0.5, BF16)
    kw = sub + 2 * WINDOW
    for sb in range(qb // sub):
        rows = slice(sb * sub, (sb + 1) * sub)
        qa0, qa1 = q_ref[rows, 0:LANES], q_ref[rows, LANES:]
        qs0, qs1 = qs_ref[rows, 0:LANES], qs_ref[rows, LANES:]
        q_all = jnp.concatenate([jnp.where(lo_half, qa0, zero), jnp.where(lo_half, qs0, zero),
                                 jnp.where(lo_half, zero, qs1), jnp.where(lo_half, zero, qa1)], axis=0) * scale
        if band:
            p0 = pl.program_id(1) * qb + sb * sub
            start = pl.multiple_of(jnp.clip(p0 - WINDOW, 0, seq - kw), WINDOW)
            qpos = p0 + lax.broadcasted_iota(jnp.int32, (sub, 1), 0)
            kpos = start + lax.broadcasted_iota(jnp.int32, (1, kw), 1)
            bias = jnp.where(jnp.abs(qpos - kpos) <= WINDOW, 0.0, NEG_INF)
            keys = jnp.concatenate([k_ref[pl.ds(start, kw), :], kc_ref[...]], axis=0)
            vals = jnp.concatenate([v_ref[pl.ds(start, kw), :], vc_ref[...]], axis=0)
        else:
            keys, vals = kc_ref[...], vc_ref[...]
        s_all = _dot_nt(q_all, keys)
        probs, dens = [], []
        for h in range(4):
            s = s_all[h * sub:(h + 1) * sub, :]
            sink = sink_ref[h]
            if band:
                s = jnp.concatenate([s[:, 0:kw] + bias, s[:, kw:]], axis=1)
            m = jnp.maximum(jnp.max(s, axis=1, keepdims=True), sink)
            p = jnp.exp(s - m)
            dens.append(jnp.sum(p, axis=1, keepdims=True) + jnp.exp(sink - m))
            probs.append(p.astype(BF16))
        o_all = _dot(jnp.concatenate(probs, axis=0), vals)
        o = [o_all[h * sub:(h + 1) * sub, :] / dens[h] for h in range(4)]
        o_ref[rows, 0:LANES] = jnp.where(lo_half, o[0], pltpu.roll(o[1], HEAD_DIM, axis=1)).astype(BF16)
        o_ref[rows, LANES:] = jnp.where(lo_half, pltpu.roll(o[2], HEAD_DIM, axis=1), o[3]).astype(BF16)


def _attention(sink, q, qs, k, v, kc, vc, *, seq, n_ctx, band):
    rows = q.shape[0]
    n_batch = rows // seq
    qb = Q_BLOCK if band else seq
    steps = seq // qb
    kern = functools.partial(_attn_kernel, band=band, seq=seq)
    seq_spec = pl.BlockSpec((seq, KV_W), lambda b, i: (b, 0))
    ctx_spec = pl.BlockSpec((n_ctx, KV_W), lambda b, i: (b, 0))
    q_spec = pl.BlockSpec((qb, GROUP_W), lambda b, i: (b * steps + i, 0))
    return pl.pallas_call(
        kern,
        grid=(n_batch, steps),
        in_specs=[pl.BlockSpec(memory_space=pltpu.SMEM), q_spec, q_spec,
                  seq_spec, seq_spec, ctx_spec, ctx_spec],
        out_specs=q_spec,
        out_shape=jax.ShapeDtypeStruct((rows, GROUP_W), BF16),
        compiler_params=_cparams("arbitrary", "arbitrary"),
        name="attn" if band else "attn_ctx",
    )(sink, q, qs, k, v, kc, vc)


POOL_HALO = max(POOL_WINDOWS) // 2


def _pool(p_ref, t0, tm, seq):
    halo = POOL_HALO
    pack = 2 * SUBLANES
    t0 = pl.multiple_of(t0, pack)
    main = p_ref[pl.ds(t0, tm), :].astype(F32)
    lo = pl.multiple_of(jnp.maximum(t0 - pack, 0), pack)
    hi = pl.multiple_of(jnp.minimum(t0 + tm, seq - pack), pack)
    prev = p_ref[pl.ds(lo, pack), :].astype(F32)[pack - halo:, :]
    nxt = p_ref[pl.ds(hi, pack), :].astype(F32)[:halo, :]
    prev = jnp.where(t0 > 0, prev, 0.0)
    nxt = jnp.where(t0 + tm < seq, nxt, 0.0)
    full = jnp.concatenate([prev, main, nxt], axis=0)
    n = tm + 2 * halo
    gch = GROUP_W // len(POOL_WINDOWS)
    first = lax.broadcasted_iota(jnp.int32, (1, LANES), 1) < gch
    means = []
    for hf in range(GROUP_W // LANES):
        wa, wb = POOL_WINDOWS[2 * hf], POOL_WINDOWS[2 * hf + 1]
        x = full[:, hf * LANES:(hf + 1) * LANES]
        sums, w, s = {}, 2, pltpu.roll(x, 1, axis=0) + x
        sums[w] = s
        while w < wb:
            s = pltpu.roll(s, w // 2, axis=0) + pltpu.roll(s, n - w // 2, axis=0)
            w *= 2
            sums[w] = s
        means.append(jnp.where(first, sums[wa] * (1.0 / wa), sums[wb] * (1.0 / wb))[halo:halo + tm, :])
    mean = jnp.concatenate(means, axis=1)
    win = jnp.concatenate([jnp.full((1, gch), w, jnp.int32) for w in POOL_WINDOWS], axis=1)

    def rescale(rows, first_pos):
        pos = first_pos + lax.broadcasted_iota(jnp.int32, (halo, 1), 0)
        cnt = jnp.minimum(pos + win // 2, seq) - jnp.maximum(pos - win // 2, 0)
        return rows * (win.astype(F32) / cnt.astype(F32))

    mean = jnp.concatenate([rescale(mean[:halo], t0), mean[halo:tm - halo],
                            rescale(mean[tm - halo:], t0 + tm - halo)], axis=0)
    return mean - main


def _route(logits):
    tm = logits.shape[0]
    lt = logits.T
    gl = lt[N_EXPERTS:N_EXPERTS + N_GROUPS]
    sub_g = lax.broadcasted_iota(jnp.int32, gl.shape, 0)
    gmax = jnp.max(gl, axis=0, keepdims=True)
    grp = jnp.min(jnp.where(gl == gmax, sub_g, N_GROUPS), axis=0, keepdims=True)
    gate_group = 1.0 / jnp.sum(jnp.exp(gl - gmax), axis=0, keepdims=True)
    el = lt[0:EXPERTS_PER_GROUP]
    for g in range(1, N_GROUPS):
        el = jnp.where(grp == g, lt[g * EXPERTS_PER_GROUP:(g + 1) * EXPERTS_PER_GROUP], el)
    sub = lax.broadcasted_iota(jnp.int32, el.shape, 0)
    m1 = jnp.max(el, axis=0, keepdims=True)
    i1 = jnp.min(jnp.where(el == m1, sub, EXPERTS_PER_GROUP), axis=0, keepdims=True)
    el2 = jnp.where(sub == i1, -jnp.inf, el)
    m2 = jnp.max(el2, axis=0, keepdims=True)
    i2 = jnp.min(jnp.where(el2 == m2, sub, EXPERTS_PER_GROUP), axis=0, keepdims=True)
    r = jnp.exp(m2 - m1)
    g1 = gate_group / (1.0 + r)
    g2 = g1 * r
    e1 = (grp * EXPERTS_PER_GROUP + i1).astype(F32)
    e2 = (grp * EXPERTS_PER_GROUP + i2).astype(F32)
    rows = jnp.where(sub == 0, e1, jnp.where(sub == 1, e2, jnp.where(sub == 2, g1, jnp.where(sub == 3, g2, 0.0))))
    cols = jnp.concatenate([rows, jnp.zeros((LANES - rows.shape[0], tm), F32)], axis=0).T
    return cols, rows


def _merge_kernel(x_ref, p_ref, ug_ref, yf_ref, ya_ref, mod_ref, wpool_ref, pscale_ref, wsgu_ref, bsgu_ref,
                  wout_ref, lng_ref, lnb_ref, wr_ref, *rest,
                  mod_row, seq, n_alias):
    x1_ref, h2_ref, route_ref, route_t_ref = rest[n_alias:]
    tm, d = x_ref.shape
    if mod_row is None:
        row = pl.program_id(0)
    else:
        row = mod_row
    t0 = pl.multiple_of(pl.program_id(1) * tm, tm)
    m = mod_ref[pl.ds(row, 1), :]
    gate1, shift2, scale2 = m[:, 2 * d:3 * d], m[:, 3 * d:4 * d], m[:, 4 * d:5 * d]
    lane = lax.broadcasted_iota(jnp.int32, (1, GROUP_W), 1)
    n_heads = wsgu_ref.shape[0] // SGU_CHUNK
    head = lane // (GROUP_W // n_heads)

    pm = min(tm, MERGE_PART)
    for part in range(tm // pm):
        r0 = part * pm
        rows = slice(r0, r0 + pm)
        pooled = _pool(p_ref, t0 + r0, pm, seq)
        y_pool = _dot(pooled.astype(BF16), wpool_ref[...]) * pscale_ref[...]

        ug = ug_ref[rows, :].astype(F32)
        u = _gelu(ug[:, 0:GROUP_W])
        v = _layer_norm(_gelu(ug[:, GROUP_W:])).astype(BF16)
        mixed = []
        for cidx in range(pm // SGU_CHUNK):
            vc = v[cidx * SGU_CHUNK:(cidx + 1) * SGU_CHUNK, :]
            full = _dot(wsgu_ref[...], vc)
            mc = bsgu_ref[...]
            for hd in range(n_heads):
                mc = mc + jnp.where(head == hd, full[hd * SGU_CHUNK:(hd + 1) * SGU_CHUNK, :], 0.0)
            mixed.append(mc)
        y_sgu = u * jnp.concatenate(mixed, axis=0)

        cat = jnp.concatenate([yf_ref[rows, :].astype(BF16), ya_ref[rows, :], y_pool.astype(BF16),
                               y_sgu.astype(BF16)], axis=1)
        y = _dot(cat, wout_ref[...])
        x1 = _layer_norm(RES_ALPHA * x_ref[rows, :] + gate1 * y) * lng_ref[...] + lnb_ref[...]
        x1_ref[rows, :] = x1
        h2 = _layer_norm(x1) * (1.0 + scale2) + shift2
        _store_rows(h2_ref, h2, first=r0)
        lg = _dot(h2.astype(BF16), wr_ref[...])
        route_ref[rows, :], route_t_ref[:, rows] = _route(lg[:, 0:LANES] + lg[:, LANES:])


def _merge(x2, p, ug, y_four, y_attn, mod, w_pool_bd, pool_scale, w_sgu_stack, b_sgu_exp, w_out,
           ln_g, ln_b, w_route, aliased, *, mod_row, seq, tm, row_off, total_rows):
    rows, d = x2.shape
    n_batch, steps = rows // seq, seq // tm
    off = row_off // tm
    kern = functools.partial(_merge_kernel, mod_row=mod_row, seq=seq, n_alias=len(aliased))
    row_spec = lambda w: pl.BlockSpec((tm, w), lambda b, i: (b * steps + i, 0))
    const = lambda a: pl.BlockSpec(a.shape, lambda b, i: (0,) * a.ndim)
    consts = (mod, w_pool_bd, pool_scale, w_sgu_stack, b_sgu_exp, w_out, ln_g, ln_b, w_route)
    n_in = 5 + len(consts)
    out_shapes = [jax.ShapeDtypeStruct((total_rows, d), F32),
                  jax.ShapeDtypeStruct((total_rows * ROW_WORDS, LANES), jnp.uint32),
                  jax.ShapeDtypeStruct((total_rows, LANES), F32),
                  jax.ShapeDtypeStruct((SUBLANES, total_rows), F32)]
    out_specs = [pl.BlockSpec((tm, d), lambda b, i: (off + b * steps + i, 0)),
                 pl.BlockSpec((tm * ROW_WORDS, LANES), lambda b, i: (off + b * steps + i, 0)),
                 pl.BlockSpec((tm, LANES), lambda b, i: (off + b * steps + i, 0)),
                 pl.BlockSpec((SUBLANES, tm), lambda b, i: (0, off + b * steps + i))]
    return pl.pallas_call(
        kern,
        grid=(n_batch, steps),
        in_specs=[row_spec(d), pl.BlockSpec((seq, GROUP_W), lambda b, i: (b, 0)),
                  row_spec(2 * GROUP_W), row_spec(GROUP_W), row_spec(GROUP_W)]
                 + [const(a) for a in consts]
                 + [pl.BlockSpec(memory_space=pl.ANY)] * len(aliased),
        out_specs=out_specs,
        out_shape=out_shapes,
        input_output_aliases={n_in + k: k for k in range(len(aliased))},
        compiler_params=_cparams("arbitrary", "arbitrary"),
        name="merge",
    )(x2, p, ug, y_four, y_attn, *consts, *aliased)


def _plan_kernel(route_ref, dest_ref, cnt_out_ref, cnt_ref, start_ref, carry_ref):
    ph, t = pl.program_id(0), pl.program_id(1)
    tm = route_ref.shape[1]
    rt = route_ref[...]
    e1 = rt[0:1, :].astype(jnp.int32)
    e2 = rt[1:2, :].astype(jnp.int32)
    sub = lax.broadcasted_iota(jnp.int32, (N_EXPERTS, tm), 0)
    hit1, hit2 = sub == e1, sub == e2
    onehot = jnp.where(hit1 | hit2, 1.0, 0.0)
    tile_cnt = jnp.sum(onehot, axis=1, keepdims=True)

    @pl.when((ph == 0) & (t == 0))
    def _():
        cnt_ref[...] = jnp.zeros_like(cnt_ref)

    @pl.when(ph == 0)
    def _():
        cnt_ref[...] += tile_cnt

    @pl.when((ph == 1) & (t == 0))
    def _():
        cnt = cnt_ref[...]
        padded = jnp.floor((cnt + (MOE_BLOCK - 1.0)) * (1.0 / MOE_BLOCK)) * MOE_BLOCK
        row = lax.broadcasted_iota(jnp.int32, cnt.shape, 0)
        incl = padded
        sh = 1
        while sh < N_EXPERTS:
            incl = incl + jnp.where(row >= sh, pltpu.roll(incl, sh, axis=0), 0.0)
            sh *= 2
        start_ref[...] = incl - padded
        carry_ref[...] = jnp.zeros_like(carry_ref)
        cnt_out_ref[...] = cnt

    @pl.when(ph == 1)
    def _():
        r_i = lax.broadcasted_iota(jnp.int32, (tm, tm), 0)
        c_i = lax.broadcasted_iota(jnp.int32, (tm, tm), 1)
        before = jnp.where(r_i < c_i, 1.0, 0.0).astype(BF16)
        rank = _dot(onehot.astype(BF16), before)
        base = start_ref[:, 0:1] + carry_ref[:, 0:1] + rank
        d1 = jnp.sum(jnp.where(hit1, base, 0.0), axis=0, keepdims=True)
        d2 = jnp.sum(jnp.where(hit2, base, 0.0), axis=0, keepdims=True)
        sub8 = lax.broadcasted_iota(jnp.int32, (SUBLANES, tm), 0)
        dest_ref[...] = jnp.where(sub8 == 0, d1, d2).astype(jnp.int32)
        carry_ref[...] += tile_cnt


def _plan(route_t, tm):
    rows = route_t.shape[1]
    n_t = rows // tm
    return pl.pallas_call(
        _plan_kernel,
        grid=(2, n_t),
        in_specs=[pl.BlockSpec((SUBLANES, tm), lambda ph, t: (0, t))],
        out_specs=[pl.BlockSpec((None, SUBLANES, tm), lambda ph, t: (t * ph, 0, 0)),
                   pl.BlockSpec((N_EXPERTS, LANES), lambda ph, t: (0, 0))],
        out_shape=[jax.ShapeDtypeStruct((n_t, SUBLANES, tm), jnp.int32),
                   jax.ShapeDtypeStruct((N_EXPERTS, LANES), F32)],
        scratch_shapes=[pltpu.VMEM((N_EXPERTS, LANES), F32)] * 3,
        compiler_params=_cparams("arbitrary", "arbitrary"),
        name="moe_plan",
    )(route_t)


def _row_copy(src_ref, src_row, dst_ref, dst_row, sem):
    return pltpu.make_async_copy(
        src_ref.at[pl.ds(pl.multiple_of(src_row * ROW_WORDS, ROW_WORDS), ROW_WORDS)],
        dst_ref.at[pl.ds(pl.multiple_of(dst_row * ROW_WORDS, ROW_WORDS), ROW_WORDS)], sem)


def _dispatch_kernel(dest_ref, h2_ref, xs_ref, sem, *, tm):
    def body(r4, c):
        for u in range(DMA_UNROLL):
            r = r4 * DMA_UNROLL + u
            for k in range(2):
                _row_copy(h2_ref, r, xs_ref, dest_ref[0, 0, k * tm + r], sem).start(priority=k)
        return c

    lax.fori_loop(0, tm // DMA_UNROLL, body, 0)
    for k in range(2):
        pltpu.make_async_copy(h2_ref, xs_ref.at[pl.ds(0, tm * ROW_WORDS)], sem).wait()


def _dispatch(dest, h2_tiles, n_slots, tm):
    n_t = dest.shape[0]
    return pl.pallas_call(
        functools.partial(_dispatch_kernel, tm=tm),
        grid=(n_t,),
        in_specs=[pl.BlockSpec((1, 1, 2 * tm), lambda i: (i, 0, 0), memory_space=pltpu.SMEM),
                  pl.BlockSpec((tm * ROW_WORDS, LANES), lambda i: (i, 0))],
        out_specs=pl.BlockSpec(memory_space=pl.ANY),
        out_shape=jax.ShapeDtypeStruct((n_slots * ROW_WORDS, LANES), jnp.uint32),
        scratch_shapes=[pltpu.SemaphoreType.DMA],
        compiler_params=_cparams("arbitrary"),
        name="moe_dispatch",
    )(dest, h2_tiles)


def _expert_kernel(be_ref, bn_ref, first_ref, slot_ref, nxt_ref, nxt2_ref, _blk_ref,
                   x_ref, wg_hbm, wu_hbm, wd_hbm, y_ref,
                   wg_buf, wu_buf, wd_buf, wg_bf, wu_bf, wd_bf, sem, *, layer):
    i = pl.program_id(0)
    mb = x_ref.shape[0] // ROW_WORDS
    half = wg_bf.shape[0] // 2

    def weight_copies(e, s):
        return [pltpu.make_async_copy(wg_hbm.at[layer, e], wg_buf.at[s], sem.at[s]),
                pltpu.make_async_copy(wu_hbm.at[layer, e], wu_buf.at[s], sem.at[s]),
                pltpu.make_async_copy(wd_hbm.at[layer, e], wd_buf.at[s], sem.at[s])]

    @pl.when(i == 0)
    def _():
        for cp in weight_copies(be_ref[0], 0):
            cp.start()

        @pl.when(nxt_ref[0] >= 0)
        def _():
            for cp in weight_copies(nxt_ref[0], 1):
                cp.start()

    @pl.when(first_ref[i] == 1)
    def _():
        s = slot_ref[i]
        for cp in weight_copies(be_ref[i], s):
            cp.wait()

        @pl.when(nxt2_ref[i] >= 0)
        def _():
            s2 = jnp.where(s == 0, WEIGHT_SLOTS - 1, s - 1)
            for cp in weight_copies(nxt2_ref[i], s2):
                cp.start()

        wg_bf[...] = wg_buf[s].astype(BF16)
        wu_bf[...] = wu_buf[s].astype(BF16)
        wd_bf[...] = wd_buf[s].astype(BF16)

    def compute(m):
        live = lax.broadcasted_iota(jnp.int32, (m, 1), 0) < bn_ref[i]
        x_lo, x_hi = _load_rows(x_ref, 0, m)
        x_lo = jnp.where(live, x_lo, 0.0).astype(BF16)
        x_hi = jnp.where(live, x_hi, 0.0).astype(BF16)
        g = _dot(x_lo, wg_bf[0:half, :]) + _dot(x_hi, wg_bf[half:, :])
        u = _dot(x_lo, wu_bf[0:half, :]) + _dot(x_hi, wu_bf[half:, :])
        hid = (_silu(g) * u).astype(BF16)
        _store_rows(y_ref, _dot(hid, wd_bf[...]))

    for parts in range(1, mb // MOE_PART + 1):
        @pl.when((bn_ref[i] > (parts - 1) * MOE_PART) & (bn_ref[i] <= parts * MOE_PART))
        def _(parts=parts):
            compute(parts * MOE_PART)


def _experts(table, xs, w_gate, w_up, w_down, layer):
    n_blocks = table[0].shape[0]
    _, _, d, de = w_gate.shape
    shape = (MOE_BLOCK * ROW_WORDS, LANES)
    hbm = pl.BlockSpec(memory_space=pl.ANY)
    return pl.pallas_call(
        functools.partial(_expert_kernel, layer=layer),
        grid_spec=pltpu.PrefetchScalarGridSpec(
            num_scalar_prefetch=len(table),
            grid=(n_blocks,),
            in_specs=[pl.BlockSpec(shape, lambda i, *t: (t[-1][i], 0)), hbm, hbm, hbm],
            out_specs=pl.BlockSpec(shape, lambda i, *t: (t[-1][i], 0)),
            scratch_shapes=[pltpu.VMEM((WEIGHT_SLOTS, d, de), F32), pltpu.VMEM((WEIGHT_SLOTS, d, de), F32),
                            pltpu.VMEM((WEIGHT_SLOTS, de, d), F32),
                            pltpu.VMEM((d, de), BF16), pltpu.VMEM((d, de), BF16), pltpu.VMEM((de, d), BF16),
                            pltpu.SemaphoreType.DMA((WEIGHT_SLOTS,))]),
        out_shape=jax.ShapeDtypeStruct(xs.shape, jnp.uint32),
        compiler_params=_cparams("arbitrary"),
        name="moe_experts",
    )(*table, xs, w_gate, w_up, w_down)


def _combine_body(dest_ref, dest_next_ref, x1_ref, route_ref, m, lng_ref, lnb_ref, y_ref, o_ref, buf_ref, sem):
    tm, d = x1_ref.shape
    i = pl.program_id(0)
    n = pl.num_programs(0)
    slot = i % 2

    def gather(idx_ref, s):
        def body(r4, c):
            for u in range(DMA_UNROLL):
                r = r4 * DMA_UNROLL + u
                for k in range(2):
                    _row_copy(y_ref, idx_ref[0, 0, k * tm + r], buf_ref.at[s], k * tm + r,
                              sem.at[s]).start(priority=k)
            return c

        lax.fori_loop(0, tm // DMA_UNROLL, body, 0)

    @pl.when(i == 0)
    def _():
        gather(dest_ref, 0)

    @pl.when(i + 1 < n)
    def _():
        gather(dest_next_ref, 1 - slot)

    for k in range(2):
        pltpu.make_async_copy(y_ref.at[pl.ds(0, tm * ROW_WORDS)],
                              buf_ref.at[slot, pl.ds(0, tm * ROW_WORDS)], sem.at[slot]).wait()

    gate2 = m[:, 5 * d:6 * d]
    rt = route_ref[...]
    f = jnp.zeros((tm, d), F32)
    for k in range(2):
        lo, hi = _load_rows(buf_ref.at[slot], k * tm, tm)
        f = f + jnp.concatenate([lo, hi], axis=1) * rt[:, 2 + k:3 + k]
    o_ref[...] = _layer_norm(RES_ALPHA * x1_ref[...] + gate2 * f) * lng_ref[...] + lnb_ref[...]


def _combine_kernel(dest_ref, dest_next_ref, x1_ref, route_ref, mod_ref, lng_ref, lnb_ref, y_ref, o_ref,
                    buf_ref, sem, *, mod_row, rows_per_batch):
    m = _mod_row(mod_ref, mod_row, x1_ref.shape[0], rows_per_batch)
    _combine_body(dest_ref, dest_next_ref, x1_ref, route_ref, m, lng_ref, lnb_ref, y_ref, o_ref, buf_ref, sem)


def _combine_specs(tm, d, off, steps):
    return [pl.BlockSpec((1, 1, 2 * tm), lambda i: (off + i, 0, 0), memory_space=pltpu.SMEM),
            pl.BlockSpec((1, 1, 2 * tm), lambda i: (off + jnp.minimum(i + 1, steps - 1), 0, 0),
                         memory_space=pltpu.SMEM),
            pl.BlockSpec((tm, d), lambda i: (off + i, 0)),
            pl.BlockSpec((tm, LANES), lambda i: (off + i, 0))]


def _combine_scratch(tm):
    return [pltpu.VMEM((2, 2 * tm * ROW_WORDS, LANES), jnp.uint32), pltpu.SemaphoreType.DMA((2,))]


def _combine(dest, x1, route, mod, ln_g, ln_b, y_tiles, *, tm, row_off, rows, mod_row, rows_per_batch):
    d = x1.shape[1]
    steps = rows // tm
    kern = functools.partial(_combine_kernel, mod_row=mod_row, rows_per_batch=rows_per_batch)
    const = lambda a: pl.BlockSpec(a.shape, lambda i: (0,) * a.ndim)
    return pl.pallas_call(
        kern,
        grid=(steps,),
        in_specs=_combine_specs(tm, d, row_off // tm, steps)
                 + [const(mod), const(ln_g), const(ln_b), pl.BlockSpec(memory_space=pl.ANY)],
        out_specs=pl.BlockSpec((tm, d), lambda i: (i, 0)),
        out_shape=jax.ShapeDtypeStruct((rows, d), F32),
        scratch_shapes=_combine_scratch(tm),
        compiler_params=_cparams("arbitrary"),
        name="moe_combine",
    )(dest, dest, x1, route, mod, ln_g, ln_b, y_tiles)


def _rope_tables(n_pos):
    rows = n_pos // GRID_W
    row = jnp.repeat(jnp.arange(rows), GRID_W).astype(F32)
    col = jnp.tile(jnp.arange(GRID_W), rows).astype(F32)
    n_freq = HEAD_DIM // 4
    freq = ROPE_BASE ** (-jnp.arange(n_freq, dtype=F32) / n_freq)
    ang_r, ang_c = row[:, None] * freq, col[:, None] * freq
    cos_h = jnp.concatenate([jnp.cos(ang_r)] * 2 + [jnp.cos(ang_c)] * 2, axis=1)
    sin_h = jnp.concatenate([-jnp.sin(ang_r), jnp.sin(ang_r), -jnp.sin(ang_c), jnp.sin(ang_c)], axis=1)
    return jnp.tile(cos_h, (1, 2)), jnp.tile(sin_h, (1, 2))


def _block_table(counts, n_blocks):
    cnt = counts.astype(jnp.int32)
    padded = (cnt + MOE_BLOCK - 1) // MOE_BLOCK * MOE_BLOCK
    pad_end = jnp.cumsum(padded)
    pad_start = pad_end - padded
    blk_start = jnp.arange(n_blocks, dtype=jnp.int32)[:, None] * MOE_BLOCK
    be = jnp.minimum(jnp.sum((pad_end[None, :] <= blk_start).astype(jnp.int32), axis=1), N_EXPERTS - 1)
    ids = jnp.arange(N_EXPERTS, dtype=jnp.int32)
    mine = be[:, None] == ids[None, :]
    fill = jnp.sum(jnp.where(mine, cnt[None, :] + pad_start[None, :], 0), axis=1) - blk_start[:, 0]
    bn = jnp.clip(fill, 0, MOE_BLOCK)
    prev = jnp.concatenate([jnp.full((1,), -1, jnp.int32), be[:-1]])
    first = ((bn > 0) & (be != prev)).astype(jnp.int32)
    slot = (jnp.cumsum(first) - 1) % WEIGHT_SLOTS
    later = (ids[None, :] > ids[:, None]) & (cnt[None, :] > 0)
    nxt_e = jnp.min(jnp.where(later, ids[None, :], N_EXPERTS), axis=1)
    hop = nxt_e[:, None] == ids[None, :]
    nxt2_e = jnp.sum(jnp.where(hop, nxt_e[None, :], 0), axis=1) + jnp.where(nxt_e == N_EXPERTS, N_EXPERTS, 0)
    lookup = lambda tab: jnp.sum(jnp.where(mine, jnp.where(tab >= N_EXPERTS, -1, tab)[None, :], 0), axis=1)
    blk_idx = jnp.minimum(blk_start[:, 0] // MOE_BLOCK, jnp.maximum(jnp.sum((bn > 0).astype(jnp.int32)) - 1, 0))
    return (be, bn, first, slot.astype(jnp.int32), lookup(nxt_e).astype(jnp.int32),
            lookup(nxt2_e).astype(jnp.int32), blk_idx.astype(jnp.int32))


def _moe(route_t, h2_tiles, w_gate, w_up, w_down, layer):
    rows = route_t.shape[1]
    tm = ROW_TILE
    n_blocks = -(-(2 * rows) // MOE_BLOCK) + N_EXPERTS
    dest8, counts = _plan(route_t, tm)
    dest = dest8[:, 0:2, :].reshape(rows // tm, 1, 2 * tm)
    table = _block_table(counts[:, 0], n_blocks)
    xs = _dispatch(dest, h2_tiles, n_blocks * MOE_BLOCK, tm)
    ys = _experts(table, xs, w_gate, w_up, w_down, layer)
    tc = COMBINE_TILE
    dest_c = dest8[:, 0:2, :].reshape(rows // tm, 2, tm // tc, tc).transpose(0, 2, 1, 3).reshape(rows // tc, 1, 2 * tc)
    return dest_c, ys


def kernel(x, c, ctx, c_ctx, w_ada, b_ada, w_in, w_fourier, attn_sink, w_pool, pool_scale, w_sgu, b_sgu,
           w_out, ln1_g, ln1_b, w_router_group, w_router_expert, w_exp_gate, w_exp_up, w_exp_down,
           ln2_g, ln2_b):
    b, s, d = x.shape
    n_ctx = ctx.shape[1]
    n_layers = w_in.shape[0]
    tm = ROW_TILE
    cond = jnp.concatenate([c, c_ctx[None, :], jnp.zeros((SUBLANES - b - 1, d), F32)], axis=0)
    mod_all = _ada(cond, w_ada, b_ada[:, None, :])
    cos_t, sin_t = _rope_tables(s)
    x2 = x.reshape(b * s, d)
    c2 = ctx.reshape(b * n_ctx, d)
    n_sgu = w_sgu.shape[1]
    for layer in range(n_layers):
        last = layer == n_layers - 1
        mod = mod_all[layer]
        w_in_l = w_in[layer].astype(BF16)
        wf = w_fourier[layer].astype(BF16)
        w_pool_bd = jax.scipy.linalg.block_diag(*[w_pool[layer, g] for g in range(w_pool.shape[1])]).astype(BF16)
        w_sgu_stack = w_sgu[layer].reshape(n_sgu * SGU_CHUNK, SGU_CHUNK).astype(BF16)
        b_sgu_exp = jnp.repeat(b_sgu[layer].T, GROUP_W // n_sgu, axis=1)
        w_router = jnp.concatenate([w_router_expert[layer].reshape(d, N_EXPERTS), w_router_group[layer]], axis=1)
        w_router = jnp.pad(w_router, ((0, 0), (0, LANES - w_router.shape[1])))
        wr_hi = w_router.astype(BF16)
        w_route = jnp.concatenate([wr_hi, (w_router - wr_hi.astype(F32)).astype(BF16)], axis=1)
        merge_consts = (mod, w_pool_bd, pool_scale[layer][None, :], w_sgu_stack, b_sgu_exp,
                        w_out[layer].astype(BF16), ln1_g[layer][None, :], ln1_b[layer][None, :], w_route)
        sink = attn_sink[layer]

        a, q, qs, k, v, p, ug = _proj(x2, mod, w_in_l, cos_t, sin_t, mod_row=None, rows_per_batch=s,
                                      rope=True, tm=tm, a_pitch=FFT_PITCH)
        ac, qc, qsc, kc, vc, pc, ugc = _proj(c2, mod, w_in_l, cos_t, sin_t, mod_row=b, rows_per_batch=n_ctx,
                                             rope=False, tm=n_ctx, a_pitch=FFT_R)
        y_four = _fourier(a, wf, s)
        y_attn = _attention(sink, q, qs, k, v, kc, vc, seq=s, n_ctx=n_ctx, band=True)
        total = b * s + (0 if last else b * n_ctx)
        merged = _merge(x2, p, ug, y_four, y_attn, *merge_consts, (),
                        mod_row=None, seq=s, tm=tm, row_off=0, total_rows=total)
        if not last:
            yc_four = _fourier_small(ac, wf, n_ctx)
            yc_attn = _attention(sink, qc, qsc, kc, vc, kc, vc, seq=n_ctx, n_ctx=n_ctx, band=False)
            merged = _merge(c2, pc, ugc, yc_four, yc_attn, *merge_consts, tuple(merged),
                            mod_row=b, seq=n_ctx, tm=n_ctx, row_off=b * s, total_rows=total)
        x1, h2_tiles, route, route_t = merged
        dest, ys = _moe(route_t, h2_tiles, w_exp_gate, w_exp_up, w_exp_down, layer)
        ln_g, ln_b = ln2_g[layer][None, :], ln2_b[layer][None, :]
        x2 = _combine(dest, x1, route, mod, ln_g, ln_b, ys, tm=COMBINE_TILE, row_off=0, rows=b * s,
                      mod_row=None, rows_per_batch=s)
        if not last:
            c2 = _combine(dest, x1, route, mod, ln_g, ln_b, ys, tm=COMBINE_TILE, row_off=b * s,
                          rows=b * n_ctx, mod_row=b, rows_per_batch=n_ctx)
    return x2.reshape(b, s, d)
```

```python
import functools
import math

import numpy as np
import jax
import jax.numpy as jnp
from jax import lax
from jax.experimental import pallas as pl
from jax.experimental.pallas import tpu as pltpu

GRID_W = 64
HEAD_DIM = 64
GROUP_W = 256
KV_W = 128
WINDOW = 128
POOL_WINDOWS = (2, 4, 8, 16)
SGU_CHUNK = 128
N_GROUPS = 4
EXPERTS_PER_GROUP = 8
N_EXPERTS = 32
ROPE_BASE = 10000.0
LN_EPS = 1e-6
NEG_INF = -1e30
DEPTH = 2
RES_ALPHA = (2 * DEPTH) ** 0.25

LANES = 128
SUBLANES = 8
VMEM_LIMIT = 48 * 1024 * 1024

ROW_TILE = 1024
Q_BLOCK = 1024
COMBINE_TILE = 512
WEIGHT_SLOTS = 3
MOE_BLOCK = 512
MOE_PART = 256
FFT_R = 64
FFT_PITCH = 72
FFT_UNROLL = 16
DMA_UNROLL = 8
MERGE_PART = 512

BF16 = jnp.bfloat16
F32 = jnp.float32


def _cparams(*sem):
    return pltpu.CompilerParams(dimension_semantics=sem, vmem_limit_bytes=VMEM_LIMIT)


def _dot(a, b):
    return jnp.dot(a, b, preferred_element_type=F32)


def _dot_nt(a, b):
    return lax.dot_general(a, b, (((1,), (1,)), ((), ())), preferred_element_type=F32)


def _layer_norm(t):
    mu = jnp.mean(t, axis=-1, keepdims=True)
    d = t - mu
    var = jnp.mean(d * d, axis=-1, keepdims=True)
    return d * lax.rsqrt(var + LN_EPS)


def _silu(t):
    return t * (1.0 / (1.0 + jnp.exp(-t)))


def _gelu(t):
    return 0.5 * t * (1.0 + lax.erf(t * (1.0 / math.sqrt(2.0))))


ROW_WORDS = 4
HI_MASK = 0xFFFF0000


def _pack_rows(t):
    half = t.shape[1] // 2
    lo = lax.bitcast_convert_type(t[:, :half].astype(BF16).astype(F32), jnp.uint32)
    hi = lax.bitcast_convert_type(t[:, half:].astype(BF16).astype(F32), jnp.uint32)
    return (lo >> 16) | (hi & jnp.uint32(HI_MASK))


def _unpack_rows(w):
    return (lax.bitcast_convert_type(w << 16, F32),
            lax.bitcast_convert_type(w & jnp.uint32(HI_MASK), F32))


def _store_rows(ref, t, first=0):
    w = _pack_rows(t)
    for j in range(ROW_WORDS):
        ref[pl.ds(first * ROW_WORDS + j, t.shape[0], stride=ROW_WORDS), :] = w[:, j * LANES:(j + 1) * LANES]


def _load_rows(ref, first, m):
    w = jnp.concatenate([ref[pl.ds(first * ROW_WORDS + j, m, stride=ROW_WORDS), :] for j in range(ROW_WORDS)],
                        axis=1)
    return _unpack_rows(w)


def _ada_kernel(c_ref, w_ref, b_ref, o_ref):
    s = _silu(c_ref[...]).astype(BF16)
    o_ref[...] = _dot(s, w_ref[...].astype(BF16)) + b_ref[...]


def _ada(cond, w_ada, b_ada):
    n_layers, d, n = w_ada.shape
    tn = n // 4
    return pl.pallas_call(
        _ada_kernel,
        grid=(n_layers, n // tn),
        in_specs=[
            pl.BlockSpec((SUBLANES, d), lambda l, j: (0, 0)),
            pl.BlockSpec((None, d, tn), lambda l, j: (l, 0, j)),
            pl.BlockSpec((None, 1, tn), lambda l, j: (l, 0, j)),
        ],
        out_specs=pl.BlockSpec((None, SUBLANES, tn), lambda l, j: (l, 0, j)),
        out_shape=jax.ShapeDtypeStruct((n_layers, SUBLANES, n), F32),
        compiler_params=_cparams("arbitrary", "arbitrary"),
        name="ada",
    )(cond, w_ada, b_ada)


def _rope(t, cos_t, sin_t):
    lane = lax.broadcasted_iota(jnp.int32, t.shape, 1)
    first = (lane % 32) < 16
    partner = jnp.where(first, pltpu.roll(t, LANES - 16, axis=1), pltpu.roll(t, 16, axis=1))
    return t * cos_t + partner * sin_t


def _proj_body(x, m, w_ref, cos_ref, sin_ref, outs, *, rope, a_pitch):
    a_ref, q_ref, qs_ref, k_ref, v_ref, p_ref, ug_ref = outs
    tm, d = x.shape
    shift, scale = m[:, 0:d], m[:, d:2 * d]
    h = _layer_norm(x) * (1.0 + scale) + shift
    z = _dot(h.astype(BF16), w_ref[...])
    pad = jnp.zeros((a_pitch - FFT_R, LANES), F32)
    for g in range(tm // FFT_R):
        for hf in range(2):
            grp = z[g * FFT_R:(g + 1) * FFT_R, hf * LANES:(hf + 1) * LANES]
            if a_pitch > FFT_R:
                grp = jnp.concatenate([grp, pad], axis=0)
            a_ref[hf, g * a_pitch:(g + 1) * a_pitch, :] = grp
    q0, q1 = z[:, 256:384], z[:, 384:512]
    k = z[:, 512:640]
    if rope:
        cos_t, sin_t = cos_ref[...], sin_ref[...]
        q0, q1, k = _rope(q0, cos_t, sin_t), _rope(q1, cos_t, sin_t), _rope(k, cos_t, sin_t)
    q_ref[:, 0:128] = q0.astype(BF16)
    q_ref[:, 128:256] = q1.astype(BF16)
    qs_ref[:, 0:128] = pltpu.roll(q0, HEAD_DIM, axis=1).astype(BF16)
    qs_ref[:, 128:256] = pltpu.roll(q1, HEAD_DIM, axis=1).astype(BF16)
    k_ref[...] = k.astype(BF16)
    v_ref[...] = z[:, 640:768].astype(BF16)
    p_ref[...] = z[:, 768:1024].astype(BF16)
    ug_ref[...] = z[:, 1024:1536].astype(BF16)


def _mod_row(mod_ref, mod_row, tm, rows_per_batch):
    row = (pl.program_id(0) * tm) // rows_per_batch if mod_row is None else mod_row
    return mod_ref[pl.ds(row, 1), :]


def _proj_kernel(x_ref, mod_ref, w_ref, cos_ref, sin_ref, *outs, mod_row, rows_per_batch, rope, a_pitch):
    m = _mod_row(mod_ref, mod_row, x_ref.shape[0], rows_per_batch)
    _proj_body(x_ref[...], m, w_ref, cos_ref, sin_ref, outs, rope=rope, a_pitch=a_pitch)


def _proj_specs(rows, tm, a_pitch, seq_steps):
    row_spec = lambda w: pl.BlockSpec((tm, w), lambda i: (i, 0))
    out_w = (256, 256, 128, 128, 256, 512)
    ta = tm // FFT_R * a_pitch
    out_specs = [pl.BlockSpec((2, ta, LANES), lambda i: (0, i, 0))] + [row_spec(w) for w in out_w]
    out_shape = ([jax.ShapeDtypeStruct((2, rows // FFT_R * a_pitch, LANES), F32)]
                 + [jax.ShapeDtypeStruct((rows, w), BF16) for w in out_w])
    table_spec = pl.BlockSpec((tm, LANES), lambda i: (i % seq_steps, 0))
    return table_spec, out_specs, out_shape


def _proj(x2, mod, w_in, cos_t, sin_t, *, mod_row, rows_per_batch, rope, tm, a_pitch):
    rows, d = x2.shape
    kern = functools.partial(_proj_kernel, mod_row=mod_row, rows_per_batch=rows_per_batch,
                             rope=rope, a_pitch=a_pitch)
    table_spec, out_specs, out_shape = _proj_specs(rows, tm, a_pitch, cos_t.shape[0] // tm)
    return pl.pallas_call(
        kern,
        grid=(rows // tm,),
        in_specs=[
            pl.BlockSpec((tm, d), lambda i: (i, 0)),
            pl.BlockSpec(mod.shape, lambda i: (0, 0)),
            pl.BlockSpec(w_in.shape, lambda i: (0, 0)),
            table_spec, table_spec,
        ],
        out_specs=out_specs,
        out_shape=out_shape,
        compiler_params=_cparams("arbitrary"),
        name="proj",
    )(x2, mod, w_in, cos_t, sin_t)


def _fft_tables(n_pos):
    r = FFT_R
    assert n_pos == r * r
    kb = np.arange(r)[None, :, None]
    na = np.arange(r)[:, None, None]
    nb = np.arange(r)[None, None, :]
    ang = 2.0 * np.pi * ((kb * (na + r * nb)) % n_pos) / n_pos
    m1 = np.concatenate([np.cos(ang), -np.sin(ang)], axis=1)
    ka = np.arange(r)[:, None]
    n2 = np.arange(r)[None, :]
    ang2 = 2.0 * np.pi * ((ka * n2) % r) / r
    c2, s2 = np.cos(ang2), np.sin(ang2)
    w2 = np.block([[c2, s2], [-s2, c2]])
    return m1, w2


def _channel_tables(n_pos):
    h = HEAD_DIM
    c = np.arange(h)
    ang = 2.0 * np.pi * ((c[:, None] * c[None, :]) % h) / h
    scale = 1.0 / math.sqrt(n_pos * h)
    eye = np.eye(GROUP_W // h)
    cc = np.kron(eye, np.cos(ang)) * scale
    ss = np.kron(eye, np.sin(ang)) * scale
    return np.concatenate([cc, ss], axis=0)


def _fourier_kernel(a_ref, m1_ref, w2_ref, ch_ref, wf_ref, o_ref, z_ref, y_ref):
    r, pt = FFT_R, FFT_PITCH

    def step1(i, c):
        for u in range(FFT_UNROLL):
            na = i * FFT_UNROLL + u
            rows = jnp.concatenate([a_ref[0, pl.ds(na, r, stride=pt), :],
                                    a_ref[1, pl.ds(na, r, stride=pt), :]], axis=1)
            z = _dot(m1_ref[na], rows.astype(BF16))
            base = pl.multiple_of(na * pt, SUBLANES)
            z_ref[0, pl.ds(base, r), :] = z[0:r, 0:LANES]
            z_ref[1, pl.ds(base, r), :] = z[0:r, LANES:]
            z_ref[2, pl.ds(base, r), :] = z[r:, 0:LANES]
            z_ref[3, pl.ds(base, r), :] = z[r:, LANES:]
        return c

    lax.fori_loop(0, r // FFT_UNROLL, step1, 0)

    def step2(i, c):
        for u in range(FFT_UNROLL):
            kb = i * FFT_UNROLL + u
            q = [z_ref[j, pl.ds(kb, r, stride=pt), :] for j in range(4)]
            zs = jnp.concatenate([jnp.concatenate(q[0:2], axis=1),
                                  jnp.concatenate(q[2:4], axis=1)], axis=0)
            y = _dot(w2_ref[...], zs.astype(BF16))
            base = pl.multiple_of(kb * r, r)
            y_ref[0, pl.ds(base, r), :] = y[0:r, 0:LANES]
            y_ref[1, pl.ds(base, r), :] = y[0:r, LANES:]
            y_ref[2, pl.ds(base, r), :] = y[r:, 0:LANES]
            y_ref[3, pl.ds(base, r), :] = y[r:, LANES:]
        return c

    lax.fori_loop(0, r // FFT_UNROLL, step2, 0)

    chunk = 8 * r
    for cidx in range(r * r // chunk):
        yy = jnp.concatenate([y_ref[j, cidx * chunk:(cidx + 1) * chunk, :] for j in range(4)], axis=1)
        f = _dot(yy.astype(BF16), ch_ref[...])
        g = _dot(f.astype(BF16), wf_ref[...])
        for gi in range(chunk // r):
            kb = cidx * (chunk // r) + gi
            z_ref[0, kb * pt:kb * pt + r, :] = g[gi * r:(gi + 1) * r, 0:LANES]
            z_ref[1, kb * pt:kb * pt + r, :] = g[gi * r:(gi + 1) * r, LANES:]

    def step3(i, c):
        for u in range(FFT_UNROLL):
            ka = i * FFT_UNROLL + u
            base = pl.multiple_of(ka * r, r)
            o_ref[pl.ds(base, r), 0:LANES] = z_ref[0, pl.ds(ka, r, stride=pt), :]
            o_ref[pl.ds(base, r), LANES:] = z_ref[1, pl.ds(ka, r, stride=pt), :]
        return c

    lax.fori_loop(0, r // FFT_UNROLL, step3, 0)


def _fourier(a3, w_fourier, n_pos):
    rows = a3.shape[1] // FFT_PITCH * FFT_R
    gw = GROUP_W
    m1, w2 = _fft_tables(n_pos)
    ch = _channel_tables(n_pos)
    const = lambda shape: pl.BlockSpec(shape, lambda b: (0,) * len(shape))
    return pl.pallas_call(
        _fourier_kernel,
        grid=(rows // n_pos,),
        in_specs=[
            pl.BlockSpec((2, FFT_R * FFT_PITCH, LANES), lambda b: (0, b, 0)),
            const(m1.shape), const(w2.shape), const(ch.shape), const(w_fourier.shape),
        ],
        out_specs=pl.BlockSpec((n_pos, gw), lambda b: (b, 0)),
        out_shape=jax.ShapeDtypeStruct((rows, gw), F32),
        scratch_shapes=[pltpu.VMEM((4, FFT_R * FFT_PITCH, LANES), F32), pltpu.VMEM((4, n_pos, LANES), F32)],
        compiler_params=_cparams("arbitrary"),
        name="fourier",
    )(a3, jnp.asarray(m1, BF16), jnp.asarray(w2, BF16), jnp.asarray(ch, BF16), w_fourier)


def _fourier_small_kernel(a_ref, cs_ref, ch_ref, wf_ref, o_ref):
    n = a_ref.shape[1]
    a = jnp.concatenate([a_ref[0], a_ref[1]], axis=1)
    pq = _dot(cs_ref[...], a.astype(BF16))
    y = jnp.concatenate([pq[0:n], pq[n:2 * n]], axis=1).astype(BF16)
    f = _dot(y, ch_ref[...])
    o_ref[...] = _dot(f.astype(BF16), wf_ref[...])


def _fourier_small(a3, w_fourier, n_pos):
    _, rows, _ = a3.shape
    gw = GROUP_W
    k = np.arange(n_pos)
    ang = 2.0 * np.pi * ((k[:, None] * k[None, :]) % n_pos) / n_pos
    cs = np.concatenate([np.cos(ang), -np.sin(ang)], axis=0)
    ch = _channel_tables(n_pos)
    const = lambda shape: pl.BlockSpec(shape, lambda b: (0,) * len(shape))
    return pl.pallas_call(
        _fourier_small_kernel,
        grid=(rows // n_pos,),
        in_specs=[pl.BlockSpec((2, n_pos, LANES), lambda b: (0, b, 0)),
                  const(cs.shape), const(ch.shape), const(w_fourier.shape)],
        out_specs=pl.BlockSpec((n_pos, gw), lambda b: (b, 0)),
        out_shape=jax.ShapeDtypeStruct((rows, gw), F32),
        compiler_params=_cparams("arbitrary"),
        name="fourier_ctx",
    )(a3, jnp.asarray(cs, BF16), jnp.asarray(ch, BF16), w_fourier)


ATTN_SUB = 128


def _attn_kernel(sink_ref, q_ref, qs_ref, k_ref, v_ref, kc_ref, vc_ref, o_ref, *, band, seq):
    qb = q_ref.shape[0]
    sub = ATTN_SUB
    lane = lax.broadcasted_iota(jnp.int32, (1, LANES), 1)
    lo_half = lane < HEAD_DIM
    zero = jnp.zeros((), BF16)
    scale = jnp.asarray(HEAD_DIM ** -0.5, BF16)
    kw = sub + 2 * WINDOW
    for sb in range(qb // sub):
        rows = slice(sb * sub, (sb + 1) * sub)
        qa0, qa1 = q_ref[rows, 0:LANES], q_ref[rows, LANES:]
        qs0, qs1 = qs_ref[rows, 0:LANES], qs_ref[rows, LANES:]
        q_all = jnp.concatenate([jnp.where(lo_half, qa0, zero), jnp.where(lo_half, qs0, zero),
                                 jnp.where(lo_half, zero, qs1), jnp.where(lo_half, zero, qa1)], axis=0) * scale
        if band:
            p0 = pl.program_id(1) * qb + sb * sub
            start = pl.multiple_of(jnp.clip(p0 - WINDOW, 0, seq - kw), WINDOW)
            qpos = p0 + lax.broadcasted_iota(jnp.int32, (sub, 1), 0)
            kpos = start + lax.broadcasted_iota(jnp.int32, (1, kw), 1)
            bias = jnp.where(jnp.abs(qpos - kpos) <= WINDOW, 0.0, NEG_INF)
            keys = jnp.concatenate([k_ref[pl.ds(start, kw), :], kc_ref[...]], axis=0)
            vals = jnp.concatenate([v_ref[pl.ds(start, kw), :], vc_ref[...]], axis=0)
        else:
            keys, vals = kc_ref[...], vc_ref[...]
        s_all = _dot_nt(q_all, keys)
        probs, dens = [], []
        for h in range(4):
            s = s_all[h * sub:(h + 1) * sub, :]
            sink = sink_ref[h]
            if band:
                s = jnp.concatenate([s[:, 0:kw] + bias, s[:, kw:]], axis=1)
            m = jnp.maximum(jnp.max(s, axis=1, keepdims=True), sink)
            p = jnp.exp(s - m)
            dens.append(jnp.sum(p, axis=1, keepdims=True) + jnp.exp(sink - m))
            probs.append(p.astype(BF16))
        o_all = _dot(jnp.concatenate(probs, axis=0), vals)
        o = [o_all[h * sub:(h + 1) * sub, :] / dens[h] for h in range(4)]
        o_ref[rows, 0:LANES] = jnp.where(lo_half, o[0], pltpu.roll(o[1], HEAD_DIM, axis=1)).astype(BF16)
        o_ref[rows, LANES:] = jnp.where(lo_half, pltpu.roll(o[2], HEAD_DIM, axis=1), o[3]).astype(BF16)


def _attention(sink, q, qs, k, v, kc, vc, *, seq, n_ctx, band):
    rows = q.shape[0]
    n_batch = rows // seq
    qb = Q_BLOCK if band else seq
    steps = seq // qb
    kern = functools.partial(_attn_kernel, band=band, seq=seq)
    seq_spec = pl.BlockSpec((seq, KV_W), lambda b, i: (b, 0))
    ctx_spec = pl.BlockSpec((n_ctx, KV_W), lambda b, i: (b, 0))
    q_spec = pl.BlockSpec((qb, GROUP_W), lambda b, i: (b * steps + i, 0))
    return pl.pallas_call(
        kern,
        grid=(n_batch, steps),
        in_specs=[pl.BlockSpec(memory_space=pltpu.SMEM), q_spec, q_spec,
                  seq_spec, seq_spec, ctx_spec, ctx_spec],
        out_specs=q_spec,
        out_shape=jax.ShapeDtypeStruct((rows, GROUP_W), BF16),
        compiler_params=_cparams("arbitrary", "arbitrary"),
        name="attn" if band else "attn_ctx",
    )(sink, q, qs, k, v, kc, vc)


POOL_HALO = max(POOL_WINDOWS) // 2


def _pool(p_ref, t0, tm, seq):
    halo = POOL_HALO
    pack = 2 * SUBLANES
    t0 = pl.multiple_of(t0, pack)
    main = p_ref[pl.ds(t0, tm), :].astype(F32)
    lo = pl.multiple_of(jnp.maximum(t0 - pack, 0), pack)
    hi = pl.multiple_of(jnp.minimum(t0 + tm, seq - pack), pack)
    prev = p_ref[pl.ds(lo, pack), :].astype(F32)[pack - halo:, :]
    nxt = p_ref[pl.ds(hi, pack), :].astype(F32)[:halo, :]
    prev = jnp.where(t0 > 0, prev, 0.0)
    nxt = jnp.where(t0 + tm < seq, nxt, 0.0)
    full = jnp.concatenate([prev, main, nxt], axis=0)
    n = tm + 2 * halo
    gch = GROUP_W // len(POOL_WINDOWS)
    first = lax.broadcasted_iota(jnp.int32, (1, LANES), 1) < gch
    means = []
    for hf in range(GROUP_W // LANES):
        wa, wb = POOL_WINDOWS[2 * hf], POOL_WINDOWS[2 * hf + 1]
        x = full[:, hf * LANES:(hf + 1) * LANES]
        sums, w, s = {}, 2, pltpu.roll(x, 1, axis=0) + x
        sums[w] = s
        while w < wb:
            s = pltpu.roll(s, w // 2, axis=0) + pltpu.roll(s, n - w // 2, axis=0)
            w *= 2
            sums[w] = s
        means.append(jnp.where(first, sums[wa] * (1.0 / wa), sums[wb] * (1.0 / wb))[halo:halo + tm, :])
    mean = jnp.concatenate(means, axis=1)
    win = jnp.concatenate([jnp.full((1, gch), w, jnp.int32) for w in POOL_WINDOWS], axis=1)

    def rescale(rows, first_pos):
        pos = first_pos + lax.broadcasted_iota(jnp.int32, (halo, 1), 0)
        cnt = jnp.minimum(pos + win // 2, seq) - jnp.maximum(pos - win // 2, 0)
        return rows * (win.astype(F32) / cnt.astype(F32))

    mean = jnp.concatenate([rescale(mean[:halo], t0), mean[halo:tm - halo],
                            rescale(mean[tm - halo:], t0 + tm - halo)], axis=0)
    return mean - main


def _route(logits):
    tm = logits.shape[0]
    lt = logits.T
    gl = lt[N_EXPERTS:N_EXPERTS + N_GROUPS]
    sub_g = lax.broadcasted_iota(jnp.int32, gl.shape, 0)
    gmax = jnp.max(gl, axis=0, keepdims=True)
    grp = jnp.min(jnp.where(gl == gmax, sub_g, N_GROUPS), axis=0, keepdims=True)
    gate_group = 1.0 / jnp.sum(jnp.exp(gl - gmax), axis=0, keepdims=True)
    el = lt[0:EXPERTS_PER_GROUP]
    for g in range(1, N_GROUPS):
        el = jnp.where(grp == g, lt[g * EXPERTS_PER_GROUP:(g + 1) * EXPERTS_PER_GROUP], el)
    sub = lax.broadcasted_iota(jnp.int32, el.shape, 0)
    m1 = jnp.max(el, axis=0, keepdims=True)
    i1 = jnp.min(jnp.where(el == m1, sub, EXPERTS_PER_GROUP), axis=0, keepdims=True)
    el2 = jnp.where(sub == i1, -jnp.inf, el)
    m2 = jnp.max(el2, axis=0, keepdims=True)
    i2 = jnp.min(jnp.where(el2 == m2, sub, EXPERTS_PER_GROUP), axis=0, keepdims=True)
    r = jnp.exp(m2 - m1)
    g1 = gate_group / (1.0 + r)
    g2 = g1 * r
    e1 = (grp * EXPERTS_PER_GROUP + i1).astype(F32)
    e2 = (grp * EXPERTS_PER_GROUP + i2).astype(F32)
    rows = jnp.where(sub == 0, e1, jnp.where(sub == 1, e2, jnp.where(sub == 2, g1, jnp.where(sub == 3, g2, 0.0))))
    cols = jnp.concatenate([rows, jnp.zeros((LANES - rows.shape[0], tm), F32)], axis=0).T
    return cols, rows


def _merge_kernel(x_ref, p_ref, ug_ref, yf_ref, ya_ref, mod_ref, wpool_ref, pscale_ref, wsgu_ref, bsgu_ref,
                  wout_ref, lng_ref, lnb_ref, wr_ref, *rest,
                  mod_row, seq, n_alias):
    x1_ref, h2_ref, route_ref, route_t_ref = rest[n_alias:]
    tm, d = x_ref.shape
    if mod_row is None:
        row = pl.program_id(0)
    else:
        row = mod_row
    t0 = pl.multiple_of(pl.program_id(1) * tm, tm)
    m = mod_ref[pl.ds(row, 1), :]
    gate1, shift2, scale2 = m[:, 2 * d:3 * d], m[:, 3 * d:4 * d], m[:, 4 * d:5 * d]
    lane = lax.broadcasted_iota(jnp.int32, (1, GROUP_W), 1)
    n_heads = wsgu_ref.shape[0] // SGU_CHUNK
    head = lane // (GROUP_W // n_heads)

    pm = min(tm, MERGE_PART)
    for part in range(tm // pm):
        r0 = part * pm
        rows = slice(r0, r0 + pm)
        pooled = _pool(p_ref, t0 + r0, pm, seq)
        y_pool = _dot(pooled.astype(BF16), wpool_ref[...]) * pscale_ref[...]

        ug = ug_ref[rows, :].astype(F32)
        u = _gelu(ug[:, 0:GROUP_W])
        v = _layer_norm(_gelu(ug[:, GROUP_W:])).astype(BF16)
        mixed = []
        for cidx in range(pm // SGU_CHUNK):
            vc = v[cidx * SGU_CHUNK:(cidx + 1) * SGU_CHUNK, :]
            full = _dot(wsgu_ref[...], vc)
            mc = bsgu_ref[...]
            for hd in range(n_heads):
                mc = mc + jnp.where(head == hd, full[hd * SGU_CHUNK:(hd + 1) * SGU_CHUNK, :], 0.0)
            mixed.append(mc)
        y_sgu = u * jnp.concatenate(mixed, axis=0)

        cat = jnp.concatenate([yf_ref[rows, :].astype(BF16), ya_ref[rows, :], y_pool.astype(BF16),
                               y_sgu.astype(BF16)], axis=1)
        y = _dot(cat, wout_ref[...])
        x1 = _layer_norm(RES_ALPHA * x_ref[rows, :] + gate1 * y) * lng_ref[...] + lnb_ref[...]
        x1_ref[rows, :] = x1
        h2 = _layer_norm(x1) * (1.0 + scale2) + shift2
        _store_rows(h2_ref, h2, first=r0)
        lg = _dot(h2.astype(BF16), wr_ref[...])
        route_ref[rows, :], route_t_ref[:, rows] = _route(lg[:, 0:LANES] + lg[:, LANES:])


def _merge(x2, p, ug, y_four, y_attn, mod, w_pool_bd, pool_scale, w_sgu_stack, b_sgu_exp, w_out,
           ln_g, ln_b, w_route, aliased, *, mod_row, seq, tm, row_off, total_rows):
    rows, d = x2.shape
    n_batch, steps = rows // seq, seq // tm
    off = row_off // tm
    kern = functools.partial(_merge_kernel, mod_row=mod_row, seq=seq, n_alias=len(aliased))
    row_spec = lambda w: pl.BlockSpec((tm, w), lambda b, i: (b * steps + i, 0))
    const = lambda a: pl.BlockSpec(a.shape, lambda b, i: (0,) * a.ndim)
    consts = (mod, w_pool_bd, pool_scale, w_sgu_stack, b_sgu_exp, w_out, ln_g, ln_b, w_route)
    n_in = 5 + len(consts)
    out_shapes = [jax.ShapeDtypeStruct((total_rows, d), F32),
                  jax.ShapeDtypeStruct((total_rows * ROW_WORDS, LANES), jnp.uint32),
                  jax.ShapeDtypeStruct((total_rows, LANES), F32),
                  jax.ShapeDtypeStruct((SUBLANES, total_rows), F32)]
    out_specs = [pl.BlockSpec((tm, d), lambda b, i: (off + b * steps + i, 0)),
                 pl.BlockSpec((tm * ROW_WORDS, LANES), lambda b, i: (off + b * steps + i, 0)),
                 pl.BlockSpec((tm, LANES), lambda b, i: (off + b * steps + i, 0)),
                 pl.BlockSpec((SUBLANES, tm), lambda b, i: (0, off + b * steps + i))]
    return pl.pallas_call(
        kern,
        grid=(n_batch, steps),
        in_specs=[row_spec(d), pl.BlockSpec((seq, GROUP_W), lambda b, i: (b, 0)),
                  row_spec(2 * GROUP_W), row_spec(GROUP_W), row_spec(GROUP_W)]
                 + [const(a) for a in consts]
                 + [pl.BlockSpec(memory_space=pl.ANY)] * len(aliased),
        out_specs=out_specs,
        out_shape=out_shapes,
        input_output_aliases={n_in + k: k for k in range(len(aliased))},
        compiler_params=_cparams("arbitrary", "arbitrary"),
        name="merge",
    )(x2, p, ug, y_four, y_attn, *consts, *aliased)


def _plan_kernel(route_ref, dest_ref, cnt_out_ref, cnt_ref, start_ref, carry_ref):
    ph, t = pl.program_id(0), pl.program_id(1)
    tm = route_ref.shape[1]
    rt = route_ref[...]
    e1 = rt[0:1, :].astype(jnp.int32)
    e2 = rt[1:2, :].astype(jnp.int32)
    sub = lax.broadcasted_iota(jnp.int32, (N_EXPERTS, tm), 0)
    hit1, hit2 = sub == e1, sub == e2
    onehot = jnp.where(hit1 | hit2, 1.0, 0.0)
    tile_cnt = jnp.sum(onehot, axis=1, keepdims=True)

    @pl.when((ph == 0) & (t == 0))
    def _():
        cnt_ref[...] = jnp.zeros_like(cnt_ref)

    @pl.when(ph == 0)
    def _():
        cnt_ref[...] += tile_cnt

    @pl.when((ph == 1) & (t == 0))
    def _():
        cnt = cnt_ref[...]
        padded = jnp.floor((cnt + (MOE_BLOCK - 1.0)) * (1.0 / MOE_BLOCK)) * MOE_BLOCK
        row = lax.broadcasted_iota(jnp.int32, cnt.shape, 0)
        incl = padded
        sh = 1
        while sh < N_EXPERTS:
            incl = incl + jnp.where(row >= sh, pltpu.roll(incl, sh, axis=0), 0.0)
            sh *= 2
        start_ref[...] = incl - padded
        carry_ref[...] = jnp.zeros_like(carry_ref)
        cnt_out_ref[...] = cnt

    @pl.when(ph == 1)
    def _():
        r_i = lax.broadcasted_iota(jnp.int32, (tm, tm), 0)
        c_i = lax.broadcasted_iota(jnp.int32, (tm, tm), 1)
        before = jnp.where(r_i < c_i, 1.0, 0.0).astype(BF16)
        rank = _dot(onehot.astype(BF16), before)
        base = start_ref[:, 0:1] + carry_ref[:, 0:1] + rank
        d1 = jnp.sum(jnp.where(hit1, base, 0.0), axis=0, keepdims=True)
        d2 = jnp.sum(jnp.where(hit2, base, 0.0), axis=0, keepdims=True)
        sub8 = lax.broadcasted_iota(jnp.int32, (SUBLANES, tm), 0)
        dest_ref[...] = jnp.where(sub8 == 0, d1, d2).astype(jnp.int32)
        carry_ref[...] += tile_cnt


def _plan(route_t, tm):
    rows = route_t.shape[1]
    n_t = rows // tm
    return pl.pallas_call(
        _plan_kernel,
        grid=(2, n_t),
        in_specs=[pl.BlockSpec((SUBLANES, tm), lambda ph, t: (0, t))],
        out_specs=[pl.BlockSpec((None, SUBLANES, tm), lambda ph, t: (t * ph, 0, 0)),
                   pl.BlockSpec((N_EXPERTS, LANES), lambda ph, t: (0, 0))],
        out_shape=[jax.ShapeDtypeStruct((n_t, SUBLANES, tm), jnp.int32),
                   jax.ShapeDtypeStruct((N_EXPERTS, LANES), F32)],
        scratch_shapes=[pltpu.VMEM((N_EXPERTS, LANES), F32)] * 3,
        compiler_params=_cparams("arbitrary", "arbitrary"),
        name="moe_plan",
    )(route_t)


def _row_copy(src_ref, src_row, dst_ref, dst_row, sem):
    return pltpu.make_async_copy(
        src_ref.at[pl.ds(pl.multiple_of(src_row * ROW_WORDS, ROW_WORDS), ROW_WORDS)],
        dst_ref.at[pl.ds(pl.multiple_of(dst_row * ROW_WORDS, ROW_WORDS), ROW_WORDS)], sem)


def _dispatch_kernel(dest_ref, h2_ref, xs_ref, sem, *, tm):
    def body(r4, c):
        for u in range(DMA_UNROLL):
            r = r4 * DMA_UNROLL + u
            for k in range(2):
                _row_copy(h2_ref, r, xs_ref, dest_ref[0, 0, k * tm + r], sem).start(priority=k)
        return c

    lax.fori_loop(0, tm // DMA_UNROLL, body, 0)
    for k in range(2):
        pltpu.make_async_copy(h2_ref, xs_ref.at[pl.ds(0, tm * ROW_WORDS)], sem).wait()


def _dispatch(dest, h2_tiles, n_slots, tm):
    n_t = dest.shape[0]
    return pl.pallas_call(
        functools.partial(_dispatch_kernel, tm=tm),
        grid=(n_t,),
        in_specs=[pl.BlockSpec((1, 1, 2 * tm), lambda i: (i, 0, 0), memory_space=pltpu.SMEM),
                  pl.BlockSpec((tm * ROW_WORDS, LANES), lambda i: (i, 0))],
        out_specs=pl.BlockSpec(memory_space=pl.ANY),
        out_shape=jax.ShapeDtypeStruct((n_slots * ROW_WORDS, LANES), jnp.uint32),
        scratch_shapes=[pltpu.SemaphoreType.DMA],
        compiler_params=_cparams("arbitrary"),
        name="moe_dispatch",
    )(dest, h2_tiles)


def _expert_kernel(be_ref, bn_ref, first_ref, slot_ref, nxt_ref, nxt2_ref, _blk_ref,
                   x_ref, wg_hbm, wu_hbm, wd_hbm, y_ref,
                   wg_buf, wu_buf, wd_buf, wg_bf, wu_bf, wd_bf, sem, *, layer):
    i = pl.program_id(0)
    mb = x_ref.shape[0] // ROW_WORDS
    half = wg_bf.shape[0] // 2

    def weight_copies(e, s):
        return [pltpu.make_async_copy(wg_hbm.at[layer, e], wg_buf.at[s], sem.at[s]),
                pltpu.make_async_copy(wu_hbm.at[layer, e], wu_buf.at[s], sem.at[s]),
                pltpu.make_async_copy(wd_hbm.at[layer, e], wd_buf.at[s], sem.at[s])]

    @pl.when(i == 0)
    def _():
        for cp in weight_copies(be_ref[0], 0):
            cp.start()

        @pl.when(nxt_ref[0] >= 0)
        def _():
            for cp in weight_copies(nxt_ref[0], 1):
                cp.start(priority=1)

    @pl.when(first_ref[i] == 1)
    def _():
        s = slot_ref[i]
        for cp in weight_copies(be_ref[i], s):
            cp.wait()

        @pl.when(nxt2_ref[i] >= 0)
        def _():
            s2 = jnp.where(s == 0, WEIGHT_SLOTS - 1, s - 1)
            for cp in weight_copies(nxt2_ref[i], s2):
                cp.start(priority=1)

        wg_bf[...] = wg_buf[s].astype(BF16)
        wu_bf[...] = wu_buf[s].astype(BF16)
        wd_bf[...] = wd_buf[s].astype(BF16)

    def compute(m):
        live = lax.broadcasted_iota(jnp.int32, (m, 1), 0) < bn_ref[i]
        x_lo, x_hi = _load_rows(x_ref, 0, m)
        x_lo = jnp.where(live, x_lo, 0.0).astype(BF16)
        x_hi = jnp.where(live, x_hi, 0.0).astype(BF16)
        g = _dot(x_lo, wg_bf[0:half, :]) + _dot(x_hi, wg_bf[half:, :])
        u = _dot(x_lo, wu_bf[0:half, :]) + _dot(x_hi, wu_bf[half:, :])
        hid = (_silu(g) * u).astype(BF16)
        _store_rows(y_ref, _dot(hid, wd_bf[...]))

    for parts in range(1, mb // MOE_PART + 1):
        @pl.when((bn_ref[i] > (parts - 1) * MOE_PART) & (bn_ref[i] <= parts * MOE_PART))
        def _(parts=parts):
            compute(parts * MOE_PART)


def _experts(table, xs, w_gate, w_up, w_down, layer):
    n_blocks = table[0].shape[0]
    _, _, d, de = w_gate.shape
    shape = (MOE_BLOCK * ROW_WORDS, LANES)
    hbm = pl.BlockSpec(memory_space=pl.ANY)
    return pl.pallas_call(
        functools.partial(_expert_kernel, layer=layer),
        grid_spec=pltpu.PrefetchScalarGridSpec(
            num_scalar_prefetch=len(table),
            grid=(n_blocks,),
            in_specs=[pl.BlockSpec(shape, lambda i, *t: (t[-1][i], 0)), hbm, hbm, hbm],
            out_specs=pl.BlockSpec(shape, lambda i, *t: (t[-1][i], 0)),
            scratch_shapes=[pltpu.VMEM((WEIGHT_SLOTS, d, de), F32), pltpu.VMEM((WEIGHT_SLOTS, d, de), F32),
                            pltpu.VMEM((WEIGHT_SLOTS, de, d), F32),
                            pltpu.VMEM((d, de), BF16), pltpu.VMEM((d, de), BF16), pltpu.VMEM((de, d), BF16),
                            pltpu.SemaphoreType.DMA((WEIGHT_SLOTS,))]),
        out_shape=jax.ShapeDtypeStruct(xs.shape, jnp.uint32),
        compiler_params=_cparams("arbitrary"),
        name="moe_experts",
    )(*table, xs, w_gate, w_up, w_down)


def _combine_body(dest_ref, dest_next_ref, x1_ref, route_ref, m, lng_ref, lnb_ref, y_ref, o_ref, buf_ref, sem):
    tm, d = x1_ref.shape
    i = pl.program_id(0)
    n = pl.num_programs(0)
    slot = i % 2

    def gather(idx_ref, s):
        def body(r4, c):
            for u in range(DMA_UNROLL):
                r = r4 * DMA_UNROLL + u
                for k in range(2):
                    _row_copy(y_ref, idx_ref[0, 0, k * tm + r], buf_ref.at[s], k * tm + r,
                              sem.at[s]).start(priority=k)
            return c

        lax.fori_loop(0, tm // DMA_UNROLL, body, 0)

    @pl.when(i == 0)
    def _():
        gather(dest_ref, 0)

    @pl.when(i + 1 < n)
    def _():
        gather(dest_next_ref, 1 - slot)

    for k in range(2):
        pltpu.make_async_copy(y_ref.at[pl.ds(0, tm * ROW_WORDS)],
                              buf_ref.at[slot, pl.ds(0, tm * ROW_WORDS)], sem.at[slot]).wait()

    gate2 = m[:, 5 * d:6 * d]
    rt = route_ref[...]
    f = jnp.zeros((tm, d), F32)
    for k in range(2):
        lo, hi = _load_rows(buf_ref.at[slot], k * tm, tm)
        f = f + jnp.concatenate([lo, hi], axis=1) * rt[:, 2 + k:3 + k]
    o_ref[...] = _layer_norm(RES_ALPHA * x1_ref[...] + gate2 * f) * lng_ref[...] + lnb_ref[...]


def _combine_kernel(dest_ref, dest_next_ref, x1_ref, route_ref, mod_ref, lng_ref, lnb_ref, y_ref, o_ref,
                    buf_ref, sem, *, mod_row, rows_per_batch):
    m = _mod_row(mod_ref, mod_row, x1_ref.shape[0], rows_per_batch)
    _combine_body(dest_ref, dest_next_ref, x1_ref, route_ref, m, lng_ref, lnb_ref, y_ref, o_ref, buf_ref, sem)


def _combine_specs(tm, d, off, steps):
    return [pl.BlockSpec((1, 1, 2 * tm), lambda i: (off + i, 0, 0), memory_space=pltpu.SMEM),
            pl.BlockSpec((1, 1, 2 * tm), lambda i: (off + jnp.minimum(i + 1, steps - 1), 0, 0),
                         memory_space=pltpu.SMEM),
            pl.BlockSpec((tm, d), lambda i: (off + i, 0)),
            pl.BlockSpec((tm, LANES), lambda i: (off + i, 0))]


def _combine_scratch(tm):
    return [pltpu.VMEM((2, 2 * tm * ROW_WORDS, LANES), jnp.uint32), pltpu.SemaphoreType.DMA((2,))]


def _combine(dest, x1, route, mod, ln_g, ln_b, y_tiles, *, tm, row_off, rows, mod_row, rows_per_batch):
    d = x1.shape[1]
    steps = rows // tm
    kern = functools.partial(_combine_kernel, mod_row=mod_row, rows_per_batch=rows_per_batch)
    const = lambda a: pl.BlockSpec(a.shape, lambda i: (0,) * a.ndim)
    return pl.pallas_call(
        kern,
        grid=(steps,),
        in_specs=_combine_specs(tm, d, row_off // tm, steps)
                 + [const(mod), const(ln_g), const(ln_b), pl.BlockSpec(memory_space=pl.ANY)],
        out_specs=pl.BlockSpec((tm, d), lambda i: (i, 0)),
        out_shape=jax.ShapeDtypeStruct((rows, d), F32),
        scratch_shapes=_combine_scratch(tm),
        compiler_params=_cparams("arbitrary"),
        name="moe_combine",
    )(dest, dest, x1, route, mod, ln_g, ln_b, y_tiles)


def _rope_tables(n_pos):
    rows = n_pos // GRID_W
    row = jnp.repeat(jnp.arange(rows), GRID_W).astype(F32)
    col = jnp.tile(jnp.arange(GRID_W), rows).astype(F32)
    n_freq = HEAD_DIM // 4
    freq = ROPE_BASE ** (-jnp.arange(n_freq, dtype=F32) / n_freq)
    ang_r, ang_c = row[:, None] * freq, col[:, None] * freq
    cos_h = jnp.concatenate([jnp.cos(ang_r)] * 2 + [jnp.cos(ang_c)] * 2, axis=1)
    sin_h = jnp.concatenate([-jnp.sin(ang_r), jnp.sin(ang_r), -jnp.sin(ang_c), jnp.sin(ang_c)], axis=1)
    return jnp.tile(cos_h, (1, 2)), jnp.tile(sin_h, (1, 2))


def _block_table(counts, n_blocks):
    cnt = counts.astype(jnp.int32)
    padded = (cnt + MOE_BLOCK - 1) // MOE_BLOCK * MOE_BLOCK
    pad_end = jnp.cumsum(padded)
    pad_start = pad_end - padded
    blk_start = jnp.arange(n_blocks, dtype=jnp.int32)[:, None] * MOE_BLOCK
    be = jnp.minimum(jnp.sum((pad_end[None, :] <= blk_start).astype(jnp.int32), axis=1), N_EXPERTS - 1)
    ids = jnp.arange(N_EXPERTS, dtype=jnp.int32)
    mine = be[:, None] == ids[None, :]
    fill = jnp.sum(jnp.where(mine, cnt[None, :] + pad_start[None, :], 0), axis=1) - blk_start[:, 0]
    bn = jnp.clip(fill, 0, MOE_BLOCK)
    prev = jnp.concatenate([jnp.full((1,), -1, jnp.int32), be[:-1]])
    first = ((bn > 0) & (be != prev)).astype(jnp.int32)
    slot = (jnp.cumsum(first) - 1) % WEIGHT_SLOTS
    later = (ids[None, :] > ids[:, None]) & (cnt[None, :] > 0)
    nxt_e = jnp.min(jnp.where(later, ids[None, :], N_EXPERTS), axis=1)
    hop = nxt_e[:, None] == ids[None, :]
    nxt2_e = jnp.sum(jnp.where(hop, nxt_e[None, :], 0), axis=1) + jnp.where(nxt_e == N_EXPERTS, N_EXPERTS, 0)
    lookup = lambda tab: jnp.sum(jnp.where(mine, jnp.where(tab >= N_EXPERTS, -1, tab)[None, :], 0), axis=1)
    blk_idx = jnp.minimum(blk_start[:, 0] // MOE_BLOCK, jnp.maximum(jnp.sum((bn > 0).astype(jnp.int32)) - 1, 0))
    return (be, bn, first, slot.astype(jnp.int32), lookup(nxt_e).astype(jnp.int32),
            lookup(nxt2_e).astype(jnp.int32), blk_idx.astype(jnp.int32))


def _moe(route_t, h2_tiles, w_gate, w_up, w_down, layer):
    rows = route_t.shape[1]
    tm = ROW_TILE
    n_blocks = -(-(2 * rows) // MOE_BLOCK) + N_EXPERTS
    dest8, counts = _plan(route_t, tm)
    dest = dest8[:, 0:2, :].reshape(rows // tm, 1, 2 * tm)
    table = _block_table(counts[:, 0], n_blocks)
    xs = _dispatch(dest, h2_tiles, n_blocks * MOE_BLOCK, tm)
    ys = _experts(table, xs, w_gate, w_up, w_down, layer)
    tc = COMBINE_TILE
    dest_c = dest8[:, 0:2, :].reshape(rows // tm, 2, tm // tc, tc).transpose(0, 2, 1, 3).reshape(rows // tc, 1, 2 * tc)
    return dest_c, ys


def kernel(x, c, ctx, c_ctx, w_ada, b_ada, w_in, w_fourier, attn_sink, w_pool, pool_scale, w_sgu, b_sgu,
           w_out, ln1_g, ln1_b, w_router_group, w_router_expert, w_exp_gate, w_exp_up, w_exp_down,
           ln2_g, ln2_b):
    b, s, d = x.shape
    n_ctx = ctx.shape[1]
    n_layers = w_in.shape[0]
    tm = ROW_TILE
    cond = jnp.concatenate([c, c_ctx[None, :], jnp.zeros((SUBLANES - b - 1, d), F32)], axis=0)
    mod_all = _ada(cond, w_ada, b_ada[:, None, :])
    cos_t, sin_t = _rope_tables(s)
    x2 = x.reshape(b * s, d)
    c2 = ctx.reshape(b * n_ctx, d)
    n_sgu = w_sgu.shape[1]
    for layer in range(n_layers):
        last = layer == n_layers - 1
        mod = mod_all[layer]
        w_in_l = w_in[layer].astype(BF16)
        wf = w_fourier[layer].astype(BF16)
        w_pool_bd = jax.scipy.linalg.block_diag(*[w_pool[layer, g] for g in range(w_pool.shape[1])]).astype(BF16)
        w_sgu_stack = w_sgu[layer].reshape(n_sgu * SGU_CHUNK, SGU_CHUNK).astype(BF16)
        b_sgu_exp = jnp.repeat(b_sgu[layer].T, GROUP_W // n_sgu, axis=1)
        w_router = jnp.concatenate([w_router_expert[layer].reshape(d, N_EXPERTS), w_router_group[layer]], axis=1)
        w_router = jnp.pad(w_router, ((0, 0), (0, LANES - w_router.shape[1])))
        wr_hi = w_router.astype(BF16)
        w_route = jnp.concatenate([wr_hi, (w_router - wr_hi.astype(F32)).astype(BF16)], axis=1)
        merge_consts = (mod, w_pool_bd, pool_scale[layer][None, :], w_sgu_stack, b_sgu_exp,
                        w_out[layer].astype(BF16), ln1_g[layer][None, :], ln1_b[layer][None, :], w_route)
        sink = attn_sink[layer]

        a, q, qs, k, v, p, ug = _proj(x2, mod, w_in_l, cos_t, sin_t, mod_row=None, rows_per_batch=s,
                                      rope=True, tm=tm, a_pitch=FFT_PITCH)
        ac, qc, qsc, kc, vc, pc, ugc = _proj(c2, mod, w_in_l, cos_t, sin_t, mod_row=b, rows_per_batch=n_ctx,
                                             rope=False, tm=n_ctx, a_pitch=FFT_R)
        y_four = _fourier(a, wf, s)
        y_attn = _attention(sink, q, qs, k, v, kc, vc, seq=s, n_ctx=n_ctx, band=True)
        total = b * s + (0 if last else b * n_ctx)
        merged = _merge(x2, p, ug, y_four, y_attn, *merge_consts, (),
                        mod_row=None, seq=s, tm=tm, row_off=0, total_rows=total)
        if not last:
            yc_four = _fourier_small(ac, wf, n_ctx)
            yc_attn = _attention(sink, qc, qsc, kc, vc, kc, vc, seq=n_ctx, n_ctx=n_ctx, band=False)
            merged = _merge(c2, pc, ugc, yc_four, yc_attn, *merge_consts, tuple(merged),
                            mod_row=b, seq=n_ctx, tm=n_ctx, row_off=b * s, total_rows=total)
        x1, h2_tiles, route, route_t = merged
        dest, ys = _moe(route_t, h2_tiles, w_exp_gate, w_exp_up, w_exp_down, layer)
        ln_g, ln_b = ln2_g[layer][None, :], ln2_b[layer][None, :]
        x2 = _combine(dest, x1, route, mod, ln_g, ln_b, ys, tm=COMBINE_TILE, row_off=0, rows=b * s,
                      mod_row=None, rows_per_batch=s)
        if not last:
            c2 = _combine(dest, x1, route, mod, ln_g, ln_b, ys, tm=COMBINE_TILE, row_off=b * s,
                          rows=b * n_ctx, mod_row=b, rows_per_batch=n_ctx)
    return x2.reshape(b, s, d)
```

```python
import functools
import math

import numpy as np
import jax
import jax.numpy as jnp
from jax import lax
from jax.experimental import pallas as pl
from jax.experimental.pallas import tpu as pltpu

GRID_W = 64
HEAD_DIM = 64
GROUP_W = 256
KV_W = 128
WINDOW = 128
POOL_WINDOWS = (2, 4, 8, 16)
SGU_CHUNK = 128
N_GROUPS = 4
EXPERTS_PER_GROUP = 8
N_EXPERTS = 32
ROPE_BASE = 10000.0
LN_EPS = 1e-6
NEG_INF = -1e30
DEPTH = 2
RES_ALPHA = (2 * DEPTH) ** 0.25

LANES = 128
SUBLANES = 8
VMEM_LIMIT = 48 * 1024 * 1024

ROW_TILE = 1024
Q_BLOCK = 1024
COMBINE_TILE = 512
WEIGHT_SLOTS = 3
MOE_BLOCK = 512
MOE_PART = 256
MOE_STEP_BLOCKS = 4
FFT_R = 64
FFT_PITCH = 72
FFT_UNROLL = 16
DMA_UNROLL = 8
MERGE_PART = 512

BF16 = jnp.bfloat16
F32 = jnp.float32


def _cparams(*sem):
    return pltpu.CompilerParams(dimension_semantics=sem, vmem_limit_bytes=VMEM_LIMIT)


def _dot(a, b):
    return jnp.dot(a, b, preferred_element_type=F32)


def _dot_nt(a, b):
    return lax.dot_general(a, b, (((1,), (1,)), ((), ())), preferred_element_type=F32)


def _layer_norm(t):
    mu = jnp.mean(t, axis=-1, keepdims=True)
    d = t - mu
    var = jnp.mean(d * d, axis=-1, keepdims=True)
    return d * lax.rsqrt(var + LN_EPS)


def _silu(t):
    return t * (1.0 / (1.0 + jnp.exp(-t)))


def _gelu(t):
    return 0.5 * t * (1.0 + lax.erf(t * (1.0 / math.sqrt(2.0))))


ROW_WORDS = 4
HI_MASK = 0xFFFF0000


def _pack_rows(t):
    half = t.shape[1] // 2
    lo = lax.bitcast_convert_type(t[:, :half].astype(BF16).astype(F32), jnp.uint32)
    hi = lax.bitcast_convert_type(t[:, half:].astype(BF16).astype(F32), jnp.uint32)
    return (lo >> 16) | (hi & jnp.uint32(HI_MASK))


def _unpack_rows(w):
    return (lax.bitcast_convert_type(w << 16, F32),
            lax.bitcast_convert_type(w & jnp.uint32(HI_MASK), F32))


def _store_rows(ref, t, first=0):
    w = _pack_rows(t)
    for j in range(ROW_WORDS):
        ref[pl.ds(first * ROW_WORDS + j, t.shape[0], stride=ROW_WORDS), :] = w[:, j * LANES:(j + 1) * LANES]


def _load_rows(ref, first, m):
    w = jnp.concatenate([ref[pl.ds(first * ROW_WORDS + j, m, stride=ROW_WORDS), :] for j in range(ROW_WORDS)],
                        axis=1)
    return _unpack_rows(w)


def _ada_kernel(c_ref, w_ref, b_ref, o_ref):
    s = _silu(c_ref[...]).astype(BF16)
    o_ref[...] = _dot(s, w_ref[...].astype(BF16)) + b_ref[...]


def _ada(cond, w_ada, b_ada):
    n_layers, d, n = w_ada.shape
    tn = n // 4
    return pl.pallas_call(
        _ada_kernel,
        grid=(n_layers, n // tn),
        in_specs=[
            pl.BlockSpec((SUBLANES, d), lambda l, j: (0, 0)),
            pl.BlockSpec((None, d, tn), lambda l, j: (l, 0, j)),
            pl.BlockSpec((None, 1, tn), lambda l, j: (l, 0, j)),
        ],
        out_specs=pl.BlockSpec((None, SUBLANES, tn), lambda l, j: (l, 0, j)),
        out_shape=jax.ShapeDtypeStruct((n_layers, SUBLANES, n), F32),
        compiler_params=_cparams("arbitrary", "arbitrary"),
        name="ada",
    )(cond, w_ada, b_ada)


def _rope(t, cos_t, sin_t):
    lane = lax.broadcasted_iota(jnp.int32, t.shape, 1)
    first = (lane % 32) < 16
    partner = jnp.where(first, pltpu.roll(t, LANES - 16, axis=1), pltpu.roll(t, 16, axis=1))
    return t * cos_t + partner * sin_t


def _proj_body(x, m, w_ref, cos_ref, sin_ref, outs, *, rope, a_pitch):
    a_ref, q_ref, qs_ref, k_ref, v_ref, p_ref, ug_ref = outs
    tm, d = x.shape
    shift, scale = m[:, 0:d], m[:, d:2 * d]
    h = _layer_norm(x) * (1.0 + scale) + shift
    z = _dot(h.astype(BF16), w_ref[...])
    pad = jnp.zeros((a_pitch - FFT_R, LANES), F32)
    for g in range(tm // FFT_R):
        for hf in range(2):
            grp = z[g * FFT_R:(g + 1) * FFT_R, hf * LANES:(hf + 1) * LANES]
            if a_pitch > FFT_R:
                grp = jnp.concatenate([grp, pad], axis=0)
            a_ref[hf, g * a_pitch:(g + 1) * a_pitch, :] = grp
    q0, q1 = z[:, 256:384], z[:, 384:512]
    k = z[:, 512:640]
    if rope:
        cos_t, sin_t = cos_ref[...], sin_ref[...]
        q0, q1, k = _rope(q0, cos_t, sin_t), _rope(q1, cos_t, sin_t), _rope(k, cos_t, sin_t)
    q_ref[:, 0:128] = q0.astype(BF16)
    q_ref[:, 128:256] = q1.astype(BF16)
    qs_ref[:, 0:128] = pltpu.roll(q0, HEAD_DIM, axis=1).astype(BF16)
    qs_ref[:, 128:256] = pltpu.roll(q1, HEAD_DIM, axis=1).astype(BF16)
    k_ref[...] = k.astype(BF16)
    v_ref[...] = z[:, 640:768].astype(BF16)
    p_ref[...] = z[:, 768:1024].astype(BF16)
    ug_ref[...] = z[:, 1024:1536].astype(BF16)


def _mod_row(mod_ref, mod_row, tm, rows_per_batch):
    row = (pl.program_id(0) * tm) // rows_per_batch if mod_row is None else mod_row
    return mod_ref[pl.ds(row, 1), :]


def _proj_kernel(x_ref, mod_ref, w_ref, cos_ref, sin_ref, *outs, mod_row, rows_per_batch, rope, a_pitch):
    m = _mod_row(mod_ref, mod_row, x_ref.shape[0], rows_per_batch)
    _proj_body(x_ref[...], m, w_ref, cos_ref, sin_ref, outs, rope=rope, a_pitch=a_pitch)


def _proj_specs(rows, tm, a_pitch, seq_steps):
    row_spec = lambda w: pl.BlockSpec((tm, w), lambda i: (i, 0))
    out_w = (256, 256, 128, 128, 256, 512)
    ta = tm // FFT_R * a_pitch
    out_specs = [pl.BlockSpec((2, ta, LANES), lambda i: (0, i, 0))] + [row_spec(w) for w in out_w]
    out_shape = ([jax.ShapeDtypeStruct((2, rows // FFT_R * a_pitch, LANES), F32)]
                 + [jax.ShapeDtypeStruct((rows, w), BF16) for w in out_w])
    table_spec = pl.BlockSpec((tm, LANES), lambda i: (i % seq_steps, 0))
    return table_spec, out_specs, out_shape


def _proj(x2, mod, w_in, cos_t, sin_t, *, mod_row, rows_per_batch, rope, tm, a_pitch):
    rows, d = x2.shape
    kern = functools.partial(_proj_kernel, mod_row=mod_row, rows_per_batch=rows_per_batch,
                             rope=rope, a_pitch=a_pitch)
    table_spec, out_specs, out_shape = _proj_specs(rows, tm, a_pitch, cos_t.shape[0] // tm)
    return pl.pallas_call(
        kern,
        grid=(rows // tm,),
        in_specs=[
            pl.BlockSpec((tm, d), lambda i: (i, 0)),
            pl.BlockSpec(mod.shape, lambda i: (0, 0)),
            pl.BlockSpec(w_in.shape, lambda i: (0, 0)),
            table_spec, table_spec,
        ],
        out_specs=out_specs,
        out_shape=out_shape,
        compiler_params=_cparams("arbitrary"),
        name="proj",
    )(x2, mod, w_in, cos_t, sin_t)


def _fft_tables(n_pos):
    r = FFT_R
    assert n_pos == r * r
    kb = np.arange(r)[None, :, None]
    na = np.arange(r)[:, None, None]
    nb = np.arange(r)[None, None, :]
    ang = 2.0 * np.pi * ((kb * (na + r * nb)) % n_pos) / n_pos
    m1 = np.concatenate([np.cos(ang), -np.sin(ang)], axis=1)
    ka = np.arange(r)[:, None]
    n2 = np.arange(r)[None, :]
    ang2 = 2.0 * np.pi * ((ka * n2) % r) / r
    c2, s2 = np.cos(ang2), np.sin(ang2)
    w2 = np.block([[c2, s2], [-s2, c2]])
    return m1, w2


def _channel_tables(n_pos):
    h = HEAD_DIM
    c = np.arange(h)
    ang = 2.0 * np.pi * ((c[:, None] * c[None, :]) % h) / h
    scale = 1.0 / math.sqrt(n_pos * h)
    eye = np.eye(GROUP_W // h)
    cc = np.kron(eye, np.cos(ang)) * scale
    ss = np.kron(eye, np.sin(ang)) * scale
    return np.concatenate([cc, ss], axis=0)


def _fourier_kernel(a_ref, m1_ref, w2_ref, ch_ref, wf_ref, o_ref, z_ref, y_ref):
    r, pt = FFT_R, FFT_PITCH

    def step1(i, c):
        for u in range(FFT_UNROLL):
            na = i * FFT_UNROLL + u
            rows = jnp.concatenate([a_ref[0, pl.ds(na, r, stride=pt), :],
                                    a_ref[1, pl.ds(na, r, stride=pt), :]], axis=1)
            z = _dot(m1_ref[na], rows.astype(BF16))
            base = pl.multiple_of(na * pt, SUBLANES)
            z_ref[0, pl.ds(base, r), :] = z[0:r, 0:LANES]
            z_ref[1, pl.ds(base, r), :] = z[0:r, LANES:]
            z_ref[2, pl.ds(base, r), :] = z[r:, 0:LANES]
            z_ref[3, pl.ds(base, r), :] = z[r:, LANES:]
        return c

    lax.fori_loop(0, r // FFT_UNROLL, step1, 0)

    def step2(i, c):
        for u in range(FFT_UNROLL):
            kb = i * FFT_UNROLL + u
            q = [z_ref[j, pl.ds(kb, r, stride=pt), :] for j in range(4)]
            zs = jnp.concatenate([jnp.concatenate(q[0:2], axis=1),
                                  jnp.concatenate(q[2:4], axis=1)], axis=0)
            y = _dot(w2_ref[...], zs.astype(BF16))
            base = pl.multiple_of(kb * r, r)
            y_ref[0, pl.ds(base, r), :] = y[0:r, 0:LANES]
            y_ref[1, pl.ds(base, r), :] = y[0:r, LANES:]
            y_ref[2, pl.ds(base, r), :] = y[r:, 0:LANES]
            y_ref[3, pl.ds(base, r), :] = y[r:, LANES:]
        return c

    lax.fori_loop(0, r // FFT_UNROLL, step2, 0)

    chunk = 8 * r
    for cidx in range(r * r // chunk):
        yy = jnp.concatenate([y_ref[j, cidx * chunk:(cidx + 1) * chunk, :] for j in range(4)], axis=1)
        f = _dot(yy.astype(BF16), ch_ref[...])
        g = _dot(f.astype(BF16), wf_ref[...])
        for gi in range(chunk // r):
            kb = cidx * (chunk // r) + gi
            z_ref[0, kb * pt:kb * pt + r, :] = g[gi * r:(gi + 1) * r, 0:LANES]
            z_ref[1, kb * pt:kb * pt + r, :] = g[gi * r:(gi + 1) * r, LANES:]

    def step3(i, c):
        for u in range(FFT_UNROLL):
            ka = i * FFT_UNROLL + u
            base = pl.multiple_of(ka * r, r)
            o_ref[pl.ds(base, r), 0:LANES] = z_ref[0, pl.ds(ka, r, stride=pt), :]
            o_ref[pl.ds(base, r), LANES:] = z_ref[1, pl.ds(ka, r, stride=pt), :]
        return c

    lax.fori_loop(0, r // FFT_UNROLL, step3, 0)


def _fourier(a3, w_fourier, n_pos):
    rows = a3.shape[1] // FFT_PITCH * FFT_R
    gw = GROUP_W
    m1, w2 = _fft_tables(n_pos)
    ch = _channel_tables(n_pos)
    const = lambda shape: pl.BlockSpec(shape, lambda b: (0,) * len(shape))
    return pl.pallas_call(
        _fourier_kernel,
        grid=(rows // n_pos,),
        in_specs=[
            pl.BlockSpec((2, FFT_R * FFT_PITCH, LANES), lambda b: (0, b, 0)),
            const(m1.shape), const(w2.shape), const(ch.shape), const(w_fourier.shape),
        ],
        out_specs=pl.BlockSpec((n_pos, gw), lambda b: (b, 0)),
        out_shape=jax.ShapeDtypeStruct((rows, gw), F32),
        scratch_shapes=[pltpu.VMEM((4, FFT_R * FFT_PITCH, LANES), F32), pltpu.VMEM((4, n_pos, LANES), F32)],
        compiler_params=_cparams("arbitrary"),
        name="fourier",
    )(a3, jnp.asarray(m1, BF16), jnp.asarray(w2, BF16), jnp.asarray(ch, BF16), w_fourier)


def _fourier_small_kernel(a_ref, cs_ref, ch_ref, wf_ref, o_ref):
    n = a_ref.shape[1]
    a = jnp.concatenate([a_ref[0], a_ref[1]], axis=1)
    pq = _dot(cs_ref[...], a.astype(BF16))
    y = jnp.concatenate([pq[0:n], pq[n:2 * n]], axis=1).astype(BF16)
    f = _dot(y, ch_ref[...])
    o_ref[...] = _dot(f.astype(BF16), wf_ref[...])


def _fourier_small(a3, w_fourier, n_pos):
    _, rows, _ = a3.shape
    gw = GROUP_W
    k = np.arange(n_pos)
    ang = 2.0 * np.pi * ((k[:, None] * k[None, :]) % n_pos) / n_pos
    cs = np.concatenate([np.cos(ang), -np.sin(ang)], axis=0)
    ch = _channel_tables(n_pos)
    const = lambda shape: pl.BlockSpec(shape, lambda b: (0,) * len(shape))
    return pl.pallas_call(
        _fourier_small_kernel,
        grid=(rows // n_pos,),
        in_specs=[pl.BlockSpec((2, n_pos, LANES), lambda b: (0, b, 0)),
                  const(cs.shape), const(ch.shape), const(w_fourier.shape)],
        out_specs=pl.BlockSpec((n_pos, gw), lambda b: (b, 0)),
        out_shape=jax.ShapeDtypeStruct((rows, gw), F32),
        compiler_params=_cparams("arbitrary"),
        name="fourier_ctx",
    )(a3, jnp.asarray(cs, BF16), jnp.asarray(ch, BF16), w_fourier)


ATTN_SUB = 128


def _attn_kernel(sink_ref, q_ref, qs_ref, k_ref, v_ref, kc_ref, vc_ref, o_ref, *, band, seq):
    qb = q_ref.shape[0]
    sub = ATTN_SUB
    lane = lax.broadcasted_iota(jnp.int32, (1, LANES), 1)
    lo_half = lane < HEAD_DIM
    zero = jnp.zeros((), BF16)
    scale = jnp.asarray(HEAD_DIM ** -0.5, BF16)
    kw = sub + 2 * WINDOW
    for sb in range(qb // sub):
        rows = slice(sb * sub, (sb + 1) * sub)
        qa0, qa1 = q_ref[rows, 0:LANES], q_ref[rows, LANES:]
        qs0, qs1 = qs_ref[rows, 0:LANES], qs_ref[rows, LANES:]
        q_all = jnp.concatenate([jnp.where(lo_half, qa0, zero), jnp.where(lo_half, qs0, zero),
                                 jnp.where(lo_half, zero, qs1), jnp.where(lo_half, zero, qa1)], axis=0) * scale
        if band:
            p0 = pl.program_id(1) * qb + sb * sub
            start = pl.multiple_of(jnp.clip(p0 - WINDOW, 0, seq - kw), WINDOW)
            qpos = p0 + lax.broadcasted_iota(jnp.int32, (sub, 1), 0)
            kpos = start + lax.broadcasted_iota(jnp.int32, (1, kw), 1)
            bias = jnp.where(jnp.abs(qpos - kpos) <= WINDOW, 0.0, NEG_INF)
            keys = jnp.concatenate([k_ref[pl.ds(start, kw), :], kc_ref[...]], axis=0)
            vals = jnp.concatenate([v_ref[pl.ds(start, kw), :], vc_ref[...]], axis=0)
        else:
            keys, vals = kc_ref[...], vc_ref[...]
        s_all = _dot_nt(q_all, keys)
        probs, dens = [], []
        for h in range(4):
            s = s_all[h * sub:(h + 1) * sub, :]
            sink = sink_ref[h]
            if band:
                s = jnp.concatenate([s[:, 0:kw] + bias, s[:, kw:]], axis=1)
            m = jnp.maximum(jnp.max(s, axis=1, keepdims=True), sink)
            p = jnp.exp(s - m)
            dens.append(jnp.sum(p, axis=1, keepdims=True) + jnp.exp(sink - m))
            probs.append(p.astype(BF16))
        o_all = _dot(jnp.concatenate(probs, axis=0), vals)
        o = [o_all[h * sub:(h + 1) * sub, :] / dens[h] for h in range(4)]
        o_ref[rows, 0:LANES] = jnp.where(lo_half, o[0], pltpu.roll(o[1], HEAD_DIM, axis=1)).astype(BF16)
        o_ref[rows, LANES:] = jnp.where(lo_half, pltpu.roll(o[2], HEAD_DIM, axis=1), o[3]).astype(BF16)


def _attention(sink, q, qs, k, v, kc, vc, *, seq, n_ctx, band):
    rows = q.shape[0]
    n_batch = rows // seq
    qb = Q_BLOCK if band else seq
    steps = seq // qb
    kern = functools.partial(_attn_kernel, band=band, seq=seq)
    seq_spec = pl.BlockSpec((seq, KV_W), lambda b, i: (b, 0))
    ctx_spec = pl.BlockSpec((n_ctx, KV_W), lambda b, i: (b, 0))
    q_spec = pl.BlockSpec((qb, GROUP_W), lambda b, i: (b * steps + i, 0))
    return pl.pallas_call(
        kern,
        grid=(n_batch, steps),
        in_specs=[pl.BlockSpec(memory_space=pltpu.SMEM), q_spec, q_spec,
                  seq_spec, seq_spec, ctx_spec, ctx_spec],
        out_specs=q_spec,
        out_shape=jax.ShapeDtypeStruct((rows, GROUP_W), BF16),
        compiler_params=_cparams("arbitrary", "arbitrary"),
        name="attn" if band else "attn_ctx",
    )(sink, q, qs, k, v, kc, vc)


POOL_HALO = max(POOL_WINDOWS) // 2


def _pool(p_ref, t0, tm, seq):
    halo = POOL_HALO
    pack = 2 * SUBLANES
    t0 = pl.multiple_of(t0, pack)
    main = p_ref[pl.ds(t0, tm), :].astype(F32)
    lo = pl.multiple_of(jnp.maximum(t0 - pack, 0), pack)
    hi = pl.multiple_of(jnp.minimum(t0 + tm, seq - pack), pack)
    prev = p_ref[pl.ds(lo, pack), :].astype(F32)[pack - halo:, :]
    nxt = p_ref[pl.ds(hi, pack), :].astype(F32)[:halo, :]
    prev = jnp.where(t0 > 0, prev, 0.0)
    nxt = jnp.where(t0 + tm < seq, nxt, 0.0)
    full = jnp.concatenate([prev, main, nxt], axis=0)
    n = tm + 2 * halo
    gch = GROUP_W // len(POOL_WINDOWS)
    first = lax.broadcasted_iota(jnp.int32, (1, LANES), 1) < gch
    means = []
    for hf in range(GROUP_W // LANES):
        wa, wb = POOL_WINDOWS[2 * hf], POOL_WINDOWS[2 * hf + 1]
        x = full[:, hf * LANES:(hf + 1) * LANES]
        sums, w, s = {}, 2, pltpu.roll(x, 1, axis=0) + x
        sums[w] = s
        while w < wb:
            s = pltpu.roll(s, w // 2, axis=0) + pltpu.roll(s, n - w // 2, axis=0)
            w *= 2
            sums[w] = s
        means.append(jnp.where(first, sums[wa] * (1.0 / wa), sums[wb] * (1.0 / wb))[halo:halo + tm, :])
    mean = jnp.concatenate(means, axis=1)
    win = jnp.concatenate([jnp.full((1, gch), w, jnp.int32) for w in POOL_WINDOWS], axis=1)

    def rescale(rows, first_pos):
        pos = first_pos + lax.broadcasted_iota(jnp.int32, (halo, 1), 0)
        cnt = jnp.minimum(pos + win // 2, seq) - jnp.maximum(pos - win // 2, 0)
        return rows * (win.astype(F32) / cnt.astype(F32))

    mean = jnp.concatenate([rescale(mean[:halo], t0), mean[halo:tm - halo],
                            rescale(mean[tm - halo:], t0 + tm - halo)], axis=0)
    return mean - main


def _route(logits):
    tm = logits.shape[0]
    lt = logits.T
    gl = lt[N_EXPERTS:N_EXPERTS + N_GROUPS]
    sub_g = lax.broadcasted_iota(jnp.int32, gl.shape, 0)
    gmax = jnp.max(gl, axis=0, keepdims=True)
    grp = jnp.min(jnp.where(gl == gmax, sub_g, N_GROUPS), axis=0, keepdims=True)
    gate_group = 1.0 / jnp.sum(jnp.exp(gl - gmax), axis=0, keepdims=True)
    el = lt[0:EXPERTS_PER_GROUP]
    for g in range(1, N_GROUPS):
        el = jnp.where(grp == g, lt[g * EXPERTS_PER_GROUP:(g + 1) * EXPERTS_PER_GROUP], el)
    sub = lax.broadcasted_iota(jnp.int32, el.shape, 0)
    m1 = jnp.max(el, axis=0, keepdims=True)
    i1 = jnp.min(jnp.where(el == m1, sub, EXPERTS_PER_GROUP), axis=0, keepdims=True)
    el2 = jnp.where(sub == i1, -jnp.inf, el)
    m2 = jnp.max(el2, axis=0, keepdims=True)
    i2 = jnp.min(jnp.where(el2 == m2, sub, EXPERTS_PER_GROUP), axis=0, keepdims=True)
    r = jnp.exp(m2 - m1)
    g1 = gate_group / (1.0 + r)
    g2 = g1 * r
    e1 = (grp * EXPERTS_PER_GROUP + i1).astype(F32)
    e2 = (grp * EXPERTS_PER_GROUP + i2).astype(F32)
    rows = jnp.where(sub == 0, e1, jnp.where(sub == 1, e2, jnp.where(sub == 2, g1, jnp.where(sub == 3, g2, 0.0))))
    cols = jnp.concatenate([rows, jnp.zeros((LANES - rows.shape[0], tm), F32)], axis=0).T
    return cols, rows


def _merge_kernel(x_ref, p_ref, ug_ref, yf_ref, ya_ref, mod_ref, wpool_ref, pscale_ref, wsgu_ref, bsgu_ref,
                  wout_ref, lng_ref, lnb_ref, wr_ref, *rest,
                  mod_row, seq, n_alias):
    x1_ref, h2_ref, route_ref, route_t_ref = rest[n_alias:]
    tm, d = x_ref.shape
    if mod_row is None:
        row = pl.program_id(0)
    else:
        row = mod_row
    t0 = pl.multiple_of(pl.program_id(1) * tm, tm)
    m = mod_ref[pl.ds(row, 1), :]
    gate1, shift2, scale2 = m[:, 2 * d:3 * d], m[:, 3 * d:4 * d], m[:, 4 * d:5 * d]
    lane = lax.broadcasted_iota(jnp.int32, (1, GROUP_W), 1)
    n_heads = wsgu_ref.shape[0] // SGU_CHUNK
    head = lane // (GROUP_W // n_heads)

    pm = min(tm, MERGE_PART)
    for part in range(tm // pm):
        r0 = part * pm
        rows = slice(r0, r0 + pm)
        pooled = _pool(p_ref, t0 + r0, pm, seq)
        y_pool = _dot(pooled.astype(BF16), wpool_ref[...]) * pscale_ref[...]

        ug = ug_ref[rows, :].astype(F32)
        u = _gelu(ug[:, 0:GROUP_W])
        v = _layer_norm(_gelu(ug[:, GROUP_W:])).astype(BF16)
        mixed = []
        for cidx in range(pm // SGU_CHUNK):
            vc = v[cidx * SGU_CHUNK:(cidx + 1) * SGU_CHUNK, :]
            full = _dot(wsgu_ref[...], vc)
            mc = bsgu_ref[...]
            for hd in range(n_heads):
                mc = mc + jnp.where(head == hd, full[hd * SGU_CHUNK:(hd + 1) * SGU_CHUNK, :], 0.0)
            mixed.append(mc)
        y_sgu = u * jnp.concatenate(mixed, axis=0)

        cat = jnp.concatenate([yf_ref[rows, :].astype(BF16), ya_ref[rows, :], y_pool.astype(BF16),
                               y_sgu.astype(BF16)], axis=1)
        y = _dot(cat, wout_ref[...])
        x1 = _layer_norm(RES_ALPHA * x_ref[rows, :] + gate1 * y) * lng_ref[...] + lnb_ref[...]
        x1_ref[rows, :] = x1
        h2 = _layer_norm(x1) * (1.0 + scale2) + shift2
        _store_rows(h2_ref, h2, first=r0)
        lg = _dot(h2.astype(BF16), wr_ref[...])
        route_ref[rows, :], route_t_ref[:, rows] = _route(lg[:, 0:LANES] + lg[:, LANES:])


def _merge(x2, p, ug, y_four, y_attn, mod, w_pool_bd, pool_scale, w_sgu_stack, b_sgu_exp, w_out,
           ln_g, ln_b, w_route, aliased, *, mod_row, seq, tm, row_off, total_rows):
    rows, d = x2.shape
    n_batch, steps = rows // seq, seq // tm
    off = row_off // tm
    kern = functools.partial(_merge_kernel, mod_row=mod_row, seq=seq, n_alias=len(aliased))
    row_spec = lambda w: pl.BlockSpec((tm, w), lambda b, i: (b * steps + i, 0))
    const = lambda a: pl.BlockSpec(a.shape, lambda b, i: (0,) * a.ndim)
    consts = (mod, w_pool_bd, pool_scale, w_sgu_stack, b_sgu_exp, w_out, ln_g, ln_b, w_route)
    n_in = 5 + len(consts)
    out_shapes = [jax.ShapeDtypeStruct((total_rows, d), F32),
                  jax.ShapeDtypeStruct((total_rows * ROW_WORDS, LANES), jnp.uint32),
                  jax.ShapeDtypeStruct((total_rows, LANES), F32),
                  jax.ShapeDtypeStruct((SUBLANES, total_rows), F32)]
    out_specs = [pl.BlockSpec((tm, d), lambda b, i: (off + b * steps + i, 0)),
                 pl.BlockSpec((tm * ROW_WORDS, LANES), lambda b, i: (off + b * steps + i, 0)),
                 pl.BlockSpec((tm, LANES), lambda b, i: (off + b * steps + i, 0)),
                 pl.BlockSpec((SUBLANES, tm), lambda b, i: (0, off + b * steps + i))]
    return pl.pallas_call(
        kern,
        grid=(n_batch, steps),
        in_specs=[row_spec(d), pl.BlockSpec((seq, GROUP_W), lambda b, i: (b, 0)),
                  row_spec(2 * GROUP_W), row_spec(GROUP_W), row_spec(GROUP_W)]
                 + [const(a) for a in consts]
                 + [pl.BlockSpec(memory_space=pl.ANY)] * len(aliased),
        out_specs=out_specs,
        out_shape=out_shapes,
        input_output_aliases={n_in + k: k for k in range(len(aliased))},
        compiler_params=_cparams("arbitrary", "arbitrary"),
        name="merge",
    )(x2, p, ug, y_four, y_attn, *consts, *aliased)


def _plan_kernel(route_ref, dest_ref, cnt_out_ref, cnt_ref, start_ref, carry_ref):
    ph, t = pl.program_id(0), pl.program_id(1)
    tm = route_ref.shape[1]
    rt = route_ref[...]
    e1 = rt[0:1, :].astype(jnp.int32)
    e2 = rt[1:2, :].astype(jnp.int32)
    sub = lax.broadcasted_iota(jnp.int32, (N_EXPERTS, tm), 0)
    hit1, hit2 = sub == e1, sub == e2
    onehot = jnp.where(hit1 | hit2, 1.0, 0.0)
    tile_cnt = jnp.sum(onehot, axis=1, keepdims=True)

    @pl.when((ph == 0) & (t == 0))
    def _():
        cnt_ref[...] = jnp.zeros_like(cnt_ref)

    @pl.when(ph == 0)
    def _():
        cnt_ref[...] += tile_cnt

    @pl.when((ph == 1) & (t == 0))
    def _():
        cnt = cnt_ref[...]
        padded = jnp.floor((cnt + (MOE_BLOCK - 1.0)) * (1.0 / MOE_BLOCK)) * MOE_BLOCK
        row = lax.broadcasted_iota(jnp.int32, cnt.shape, 0)
        incl = padded
        sh = 1
        while sh < N_EXPERTS:
            incl = incl + jnp.where(row >= sh, pltpu.roll(incl, sh, axis=0), 0.0)
            sh *= 2
        start_ref[...] = incl - padded
        carry_ref[...] = jnp.zeros_like(carry_ref)
        cnt_out_ref[...] = cnt

    @pl.when(ph == 1)
    def _():
        r_i = lax.broadcasted_iota(jnp.int32, (tm, tm), 0)
        c_i = lax.broadcasted_iota(jnp.int32, (tm, tm), 1)
        before = jnp.where(r_i < c_i, 1.0, 0.0).astype(BF16)
        rank = _dot(onehot.astype(BF16), before)
        base = start_ref[:, 0:1] + carry_ref[:, 0:1] + rank
        d1 = jnp.sum(jnp.where(hit1, base, 0.0), axis=0, keepdims=True)
        d2 = jnp.sum(jnp.where(hit2, base, 0.0), axis=0, keepdims=True)
        sub8 = lax.broadcasted_iota(jnp.int32, (SUBLANES, tm), 0)
        dest_ref[...] = jnp.where(sub8 == 0, d1, d2).astype(jnp.int32)
        carry_ref[...] += tile_cnt


def _plan(route_t, tm):
    rows = route_t.shape[1]
    n_t = rows // tm
    return pl.pallas_call(
        _plan_kernel,
        grid=(2, n_t),
        in_specs=[pl.BlockSpec((SUBLANES, tm), lambda ph, t: (0, t))],
        out_specs=[pl.BlockSpec((None, SUBLANES, tm), lambda ph, t: (t * ph, 0, 0)),
                   pl.BlockSpec((N_EXPERTS, LANES), lambda ph, t: (0, 0))],
        out_shape=[jax.ShapeDtypeStruct((n_t, SUBLANES, tm), jnp.int32),
                   jax.ShapeDtypeStruct((N_EXPERTS, LANES), F32)],
        scratch_shapes=[pltpu.VMEM((N_EXPERTS, LANES), F32)] * 3,
        compiler_params=_cparams("arbitrary", "arbitrary"),
        name="moe_plan",
    )(route_t)


def _row_copy(src_ref, src_row, dst_ref, dst_row, sem):
    return pltpu.make_async_copy(
        src_ref.at[pl.ds(pl.multiple_of(src_row * ROW_WORDS, ROW_WORDS), ROW_WORDS)],
        dst_ref.at[pl.ds(pl.multiple_of(dst_row * ROW_WORDS, ROW_WORDS), ROW_WORDS)], sem)


def _dispatch_kernel(dest_ref, h2_ref, xs_ref, sem, *, tm):
    def body(r4, c):
        for u in range(DMA_UNROLL):
            r = r4 * DMA_UNROLL + u
            for k in range(2):
                _row_copy(h2_ref, r, xs_ref, dest_ref[0, 0, k * tm + r], sem).start(priority=k)
        return c

    lax.fori_loop(0, tm // DMA_UNROLL, body, 0)
    for k in range(2):
        pltpu.make_async_copy(h2_ref, xs_ref.at[pl.ds(0, tm * ROW_WORDS)], sem).wait()


def _dispatch(dest, h2_tiles, n_slots, tm):
    n_t = dest.shape[0]
    return pl.pallas_call(
        functools.partial(_dispatch_kernel, tm=tm),
        grid=(n_t,),
        in_specs=[pl.BlockSpec((1, 1, 2 * tm), lambda i: (i, 0, 0), memory_space=pltpu.SMEM),
                  pl.BlockSpec((tm * ROW_WORDS, LANES), lambda i: (i, 0))],
        out_specs=pl.BlockSpec(memory_space=pl.ANY),
        out_shape=jax.ShapeDtypeStruct((n_slots * ROW_WORDS, LANES), jnp.uint32),
        scratch_shapes=[pltpu.SemaphoreType.DMA],
        compiler_params=_cparams("arbitrary"),
        name="moe_dispatch",
    )(dest, h2_tiles)


def _expert_kernel(be_ref, bn_ref, first_ref, slot_ref, nxt_ref, nxt2_ref, _blk_ref,
                   x_ref, wg_hbm, wu_hbm, wd_hbm, y_ref,
                   wg_buf, wu_buf, wd_buf, wg_bf, wu_bf, wd_bf, sem, *, layer):
    mb = MOE_BLOCK
    half = wg_bf.shape[0] // 2

    def weight_copies(e, s):
        return [pltpu.make_async_copy(wg_hbm.at[layer, e], wg_buf.at[s], sem.at[s]),
                pltpu.make_async_copy(wu_hbm.at[layer, e], wu_buf.at[s], sem.at[s]),
                pltpu.make_async_copy(wd_hbm.at[layer, e], wd_buf.at[s], sem.at[s])]

    @pl.when(pl.program_id(0) == 0)
    def _():
        for cp in weight_copies(be_ref[0], 0):
            cp.start()

        @pl.when(nxt_ref[0] >= 0)
        def _():
            for cp in weight_copies(nxt_ref[0], 1):
                cp.start(priority=1)

    def block(i, row0):
        @pl.when(first_ref[i] == 1)
        def _():
            s = slot_ref[i]
            for cp in weight_copies(be_ref[i], s):
                cp.wait()

            @pl.when(nxt2_ref[i] >= 0)
            def _():
                s2 = jnp.where(s == 0, WEIGHT_SLOTS - 1, s - 1)
                for cp in weight_copies(nxt2_ref[i], s2):
                    cp.start(priority=1)

            wg_bf[...] = wg_buf[s].astype(BF16)
            wu_bf[...] = wu_buf[s].astype(BF16)
            wd_bf[...] = wd_buf[s].astype(BF16)

        def compute(m):
            live = lax.broadcasted_iota(jnp.int32, (m, 1), 0) < bn_ref[i]
            x_lo, x_hi = _load_rows(x_ref, row0, m)
            x_lo = jnp.where(live, x_lo, 0.0).astype(BF16)
            x_hi = jnp.where(live, x_hi, 0.0).astype(BF16)
            g = _dot(x_lo, wg_bf[0:half, :]) + _dot(x_hi, wg_bf[half:, :])
            u = _dot(x_lo, wu_bf[0:half, :]) + _dot(x_hi, wu_bf[half:, :])
            hid = (_silu(g) * u).astype(BF16)
            _store_rows(y_ref, _dot(hid, wd_bf[...]), first=row0)

        for parts in range(1, mb // MOE_PART + 1):
            @pl.when((bn_ref[i] > (parts - 1) * MOE_PART) & (bn_ref[i] <= parts * MOE_PART))
            def _(parts=parts):
                compute(parts * MOE_PART)

    for sb in range(MOE_STEP_BLOCKS):
        block(pl.program_id(0) * MOE_STEP_BLOCKS + sb, sb * mb)


def _experts(table, xs, w_gate, w_up, w_down, layer):
    n_blocks = table[0].shape[0]
    _, _, d, de = w_gate.shape
    shape = (MOE_STEP_BLOCKS * MOE_BLOCK * ROW_WORDS, LANES)
    hbm = pl.BlockSpec(memory_space=pl.ANY)
    return pl.pallas_call(
        functools.partial(_expert_kernel, layer=layer),
        grid_spec=pltpu.PrefetchScalarGridSpec(
            num_scalar_prefetch=len(table),
            grid=(n_blocks // MOE_STEP_BLOCKS,),
            in_specs=[pl.BlockSpec(shape, lambda i, *t: (t[-1][i], 0)), hbm, hbm, hbm],
            out_specs=pl.BlockSpec(shape, lambda i, *t: (t[-1][i], 0)),
            scratch_shapes=[pltpu.VMEM((WEIGHT_SLOTS, d, de), F32), pltpu.VMEM((WEIGHT_SLOTS, d, de), F32),
                            pltpu.VMEM((WEIGHT_SLOTS, de, d), F32),
                            pltpu.VMEM((d, de), BF16), pltpu.VMEM((d, de), BF16), pltpu.VMEM((de, d), BF16),
                            pltpu.SemaphoreType.DMA((WEIGHT_SLOTS,))]),
        out_shape=jax.ShapeDtypeStruct(xs.shape, jnp.uint32),
        compiler_params=_cparams("arbitrary"),
        name="moe_experts",
    )(*table, xs, w_gate, w_up, w_down)


def _combine_body(dest_ref, dest_next_ref, x1_ref, route_ref, m, lng_ref, lnb_ref, y_ref, o_ref, buf_ref, sem):
    tm, d = x1_ref.shape
    i = pl.program_id(0)
    n = pl.num_programs(0)
    slot = i % 2

    def gather(idx_ref, s):
        def body(r4, c):
            for u in range(DMA_UNROLL):
                r = r4 * DMA_UNROLL + u
                for k in range(2):
                    _row_copy(y_ref, idx_ref[0, 0, k * tm + r], buf_ref.at[s], k * tm + r,
                              sem.at[s]).start(priority=k)
            return c

        lax.fori_loop(0, tm // DMA_UNROLL, body, 0)

    @pl.when(i == 0)
    def _():
        gather(dest_ref, 0)

    @pl.when(i + 1 < n)
    def _():
        gather(dest_next_ref, 1 - slot)

    for k in range(2):
        pltpu.make_async_copy(y_ref.at[pl.ds(0, tm * ROW_WORDS)],
                              buf_ref.at[slot, pl.ds(0, tm * ROW_WORDS)], sem.at[slot]).wait()

    gate2 = m[:, 5 * d:6 * d]
    rt = route_ref[...]
    f = jnp.zeros((tm, d), F32)
    for k in range(2):
        lo, hi = _load_rows(buf_ref.at[slot], k * tm, tm)
        f = f + jnp.concatenate([lo, hi], axis=1) * rt[:, 2 + k:3 + k]
    o_ref[...] = _layer_norm(RES_ALPHA * x1_ref[...] + gate2 * f) * lng_ref[...] + lnb_ref[...]


def _combine_kernel(dest_ref, dest_next_ref, x1_ref, route_ref, mod_ref, lng_ref, lnb_ref, y_ref, o_ref,
                    buf_ref, sem, *, mod_row, rows_per_batch):
    m = _mod_row(mod_ref, mod_row, x1_ref.shape[0], rows_per_batch)
    _combine_body(dest_ref, dest_next_ref, x1_ref, route_ref, m, lng_ref, lnb_ref, y_ref, o_ref, buf_ref, sem)


def _combine_specs(tm, d, off, steps):
    return [pl.BlockSpec((1, 1, 2 * tm), lambda i: (off + i, 0, 0), memory_space=pltpu.SMEM),
            pl.BlockSpec((1, 1, 2 * tm), lambda i: (off + jnp.minimum(i + 1, steps - 1), 0, 0),
                         memory_space=pltpu.SMEM),
            pl.BlockSpec((tm, d), lambda i: (off + i, 0)),
            pl.BlockSpec((tm, LANES), lambda i: (off + i, 0))]


def _combine_scratch(tm):
    return [pltpu.VMEM((2, 2 * tm * ROW_WORDS, LANES), jnp.uint32), pltpu.SemaphoreType.DMA((2,))]


def _combine(dest, x1, route, mod, ln_g, ln_b, y_tiles, *, tm, row_off, rows, mod_row, rows_per_batch):
    d = x1.shape[1]
    steps = rows // tm
    kern = functools.partial(_combine_kernel, mod_row=mod_row, rows_per_batch=rows_per_batch)
    const = lambda a: pl.BlockSpec(a.shape, lambda i: (0,) * a.ndim)
    return pl.pallas_call(
        kern,
        grid=(steps,),
        in_specs=_combine_specs(tm, d, row_off // tm, steps)
                 + [const(mod), const(ln_g), const(ln_b), pl.BlockSpec(memory_space=pl.ANY)],
        out_specs=pl.BlockSpec((tm, d), lambda i: (i, 0)),
        out_shape=jax.ShapeDtypeStruct((rows, d), F32),
        scratch_shapes=_combine_scratch(tm),
        compiler_params=_cparams("arbitrary"),
        name="moe_combine",
    )(dest, dest, x1, route, mod, ln_g, ln_b, y_tiles)


def _rope_tables(n_pos):
    rows = n_pos // GRID_W
    row = jnp.repeat(jnp.arange(rows), GRID_W).astype(F32)
    col = jnp.tile(jnp.arange(GRID_W), rows).astype(F32)
    n_freq = HEAD_DIM // 4
    freq = ROPE_BASE ** (-jnp.arange(n_freq, dtype=F32) / n_freq)
    ang_r, ang_c = row[:, None] * freq, col[:, None] * freq
    cos_h = jnp.concatenate([jnp.cos(ang_r)] * 2 + [jnp.cos(ang_c)] * 2, axis=1)
    sin_h = jnp.concatenate([-jnp.sin(ang_r), jnp.sin(ang_r), -jnp.sin(ang_c), jnp.sin(ang_c)], axis=1)
    return jnp.tile(cos_h, (1, 2)), jnp.tile(sin_h, (1, 2))


def _block_table(counts, n_blocks):
    cnt = counts.astype(jnp.int32)
    padded = (cnt + MOE_BLOCK - 1) // MOE_BLOCK * MOE_BLOCK
    pad_end = jnp.cumsum(padded)
    pad_start = pad_end - padded
    blk_start = jnp.arange(n_blocks, dtype=jnp.int32)[:, None] * MOE_BLOCK
    be = jnp.minimum(jnp.sum((pad_end[None, :] <= blk_start).astype(jnp.int32), axis=1), N_EXPERTS - 1)
    ids = jnp.arange(N_EXPERTS, dtype=jnp.int32)
    mine = be[:, None] == ids[None, :]
    fill = jnp.sum(jnp.where(mine, cnt[None, :] + pad_start[None, :], 0), axis=1) - blk_start[:, 0]
    bn = jnp.clip(fill, 0, MOE_BLOCK)
    prev = jnp.concatenate([jnp.full((1,), -1, jnp.int32), be[:-1]])
    first = ((bn > 0) & (be != prev)).astype(jnp.int32)
    slot = (jnp.cumsum(first) - 1) % WEIGHT_SLOTS
    later = (ids[None, :] > ids[:, None]) & (cnt[None, :] > 0)
    nxt_e = jnp.min(jnp.where(later, ids[None, :], N_EXPERTS), axis=1)
    hop = nxt_e[:, None] == ids[None, :]
    nxt2_e = jnp.sum(jnp.where(hop, nxt_e[None, :], 0), axis=1) + jnp.where(nxt_e == N_EXPERTS, N_EXPERTS, 0)
    lookup = lambda tab: jnp.sum(jnp.where(mine, jnp.where(tab >= N_EXPERTS, -1, tab)[None, :], 0), axis=1)
    last_step = jnp.maximum(jnp.sum((bn > 0).astype(jnp.int32)) - 1, 0) // MOE_STEP_BLOCKS
    step_idx = jnp.minimum(jnp.arange(n_blocks // MOE_STEP_BLOCKS, dtype=jnp.int32), last_step)
    return (be, bn, first, slot.astype(jnp.int32), lookup(nxt_e).astype(jnp.int32),
            lookup(nxt2_e).astype(jnp.int32), step_idx.astype(jnp.int32))


def _moe(route_t, h2_tiles, w_gate, w_up, w_down, layer):
    rows = route_t.shape[1]
    tm = ROW_TILE
    n_blocks = -(-(2 * rows) // MOE_BLOCK) + N_EXPERTS
    n_blocks = -(-n_blocks // MOE_STEP_BLOCKS) * MOE_STEP_BLOCKS
    dest8, counts = _plan(route_t, tm)
    dest = dest8[:, 0:2, :].reshape(rows // tm, 1, 2 * tm)
    table = _block_table(counts[:, 0], n_blocks)
    xs = _dispatch(dest, h2_tiles, n_blocks * MOE_BLOCK, tm)
    ys = _experts(table, xs, w_gate, w_up, w_down, layer)
    tc = COMBINE_TILE
    dest_c = dest8[:, 0:2, :].reshape(rows // tm, 2, tm // tc, tc).transpose(0, 2, 1, 3).reshape(rows // tc, 1, 2 * tc)
    return dest_c, ys


def kernel(x, c, ctx, c_ctx, w_ada, b_ada, w_in, w_fourier, attn_sink, w_pool, pool_scale, w_sgu, b_sgu,
           w_out, ln1_g, ln1_b, w_router_group, w_router_expert, w_exp_gate, w_exp_up, w_exp_down,
           ln2_g, ln2_b):
    b, s, d = x.shape
    n_ctx = ctx.shape[1]
    n_layers = w_in.shape[0]
    tm = ROW_TILE
    assert n_layers == DEPTH and s == FFT_R * FFT_R and s % tm == 0 and b + 1 <= SUBLANES
    assert (b * (s + n_ctx)) % tm == 0 and n_ctx % SGU_CHUNK == 0
    cond = jnp.concatenate([c, c_ctx[None, :], jnp.zeros((SUBLANES - b - 1, d), F32)], axis=0)
    mod_all = _ada(cond, w_ada, b_ada[:, None, :])
    cos_t, sin_t = _rope_tables(s)
    x2 = x.reshape(b * s, d)
    c2 = ctx.reshape(b * n_ctx, d)
    n_sgu = w_sgu.shape[1]
    for layer in range(n_layers):
        last = layer == n_layers - 1
        mod = mod_all[layer]
        w_in_l = w_in[layer].astype(BF16)
        wf = w_fourier[layer].astype(BF16)
        w_pool_bd = jax.scipy.linalg.block_diag(*[w_pool[layer, g] for g in range(w_pool.shape[1])]).astype(BF16)
        w_sgu_stack = w_sgu[layer].reshape(n_sgu * SGU_CHUNK, SGU_CHUNK).astype(BF16)
        b_sgu_exp = jnp.repeat(b_sgu[layer].T, GROUP_W // n_sgu, axis=1)
        w_router = jnp.concatenate([w_router_expert[layer].reshape(d, N_EXPERTS), w_router_group[layer]], axis=1)
        w_router = jnp.pad(w_router, ((0, 0), (0, LANES - w_router.shape[1])))
        wr_hi = w_router.astype(BF16)
        w_route = jnp.concatenate([wr_hi, (w_router - wr_hi.astype(F32)).astype(BF16)], axis=1)
        merge_consts = (mod, w_pool_bd, pool_scale[layer][None, :], w_sgu_stack, b_sgu_exp,
                        w_out[layer].astype(BF16), ln1_g[layer][None, :], ln1_b[layer][None, :], w_route)
        sink = attn_sink[layer]

        a, q, qs, k, v, p, ug = _proj(x2, mod, w_in_l, cos_t, sin_t, mod_row=None, rows_per_batch=s,
                                      rope=True, tm=tm, a_pitch=FFT_PITCH)
        ac, qc, qsc, kc, vc, pc, ugc = _proj(c2, mod, w_in_l, cos_t, sin_t, mod_row=b, rows_per_batch=n_ctx,
                                             rope=False, tm=n_ctx, a_pitch=FFT_R)
        y_four = _fourier(a, wf, s)
        y_attn = _attention(sink, q, qs, k, v, kc, vc, seq=s, n_ctx=n_ctx, band=True)
        total = b * s + (0 if last else b * n_ctx)
        merged = _merge(x2, p, ug, y_four, y_attn, *merge_consts, (),
                        mod_row=None, seq=s, tm=tm, row_off=0, total_rows=total)
        if not last:
            yc_four = _fourier_small(ac, wf, n_ctx)
            yc_attn = _attention(sink, qc, qsc, kc, vc, kc, vc, seq=n_ctx, n_ctx=n_ctx, band=False)
            merged = _merge(c2, pc, ugc, yc_four, yc_attn, *merge_consts, tuple(merged),
                            mod_row=b, seq=n_ctx, tm=n_ctx, row_off=b * s, total_rows=total)
        x1, h2_tiles, route, route_t = merged
        dest, ys = _moe(route_t, h2_tiles, w_exp_gate, w_exp_up, w_exp_down, layer)
        ln_g, ln_b = ln2_g[layer][None, :], ln2_b[layer][None, :]
        x2 = _combine(dest, x1, route, mod, ln_g, ln_b, ys, tm=COMBINE_TILE, row_off=0, rows=b * s,
                      mod_row=None, rows_per_batch=s)
        if not last:
            c2 = _combine(dest, x1, route, mod, ln_g, ln_b, ys, tm=COMBINE_TILE, row_off=b * s,
                          rows=b * n_ctx, mod_row=b, rows_per_batch=n_ctx)
    return x2.reshape(b, s, d)
```

```python
import functools
import math

import numpy as np
import jax
import jax.numpy as jnp
from jax import lax
from jax.experimental import pallas as pl
from jax.experimental.pallas import tpu as pltpu

GRID_W = 64
HEAD_DIM = 64
GROUP_W = 256
KV_W = 128
WINDOW = 128
POOL_WINDOWS = (2, 4, 8, 16)
SGU_CHUNK = 128
N_GROUPS = 4
EXPERTS_PER_GROUP = 8
N_EXPERTS = 32
ROPE_BASE = 10000.0
LN_EPS = 1e-6
NEG_INF = -1e30
DEPTH = 2
RES_ALPHA = (2 * DEPTH) ** 0.25

LANES = 128
SUBLANES = 8
VMEM_LIMIT = 48 * 1024 * 1024

ROW_TILE = 1024
Q_BLOCK = 1024
COMBINE_TILE = 512
WEIGHT_SLOTS = 3
MOE_BLOCK = 512
MOE_PART = 256
MOE_STEP_BLOCKS = 4
FFT_R = 64
FFT_PITCH = 72
FFT_UNROLL = 16
DMA_UNROLL = 8
MERGE_PART = 512

BF16 = jnp.bfloat16
F32 = jnp.float32


def _cparams(*sem):
    return pltpu.CompilerParams(dimension_semantics=sem, vmem_limit_bytes=VMEM_LIMIT)


def _dot(a, b):
    return jnp.dot(a, b, preferred_element_type=F32)


def _dot_nt(a, b):
    return lax.dot_general(a, b, (((1,), (1,)), ((), ())), preferred_element_type=F32)


def _layer_norm(t):
    mu = jnp.mean(t, axis=-1, keepdims=True)
    d = t - mu
    var = jnp.mean(d * d, axis=-1, keepdims=True)
    return d * lax.rsqrt(var + LN_EPS)


def _silu(t):
    return t * (1.0 / (1.0 + jnp.exp(-t)))


def _gelu(t):
    return 0.5 * t * (1.0 + lax.erf(t * (1.0 / math.sqrt(2.0))))


ROW_WORDS = 4
HI_MASK = 0xFFFF0000


def _pack_rows(t):
    half = t.shape[1] // 2
    lo = lax.bitcast_convert_type(t[:, :half].astype(BF16).astype(F32), jnp.uint32)
    hi = lax.bitcast_convert_type(t[:, half:].astype(BF16).astype(F32), jnp.uint32)
    return (lo >> 16) | (hi & jnp.uint32(HI_MASK))


def _unpack_rows(w):
    return (lax.bitcast_convert_type(w << 16, F32),
            lax.bitcast_convert_type(w & jnp.uint32(HI_MASK), F32))


def _store_rows(ref, t, first=0):
    w = _pack_rows(t)
    for j in range(ROW_WORDS):
        ref[pl.ds(first * ROW_WORDS + j, t.shape[0], stride=ROW_WORDS), :] = w[:, j * LANES:(j + 1) * LANES]


def _load_rows(ref, first, m):
    w = jnp.concatenate([ref[pl.ds(first * ROW_WORDS + j, m, stride=ROW_WORDS), :] for j in range(ROW_WORDS)],
                        axis=1)
    return _unpack_rows(w)


def _ada_kernel(c_ref, w_ref, b_ref, o_ref):
    s = _silu(c_ref[...]).astype(BF16)
    o_ref[...] = _dot(s, w_ref[...].astype(BF16)) + b_ref[...]


def _ada(cond, w_ada, b_ada):
    n_layers, d, n = w_ada.shape
    tn = n // 4
    return pl.pallas_call(
        _ada_kernel,
        grid=(n_layers, n // tn),
        in_specs=[
            pl.BlockSpec((SUBLANES, d), lambda l, j: (0, 0)),
            pl.BlockSpec((None, d, tn), lambda l, j: (l, 0, j)),
            pl.BlockSpec((None, 1, tn), lambda l, j: (l, 0, j)),
        ],
        out_specs=pl.BlockSpec((None, SUBLANES, tn), lambda l, j: (l, 0, j)),
        out_shape=jax.ShapeDtypeStruct((n_layers, SUBLANES, n), F32),
        compiler_params=_cparams("arbitrary", "arbitrary"),
        name="ada",
    )(cond, w_ada, b_ada)


def _rope(t, cos_t, sin_t):
    lane = lax.broadcasted_iota(jnp.int32, t.shape, 1)
    first = (lane % 32) < 16
    partner = jnp.where(first, pltpu.roll(t, LANES - 16, axis=1), pltpu.roll(t, 16, axis=1))
    return t * cos_t + partner * sin_t


def _proj_body(x, m, w_ref, cos_ref, sin_ref, outs, *, rope, a_pitch):
    a_ref, q_ref, qs_ref, k_ref, v_ref, p_ref, ug_ref = outs
    tm, d = x.shape
    shift, scale = m[:, 0:d], m[:, d:2 * d]
    h = _layer_norm(x) * (1.0 + scale) + shift
    z = _dot(h.astype(BF16), w_ref[...])
    pad = jnp.zeros((a_pitch - FFT_R, LANES), F32)
    for g in range(tm // FFT_R):
        for hf in range(2):
            grp = z[g * FFT_R:(g + 1) * FFT_R, hf * LANES:(hf + 1) * LANES]
            if a_pitch > FFT_R:
                grp = jnp.concatenate([grp, pad], axis=0)
            a_ref[hf, g * a_pitch:(g + 1) * a_pitch, :] = grp
    q0, q1 = z[:, 256:384], z[:, 384:512]
    k = z[:, 512:640]
    if rope:
        cos_t, sin_t = cos_ref[...], sin_ref[...]
        q0, q1, k = _rope(q0, cos_t, sin_t), _rope(q1, cos_t, sin_t), _rope(k, cos_t, sin_t)
    q_ref[:, 0:128] = q0.astype(BF16)
    q_ref[:, 128:256] = q1.astype(BF16)
    qs_ref[:, 0:128] = pltpu.roll(q0, HEAD_DIM, axis=1).astype(BF16)
    qs_ref[:, 128:256] = pltpu.roll(q1, HEAD_DIM, axis=1).astype(BF16)
    k_ref[...] = k.astype(BF16)
    v_ref[...] = z[:, 640:768].astype(BF16)
    p_ref[...] = z[:, 768:1024].astype(BF16)
    ug_ref[...] = z[:, 1024:1536].astype(BF16)


def _mod_row(mod_ref, mod_row, tm, rows_per_batch):
    row = (pl.program_id(0) * tm) // rows_per_batch if mod_row is None else mod_row
    return mod_ref[pl.ds(row, 1), :]


def _proj_kernel(x_ref, mod_ref, w_ref, cos_ref, sin_ref, *outs, mod_row, rows_per_batch, rope, a_pitch):
    m = _mod_row(mod_ref, mod_row, x_ref.shape[0], rows_per_batch)
    _proj_body(x_ref[...], m, w_ref, cos_ref, sin_ref, outs, rope=rope, a_pitch=a_pitch)


def _proj_specs(rows, tm, a_pitch, seq_steps):
    row_spec = lambda w: pl.BlockSpec((tm, w), lambda i: (i, 0))
    out_w = (256, 256, 128, 128, 256, 512)
    ta = tm // FFT_R * a_pitch
    out_specs = [pl.BlockSpec((2, ta, LANES), lambda i: (0, i, 0))] + [row_spec(w) for w in out_w]
    out_shape = ([jax.ShapeDtypeStruct((2, rows // FFT_R * a_pitch, LANES), F32)]
                 + [jax.ShapeDtypeStruct((rows, w), BF16) for w in out_w])
    table_spec = pl.BlockSpec((tm, LANES), lambda i: (i % seq_steps, 0))
    return table_spec, out_specs, out_shape


def _proj(x2, mod, w_in, cos_t, sin_t, *, mod_row, rows_per_batch, rope, tm, a_pitch):
    rows, d = x2.shape
    kern = functools.partial(_proj_kernel, mod_row=mod_row, rows_per_batch=rows_per_batch,
                             rope=rope, a_pitch=a_pitch)
    table_spec, out_specs, out_shape = _proj_specs(rows, tm, a_pitch, cos_t.shape[0] // tm)
    return pl.pallas_call(
        kern,
        grid=(rows // tm,),
        in_specs=[
            pl.BlockSpec((tm, d), lambda i: (i, 0)),
            pl.BlockSpec(mod.shape, lambda i: (0, 0)),
            pl.BlockSpec(w_in.shape, lambda i: (0, 0)),
            table_spec, table_spec,
        ],
        out_specs=out_specs,
        out_shape=out_shape,
        compiler_params=_cparams("arbitrary"),
        name="proj",
    )(x2, mod, w_in, cos_t, sin_t)


def _fft_tables(n_pos):
    r = FFT_R
    assert n_pos == r * r
    kb = np.arange(r)[None, :, None]
    na = np.arange(r)[:, None, None]
    nb = np.arange(r)[None, None, :]
    ang = 2.0 * np.pi * ((kb * (na + r * nb)) % n_pos) / n_pos
    m1 = np.concatenate([np.cos(ang), -np.sin(ang)], axis=1)
    ka = np.arange(r)[:, None]
    n2 = np.arange(r)[None, :]
    ang2 = 2.0 * np.pi * ((ka * n2) % r) / r
    c2, s2 = np.cos(ang2), np.sin(ang2)
    w2 = np.block([[c2, s2], [-s2, c2]])
    return m1, w2


def _channel_tables(n_pos):
    h = HEAD_DIM
    c = np.arange(h)
    ang = 2.0 * np.pi * ((c[:, None] * c[None, :]) % h) / h
    scale = 1.0 / math.sqrt(n_pos * h)
    eye = np.eye(GROUP_W // h)
    cc = np.kron(eye, np.cos(ang)) * scale
    ss = np.kron(eye, np.sin(ang)) * scale
    return np.concatenate([cc, ss], axis=0)


def _fourier_kernel(a_ref, m1_ref, w2_ref, ch_ref, wf_ref, o_ref, z_ref, y_ref):
    r, pt = FFT_R, FFT_PITCH

    def step1(i, c):
        for u in range(FFT_UNROLL):
            na = i * FFT_UNROLL + u
            rows = jnp.concatenate([a_ref[0, pl.ds(na, r, stride=pt), :],
                                    a_ref[1, pl.ds(na, r, stride=pt), :]], axis=1)
            z = _dot(m1_ref[na], rows.astype(BF16))
            base = pl.multiple_of(na * pt, SUBLANES)
            z_ref[0, pl.ds(base, r), :] = z[0:r, 0:LANES]
            z_ref[1, pl.ds(base, r), :] = z[0:r, LANES:]
            z_ref[2, pl.ds(base, r), :] = z[r:, 0:LANES]
            z_ref[3, pl.ds(base, r), :] = z[r:, LANES:]
        return c

    lax.fori_loop(0, r // FFT_UNROLL, step1, 0)

    def step2(i, c):
        for u in range(FFT_UNROLL):
            kb = i * FFT_UNROLL + u
            q = [z_ref[j, pl.ds(kb, r, stride=pt), :] for j in range(4)]
            zs = jnp.concatenate([jnp.concatenate(q[0:2], axis=1),
                                  jnp.concatenate(q[2:4], axis=1)], axis=0)
            y = _dot(w2_ref[...], zs.astype(BF16))
            base = pl.multiple_of(kb * r, r)
            y_ref[0, pl.ds(base, r), :] = y[0:r, 0:LANES]
            y_ref[1, pl.ds(base, r), :] = y[0:r, LANES:]
            y_ref[2, pl.ds(base, r), :] = y[r:, 0:LANES]
            y_ref[3, pl.ds(base, r), :] = y[r:, LANES:]
        return c

    lax.fori_loop(0, r // FFT_UNROLL, step2, 0)

    chunk = 8 * r
    for cidx in range(r * r // chunk):
        yy = jnp.concatenate([y_ref[j, cidx * chunk:(cidx + 1) * chunk, :] for j in range(4)], axis=1)
        f = _dot(yy.astype(BF16), ch_ref[...])
        g = _dot(f.astype(BF16), wf_ref[...])
        for gi in range(chunk // r):
            kb = cidx * (chunk // r) + gi
            z_ref[0, kb * pt:kb * pt + r, :] = g[gi * r:(gi + 1) * r, 0:LANES]
            z_ref[1, kb * pt:kb * pt + r, :] = g[gi * r:(gi + 1) * r, LANES:]

    def step3(i, c):
        for u in range(FFT_UNROLL):
            ka = i * FFT_UNROLL + u
            base = pl.multiple_of(ka * r, r)
            o_ref[pl.ds(base, r), 0:LANES] = z_ref[0, pl.ds(ka, r, stride=pt), :]
            o_ref[pl.ds(base, r), LANES:] = z_ref[1, pl.ds(ka, r, stride=pt), :]
        return c

    lax.fori_loop(0, r // FFT_UNROLL, step3, 0)


def _fourier(a3, w_fourier, n_pos):
    rows = a3.shape[1] // FFT_PITCH * FFT_R
    gw = GROUP_W
    m1, w2 = _fft_tables(n_pos)
    ch = _channel_tables(n_pos)
    const = lambda shape: pl.BlockSpec(shape, lambda b: (0,) * len(shape))
    return pl.pallas_call(
        _fourier_kernel,
        grid=(rows // n_pos,),
        in_specs=[
            pl.BlockSpec((2, FFT_R * FFT_PITCH, LANES), lambda b: (0, b, 0)),
            const(m1.shape), const(w2.shape), const(ch.shape), const(w_fourier.shape),
        ],
        out_specs=pl.BlockSpec((n_pos, gw), lambda b: (b, 0)),
        out_shape=jax.ShapeDtypeStruct((rows, gw), F32),
        scratch_shapes=[pltpu.VMEM((4, FFT_R * FFT_PITCH, LANES), F32), pltpu.VMEM((4, n_pos, LANES), F32)],
        compiler_params=_cparams("arbitrary"),
        name="fourier",
    )(a3, jnp.asarray(m1, BF16), jnp.asarray(w2, BF16), jnp.asarray(ch, BF16), w_fourier)


def _fourier_small_kernel(a_ref, cs_ref, ch_ref, wf_ref, o_ref):
    n = a_ref.shape[1]
    a = jnp.concatenate([a_ref[0], a_ref[1]], axis=1)
    pq = _dot(cs_ref[...], a.astype(BF16))
    y = jnp.concatenate([pq[0:n], pq[n:2 * n]], axis=1).astype(BF16)
    f = _dot(y, ch_ref[...])
    o_ref[...] = _dot(f.astype(BF16), wf_ref[...])


def _fourier_small(a3, w_fourier, n_pos):
    _, rows, _ = a3.shape
    gw = GROUP_W
    k = np.arange(n_pos)
    ang = 2.0 * np.pi * ((k[:, None] * k[None, :]) % n_pos) / n_pos
    cs = np.concatenate([np.cos(ang), -np.sin(ang)], axis=0)
    ch = _channel_tables(n_pos)
    const = lambda shape: pl.BlockSpec(shape, lambda b: (0,) * len(shape))
    return pl.pallas_call(
        _fourier_small_kernel,
        grid=(rows // n_pos,),
        in_specs=[pl.BlockSpec((2, n_pos, LANES), lambda b: (0, b, 0)),
                  const(cs.shape), const(ch.shape), const(w_fourier.shape)],
        out_specs=pl.BlockSpec((n_pos, gw), lambda b: (b, 0)),
        out_shape=jax.ShapeDtypeStruct((rows, gw), F32),
        compiler_params=_cparams("arbitrary"),
        name="fourier_ctx",
    )(a3, jnp.asarray(cs, BF16), jnp.asarray(ch, BF16), w_fourier)


ATTN_SUB = 128


def _attn_kernel(sink_ref, q_ref, qs_ref, k_ref, v_ref, kc_ref, vc_ref, o_ref, *, band, seq):
    qb = q_ref.shape[0]
    sub = ATTN_SUB
    lane = lax.broadcasted_iota(jnp.int32, (1, LANES), 1)
    lo_half = lane < HEAD_DIM
    zero = jnp.zeros((), BF16)
    scale = jnp.asarray(HEAD_DIM ** -0.5, BF16)
    kw = sub + 2 * WINDOW
    for sb in range(qb // sub):
        rows = slice(sb * sub, (sb + 1) * sub)
        qa0, qa1 = q_ref[rows, 0:LANES], q_ref[rows, LANES:]
        qs0, qs1 = qs_ref[rows, 0:LANES], qs_ref[rows, LANES:]
        q_all = jnp.concatenate([jnp.where(lo_half, qa0, zero), jnp.where(lo_half, qs0, zero),
                                 jnp.where(lo_half, zero, qs1), jnp.where(lo_half, zero, qa1)], axis=0) * scale
        if band:
            p0 = pl.program_id(1) * qb + sb * sub
            start = pl.multiple_of(jnp.clip(p0 - WINDOW, 0, seq - kw), WINDOW)
            qpos = p0 + lax.broadcasted_iota(jnp.int32, (sub, 1), 0)
            kpos = start + lax.broadcasted_iota(jnp.int32, (1, kw), 1)
            bias = jnp.where(jnp.abs(qpos - kpos) <= WINDOW, 0.0, NEG_INF)
            keys = jnp.concatenate([k_ref[pl.ds(start, kw), :], kc_ref[...]], axis=0)
            vals = jnp.concatenate([v_ref[pl.ds(start, kw), :], vc_ref[...]], axis=0)
        else:
            keys, vals = kc_ref[...], vc_ref[...]
        s_all = _dot_nt(q_all, keys)
        probs, dens = [], []
        for h in range(4):
            s = s_all[h * sub:(h + 1) * sub, :]
            sink = sink_ref[h]
            if band:
                s = jnp.concatenate([s[:, 0:kw] + bias, s[:, kw:]], axis=1)
            m = jnp.maximum(jnp.max(s, axis=1, keepdims=True), sink)
            p = jnp.exp(s - m)
            dens.append(jnp.sum(p, axis=1, keepdims=True) + jnp.exp(sink - m))
            probs.append(p.astype(BF16))
        o_all = _dot(jnp.concatenate(probs, axis=0), vals)
        o = [o_all[h * sub:(h + 1) * sub, :] / dens[h] for h in range(4)]
        o_ref[rows, 0:LANES] = jnp.where(lo_half, o[0], pltpu.roll(o[1], HEAD_DIM, axis=1)).astype(BF16)
        o_ref[rows, LANES:] = jnp.where(lo_half, pltpu.roll(o[2], HEAD_DIM, axis=1), o[3]).astype(BF16)


def _attention(sink, q, qs, k, v, kc, vc, *, seq, n_ctx, band):
    rows = q.shape[0]
    n_batch = rows // seq
    qb = Q_BLOCK if band else seq
    steps = seq // qb
    kern = functools.partial(_attn_kernel, band=band, seq=seq)
    seq_spec = pl.BlockSpec((seq, KV_W), lambda b, i: (b, 0))
    ctx_spec = pl.BlockSpec((n_ctx, KV_W), lambda b, i: (b, 0))
    q_spec = pl.BlockSpec((qb, GROUP_W), lambda b, i: (b * steps + i, 0))
    return pl.pallas_call(
        kern,
        grid=(n_batch, steps),
        in_specs=[pl.BlockSpec(memory_space=pltpu.SMEM), q_spec, q_spec,
                  seq_spec, seq_spec, ctx_spec, ctx_spec],
        out_specs=q_spec,
        out_shape=jax.ShapeDtypeStruct((rows, GROUP_W), BF16),
        compiler_params=_cparams("arbitrary", "arbitrary"),
        name="attn" if band else "attn_ctx",
    )(sink, q, qs, k, v, kc, vc)


POOL_HALO = max(POOL_WINDOWS) // 2


def _pool(p_ref, t0, tm, seq):
    halo = POOL_HALO
    pack = 2 * SUBLANES
    t0 = pl.multiple_of(t0, pack)
    main = p_ref[pl.ds(t0, tm), :].astype(F32)
    lo = pl.multiple_of(jnp.maximum(t0 - pack, 0), pack)
    hi = pl.multiple_of(jnp.minimum(t0 + tm, seq - pack), pack)
    prev = p_ref[pl.ds(lo, pack), :].astype(F32)[pack - halo:, :]
    nxt = p_ref[pl.ds(hi, pack), :].astype(F32)[:halo, :]
    prev = jnp.where(t0 > 0, prev, 0.0)
    nxt = jnp.where(t0 + tm < seq, nxt, 0.0)
    full = jnp.concatenate([prev, main, nxt], axis=0)
    n = tm + 2 * halo
    gch = GROUP_W // len(POOL_WINDOWS)
    first = lax.broadcasted_iota(jnp.int32, (1, LANES), 1) < gch
    means = []
    for hf in range(GROUP_W // LANES):
        wa, wb = POOL_WINDOWS[2 * hf], POOL_WINDOWS[2 * hf + 1]
        x = full[:, hf * LANES:(hf + 1) * LANES]
        sums, w, s = {}, 2, pltpu.roll(x, 1, axis=0) + x
        sums[w] = s
        while w < wb:
            s = pltpu.roll(s, w // 2, axis=0) + pltpu.roll(s, n - w // 2, axis=0)
            w *= 2
            sums[w] = s
        means.append(jnp.where(first, sums[wa] * (1.0 / wa), sums[wb] * (1.0 / wb))[halo:halo + tm, :])
    mean = jnp.concatenate(means, axis=1)
    win = jnp.concatenate([jnp.full((1, gch), w, jnp.int32) for w in POOL_WINDOWS], axis=1)

    def rescale(rows, first_pos):
        pos = first_pos + lax.broadcasted_iota(jnp.int32, (halo, 1), 0)
        cnt = jnp.minimum(pos + win // 2, seq) - jnp.maximum(pos - win // 2, 0)
        return rows * (win.astype(F32) / cnt.astype(F32))

    mean = jnp.concatenate([rescale(mean[:halo], t0), mean[halo:tm - halo],
                            rescale(mean[tm - halo:], t0 + tm - halo)], axis=0)
    return mean - main


def _route(logits):
    tm = logits.shape[0]
    lt = logits.T
    gl = lt[N_EXPERTS:N_EXPERTS + N_GROUPS]
    sub_g = lax.broadcasted_iota(jnp.int32, gl.shape, 0)
    gmax = jnp.max(gl, axis=0, keepdims=True)
    grp = jnp.min(jnp.where(gl == gmax, sub_g, N_GROUPS), axis=0, keepdims=True)
    gate_group = 1.0 / jnp.sum(jnp.exp(gl - gmax), axis=0, keepdims=True)
    el = lt[0:EXPERTS_PER_GROUP]
    for g in range(1, N_GROUPS):
        el = jnp.where(grp == g, lt[g * EXPERTS_PER_GROUP:(g + 1) * EXPERTS_PER_GROUP], el)
    sub = lax.broadcasted_iota(jnp.int32, el.shape, 0)
    m1 = jnp.max(el, axis=0, keepdims=True)
    i1 = jnp.min(jnp.where(el == m1, sub, EXPERTS_PER_GROUP), axis=0, keepdims=True)
    el2 = jnp.where(sub == i1, -jnp.inf, el)
    m2 = jnp.max(el2, axis=0, keepdims=True)
    i2 = jnp.min(jnp.where(el2 == m2, sub, EXPERTS_PER_GROUP), axis=0, keepdims=True)
    r = jnp.exp(m2 - m1)
    g1 = gate_group / (1.0 + r)
    g2 = g1 * r
    e1 = (grp * EXPERTS_PER_GROUP + i1).astype(F32)
    e2 = (grp * EXPERTS_PER_GROUP + i2).astype(F32)
    rows = jnp.where(sub == 0, e1, jnp.where(sub == 1, e2, jnp.where(sub == 2, g1, jnp.where(sub == 3, g2, 0.0))))
    cols = jnp.concatenate([rows, jnp.zeros((LANES - rows.shape[0], tm), F32)], axis=0).T
    return cols, rows


def _merge_kernel(x_ref, p_ref, ug_ref, yf_ref, ya_ref, mod_ref, wpool_ref, pscale_ref, wsgu_ref, bsgu_ref,
                  wout_ref, lng_ref, lnb_ref, wr_ref, *rest,
                  mod_row, seq, n_alias):
    x1_ref, h2_ref, route_ref, route_t_ref = rest[n_alias:]
    tm, d = x_ref.shape
    if mod_row is None:
        row = pl.program_id(0)
    else:
        row = mod_row
    t0 = pl.multiple_of(pl.program_id(1) * tm, tm)
    m = mod_ref[pl.ds(row, 1), :]
    gate1, shift2, scale2 = m[:, 2 * d:3 * d], m[:, 3 * d:4 * d], m[:, 4 * d:5 * d]
    lane = lax.broadcasted_iota(jnp.int32, (1, GROUP_W), 1)
    n_heads = wsgu_ref.shape[0] // SGU_CHUNK
    head = lane // (GROUP_W // n_heads)

    pm = min(tm, MERGE_PART)
    for part in range(tm // pm):
        r0 = part * pm
        rows = slice(r0, r0 + pm)
        pooled = _pool(p_ref, t0 + r0, pm, seq)
        y_pool = _dot(pooled.astype(BF16), wpool_ref[...]) * pscale_ref[...]

        ug = ug_ref[rows, :].astype(F32)
        u = _gelu(ug[:, 0:GROUP_W])
        v = _layer_norm(_gelu(ug[:, GROUP_W:])).astype(BF16)
        mixed = []
        for cidx in range(pm // SGU_CHUNK):
            vc = v[cidx * SGU_CHUNK:(cidx + 1) * SGU_CHUNK, :]
            full = _dot(wsgu_ref[...], vc)
            mc = bsgu_ref[...]
            for hd in range(n_heads):
                mc = mc + jnp.where(head == hd, full[hd * SGU_CHUNK:(hd + 1) * SGU_CHUNK, :], 0.0)
            mixed.append(mc)
        y_sgu = u * jnp.concatenate(mixed, axis=0)

        cat = jnp.concatenate([yf_ref[rows, :].astype(BF16), ya_ref[rows, :], y_pool.astype(BF16),
                               y_sgu.astype(BF16)], axis=1)
        y = _dot(cat, wout_ref[...])
        x1 = _layer_norm(RES_ALPHA * x_ref[rows, :] + gate1 * y) * lng_ref[...] + lnb_ref[...]
        x1_ref[rows, :] = x1
        h2 = _layer_norm(x1) * (1.0 + scale2) + shift2
        _store_rows(h2_ref, h2, first=r0)
        lg = _dot(h2.astype(BF16), wr_ref[...])
        route_ref[rows, :], route_t_ref[:, rows] = _route(lg[:, 0:LANES] + lg[:, LANES:])


def _merge(x2, p, ug, y_four, y_attn, mod, w_pool_bd, pool_scale, w_sgu_stack, b_sgu_exp, w_out,
           ln_g, ln_b, w_route, aliased, *, mod_row, seq, tm, row_off, total_rows):
    rows, d = x2.shape
    n_batch, steps = rows // seq, seq // tm
    off = row_off // tm
    kern = functools.partial(_merge_kernel, mod_row=mod_row, seq=seq, n_alias=len(aliased))
    row_spec = lambda w: pl.BlockSpec((tm, w), lambda b, i: (b * steps + i, 0))
    const = lambda a: pl.BlockSpec(a.shape, lambda b, i: (0,) * a.ndim)
    consts = (mod, w_pool_bd, pool_scale, w_sgu_stack, b_sgu_exp, w_out, ln_g, ln_b, w_route)
    n_in = 5 + len(consts)
    out_shapes = [jax.ShapeDtypeStruct((total_rows, d), F32),
                  jax.ShapeDtypeStruct((total_rows * ROW_WORDS, LANES), jnp.uint32),
                  jax.ShapeDtypeStruct((total_rows, LANES), F32),
                  jax.ShapeDtypeStruct((SUBLANES, total_rows), F32)]
    out_specs = [pl.BlockSpec((tm, d), lambda b, i: (off + b * steps + i, 0)),
                 pl.BlockSpec((tm * ROW_WORDS, LANES), lambda b, i: (off + b * steps + i, 0)),
                 pl.BlockSpec((tm, LANES), lambda b, i: (off + b * steps + i, 0)),
                 pl.BlockSpec((SUBLANES, tm), lambda b, i: (0, off + b * steps + i))]
    return pl.pallas_call(
        kern,
        grid=(n_batch, steps),
        in_specs=[row_spec(d), pl.BlockSpec((seq, GROUP_W), lambda b, i: (b, 0)),
                  row_spec(2 * GROUP_W), row_spec(GROUP_W), row_spec(GROUP_W)]
                 + [const(a) for a in consts]
                 + [pl.BlockSpec(memory_space=pl.ANY)] * len(aliased),
        out_specs=out_specs,
        out_shape=out_shapes,
        input_output_aliases={n_in + k: k for k in range(len(aliased))},
        compiler_params=_cparams("arbitrary", "arbitrary"),
        name="merge",
    )(x2, p, ug, y_four, y_attn, *consts, *aliased)


def _plan_kernel(route_ref, dest_ref, cnt_out_ref, cnt_ref, start_ref, carry_ref):
    ph, t = pl.program_id(0), pl.program_id(1)
    tm = route_ref.shape[1]
    rt = route_ref[...]
    e1 = rt[0:1, :].astype(jnp.int32)
    e2 = rt[1:2, :].astype(jnp.int32)
    sub = lax.broadcasted_iota(jnp.int32, (N_EXPERTS, tm), 0)
    hit1, hit2 = sub == e1, sub == e2
    onehot = jnp.where(hit1 | hit2, 1.0, 0.0)
    tile_cnt = jnp.sum(onehot, axis=1, keepdims=True)

    @pl.when((ph == 0) & (t == 0))
    def _():
        cnt_ref[...] = jnp.zeros_like(cnt_ref)

    @pl.when(ph == 0)
    def _():
        cnt_ref[...] += tile_cnt

    @pl.when((ph == 1) & (t == 0))
    def _():
        cnt = cnt_ref[...]
        padded = jnp.floor((cnt + (MOE_BLOCK - 1.0)) * (1.0 / MOE_BLOCK)) * MOE_BLOCK
        row = lax.broadcasted_iota(jnp.int32, cnt.shape, 0)
        incl = padded
        sh = 1
        while sh < N_EXPERTS:
            incl = incl + jnp.where(row >= sh, pltpu.roll(incl, sh, axis=0), 0.0)
            sh *= 2
        start_ref[...] = incl - padded
        carry_ref[...] = jnp.zeros_like(carry_ref)
        cnt_out_ref[...] = cnt

    @pl.when(ph == 1)
    def _():
        r_i = lax.broadcasted_iota(jnp.int32, (tm, tm), 0)
        c_i = lax.broadcasted_iota(jnp.int32, (tm, tm), 1)
        before = jnp.where(r_i < c_i, 1.0, 0.0).astype(BF16)
        rank = _dot(onehot.astype(BF16), before)
        base = start_ref[:, 0:1] + carry_ref[:, 0:1] + rank
        d1 = jnp.sum(jnp.where(hit1, base, 0.0), axis=0, keepdims=True)
        d2 = jnp.sum(jnp.where(hit2, base, 0.0), axis=0, keepdims=True)
        sub8 = lax.broadcasted_iota(jnp.int32, (SUBLANES, tm), 0)
        dest_ref[...] = jnp.where(sub8 == 0, d1, d2).astype(jnp.int32)
        carry_ref[...] += tile_cnt


def _plan(route_t, tm):
    rows = route_t.shape[1]
    n_t = rows // tm
    return pl.pallas_call(
        _plan_kernel,
        grid=(2, n_t),
        in_specs=[pl.BlockSpec((SUBLANES, tm), lambda ph, t: (0, t))],
        out_specs=[pl.BlockSpec((None, SUBLANES, tm), lambda ph, t: (t * ph, 0, 0)),
                   pl.BlockSpec((N_EXPERTS, LANES), lambda ph, t: (0, 0))],
        out_shape=[jax.ShapeDtypeStruct((n_t, SUBLANES, tm), jnp.int32),
                   jax.ShapeDtypeStruct((N_EXPERTS, LANES), F32)],
        scratch_shapes=[pltpu.VMEM((N_EXPERTS, LANES), F32)] * 3,
        compiler_params=_cparams("arbitrary", "arbitrary"),
        name="moe_plan",
    )(route_t)


def _row_copy(src_ref, src_row, dst_ref, dst_row, sem):
    return pltpu.make_async_copy(
        src_ref.at[pl.ds(pl.multiple_of(src_row * ROW_WORDS, ROW_WORDS), ROW_WORDS)],
        dst_ref.at[pl.ds(pl.multiple_of(dst_row * ROW_WORDS, ROW_WORDS), ROW_WORDS)], sem)


def _dispatch_kernel(dest_ref, h2_ref, xs_ref, sem, *, tm):
    def body(r4, c):
        for u in range(DMA_UNROLL):
            r = r4 * DMA_UNROLL + u
            for k in range(2):
                _row_copy(h2_ref, r, xs_ref, dest_ref[0, 0, k * tm + r], sem).start(priority=k)
        return c

    lax.fori_loop(0, tm // DMA_UNROLL, body, 0)
    for k in range(2):
        pltpu.make_async_copy(h2_ref, xs_ref.at[pl.ds(0, tm * ROW_WORDS)], sem).wait()


def _dispatch(dest, h2_tiles, n_slots, tm):
    n_t = dest.shape[0]
    return pl.pallas_call(
        functools.partial(_dispatch_kernel, tm=tm),
        grid=(n_t,),
        in_specs=[pl.BlockSpec((1, 1, 2 * tm), lambda i: (i, 0, 0), memory_space=pltpu.SMEM),
                  pl.BlockSpec((tm * ROW_WORDS, LANES), lambda i: (i, 0))],
        out_specs=pl.BlockSpec(memory_space=pl.ANY),
        out_shape=jax.ShapeDtypeStruct((n_slots * ROW_WORDS, LANES), jnp.uint32),
        scratch_shapes=[pltpu.SemaphoreType.DMA],
        compiler_params=_cparams("arbitrary"),
        name="moe_dispatch",
    )(dest, h2_tiles)


def _expert_kernel(be_ref, bn_ref, first_ref, slot_ref, nxt_ref, nxt2_ref, _blk_ref,
                   x_ref, wg_hbm, wu_hbm, wd_hbm, y_ref,
                   wg_buf, wu_buf, wd_buf, wg_bf, wu_bf, wd_bf, sem, *, layer):
    mb = MOE_BLOCK
    half = wg_bf.shape[0] // 2

    def weight_copies(e, s):
        return [pltpu.make_async_copy(wg_hbm.at[layer, e], wg_buf.at[s], sem.at[s]),
                pltpu.make_async_copy(wu_hbm.at[layer, e], wu_buf.at[s], sem.at[s]),
                pltpu.make_async_copy(wd_hbm.at[layer, e], wd_buf.at[s], sem.at[s])]

    @pl.when(pl.program_id(0) == 0)
    def _():
        for cp in weight_copies(be_ref[0], 0):
            cp.start()

        @pl.when(nxt_ref[0] >= 0)
        def _():
            for cp in weight_copies(nxt_ref[0], 1):
                cp.start(priority=1)

    def block(i, row0):
        @pl.when(first_ref[i] == 1)
        def _():
            s = slot_ref[i]
            for cp in weight_copies(be_ref[i], s):
                cp.wait()

            @pl.when(nxt2_ref[i] >= 0)
            def _():
                s2 = jnp.where(s == 0, WEIGHT_SLOTS - 1, s - 1)
                for cp in weight_copies(nxt2_ref[i], s2):
                    cp.start(priority=1)

            wg_bf[...] = wg_buf[s].astype(BF16)
            wu_bf[...] = wu_buf[s].astype(BF16)
            wd_bf[...] = wd_buf[s].astype(BF16)

        def compute(m):
            live = lax.broadcasted_iota(jnp.int32, (m, 1), 0) < bn_ref[i]
            x_lo, x_hi = _load_rows(x_ref, row0, m)
            x_lo = jnp.where(live, x_lo, 0.0).astype(BF16)
            x_hi = jnp.where(live, x_hi, 0.0).astype(BF16)
            g = _dot(x_lo, wg_bf[0:half, :]) + _dot(x_hi, wg_bf[half:, :])
            u = _dot(x_lo, wu_bf[0:half, :]) + _dot(x_hi, wu_bf[half:, :])
            hid = (_silu(g) * u).astype(BF16)
            _store_rows(y_ref, _dot(hid, wd_bf[...]), first=row0)

        for parts in range(1, mb // MOE_PART + 1):
            @pl.when((bn_ref[i] > (parts - 1) * MOE_PART) & (bn_ref[i] <= parts * MOE_PART))
            def _(parts=parts):
                compute(parts * MOE_PART)

    for sb in range(MOE_STEP_BLOCKS):
        block(pl.program_id(0) * MOE_STEP_BLOCKS + sb, sb * mb)


def _experts(table, xs, w_gate, w_up, w_down, layer):
    n_blocks = table[0].shape[0]
    _, _, d, de = w_gate.shape
    shape = (MOE_STEP_BLOCKS * MOE_BLOCK * ROW_WORDS, LANES)
    hbm = pl.BlockSpec(memory_space=pl.ANY)
    return pl.pallas_call(
        functools.partial(_expert_kernel, layer=layer),
        grid_spec=pltpu.PrefetchScalarGridSpec(
            num_scalar_prefetch=len(table),
            grid=(n_blocks // MOE_STEP_BLOCKS,),
            in_specs=[pl.BlockSpec(shape, lambda i, *t: (t[-1][i], 0)), hbm, hbm, hbm],
            out_specs=pl.BlockSpec(shape, lambda i, *t: (t[-1][i], 0)),
            scratch_shapes=[pltpu.VMEM((WEIGHT_SLOTS, d, de), F32), pltpu.VMEM((WEIGHT_SLOTS, d, de), F32),
                            pltpu.VMEM((WEIGHT_SLOTS, de, d), F32),
                            pltpu.VMEM((d, de), BF16), pltpu.VMEM((d, de), BF16), pltpu.VMEM((de, d), BF16),
                            pltpu.SemaphoreType.DMA((WEIGHT_SLOTS,))]),
        out_shape=jax.ShapeDtypeStruct(xs.shape, jnp.uint32),
        compiler_params=_cparams("arbitrary"),
        name="moe_experts",
    )(*table, xs, w_gate, w_up, w_down)


def _combine_kernel(dest_ref, dest_next_ref, x1_ref, route_ref, mod_ref, lng_ref, lnb_ref, y_ref, o_ref,
                    buf_ref, sem, *, mod_row, rows_per_batch):
    tc = COMBINE_TILE
    d = x1_ref.shape[1]
    i = pl.program_id(0)
    n = pl.num_programs(0)
    m = _mod_row(mod_ref, mod_row, x1_ref.shape[0], rows_per_batch)
    gate2 = m[:, 5 * d:6 * d]

    def gather(idx_ref, blk, s):
        def body(r4, c):
            for u in range(DMA_UNROLL):
                r = r4 * DMA_UNROLL + u
                for k in range(2):
                    _row_copy(y_ref, idx_ref[blk, 0, k * tc + r], buf_ref.at[s], k * tc + r,
                              sem.at[s]).start(priority=k)
            return c

        lax.fori_loop(0, tc // DMA_UNROLL, body, 0)

    def finish(hf):
        for k in range(2):
            pltpu.make_async_copy(y_ref.at[pl.ds(0, tc * ROW_WORDS)],
                                  buf_ref.at[hf, pl.ds(0, tc * ROW_WORDS)], sem.at[hf]).wait()
        rows = slice(hf * tc, (hf + 1) * tc)
        rt = route_ref[rows, :]
        f = jnp.zeros((tc, d), F32)
        for k in range(2):
            lo, hi = _load_rows(buf_ref.at[hf], k * tc, tc)
            f = f + jnp.concatenate([lo, hi], axis=1) * rt[:, 2 + k:3 + k]
        o_ref[rows, :] = _layer_norm(RES_ALPHA * x1_ref[rows, :] + gate2 * f) * lng_ref[...] + lnb_ref[...]

    @pl.when(i == 0)
    def _():
        gather(dest_ref, 0, 0)

    gather(dest_ref, 1, 1)
    finish(0)

    @pl.when(i + 1 < n)
    def _():
        gather(dest_next_ref, 0, 0)

    finish(1)


def _combine(dest, x1, route, mod, ln_g, ln_b, y_tiles, *, row_off, rows, mod_row, rows_per_batch):
    d = x1.shape[1]
    tc = COMBINE_TILE
    tm = 2 * tc
    steps = rows // tm
    off = row_off // tm
    kern = functools.partial(_combine_kernel, mod_row=mod_row, rows_per_batch=rows_per_batch)
    const = lambda a: pl.BlockSpec(a.shape, lambda i: (0,) * a.ndim)
    return pl.pallas_call(
        kern,
        grid=(steps,),
        in_specs=[pl.BlockSpec((2, 1, 2 * tc), lambda i: (off + i, 0, 0), memory_space=pltpu.SMEM),
                  pl.BlockSpec((1, 1, 2 * tc), lambda i: (2 * (off + jnp.minimum(i + 1, steps - 1)), 0, 0),
                               memory_space=pltpu.SMEM),
                  pl.BlockSpec((tm, d), lambda i: (off + i, 0)),
                  pl.BlockSpec((tm, LANES), lambda i: (off + i, 0)),
                  const(mod), const(ln_g), const(ln_b), pl.BlockSpec(memory_space=pl.ANY)],
        out_specs=pl.BlockSpec((tm, d), lambda i: (i, 0)),
        out_shape=jax.ShapeDtypeStruct((rows, d), F32),
        scratch_shapes=[pltpu.VMEM((2, 2 * tc * ROW_WORDS, LANES), jnp.uint32), pltpu.SemaphoreType.DMA((2,))],
        compiler_params=_cparams("arbitrary"),
        name="moe_combine",
    )(dest, dest, x1, route, mod, ln_g, ln_b, y_tiles)


def _rope_tables(n_pos):
    rows = n_pos // GRID_W
    row = jnp.repeat(jnp.arange(rows), GRID_W).astype(F32)
    col = jnp.tile(jnp.arange(GRID_W), rows).astype(F32)
    n_freq = HEAD_DIM // 4
    freq = ROPE_BASE ** (-jnp.arange(n_freq, dtype=F32) / n_freq)
    ang_r, ang_c = row[:, None] * freq, col[:, None] * freq
    cos_h = jnp.concatenate([jnp.cos(ang_r)] * 2 + [jnp.cos(ang_c)] * 2, axis=1)
    sin_h = jnp.concatenate([-jnp.sin(ang_r), jnp.sin(ang_r), -jnp.sin(ang_c), jnp.sin(ang_c)], axis=1)
    return jnp.tile(cos_h, (1, 2)), jnp.tile(sin_h, (1, 2))


def _block_table(counts, n_blocks):
    cnt = counts.astype(jnp.int32)
    padded = (cnt + MOE_BLOCK - 1) // MOE_BLOCK * MOE_BLOCK
    pad_end = jnp.cumsum(padded)
    pad_start = pad_end - padded
    blk_start = jnp.arange(n_blocks, dtype=jnp.int32)[:, None] * MOE_BLOCK
    be = jnp.minimum(jnp.sum((pad_end[None, :] <= blk_start).astype(jnp.int32), axis=1), N_EXPERTS - 1)
    ids = jnp.arange(N_EXPERTS, dtype=jnp.int32)
    mine = be[:, None] == ids[None, :]
    fill = jnp.sum(jnp.where(mine, cnt[None, :] + pad_start[None, :], 0), axis=1) - blk_start[:, 0]
    bn = jnp.clip(fill, 0, MOE_BLOCK)
    prev = jnp.concatenate([jnp.full((1,), -1, jnp.int32), be[:-1]])
    first = ((bn > 0) & (be != prev)).astype(jnp.int32)
    slot = (jnp.cumsum(first) - 1) % WEIGHT_SLOTS
    later = (ids[None, :] > ids[:, None]) & (cnt[None, :] > 0)
    nxt_e = jnp.min(jnp.where(later, ids[None, :], N_EXPERTS), axis=1)
    hop = nxt_e[:, None] == ids[None, :]
    nxt2_e = jnp.sum(jnp.where(hop, nxt_e[None, :], 0), axis=1) + jnp.where(nxt_e == N_EXPERTS, N_EXPERTS, 0)
    lookup = lambda tab: jnp.sum(jnp.where(mine, jnp.where(tab >= N_EXPERTS, -1, tab)[None, :], 0), axis=1)
    last_step = jnp.maximum(jnp.sum((bn > 0).astype(jnp.int32)) - 1, 0) // MOE_STEP_BLOCKS
    step_idx = jnp.minimum(jnp.arange(n_blocks // MOE_STEP_BLOCKS, dtype=jnp.int32), last_step)
    return (be, bn, first, slot.astype(jnp.int32), lookup(nxt_e).astype(jnp.int32),
            lookup(nxt2_e).astype(jnp.int32), step_idx.astype(jnp.int32))


def _moe(route_t, h2_tiles, w_gate, w_up, w_down, layer):
    rows = route_t.shape[1]
    tm = ROW_TILE
    n_blocks = -(-(2 * rows) // MOE_BLOCK) + N_EXPERTS
    n_blocks = -(-n_blocks // MOE_STEP_BLOCKS) * MOE_STEP_BLOCKS
    dest8, counts = _plan(route_t, tm)
    dest = dest8[:, 0:2, :].reshape(rows // tm, 1, 2 * tm)
    table = _block_table(counts[:, 0], n_blocks)
    xs = _dispatch(dest, h2_tiles, n_blocks * MOE_BLOCK, tm)
    ys = _experts(table, xs, w_gate, w_up, w_down, layer)
    tc = COMBINE_TILE
    dest_c = dest8[:, 0:2, :].reshape(rows // tm, 2, tm // tc, tc).transpose(0, 2, 1, 3).reshape(rows // tc, 1, 2 * tc)
    return dest_c, ys


def kernel(x, c, ctx, c_ctx, w_ada, b_ada, w_in, w_fourier, attn_sink, w_pool, pool_scale, w_sgu, b_sgu,
           w_out, ln1_g, ln1_b, w_router_group, w_router_expert, w_exp_gate, w_exp_up, w_exp_down,
           ln2_g, ln2_b):
    b, s, d = x.shape
    n_ctx = ctx.shape[1]
    n_layers = w_in.shape[0]
    tm = ROW_TILE
    assert n_layers == DEPTH and s == FFT_R * FFT_R and s % tm == 0 and b + 1 <= SUBLANES
    assert (b * (s + n_ctx)) % tm == 0 and n_ctx % SGU_CHUNK == 0
    cond = jnp.concatenate([c, c_ctx[None, :], jnp.zeros((SUBLANES - b - 1, d), F32)], axis=0)
    mod_all = _ada(cond, w_ada, b_ada[:, None, :])
    cos_t, sin_t = _rope_tables(s)
    x2 = x.reshape(b * s, d)
    c2 = ctx.reshape(b * n_ctx, d)
    n_sgu = w_sgu.shape[1]
    for layer in range(n_layers):
        last = layer == n_layers - 1
        mod = mod_all[layer]
        w_in_l = w_in[layer].astype(BF16)
        wf = w_fourier[layer].astype(BF16)
        w_pool_bd = jax.scipy.linalg.block_diag(*[w_pool[layer, g] for g in range(w_pool.shape[1])]).astype(BF16)
        w_sgu_stack = w_sgu[layer].reshape(n_sgu * SGU_CHUNK, SGU_CHUNK).astype(BF16)
        b_sgu_exp = jnp.repeat(b_sgu[layer].T, GROUP_W // n_sgu, axis=1)
        w_router = jnp.concatenate([w_router_expert[layer].reshape(d, N_EXPERTS), w_router_group[layer]], axis=1)
        w_router = jnp.pad(w_router, ((0, 0), (0, LANES - w_router.shape[1])))
        wr_hi = w_router.astype(BF16)
        w_route = jnp.concatenate([wr_hi, (w_router - wr_hi.astype(F32)).astype(BF16)], axis=1)
        merge_consts = (mod, w_pool_bd, pool_scale[layer][None, :], w_sgu_stack, b_sgu_exp,
                        w_out[layer].astype(BF16), ln1_g[layer][None, :], ln1_b[layer][None, :], w_route)
        sink = attn_sink[layer]

        a, q, qs, k, v, p, ug = _proj(x2, mod, w_in_l, cos_t, sin_t, mod_row=None, rows_per_batch=s,
                                      rope=True, tm=tm, a_pitch=FFT_PITCH)
        ac, qc, qsc, kc, vc, pc, ugc = _proj(c2, mod, w_in_l, cos_t, sin_t, mod_row=b, rows_per_batch=n_ctx,
                                             rope=False, tm=n_ctx, a_pitch=FFT_R)
        y_four = _fourier(a, wf, s)
        y_attn = _attention(sink, q, qs, k, v, kc, vc, seq=s, n_ctx=n_ctx, band=True)
        total = b * s + (0 if last else b * n_ctx)
        merged = _merge(x2, p, ug, y_four, y_attn, *merge_consts, (),
                        mod_row=None, seq=s, tm=tm, row_off=0, total_rows=total)
        if not last:
            yc_four = _fourier_small(ac, wf, n_ctx)
            yc_attn = _attention(sink, qc, qsc, kc, vc, kc, vc, seq=n_ctx, n_ctx=n_ctx, band=False)
            merged = _merge(c2, pc, ugc, yc_four, yc_attn, *merge_consts, tuple(merged),
                            mod_row=b, seq=n_ctx, tm=n_ctx, row_off=b * s, total_rows=total)
        x1, h2_tiles, route, route_t = merged
        dest, ys = _moe(route_t, h2_tiles, w_exp_gate, w_exp_up, w_exp_down, layer)
        ln_g, ln_b = ln2_g[layer][None, :], ln2_b[layer][None, :]
        x2 = _combine(dest, x1, route, mod, ln_g, ln_b, ys, row_off=0, rows=b * s,
                      mod_row=None, rows_per_batch=s)
        if not last:
            c2 = _combine(dest, x1, route, mod, ln_g, ln_b, ys, row_off=b * s,
                          rows=b * n_ctx, mod_row=b, rows_per_batch=n_ctx)
    return x2.reshape(b, s, d)
```

```python
import functools
import math

import numpy as np
import jax
import jax.numpy as jnp
from jax import lax
from jax.experimental import pallas as pl
from jax.experimental.pallas import tpu as pltpu

GRID_W = 64
HEAD_DIM = 64
GROUP_W = 256
KV_W = 128
WINDOW = 128
POOL_WINDOWS = (2, 4, 8, 16)
SGU_CHUNK = 128
N_GROUPS = 4
EXPERTS_PER_GROUP = 8
N_EXPERTS = 32
ROPE_BASE = 10000.0
LN_EPS = 1e-6
NEG_INF = -1e30
DEPTH = 2
RES_ALPHA = (2 * DEPTH) ** 0.25

LANES = 128
SUBLANES = 8
VMEM_LIMIT = 48 * 1024 * 1024

ROW_TILE = 1024
Q_BLOCK = 1024
COMBINE_TILE = 512
WEIGHT_SLOTS = 3
MOE_BLOCK = 512
MOE_PART = 256
MOE_STEP_BLOCKS = 4
FFT_R = 64
FFT_PITCH = 72
FFT_UNROLL = 16
DMA_UNROLL = 8
MERGE_PART = 512

BF16 = jnp.bfloat16
F32 = jnp.float32


def _cparams(*sem):
    return pltpu.CompilerParams(dimension_semantics=sem, vmem_limit_bytes=VMEM_LIMIT)


def _dot(a, b):
    return jnp.dot(a, b, preferred_element_type=F32)


def _dot_nt(a, b):
    return lax.dot_general(a, b, (((1,), (1,)), ((), ())), preferred_element_type=F32)


def _layer_norm(t):
    mu = jnp.mean(t, axis=-1, keepdims=True)
    d = t - mu
    var = jnp.mean(d * d, axis=-1, keepdims=True)
    return d * lax.rsqrt(var + LN_EPS)


def _silu(t):
    return t * (1.0 / (1.0 + jnp.exp(-t)))


def _gelu(t):
    return 0.5 * t * (1.0 + lax.erf(t * (1.0 / math.sqrt(2.0))))


ROW_WORDS = 4
HI_MASK = 0xFFFF0000


def _pack_rows(t):
    half = t.shape[1] // 2
    lo = lax.bitcast_convert_type(t[:, :half].astype(BF16).astype(F32), jnp.uint32)
    hi = lax.bitcast_convert_type(t[:, half:].astype(BF16).astype(F32), jnp.uint32)
    return (lo >> 16) | (hi & jnp.uint32(HI_MASK))


def _unpack_rows(w):
    return (lax.bitcast_convert_type(w << 16, F32),
            lax.bitcast_convert_type(w & jnp.uint32(HI_MASK), F32))


def _store_rows(ref, t, first=0):
    w = _pack_rows(t)
    for j in range(ROW_WORDS):
        ref[pl.ds(first * ROW_WORDS + j, t.shape[0], stride=ROW_WORDS), :] = w[:, j * LANES:(j + 1) * LANES]


def _load_rows(ref, first, m):
    w = jnp.concatenate([ref[pl.ds(first * ROW_WORDS + j, m, stride=ROW_WORDS), :] for j in range(ROW_WORDS)],
                        axis=1)
    return _unpack_rows(w)


def _ada_kernel(c_ref, w_ref, b_ref, o_ref):
    s = _silu(c_ref[...]).astype(BF16)
    o_ref[...] = _dot(s, w_ref[...].astype(BF16)) + b_ref[...]


def _ada(cond, w_ada, b_ada):
    n_layers, d, n = w_ada.shape
    tn = n // 4
    return pl.pallas_call(
        _ada_kernel,
        grid=(n_layers, n // tn),
        in_specs=[
            pl.BlockSpec((SUBLANES, d), lambda l, j: (0, 0)),
            pl.BlockSpec((None, d, tn), lambda l, j: (l, 0, j)),
            pl.BlockSpec((None, 1, tn), lambda l, j: (l, 0, j)),
        ],
        out_specs=pl.BlockSpec((None, SUBLANES, tn), lambda l, j: (l, 0, j)),
        out_shape=jax.ShapeDtypeStruct((n_layers, SUBLANES, n), F32),
        compiler_params=_cparams("arbitrary", "arbitrary"),
        name="ada",
    )(cond, w_ada, b_ada)


def _rope(t, cos_t, sin_t):
    lane = lax.broadcasted_iota(jnp.int32, t.shape, 1)
    first = (lane % 32) < 16
    partner = jnp.where(first, pltpu.roll(t, LANES - 16, axis=1), pltpu.roll(t, 16, axis=1))
    return t * cos_t + partner * sin_t


def _proj_body(x, m, w_ref, cos_ref, sin_ref, outs, *, rope, a_pitch):
    a_ref, q_ref, qs_ref, k_ref, v_ref, p_ref, ug_ref = outs
    tm, d = x.shape
    shift, scale = m[:, 0:d], m[:, d:2 * d]
    h = _layer_norm(x) * (1.0 + scale) + shift
    z = _dot(h.astype(BF16), w_ref[...])
    pad = jnp.zeros((a_pitch - FFT_R, LANES), F32)
    for g in range(tm // FFT_R):
        for hf in range(2):
            grp = z[g * FFT_R:(g + 1) * FFT_R, hf * LANES:(hf + 1) * LANES]
            if a_pitch > FFT_R:
                grp = jnp.concatenate([grp, pad], axis=0)
            a_ref[hf, g * a_pitch:(g + 1) * a_pitch, :] = grp
    q0, q1 = z[:, 256:384], z[:, 384:512]
    k = z[:, 512:640]
    if rope:
        cos_t, sin_t = cos_ref[...], sin_ref[...]
        q0, q1, k = _rope(q0, cos_t, sin_t), _rope(q1, cos_t, sin_t), _rope(k, cos_t, sin_t)
    q_ref[:, 0:128] = q0.astype(BF16)
    q_ref[:, 128:256] = q1.astype(BF16)
    qs_ref[:, 0:128] = pltpu.roll(q0, HEAD_DIM, axis=1).astype(BF16)
    qs_ref[:, 128:256] = pltpu.roll(q1, HEAD_DIM, axis=1).astype(BF16)
    k_ref[...] = k.astype(BF16)
    v_ref[...] = z[:, 640:768].astype(BF16)
    p_ref[...] = z[:, 768:1024].astype(BF16)
    ug_ref[...] = z[:, 1024:1536].astype(BF16)


def _mod_row(mod_ref, mod_row, tm, rows_per_batch):
    row = (pl.program_id(0) * tm) // rows_per_batch if mod_row is None else mod_row
    return mod_ref[pl.ds(row, 1), :]


def _proj_kernel(x_ref, mod_ref, w_ref, cos_ref, sin_ref, *outs, mod_row, rows_per_batch, rope, a_pitch):
    m = _mod_row(mod_ref, mod_row, x_ref.shape[0], rows_per_batch)
    _proj_body(x_ref[...], m, w_ref, cos_ref, sin_ref, outs, rope=rope, a_pitch=a_pitch)


def _proj_specs(rows, tm, a_pitch, seq_steps):
    row_spec = lambda w: pl.BlockSpec((tm, w), lambda i: (i, 0))
    out_w = (256, 256, 128, 128, 256, 512)
    ta = tm // FFT_R * a_pitch
    out_specs = [pl.BlockSpec((2, ta, LANES), lambda i: (0, i, 0))] + [row_spec(w) for w in out_w]
    out_shape = ([jax.ShapeDtypeStruct((2, rows // FFT_R * a_pitch, LANES), F32)]
                 + [jax.ShapeDtypeStruct((rows, w), BF16) for w in out_w])
    table_spec = pl.BlockSpec((tm, LANES), lambda i: (i % seq_steps, 0))
    return table_spec, out_specs, out_shape


def _proj(x2, mod, w_in, cos_t, sin_t, *, mod_row, rows_per_batch, rope, tm, a_pitch):
    rows, d = x2.shape
    kern = functools.partial(_proj_kernel, mod_row=mod_row, rows_per_batch=rows_per_batch,
                             rope=rope, a_pitch=a_pitch)
    table_spec, out_specs, out_shape = _proj_specs(rows, tm, a_pitch, cos_t.shape[0] // tm)
    return pl.pallas_call(
        kern,
        grid=(rows // tm,),
        in_specs=[
            pl.BlockSpec((tm, d), lambda i: (i, 0)),
            pl.BlockSpec(mod.shape, lambda i: (0, 0)),
            pl.BlockSpec(w_in.shape, lambda i: (0, 0)),
            table_spec, table_spec,
        ],
        out_specs=out_specs,
        out_shape=out_shape,
        compiler_params=_cparams("arbitrary"),
        name="proj",
    )(x2, mod, w_in, cos_t, sin_t)


def _fft_tables(n_pos):
    r = FFT_R
    assert n_pos == r * r
    kb = np.arange(r)[None, :, None]
    na = np.arange(r)[:, None, None]
    nb = np.arange(r)[None, None, :]
    ang = 2.0 * np.pi * ((kb * (na + r * nb)) % n_pos) / n_pos
    m1 = np.concatenate([np.cos(ang), -np.sin(ang)], axis=1)
    ka = np.arange(r)[:, None]
    n2 = np.arange(r)[None, :]
    ang2 = 2.0 * np.pi * ((ka * n2) % r) / r
    c2, s2 = np.cos(ang2), np.sin(ang2)
    w2 = np.block([[c2, s2], [-s2, c2]])
    return m1, w2


def _channel_tables(n_pos):
    h = HEAD_DIM
    c = np.arange(h)
    ang = 2.0 * np.pi * ((c[:, None] * c[None, :]) % h) / h
    scale = 1.0 / math.sqrt(n_pos * h)
    eye = np.eye(GROUP_W // h)
    cc = np.kron(eye, np.cos(ang)) * scale
    ss = np.kron(eye, np.sin(ang)) * scale
    return np.concatenate([cc, ss], axis=0)


def _fourier_kernel(a_ref, m1_ref, w2_ref, ch_ref, wf_ref, o_ref, z_ref, y_ref):
    r, pt = FFT_R, FFT_PITCH

    def step1(i, c):
        for u in range(FFT_UNROLL):
            na = i * FFT_UNROLL + u
            rows = jnp.concatenate([a_ref[0, pl.ds(na, r, stride=pt), :],
                                    a_ref[1, pl.ds(na, r, stride=pt), :]], axis=1)
            z = _dot(m1_ref[na], rows.astype(BF16))
            base = pl.multiple_of(na * pt, SUBLANES)
            z_ref[0, pl.ds(base, r), :] = z[0:r, 0:LANES]
            z_ref[1, pl.ds(base, r), :] = z[0:r, LANES:]
            z_ref[2, pl.ds(base, r), :] = z[r:, 0:LANES]
            z_ref[3, pl.ds(base, r), :] = z[r:, LANES:]
        return c

    lax.fori_loop(0, r // FFT_UNROLL, step1, 0)

    def step2(i, c):
        for u in range(FFT_UNROLL):
            kb = i * FFT_UNROLL + u
            q = [z_ref[j, pl.ds(kb, r, stride=pt), :] for j in range(4)]
            zs = jnp.concatenate([jnp.concatenate(q[0:2], axis=1),
                                  jnp.concatenate(q[2:4], axis=1)], axis=0)
            y = _dot(w2_ref[...], zs.astype(BF16))
            base = pl.multiple_of(kb * r, r)
            y_ref[0, pl.ds(base, r), :] = y[0:r, 0:LANES]
            y_ref[1, pl.ds(base, r), :] = y[0:r, LANES:]
            y_ref[2, pl.ds(base, r), :] = y[r:, 0:LANES]
            y_ref[3, pl.ds(base, r), :] = y[r:, LANES:]
        return c

    lax.fori_loop(0, r // FFT_UNROLL, step2, 0)

    chunk = 8 * r
    for cidx in range(r * r // chunk):
        yy = jnp.concatenate([y_ref[j, cidx * chunk:(cidx + 1) * chunk, :] for j in range(4)], axis=1)
        f = _dot(yy.astype(BF16), ch_ref[...])
        g = _dot(f.astype(BF16), wf_ref[...])
        for gi in range(chunk // r):
            kb = cidx * (chunk // r) + gi
            z_ref[0, kb * pt:kb * pt + r, :] = g[gi * r:(gi + 1) * r, 0:LANES]
            z_ref[1, kb * pt:kb * pt + r, :] = g[gi * r:(gi + 1) * r, LANES:]

    def step3(i, c):
        for u in range(FFT_UNROLL):
            ka = i * FFT_UNROLL + u
            base = pl.multiple_of(ka * r, r)
            o_ref[pl.ds(base, r), 0:LANES] = z_ref[0, pl.ds(ka, r, stride=pt), :]
            o_ref[pl.ds(base, r), LANES:] = z_ref[1, pl.ds(ka, r, stride=pt), :]
        return c

    lax.fori_loop(0, r // FFT_UNROLL, step3, 0)


def _fourier(a3, w_fourier, n_pos):
    rows = a3.shape[1] // FFT_PITCH * FFT_R
    gw = GROUP_W
    m1, w2 = _fft_tables(n_pos)
    ch = _channel_tables(n_pos)
    const = lambda shape: pl.BlockSpec(shape, lambda b: (0,) * len(shape))
    return pl.pallas_call(
        _fourier_kernel,
        grid=(rows // n_pos,),
        in_specs=[
            pl.BlockSpec((2, FFT_R * FFT_PITCH, LANES), lambda b: (0, b, 0)),
            const(m1.shape), const(w2.shape), const(ch.shape), const(w_fourier.shape),
        ],
        out_specs=pl.BlockSpec((n_pos, gw), lambda b: (b, 0)),
        out_shape=jax.ShapeDtypeStruct((rows, gw), F32),
        scratch_shapes=[pltpu.VMEM((4, FFT_R * FFT_PITCH, LANES), F32), pltpu.VMEM((4, n_pos, LANES), F32)],
        compiler_params=_cparams("arbitrary"),
        name="fourier",
    )(a3, jnp.asarray(m1, BF16), jnp.asarray(w2, BF16), jnp.asarray(ch, BF16), w_fourier)


def _fourier_small_kernel(a_ref, cs_ref, ch_ref, wf_ref, o_ref):
    n = a_ref.shape[1]
    a = jnp.concatenate([a_ref[0], a_ref[1]], axis=1)
    pq = _dot(cs_ref[...], a.astype(BF16))
    y = jnp.concatenate([pq[0:n], pq[n:2 * n]], axis=1).astype(BF16)
    f = _dot(y, ch_ref[...])
    o_ref[...] = _dot(f.astype(BF16), wf_ref[...])


def _fourier_small(a3, w_fourier, n_pos):
    _, rows, _ = a3.shape
    gw = GROUP_W
    k = np.arange(n_pos)
    ang = 2.0 * np.pi * ((k[:, None] * k[None, :]) % n_pos) / n_pos
    cs = np.concatenate([np.cos(ang), -np.sin(ang)], axis=0)
    ch = _channel_tables(n_pos)
    const = lambda shape: pl.BlockSpec(shape, lambda b: (0,) * len(shape))
    return pl.pallas_call(
        _fourier_small_kernel,
        grid=(rows // n_pos,),
        in_specs=[pl.BlockSpec((2, n_pos, LANES), lambda b: (0, b, 0)),
                  const(cs.shape), const(ch.shape), const(w_fourier.shape)],
        out_specs=pl.BlockSpec((n_pos, gw), lambda b: (b, 0)),
        out_shape=jax.ShapeDtypeStruct((rows, gw), F32),
        compiler_params=_cparams("arbitrary"),
        name="fourier_ctx",
    )(a3, jnp.asarray(cs, BF16), jnp.asarray(ch, BF16), w_fourier)


ATTN_SUB = 128


def _attn_kernel(sink_ref, q_ref, qs_ref, k_ref, v_ref, kc_ref, vc_ref, o_ref, *, band, seq):
    qb = q_ref.shape[0]
    sub = ATTN_SUB
    lane = lax.broadcasted_iota(jnp.int32, (1, LANES), 1)
    lo_half = lane < HEAD_DIM
    zero = jnp.zeros((), BF16)
    scale = jnp.asarray(HEAD_DIM ** -0.5, BF16)
    kw = sub + 2 * WINDOW
    for sb in range(qb // sub):
        rows = slice(sb * sub, (sb + 1) * sub)
        qa0, qa1 = q_ref[rows, 0:LANES], q_ref[rows, LANES:]
        qs0, qs1 = qs_ref[rows, 0:LANES], qs_ref[rows, LANES:]
        q_all = jnp.concatenate([jnp.where(lo_half, qa0, zero), jnp.where(lo_half, qs0, zero),
                                 jnp.where(lo_half, zero, qs1), jnp.where(lo_half, zero, qa1)], axis=0) * scale
        if band:
            p0 = pl.program_id(1) * qb + sb * sub
            start = pl.multiple_of(jnp.clip(p0 - WINDOW, 0, seq - kw), WINDOW)
            qpos = p0 + lax.broadcasted_iota(jnp.int32, (sub, 1), 0)
            kpos = start + lax.broadcasted_iota(jnp.int32, (1, kw), 1)
            bias = jnp.where(jnp.abs(qpos - kpos) <= WINDOW, 0.0, NEG_INF)
            keys = jnp.concatenate([k_ref[pl.ds(start, kw), :], kc_ref[...]], axis=0)
            vals = jnp.concatenate([v_ref[pl.ds(start, kw), :], vc_ref[...]], axis=0)
        else:
            keys, vals = kc_ref[...], vc_ref[...]
        s_all = _dot_nt(q_all, keys)
        probs, dens = [], []
        for h in range(4):
            s = s_all[h * sub:(h + 1) * sub, :]
            sink = sink_ref[h]
            if band:
                s = jnp.concatenate([s[:, 0:kw] + bias, s[:, kw:]], axis=1)
            m = jnp.maximum(jnp.max(s, axis=1, keepdims=True), sink)
            p = jnp.exp(s - m)
            dens.append(jnp.sum(p, axis=1, keepdims=True) + jnp.exp(sink - m))
            probs.append(p.astype(BF16))
        o_all = _dot(jnp.concatenate(probs, axis=0), vals)
        o = [o_all[h * sub:(h + 1) * sub, :] / dens[h] for h in range(4)]
        o_ref[rows, 0:LANES] = jnp.where(lo_half, o[0], pltpu.roll(o[1], HEAD_DIM, axis=1)).astype(BF16)
        o_ref[rows, LANES:] = jnp.where(lo_half, pltpu.roll(o[2], HEAD_DIM, axis=1), o[3]).astype(BF16)


def _attention(sink, q, qs, k, v, kc, vc, *, seq, n_ctx, band):
    rows = q.shape[0]
    n_batch = rows // seq
    qb = Q_BLOCK if band else seq
    steps = seq // qb
    kern = functools.partial(_attn_kernel, band=band, seq=seq)
    seq_spec = pl.BlockSpec((seq, KV_W), lambda b, i: (b, 0))
    ctx_spec = pl.BlockSpec((n_ctx, KV_W), lambda b, i: (b, 0))
    q_spec = pl.BlockSpec((qb, GROUP_W), lambda b, i: (b * steps + i, 0))
    return pl.pallas_call(
        kern,
        grid=(n_batch, steps),
        in_specs=[pl.BlockSpec(memory_space=pltpu.SMEM), q_spec, q_spec,
                  seq_spec, seq_spec, ctx_spec, ctx_spec],
        out_specs=q_spec,
        out_shape=jax.ShapeDtypeStruct((rows, GROUP_W), BF16),
        compiler_params=_cparams("arbitrary", "arbitrary"),
        name="attn" if band else "attn_ctx",
    )(sink, q, qs, k, v, kc, vc)


POOL_HALO = max(POOL_WINDOWS) // 2


def _pool(p_ref, t0, tm, seq):
    halo = POOL_HALO
    pack = 2 * SUBLANES
    t0 = pl.multiple_of(t0, pack)
    main = p_ref[pl.ds(t0, tm), :].astype(F32)
    lo = pl.multiple_of(jnp.maximum(t0 - pack, 0), pack)
    hi = pl.multiple_of(jnp.minimum(t0 + tm, seq - pack), pack)
    prev = p_ref[pl.ds(lo, pack), :].astype(F32)[pack - halo:, :]
    nxt = p_ref[pl.ds(hi, pack), :].astype(F32)[:halo, :]
    prev = jnp.where(t0 > 0, prev, 0.0)
    nxt = jnp.where(t0 + tm < seq, nxt, 0.0)
    full = jnp.concatenate([prev, main, nxt], axis=0)
    n = tm + 2 * halo
    gch = GROUP_W // len(POOL_WINDOWS)
    first = lax.broadcasted_iota(jnp.int32, (1, LANES), 1) < gch
    means = []
    for hf in range(GROUP_W // LANES):
        wa, wb = POOL_WINDOWS[2 * hf], POOL_WINDOWS[2 * hf + 1]
        x = full[:, hf * LANES:(hf + 1) * LANES]
        sums, w, s = {}, 2, pltpu.roll(x, 1, axis=0) + x
        sums[w] = s
        while w < wb:
            s = pltpu.roll(s, w // 2, axis=0) + pltpu.roll(s, n - w // 2, axis=0)
            w *= 2
            sums[w] = s
        means.append(jnp.where(first, sums[wa] * (1.0 / wa), sums[wb] * (1.0 / wb))[halo:halo + tm, :])
    mean = jnp.concatenate(means, axis=1)
    win = jnp.concatenate([jnp.full((1, gch), w, jnp.int32) for w in POOL_WINDOWS], axis=1)

    def rescale(rows, first_pos):
        pos = first_pos + lax.broadcasted_iota(jnp.int32, (halo, 1), 0)
        cnt = jnp.minimum(pos + win // 2, seq) - jnp.maximum(pos - win // 2, 0)
        return rows * (win.astype(F32) / cnt.astype(F32))

    mean = jnp.concatenate([rescale(mean[:halo], t0), mean[halo:tm - halo],
                            rescale(mean[tm - halo:], t0 + tm - halo)], axis=0)
    return mean - main


def _route(logits):
    tm = logits.shape[0]
    lt = logits.T
    gl = lt[N_EXPERTS:N_EXPERTS + N_GROUPS]
    sub_g = lax.broadcasted_iota(jnp.int32, gl.shape, 0)
    gmax = jnp.max(gl, axis=0, keepdims=True)
    grp = jnp.min(jnp.where(gl == gmax, sub_g, N_GROUPS), axis=0, keepdims=True)
    gate_group = 1.0 / jnp.sum(jnp.exp(gl - gmax), axis=0, keepdims=True)
    el = lt[0:EXPERTS_PER_GROUP]
    for g in range(1, N_GROUPS):
        el = jnp.where(grp == g, lt[g * EXPERTS_PER_GROUP:(g + 1) * EXPERTS_PER_GROUP], el)
    sub = lax.broadcasted_iota(jnp.int32, el.shape, 0)
    m1 = jnp.max(el, axis=0, keepdims=True)
    i1 = jnp.min(jnp.where(el == m1, sub, EXPERTS_PER_GROUP), axis=0, keepdims=True)
    el2 = jnp.where(sub == i1, -jnp.inf, el)
    m2 = jnp.max(el2, axis=0, keepdims=True)
    i2 = jnp.min(jnp.where(el2 == m2, sub, EXPERTS_PER_GROUP), axis=0, keepdims=True)
    r = jnp.exp(m2 - m1)
    g1 = gate_group / (1.0 + r)
    g2 = g1 * r
    e1 = (grp * EXPERTS_PER_GROUP + i1).astype(F32)
    e2 = (grp * EXPERTS_PER_GROUP + i2).astype(F32)
    rows = jnp.where(sub == 0, e1, jnp.where(sub == 1, e2, jnp.where(sub == 2, g1, jnp.where(sub == 3, g2, 0.0))))
    cols = jnp.concatenate([rows, jnp.zeros((LANES - rows.shape[0], tm), F32)], axis=0).T
    return cols, rows


def _merge_kernel(x_ref, p_ref, ug_ref, yf_ref, ya_ref, mod_ref, wpool_ref, pscale_ref, wsgu_ref, bsgu_ref,
                  wout_ref, lng_ref, lnb_ref, wr_ref, *rest,
                  mod_row, seq, n_alias):
    x1_ref, h2_ref, route_ref, route_t_ref = rest[n_alias:]
    tm, d = x_ref.shape
    if mod_row is None:
        row = pl.program_id(0)
    else:
        row = mod_row
    t0 = pl.multiple_of(pl.program_id(1) * tm, tm)
    m = mod_ref[pl.ds(row, 1), :]
    gate1, shift2, scale2 = m[:, 2 * d:3 * d], m[:, 3 * d:4 * d], m[:, 4 * d:5 * d]
    lane = lax.broadcasted_iota(jnp.int32, (1, GROUP_W), 1)
    n_heads = wsgu_ref.shape[0] // SGU_CHUNK
    head = lane // (GROUP_W // n_heads)

    pm = min(tm, MERGE_PART)
    for part in range(tm // pm):
        r0 = part * pm
        rows = slice(r0, r0 + pm)
        pooled = _pool(p_ref, t0 + r0, pm, seq)
        y_pool = _dot(pooled.astype(BF16), wpool_ref[...]) * pscale_ref[...]

        ug = ug_ref[rows, :].astype(F32)
        u = _gelu(ug[:, 0:GROUP_W])
        v = _layer_norm(_gelu(ug[:, GROUP_W:])).astype(BF16)
        mixed = []
        for cidx in range(pm // SGU_CHUNK):
            vc = v[cidx * SGU_CHUNK:(cidx + 1) * SGU_CHUNK, :]
            full = _dot(wsgu_ref[...], vc)
            mc = bsgu_ref[...]
            for hd in range(n_heads):
                mc = mc + jnp.where(head == hd, full[hd * SGU_CHUNK:(hd + 1) * SGU_CHUNK, :], 0.0)
            mixed.append(mc)
        y_sgu = u * jnp.concatenate(mixed, axis=0)

        cat = jnp.concatenate([yf_ref[rows, :].astype(BF16), ya_ref[rows, :], y_pool.astype(BF16),
                               y_sgu.astype(BF16)], axis=1)
        y = _dot(cat, wout_ref[...])
        x1 = _layer_norm(RES_ALPHA * x_ref[rows, :] + gate1 * y) * lng_ref[...] + lnb_ref[...]
        x1_ref[rows, :] = x1
        h2 = _layer_norm(x1) * (1.0 + scale2) + shift2
        _store_rows(h2_ref, h2, first=r0)
        lg = _dot(h2.astype(BF16), wr_ref[...])
        route_ref[rows, :], route_t_ref[:, rows] = _route(lg[:, 0:LANES] + lg[:, LANES:])


def _merge(x2, p, ug, y_four, y_attn, mod, w_pool_bd, pool_scale, w_sgu_stack, b_sgu_exp, w_out,
           ln_g, ln_b, w_route, aliased, *, mod_row, seq, tm, row_off, total_rows):
    rows, d = x2.shape
    n_batch, steps = rows // seq, seq // tm
    off = row_off // tm
    kern = functools.partial(_merge_kernel, mod_row=mod_row, seq=seq, n_alias=len(aliased))
    row_spec = lambda w: pl.BlockSpec((tm, w), lambda b, i: (b * steps + i, 0))
    const = lambda a: pl.BlockSpec(a.shape, lambda b, i: (0,) * a.ndim)
    consts = (mod, w_pool_bd, pool_scale, w_sgu_stack, b_sgu_exp, w_out, ln_g, ln_b, w_route)
    n_in = 5 + len(consts)
    out_shapes = [jax.ShapeDtypeStruct((total_rows, d), F32),
                  jax.ShapeDtypeStruct((total_rows * ROW_WORDS, LANES), jnp.uint32),
                  jax.ShapeDtypeStruct((total_rows, LANES), F32),
                  jax.ShapeDtypeStruct((SUBLANES, total_rows), F32)]
    out_specs = [pl.BlockSpec((tm, d), lambda b, i: (off + b * steps + i, 0)),
                 pl.BlockSpec((tm * ROW_WORDS, LANES), lambda b, i: (off + b * steps + i, 0)),
                 pl.BlockSpec((tm, LANES), lambda b, i: (off + b * steps + i, 0)),
                 pl.BlockSpec((SUBLANES, tm), lambda b, i: (0, off + b * steps + i))]
    return pl.pallas_call(
        kern,
        grid=(n_batch, steps),
        in_specs=[row_spec(d), pl.BlockSpec((seq, GROUP_W), lambda b, i: (b, 0)),
                  row_spec(2 * GROUP_W), row_spec(GROUP_W), row_spec(GROUP_W)]
                 + [const(a) for a in consts]
                 + [pl.BlockSpec(memory_space=pl.ANY)] * len(aliased),
        out_specs=out_specs,
        out_shape=out_shapes,
        input_output_aliases={n_in + k: k for k in range(len(aliased))},
        compiler_params=_cparams("arbitrary", "arbitrary"),
        name="merge",
    )(x2, p, ug, y_four, y_attn, *consts, *aliased)


def _plan_kernel(route_ref, dest_ref, cnt_out_ref, cnt_ref, start_ref, carry_ref):
    ph, t = pl.program_id(0), pl.program_id(1)
    tm = route_ref.shape[1]
    rt = route_ref[...]
    e1 = rt[0:1, :].astype(jnp.int32)
    e2 = rt[1:2, :].astype(jnp.int32)
    sub = lax.broadcasted_iota(jnp.int32, (N_EXPERTS, tm), 0)
    hit1, hit2 = sub == e1, sub == e2
    onehot = jnp.where(hit1 | hit2, 1.0, 0.0)
    tile_cnt = jnp.sum(onehot, axis=1, keepdims=True)

    @pl.when((ph == 0) & (t == 0))
    def _():
        cnt_ref[...] = jnp.zeros_like(cnt_ref)

    @pl.when(ph == 0)
    def _():
        cnt_ref[...] += tile_cnt

    @pl.when((ph == 1) & (t == 0))
    def _():
        cnt = cnt_ref[...]
        padded = jnp.floor((cnt + (MOE_BLOCK - 1.0)) * (1.0 / MOE_BLOCK)) * MOE_BLOCK
        row = lax.broadcasted_iota(jnp.int32, cnt.shape, 0)
        incl = padded
        sh = 1
        while sh < N_EXPERTS:
            incl = incl + jnp.where(row >= sh, pltpu.roll(incl, sh, axis=0), 0.0)
            sh *= 2
        start_ref[...] = incl - padded
        carry_ref[...] = jnp.zeros_like(carry_ref)
        cnt_out_ref[...] = cnt

    @pl.when(ph == 1)
    def _():
        r_i = lax.broadcasted_iota(jnp.int32, (tm, tm), 0)
        c_i = lax.broadcasted_iota(jnp.int32, (tm, tm), 1)
        before = jnp.where(r_i < c_i, 1.0, 0.0).astype(BF16)
        rank = _dot(onehot.astype(BF16), before)
        base = start_ref[:, 0:1] + carry_ref[:, 0:1] + rank
        d1 = jnp.sum(jnp.where(hit1, base, 0.0), axis=0, keepdims=True)
        d2 = jnp.sum(jnp.where(hit2, base, 0.0), axis=0, keepdims=True)
        sub8 = lax.broadcasted_iota(jnp.int32, (SUBLANES, tm), 0)
        dest_ref[...] = jnp.where(sub8 == 0, d1, d2).astype(jnp.int32)
        carry_ref[...] += tile_cnt


def _plan(route_t, tm):
    rows = route_t.shape[1]
    n_t = rows // tm
    return pl.pallas_call(
        _plan_kernel,
        grid=(2, n_t),
        in_specs=[pl.BlockSpec((SUBLANES, tm), lambda ph, t: (0, t))],
        out_specs=[pl.BlockSpec((None, SUBLANES, tm), lambda ph, t: (t * ph, 0, 0)),
                   pl.BlockSpec((N_EXPERTS, LANES), lambda ph, t: (0, 0))],
        out_shape=[jax.ShapeDtypeStruct((n_t, SUBLANES, tm), jnp.int32),
                   jax.ShapeDtypeStruct((N_EXPERTS, LANES), F32)],
        scratch_shapes=[pltpu.VMEM((N_EXPERTS, LANES), F32)] * 3,
        compiler_params=_cparams("arbitrary", "arbitrary"),
        name="moe_plan",
    )(route_t)


def _row_copy(src_ref, src_row, dst_ref, dst_row, sem):
    return pltpu.make_async_copy(
        src_ref.at[pl.ds(pl.multiple_of(src_row * ROW_WORDS, ROW_WORDS), ROW_WORDS)],
        dst_ref.at[pl.ds(pl.multiple_of(dst_row * ROW_WORDS, ROW_WORDS), ROW_WORDS)], sem)


def _dispatch_kernel(dest_ref, h2_ref, xs_ref, sem, *, tm):
    def body(r4, c):
        for u in range(DMA_UNROLL):
            r = r4 * DMA_UNROLL + u
            for k in range(2):
                _row_copy(h2_ref, r, xs_ref, dest_ref[0, 0, k * tm + r], sem).start(priority=k)
        return c

    lax.fori_loop(0, tm // DMA_UNROLL, body, 0)
    for k in range(2):
        pltpu.make_async_copy(h2_ref, xs_ref.at[pl.ds(0, tm * ROW_WORDS)], sem).wait()


def _dispatch(dest, h2_tiles, n_slots, tm):
    n_t = dest.shape[0]
    return pl.pallas_call(
        functools.partial(_dispatch_kernel, tm=tm),
        grid=(n_t,),
        in_specs=[pl.BlockSpec((1, 1, 2 * tm), lambda i: (i, 0, 0), memory_space=pltpu.SMEM),
                  pl.BlockSpec((tm * ROW_WORDS, LANES), lambda i: (i, 0))],
        out_specs=pl.BlockSpec(memory_space=pl.ANY),
        out_shape=jax.ShapeDtypeStruct((n_slots * ROW_WORDS, LANES), jnp.uint32),
        scratch_shapes=[pltpu.SemaphoreType.DMA],
        compiler_params=_cparams("arbitrary"),
        name="moe_dispatch",
    )(dest, h2_tiles)


def _expert_kernel(be_ref, bn_ref, first_ref, slot_ref, nxt_ref, nxt2_ref, _blk_ref,
                   x_ref, wg_hbm, wu_hbm, wd_hbm, y_ref,
                   wg_buf, wu_buf, wd_buf, wg_bf, wu_bf, wd_bf, sem, *, layer):
    mb = MOE_BLOCK
    half = wg_bf.shape[0] // 2

    def weight_copies(e, s):
        return [pltpu.make_async_copy(wg_hbm.at[layer, e], wg_buf.at[s], sem.at[s]),
                pltpu.make_async_copy(wu_hbm.at[layer, e], wu_buf.at[s], sem.at[s]),
                pltpu.make_async_copy(wd_hbm.at[layer, e], wd_buf.at[s], sem.at[s])]

    @pl.when(pl.program_id(0) == 0)
    def _():
        for cp in weight_copies(be_ref[0], 0):
            cp.start()

        @pl.when(nxt_ref[0] >= 0)
        def _():
            for cp in weight_copies(nxt_ref[0], 1):
                cp.start(priority=1)

    def block(i, row0):
        @pl.when(first_ref[i] == 1)
        def _():
            s = slot_ref[i]
            for cp in weight_copies(be_ref[i], s):
                cp.wait()

            @pl.when(nxt2_ref[i] >= 0)
            def _():
                s2 = jnp.where(s == 0, WEIGHT_SLOTS - 1, s - 1)
                for cp in weight_copies(nxt2_ref[i], s2):
                    cp.start(priority=1)

            wg_bf[...] = wg_buf[s].astype(BF16)
            wu_bf[...] = wu_buf[s].astype(BF16)
            wd_bf[...] = wd_buf[s].astype(BF16)

        def compute(m):
            live = lax.broadcasted_iota(jnp.int32, (m, 1), 0) < bn_ref[i]
            x_lo, x_hi = _load_rows(x_ref, row0, m)
            x_lo = jnp.where(live, x_lo, 0.0).astype(BF16)
            x_hi = jnp.where(live, x_hi, 0.0).astype(BF16)
            g = _dot(x_lo, wg_bf[0:half, :]) + _dot(x_hi, wg_bf[half:, :])
            u = _dot(x_lo, wu_bf[0:half, :]) + _dot(x_hi, wu_bf[half:, :])
            hid = (_silu(g) * u).astype(BF16)
            _store_rows(y_ref, _dot(hid, wd_bf[...]), first=row0)

        for parts in range(1, mb // MOE_PART + 1):
            @pl.when((bn_ref[i] > (parts - 1) * MOE_PART) & (bn_ref[i] <= parts * MOE_PART))
            def _(parts=parts):
                compute(parts * MOE_PART)

    for sb in range(MOE_STEP_BLOCKS):
        block(pl.program_id(0) * MOE_STEP_BLOCKS + sb, sb * mb)


def _experts(table, xs, w_gate, w_up, w_down, layer):
    n_blocks = table[0].shape[0]
    _, _, d, de = w_gate.shape
    shape = (MOE_STEP_BLOCKS * MOE_BLOCK * ROW_WORDS, LANES)
    hbm = pl.BlockSpec(memory_space=pl.ANY)
    return pl.pallas_call(
        functools.partial(_expert_kernel, layer=layer),
        grid_spec=pltpu.PrefetchScalarGridSpec(
            num_scalar_prefetch=len(table),
            grid=(n_blocks // MOE_STEP_BLOCKS,),
            in_specs=[pl.BlockSpec(shape, lambda i, *t: (t[-1][i], 0)), hbm, hbm, hbm],
            out_specs=pl.BlockSpec(shape, lambda i, *t: (t[-1][i], 0)),
            scratch_shapes=[pltpu.VMEM((WEIGHT_SLOTS, d, de), F32), pltpu.VMEM((WEIGHT_SLOTS, d, de), F32),
                            pltpu.VMEM((WEIGHT_SLOTS, de, d), F32),
                            pltpu.VMEM((d, de), BF16), pltpu.VMEM((d, de), BF16), pltpu.VMEM((de, d), BF16),
                            pltpu.SemaphoreType.DMA((WEIGHT_SLOTS,))]),
        out_shape=jax.ShapeDtypeStruct(xs.shape, jnp.uint32),
        compiler_params=_cparams("arbitrary"),
        name="moe_experts",
    )(*table, xs, w_gate, w_up, w_down)


def _combine_kernel(dest_ref, dest_next_ref, x1_ref, route_ref, mod_ref, lng_ref, lnb_ref, y_ref, o_ref,
                    buf_ref, sem, *, mod_row, rows_per_batch):
    tm, d = x1_ref.shape
    i = pl.program_id(0)
    n = pl.num_programs(0)
    slot = i % 2
    m = _mod_row(mod_ref, mod_row, tm, rows_per_batch)

    def gather(idx_ref, s):
        def body(r4, c):
            for u in range(DMA_UNROLL):
                r = r4 * DMA_UNROLL + u
                for k in range(2):
                    _row_copy(y_ref, idx_ref[0, 0, k * tm + r], buf_ref.at[s], k * tm + r,
                              sem.at[s]).start(priority=k)
            return c

        lax.fori_loop(0, tm // DMA_UNROLL, body, 0)

    @pl.when(i == 0)
    def _():
        gather(dest_ref, 0)

    @pl.when(i + 1 < n)
    def _():
        gather(dest_next_ref, 1 - slot)

    for k in range(2):
        pltpu.make_async_copy(y_ref.at[pl.ds(0, tm * ROW_WORDS)],
                              buf_ref.at[slot, pl.ds(0, tm * ROW_WORDS)], sem.at[slot]).wait()

    gate2 = m[:, 5 * d:6 * d]
    rt = route_ref[...]
    f = jnp.zeros((tm, d), F32)
    for k in range(2):
        lo, hi = _load_rows(buf_ref.at[slot], k * tm, tm)
        f = f + jnp.concatenate([lo, hi], axis=1) * rt[:, 2 + k:3 + k]
    o_ref[...] = _layer_norm(RES_ALPHA * x1_ref[...] + gate2 * f) * lng_ref[...] + lnb_ref[...]


def _combine(dest, x1, route, mod, ln_g, ln_b, y_tiles, *, row_off, rows, mod_row, rows_per_batch):
    d = x1.shape[1]
    tm = COMBINE_TILE
    steps = rows // tm
    off = row_off // tm
    kern = functools.partial(_combine_kernel, mod_row=mod_row, rows_per_batch=rows_per_batch)
    const = lambda a: pl.BlockSpec(a.shape, lambda i: (0,) * a.ndim)
    return pl.pallas_call(
        kern,
        grid=(steps,),
        in_specs=[pl.BlockSpec((1, 1, 2 * tm), lambda i: (off + i, 0, 0), memory_space=pltpu.SMEM),
                  pl.BlockSpec((1, 1, 2 * tm), lambda i: (off + jnp.minimum(i + 1, steps - 1), 0, 0),
                               memory_space=pltpu.SMEM),
                  pl.BlockSpec((tm, d), lambda i: (off + i, 0)),
                  pl.BlockSpec((tm, LANES), lambda i: (off + i, 0)),
                  const(mod), const(ln_g), const(ln_b), pl.BlockSpec(memory_space=pl.ANY)],
        out_specs=pl.BlockSpec((tm, d), lambda i: (i, 0)),
        out_shape=jax.ShapeDtypeStruct((rows, d), F32),
        scratch_shapes=[pltpu.VMEM((2, 2 * tm * ROW_WORDS, LANES), jnp.uint32), pltpu.SemaphoreType.DMA((2,))],
        compiler_params=_cparams("arbitrary"),
        name="moe_combine",
    )(dest, dest, x1, route, mod, ln_g, ln_b, y_tiles)


def _rope_tables(n_pos):
    rows = n_pos // GRID_W
    row = jnp.repeat(jnp.arange(rows), GRID_W).astype(F32)
    col = jnp.tile(jnp.arange(GRID_W), rows).astype(F32)
    n_freq = HEAD_DIM // 4
    freq = ROPE_BASE ** (-jnp.arange(n_freq, dtype=F32) / n_freq)
    ang_r, ang_c = row[:, None] * freq, col[:, None] * freq
    cos_h = jnp.concatenate([jnp.cos(ang_r)] * 2 + [jnp.cos(ang_c)] * 2, axis=1)
    sin_h = jnp.concatenate([-jnp.sin(ang_r), jnp.sin(ang_r), -jnp.sin(ang_c), jnp.sin(ang_c)], axis=1)
    return jnp.tile(cos_h, (1, 2)), jnp.tile(sin_h, (1, 2))


def _block_table(counts, n_blocks):
    cnt = counts.astype(jnp.int32)
    padded = (cnt + MOE_BLOCK - 1) // MOE_BLOCK * MOE_BLOCK
    pad_end = jnp.cumsum(padded)
    pad_start = pad_end - padded
    blk_start = jnp.arange(n_blocks, dtype=jnp.int32)[:, None] * MOE_BLOCK
    be = jnp.minimum(jnp.sum((pad_end[None, :] <= blk_start).astype(jnp.int32), axis=1), N_EXPERTS - 1)
    ids = jnp.arange(N_EXPERTS, dtype=jnp.int32)
    mine = be[:, None] == ids[None, :]
    fill = jnp.sum(jnp.where(mine, cnt[None, :] + pad_start[None, :], 0), axis=1) - blk_start[:, 0]
    bn = jnp.clip(fill, 0, MOE_BLOCK)
    prev = jnp.concatenate([jnp.full((1,), -1, jnp.int32), be[:-1]])
    first = ((bn > 0) & (be != prev)).astype(jnp.int32)
    slot = (jnp.cumsum(first) - 1) % WEIGHT_SLOTS
    later = (ids[None, :] > ids[:, None]) & (cnt[None, :] > 0)
    nxt_e = jnp.min(jnp.where(later, ids[None, :], N_EXPERTS), axis=1)
    hop = nxt_e[:, None] == ids[None, :]
    nxt2_e = jnp.sum(jnp.where(hop, nxt_e[None, :], 0), axis=1) + jnp.where(nxt_e == N_EXPERTS, N_EXPERTS, 0)
    lookup = lambda tab: jnp.sum(jnp.where(mine, jnp.where(tab >= N_EXPERTS, -1, tab)[None, :], 0), axis=1)
    last_step = jnp.maximum(jnp.sum((bn > 0).astype(jnp.int32)) - 1, 0) // MOE_STEP_BLOCKS
    step_idx = jnp.minimum(jnp.arange(n_blocks // MOE_STEP_BLOCKS, dtype=jnp.int32), last_step)
    return (be, bn, first, slot.astype(jnp.int32), lookup(nxt_e).astype(jnp.int32),
            lookup(nxt2_e).astype(jnp.int32), step_idx.astype(jnp.int32))


def _moe(route_t, h2_tiles, w_gate, w_up, w_down, layer):
    rows = route_t.shape[1]
    tm = ROW_TILE
    n_blocks = -(-(2 * rows) // MOE_BLOCK) + N_EXPERTS
    n_blocks = -(-n_blocks // MOE_STEP_BLOCKS) * MOE_STEP_BLOCKS
    dest8, counts = _plan(route_t, tm)
    dest = dest8[:, 0:2, :].reshape(rows // tm, 1, 2 * tm)
    table = _block_table(counts[:, 0], n_blocks)
    xs = _dispatch(dest, h2_tiles, n_blocks * MOE_BLOCK, tm)
    ys = _experts(table, xs, w_gate, w_up, w_down, layer)
    tc = COMBINE_TILE
    dest_c = dest8[:, 0:2, :].reshape(rows // tm, 2, tm // tc, tc).transpose(0, 2, 1, 3).reshape(rows // tc, 1, 2 * tc)
    return dest_c, ys


def kernel(x, c, ctx, c_ctx, w_ada, b_ada, w_in, w_fourier, attn_sink, w_pool, pool_scale, w_sgu, b_sgu,
           w_out, ln1_g, ln1_b, w_router_group, w_router_expert, w_exp_gate, w_exp_up, w_exp_down,
           ln2_g, ln2_b):
    b, s, d = x.shape
    n_ctx = ctx.shape[1]
    n_layers = w_in.shape[0]
    tm = ROW_TILE
    assert n_layers == DEPTH and s == FFT_R * FFT_R and s % tm == 0 and b + 1 <= SUBLANES
    assert (b * (s + n_ctx)) % tm == 0 and n_ctx % SGU_CHUNK == 0
    cond = jnp.concatenate([c, c_ctx[None, :], jnp.zeros((SUBLANES - b - 1, d), F32)], axis=0)
    mod_all = _ada(cond, w_ada, b_ada[:, None, :])
    cos_t, sin_t = _rope_tables(s)
    x2 = x.reshape(b * s, d)
    c2 = ctx.reshape(b * n_ctx, d)
    n_sgu = w_sgu.shape[1]
    for layer in range(n_layers):
        last = layer == n_layers - 1
        mod = mod_all[layer]
        w_in_l = w_in[layer].astype(BF16)
        wf = w_fourier[layer].astype(BF16)
        w_pool_bd = jax.scipy.linalg.block_diag(*[w_pool[layer, g] for g in range(w_pool.shape[1])]).astype(BF16)
        w_sgu_stack = w_sgu[layer].reshape(n_sgu * SGU_CHUNK, SGU_CHUNK).astype(BF16)
        b_sgu_exp = jnp.repeat(b_sgu[layer].T, GROUP_W // n_sgu, axis=1)
        w_router = jnp.concatenate([w_router_expert[layer].reshape(d, N_EXPERTS), w_router_group[layer]], axis=1)
        w_router = jnp.pad(w_router, ((0, 0), (0, LANES - w_router.shape[1])))
        wr_hi = w_router.astype(BF16)
        w_route = jnp.concatenate([wr_hi, (w_router - wr_hi.astype(F32)).astype(BF16)], axis=1)
        merge_consts = (mod, w_pool_bd, pool_scale[layer][None, :], w_sgu_stack, b_sgu_exp,
                        w_out[layer].astype(BF16), ln1_g[layer][None, :], ln1_b[layer][None, :], w_route)
        sink = attn_sink[layer]

        a, q, qs, k, v, p, ug = _proj(x2, mod, w_in_l, cos_t, sin_t, mod_row=None, rows_per_batch=s,
                                      rope=True, tm=tm, a_pitch=FFT_PITCH)
        ac, qc, qsc, kc, vc, pc, ugc = _proj(c2, mod, w_in_l, cos_t, sin_t, mod_row=b, rows_per_batch=n_ctx,
                                             rope=False, tm=n_ctx, a_pitch=FFT_R)
        y_four = _fourier(a, wf, s)
        y_attn = _attention(sink, q, qs, k, v, kc, vc, seq=s, n_ctx=n_ctx, band=True)
        total = b * s + (0 if last else b * n_ctx)
        merged = _merge(x2, p, ug, y_four, y_attn, *merge_consts, (),
                        mod_row=None, seq=s, tm=tm, row_off=0, total_rows=total)
        if not last:
            yc_four = _fourier_small(ac, wf, n_ctx)
            yc_attn = _attention(sink, qc, qsc, kc, vc, kc, vc, seq=n_ctx, n_ctx=n_ctx, band=False)
            merged = _merge(c2, pc, ugc, yc_four, yc_attn, *merge_consts, tuple(merged),
                            mod_row=b, seq=n_ctx, tm=n_ctx, row_off=b * s, total_rows=total)
        x1, h2_tiles, route, route_t = merged
        dest, ys = _moe(route_t, h2_tiles, w_exp_gate, w_exp_up, w_exp_down, layer)
        ln_g, ln_b = ln2_g[layer][None, :], ln2_b[layer][None, :]
        x2 = _combine(dest, x1, route, mod, ln_g, ln_b, ys, row_off=0, rows=b * s,
                      mod_row=None, rows_per_batch=s)
        if not last:
            c2 = _combine(dest, x1, route, mod, ln_g, ln_b, ys, row_off=b * s,
                          rows=b * n_ctx, mod_row=b, rows_per_batch=n_ctx)
    return x2.reshape(b, s, d)
```

```python
import functools
import math

import numpy as np
import jax
import jax.numpy as jnp
from jax import lax
from jax.experimental import pallas as pl
from jax.experimental.pallas import tpu as pltpu

GRID_W = 64
HEAD_DIM = 64
GROUP_W = 256
KV_W = 128
WINDOW = 128
POOL_WINDOWS = (2, 4, 8, 16)
SGU_CHUNK = 128
N_GROUPS = 4
EXPERTS_PER_GROUP = 8
N_EXPERTS = 32
ROPE_BASE = 10000.0
LN_EPS = 1e-6
NEG_INF = -1e30
DEPTH = 2
RES_ALPHA = (2 * DEPTH) ** 0.25

LANES = 128
SUBLANES = 8
VMEM_LIMIT = 48 * 1024 * 1024

ROW_TILE = 1024
Q_BLOCK = 2048
COMBINE_TILE = 256
WEIGHT_SLOTS = 3
MOE_BLOCK = 512
MOE_PART = 256
MOE_STEP_BLOCKS = 4
FFT_R = 64
FFT_PITCH = 72
FFT_UNROLL = 16
DMA_UNROLL = 8
MERGE_PART = 512

BF16 = jnp.bfloat16
F32 = jnp.float32


def _cparams(*sem):
    return pltpu.CompilerParams(dimension_semantics=sem, vmem_limit_bytes=VMEM_LIMIT)


def _dot(a, b):
    return jnp.dot(a, b, preferred_element_type=F32)


def _dot_nt(a, b):
    return lax.dot_general(a, b, (((1,), (1,)), ((), ())), preferred_element_type=F32)


def _layer_norm(t):
    mu = jnp.mean(t, axis=-1, keepdims=True)
    d = t - mu
    var = jnp.mean(d * d, axis=-1, keepdims=True)
    return d * lax.rsqrt(var + LN_EPS)


def _silu(t):
    return t * (1.0 / (1.0 + jnp.exp(-t)))


def _gelu(t):
    return 0.5 * t * (1.0 + lax.erf(t * (1.0 / math.sqrt(2.0))))


ROW_WORDS = 4
HI_MASK = 0xFFFF0000


def _pack_rows(t):
    half = t.shape[1] // 2
    lo = lax.bitcast_convert_type(t[:, :half].astype(BF16).astype(F32), jnp.uint32)
    hi = lax.bitcast_convert_type(t[:, half:].astype(BF16).astype(F32), jnp.uint32)
    return (lo >> 16) | (hi & jnp.uint32(HI_MASK))


def _unpack_rows(w):
    return (lax.bitcast_convert_type(w << 16, F32),
            lax.bitcast_convert_type(w & jnp.uint32(HI_MASK), F32))


def _store_rows(ref, t, first=0):
    w = _pack_rows(t)
    for j in range(ROW_WORDS):
        ref[pl.ds(first * ROW_WORDS + j, t.shape[0], stride=ROW_WORDS), :] = w[:, j * LANES:(j + 1) * LANES]


def _load_rows(ref, first, m):
    w = jnp.concatenate([ref[pl.ds(first * ROW_WORDS + j, m, stride=ROW_WORDS), :] for j in range(ROW_WORDS)],
                        axis=1)
    return _unpack_rows(w)


def _ada_kernel(c_ref, w_ref, b_ref, o_ref):
    s = _silu(c_ref[...]).astype(BF16)
    o_ref[...] = _dot(s, w_ref[...].astype(BF16)) + b_ref[...]


def _ada(cond, w_ada, b_ada):
    n_layers, d, n = w_ada.shape
    tn = n // 4
    return pl.pallas_call(
        _ada_kernel,
        grid=(n_layers, n // tn),
        in_specs=[
            pl.BlockSpec((SUBLANES, d), lambda l, j: (0, 0)),
            pl.BlockSpec((None, d, tn), lambda l, j: (l, 0, j)),
            pl.BlockSpec((None, 1, tn), lambda l, j: (l, 0, j)),
        ],
        out_specs=pl.BlockSpec((None, SUBLANES, tn), lambda l, j: (l, 0, j)),
        out_shape=jax.ShapeDtypeStruct((n_layers, SUBLANES, n), F32),
        compiler_params=_cparams("arbitrary", "arbitrary"),
        name="ada",
    )(cond, w_ada, b_ada)


def _rope(t, cos_t, sin_t):
    lane = lax.broadcasted_iota(jnp.int32, t.shape, 1)
    first = (lane % 32) < 16
    partner = jnp.where(first, pltpu.roll(t, LANES - 16, axis=1), pltpu.roll(t, 16, axis=1))
    return t * cos_t + partner * sin_t


def _proj_body(x, m, w_ref, cos_ref, sin_ref, outs, *, rope, a_pitch):
    a_ref, q_ref, qs_ref, k_ref, v_ref, p_ref, ug_ref = outs
    tm, d = x.shape
    shift, scale = m[:, 0:d], m[:, d:2 * d]
    h = _layer_norm(x) * (1.0 + scale) + shift
    z = _dot(h.astype(BF16), w_ref[...])
    pad = jnp.zeros((a_pitch - FFT_R, LANES), F32)
    for g in range(tm // FFT_R):
        for hf in range(2):
            grp = z[g * FFT_R:(g + 1) * FFT_R, hf * LANES:(hf + 1) * LANES]
            if a_pitch > FFT_R:
                grp = jnp.concatenate([grp, pad], axis=0)
            a_ref[hf, g * a_pitch:(g + 1) * a_pitch, :] = grp
    q0, q1 = z[:, 256:384], z[:, 384:512]
    k = z[:, 512:640]
    if rope:
        cos_t, sin_t = cos_ref[...], sin_ref[...]
        q0, q1, k = _rope(q0, cos_t, sin_t), _rope(q1, cos_t, sin_t), _rope(k, cos_t, sin_t)
    q_ref[:, 0:128] = q0.astype(BF16)
    q_ref[:, 128:256] = q1.astype(BF16)
    qs_ref[:, 0:128] = pltpu.roll(q0, HEAD_DIM, axis=1).astype(BF16)
    qs_ref[:, 128:256] = pltpu.roll(q1, HEAD_DIM, axis=1).astype(BF16)
    k_ref[...] = k.astype(BF16)
    v_ref[...] = z[:, 640:768].astype(BF16)
    p_ref[...] = z[:, 768:1024].astype(BF16)
    ug_ref[...] = z[:, 1024:1536].astype(BF16)


def _mod_row(mod_ref, mod_row, tm, rows_per_batch):
    row = (pl.program_id(0) * tm) // rows_per_batch if mod_row is None else mod_row
    return mod_ref[pl.ds(row, 1), :]


def _proj_kernel(x_ref, mod_ref, w_ref, cos_ref, sin_ref, *outs, mod_row, rows_per_batch, rope, a_pitch):
    m = _mod_row(mod_ref, mod_row, x_ref.shape[0], rows_per_batch)
    _proj_body(x_ref[...], m, w_ref, cos_ref, sin_ref, outs, rope=rope, a_pitch=a_pitch)


def _proj_specs(rows, tm, a_pitch, seq_steps):
    row_spec = lambda w: pl.BlockSpec((tm, w), lambda i: (i, 0))
    out_w = (256, 256, 128, 128, 256, 512)
    ta = tm // FFT_R * a_pitch
    out_specs = [pl.BlockSpec((2, ta, LANES), lambda i: (0, i, 0))] + [row_spec(w) for w in out_w]
    out_shape = ([jax.ShapeDtypeStruct((2, rows // FFT_R * a_pitch, LANES), F32)]
                 + [jax.ShapeDtypeStruct((rows, w), BF16) for w in out_w])
    table_spec = pl.BlockSpec((tm, LANES), lambda i: (i % seq_steps, 0))
    return table_spec, out_specs, out_shape


def _proj(x2, mod, w_in, cos_t, sin_t, *, mod_row, rows_per_batch, rope, tm, a_pitch):
    rows, d = x2.shape
    kern = functools.partial(_proj_kernel, mod_row=mod_row, rows_per_batch=rows_per_batch,
                             rope=rope, a_pitch=a_pitch)
    table_spec, out_specs, out_shape = _proj_specs(rows, tm, a_pitch, cos_t.shape[0] // tm)
    return pl.pallas_call(
        kern,
        grid=(rows // tm,),
        in_specs=[
            pl.BlockSpec((tm, d), lambda i: (i, 0)),
            pl.BlockSpec(mod.shape, lambda i: (0, 0)),
            pl.BlockSpec(w_in.shape, lambda i: (0, 0)),
            table_spec, table_spec,
        ],
        out_specs=out_specs,
        out_shape=out_shape,
        compiler_params=_cparams("arbitrary"),
        name="proj",
    )(x2, mod, w_in, cos_t, sin_t)


def _fft_tables(n_pos):
    r = FFT_R
    assert n_pos == r * r
    kb = np.arange(r)[None, :, None]
    na = np.arange(r)[:, None, None]
    nb = np.arange(r)[None, None, :]
    ang = 2.0 * np.pi * ((kb * (na + r * nb)) % n_pos) / n_pos
    m1 = np.concatenate([np.cos(ang), -np.sin(ang)], axis=1)
    ka = np.arange(r)[:, None]
    n2 = np.arange(r)[None, :]
    ang2 = 2.0 * np.pi * ((ka * n2) % r) / r
    c2, s2 = np.cos(ang2), np.sin(ang2)
    w2 = np.block([[c2, s2], [-s2, c2]])
    return m1, w2


def _channel_tables(n_pos):
    h = HEAD_DIM
    c = np.arange(h)
    ang = 2.0 * np.pi * ((c[:, None] * c[None, :]) % h) / h
    scale = 1.0 / math.sqrt(n_pos * h)
    eye = np.eye(GROUP_W // h)
    cc = np.kron(eye, np.cos(ang)) * scale
    ss = np.kron(eye, np.sin(ang)) * scale
    return np.concatenate([cc, ss], axis=0)


def _fourier_kernel(a_ref, m1_ref, w2_ref, ch_ref, wf_ref, o_ref, z_ref, y_ref):
    r, pt = FFT_R, FFT_PITCH

    def step1(i, c):
        for u in range(FFT_UNROLL):
            na = i * FFT_UNROLL + u
            rows = jnp.concatenate([a_ref[0, pl.ds(na, r, stride=pt), :],
                                    a_ref[1, pl.ds(na, r, stride=pt), :]], axis=1)
            z = _dot(m1_ref[na], rows.astype(BF16))
            base = pl.multiple_of(na * pt, SUBLANES)
            z_ref[0, pl.ds(base, r), :] = z[0:r, 0:LANES]
            z_ref[1, pl.ds(base, r), :] = z[0:r, LANES:]
            z_ref[2, pl.ds(base, r), :] = z[r:, 0:LANES]
            z_ref[3, pl.ds(base, r), :] = z[r:, LANES:]
        return c

    lax.fori_loop(0, r // FFT_UNROLL, step1, 0)

    def step2(i, c):
        for u in range(FFT_UNROLL):
            kb = i * FFT_UNROLL + u
            q = [z_ref[j, pl.ds(kb, r, stride=pt), :] for j in range(4)]
            zs = jnp.concatenate([jnp.concatenate(q[0:2], axis=1),
                                  jnp.concatenate(q[2:4], axis=1)], axis=0)
            y = _dot(w2_ref[...], zs.astype(BF16))
            base = pl.multiple_of(kb * r, r)
            y_ref[0, pl.ds(base, r), :] = y[0:r, 0:LANES]
            y_ref[1, pl.ds(base, r), :] = y[0:r, LANES:]
            y_ref[2, pl.ds(base, r), :] = y[r:, 0:LANES]
            y_ref[3, pl.ds(base, r), :] = y[r:, LANES:]
        return c

    lax.fori_loop(0, r // FFT_UNROLL, step2, 0)

    chunk = 8 * r
    for cidx in range(r * r // chunk):
        yy = jnp.concatenate([y_ref[j, cidx * chunk:(cidx + 1) * chunk, :] for j in range(4)], axis=1)
        f = _dot(yy.astype(BF16), ch_ref[...])
        g = _dot(f.astype(BF16), wf_ref[...])
        for gi in range(chunk // r):
            kb = cidx * (chunk // r) + gi
            z_ref[0, kb * pt:kb * pt + r, :] = g[gi * r:(gi + 1) * r, 0:LANES]
            z_ref[1, kb * pt:kb * pt + r, :] = g[gi * r:(gi + 1) * r, LANES:]

    def step3(i, c):
        for u in range(FFT_UNROLL):
            ka = i * FFT_UNROLL + u
            base = pl.multiple_of(ka * r, r)
            o_ref[pl.ds(base, r), 0:LANES] = z_ref[0, pl.ds(ka, r, stride=pt), :]
            o_ref[pl.ds(base, r), LANES:] = z_ref[1, pl.ds(ka, r, stride=pt), :]
        return c

    lax.fori_loop(0, r // FFT_UNROLL, step3, 0)


def _fourier(a3, w_fourier, n_pos):
    rows = a3.shape[1] // FFT_PITCH * FFT_R
    gw = GROUP_W
    m1, w2 = _fft_tables(n_pos)
    ch = _channel_tables(n_pos)
    const = lambda shape: pl.BlockSpec(shape, lambda b: (0,) * len(shape))
    return pl.pallas_call(
        _fourier_kernel,
        grid=(rows // n_pos,),
        in_specs=[
            pl.BlockSpec((2, FFT_R * FFT_PITCH, LANES), lambda b: (0, b, 0)),
            const(m1.shape), const(w2.shape), const(ch.shape), const(w_fourier.shape),
        ],
        out_specs=pl.BlockSpec((n_pos, gw), lambda b: (b, 0)),
        out_shape=jax.ShapeDtypeStruct((rows, gw), F32),
        scratch_shapes=[pltpu.VMEM((4, FFT_R * FFT_PITCH, LANES), F32), pltpu.VMEM((4, n_pos, LANES), F32)],
        compiler_params=_cparams("arbitrary"),
        name="fourier",
    )(a3, jnp.asarray(m1, BF16), jnp.asarray(w2, BF16), jnp.asarray(ch, BF16), w_fourier)


def _fourier_small_kernel(a_ref, cs_ref, ch_ref, wf_ref, o_ref):
    n = a_ref.shape[1]
    a = jnp.concatenate([a_ref[0], a_ref[1]], axis=1)
    pq = _dot(cs_ref[...], a.astype(BF16))
    y = jnp.concatenate([pq[0:n], pq[n:2 * n]], axis=1).astype(BF16)
    f = _dot(y, ch_ref[...])
    o_ref[...] = _dot(f.astype(BF16), wf_ref[...])


def _fourier_small(a3, w_fourier, n_pos):
    _, rows, _ = a3.shape
    gw = GROUP_W
    k = np.arange(n_pos)
    ang = 2.0 * np.pi * ((k[:, None] * k[None, :]) % n_pos) / n_pos
    cs = np.concatenate([np.cos(ang), -np.sin(ang)], axis=0)
    ch = _channel_tables(n_pos)
    const = lambda shape: pl.BlockSpec(shape, lambda b: (0,) * len(shape))
    return pl.pallas_call(
        _fourier_small_kernel,
        grid=(rows // n_pos,),
        in_specs=[pl.BlockSpec((2, n_pos, LANES), lambda b: (0, b, 0)),
                  const(cs.shape), const(ch.shape), const(w_fourier.shape)],
        out_specs=pl.BlockSpec((n_pos, gw), lambda b: (b, 0)),
        out_shape=jax.ShapeDtypeStruct((rows, gw), F32),
        compiler_params=_cparams("arbitrary"),
        name="fourier_ctx",
    )(a3, jnp.asarray(cs, BF16), jnp.asarray(ch, BF16), w_fourier)


ATTN_SUB = 128


def _attn_kernel(sink_ref, q_ref, qs_ref, k_ref, v_ref, kc_ref, vc_ref, o_ref, *, band, seq):
    qb = q_ref.shape[0]
    sub = ATTN_SUB
    lane = lax.broadcasted_iota(jnp.int32, (1, LANES), 1)
    lo_half = lane < HEAD_DIM
    zero = jnp.zeros((), BF16)
    scale = jnp.asarray(HEAD_DIM ** -0.5, BF16)
    kw = sub + 2 * WINDOW
    for sb in range(qb // sub):
        rows = slice(sb * sub, (sb + 1) * sub)
        qa0, qa1 = q_ref[rows, 0:LANES], q_ref[rows, LANES:]
        qs0, qs1 = qs_ref[rows, 0:LANES], qs_ref[rows, LANES:]
        q_all = jnp.concatenate([jnp.where(lo_half, qa0, zero), jnp.where(lo_half, qs0, zero),
                                 jnp.where(lo_half, zero, qs1), jnp.where(lo_half, zero, qa1)], axis=0) * scale
        if band:
            p0 = pl.program_id(1) * qb + sb * sub
            start = pl.multiple_of(jnp.clip(p0 - WINDOW, 0, seq - kw), WINDOW)
            qpos = p0 + lax.broadcasted_iota(jnp.int32, (sub, 1), 0)
            kpos = start + lax.broadcasted_iota(jnp.int32, (1, kw), 1)
            bias = jnp.where(jnp.abs(qpos - kpos) <= WINDOW, 0.0, NEG_INF)
            keys = jnp.concatenate([k_ref[pl.ds(start, kw), :], kc_ref[...]], axis=0)
            vals = jnp.concatenate([v_ref[pl.ds(start, kw), :], vc_ref[...]], axis=0)
        else:
            keys, vals = kc_ref[...], vc_ref[...]
        s_all = _dot_nt(q_all, keys)
        probs, dens = [], []
        for h in range(4):
            s = s_all[h * sub:(h + 1) * sub, :]
            sink = sink_ref[h]
            if band:
                s = jnp.concatenate([s[:, 0:kw] + bias, s[:, kw:]], axis=1)
            m = jnp.maximum(jnp.max(s, axis=1, keepdims=True), sink)
            p = jnp.exp(s - m)
            dens.append(jnp.sum(p, axis=1, keepdims=True) + jnp.exp(sink - m))
            probs.append(p.astype(BF16))
        o_all = _dot(jnp.concatenate(probs, axis=0), vals)
        o = [o_all[h * sub:(h + 1) * sub, :] / dens[h] for h in range(4)]
        o_ref[rows, 0:LANES] = jnp.where(lo_half, o[0], pltpu.roll(o[1], HEAD_DIM, axis=1)).astype(BF16)
        o_ref[rows, LANES:] = jnp.where(lo_half, pltpu.roll(o[2], HEAD_DIM, axis=1), o[3]).astype(BF16)


def _attention(sink, q, qs, k, v, kc, vc, *, seq, n_ctx, band):
    rows = q.shape[0]
    n_batch = rows // seq
    qb = Q_BLOCK if band else seq
    steps = seq // qb
    kern = functools.partial(_attn_kernel, band=band, seq=seq)
    seq_spec = pl.BlockSpec((seq, KV_W), lambda b, i: (b, 0))
    ctx_spec = pl.BlockSpec((n_ctx, KV_W), lambda b, i: (b, 0))
    q_spec = pl.BlockSpec((qb, GROUP_W), lambda b, i: (b * steps + i, 0))
    return pl.pallas_call(
        kern,
        grid=(n_batch, steps),
        in_specs=[pl.BlockSpec(memory_space=pltpu.SMEM), q_spec, q_spec,
                  seq_spec, seq_spec, ctx_spec, ctx_spec],
        out_specs=q_spec,
        out_shape=jax.ShapeDtypeStruct((rows, GROUP_W), BF16),
        compiler_params=_cparams("arbitrary", "arbitrary"),
        name="attn" if band else "attn_ctx",
    )(sink, q, qs, k, v, kc, vc)


POOL_HALO = max(POOL_WINDOWS) // 2


def _pool(p_ref, t0, tm, seq):
    halo = POOL_HALO
    pack = 2 * SUBLANES
    t0 = pl.multiple_of(t0, pack)
    main = p_ref[pl.ds(t0, tm), :].astype(F32)
    lo = pl.multiple_of(jnp.maximum(t0 - pack, 0), pack)
    hi = pl.multiple_of(jnp.minimum(t0 + tm, seq - pack), pack)
    prev = p_ref[pl.ds(lo, pack), :].astype(F32)[pack - halo:, :]
    nxt = p_ref[pl.ds(hi, pack), :].astype(F32)[:halo, :]
    prev = jnp.where(t0 > 0, prev, 0.0)
    nxt = jnp.where(t0 + tm < seq, nxt, 0.0)
    full = jnp.concatenate([prev, main, nxt], axis=0)
    n = tm + 2 * halo
    gch = GROUP_W // len(POOL_WINDOWS)
    first = lax.broadcasted_iota(jnp.int32, (1, LANES), 1) < gch
    means = []
    for hf in range(GROUP_W // LANES):
        wa, wb = POOL_WINDOWS[2 * hf], POOL_WINDOWS[2 * hf + 1]
        x = full[:, hf * LANES:(hf + 1) * LANES]
        sums, w, s = {}, 2, pltpu.roll(x, 1, axis=0) + x
        sums[w] = s
        while w < wb:
            s = pltpu.roll(s, w // 2, axis=0) + pltpu.roll(s, n - w // 2, axis=0)
            w *= 2
            sums[w] = s
        means.append(jnp.where(first, sums[wa] * (1.0 / wa), sums[wb] * (1.0 / wb))[halo:halo + tm, :])
    mean = jnp.concatenate(means, axis=1)
    win = jnp.concatenate([jnp.full((1, gch), w, jnp.int32) for w in POOL_WINDOWS], axis=1)

    def rescale(rows, first_pos):
        pos = first_pos + lax.broadcasted_iota(jnp.int32, (halo, 1), 0)
        cnt = jnp.minimum(pos + win // 2, seq) - jnp.maximum(pos - win // 2, 0)
        return rows * (win.astype(F32) / cnt.astype(F32))

    mean = jnp.concatenate([rescale(mean[:halo], t0), mean[halo:tm - halo],
                            rescale(mean[tm - halo:], t0 + tm - halo)], axis=0)
    return mean - main


def _route(logits):
    tm = logits.shape[0]
    lt = logits.T
    gl = lt[N_EXPERTS:N_EXPERTS + N_GROUPS]
    sub_g = lax.broadcasted_iota(jnp.int32, gl.shape, 0)
    gmax = jnp.max(gl, axis=0, keepdims=True)
    grp = jnp.min(jnp.where(gl == gmax, sub_g, N_GROUPS), axis=0, keepdims=True)
    gate_group = 1.0 / jnp.sum(jnp.exp(gl - gmax), axis=0, keepdims=True)
    el = lt[0:EXPERTS_PER_GROUP]
    for g in range(1, N_GROUPS):
        el = jnp.where(grp == g, lt[g * EXPERTS_PER_GROUP:(g + 1) * EXPERTS_PER_GROUP], el)
    sub = lax.broadcasted_iota(jnp.int32, el.shape, 0)
    m1 = jnp.max(el, axis=0, keepdims=True)
    i1 = jnp.min(jnp.where(el == m1, sub, EXPERTS_PER_GROUP), axis=0, keepdims=True)
    el2 = jnp.where(sub == i1, -jnp.inf, el)
    m2 = jnp.max(el2, axis=0, keepdims=True)
    i2 = jnp.min(jnp.where(el2 == m2, sub, EXPERTS_PER_GROUP), axis=0, keepdims=True)
    r = jnp.exp(m2 - m1)
    g1 = gate_group / (1.0 + r)
    g2 = g1 * r
    e1 = (grp * EXPERTS_PER_GROUP + i1).astype(F32)
    e2 = (grp * EXPERTS_PER_GROUP + i2).astype(F32)
    rows = jnp.where(sub == 0, e1, jnp.where(sub == 1, e2, jnp.where(sub == 2, g1, jnp.where(sub == 3, g2, 0.0))))
    cols = jnp.concatenate([rows, jnp.zeros((LANES - rows.shape[0], tm), F32)], axis=0).T
    return cols, rows


def _merge_kernel(x_ref, p_ref, ug_ref, yf_ref, ya_ref, mod_ref, wpool_ref, pscale_ref, wsgu_ref, bsgu_ref,
                  wout_ref, lng_ref, lnb_ref, wr_ref, *rest,
                  mod_row, seq, n_alias):
    x1_ref, h2_ref, route_ref, route_t_ref = rest[n_alias:]
    tm, d = x_ref.shape
    if mod_row is None:
        row = pl.program_id(0)
    else:
        row = mod_row
    t0 = pl.multiple_of(pl.program_id(1) * tm, tm)
    m = mod_ref[pl.ds(row, 1), :]
    gate1, shift2, scale2 = m[:, 2 * d:3 * d], m[:, 3 * d:4 * d], m[:, 4 * d:5 * d]
    lane = lax.broadcasted_iota(jnp.int32, (1, GROUP_W), 1)
    n_heads = wsgu_ref.shape[0] // SGU_CHUNK
    head = lane // (GROUP_W // n_heads)

    pm = min(tm, MERGE_PART)
    for part in range(tm // pm):
        r0 = part * pm
        rows = slice(r0, r0 + pm)
        pooled = _pool(p_ref, t0 + r0, pm, seq)
        y_pool = _dot(pooled.astype(BF16), wpool_ref[...]) * pscale_ref[...]

        ug = ug_ref[rows, :].astype(F32)
        u = _gelu(ug[:, 0:GROUP_W])
        v = _layer_norm(_gelu(ug[:, GROUP_W:])).astype(BF16)
        mixed = []
        for cidx in range(pm // SGU_CHUNK):
            vc = v[cidx * SGU_CHUNK:(cidx + 1) * SGU_CHUNK, :]
            full = _dot(wsgu_ref[...], vc)
            mc = bsgu_ref[...]
            for hd in range(n_heads):
                mc = mc + jnp.where(head == hd, full[hd * SGU_CHUNK:(hd + 1) * SGU_CHUNK, :], 0.0)
            mixed.append(mc)
        y_sgu = u * jnp.concatenate(mixed, axis=0)

        cat = jnp.concatenate([yf_ref[rows, :].astype(BF16), ya_ref[rows, :], y_pool.astype(BF16),
                               y_sgu.astype(BF16)], axis=1)
        y = _dot(cat, wout_ref[...])
        x1 = _layer_norm(RES_ALPHA * x_ref[rows, :] + gate1 * y) * lng_ref[...] + lnb_ref[...]
        x1_ref[rows, :] = x1
        h2 = _layer_norm(x1) * (1.0 + scale2) + shift2
        _store_rows(h2_ref, h2, first=r0)
        lg = _dot(h2.astype(BF16), wr_ref[...])
        route_ref[rows, :], route_t_ref[:, rows] = _route(lg[:, 0:LANES] + lg[:, LANES:])


def _merge(x2, p, ug, y_four, y_attn, mod, w_pool_bd, pool_scale, w_sgu_stack, b_sgu_exp, w_out,
           ln_g, ln_b, w_route, aliased, *, mod_row, seq, tm, row_off, total_rows):
    rows, d = x2.shape
    n_batch, steps = rows // seq, seq // tm
    off = row_off // tm
    kern = functools.partial(_merge_kernel, mod_row=mod_row, seq=seq, n_alias=len(aliased))
    row_spec = lambda w: pl.BlockSpec((tm, w), lambda b, i: (b * steps + i, 0))
    const = lambda a: pl.BlockSpec(a.shape, lambda b, i: (0,) * a.ndim)
    consts = (mod, w_pool_bd, pool_scale, w_sgu_stack, b_sgu_exp, w_out, ln_g, ln_b, w_route)
    n_in = 5 + len(consts)
    out_shapes = [jax.ShapeDtypeStruct((total_rows, d), F32),
                  jax.ShapeDtypeStruct((total_rows * ROW_WORDS, LANES), jnp.uint32),
                  jax.ShapeDtypeStruct((total_rows, LANES), F32),
                  jax.ShapeDtypeStruct((SUBLANES, total_rows), F32)]
    out_specs = [pl.BlockSpec((tm, d), lambda b, i: (off + b * steps + i, 0)),
                 pl.BlockSpec((tm * ROW_WORDS, LANES), lambda b, i: (off + b * steps + i, 0)),
                 pl.BlockSpec((tm, LANES), lambda b, i: (off + b * steps + i, 0)),
                 pl.BlockSpec((SUBLANES, tm), lambda b, i: (0, off + b * steps + i))]
    return pl.pallas_call(
        kern,
        grid=(n_batch, steps),
        in_specs=[row_spec(d), pl.BlockSpec((seq, GROUP_W), lambda b, i: (b, 0)),
                  row_spec(2 * GROUP_W), row_spec(GROUP_W), row_spec(GROUP_W)]
                 + [const(a) for a in consts]
                 + [pl.BlockSpec(memory_space=pl.ANY)] * len(aliased),
        out_specs=out_specs,
        out_shape=out_shapes,
        input_output_aliases={n_in + k: k for k in range(len(aliased))},
        compiler_params=_cparams("arbitrary", "arbitrary"),
        name="merge",
    )(x2, p, ug, y_four, y_attn, *consts, *aliased)


def _plan_kernel(route_ref, dest_ref, cnt_out_ref, cnt_ref, start_ref, carry_ref):
    ph, t = pl.program_id(0), pl.program_id(1)
    tm = route_ref.shape[1]
    rt = route_ref[...]
    e1 = rt[0:1, :].astype(jnp.int32)
    e2 = rt[1:2, :].astype(jnp.int32)
    sub = lax.broadcasted_iota(jnp.int32, (N_EXPERTS, tm), 0)
    hit1, hit2 = sub == e1, sub == e2
    onehot = jnp.where(hit1 | hit2, 1.0, 0.0)
    tile_cnt = jnp.sum(onehot, axis=1, keepdims=True)

    @pl.when((ph == 0) & (t == 0))
    def _():
        cnt_ref[...] = jnp.zeros_like(cnt_ref)

    @pl.when(ph == 0)
    def _():
        cnt_ref[...] += tile_cnt

    @pl.when((ph == 1) & (t == 0))
    def _():
        cnt = cnt_ref[...]
        padded = jnp.floor((cnt + (MOE_BLOCK - 1.0)) * (1.0 / MOE_BLOCK)) * MOE_BLOCK
        row = lax.broadcasted_iota(jnp.int32, cnt.shape, 0)
        incl = padded
        sh = 1
        while sh < N_EXPERTS:
            incl = incl + jnp.where(row >= sh, pltpu.roll(incl, sh, axis=0), 0.0)
            sh *= 2
        start_ref[...] = incl - padded
        carry_ref[...] = jnp.zeros_like(carry_ref)
        cnt_out_ref[...] = cnt

    @pl.when(ph == 1)
    def _():
        r_i = lax.broadcasted_iota(jnp.int32, (tm, tm), 0)
        c_i = lax.broadcasted_iota(jnp.int32, (tm, tm), 1)
        before = jnp.where(r_i < c_i, 1.0, 0.0).astype(BF16)
        rank = _dot(onehot.astype(BF16), before)
        base = start_ref[:, 0:1] + carry_ref[:, 0:1] + rank
        d1 = jnp.sum(jnp.where(hit1, base, 0.0), axis=0, keepdims=True)
        d2 = jnp.sum(jnp.where(hit2, base, 0.0), axis=0, keepdims=True)
        sub8 = lax.broadcasted_iota(jnp.int32, (SUBLANES, tm), 0)
        dest_ref[...] = jnp.where(sub8 == 0, d1, d2).astype(jnp.int32)
        carry_ref[...] += tile_cnt


def _plan(route_t, tm):
    rows = route_t.shape[1]
    n_t = rows // tm
    return pl.pallas_call(
        _plan_kernel,
        grid=(2, n_t),
        in_specs=[pl.BlockSpec((SUBLANES, tm), lambda ph, t: (0, t))],
        out_specs=[pl.BlockSpec((None, SUBLANES, tm), lambda ph, t: (t * ph, 0, 0)),
                   pl.BlockSpec((N_EXPERTS, LANES), lambda ph, t: (0, 0))],
        out_shape=[jax.ShapeDtypeStruct((n_t, SUBLANES, tm), jnp.int32),
                   jax.ShapeDtypeStruct((N_EXPERTS, LANES), F32)],
        scratch_shapes=[pltpu.VMEM((N_EXPERTS, LANES), F32)] * 3,
        compiler_params=_cparams("arbitrary", "arbitrary"),
        name="moe_plan",
    )(route_t)


def _row_copy(src_ref, src_row, dst_ref, dst_row, sem):
    return pltpu.make_async_copy(
        src_ref.at[pl.ds(pl.multiple_of(src_row * ROW_WORDS, ROW_WORDS), ROW_WORDS)],
        dst_ref.at[pl.ds(pl.multiple_of(dst_row * ROW_WORDS, ROW_WORDS), ROW_WORDS)], sem)


def _dispatch_kernel(dest_ref, h2_ref, xs_ref, sem, *, tm):
    def body(r4, c):
        for u in range(DMA_UNROLL):
            r = r4 * DMA_UNROLL + u
            for k in range(2):
                _row_copy(h2_ref, r, xs_ref, dest_ref[0, 0, k * tm + r], sem).start(priority=k)
        return c

    lax.fori_loop(0, tm // DMA_UNROLL, body, 0)
    for k in range(2):
        pltpu.make_async_copy(h2_ref, xs_ref.at[pl.ds(0, tm * ROW_WORDS)], sem).wait()


def _dispatch(dest, h2_tiles, n_slots, tm):
    n_t = dest.shape[0]
    return pl.pallas_call(
        functools.partial(_dispatch_kernel, tm=tm),
        grid=(n_t,),
        in_specs=[pl.BlockSpec((1, 1, 2 * tm), lambda i: (i, 0, 0), memory_space=pltpu.SMEM),
                  pl.BlockSpec((tm * ROW_WORDS, LANES), lambda i: (i, 0))],
        out_specs=pl.BlockSpec(memory_space=pl.ANY),
        out_shape=jax.ShapeDtypeStruct((n_slots * ROW_WORDS, LANES), jnp.uint32),
        scratch_shapes=[pltpu.SemaphoreType.DMA],
        compiler_params=_cparams("arbitrary"),
        name="moe_dispatch",
    )(dest, h2_tiles)


def _expert_kernel(be_ref, bn_ref, first_ref, slot_ref, nxt_ref, nxt2_ref, _blk_ref,
                   x_ref, wg_hbm, wu_hbm, wd_hbm, y_ref,
                   wg_buf, wu_buf, wd_buf, wg_bf, wu_bf, wd_bf, sem, *, layer):
    mb = MOE_BLOCK
    half = wg_bf.shape[0] // 2

    def weight_copies(e, s):
        return [pltpu.make_async_copy(wg_hbm.at[layer, e], wg_buf.at[s], sem.at[s]),
                pltpu.make_async_copy(wu_hbm.at[layer, e], wu_buf.at[s], sem.at[s]),
                pltpu.make_async_copy(wd_hbm.at[layer, e], wd_buf.at[s], sem.at[s])]

    @pl.when(pl.program_id(0) == 0)
    def _():
        for cp in weight_copies(be_ref[0], 0):
            cp.start()

        @pl.when(nxt_ref[0] >= 0)
        def _():
            for cp in weight_copies(nxt_ref[0], 1):
                cp.start(priority=1)

    def block(i, row0):
        @pl.when(first_ref[i] == 1)
        def _():
            s = slot_ref[i]
            for cp in weight_copies(be_ref[i], s):
                cp.wait()

            @pl.when(nxt2_ref[i] >= 0)
            def _():
                s2 = jnp.where(s == 0, WEIGHT_SLOTS - 1, s - 1)
                for cp in weight_copies(nxt2_ref[i], s2):
                    cp.start(priority=1)

            wg_bf[...] = wg_buf[s].astype(BF16)
            wu_bf[...] = wu_buf[s].astype(BF16)
            wd_bf[...] = wd_buf[s].astype(BF16)

        def compute(m):
            live = lax.broadcasted_iota(jnp.int32, (m, 1), 0) < bn_ref[i]
            x_lo, x_hi = _load_rows(x_ref, row0, m)
            x_lo = jnp.where(live, x_lo, 0.0).astype(BF16)
            x_hi = jnp.where(live, x_hi, 0.0).astype(BF16)
            g = _dot(x_lo, wg_bf[0:half, :]) + _dot(x_hi, wg_bf[half:, :])
            u = _dot(x_lo, wu_bf[0:half, :]) + _dot(x_hi, wu_bf[half:, :])
            hid = (_silu(g) * u).astype(BF16)
            _store_rows(y_ref, _dot(hid, wd_bf[...]), first=row0)

        for parts in range(1, mb // MOE_PART + 1):
            @pl.when((bn_ref[i] > (parts - 1) * MOE_PART) & (bn_ref[i] <= parts * MOE_PART))
            def _(parts=parts):
                compute(parts * MOE_PART)

    for sb in range(MOE_STEP_BLOCKS):
        block(pl.program_id(0) * MOE_STEP_BLOCKS + sb, sb * mb)


def _experts(table, xs, w_gate, w_up, w_down, layer):
    n_blocks = table[0].shape[0]
    _, _, d, de = w_gate.shape
    shape = (MOE_STEP_BLOCKS * MOE_BLOCK * ROW_WORDS, LANES)
    hbm = pl.BlockSpec(memory_space=pl.ANY)
    return pl.pallas_call(
        functools.partial(_expert_kernel, layer=layer),
        grid_spec=pltpu.PrefetchScalarGridSpec(
            num_scalar_prefetch=len(table),
            grid=(n_blocks // MOE_STEP_BLOCKS,),
            in_specs=[pl.BlockSpec(shape, lambda i, *t: (t[-1][i], 0)), hbm, hbm, hbm],
            out_specs=pl.BlockSpec(shape, lambda i, *t: (t[-1][i], 0)),
            scratch_shapes=[pltpu.VMEM((WEIGHT_SLOTS, d, de), F32), pltpu.VMEM((WEIGHT_SLOTS, d, de), F32),
                            pltpu.VMEM((WEIGHT_SLOTS, de, d), F32),
                            pltpu.VMEM((d, de), BF16), pltpu.VMEM((d, de), BF16), pltpu.VMEM((de, d), BF16),
                            pltpu.SemaphoreType.DMA((WEIGHT_SLOTS,))]),
        out_shape=jax.ShapeDtypeStruct(xs.shape, jnp.uint32),
        compiler_params=_cparams("arbitrary"),
        name="moe_experts",
    )(*table, xs, w_gate, w_up, w_down)


def _combine_kernel(dest_ref, dest_next_ref, x1_ref, route_ref, mod_ref, lng_ref, lnb_ref, y_ref, o_ref,
                    buf_ref, sem, *, mod_row, rows_per_batch):
    tm, d = x1_ref.shape
    i = pl.program_id(0)
    n = pl.num_programs(0)
    slot = i % 2
    m = _mod_row(mod_ref, mod_row, tm, rows_per_batch)

    def gather(idx_ref, s):
        def body(r4, c):
            for u in range(DMA_UNROLL):
                r = r4 * DMA_UNROLL + u
                for k in range(2):
                    _row_copy(y_ref, idx_ref[0, 0, k * tm + r], buf_ref.at[s], k * tm + r,
                              sem.at[s]).start(priority=k)
            return c

        lax.fori_loop(0, tm // DMA_UNROLL, body, 0)

    @pl.when(i == 0)
    def _():
        gather(dest_ref, 0)

    @pl.when(i + 1 < n)
    def _():
        gather(dest_next_ref, 1 - slot)

    for k in range(2):
        pltpu.make_async_copy(y_ref.at[pl.ds(0, tm * ROW_WORDS)],
                              buf_ref.at[slot, pl.ds(0, tm * ROW_WORDS)], sem.at[slot]).wait()

    gate2 = m[:, 5 * d:6 * d]
    rt = route_ref[...]
    f = jnp.zeros((tm, d), F32)
    for k in range(2):
        lo, hi = _load_rows(buf_ref.at[slot], k * tm, tm)
        f = f + jnp.concatenate([lo, hi], axis=1) * rt[:, 2 + k:3 + k]
    o_ref[...] = _layer_norm(RES_ALPHA * x1_ref[...] + gate2 * f) * lng_ref[...] + lnb_ref[...]


def _combine(dest, x1, route, mod, ln_g, ln_b, y_tiles, *, row_off, rows, mod_row, rows_per_batch):
    d = x1.shape[1]
    tm = COMBINE_TILE
    steps = rows // tm
    off = row_off // tm
    kern = functools.partial(_combine_kernel, mod_row=mod_row, rows_per_batch=rows_per_batch)
    const = lambda a: pl.BlockSpec(a.shape, lambda i: (0,) * a.ndim)
    return pl.pallas_call(
        kern,
        grid=(steps,),
        in_specs=[pl.BlockSpec((1, 1, 2 * tm), lambda i: (off + i, 0, 0), memory_space=pltpu.SMEM),
                  pl.BlockSpec((1, 1, 2 * tm), lambda i: (off + jnp.minimum(i + 1, steps - 1), 0, 0),
                               memory_space=pltpu.SMEM),
                  pl.BlockSpec((tm, d), lambda i: (off + i, 0)),
                  pl.BlockSpec((tm, LANES), lambda i: (off + i, 0)),
                  const(mod), const(ln_g), const(ln_b), pl.BlockSpec(memory_space=pl.ANY)],
        out_specs=pl.BlockSpec((tm, d), lambda i: (i, 0)),
        out_shape=jax.ShapeDtypeStruct((rows, d), F32),
        scratch_shapes=[pltpu.VMEM((2, 2 * tm * ROW_WORDS, LANES), jnp.uint32), pltpu.SemaphoreType.DMA((2,))],
        compiler_params=_cparams("arbitrary"),
        name="moe_combine",
    )(dest, dest, x1, route, mod, ln_g, ln_b, y_tiles)


def _rope_tables(n_pos):
    rows = n_pos // GRID_W
    row = jnp.repeat(jnp.arange(rows), GRID_W).astype(F32)
    col = jnp.tile(jnp.arange(GRID_W), rows).astype(F32)
    n_freq = HEAD_DIM // 4
    freq = ROPE_BASE ** (-jnp.arange(n_freq, dtype=F32) / n_freq)
    ang_r, ang_c = row[:, None] * freq, col[:, None] * freq
    cos_h = jnp.concatenate([jnp.cos(ang_r)] * 2 + [jnp.cos(ang_c)] * 2, axis=1)
    sin_h = jnp.concatenate([-jnp.sin(ang_r), jnp.sin(ang_r), -jnp.sin(ang_c), jnp.sin(ang_c)], axis=1)
    return jnp.tile(cos_h, (1, 2)), jnp.tile(sin_h, (1, 2))


def _block_table(counts, n_blocks):
    cnt = counts.astype(jnp.int32)
    padded = (cnt + MOE_BLOCK - 1) // MOE_BLOCK * MOE_BLOCK
    pad_end = jnp.cumsum(padded)
    pad_start = pad_end - padded
    blk_start = jnp.arange(n_blocks, dtype=jnp.int32)[:, None] * MOE_BLOCK
    be = jnp.minimum(jnp.sum((pad_end[None, :] <= blk_start).astype(jnp.int32), axis=1), N_EXPERTS - 1)
    ids = jnp.arange(N_EXPERTS, dtype=jnp.int32)
    mine = be[:, None] == ids[None, :]
    fill = jnp.sum(jnp.where(mine, cnt[None, :] + pad_start[None, :], 0), axis=1) - blk_start[:, 0]
    bn = jnp.clip(fill, 0, MOE_BLOCK)
    prev = jnp.concatenate([jnp.full((1,), -1, jnp.int32), be[:-1]])
    first = ((bn > 0) & (be != prev)).astype(jnp.int32)
    slot = (jnp.cumsum(first) - 1) % WEIGHT_SLOTS
    later = (ids[None, :] > ids[:, None]) & (cnt[None, :] > 0)
    nxt_e = jnp.min(jnp.where(later, ids[None, :], N_EXPERTS), axis=1)
    hop = nxt_e[:, None] == ids[None, :]
    nxt2_e = jnp.sum(jnp.where(hop, nxt_e[None, :], 0), axis=1) + jnp.where(nxt_e == N_EXPERTS, N_EXPERTS, 0)
    lookup = lambda tab: jnp.sum(jnp.where(mine, jnp.where(tab >= N_EXPERTS, -1, tab)[None, :], 0), axis=1)
    last_step = jnp.maximum(jnp.sum((bn > 0).astype(jnp.int32)) - 1, 0) // MOE_STEP_BLOCKS
    step_idx = jnp.minimum(jnp.arange(n_blocks // MOE_STEP_BLOCKS, dtype=jnp.int32), last_step)
    return (be, bn, first, slot.astype(jnp.int32), lookup(nxt_e).astype(jnp.int32),
            lookup(nxt2_e).astype(jnp.int32), step_idx.astype(jnp.int32))


def _moe(route_t, h2_tiles, w_gate, w_up, w_down, layer):
    rows = route_t.shape[1]
    tm = ROW_TILE
    n_blocks = -(-(2 * rows) // MOE_BLOCK) + N_EXPERTS
    n_blocks = -(-n_blocks // MOE_STEP_BLOCKS) * MOE_STEP_BLOCKS
    dest8, counts = _plan(route_t, tm)
    dest = dest8[:, 0:2, :].reshape(rows // tm, 1, 2 * tm)
    table = _block_table(counts[:, 0], n_blocks)
    xs = _dispatch(dest, h2_tiles, n_blocks * MOE_BLOCK, tm)
    ys = _experts(table, xs, w_gate, w_up, w_down, layer)
    tc = COMBINE_TILE
    dest_c = dest8[:, 0:2, :].reshape(rows // tm, 2, tm // tc, tc).transpose(0, 2, 1, 3).reshape(rows // tc, 1, 2 * tc)
    return dest_c, ys


def kernel(x, c, ctx, c_ctx, w_ada, b_ada, w_in, w_fourier, attn_sink, w_pool, pool_scale, w_sgu, b_sgu,
           w_out, ln1_g, ln1_b, w_router_group, w_router_expert, w_exp_gate, w_exp_up, w_exp_down,
           ln2_g, ln2_b):
    b, s, d = x.shape
    n_ctx = ctx.shape[1]
    n_layers = w_in.shape[0]
    tm = ROW_TILE
    assert n_layers == DEPTH and s == FFT_R * FFT_R and s % tm == 0 and b + 1 <= SUBLANES
    assert (b * (s + n_ctx)) % tm == 0 and n_ctx % SGU_CHUNK == 0
    cond = jnp.concatenate([c, c_ctx[None, :], jnp.zeros((SUBLANES - b - 1, d), F32)], axis=0)
    mod_all = _ada(cond, w_ada, b_ada[:, None, :])
    cos_t, sin_t = _rope_tables(s)
    x2 = x.reshape(b * s, d)
    c2 = ctx.reshape(b * n_ctx, d)
    n_sgu = w_sgu.shape[1]
    for layer in range(n_layers):
        last = layer == n_layers - 1
        mod = mod_all[layer]
        w_in_l = w_in[layer].astype(BF16)
        wf = w_fourier[layer].astype(BF16)
        w_pool_bd = jax.scipy.linalg.block_diag(*[w_pool[layer, g] for g in range(w_pool.shape[1])]).astype(BF16)
        w_sgu_stack = w_sgu[layer].reshape(n_sgu * SGU_CHUNK, SGU_CHUNK).astype(BF16)
        b_sgu_exp = jnp.repeat(b_sgu[layer].T, GROUP_W // n_sgu, axis=1)
        w_router = jnp.concatenate([w_router_expert[layer].reshape(d, N_EXPERTS), w_router_group[layer]], axis=1)
        w_router = jnp.pad(w_router, ((0, 0), (0, LANES - w_router.shape[1])))
        wr_hi = w_router.astype(BF16)
        w_route = jnp.concatenate([wr_hi, (w_router - wr_hi.astype(F32)).astype(BF16)], axis=1)
        merge_consts = (mod, w_pool_bd, pool_scale[layer][None, :], w_sgu_stack, b_sgu_exp,
                        w_out[layer].astype(BF16), ln1_g[layer][None, :], ln1_b[layer][None, :], w_route)
        sink = attn_sink[layer]

        a, q, qs, k, v, p, ug = _proj(x2, mod, w_in_l, cos_t, sin_t, mod_row=None, rows_per_batch=s,
                                      rope=True, tm=tm, a_pitch=FFT_PITCH)
        ac, qc, qsc, kc, vc, pc, ugc = _proj(c2, mod, w_in_l, cos_t, sin_t, mod_row=b, rows_per_batch=n_ctx,
                                             rope=False, tm=n_ctx, a_pitch=FFT_R)
        y_four = _fourier(a, wf, s)
        y_attn = _attention(sink, q, qs, k, v, kc, vc, seq=s, n_ctx=n_ctx, band=True)
        total = b * s + (0 if last else b * n_ctx)
        merged = _merge(x2, p, ug, y_four, y_attn, *merge_consts, (),
                        mod_row=None, seq=s, tm=tm, row_off=0, total_rows=total)
        if not last:
            yc_four = _fourier_small(ac, wf, n_ctx)
            yc_attn = _attention(sink, qc, qsc, kc, vc, kc, vc, seq=n_ctx, n_ctx=n_ctx, band=False)
            merged = _merge(c2, pc, ugc, yc_four, yc_attn, *merge_consts, tuple(merged),
                            mod_row=b, seq=n_ctx, tm=n_ctx, row_off=b * s, total_rows=total)
        x1, h2_tiles, route, route_t = merged
        dest, ys = _moe(route_t, h2_tiles, w_exp_gate, w_exp_up, w_exp_down, layer)
        ln_g, ln_b = ln2_g[layer][None, :], ln2_b[layer][None, :]
        x2 = _combine(dest, x1, route, mod, ln_g, ln_b, ys, row_off=0, rows=b * s,
                      mod_row=None, rows_per_batch=s)
        if not last:
            c2 = _combine(dest, x1, route, mod, ln_g, ln_b, ys, row_off=b * s,
                          rows=b * n_ctx, mod_row=b, rows_per_batch=n_ctx)
    return x2.reshape(b, s, d)
```

```python
import functools
import math

import numpy as np
import jax
import jax.numpy as jnp
from jax import lax
from jax.experimental import pallas as pl
from jax.experimental.pallas import tpu as pltpu

GRID_W = 64
HEAD_DIM = 64
GROUP_W = 256
KV_W = 128
WINDOW = 128
POOL_WINDOWS = (2, 4, 8, 16)
SGU_CHUNK = 128
N_GROUPS = 4
EXPERTS_PER_GROUP = 8
N_EXPERTS = 32
ROPE_BASE = 10000.0
LN_EPS = 1e-6
NEG_INF = -1e30
DEPTH = 2
RES_ALPHA = (2 * DEPTH) ** 0.25

LANES = 128
SUBLANES = 8
VMEM_LIMIT = 48 * 1024 * 1024

ROW_TILE = 1024
Q_BLOCK = 2048
COMBINE_TILE = 512
WEIGHT_SLOTS = 3
MOE_BLOCK = 512
MOE_PART = 256
MOE_STEP_BLOCKS = 4
FFT_R = 64
FFT_PITCH = 72
FFT_UNROLL = 32
DMA_UNROLL = 8
PROJ_PART = 512
MERGE_PART = 512

BF16 = jnp.bfloat16
F32 = jnp.float32


def _cparams(*sem):
    return pltpu.CompilerParams(dimension_semantics=sem, vmem_limit_bytes=VMEM_LIMIT)


def _dot(a, b):
    return jnp.dot(a, b, preferred_element_type=F32)


def _dot_nt(a, b):
    return lax.dot_general(a, b, (((1,), (1,)), ((), ())), preferred_element_type=F32)


def _layer_norm(t):
    mu = jnp.mean(t, axis=-1, keepdims=True)
    d = t - mu
    var = jnp.mean(d * d, axis=-1, keepdims=True)
    return d * lax.rsqrt(var + LN_EPS)


def _silu(t):
    return t * (1.0 / (1.0 + jnp.exp(-t)))


def _gelu(t):
    return 0.5 * t * (1.0 + lax.erf(t * (1.0 / math.sqrt(2.0))))


ROW_WORDS = 4
HI_MASK = 0xFFFF0000


def _pack_rows(t):
    half = t.shape[1] // 2
    lo = lax.bitcast_convert_type(t[:, :half].astype(BF16).astype(F32), jnp.uint32)
    hi = lax.bitcast_convert_type(t[:, half:].astype(BF16).astype(F32), jnp.uint32)
    return (lo >> 16) | hi


def _unpack_rows(w):
    return (lax.bitcast_convert_type(w << 16, F32),
            lax.bitcast_convert_type(w & jnp.uint32(HI_MASK), F32))


def _store_rows(ref, t, first=0):
    w = _pack_rows(t)
    for j in range(ROW_WORDS):
        ref[pl.ds(first * ROW_WORDS + j, t.shape[0], stride=ROW_WORDS), :] = w[:, j * LANES:(j + 1) * LANES]


def _load_rows(ref, first, m):
    w = jnp.concatenate([ref[pl.ds(first * ROW_WORDS + j, m, stride=ROW_WORDS), :] for j in range(ROW_WORDS)],
                        axis=1)
    return _unpack_rows(w)


def _ada_kernel(c_ref, w_ref, b_ref, o_ref):
    s = _silu(c_ref[...]).astype(BF16)
    o_ref[...] = _dot(s, w_ref[...].astype(BF16)) + b_ref[...]


def _ada(cond, w_ada, b_ada):
    n_layers, d, n = w_ada.shape
    tn = n // 4
    return pl.pallas_call(
        _ada_kernel,
        grid=(n_layers, n // tn),
        in_specs=[
            pl.BlockSpec((SUBLANES, d), lambda l, j: (0, 0)),
            pl.BlockSpec((None, d, tn), lambda l, j: (l, 0, j)),
            pl.BlockSpec((None, 1, tn), lambda l, j: (l, 0, j)),
        ],
        out_specs=pl.BlockSpec((None, SUBLANES, tn), lambda l, j: (l, 0, j)),
        out_shape=jax.ShapeDtypeStruct((n_layers, SUBLANES, n), F32),
        compiler_params=_cparams("arbitrary", "arbitrary"),
        name="ada",
    )(cond, w_ada, b_ada)


def _rope(t, cos_t, sin_t):
    lane = lax.broadcasted_iota(jnp.int32, t.shape, 1)
    first = (lane % 32) < 16
    partner = jnp.where(first, pltpu.roll(t, LANES - 16, axis=1), pltpu.roll(t, 16, axis=1))
    return t * cos_t + partner * sin_t


def _proj_body(x_ref, m, w_ref, cos_ref, sin_ref, outs, *, rope, a_pitch, row0, pm):
    a_ref, q_ref, qs_ref, k_ref, v_ref, p_ref, ug_ref = outs
    rows = slice(row0, row0 + pm)
    x = x_ref[rows, :]
    d = x.shape[1]
    shift, scale = m[:, 0:d], m[:, d:2 * d]
    h = _layer_norm(x) * (1.0 + scale) + shift
    z = _dot(h.astype(BF16), w_ref[...])
    pad = jnp.zeros((a_pitch - FFT_R, LANES), F32)
    for g in range(pm // FFT_R):
        gg = row0 // FFT_R + g
        for hf in range(2):
            grp = z[g * FFT_R:(g + 1) * FFT_R, hf * LANES:(hf + 1) * LANES]
            if a_pitch > FFT_R:
                grp = jnp.concatenate([grp, pad], axis=0)
            a_ref[hf, gg * a_pitch:(gg + 1) * a_pitch, :] = grp
    q0, q1 = z[:, 256:384], z[:, 384:512]
    k = z[:, 512:640]
    if rope:
        cos_t, sin_t = cos_ref[rows, :], sin_ref[rows, :]
        q0, q1, k = _rope(q0, cos_t, sin_t), _rope(q1, cos_t, sin_t), _rope(k, cos_t, sin_t)
    q_ref[rows, 0:128] = q0.astype(BF16)
    q_ref[rows, 128:256] = q1.astype(BF16)
    qs_ref[rows, 0:128] = pltpu.roll(q0, HEAD_DIM, axis=1).astype(BF16)
    qs_ref[rows, 128:256] = pltpu.roll(q1, HEAD_DIM, axis=1).astype(BF16)
    k_ref[rows, :] = k.astype(BF16)
    v_ref[rows, :] = z[:, 640:768].astype(BF16)
    p_ref[rows, :] = z[:, 768:1024].astype(BF16)
    ug_ref[rows, :] = z[:, 1024:1536].astype(BF16)


def _mod_row(mod_ref, mod_row, tm, rows_per_batch):
    row = (pl.program_id(0) * tm) // rows_per_batch if mod_row is None else mod_row
    return mod_ref[pl.ds(row, 1), :]


def _proj_kernel(x_ref, mod_ref, w_ref, cos_ref, sin_ref, *outs, mod_row, rows_per_batch, rope, a_pitch):
    tm = x_ref.shape[0]
    m = _mod_row(mod_ref, mod_row, tm, rows_per_batch)
    pm = min(tm, PROJ_PART)
    for part in range(tm // pm):
        _proj_body(x_ref, m, w_ref, cos_ref, sin_ref, outs, rope=rope, a_pitch=a_pitch, row0=part * pm, pm=pm)


def _proj_specs(rows, tm, a_pitch, seq_steps):
    row_spec = lambda w: pl.BlockSpec((tm, w), lambda i: (i, 0))
    out_w = (256, 256, 128, 128, 256, 512)
    ta = tm // FFT_R * a_pitch
    out_specs = [pl.BlockSpec((2, ta, LANES), lambda i: (0, i, 0))] + [row_spec(w) for w in out_w]
    out_shape = ([jax.ShapeDtypeStruct((2, rows // FFT_R * a_pitch, LANES), F32)]
                 + [jax.ShapeDtypeStruct((rows, w), BF16) for w in out_w])
    table_spec = pl.BlockSpec((tm, LANES), lambda i: (i % seq_steps, 0))
    return table_spec, out_specs, out_shape


def _proj(x2, mod, w_in, cos_t, sin_t, *, mod_row, rows_per_batch, rope, tm, a_pitch):
    rows, d = x2.shape
    kern = functools.partial(_proj_kernel, mod_row=mod_row, rows_per_batch=rows_per_batch,
                             rope=rope, a_pitch=a_pitch)
    table_spec, out_specs, out_shape = _proj_specs(rows, tm, a_pitch, cos_t.shape[0] // tm)
    return pl.pallas_call(
        kern,
        grid=(rows // tm,),
        in_specs=[
            pl.BlockSpec((tm, d), lambda i: (i, 0)),
            pl.BlockSpec(mod.shape, lambda i: (0, 0)),
            pl.BlockSpec(w_in.shape, lambda i: (0, 0)),
            table_spec, table_spec,
        ],
        out_specs=out_specs,
        out_shape=out_shape,
        compiler_params=_cparams("arbitrary"),
        name="proj",
    )(x2, mod, w_in, cos_t, sin_t)


def _fft_tables(n_pos):
    r = FFT_R
    assert n_pos == r * r
    kb = np.arange(r)[None, :, None]
    na = np.arange(r)[:, None, None]
    nb = np.arange(r)[None, None, :]
    ang = 2.0 * np.pi * ((kb * (na + r * nb)) % n_pos) / n_pos
    m1 = np.concatenate([np.cos(ang), -np.sin(ang)], axis=1)
    ka = np.arange(r)[:, None]
    n2 = np.arange(r)[None, :]
    ang2 = 2.0 * np.pi * ((ka * n2) % r) / r
    c2, s2 = np.cos(ang2), np.sin(ang2)
    w2 = np.block([[c2, s2], [-s2, c2]])
    return m1, w2


def _channel_tables(n_pos):
    h = HEAD_DIM
    c = np.arange(h)
    ang = 2.0 * np.pi * ((c[:, None] * c[None, :]) % h) / h
    scale = 1.0 / math.sqrt(n_pos * h)
    eye = np.eye(GROUP_W // h)
    cc = np.kron(eye, np.cos(ang)) * scale
    ss = np.kron(eye, np.sin(ang)) * scale
    return np.concatenate([cc, ss], axis=0)


def _fourier_kernel(a_ref, m1_ref, w2_ref, ch_ref, wf_ref, o_ref, z_ref, y_ref):
    r, pt = FFT_R, FFT_PITCH

    def step1(i, c):
        for u in range(FFT_UNROLL):
            na = i * FFT_UNROLL + u
            rows = jnp.concatenate([a_ref[0, pl.ds(na, r, stride=pt), :],
                                    a_ref[1, pl.ds(na, r, stride=pt), :]], axis=1)
            z = _dot(m1_ref[na], rows.astype(BF16))
            base = pl.multiple_of(na * pt, SUBLANES)
            z_ref[0, pl.ds(base, r), :] = z[0:r, 0:LANES]
            z_ref[1, pl.ds(base, r), :] = z[0:r, LANES:]
            z_ref[2, pl.ds(base, r), :] = z[r:, 0:LANES]
            z_ref[3, pl.ds(base, r), :] = z[r:, LANES:]
        return c

    lax.fori_loop(0, r // FFT_UNROLL, step1, 0)

    def step2(i, c):
        for u in range(FFT_UNROLL):
            kb = i * FFT_UNROLL + u
            q = [z_ref[j, pl.ds(kb, r, stride=pt), :] for j in range(4)]
            zs = jnp.concatenate([jnp.concatenate(q[0:2], axis=1),
                                  jnp.concatenate(q[2:4], axis=1)], axis=0)
            y = _dot(w2_ref[...], zs.astype(BF16))
            base = pl.multiple_of(kb * r, r)
            y_ref[0, pl.ds(base, r), :] = y[0:r, 0:LANES]
            y_ref[1, pl.ds(base, r), :] = y[0:r, LANES:]
            y_ref[2, pl.ds(base, r), :] = y[r:, 0:LANES]
            y_ref[3, pl.ds(base, r), :] = y[r:, LANES:]
        return c

    lax.fori_loop(0, r // FFT_UNROLL, step2, 0)

    chunk = 8 * r
    for cidx in range(r * r // chunk):
        yy = jnp.concatenate([y_ref[j, cidx * chunk:(cidx + 1) * chunk, :] for j in range(4)], axis=1)
        f = _dot(yy.astype(BF16), ch_ref[...])
        g = _dot(f.astype(BF16), wf_ref[...])
        for gi in range(chunk // r):
            kb = cidx * (chunk // r) + gi
            z_ref[0, kb * pt:kb * pt + r, :] = g[gi * r:(gi + 1) * r, 0:LANES]
            z_ref[1, kb * pt:kb * pt + r, :] = g[gi * r:(gi + 1) * r, LANES:]

    def step3(i, c):
        for u in range(FFT_UNROLL):
            ka = i * FFT_UNROLL + u
            base = pl.multiple_of(ka * r, r)
            o_ref[pl.ds(base, r), 0:LANES] = z_ref[0, pl.ds(ka, r, stride=pt), :]
            o_ref[pl.ds(base, r), LANES:] = z_ref[1, pl.ds(ka, r, stride=pt), :]
        return c

    lax.fori_loop(0, r // FFT_UNROLL, step3, 0)


def _fourier(a3, w_fourier, n_pos):
    rows = a3.shape[1] // FFT_PITCH * FFT_R
    gw = GROUP_W
    m1, w2 = _fft_tables(n_pos)
    ch = _channel_tables(n_pos)
    const = lambda shape: pl.BlockSpec(shape, lambda b: (0,) * len(shape))
    return pl.pallas_call(
        _fourier_kernel,
        grid=(rows // n_pos,),
        in_specs=[
            pl.BlockSpec((2, FFT_R * FFT_PITCH, LANES), lambda b: (0, b, 0)),
            const(m1.shape), const(w2.shape), const(ch.shape), const(w_fourier.shape),
        ],
        out_specs=pl.BlockSpec((n_pos, gw), lambda b: (b, 0)),
        out_shape=jax.ShapeDtypeStruct((rows, gw), F32),
        scratch_shapes=[pltpu.VMEM((4, FFT_R * FFT_PITCH, LANES), F32), pltpu.VMEM((4, n_pos, LANES), F32)],
        compiler_params=_cparams("arbitrary"),
        name="fourier",
    )(a3, jnp.asarray(m1, BF16), jnp.asarray(w2, BF16), jnp.asarray(ch, BF16), w_fourier)


def _fourier_small_kernel(a_ref, cs_ref, ch_ref, wf_ref, o_ref):
    n = a_ref.shape[1]
    a = jnp.concatenate([a_ref[0], a_ref[1]], axis=1)
    pq = _dot(cs_ref[...], a.astype(BF16))
    y = jnp.concatenate([pq[0:n], pq[n:2 * n]], axis=1).astype(BF16)
    f = _dot(y, ch_ref[...])
    o_ref[...] = _dot(f.astype(BF16), wf_ref[...])


def _fourier_small(a3, w_fourier, n_pos):
    _, rows, _ = a3.shape
    gw = GROUP_W
    k = np.arange(n_pos)
    ang = 2.0 * np.pi * ((k[:, None] * k[None, :]) % n_pos) / n_pos
    cs = np.concatenate([np.cos(ang), -np.sin(ang)], axis=0)
    ch = _channel_tables(n_pos)
    const = lambda shape: pl.BlockSpec(shape, lambda b: (0,) * len(shape))
    return pl.pallas_call(
        _fourier_small_kernel,
        grid=(rows // n_pos,),
        in_specs=[pl.BlockSpec((2, n_pos, LANES), lambda b: (0, b, 0)),
                  const(cs.shape), const(ch.shape), const(w_fourier.shape)],
        out_specs=pl.BlockSpec((n_pos, gw), lambda b: (b, 0)),
        out_shape=jax.ShapeDtypeStruct((rows, gw), F32),
        compiler_params=_cparams("arbitrary"),
        name="fourier_ctx",
    )(a3, jnp.asarray(cs, BF16), jnp.asarray(ch, BF16), w_fourier)


ATTN_SUB = 128


def _attn_kernel(sink_ref, q_ref, qs_ref, k_ref, v_ref, kc_ref, vc_ref, o_ref, *, band, seq):
    qb = q_ref.shape[0]
    sub = ATTN_SUB
    lane = lax.broadcasted_iota(jnp.int32, (1, LANES), 1)
    lo_half = lane < HEAD_DIM
    zero = jnp.zeros((), BF16)
    scale = jnp.asarray(HEAD_DIM ** -0.5, BF16)
    kw = sub + 2 * WINDOW
    for sb in range(qb // sub):
        rows = slice(sb * sub, (sb + 1) * sub)
        qa0, qa1 = q_ref[rows, 0:LANES], q_ref[rows, LANES:]
        qs0, qs1 = qs_ref[rows, 0:LANES], qs_ref[rows, LANES:]
        q_all = jnp.concatenate([jnp.where(lo_half, qa0, zero), jnp.where(lo_half, qs0, zero),
                                 jnp.where(lo_half, zero, qs1), jnp.where(lo_half, zero, qa1)], axis=0) * scale
        if band:
            p0 = pl.program_id(1) * qb + sb * sub
            start = pl.multiple_of(jnp.clip(p0 - WINDOW, 0, seq - kw), WINDOW)
            qpos = p0 + lax.broadcasted_iota(jnp.int32, (sub, 1), 0)
            kpos = start + lax.broadcasted_iota(jnp.int32, (1, kw), 1)
            bias = jnp.where(jnp.abs(qpos - kpos) <= WINDOW, 0.0, NEG_INF)
            keys = jnp.concatenate([k_ref[pl.ds(start, kw), :], kc_ref[...]], axis=0)
            vals = jnp.concatenate([v_ref[pl.ds(start, kw), :], vc_ref[...]], axis=0)
        else:
            keys, vals = kc_ref[...], vc_ref[...]
        s_all = _dot_nt(q_all, keys)
        probs, dens = [], []
        for h in range(4):
            s = s_all[h * sub:(h + 1) * sub, :]
            sink = sink_ref[h]
            if band:
                s = jnp.concatenate([s[:, 0:kw] + bias, s[:, kw:]], axis=1)
            m = jnp.maximum(jnp.max(s, axis=1, keepdims=True), sink)
            p = jnp.exp(s - m)
            dens.append(jnp.sum(p, axis=1, keepdims=True) + jnp.exp(sink - m))
            probs.append(p.astype(BF16))
        o_all = _dot(jnp.concatenate(probs, axis=0), vals)
        o = [o_all[h * sub:(h + 1) * sub, :] / dens[h] for h in range(4)]
        o_ref[rows, 0:LANES] = jnp.where(lo_half, o[0], pltpu.roll(o[1], HEAD_DIM, axis=1)).astype(BF16)
        o_ref[rows, LANES:] = jnp.where(lo_half, pltpu.roll(o[2], HEAD_DIM, axis=1), o[3]).astype(BF16)


def _attention(sink, q, qs, k, v, kc, vc, *, seq, n_ctx, band):
    rows = q.shape[0]
    n_batch = rows // seq
    qb = Q_BLOCK if band else seq
    steps = seq // qb
    kern = functools.partial(_attn_kernel, band=band, seq=seq)
    seq_spec = pl.BlockSpec((seq, KV_W), lambda b, i: (b, 0))
    ctx_spec = pl.BlockSpec((n_ctx, KV_W), lambda b, i: (b, 0))
    q_spec = pl.BlockSpec((qb, GROUP_W), lambda b, i: (b * steps + i, 0))
    return pl.pallas_call(
        kern,
        grid=(n_batch, steps),
        in_specs=[pl.BlockSpec(memory_space=pltpu.SMEM), q_spec, q_spec,
                  seq_spec, seq_spec, ctx_spec, ctx_spec],
        out_specs=q_spec,
        out_shape=jax.ShapeDtypeStruct((rows, GROUP_W), BF16),
        compiler_params=_cparams("arbitrary", "arbitrary"),
        name="attn" if band else "attn_ctx",
    )(sink, q, qs, k, v, kc, vc)


POOL_HALO = max(POOL_WINDOWS) // 2


def _pool(p_ref, t0, tm, seq):
    halo = POOL_HALO
    pack = 2 * SUBLANES
    t0 = pl.multiple_of(t0, pack)
    main = p_ref[pl.ds(t0, tm), :].astype(F32)
    lo = pl.multiple_of(jnp.maximum(t0 - pack, 0), pack)
    hi = pl.multiple_of(jnp.minimum(t0 + tm, seq - pack), pack)
    prev = p_ref[pl.ds(lo, pack), :].astype(F32)[pack - halo:, :]
    nxt = p_ref[pl.ds(hi, pack), :].astype(F32)[:halo, :]
    prev = jnp.where(t0 > 0, prev, 0.0)
    nxt = jnp.where(t0 + tm < seq, nxt, 0.0)
    full = jnp.concatenate([prev, main, nxt], axis=0)
    n = tm + 2 * halo
    gch = GROUP_W // len(POOL_WINDOWS)
    first = lax.broadcasted_iota(jnp.int32, (1, LANES), 1) < gch
    means = []
    for hf in range(GROUP_W // LANES):
        wa, wb = POOL_WINDOWS[2 * hf], POOL_WINDOWS[2 * hf + 1]
        x = full[:, hf * LANES:(hf + 1) * LANES]
        sums, w, s = {}, 2, pltpu.roll(x, 1, axis=0) + x
        sums[w] = s
        while w < wb:
            s = pltpu.roll(s, w // 2, axis=0) + pltpu.roll(s, n - w // 2, axis=0)
            w *= 2
            sums[w] = s
        means.append(jnp.where(first, sums[wa] * (1.0 / wa), sums[wb] * (1.0 / wb))[halo:halo + tm, :])
    mean = jnp.concatenate(means, axis=1)
    win = jnp.concatenate([jnp.full((1, gch), w, jnp.int32) for w in POOL_WINDOWS], axis=1)

    def rescale(rows, first_pos):
        pos = first_pos + lax.broadcasted_iota(jnp.int32, (halo, 1), 0)
        cnt = jnp.minimum(pos + win // 2, seq) - jnp.maximum(pos - win // 2, 0)
        return rows * (win.astype(F32) / cnt.astype(F32))

    mean = jnp.concatenate([rescale(mean[:halo], t0), mean[halo:tm - halo],
                            rescale(mean[tm - halo:], t0 + tm - halo)], axis=0)
    return mean - main


def _route(logits):
    tm = logits.shape[0]
    lt = logits.T
    gl = lt[N_EXPERTS:N_EXPERTS + N_GROUPS]
    sub_g = lax.broadcasted_iota(jnp.int32, gl.shape, 0)
    gmax = jnp.max(gl, axis=0, keepdims=True)
    grp = jnp.min(jnp.where(gl == gmax, sub_g, N_GROUPS), axis=0, keepdims=True)
    gate_group = 1.0 / jnp.sum(jnp.exp(gl - gmax), axis=0, keepdims=True)
    el = lt[0:EXPERTS_PER_GROUP]
    for g in range(1, N_GROUPS):
        el = jnp.where(grp == g, lt[g * EXPERTS_PER_GROUP:(g + 1) * EXPERTS_PER_GROUP], el)
    sub = lax.broadcasted_iota(jnp.int32, el.shape, 0)
    m1 = jnp.max(el, axis=0, keepdims=True)
    i1 = jnp.min(jnp.where(el == m1, sub, EXPERTS_PER_GROUP), axis=0, keepdims=True)
    el2 = jnp.where(sub == i1, -jnp.inf, el)
    m2 = jnp.max(el2, axis=0, keepdims=True)
    i2 = jnp.min(jnp.where(el2 == m2, sub, EXPERTS_PER_GROUP), axis=0, keepdims=True)
    r = jnp.exp(m2 - m1)
    g1 = gate_group / (1.0 + r)
    g2 = g1 * r
    e1 = (grp * EXPERTS_PER_GROUP + i1).astype(F32)
    e2 = (grp * EXPERTS_PER_GROUP + i2).astype(F32)
    rows = jnp.where(sub == 0, e1, jnp.where(sub == 1, e2, jnp.where(sub == 2, g1, jnp.where(sub == 3, g2, 0.0))))
    cols = jnp.concatenate([rows, jnp.zeros((LANES - rows.shape[0], tm), F32)], axis=0).T
    return cols, rows


def _merge_kernel(x_ref, p_ref, ug_ref, yf_ref, ya_ref, mod_ref, wpool_ref, pscale_ref, wsgu_ref, bsgu_ref,
                  wout_ref, lng_ref, lnb_ref, wr_ref, *rest,
                  mod_row, seq, n_alias):
    x1_ref, h2_ref, route_ref, route_t_ref = rest[n_alias:]
    tm, d = x_ref.shape
    if mod_row is None:
        row = pl.program_id(0)
    else:
        row = mod_row
    t0 = pl.multiple_of(pl.program_id(1) * tm, tm)
    m = mod_ref[pl.ds(row, 1), :]
    gate1, shift2, scale2 = m[:, 2 * d:3 * d], m[:, 3 * d:4 * d], m[:, 4 * d:5 * d]
    lane = lax.broadcasted_iota(jnp.int32, (1, GROUP_W), 1)
    n_heads = wsgu_ref.shape[0] // SGU_CHUNK
    head = lane // (GROUP_W // n_heads)

    pm = min(tm, MERGE_PART)
    for part in range(tm // pm):
        r0 = part * pm
        rows = slice(r0, r0 + pm)
        pooled = _pool(p_ref, t0 + r0, pm, seq)
        y_pool = _dot(pooled.astype(BF16), wpool_ref[...]) * pscale_ref[...]

        ug = ug_ref[rows, :].astype(F32)
        u = _gelu(ug[:, 0:GROUP_W])
        v = _layer_norm(_gelu(ug[:, GROUP_W:])).astype(BF16)
        mixed = []
        for cidx in range(pm // SGU_CHUNK):
            vc = v[cidx * SGU_CHUNK:(cidx + 1) * SGU_CHUNK, :]
            full = _dot(wsgu_ref[...], vc)
            mc = bsgu_ref[...]
            for hd in range(n_heads):
                mc = mc + jnp.where(head == hd, full[hd * SGU_CHUNK:(hd + 1) * SGU_CHUNK, :], 0.0)
            mixed.append(mc)
        y_sgu = u * jnp.concatenate(mixed, axis=0)

        cat = jnp.concatenate([yf_ref[rows, :].astype(BF16), ya_ref[rows, :], y_pool.astype(BF16),
                               y_sgu.astype(BF16)], axis=1)
        y = _dot(cat, wout_ref[...])
        x1 = _layer_norm(RES_ALPHA * x_ref[rows, :] + gate1 * y) * lng_ref[...] + lnb_ref[...]
        x1_ref[rows, :] = x1
        h2 = _layer_norm(x1) * (1.0 + scale2) + shift2
        _store_rows(h2_ref, h2, first=r0)
        lg = _dot(h2.astype(BF16), wr_ref[...])
        route_ref[rows, :], route_t_ref[:, rows] = _route(lg[:, 0:LANES] + lg[:, LANES:])


def _merge(x2, p, ug, y_four, y_attn, mod, w_pool_bd, pool_scale, w_sgu_stack, b_sgu_exp, w_out,
           ln_g, ln_b, w_route, aliased, *, mod_row, seq, tm, row_off, total_rows):
    rows, d = x2.shape
    n_batch, steps = rows // seq, seq // tm
    off = row_off // tm
    kern = functools.partial(_merge_kernel, mod_row=mod_row, seq=seq, n_alias=len(aliased))
    row_spec = lambda w: pl.BlockSpec((tm, w), lambda b, i: (b * steps + i, 0))
    const = lambda a: pl.BlockSpec(a.shape, lambda b, i: (0,) * a.ndim)
    consts = (mod, w_pool_bd, pool_scale, w_sgu_stack, b_sgu_exp, w_out, ln_g, ln_b, w_route)
    n_in = 5 + len(consts)
    out_shapes = [jax.ShapeDtypeStruct((total_rows, d), F32),
                  jax.ShapeDtypeStruct((total_rows * ROW_WORDS, LANES), jnp.uint32),
                  jax.ShapeDtypeStruct((total_rows, LANES), F32),
                  jax.ShapeDtypeStruct((SUBLANES, total_rows), F32)]
    out_specs = [pl.BlockSpec((tm, d), lambda b, i: (off + b * steps + i, 0)),
                 pl.BlockSpec((tm * ROW_WORDS, LANES), lambda b, i: (off + b * steps + i, 0)),
                 pl.BlockSpec((tm, LANES), lambda b, i: (off + b * steps + i, 0)),
                 pl.BlockSpec((SUBLANES, tm), lambda b, i: (0, off + b * steps + i))]
    return pl.pallas_call(
        kern,
        grid=(n_batch, steps),
        in_specs=[row_spec(d), pl.BlockSpec((seq, GROUP_W), lambda b, i: (b, 0)),
                  row_spec(2 * GROUP_W), row_spec(GROUP_W), row_spec(GROUP_W)]
                 + [const(a) for a in consts]
                 + [pl.BlockSpec(memory_space=pl.ANY)] * len(aliased),
        out_specs=out_specs,
        out_shape=out_shapes,
        input_output_aliases={n_in + k: k for k in range(len(aliased))},
        compiler_params=_cparams("arbitrary", "arbitrary"),
        name="merge",
    )(x2, p, ug, y_four, y_attn, *consts, *aliased)


def _plan_kernel(route_ref, dest_ref, cnt_out_ref, cnt_ref, start_ref, carry_ref):
    ph, t = pl.program_id(0), pl.program_id(1)
    tm = route_ref.shape[1]
    rt = route_ref[...]
    e1 = rt[0:1, :].astype(jnp.int32)
    e2 = rt[1:2, :].astype(jnp.int32)
    sub = lax.broadcasted_iota(jnp.int32, (N_EXPERTS, tm), 0)
    hit1, hit2 = sub == e1, sub == e2
    onehot = jnp.where(hit1 | hit2, 1.0, 0.0)
    tile_cnt = jnp.sum(onehot, axis=1, keepdims=True)

    @pl.when((ph == 0) & (t == 0))
    def _():
        cnt_ref[...] = jnp.zeros_like(cnt_ref)

    @pl.when(ph == 0)
    def _():
        cnt_ref[...] += tile_cnt

    @pl.when((ph == 1) & (t == 0))
    def _():
        cnt = cnt_ref[...]
        padded = jnp.floor((cnt + (MOE_BLOCK - 1.0)) * (1.0 / MOE_BLOCK)) * MOE_BLOCK
        row = lax.broadcasted_iota(jnp.int32, cnt.shape, 0)
        incl = padded
        sh = 1
        while sh < N_EXPERTS:
            incl = incl + jnp.where(row >= sh, pltpu.roll(incl, sh, axis=0), 0.0)
            sh *= 2
        start_ref[...] = incl - padded
        carry_ref[...] = jnp.zeros_like(carry_ref)
        cnt_out_ref[...] = cnt

    @pl.when(ph == 1)
    def _():
        r_i = lax.broadcasted_iota(jnp.int32, (tm, tm), 0)
        c_i = lax.broadcasted_iota(jnp.int32, (tm, tm), 1)
        before = jnp.where(r_i < c_i, 1.0, 0.0).astype(BF16)
        rank = _dot(onehot.astype(BF16), before)
        base = start_ref[:, 0:1] + carry_ref[:, 0:1] + rank
        d1 = jnp.sum(jnp.where(hit1, base, 0.0), axis=0, keepdims=True)
        d2 = jnp.sum(jnp.where(hit2, base, 0.0), axis=0, keepdims=True)
        sub8 = lax.broadcasted_iota(jnp.int32, (SUBLANES, tm), 0)
        dest_ref[...] = jnp.where(sub8 == 0, d1, d2).astype(jnp.int32)
        carry_ref[...] += tile_cnt


def _plan(route_t, tm):
    rows = route_t.shape[1]
    n_t = rows // tm
    return pl.pallas_call(
        _plan_kernel,
        grid=(2, n_t),
        in_specs=[pl.BlockSpec((SUBLANES, tm), lambda ph, t: (0, t))],
        out_specs=[pl.BlockSpec((None, SUBLANES, tm), lambda ph, t: (t * ph, 0, 0)),
                   pl.BlockSpec((N_EXPERTS, LANES), lambda ph, t: (0, 0))],
        out_shape=[jax.ShapeDtypeStruct((n_t, SUBLANES, tm), jnp.int32),
                   jax.ShapeDtypeStruct((N_EXPERTS, LANES), F32)],
        scratch_shapes=[pltpu.VMEM((N_EXPERTS, LANES), F32)] * 3,
        compiler_params=_cparams("arbitrary", "arbitrary"),
        name="moe_plan",
    )(route_t)


def _row_copy(src_ref, src_row, dst_ref, dst_row, sem):
    return pltpu.make_async_copy(
        src_ref.at[pl.ds(pl.multiple_of(src_row * ROW_WORDS, ROW_WORDS), ROW_WORDS)],
        dst_ref.at[pl.ds(pl.multiple_of(dst_row * ROW_WORDS, ROW_WORDS), ROW_WORDS)], sem)


def _dispatch_kernel(dest_ref, h2_ref, xs_ref, sem, *, tm):
    def body(r4, c):
        for u in range(DMA_UNROLL):
            r = r4 * DMA_UNROLL + u
            for k in range(2):
                _row_copy(h2_ref, r, xs_ref, dest_ref[0, 0, k * tm + r], sem).start(priority=k)
        return c

    lax.fori_loop(0, tm // DMA_UNROLL, body, 0)
    for k in range(2):
        pltpu.make_async_copy(h2_ref, xs_ref.at[pl.ds(0, tm * ROW_WORDS)], sem).wait()


def _dispatch(dest, h2_tiles, n_slots, tm):
    n_t = dest.shape[0]
    return pl.pallas_call(
        functools.partial(_dispatch_kernel, tm=tm),
        grid=(n_t,),
        in_specs=[pl.BlockSpec((1, 1, 2 * tm), lambda i: (i, 0, 0), memory_space=pltpu.SMEM),
                  pl.BlockSpec((tm * ROW_WORDS, LANES), lambda i: (i, 0))],
        out_specs=pl.BlockSpec(memory_space=pl.ANY),
        out_shape=jax.ShapeDtypeStruct((n_slots * ROW_WORDS, LANES), jnp.uint32),
        scratch_shapes=[pltpu.SemaphoreType.DMA],
        compiler_params=_cparams("arbitrary"),
        name="moe_dispatch",
    )(dest, h2_tiles)


def _expert_kernel(be_ref, bn_ref, first_ref, slot_ref, nxt_ref, nxt2_ref, _blk_ref,
                   x_ref, wg_hbm, wu_hbm, wd_hbm, y_ref,
                   wg_buf, wu_buf, wd_buf, wg_bf, wu_bf, wd_bf, sem, *, layer):
    mb = MOE_BLOCK
    half = wg_bf.shape[0] // 2

    def weight_copies(e, s):
        return [pltpu.make_async_copy(wg_hbm.at[layer, e], wg_buf.at[s], sem.at[s]),
                pltpu.make_async_copy(wu_hbm.at[layer, e], wu_buf.at[s], sem.at[s]),
                pltpu.make_async_copy(wd_hbm.at[layer, e], wd_buf.at[s], sem.at[s])]

    @pl.when(pl.program_id(0) == 0)
    def _():
        for cp in weight_copies(be_ref[0], 0):
            cp.start()

        @pl.when(nxt_ref[0] >= 0)
        def _():
            for cp in weight_copies(nxt_ref[0], 1):
                cp.start(priority=1)

    def block(i, row0):
        @pl.when(first_ref[i] == 1)
        def _():
            s = slot_ref[i]
            for cp in weight_copies(be_ref[i], s):
                cp.wait()

            @pl.when(nxt2_ref[i] >= 0)
            def _():
                s2 = jnp.where(s == 0, WEIGHT_SLOTS - 1, s - 1)
                for cp in weight_copies(nxt2_ref[i], s2):
                    cp.start(priority=1)

            wg_bf[...] = wg_buf[s].astype(BF16)
            wu_bf[...] = wu_buf[s].astype(BF16)
            wd_bf[...] = wd_buf[s].astype(BF16)

        def compute(m):
            live = lax.broadcasted_iota(jnp.int32, (m, 1), 0) < bn_ref[i]
            x_lo, x_hi = _load_rows(x_ref, row0, m)
            x_lo = jnp.where(live, x_lo, 0.0).astype(BF16)
            x_hi = jnp.where(live, x_hi, 0.0).astype(BF16)
            g = _dot(x_lo, wg_bf[0:half, :]) + _dot(x_hi, wg_bf[half:, :])
            u = _dot(x_lo, wu_bf[0:half, :]) + _dot(x_hi, wu_bf[half:, :])
            hid = (_silu(g) * u).astype(BF16)
            _store_rows(y_ref, _dot(hid, wd_bf[...]), first=row0)

        for parts in range(1, mb // MOE_PART + 1):
            @pl.when((bn_ref[i] > (parts - 1) * MOE_PART) & (bn_ref[i] <= parts * MOE_PART))
            def _(parts=parts):
                compute(parts * MOE_PART)

    for sb in range(MOE_STEP_BLOCKS):
        block(pl.program_id(0) * MOE_STEP_BLOCKS + sb, sb * mb)


def _experts(table, xs, w_gate, w_up, w_down, layer):
    n_blocks = table[0].shape[0]
    _, _, d, de = w_gate.shape
    shape = (MOE_STEP_BLOCKS * MOE_BLOCK * ROW_WORDS, LANES)
    hbm = pl.BlockSpec(memory_space=pl.ANY)
    return pl.pallas_call(
        functools.partial(_expert_kernel, layer=layer),
        grid_spec=pltpu.PrefetchScalarGridSpec(
            num_scalar_prefetch=len(table),
            grid=(n_blocks // MOE_STEP_BLOCKS,),
            in_specs=[pl.BlockSpec(shape, lambda i, *t: (t[-1][i], 0)), hbm, hbm, hbm],
            out_specs=pl.BlockSpec(shape, lambda i, *t: (t[-1][i], 0)),
            scratch_shapes=[pltpu.VMEM((WEIGHT_SLOTS, d, de), F32), pltpu.VMEM((WEIGHT_SLOTS, d, de), F32),
                            pltpu.VMEM((WEIGHT_SLOTS, de, d), F32),
                            pltpu.VMEM((d, de), BF16), pltpu.VMEM((d, de), BF16), pltpu.VMEM((de, d), BF16),
                            pltpu.SemaphoreType.DMA((WEIGHT_SLOTS,))]),
        out_shape=jax.ShapeDtypeStruct(xs.shape, jnp.uint32),
        compiler_params=_cparams("arbitrary"),
        name="moe_experts",
    )(*table, xs, w_gate, w_up, w_down)


def _combine_kernel(dest_ref, dest_next_ref, x1_ref, route_ref, mod_ref, lng_ref, lnb_ref, y_ref, o_ref,
                    buf_ref, sem, *, mod_row, rows_per_batch):
    tm, d = x1_ref.shape
    i = pl.program_id(0)
    n = pl.num_programs(0)
    slot = i % 2
    m = _mod_row(mod_ref, mod_row, tm, rows_per_batch)

    def gather(idx_ref, s):
        def body(r4, c):
            for u in range(DMA_UNROLL):
                r = r4 * DMA_UNROLL + u
                for k in range(2):
                    _row_copy(y_ref, idx_ref[0, 0, k * tm + r], buf_ref.at[s], k * tm + r,
                              sem.at[s]).start(priority=k)
            return c

        lax.fori_loop(0, tm // DMA_UNROLL, body, 0)

    @pl.when(i == 0)
    def _():
        gather(dest_ref, 0)

    @pl.when(i + 1 < n)
    def _():
        gather(dest_next_ref, 1 - slot)

    for k in range(2):
        pltpu.make_async_copy(y_ref.at[pl.ds(0, tm * ROW_WORDS)],
                              buf_ref.at[slot, pl.ds(0, tm * ROW_WORDS)], sem.at[slot]).wait()

    gate2 = m[:, 5 * d:6 * d]
    rt = route_ref[...]
    f = jnp.zeros((tm, d), F32)
    for k in range(2):
        lo, hi = _load_rows(buf_ref.at[slot], k * tm, tm)
        f = f + jnp.concatenate([lo, hi], axis=1) * rt[:, 2 + k:3 + k]
    o_ref[...] = _layer_norm(RES_ALPHA * x1_ref[...] + gate2 * f) * lng_ref[...] + lnb_ref[...]


def _combine(dest, x1, route, mod, ln_g, ln_b, y_tiles, *, row_off, rows, mod_row, rows_per_batch):
    d = x1.shape[1]
    tm = COMBINE_TILE
    steps = rows // tm
    off = row_off // tm
    kern = functools.partial(_combine_kernel, mod_row=mod_row, rows_per_batch=rows_per_batch)
    const = lambda a: pl.BlockSpec(a.shape, lambda i: (0,) * a.ndim)
    return pl.pallas_call(
        kern,
        grid=(steps,),
        in_specs=[pl.BlockSpec((1, 1, 2 * tm), lambda i: (off + i, 0, 0), memory_space=pltpu.SMEM),
                  pl.BlockSpec((1, 1, 2 * tm), lambda i: (off + jnp.minimum(i + 1, steps - 1), 0, 0),
                               memory_space=pltpu.SMEM),
                  pl.BlockSpec((tm, d), lambda i: (off + i, 0)),
                  pl.BlockSpec((tm, LANES), lambda i: (off + i, 0)),
                  const(mod), const(ln_g), const(ln_b), pl.BlockSpec(memory_space=pl.ANY)],
        out_specs=pl.BlockSpec((tm, d), lambda i: (i, 0)),
        out_shape=jax.ShapeDtypeStruct((rows, d), F32),
        scratch_shapes=[pltpu.VMEM((2, 2 * tm * ROW_WORDS, LANES), jnp.uint32), pltpu.SemaphoreType.DMA((2,))],
        compiler_params=_cparams("arbitrary"),
        name="moe_combine",
    )(dest, dest, x1, route, mod, ln_g, ln_b, y_tiles)


def _rope_tables(n_pos):
    rows = n_pos // GRID_W
    row = jnp.repeat(jnp.arange(rows), GRID_W).astype(F32)
    col = jnp.tile(jnp.arange(GRID_W), rows).astype(F32)
    n_freq = HEAD_DIM // 4
    freq = ROPE_BASE ** (-jnp.arange(n_freq, dtype=F32) / n_freq)
    ang_r, ang_c = row[:, None] * freq, col[:, None] * freq
    cos_h = jnp.concatenate([jnp.cos(ang_r)] * 2 + [jnp.cos(ang_c)] * 2, axis=1)
    sin_h = jnp.concatenate([-jnp.sin(ang_r), jnp.sin(ang_r), -jnp.sin(ang_c), jnp.sin(ang_c)], axis=1)
    return jnp.tile(cos_h, (1, 2)), jnp.tile(sin_h, (1, 2))


def _block_table(counts, n_blocks):
    cnt = counts.astype(jnp.int32)
    padded = (cnt + MOE_BLOCK - 1) // MOE_BLOCK * MOE_BLOCK
    pad_end = jnp.cumsum(padded)
    pad_start = pad_end - padded
    blk_start = jnp.arange(n_blocks, dtype=jnp.int32)[:, None] * MOE_BLOCK
    be = jnp.minimum(jnp.sum((pad_end[None, :] <= blk_start).astype(jnp.int32), axis=1), N_EXPERTS - 1)
    ids = jnp.arange(N_EXPERTS, dtype=jnp.int32)
    mine = be[:, None] == ids[None, :]
    fill = jnp.sum(jnp.where(mine, cnt[None, :] + pad_start[None, :], 0), axis=1) - blk_start[:, 0]
    bn = jnp.clip(fill, 0, MOE_BLOCK)
    prev = jnp.concatenate([jnp.full((1,), -1, jnp.int32), be[:-1]])
    first = ((bn > 0) & (be != prev)).astype(jnp.int32)
    slot = (jnp.cumsum(first) - 1) % WEIGHT_SLOTS
    later = (ids[None, :] > ids[:, None]) & (cnt[None, :] > 0)
    nxt_e = jnp.min(jnp.where(later, ids[None, :], N_EXPERTS), axis=1)
    hop = nxt_e[:, None] == ids[None, :]
    nxt2_e = jnp.sum(jnp.where(hop, nxt_e[None, :], 0), axis=1) + jnp.where(nxt_e == N_EXPERTS, N_EXPERTS, 0)
    lookup = lambda tab: jnp.sum(jnp.where(mine, jnp.where(tab >= N_EXPERTS, -1, tab)[None, :], 0), axis=1)
    last_step = jnp.maximum(jnp.sum((bn > 0).astype(jnp.int32)) - 1, 0) // MOE_STEP_BLOCKS
    step_idx = jnp.minimum(jnp.arange(n_blocks // MOE_STEP_BLOCKS, dtype=jnp.int32), last_step)
    return (be, bn, first, slot.astype(jnp.int32), lookup(nxt_e).astype(jnp.int32),
            lookup(nxt2_e).astype(jnp.int32), step_idx.astype(jnp.int32))


def _moe(route_t, h2_tiles, w_gate, w_up, w_down, layer):
    rows = route_t.shape[1]
    tm = ROW_TILE
    n_blocks = -(-(2 * rows) // MOE_BLOCK) + N_EXPERTS
    n_blocks = -(-n_blocks // MOE_STEP_BLOCKS) * MOE_STEP_BLOCKS
    dest8, counts = _plan(route_t, tm)
    dest = dest8[:, 0:2, :].reshape(rows // tm, 1, 2 * tm)
    table = _block_table(counts[:, 0], n_blocks)
    xs = _dispatch(dest, h2_tiles, n_blocks * MOE_BLOCK, tm)
    ys = _experts(table, xs, w_gate, w_up, w_down, layer)
    tc = COMBINE_TILE
    dest_c = dest8[:, 0:2, :].reshape(rows // tm, 2, tm // tc, tc).transpose(0, 2, 1, 3).reshape(rows // tc, 1, 2 * tc)
    return dest_c, ys


def kernel(x, c, ctx, c_ctx, w_ada, b_ada, w_in, w_fourier, attn_sink, w_pool, pool_scale, w_sgu, b_sgu,
           w_out, ln1_g, ln1_b, w_router_group, w_router_expert, w_exp_gate, w_exp_up, w_exp_down,
           ln2_g, ln2_b):
    b, s, d = x.shape
    n_ctx = ctx.shape[1]
    n_layers = w_in.shape[0]
    tm = ROW_TILE
    assert n_layers == DEPTH and s == FFT_R * FFT_R and s % tm == 0 and b + 1 <= SUBLANES
    assert (b * (s + n_ctx)) % tm == 0 and n_ctx % SGU_CHUNK == 0
    cond = jnp.concatenate([c, c_ctx[None, :], jnp.zeros((SUBLANES - b - 1, d), F32)], axis=0)
    mod_all = _ada(cond, w_ada, b_ada[:, None, :])
    cos_t, sin_t = _rope_tables(s)
    x2 = x.reshape(b * s, d)
    c2 = ctx.reshape(b * n_ctx, d)
    n_sgu = w_sgu.shape[1]
    for layer in range(n_layers):
        last = layer == n_layers - 1
        mod = mod_all[layer]
        w_in_l = w_in[layer].astype(BF16)
        wf = w_fourier[layer].astype(BF16)
        w_pool_bd = jax.scipy.linalg.block_diag(*[w_pool[layer, g] for g in range(w_pool.shape[1])]).astype(BF16)
        w_sgu_stack = w_sgu[layer].reshape(n_sgu * SGU_CHUNK, SGU_CHUNK).astype(BF16)
        b_sgu_exp = jnp.repeat(b_sgu[layer].T, GROUP_W // n_sgu, axis=1)
        w_router = jnp.concatenate([w_router_expert[layer].reshape(d, N_EXPERTS), w_router_group[layer]], axis=1)
        w_router = jnp.pad(w_router, ((0, 0), (0, LANES - w_router.shape[1])))
        wr_hi = w_router.astype(BF16)
        w_route = jnp.concatenate([wr_hi, (w_router - wr_hi.astype(F32)).astype(BF16)], axis=1)
        merge_consts = (mod, w_pool_bd, pool_scale[layer][None, :], w_sgu_stack, b_sgu_exp,
                        w_out[layer].astype(BF16), ln1_g[layer][None, :], ln1_b[layer][None, :], w_route)
        sink = attn_sink[layer]

        a, q, qs, k, v, p, ug = _proj(x2, mod, w_in_l, cos_t, sin_t, mod_row=None, rows_per_batch=s,
                                      rope=True, tm=tm, a_pitch=FFT_PITCH)
        ac, qc, qsc, kc, vc, pc, ugc = _proj(c2, mod, w_in_l, cos_t, sin_t, mod_row=b, rows_per_batch=n_ctx,
                                             rope=False, tm=n_ctx, a_pitch=FFT_R)
        y_four = _fourier(a, wf, s)
        y_attn = _attention(sink, q, qs, k, v, kc, vc, seq=s, n_ctx=n_ctx, band=True)
        total = b * s + (0 if last else b * n_ctx)
        merged = _merge(x2, p, ug, y_four, y_attn, *merge_consts, (),
                        mod_row=None, seq=s, tm=tm, row_off=0, total_rows=total)
        if not last:
            yc_four = _fourier_small(ac, wf, n_ctx)
            yc_attn = _attention(sink, qc, qsc, kc, vc, kc, vc, seq=n_ctx, n_ctx=n_ctx, band=False)
            merged = _merge(c2, pc, ugc, yc_four, yc_attn, *merge_consts, tuple(merged),
                            mod_row=b, seq=n_ctx, tm=n_ctx, row_off=b * s, total_rows=total)
        x1, h2_tiles, route, route_t = merged
        dest, ys = _moe(route_t, h2_tiles, w_exp_gate, w_exp_up, w_exp_down, layer)
        ln_g, ln_b = ln2_g[layer][None, :], ln2_b[layer][None, :]
        x2 = _combine(dest, x1, route, mod, ln_g, ln_b, ys, row_off=0, rows=b * s,
                      mod_row=None, rows_per_batch=s)
        if not last:
            c2 = _combine(dest, x1, route, mod, ln_g, ln_b, ys, row_off=b * s,
                          rows=b * n_ctx, mod_row=b, rows_per_batch=n_ctx)
    return x2.reshape(b, s, d)
```

```python
import functools
import math

import numpy as np
import jax
import jax.numpy as jnp
from jax import lax
from jax.experimental import pallas as pl
from jax.experimental.pallas import tpu as pltpu

GRID_W = 64
HEAD_DIM = 64
GROUP_W = 256
KV_W = 128
WINDOW = 128
POOL_WINDOWS = (2, 4, 8, 16)
SGU_CHUNK = 128
N_GROUPS = 4
EXPERTS_PER_GROUP = 8
N_EXPERTS = 32
ROPE_BASE = 10000.0
LN_EPS = 1e-6
NEG_INF = -1e30
DEPTH = 2
RES_ALPHA = (2 * DEPTH) ** 0.25

LANES = 128
SUBLANES = 8
VMEM_LIMIT = 48 * 1024 * 1024

ROW_TILE = 1024
Q_BLOCK = 2048
COMBINE_TILE = 512
WEIGHT_SLOTS = 3
MOE_BLOCK = 512
MOE_PART = 256
MOE_STEP_BLOCKS = 4
FFT_R = 64
FFT_PITCH = 72
FFT_UNROLL = 32
DMA_UNROLL = 8
PROJ_PART = 512
MERGE_PART = 512

BF16 = jnp.bfloat16
F32 = jnp.float32


def _cparams(*sem):
    return pltpu.CompilerParams(dimension_semantics=sem, vmem_limit_bytes=VMEM_LIMIT)


def _dot(a, b):
    return jnp.dot(a, b, preferred_element_type=F32)


def _dot_nt(a, b):
    return lax.dot_general(a, b, (((1,), (1,)), ((), ())), preferred_element_type=F32)


def _layer_norm(t):
    mu = jnp.mean(t, axis=-1, keepdims=True)
    d = t - mu
    var = jnp.mean(d * d, axis=-1, keepdims=True)
    return d * lax.rsqrt(var + LN_EPS)


def _silu(t):
    return t * (1.0 / (1.0 + jnp.exp(-t)))


def _gelu(t):
    return 0.5 * t * (1.0 + lax.erf(t * (1.0 / math.sqrt(2.0))))


ROW_WORDS = 4
HI_MASK = 0xFFFF0000


def _pack_rows(t):
    half = t.shape[1] // 2
    lo = lax.bitcast_convert_type(t[:, :half].astype(BF16).astype(F32), jnp.uint32)
    hi = lax.bitcast_convert_type(t[:, half:].astype(BF16).astype(F32), jnp.uint32)
    return (lo >> 16) | hi


def _unpack_rows(w):
    return (lax.bitcast_convert_type(w << 16, F32),
            lax.bitcast_convert_type(w & jnp.uint32(HI_MASK), F32))


def _store_rows(ref, t, first=0):
    w = _pack_rows(t)
    for j in range(ROW_WORDS):
        ref[pl.ds(first * ROW_WORDS + j, t.shape[0], stride=ROW_WORDS), :] = w[:, j * LANES:(j + 1) * LANES]


def _load_rows(ref, first, m):
    w = jnp.concatenate([ref[pl.ds(first * ROW_WORDS + j, m, stride=ROW_WORDS), :] for j in range(ROW_WORDS)],
                        axis=1)
    return _unpack_rows(w)


def _ada_kernel(c_ref, w_ref, b_ref, o_ref):
    s = _silu(c_ref[...]).astype(BF16)
    o_ref[...] = _dot(s, w_ref[...].astype(BF16)) + b_ref[...]


def _ada(cond, w_ada, b_ada):
    n_layers, d, n = w_ada.shape
    tn = n // 4
    return pl.pallas_call(
        _ada_kernel,
        grid=(n_layers, n // tn),
        in_specs=[
            pl.BlockSpec((SUBLANES, d), lambda l, j: (0, 0)),
            pl.BlockSpec((None, d, tn), lambda l, j: (l, 0, j)),
            pl.BlockSpec((None, 1, tn), lambda l, j: (l, 0, j)),
        ],
        out_specs=pl.BlockSpec((None, SUBLANES, tn), lambda l, j: (l, 0, j)),
        out_shape=jax.ShapeDtypeStruct((n_layers, SUBLANES, n), F32),
        compiler_params=_cparams("arbitrary", "arbitrary"),
        name="ada",
    )(cond, w_ada, b_ada)


def _rope(t, cos_t, sin_t):
    lane = lax.broadcasted_iota(jnp.int32, t.shape, 1)
    first = (lane % 32) < 16
    partner = jnp.where(first, pltpu.roll(t, LANES - 16, axis=1), pltpu.roll(t, 16, axis=1))
    return t * cos_t + partner * sin_t


def _proj_body(x_ref, m, w_ref, cos_ref, sin_ref, outs, *, rope, a_pitch, row0, pm):
    a_ref, q_ref, qs_ref, k_ref, v_ref, p_ref, ug_ref = outs
    rows = slice(row0, row0 + pm)
    x = x_ref[rows, :]
    d = x.shape[1]
    shift, scale = m[:, 0:d], m[:, d:2 * d]
    h = _layer_norm(x) * (1.0 + scale) + shift
    z = _dot(h.astype(BF16), w_ref[...])
    pad = jnp.zeros((a_pitch - FFT_R, LANES), F32)
    for g in range(pm // FFT_R):
        gg = row0 // FFT_R + g
        for hf in range(2):
            grp = z[g * FFT_R:(g + 1) * FFT_R, hf * LANES:(hf + 1) * LANES]
            if a_pitch > FFT_R:
                grp = jnp.concatenate([grp, pad], axis=0)
            a_ref[hf, gg * a_pitch:(gg + 1) * a_pitch, :] = grp
    q0, q1 = z[:, 256:384], z[:, 384:512]
    k = z[:, 512:640]
    if rope:
        cos_t, sin_t = cos_ref[rows, :], sin_ref[rows, :]
        q0, q1, k = _rope(q0, cos_t, sin_t), _rope(q1, cos_t, sin_t), _rope(k, cos_t, sin_t)
    q_ref[rows, 0:128] = q0.astype(BF16)
    q_ref[rows, 128:256] = q1.astype(BF16)
    qs_ref[rows, 0:128] = pltpu.roll(q0, HEAD_DIM, axis=1).astype(BF16)
    qs_ref[rows, 128:256] = pltpu.roll(q1, HEAD_DIM, axis=1).astype(BF16)
    k_ref[rows, :] = k.astype(BF16)
    v_ref[rows, :] = z[:, 640:768].astype(BF16)
    p_ref[rows, :] = z[:, 768:1024].astype(BF16)
    ug_ref[rows, :] = z[:, 1024:1536].astype(BF16)


def _mod_row(mod_ref, mod_row, tm, rows_per_batch):
    row = (pl.program_id(0) * tm) // rows_per_batch if mod_row is None else mod_row
    return mod_ref[pl.ds(row, 1), :]


def _proj_kernel(x_ref, mod_ref, w_ref, cos_ref, sin_ref, *outs, mod_row, rows_per_batch, rope, a_pitch):
    tm = x_ref.shape[0]
    m = _mod_row(mod_ref, mod_row, tm, rows_per_batch)
    pm = min(tm, PROJ_PART)
    for part in range(tm // pm):
        _proj_body(x_ref, m, w_ref, cos_ref, sin_ref, outs, rope=rope, a_pitch=a_pitch, row0=part * pm, pm=pm)


def _proj_kv_kernel(x_ref, mod_ref, w_ref, k_ref, v_ref, *, mod_row):
    d = x_ref.shape[1]
    m = mod_ref[pl.ds(mod_row, 1), :]
    h = _layer_norm(x_ref[...]) * (1.0 + m[:, d:2 * d]) + m[:, 0:d]
    z = _dot(h.astype(BF16), w_ref[...])
    k_ref[...] = z[:, 0:KV_W].astype(BF16)
    v_ref[...] = z[:, KV_W:].astype(BF16)


def _proj_kv(x2, mod, w_kv, *, mod_row):
    rows, d = x2.shape
    full = lambda a: pl.BlockSpec(a.shape, lambda i: (0,) * a.ndim)
    out = pl.BlockSpec((rows, KV_W), lambda i: (0, 0))
    return pl.pallas_call(
        functools.partial(_proj_kv_kernel, mod_row=mod_row),
        grid=(1,),
        in_specs=[full(x2), full(mod), full(w_kv)],
        out_specs=[out, out],
        out_shape=[jax.ShapeDtypeStruct((rows, KV_W), BF16)] * 2,
        compiler_params=_cparams("arbitrary"),
        name="proj_kv",
    )(x2, mod, w_kv)


def _proj_specs(rows, tm, a_pitch, seq_steps):
    row_spec = lambda w: pl.BlockSpec((tm, w), lambda i: (i, 0))
    out_w = (256, 256, 128, 128, 256, 512)
    ta = tm // FFT_R * a_pitch
    out_specs = [pl.BlockSpec((2, ta, LANES), lambda i: (0, i, 0))] + [row_spec(w) for w in out_w]
    out_shape = ([jax.ShapeDtypeStruct((2, rows // FFT_R * a_pitch, LANES), F32)]
                 + [jax.ShapeDtypeStruct((rows, w), BF16) for w in out_w])
    table_spec = pl.BlockSpec((tm, LANES), lambda i: (i % seq_steps, 0))
    return table_spec, out_specs, out_shape


def _proj(x2, mod, w_in, cos_t, sin_t, *, mod_row, rows_per_batch, rope, tm, a_pitch):
    rows, d = x2.shape
    kern = functools.partial(_proj_kernel, mod_row=mod_row, rows_per_batch=rows_per_batch,
                             rope=rope, a_pitch=a_pitch)
    table_spec, out_specs, out_shape = _proj_specs(rows, tm, a_pitch, cos_t.shape[0] // tm)
    return pl.pallas_call(
        kern,
        grid=(rows // tm,),
        in_specs=[
            pl.BlockSpec((tm, d), lambda i: (i, 0)),
            pl.BlockSpec(mod.shape, lambda i: (0, 0)),
            pl.BlockSpec(w_in.shape, lambda i: (0, 0)),
            table_spec, table_spec,
        ],
        out_specs=out_specs,
        out_shape=out_shape,
        compiler_params=_cparams("arbitrary"),
        name="proj",
    )(x2, mod, w_in, cos_t, sin_t)


def _fft_tables(n_pos):
    r = FFT_R
    assert n_pos == r * r
    kb = np.arange(r)[None, :, None]
    na = np.arange(r)[:, None, None]
    nb = np.arange(r)[None, None, :]
    ang = 2.0 * np.pi * ((kb * (na + r * nb)) % n_pos) / n_pos
    m1 = np.concatenate([np.cos(ang), -np.sin(ang)], axis=1)
    ka = np.arange(r)[:, None]
    n2 = np.arange(r)[None, :]
    ang2 = 2.0 * np.pi * ((ka * n2) % r) / r
    c2, s2 = np.cos(ang2), np.sin(ang2)
    w2 = np.block([[c2, s2], [-s2, c2]])
    return m1, w2


def _channel_tables(n_pos):
    h = HEAD_DIM
    c = np.arange(h)
    ang = 2.0 * np.pi * ((c[:, None] * c[None, :]) % h) / h
    scale = 1.0 / math.sqrt(n_pos * h)
    eye = np.eye(GROUP_W // h)
    cc = np.kron(eye, np.cos(ang)) * scale
    ss = np.kron(eye, np.sin(ang)) * scale
    return np.concatenate([cc, ss], axis=0)


def _fourier_kernel(a_ref, m1_ref, w2_ref, ch_ref, wf_ref, o_ref, z_ref, y_ref):
    r, pt = FFT_R, FFT_PITCH

    def step1(i, c):
        for u in range(FFT_UNROLL):
            na = i * FFT_UNROLL + u
            rows = jnp.concatenate([a_ref[0, pl.ds(na, r, stride=pt), :],
                                    a_ref[1, pl.ds(na, r, stride=pt), :]], axis=1)
            z = _dot(m1_ref[na], rows.astype(BF16))
            base = pl.multiple_of(na * pt, SUBLANES)
            z_ref[0, pl.ds(base, r), :] = z[0:r, 0:LANES]
            z_ref[1, pl.ds(base, r), :] = z[0:r, LANES:]
            z_ref[2, pl.ds(base, r), :] = z[r:, 0:LANES]
            z_ref[3, pl.ds(base, r), :] = z[r:, LANES:]
        return c

    lax.fori_loop(0, r // FFT_UNROLL, step1, 0)

    def step2(i, c):
        for u in range(FFT_UNROLL):
            kb = i * FFT_UNROLL + u
            q = [z_ref[j, pl.ds(kb, r, stride=pt), :] for j in range(4)]
            zs = jnp.concatenate([jnp.concatenate(q[0:2], axis=1),
                                  jnp.concatenate(q[2:4], axis=1)], axis=0)
            y = _dot(w2_ref[...], zs.astype(BF16))
            base = pl.multiple_of(kb * r, r)
            y_ref[0, pl.ds(base, r), :] = y[0:r, 0:LANES]
            y_ref[1, pl.ds(base, r), :] = y[0:r, LANES:]
            y_ref[2, pl.ds(base, r), :] = y[r:, 0:LANES]
            y_ref[3, pl.ds(base, r), :] = y[r:, LANES:]
        return c

    lax.fori_loop(0, r // FFT_UNROLL, step2, 0)

    chunk = 8 * r
    for cidx in range(r * r // chunk):
        yy = jnp.concatenate([y_ref[j, cidx * chunk:(cidx + 1) * chunk, :] for j in range(4)], axis=1)
        f = _dot(yy.astype(BF16), ch_ref[...])
        g = _dot(f.astype(BF16), wf_ref[...])
        for gi in range(chunk // r):
            kb = cidx * (chunk // r) + gi
            z_ref[0, kb * pt:kb * pt + r, :] = g[gi * r:(gi + 1) * r, 0:LANES]
            z_ref[1, kb * pt:kb * pt + r, :] = g[gi * r:(gi + 1) * r, LANES:]

    def step3(i, c):
        for u in range(FFT_UNROLL):
            ka = i * FFT_UNROLL + u
            base = pl.multiple_of(ka * r, r)
            o_ref[pl.ds(base, r), 0:LANES] = z_ref[0, pl.ds(ka, r, stride=pt), :]
            o_ref[pl.ds(base, r), LANES:] = z_ref[1, pl.ds(ka, r, stride=pt), :]
        return c

    lax.fori_loop(0, r // FFT_UNROLL, step3, 0)


def _fourier(a3, w_fourier, n_pos):
    rows = a3.shape[1] // FFT_PITCH * FFT_R
    gw = GROUP_W
    m1, w2 = _fft_tables(n_pos)
    ch = _channel_tables(n_pos)
    const = lambda shape: pl.BlockSpec(shape, lambda b: (0,) * len(shape))
    return pl.pallas_call(
        _fourier_kernel,
        grid=(rows // n_pos,),
        in_specs=[
            pl.BlockSpec((2, FFT_R * FFT_PITCH, LANES), lambda b: (0, b, 0)),
            const(m1.shape), const(w2.shape), const(ch.shape), const(w_fourier.shape),
        ],
        out_specs=pl.BlockSpec((n_pos, gw), lambda b: (b, 0)),
        out_shape=jax.ShapeDtypeStruct((rows, gw), F32),
        scratch_shapes=[pltpu.VMEM((4, FFT_R * FFT_PITCH, LANES), F32), pltpu.VMEM((4, n_pos, LANES), F32)],
        compiler_params=_cparams("arbitrary"),
        name="fourier",
    )(a3, jnp.asarray(m1, BF16), jnp.asarray(w2, BF16), jnp.asarray(ch, BF16), w_fourier)


def _fourier_small_kernel(a_ref, cs_ref, ch_ref, wf_ref, o_ref):
    n = a_ref.shape[1]
    a = jnp.concatenate([a_ref[0], a_ref[1]], axis=1)
    pq = _dot(cs_ref[...], a.astype(BF16))
    y = jnp.concatenate([pq[0:n], pq[n:2 * n]], axis=1).astype(BF16)
    f = _dot(y, ch_ref[...])
    o_ref[...] = _dot(f.astype(BF16), wf_ref[...])


def _fourier_small(a3, w_fourier, n_pos):
    _, rows, _ = a3.shape
    gw = GROUP_W
    k = np.arange(n_pos)
    ang = 2.0 * np.pi * ((k[:, None] * k[None, :]) % n_pos) / n_pos
    cs = np.concatenate([np.cos(ang), -np.sin(ang)], axis=0)
    ch = _channel_tables(n_pos)
    const = lambda shape: pl.BlockSpec(shape, lambda b: (0,) * len(shape))
    return pl.pallas_call(
        _fourier_small_kernel,
        grid=(rows // n_pos,),
        in_specs=[pl.BlockSpec((2, n_pos, LANES), lambda b: (0, b, 0)),
                  const(cs.shape), const(ch.shape), const(w_fourier.shape)],
        out_specs=pl.BlockSpec((n_pos, gw), lambda b: (b, 0)),
        out_shape=jax.ShapeDtypeStruct((rows, gw), F32),
        compiler_params=_cparams("arbitrary"),
        name="fourier_ctx",
    )(a3, jnp.asarray(cs, BF16), jnp.asarray(ch, BF16), w_fourier)


ATTN_SUB = 128


def _attn_kernel(sink_ref, q_ref, qs_ref, k_ref, v_ref, kc_ref, vc_ref, o_ref, *, band, seq):
    qb = q_ref.shape[0]
    sub = ATTN_SUB
    lane = lax.broadcasted_iota(jnp.int32, (1, LANES), 1)
    lo_half = lane < HEAD_DIM
    zero = jnp.zeros((), BF16)
    scale = jnp.asarray(HEAD_DIM ** -0.5, BF16)
    kw = sub + 2 * WINDOW
    for sb in range(qb // sub):
        rows = slice(sb * sub, (sb + 1) * sub)
        qa0, qa1 = q_ref[rows, 0:LANES], q_ref[rows, LANES:]
        qs0, qs1 = qs_ref[rows, 0:LANES], qs_ref[rows, LANES:]
        q_all = jnp.concatenate([jnp.where(lo_half, qa0, zero), jnp.where(lo_half, qs0, zero),
                                 jnp.where(lo_half, zero, qs1), jnp.where(lo_half, zero, qa1)], axis=0) * scale
        if band:
            p0 = pl.program_id(1) * qb + sb * sub
            start = pl.multiple_of(jnp.clip(p0 - WINDOW, 0, seq - kw), WINDOW)
            qpos = p0 + lax.broadcasted_iota(jnp.int32, (sub, 1), 0)
            kpos = start + lax.broadcasted_iota(jnp.int32, (1, kw), 1)
            bias = jnp.where(jnp.abs(qpos - kpos) <= WINDOW, 0.0, NEG_INF)
            keys = jnp.concatenate([k_ref[pl.ds(start, kw), :], kc_ref[...]], axis=0)
            vals = jnp.concatenate([v_ref[pl.ds(start, kw), :], vc_ref[...]], axis=0)
        else:
            keys, vals = kc_ref[...], vc_ref[...]
        s_all = _dot_nt(q_all, keys)
        probs, dens = [], []
        for h in range(4):
            s = s_all[h * sub:(h + 1) * sub, :]
            sink = sink_ref[h]
            if band:
                s = jnp.concatenate([s[:, 0:kw] + bias, s[:, kw:]], axis=1)
            m = jnp.maximum(jnp.max(s, axis=1, keepdims=True), sink)
            p = jnp.exp(s - m)
            dens.append(jnp.sum(p, axis=1, keepdims=True) + jnp.exp(sink - m))
            probs.append(p.astype(BF16))
        o_all = _dot(jnp.concatenate(probs, axis=0), vals)
        o = [o_all[h * sub:(h + 1) * sub, :] / dens[h] for h in range(4)]
        o_ref[rows, 0:LANES] = jnp.where(lo_half, o[0], pltpu.roll(o[1], HEAD_DIM, axis=1)).astype(BF16)
        o_ref[rows, LANES:] = jnp.where(lo_half, pltpu.roll(o[2], HEAD_DIM, axis=1), o[3]).astype(BF16)


def _attention(sink, q, qs, k, v, kc, vc, *, seq, n_ctx, band):
    rows = q.shape[0]
    n_batch = rows // seq
    qb = Q_BLOCK if band else seq
    steps = seq // qb
    kern = functools.partial(_attn_kernel, band=band, seq=seq)
    seq_spec = pl.BlockSpec((seq, KV_W), lambda b, i: (b, 0))
    ctx_spec = pl.BlockSpec((n_ctx, KV_W), lambda b, i: (b, 0))
    q_spec = pl.BlockSpec((qb, GROUP_W), lambda b, i: (b * steps + i, 0))
    return pl.pallas_call(
        kern,
        grid=(n_batch, steps),
        in_specs=[pl.BlockSpec(memory_space=pltpu.SMEM), q_spec, q_spec,
                  seq_spec, seq_spec, ctx_spec, ctx_spec],
        out_specs=q_spec,
        out_shape=jax.ShapeDtypeStruct((rows, GROUP_W), BF16),
        compiler_params=_cparams("arbitrary", "arbitrary"),
        name="attn" if band else "attn_ctx",
    )(sink, q, qs, k, v, kc, vc)


POOL_HALO = max(POOL_WINDOWS) // 2


def _pool(p_ref, t0, tm, seq):
    halo = POOL_HALO
    pack = 2 * SUBLANES
    t0 = pl.multiple_of(t0, pack)
    main = p_ref[pl.ds(t0, tm), :].astype(F32)
    lo = pl.multiple_of(jnp.maximum(t0 - pack, 0), pack)
    hi = pl.multiple_of(jnp.minimum(t0 + tm, seq - pack), pack)
    prev = p_ref[pl.ds(lo, pack), :].astype(F32)[pack - halo:, :]
    nxt = p_ref[pl.ds(hi, pack), :].astype(F32)[:halo, :]
    prev = jnp.where(t0 > 0, prev, 0.0)
    nxt = jnp.where(t0 + tm < seq, nxt, 0.0)
    full = jnp.concatenate([prev, main, nxt], axis=0)
    n = tm + 2 * halo
    gch = GROUP_W // len(POOL_WINDOWS)
    first = lax.broadcasted_iota(jnp.int32, (1, LANES), 1) < gch
    means = []
    for hf in range(GROUP_W // LANES):
        wa, wb = POOL_WINDOWS[2 * hf], POOL_WINDOWS[2 * hf + 1]
        x = full[:, hf * LANES:(hf + 1) * LANES]
        sums, w, s = {}, 2, pltpu.roll(x, 1, axis=0) + x
        sums[w] = s
        while w < wb:
            s = pltpu.roll(s, w // 2, axis=0) + pltpu.roll(s, n - w // 2, axis=0)
            w *= 2
            sums[w] = s
        means.append(jnp.where(first, sums[wa] * (1.0 / wa), sums[wb] * (1.0 / wb))[halo:halo + tm, :])
    mean = jnp.concatenate(means, axis=1)
    win = jnp.concatenate([jnp.full((1, gch), w, jnp.int32) for w in POOL_WINDOWS], axis=1)

    def rescale(rows, first_pos):
        pos = first_pos + lax.broadcasted_iota(jnp.int32, (halo, 1), 0)
        cnt = jnp.minimum(pos + win // 2, seq) - jnp.maximum(pos - win // 2, 0)
        return rows * (win.astype(F32) / cnt.astype(F32))

    mean = jnp.concatenate([rescale(mean[:halo], t0), mean[halo:tm - halo],
                            rescale(mean[tm - halo:], t0 + tm - halo)], axis=0)
    return mean - main


def _route(logits):
    tm = logits.shape[0]
    lt = logits.T
    gl = lt[N_EXPERTS:N_EXPERTS + N_GROUPS]
    sub_g = lax.broadcasted_iota(jnp.int32, gl.shape, 0)
    gmax = jnp.max(gl, axis=0, keepdims=True)
    grp = jnp.min(jnp.where(gl == gmax, sub_g, N_GROUPS), axis=0, keepdims=True)
    gate_group = 1.0 / jnp.sum(jnp.exp(gl - gmax), axis=0, keepdims=True)
    el = lt[0:EXPERTS_PER_GROUP]
    for g in range(1, N_GROUPS):
        el = jnp.where(grp == g, lt[g * EXPERTS_PER_GROUP:(g + 1) * EXPERTS_PER_GROUP], el)
    sub = lax.broadcasted_iota(jnp.int32, el.shape, 0)
    m1 = jnp.max(el, axis=0, keepdims=True)
    i1 = jnp.min(jnp.where(el == m1, sub, EXPERTS_PER_GROUP), axis=0, keepdims=True)
    el2 = jnp.where(sub == i1, -jnp.inf, el)
    m2 = jnp.max(el2, axis=0, keepdims=True)
    i2 = jnp.min(jnp.where(el2 == m2, sub, EXPERTS_PER_GROUP), axis=0, keepdims=True)
    r = jnp.exp(m2 - m1)
    g1 = gate_group / (1.0 + r)
    g2 = g1 * r
    e1 = (grp * EXPERTS_PER_GROUP + i1).astype(F32)
    e2 = (grp * EXPERTS_PER_GROUP + i2).astype(F32)
    rows = jnp.where(sub == 0, e1, jnp.where(sub == 1, e2, jnp.where(sub == 2, g1, jnp.where(sub == 3, g2, 0.0))))
    cols = jnp.concatenate([rows, jnp.zeros((LANES - rows.shape[0], tm), F32)], axis=0).T
    return cols, rows


def _merge_kernel(x_ref, p_ref, ug_ref, yf_ref, ya_ref, mod_ref, wpool_ref, pscale_ref, wsgu_ref, bsgu_ref,
                  wout_ref, lng_ref, lnb_ref, wr_ref, *rest,
                  mod_row, seq, n_alias):
    x1_ref, h2_ref, route_ref, route_t_ref = rest[n_alias:]
    tm, d = x_ref.shape
    if mod_row is None:
        row = pl.program_id(0)
    else:
        row = mod_row
    t0 = pl.multiple_of(pl.program_id(1) * tm, tm)
    m = mod_ref[pl.ds(row, 1), :]
    gate1, shift2, scale2 = m[:, 2 * d:3 * d], m[:, 3 * d:4 * d], m[:, 4 * d:5 * d]
    lane = lax.broadcasted_iota(jnp.int32, (1, GROUP_W), 1)
    n_heads = wsgu_ref.shape[0] // SGU_CHUNK
    head = lane // (GROUP_W // n_heads)

    pm = min(tm, MERGE_PART)
    for part in range(tm // pm):
        r0 = part * pm
        rows = slice(r0, r0 + pm)
        pooled = _pool(p_ref, t0 + r0, pm, seq)
        y_pool = _dot(pooled.astype(BF16), wpool_ref[...]) * pscale_ref[...]

        ug = ug_ref[rows, :].astype(F32)
        u = _gelu(ug[:, 0:GROUP_W])
        v = _layer_norm(_gelu(ug[:, GROUP_W:])).astype(BF16)
        mixed = []
        for cidx in range(pm // SGU_CHUNK):
            vc = v[cidx * SGU_CHUNK:(cidx + 1) * SGU_CHUNK, :]
            full = _dot(wsgu_ref[...], vc)
            mc = bsgu_ref[...]
            for hd in range(n_heads):
                mc = mc + jnp.where(head == hd, full[hd * SGU_CHUNK:(hd + 1) * SGU_CHUNK, :], 0.0)
            mixed.append(mc)
        y_sgu = u * jnp.concatenate(mixed, axis=0)

        cat = jnp.concatenate([yf_ref[rows, :].astype(BF16), ya_ref[rows, :], y_pool.astype(BF16),
                               y_sgu.astype(BF16)], axis=1)
        y = _dot(cat, wout_ref[...])
        x1 = _layer_norm(RES_ALPHA * x_ref[rows, :] + gate1 * y) * lng_ref[...] + lnb_ref[...]
        x1_ref[rows, :] = x1
        h2 = _layer_norm(x1) * (1.0 + scale2) + shift2
        _store_rows(h2_ref, h2, first=r0)
        lg = _dot(h2.astype(BF16), wr_ref[...])
        route_ref[rows, :], route_t_ref[:, rows] = _route(lg[:, 0:LANES] + lg[:, LANES:])


def _merge(x2, p, ug, y_four, y_attn, mod, w_pool_bd, pool_scale, w_sgu_stack, b_sgu_exp, w_out,
           ln_g, ln_b, w_route, aliased, *, mod_row, seq, tm, row_off, total_rows):
    rows, d = x2.shape
    n_batch, steps = rows // seq, seq // tm
    off = row_off // tm
    kern = functools.partial(_merge_kernel, mod_row=mod_row, seq=seq, n_alias=len(aliased))
    row_spec = lambda w: pl.BlockSpec((tm, w), lambda b, i: (b * steps + i, 0))
    const = lambda a: pl.BlockSpec(a.shape, lambda b, i: (0,) * a.ndim)
    consts = (mod, w_pool_bd, pool_scale, w_sgu_stack, b_sgu_exp, w_out, ln_g, ln_b, w_route)
    n_in = 5 + len(consts)
    out_shapes = [jax.ShapeDtypeStruct((total_rows, d), F32),
                  jax.ShapeDtypeStruct((total_rows * ROW_WORDS, LANES), jnp.uint32),
                  jax.ShapeDtypeStruct((total_rows, LANES), F32),
                  jax.ShapeDtypeStruct((SUBLANES, total_rows), F32)]
    out_specs = [pl.BlockSpec((tm, d), lambda b, i: (off + b * steps + i, 0)),
                 pl.BlockSpec((tm * ROW_WORDS, LANES), lambda b, i: (off + b * steps + i, 0)),
                 pl.BlockSpec((tm, LANES), lambda b, i: (off + b * steps + i, 0)),
                 pl.BlockSpec((SUBLANES, tm), lambda b, i: (0, off + b * steps + i))]
    return pl.pallas_call(
        kern,
        grid=(n_batch, steps),
        in_specs=[row_spec(d), pl.BlockSpec((seq, GROUP_W), lambda b, i: (b, 0)),
                  row_spec(2 * GROUP_W), row_spec(GROUP_W), row_spec(GROUP_W)]
                 + [const(a) for a in consts]
                 + [pl.BlockSpec(memory_space=pl.ANY)] * len(aliased),
        out_specs=out_specs,
        out_shape=out_shapes,
        input_output_aliases={n_in + k: k for k in range(len(aliased))},
        compiler_params=_cparams("arbitrary", "arbitrary"),
        name="merge",
    )(x2, p, ug, y_four, y_attn, *consts, *aliased)


def _plan_kernel(route_ref, dest_ref, cnt_out_ref, cnt_ref, start_ref, carry_ref):
    ph, t = pl.program_id(0), pl.program_id(1)
    tm = route_ref.shape[1]
    rt = route_ref[...]
    e1 = rt[0:1, :].astype(jnp.int32)
    e2 = rt[1:2, :].astype(jnp.int32)
    sub = lax.broadcasted_iota(jnp.int32, (N_EXPERTS, tm), 0)
    hit1, hit2 = sub == e1, sub == e2
    onehot = jnp.where(hit1 | hit2, 1.0, 0.0)
    tile_cnt = jnp.sum(onehot, axis=1, keepdims=True)

    @pl.when((ph == 0) & (t == 0))
    def _():
        cnt_ref[...] = jnp.zeros_like(cnt_ref)

    @pl.when(ph == 0)
    def _():
        cnt_ref[...] += tile_cnt

    @pl.when((ph == 1) & (t == 0))
    def _():
        cnt = cnt_ref[...]
        padded = jnp.floor((cnt + (MOE_BLOCK - 1.0)) * (1.0 / MOE_BLOCK)) * MOE_BLOCK
        row = lax.broadcasted_iota(jnp.int32, cnt.shape, 0)
        incl = padded
        sh = 1
        while sh < N_EXPERTS:
            incl = incl + jnp.where(row >= sh, pltpu.roll(incl, sh, axis=0), 0.0)
            sh *= 2
        start_ref[...] = incl - padded
        carry_ref[...] = jnp.zeros_like(carry_ref)
        cnt_out_ref[...] = cnt

    @pl.when(ph == 1)
    def _():
        r_i = lax.broadcasted_iota(jnp.int32, (tm, tm), 0)
        c_i = lax.broadcasted_iota(jnp.int32, (tm, tm), 1)
        before = jnp.where(r_i < c_i, 1.0, 0.0).astype(BF16)
        rank = _dot(onehot.astype(BF16), before)
        base = start_ref[:, 0:1] + carry_ref[:, 0:1] + rank
        d1 = jnp.sum(jnp.where(hit1, base, 0.0), axis=0, keepdims=True)
        d2 = jnp.sum(jnp.where(hit2, base, 0.0), axis=0, keepdims=True)
        sub8 = lax.broadcasted_iota(jnp.int32, (SUBLANES, tm), 0)
        dest_ref[...] = jnp.where(sub8 == 0, d1, d2).astype(jnp.int32)
        carry_ref[...] += tile_cnt


def _plan(route_t, tm):
    rows = route_t.shape[1]
    n_t = rows // tm
    return pl.pallas_call(
        _plan_kernel,
        grid=(2, n_t),
        in_specs=[pl.BlockSpec((SUBLANES, tm), lambda ph, t: (0, t))],
        out_specs=[pl.BlockSpec((None, SUBLANES, tm), lambda ph, t: (t * ph, 0, 0)),
                   pl.BlockSpec((N_EXPERTS, LANES), lambda ph, t: (0, 0))],
        out_shape=[jax.ShapeDtypeStruct((n_t, SUBLANES, tm), jnp.int32),
                   jax.ShapeDtypeStruct((N_EXPERTS, LANES), F32)],
        scratch_shapes=[pltpu.VMEM((N_EXPERTS, LANES), F32)] * 3,
        compiler_params=_cparams("arbitrary", "arbitrary"),
        name="moe_plan",
    )(route_t)


def _row_copy(src_ref, src_row, dst_ref, dst_row, sem):
    return pltpu.make_async_copy(
        src_ref.at[pl.ds(pl.multiple_of(src_row * ROW_WORDS, ROW_WORDS), ROW_WORDS)],
        dst_ref.at[pl.ds(pl.multiple_of(dst_row * ROW_WORDS, ROW_WORDS), ROW_WORDS)], sem)


def _dispatch_kernel(dest_ref, h2_ref, xs_ref, sem, *, tm):
    def body(r4, c):
        for u in range(DMA_UNROLL):
            r = r4 * DMA_UNROLL + u
            for k in range(2):
                _row_copy(h2_ref, r, xs_ref, dest_ref[0, 0, k * tm + r], sem).start(priority=k)
        return c

    lax.fori_loop(0, tm // DMA_UNROLL, body, 0)
    for k in range(2):
        pltpu.make_async_copy(h2_ref, xs_ref.at[pl.ds(0, tm * ROW_WORDS)], sem).wait()


def _dispatch(dest, h2_tiles, n_slots, tm):
    n_t = dest.shape[0]
    return pl.pallas_call(
        functools.partial(_dispatch_kernel, tm=tm),
        grid=(n_t,),
        in_specs=[pl.BlockSpec((1, 1, 2 * tm), lambda i: (i, 0, 0), memory_space=pltpu.SMEM),
                  pl.BlockSpec((tm * ROW_WORDS, LANES), lambda i: (i, 0))],
        out_specs=pl.BlockSpec(memory_space=pl.ANY),
        out_shape=jax.ShapeDtypeStruct((n_slots * ROW_WORDS, LANES), jnp.uint32),
        scratch_shapes=[pltpu.SemaphoreType.DMA],
        compiler_params=_cparams("arbitrary"),
        name="moe_dispatch",
    )(dest, h2_tiles)


def _expert_kernel(be_ref, bn_ref, first_ref, slot_ref, nxt_ref, nxt2_ref, _blk_ref,
                   x_ref, wg_hbm, wu_hbm, wd_hbm, y_ref,
                   wg_buf, wu_buf, wd_buf, wg_bf, wu_bf, wd_bf, sem, *, layer):
    mb = MOE_BLOCK
    half = wg_bf.shape[0] // 2

    def weight_copies(e, s):
        return [pltpu.make_async_copy(wg_hbm.at[layer, e], wg_buf.at[s], sem.at[s]),
                pltpu.make_async_copy(wu_hbm.at[layer, e], wu_buf.at[s], sem.at[s]),
                pltpu.make_async_copy(wd_hbm.at[layer, e], wd_buf.at[s], sem.at[s])]

    @pl.when(pl.program_id(0) == 0)
    def _():
        for cp in weight_copies(be_ref[0], 0):
            cp.start()

        @pl.when(nxt_ref[0] >= 0)
        def _():
            for cp in weight_copies(nxt_ref[0], 1):
                cp.start(priority=1)

    def block(i, row0):
        @pl.when(first_ref[i] == 1)
        def _():
            s = slot_ref[i]
            for cp in weight_copies(be_ref[i], s):
                cp.wait()

            @pl.when(nxt2_ref[i] >= 0)
            def _():
                s2 = jnp.where(s == 0, WEIGHT_SLOTS - 1, s - 1)
                for cp in weight_copies(nxt2_ref[i], s2):
                    cp.start(priority=1)

            wg_bf[...] = wg_buf[s].astype(BF16)
            wu_bf[...] = wu_buf[s].astype(BF16)
            wd_bf[...] = wd_buf[s].astype(BF16)

        def compute(m):
            live = lax.broadcasted_iota(jnp.int32, (m, 1), 0) < bn_ref[i]
            x_lo, x_hi = _load_rows(x_ref, row0, m)
            x_lo = jnp.where(live, x_lo, 0.0).astype(BF16)
            x_hi = jnp.where(live, x_hi, 0.0).astype(BF16)
            g = _dot(x_lo, wg_bf[0:half, :]) + _dot(x_hi, wg_bf[half:, :])
            u = _dot(x_lo, wu_bf[0:half, :]) + _dot(x_hi, wu_bf[half:, :])
            hid = (_silu(g) * u).astype(BF16)
            _store_rows(y_ref, _dot(hid, wd_bf[...]), first=row0)

        for parts in range(1, mb // MOE_PART + 1):
            @pl.when((bn_ref[i] > (parts - 1) * MOE_PART) & (bn_ref[i] <= parts * MOE_PART))
            def _(parts=parts):
                compute(parts * MOE_PART)

    for sb in range(MOE_STEP_BLOCKS):
        block(pl.program_id(0) * MOE_STEP_BLOCKS + sb, sb * mb)


def _experts(table, xs, w_gate, w_up, w_down, layer):
    n_blocks = table[0].shape[0]
    _, _, d, de = w_gate.shape
    shape = (MOE_STEP_BLOCKS * MOE_BLOCK * ROW_WORDS, LANES)
    hbm = pl.BlockSpec(memory_space=pl.ANY)
    return pl.pallas_call(
        functools.partial(_expert_kernel, layer=layer),
        grid_spec=pltpu.PrefetchScalarGridSpec(
            num_scalar_prefetch=len(table),
            grid=(n_blocks // MOE_STEP_BLOCKS,),
            in_specs=[pl.BlockSpec(shape, lambda i, *t: (t[-1][i], 0)), hbm, hbm, hbm],
            out_specs=pl.BlockSpec(shape, lambda i, *t: (t[-1][i], 0)),
            scratch_shapes=[pltpu.VMEM((WEIGHT_SLOTS, d, de), F32), pltpu.VMEM((WEIGHT_SLOTS, d, de), F32),
                            pltpu.VMEM((WEIGHT_SLOTS, de, d), F32),
                            pltpu.VMEM((d, de), BF16), pltpu.VMEM((d, de), BF16), pltpu.VMEM((de, d), BF16),
                            pltpu.SemaphoreType.DMA((WEIGHT_SLOTS,))]),
        out_shape=jax.ShapeDtypeStruct(xs.shape, jnp.uint32),
        compiler_params=_cparams("arbitrary"),
        name="moe_experts",
    )(*table, xs, w_gate, w_up, w_down)


def _combine_kernel(dest_ref, dest_next_ref, x1_ref, route_ref, mod_ref, lng_ref, lnb_ref, y_ref, o_ref,
                    buf_ref, sem, *, mod_row, rows_per_batch):
    tm, d = x1_ref.shape
    i = pl.program_id(0)
    n = pl.num_programs(0)
    slot = i % 2
    m = _mod_row(mod_ref, mod_row, tm, rows_per_batch)

    def gather(idx_ref, s):
        def body(r4, c):
            for u in range(DMA_UNROLL):
                r = r4 * DMA_UNROLL + u
                for k in range(2):
                    _row_copy(y_ref, idx_ref[0, 0, k * tm + r], buf_ref.at[s], k * tm + r,
                              sem.at[s]).start(priority=k)
            return c

        lax.fori_loop(0, tm // DMA_UNROLL, body, 0)

    @pl.when(i == 0)
    def _():
        gather(dest_ref, 0)

    @pl.when(i + 1 < n)
    def _():
        gather(dest_next_ref, 1 - slot)

    for k in range(2):
        pltpu.make_async_copy(y_ref.at[pl.ds(0, tm * ROW_WORDS)],
                              buf_ref.at[slot, pl.ds(0, tm * ROW_WORDS)], sem.at[slot]).wait()

    gate2 = m[:, 5 * d:6 * d]
    rt = route_ref[...]
    f = jnp.zeros((tm, d), F32)
    for k in range(2):
        lo, hi = _load_rows(buf_ref.at[slot], k * tm, tm)
        f = f + jnp.concatenate([lo, hi], axis=1) * rt[:, 2 + k:3 + k]
    o_ref[...] = _layer_norm(RES_ALPHA * x1_ref[...] + gate2 * f) * lng_ref[...] + lnb_ref[...]


def _combine(dest, x1, route, mod, ln_g, ln_b, y_tiles, *, row_off, rows, mod_row, rows_per_batch):
    d = x1.shape[1]
    tm = COMBINE_TILE
    steps = rows // tm
    off = row_off // tm
    kern = functools.partial(_combine_kernel, mod_row=mod_row, rows_per_batch=rows_per_batch)
    const = lambda a: pl.BlockSpec(a.shape, lambda i: (0,) * a.ndim)
    return pl.pallas_call(
        kern,
        grid=(steps,),
        in_specs=[pl.BlockSpec((1, 1, 2 * tm), lambda i: (off + i, 0, 0), memory_space=pltpu.SMEM),
                  pl.BlockSpec((1, 1, 2 * tm), lambda i: (off + jnp.minimum(i + 1, steps - 1), 0, 0),
                               memory_space=pltpu.SMEM),
                  pl.BlockSpec((tm, d), lambda i: (off + i, 0)),
                  pl.BlockSpec((tm, LANES), lambda i: (off + i, 0)),
                  const(mod), const(ln_g), const(ln_b), pl.BlockSpec(memory_space=pl.ANY)],
        out_specs=pl.BlockSpec((tm, d), lambda i: (i, 0)),
        out_shape=jax.ShapeDtypeStruct((rows, d), F32),
        scratch_shapes=[pltpu.VMEM((2, 2 * tm * ROW_WORDS, LANES), jnp.uint32), pltpu.SemaphoreType.DMA((2,))],
        compiler_params=_cparams("arbitrary"),
        name="moe_combine",
    )(dest, dest, x1, route, mod, ln_g, ln_b, y_tiles)


def _rope_tables(n_pos):
    rows = n_pos // GRID_W
    row = jnp.repeat(jnp.arange(rows), GRID_W).astype(F32)
    col = jnp.tile(jnp.arange(GRID_W), rows).astype(F32)
    n_freq = HEAD_DIM // 4
    freq = ROPE_BASE ** (-jnp.arange(n_freq, dtype=F32) / n_freq)
    ang_r, ang_c = row[:, None] * freq, col[:, None] * freq
    cos_h = jnp.concatenate([jnp.cos(ang_r)] * 2 + [jnp.cos(ang_c)] * 2, axis=1)
    sin_h = jnp.concatenate([-jnp.sin(ang_r), jnp.sin(ang_r), -jnp.sin(ang_c), jnp.sin(ang_c)], axis=1)
    return jnp.tile(cos_h, (1, 2)), jnp.tile(sin_h, (1, 2))


def _block_table(counts, n_blocks):
    cnt = counts.astype(jnp.int32)
    padded = (cnt + MOE_BLOCK - 1) // MOE_BLOCK * MOE_BLOCK
    pad_end = jnp.cumsum(padded)
    pad_start = pad_end - padded
    blk_start = jnp.arange(n_blocks, dtype=jnp.int32)[:, None] * MOE_BLOCK
    be = jnp.minimum(jnp.sum((pad_end[None, :] <= blk_start).astype(jnp.int32), axis=1), N_EXPERTS - 1)
    ids = jnp.arange(N_EXPERTS, dtype=jnp.int32)
    mine = be[:, None] == ids[None, :]
    fill = jnp.sum(jnp.where(mine, cnt[None, :] + pad_start[None, :], 0), axis=1) - blk_start[:, 0]
    bn = jnp.clip(fill, 0, MOE_BLOCK)
    prev = jnp.concatenate([jnp.full((1,), -1, jnp.int32), be[:-1]])
    first = ((bn > 0) & (be != prev)).astype(jnp.int32)
    slot = (jnp.cumsum(first) - 1) % WEIGHT_SLOTS
    later = (ids[None, :] > ids[:, None]) & (cnt[None, :] > 0)
    nxt_e = jnp.min(jnp.where(later, ids[None, :], N_EXPERTS), axis=1)
    hop = nxt_e[:, None] == ids[None, :]
    nxt2_e = jnp.sum(jnp.where(hop, nxt_e[None, :], 0), axis=1) + jnp.where(nxt_e == N_EXPERTS, N_EXPERTS, 0)
    lookup = lambda tab: jnp.sum(jnp.where(mine, jnp.where(tab >= N_EXPERTS, -1, tab)[None, :], 0), axis=1)
    last_step = jnp.maximum(jnp.sum((bn > 0).astype(jnp.int32)) - 1, 0) // MOE_STEP_BLOCKS
    step_idx = jnp.minimum(jnp.arange(n_blocks // MOE_STEP_BLOCKS, dtype=jnp.int32), last_step)
    return (be, bn, first, slot.astype(jnp.int32), lookup(nxt_e).astype(jnp.int32),
            lookup(nxt2_e).astype(jnp.int32), step_idx.astype(jnp.int32))


def _moe(route_t, h2_tiles, w_gate, w_up, w_down, layer):
    rows = route_t.shape[1]
    tm = ROW_TILE
    n_blocks = -(-(2 * rows) // MOE_BLOCK) + N_EXPERTS
    n_blocks = -(-n_blocks // MOE_STEP_BLOCKS) * MOE_STEP_BLOCKS
    dest8, counts = _plan(route_t, tm)
    dest = dest8[:, 0:2, :].reshape(rows // tm, 1, 2 * tm)
    table = _block_table(counts[:, 0], n_blocks)
    xs = _dispatch(dest, h2_tiles, n_blocks * MOE_BLOCK, tm)
    ys = _experts(table, xs, w_gate, w_up, w_down, layer)
    tc = COMBINE_TILE
    dest_c = dest8[:, 0:2, :].reshape(rows // tm, 2, tm // tc, tc).transpose(0, 2, 1, 3).reshape(rows // tc, 1, 2 * tc)
    return dest_c, ys


def kernel(x, c, ctx, c_ctx, w_ada, b_ada, w_in, w_fourier, attn_sink, w_pool, pool_scale, w_sgu, b_sgu,
           w_out, ln1_g, ln1_b, w_router_group, w_router_expert, w_exp_gate, w_exp_up, w_exp_down,
           ln2_g, ln2_b):
    b, s, d = x.shape
    n_ctx = ctx.shape[1]
    n_layers = w_in.shape[0]
    tm = ROW_TILE
    assert n_layers == DEPTH and s == FFT_R * FFT_R and s % tm == 0 and b + 1 <= SUBLANES
    assert (b * (s + n_ctx)) % tm == 0 and n_ctx % SGU_CHUNK == 0
    cond = jnp.concatenate([c, c_ctx[None, :], jnp.zeros((SUBLANES - b - 1, d), F32)], axis=0)
    mod_all = _ada(cond, w_ada, b_ada[:, None, :])
    cos_t, sin_t = _rope_tables(s)
    x2 = x.reshape(b * s, d)
    c2 = ctx.reshape(b * n_ctx, d)
    n_sgu = w_sgu.shape[1]
    for layer in range(n_layers):
        last = layer == n_layers - 1
        mod = mod_all[layer]
        w_in_l = w_in[layer].astype(BF16)
        wf = w_fourier[layer].astype(BF16)
        w_pool_bd = jax.scipy.linalg.block_diag(*[w_pool[layer, g] for g in range(w_pool.shape[1])]).astype(BF16)
        w_sgu_stack = w_sgu[layer].reshape(n_sgu * SGU_CHUNK, SGU_CHUNK).astype(BF16)
        b_sgu_exp = jnp.repeat(b_sgu[layer].T, GROUP_W // n_sgu, axis=1)
        w_router = jnp.concatenate([w_router_expert[layer].reshape(d, N_EXPERTS), w_router_group[layer]], axis=1)
        w_router = jnp.pad(w_router, ((0, 0), (0, LANES - w_router.shape[1])))
        wr_hi = w_router.astype(BF16)
        w_route = jnp.concatenate([wr_hi, (w_router - wr_hi.astype(F32)).astype(BF16)], axis=1)
        merge_consts = (mod, w_pool_bd, pool_scale[layer][None, :], w_sgu_stack, b_sgu_exp,
                        w_out[layer].astype(BF16), ln1_g[layer][None, :], ln1_b[layer][None, :], w_route)
        sink = attn_sink[layer]

        a, q, qs, k, v, p, ug = _proj(x2, mod, w_in_l, cos_t, sin_t, mod_row=None, rows_per_batch=s,
                                      rope=True, tm=tm, a_pitch=FFT_PITCH)
        if last:
            kc, vc = _proj_kv(c2, mod, w_in_l[:, 2 * GROUP_W:2 * GROUP_W + 2 * KV_W], mod_row=b)
        else:
            ac, qc, qsc, kc, vc, pc, ugc = _proj(c2, mod, w_in_l, cos_t, sin_t, mod_row=b, rows_per_batch=n_ctx,
                                                 rope=False, tm=n_ctx, a_pitch=FFT_R)
        y_four = _fourier(a, wf, s)
        y_attn = _attention(sink, q, qs, k, v, kc, vc, seq=s, n_ctx=n_ctx, band=True)
        total = b * s + (0 if last else b * n_ctx)
        merged = _merge(x2, p, ug, y_four, y_attn, *merge_consts, (),
                        mod_row=None, seq=s, tm=tm, row_off=0, total_rows=total)
        if not last:
            yc_four = _fourier_small(ac, wf, n_ctx)
            yc_attn = _attention(sink, qc, qsc, kc, vc, kc, vc, seq=n_ctx, n_ctx=n_ctx, band=False)
            merged = _merge(c2, pc, ugc, yc_four, yc_attn, *merge_consts, tuple(merged),
                            mod_row=b, seq=n_ctx, tm=n_ctx, row_off=b * s, total_rows=total)
        x1, h2_tiles, route, route_t = merged
        dest, ys = _moe(route_t, h2_tiles, w_exp_gate, w_exp_up, w_exp_down, layer)
        ln_g, ln_b = ln2_g[layer][None, :], ln2_b[layer][None, :]
        x2 = _combine(dest, x1, route, mod, ln_g, ln_b, ys, row_off=0, rows=b * s,
                      mod_row=None, rows_per_batch=s)
        if not last:
            c2 = _combine(dest, x1, route, mod, ln_g, ln_b, ys, row_off=b * s,
                          rows=b * n_ctx, mod_row=b, rows_per_batch=n_ctx)
    return x2.reshape(b, s, d)
```

```python
import functools
import math

import numpy as np
import jax
import jax.numpy as jnp
from jax import lax
from jax.experimental import pallas as pl
from jax.experimental.pallas import tpu as pltpu

GRID_W = 64
HEAD_DIM = 64
GROUP_W = 256
KV_W = 128
WINDOW = 128
POOL_WINDOWS = (2, 4, 8, 16)
SGU_CHUNK = 128
N_GROUPS = 4
EXPERTS_PER_GROUP = 8
N_EXPERTS = 32
ROPE_BASE = 10000.0
LN_EPS = 1e-6
NEG_INF = -1e30
DEPTH = 2
RES_ALPHA = (2 * DEPTH) ** 0.25

LANES = 128
SUBLANES = 8
VMEM_LIMIT = 48 * 1024 * 1024

ROW_TILE = 1024
Q_BLOCK = 2048
COMBINE_TILE = 512
WEIGHT_SLOTS = 3
MOE_BLOCK = 512
MOE_PART = 256
MOE_STEP_BLOCKS = 4
FFT_R = 64
FFT_PITCH = 72
FFT_UNROLL = 32
DMA_UNROLL = 8
PROJ_PART = 256
MERGE_PART = 512

BF16 = jnp.bfloat16
F32 = jnp.float32


def _cparams(*sem):
    return pltpu.CompilerParams(dimension_semantics=sem, vmem_limit_bytes=VMEM_LIMIT)


def _dot(a, b):
    return jnp.dot(a, b, preferred_element_type=F32)


def _dot_nt(a, b):
    return lax.dot_general(a, b, (((1,), (1,)), ((), ())), preferred_element_type=F32)


def _layer_norm(t):
    mu = jnp.mean(t, axis=-1, keepdims=True)
    d = t - mu
    var = jnp.mean(d * d, axis=-1, keepdims=True)
    return d * lax.rsqrt(var + LN_EPS)


def _silu(t):
    return t * (1.0 / (1.0 + jnp.exp(-t)))


def _gelu(t):
    return 0.5 * t * (1.0 + lax.erf(t * (1.0 / math.sqrt(2.0))))


ROW_WORDS = 4
HI_MASK = 0xFFFF0000


def _pack_rows(t):
    half = t.shape[1] // 2
    lo = lax.bitcast_convert_type(t[:, :half].astype(BF16).astype(F32), jnp.uint32)
    hi = lax.bitcast_convert_type(t[:, half:].astype(BF16).astype(F32), jnp.uint32)
    return (lo >> 16) | hi


def _unpack_rows(w):
    return (lax.bitcast_convert_type(w << 16, F32),
            lax.bitcast_convert_type(w & jnp.uint32(HI_MASK), F32))


def _store_rows(ref, t, first=0):
    w = _pack_rows(t)
    for j in range(ROW_WORDS):
        ref[pl.ds(first * ROW_WORDS + j, t.shape[0], stride=ROW_WORDS), :] = w[:, j * LANES:(j + 1) * LANES]


def _load_rows(ref, first, m):
    w = jnp.concatenate([ref[pl.ds(first * ROW_WORDS + j, m, stride=ROW_WORDS), :] for j in range(ROW_WORDS)],
                        axis=1)
    return _unpack_rows(w)


def _ada_kernel(c_ref, w_ref, b_ref, o_ref):
    s = _silu(c_ref[...]).astype(BF16)
    o_ref[...] = _dot(s, w_ref[...].astype(BF16)) + b_ref[...]


def _ada(cond, w_ada, b_ada):
    n_layers, d, n = w_ada.shape
    tn = n // 4
    return pl.pallas_call(
        _ada_kernel,
        grid=(n_layers, n // tn),
        in_specs=[
            pl.BlockSpec((SUBLANES, d), lambda l, j: (0, 0)),
            pl.BlockSpec((None, d, tn), lambda l, j: (l, 0, j)),
            pl.BlockSpec((None, 1, tn), lambda l, j: (l, 0, j)),
        ],
        out_specs=pl.BlockSpec((None, SUBLANES, tn), lambda l, j: (l, 0, j)),
        out_shape=jax.ShapeDtypeStruct((n_layers, SUBLANES, n), F32),
        compiler_params=_cparams("arbitrary", "arbitrary"),
        name="ada",
    )(cond, w_ada, b_ada)


def _rope(t, cos_t, sin_t):
    lane = lax.broadcasted_iota(jnp.int32, t.shape, 1)
    first = (lane % 32) < 16
    partner = jnp.where(first, pltpu.roll(t, LANES - 16, axis=1), pltpu.roll(t, 16, axis=1))
    return t * cos_t + partner * sin_t


def _proj_body(x_ref, m, w_ref, cos_ref, sin_ref, outs, *, rope, a_pitch, row0, pm):
    a_ref, q_ref, qs_ref, k_ref, v_ref, p_ref, ug_ref = outs
    rows = slice(row0, row0 + pm)
    x = x_ref[rows, :]
    d = x.shape[1]
    shift, scale = m[:, 0:d], m[:, d:2 * d]
    h = _layer_norm(x) * (1.0 + scale) + shift
    z = _dot(h.astype(BF16), w_ref[...])
    pad = jnp.zeros((a_pitch - FFT_R, LANES), F32)
    for g in range(pm // FFT_R):
        gg = row0 // FFT_R + g
        for hf in range(2):
            grp = z[g * FFT_R:(g + 1) * FFT_R, hf * LANES:(hf + 1) * LANES]
            if a_pitch > FFT_R:
                grp = jnp.concatenate([grp, pad], axis=0)
            a_ref[hf, gg * a_pitch:(gg + 1) * a_pitch, :] = grp
    q0, q1 = z[:, 256:384], z[:, 384:512]
    k = z[:, 512:640]
    if rope:
        cos_t, sin_t = cos_ref[rows, :], sin_ref[rows, :]
        q0, q1, k = _rope(q0, cos_t, sin_t), _rope(q1, cos_t, sin_t), _rope(k, cos_t, sin_t)
    q_ref[rows, 0:128] = q0.astype(BF16)
    q_ref[rows, 128:256] = q1.astype(BF16)
    qs_ref[rows, 0:128] = pltpu.roll(q0, HEAD_DIM, axis=1).astype(BF16)
    qs_ref[rows, 128:256] = pltpu.roll(q1, HEAD_DIM, axis=1).astype(BF16)
    k_ref[rows, :] = k.astype(BF16)
    v_ref[rows, :] = z[:, 640:768].astype(BF16)
    p_ref[rows, :] = z[:, 768:1024].astype(BF16)
    ug_ref[rows, :] = z[:, 1024:1536].astype(BF16)


def _mod_row(mod_ref, mod_row, tm, rows_per_batch):
    row = (pl.program_id(0) * tm) // rows_per_batch if mod_row is None else mod_row
    return mod_ref[pl.ds(row, 1), :]


def _proj_kernel(x_ref, mod_ref, w_ref, cos_ref, sin_ref, *outs, mod_row, rows_per_batch, rope, a_pitch):
    tm = x_ref.shape[0]
    m = _mod_row(mod_ref, mod_row, tm, rows_per_batch)
    pm = min(tm, PROJ_PART)
    for part in range(tm // pm):
        _proj_body(x_ref, m, w_ref, cos_ref, sin_ref, outs, rope=rope, a_pitch=a_pitch, row0=part * pm, pm=pm)


def _proj_kv_kernel(x_ref, mod_ref, w_ref, k_ref, v_ref, *, mod_row):
    d = x_ref.shape[1]
    m = mod_ref[pl.ds(mod_row, 1), :]
    h = _layer_norm(x_ref[...]) * (1.0 + m[:, d:2 * d]) + m[:, 0:d]
    z = _dot(h.astype(BF16), w_ref[...])
    k_ref[...] = z[:, 0:KV_W].astype(BF16)
    v_ref[...] = z[:, KV_W:].astype(BF16)


def _proj_kv(x2, mod, w_kv, *, mod_row):
    rows, d = x2.shape
    full = lambda a: pl.BlockSpec(a.shape, lambda i: (0,) * a.ndim)
    out = pl.BlockSpec((rows, KV_W), lambda i: (0, 0))
    return pl.pallas_call(
        functools.partial(_proj_kv_kernel, mod_row=mod_row),
        grid=(1,),
        in_specs=[full(x2), full(mod), full(w_kv)],
        out_specs=[out, out],
        out_shape=[jax.ShapeDtypeStruct((rows, KV_W), BF16)] * 2,
        compiler_params=_cparams("arbitrary"),
        name="proj_kv",
    )(x2, mod, w_kv)


def _proj_specs(rows, tm, a_pitch, seq_steps):
    row_spec = lambda w: pl.BlockSpec((tm, w), lambda i: (i, 0))
    out_w = (256, 256, 128, 128, 256, 512)
    ta = tm // FFT_R * a_pitch
    out_specs = [pl.BlockSpec((2, ta, LANES), lambda i: (0, i, 0))] + [row_spec(w) for w in out_w]
    out_shape = ([jax.ShapeDtypeStruct((2, rows // FFT_R * a_pitch, LANES), F32)]
                 + [jax.ShapeDtypeStruct((rows, w), BF16) for w in out_w])
    table_spec = pl.BlockSpec((tm, LANES), lambda i: (i % seq_steps, 0))
    return table_spec, out_specs, out_shape


def _proj(x2, mod, w_in, cos_t, sin_t, *, mod_row, rows_per_batch, rope, tm, a_pitch):
    rows, d = x2.shape
    kern = functools.partial(_proj_kernel, mod_row=mod_row, rows_per_batch=rows_per_batch,
                             rope=rope, a_pitch=a_pitch)
    table_spec, out_specs, out_shape = _proj_specs(rows, tm, a_pitch, cos_t.shape[0] // tm)
    return pl.pallas_call(
        kern,
        grid=(rows // tm,),
        in_specs=[
            pl.BlockSpec((tm, d), lambda i: (i, 0)),
            pl.BlockSpec(mod.shape, lambda i: (0, 0)),
            pl.BlockSpec(w_in.shape, lambda i: (0, 0)),
            table_spec, table_spec,
        ],
        out_specs=out_specs,
        out_shape=out_shape,
        compiler_params=_cparams("arbitrary"),
        name="proj",
    )(x2, mod, w_in, cos_t, sin_t)


def _fft_tables(n_pos):
    r = FFT_R
    assert n_pos == r * r
    kb = np.arange(r)[None, :, None]
    na = np.arange(r)[:, None, None]
    nb = np.arange(r)[None, None, :]
    ang = 2.0 * np.pi * ((kb * (na + r * nb)) % n_pos) / n_pos
    m1 = np.concatenate([np.cos(ang), -np.sin(ang)], axis=1)
    ka = np.arange(r)[:, None]
    n2 = np.arange(r)[None, :]
    ang2 = 2.0 * np.pi * ((ka * n2) % r) / r
    c2, s2 = np.cos(ang2), np.sin(ang2)
    w2 = np.block([[c2, s2], [-s2, c2]])
    return m1, w2


def _channel_tables(n_pos):
    h = HEAD_DIM
    c = np.arange(h)
    ang = 2.0 * np.pi * ((c[:, None] * c[None, :]) % h) / h
    scale = 1.0 / math.sqrt(n_pos * h)
    eye = np.eye(GROUP_W // h)
    cc = np.kron(eye, np.cos(ang)) * scale
    ss = np.kron(eye, np.sin(ang)) * scale
    return np.concatenate([cc, ss], axis=0)


def _fourier_kernel(a_ref, m1_ref, w2_ref, ch_ref, wf_ref, o_ref, z_ref, y_ref):
    r, pt = FFT_R, FFT_PITCH

    def step1(i, c):
        for u in range(FFT_UNROLL):
            na = i * FFT_UNROLL + u
            rows = jnp.concatenate([a_ref[0, pl.ds(na, r, stride=pt), :],
                                    a_ref[1, pl.ds(na, r, stride=pt), :]], axis=1)
            z = _dot(m1_ref[na], rows.astype(BF16))
            base = pl.multiple_of(na * pt, SUBLANES)
            z_ref[0, pl.ds(base, r), :] = z[0:r, 0:LANES]
            z_ref[1, pl.ds(base, r), :] = z[0:r, LANES:]
            z_ref[2, pl.ds(base, r), :] = z[r:, 0:LANES]
            z_ref[3, pl.ds(base, r), :] = z[r:, LANES:]
        return c

    lax.fori_loop(0, r // FFT_UNROLL, step1, 0)

    def step2(i, c):
        for u in range(FFT_UNROLL):
            kb = i * FFT_UNROLL + u
            q = [z_ref[j, pl.ds(kb, r, stride=pt), :] for j in range(4)]
            zs = jnp.concatenate([jnp.concatenate(q[0:2], axis=1),
                                  jnp.concatenate(q[2:4], axis=1)], axis=0)
            y = _dot(w2_ref[...], zs.astype(BF16))
            base = pl.multiple_of(kb * r, r)
            y_ref[0, pl.ds(base, r), :] = y[0:r, 0:LANES]
            y_ref[1, pl.ds(base, r), :] = y[0:r, LANES:]
            y_ref[2, pl.ds(base, r), :] = y[r:, 0:LANES]
            y_ref[3, pl.ds(base, r), :] = y[r:, LANES:]
        return c

    lax.fori_loop(0, r // FFT_UNROLL, step2, 0)

    chunk = 8 * r
    for cidx in range(r * r // chunk):
        yy = jnp.concatenate([y_ref[j, cidx * chunk:(cidx + 1) * chunk, :] for j in range(4)], axis=1)
        f = _dot(yy.astype(BF16), ch_ref[...])
        g = _dot(f.astype(BF16), wf_ref[...])
        for gi in range(chunk // r):
            kb = cidx * (chunk // r) + gi
            z_ref[0, kb * pt:kb * pt + r, :] = g[gi * r:(gi + 1) * r, 0:LANES]
            z_ref[1, kb * pt:kb * pt + r, :] = g[gi * r:(gi + 1) * r, LANES:]

    def step3(i, c):
        for u in range(FFT_UNROLL):
            ka = i * FFT_UNROLL + u
            base = pl.multiple_of(ka * r, r)
            o_ref[pl.ds(base, r), 0:LANES] = z_ref[0, pl.ds(ka, r, stride=pt), :]
            o_ref[pl.ds(base, r), LANES:] = z_ref[1, pl.ds(ka, r, stride=pt), :]
        return c

    lax.fori_loop(0, r // FFT_UNROLL, step3, 0)


def _fourier(a3, w_fourier, n_pos):
    rows = a3.shape[1] // FFT_PITCH * FFT_R
    gw = GROUP_W
    m1, w2 = _fft_tables(n_pos)
    ch = _channel_tables(n_pos)
    const = lambda shape: pl.BlockSpec(shape, lambda b: (0,) * len(shape))
    return pl.pallas_call(
        _fourier_kernel,
        grid=(rows // n_pos,),
        in_specs=[
            pl.BlockSpec((2, FFT_R * FFT_PITCH, LANES), lambda b: (0, b, 0)),
            const(m1.shape), const(w2.shape), const(ch.shape), const(w_fourier.shape),
        ],
        out_specs=pl.BlockSpec((n_pos, gw), lambda b: (b, 0)),
        out_shape=jax.ShapeDtypeStruct((rows, gw), F32),
        scratch_shapes=[pltpu.VMEM((4, FFT_R * FFT_PITCH, LANES), F32), pltpu.VMEM((4, n_pos, LANES), F32)],
        compiler_params=_cparams("arbitrary"),
        name="fourier",
    )(a3, jnp.asarray(m1, BF16), jnp.asarray(w2, BF16), jnp.asarray(ch, BF16), w_fourier)


def _fourier_small_kernel(a_ref, cs_ref, ch_ref, wf_ref, o_ref):
    n = a_ref.shape[1]
    a = jnp.concatenate([a_ref[0], a_ref[1]], axis=1)
    pq = _dot(cs_ref[...], a.astype(BF16))
    y = jnp.concatenate([pq[0:n], pq[n:2 * n]], axis=1).astype(BF16)
    f = _dot(y, ch_ref[...])
    o_ref[...] = _dot(f.astype(BF16), wf_ref[...])


def _fourier_small(a3, w_fourier, n_pos):
    _, rows, _ = a3.shape
    gw = GROUP_W
    k = np.arange(n_pos)
    ang = 2.0 * np.pi * ((k[:, None] * k[None, :]) % n_pos) / n_pos
    cs = np.concatenate([np.cos(ang), -np.sin(ang)], axis=0)
    ch = _channel_tables(n_pos)
    const = lambda shape: pl.BlockSpec(shape, lambda b: (0,) * len(shape))
    return pl.pallas_call(
        _fourier_small_kernel,
        grid=(rows // n_pos,),
        in_specs=[pl.BlockSpec((2, n_pos, LANES), lambda b: (0, b, 0)),
                  const(cs.shape), const(ch.shape), const(w_fourier.shape)],
        out_specs=pl.BlockSpec((n_pos, gw), lambda b: (b, 0)),
        out_shape=jax.ShapeDtypeStruct((rows, gw), F32),
        compiler_params=_cparams("arbitrary"),
        name="fourier_ctx",
    )(a3, jnp.asarray(cs, BF16), jnp.asarray(ch, BF16), w_fourier)


ATTN_SUB = 128


def _attn_kernel(sink_ref, q_ref, qs_ref, k_ref, v_ref, kc_ref, vc_ref, o_ref, *, band, seq):
    qb = q_ref.shape[0]
    sub = ATTN_SUB
    lane = lax.broadcasted_iota(jnp.int32, (1, LANES), 1)
    lo_half = lane < HEAD_DIM
    zero = jnp.zeros((), BF16)
    scale = jnp.asarray(HEAD_DIM ** -0.5, BF16)
    kw = sub + 2 * WINDOW
    for sb in range(qb // sub):
        rows = slice(sb * sub, (sb + 1) * sub)
        qa0, qa1 = q_ref[rows, 0:LANES], q_ref[rows, LANES:]
        qs0, qs1 = qs_ref[rows, 0:LANES], qs_ref[rows, LANES:]
        q_all = jnp.concatenate([jnp.where(lo_half, qa0, zero), jnp.where(lo_half, qs0, zero),
                                 jnp.where(lo_half, zero, qs1), jnp.where(lo_half, zero, qa1)], axis=0) * scale
        if band:
            p0 = pl.program_id(1) * qb + sb * sub
            start = pl.multiple_of(jnp.clip(p0 - WINDOW, 0, seq - kw), WINDOW)
            qpos = p0 + lax.broadcasted_iota(jnp.int32, (sub, 1), 0)
            kpos = start + lax.broadcasted_iota(jnp.int32, (1, kw), 1)
            bias = jnp.where(jnp.abs(qpos - kpos) <= WINDOW, 0.0, NEG_INF)
            keys = jnp.concatenate([k_ref[pl.ds(start, kw), :], kc_ref[...]], axis=0)
            vals = jnp.concatenate([v_ref[pl.ds(start, kw), :], vc_ref[...]], axis=0)
        else:
            keys, vals = kc_ref[...], vc_ref[...]
        s_all = _dot_nt(q_all, keys)
        probs, dens = [], []
        for h in range(4):
            s = s_all[h * sub:(h + 1) * sub, :]
            sink = sink_ref[h]
            if band:
                s = jnp.concatenate([s[:, 0:kw] + bias, s[:, kw:]], axis=1)
            m = jnp.maximum(jnp.max(s, axis=1, keepdims=True), sink)
            p = jnp.exp(s - m)
            dens.append(jnp.sum(p, axis=1, keepdims=True) + jnp.exp(sink - m))
            probs.append(p.astype(BF16))
        o_all = _dot(jnp.concatenate(probs, axis=0), vals)
        o = [o_all[h * sub:(h + 1) * sub, :] / dens[h] for h in range(4)]
        o_ref[rows, 0:LANES] = jnp.where(lo_half, o[0], pltpu.roll(o[1], HEAD_DIM, axis=1)).astype(BF16)
        o_ref[rows, LANES:] = jnp.where(lo_half, pltpu.roll(o[2], HEAD_DIM, axis=1), o[3]).astype(BF16)


def _attention(sink, q, qs, k, v, kc, vc, *, seq, n_ctx, band):
    rows = q.shape[0]
    n_batch = rows // seq
    qb = Q_BLOCK if band else seq
    steps = seq // qb
    kern = functools.partial(_attn_kernel, band=band, seq=seq)
    seq_spec = pl.BlockSpec((seq, KV_W), lambda b, i: (b, 0))
    ctx_spec = pl.BlockSpec((n_ctx, KV_W), lambda b, i: (b, 0))
    q_spec = pl.BlockSpec((qb, GROUP_W), lambda b, i: (b * steps + i, 0))
    return pl.pallas_call(
        kern,
        grid=(n_batch, steps),
        in_specs=[pl.BlockSpec(memory_space=pltpu.SMEM), q_spec, q_spec,
                  seq_spec, seq_spec, ctx_spec, ctx_spec],
        out_specs=q_spec,
        out_shape=jax.ShapeDtypeStruct((rows, GROUP_W), BF16),
        compiler_params=_cparams("arbitrary", "arbitrary"),
        name="attn" if band else "attn_ctx",
    )(sink, q, qs, k, v, kc, vc)


POOL_HALO = max(POOL_WINDOWS) // 2


def _pool(p_ref, t0, tm, seq):
    halo = POOL_HALO
    pack = 2 * SUBLANES
    t0 = pl.multiple_of(t0, pack)
    main = p_ref[pl.ds(t0, tm), :].astype(F32)
    lo = pl.multiple_of(jnp.maximum(t0 - pack, 0), pack)
    hi = pl.multiple_of(jnp.minimum(t0 + tm, seq - pack), pack)
    prev = p_ref[pl.ds(lo, pack), :].astype(F32)[pack - halo:, :]
    nxt = p_ref[pl.ds(hi, pack), :].astype(F32)[:halo, :]
    prev = jnp.where(t0 > 0, prev, 0.0)
    nxt = jnp.where(t0 + tm < seq, nxt, 0.0)
    full = jnp.concatenate([prev, main, nxt], axis=0)
    n = tm + 2 * halo
    gch = GROUP_W // len(POOL_WINDOWS)
    first = lax.broadcasted_iota(jnp.int32, (1, LANES), 1) < gch
    means = []
    for hf in range(GROUP_W // LANES):
        wa, wb = POOL_WINDOWS[2 * hf], POOL_WINDOWS[2 * hf + 1]
        x = full[:, hf * LANES:(hf + 1) * LANES]
        sums, w, s = {}, 2, pltpu.roll(x, 1, axis=0) + x
        sums[w] = s
        while w < wb:
            s = pltpu.roll(s, w // 2, axis=0) + pltpu.roll(s, n - w // 2, axis=0)
            w *= 2
            sums[w] = s
        means.append(jnp.where(first, sums[wa] * (1.0 / wa), sums[wb] * (1.0 / wb))[halo:halo + tm, :])
    mean = jnp.concatenate(means, axis=1)
    win = jnp.concatenate([jnp.full((1, gch), w, jnp.int32) for w in POOL_WINDOWS], axis=1)

    def rescale(rows, first_pos):
        pos = first_pos + lax.broadcasted_iota(jnp.int32, (halo, 1), 0)
        cnt = jnp.minimum(pos + win // 2, seq) - jnp.maximum(pos - win // 2, 0)
        return rows * (win.astype(F32) / cnt.astype(F32))

    mean = jnp.concatenate([rescale(mean[:halo], t0), mean[halo:tm - halo],
                            rescale(mean[tm - halo:], t0 + tm - halo)], axis=0)
    return mean - main


def _route(logits):
    tm = logits.shape[0]
    lt = logits.T
    gl = lt[N_EXPERTS:N_EXPERTS + N_GROUPS]
    sub_g = lax.broadcasted_iota(jnp.int32, gl.shape, 0)
    gmax = jnp.max(gl, axis=0, keepdims=True)
    grp = jnp.min(jnp.where(gl == gmax, sub_g, N_GROUPS), axis=0, keepdims=True)
    gate_group = 1.0 / jnp.sum(jnp.exp(gl - gmax), axis=0, keepdims=True)
    el = lt[0:EXPERTS_PER_GROUP]
    for g in range(1, N_GROUPS):
        el = jnp.where(grp == g, lt[g * EXPERTS_PER_GROUP:(g + 1) * EXPERTS_PER_GROUP], el)
    sub = lax.broadcasted_iota(jnp.int32, el.shape, 0)
    m1 = jnp.max(el, axis=0, keepdims=True)
    i1 = jnp.min(jnp.where(el == m1, sub, EXPERTS_PER_GROUP), axis=0, keepdims=True)
    el2 = jnp.where(sub == i1, -jnp.inf, el)
    m2 = jnp.max(el2, axis=0, keepdims=True)
    i2 = jnp.min(jnp.where(el2 == m2, sub, EXPERTS_PER_GROUP), axis=0, keepdims=True)
    r = jnp.exp(m2 - m1)
    g1 = gate_group / (1.0 + r)
    g2 = g1 * r
    e1 = (grp * EXPERTS_PER_GROUP + i1).astype(F32)
    e2 = (grp * EXPERTS_PER_GROUP + i2).astype(F32)
    rows = jnp.where(sub == 0, e1, jnp.where(sub == 1, e2, jnp.where(sub == 2, g1, jnp.where(sub == 3, g2, 0.0))))
    cols = jnp.concatenate([rows, jnp.zeros((LANES - rows.shape[0], tm), F32)], axis=0).T
    return cols, rows


def _merge_kernel(x_ref, p_ref, ug_ref, yf_ref, ya_ref, mod_ref, wpool_ref, pscale_ref, wsgu_ref, bsgu_ref,
                  wout_ref, lng_ref, lnb_ref, wr_ref, *rest,
                  mod_row, seq, n_alias):
    x1_ref, h2_ref, route_ref, route_t_ref = rest[n_alias:]
    tm, d = x_ref.shape
    if mod_row is None:
        row = pl.program_id(0)
    else:
        row = mod_row
    t0 = pl.multiple_of(pl.program_id(1) * tm, tm)
    m = mod_ref[pl.ds(row, 1), :]
    gate1, shift2, scale2 = m[:, 2 * d:3 * d], m[:, 3 * d:4 * d], m[:, 4 * d:5 * d]
    lane = lax.broadcasted_iota(jnp.int32, (1, GROUP_W), 1)
    n_heads = wsgu_ref.shape[0] // SGU_CHUNK
    head = lane // (GROUP_W // n_heads)

    pm = min(tm, MERGE_PART)
    for part in range(tm // pm):
        r0 = part * pm
        rows = slice(r0, r0 + pm)
        pooled = _pool(p_ref, t0 + r0, pm, seq)
        y_pool = _dot(pooled.astype(BF16), wpool_ref[...]) * pscale_ref[...]

        ug = ug_ref[rows, :].astype(F32)
        u = _gelu(ug[:, 0:GROUP_W])
        v = _layer_norm(_gelu(ug[:, GROUP_W:])).astype(BF16)
        mixed = []
        for cidx in range(pm // SGU_CHUNK):
            vc = v[cidx * SGU_CHUNK:(cidx + 1) * SGU_CHUNK, :]
            full = _dot(wsgu_ref[...], vc)
            mc = bsgu_ref[...]
            for hd in range(n_heads):
                mc = mc + jnp.where(head == hd, full[hd * SGU_CHUNK:(hd + 1) * SGU_CHUNK, :], 0.0)
            mixed.append(mc)
        y_sgu = u * jnp.concatenate(mixed, axis=0)

        cat = jnp.concatenate([yf_ref[rows, :].astype(BF16), ya_ref[rows, :], y_pool.astype(BF16),
                               y_sgu.astype(BF16)], axis=1)
        y = _dot(cat, wout_ref[...])
        x1 = _layer_norm(RES_ALPHA * x_ref[rows, :] + gate1 * y) * lng_ref[...] + lnb_ref[...]
        x1_ref[rows, :] = x1
        h2 = _layer_norm(x1) * (1.0 + scale2) + shift2
        _store_rows(h2_ref, h2, first=r0)
        lg = _dot(h2.astype(BF16), wr_ref[...])
        route_ref[rows, :], route_t_ref[:, rows] = _route(lg[:, 0:LANES] + lg[:, LANES:])


def _merge(x2, p, ug, y_four, y_attn, mod, w_pool_bd, pool_scale, w_sgu_stack, b_sgu_exp, w_out,
           ln_g, ln_b, w_route, aliased, *, mod_row, seq, tm, row_off, total_rows):
    rows, d = x2.shape
    n_batch, steps = rows // seq, seq // tm
    off = row_off // tm
    kern = functools.partial(_merge_kernel, mod_row=mod_row, seq=seq, n_alias=len(aliased))
    row_spec = lambda w: pl.BlockSpec((tm, w), lambda b, i: (b * steps + i, 0))
    const = lambda a: pl.BlockSpec(a.shape, lambda b, i: (0,) * a.ndim)
    consts = (mod, w_pool_bd, pool_scale, w_sgu_stack, b_sgu_exp, w_out, ln_g, ln_b, w_route)
    n_in = 5 + len(consts)
    out_shapes = [jax.ShapeDtypeStruct((total_rows, d), F32),
                  jax.ShapeDtypeStruct((total_rows * ROW_WORDS, LANES), jnp.uint32),
                  jax.ShapeDtypeStruct((total_rows, LANES), F32),
                  jax.ShapeDtypeStruct((SUBLANES, total_rows), F32)]
    out_specs = [pl.BlockSpec((tm, d), lambda b, i: (off + b * steps + i, 0)),
                 pl.BlockSpec((tm * ROW_WORDS, LANES), lambda b, i: (off + b * steps + i, 0)),
                 pl.BlockSpec((tm, LANES), lambda b, i: (off + b * steps + i, 0)),
                 pl.BlockSpec((SUBLANES, tm), lambda b, i: (0, off + b * steps + i))]
    return pl.pallas_call(
        kern,
        grid=(n_batch, steps),
        in_specs=[row_spec(d), pl.BlockSpec((seq, GROUP_W), lambda b, i: (b, 0)),
                  row_spec(2 * GROUP_W), row_spec(GROUP_W), row_spec(GROUP_W)]
                 + [const(a) for a in consts]
                 + [pl.BlockSpec(memory_space=pl.ANY)] * len(aliased),
        out_specs=out_specs,
        out_shape=out_shapes,
        input_output_aliases={n_in + k: k for k in range(len(aliased))},
        compiler_params=_cparams("arbitrary", "arbitrary"),
        name="merge",
    )(x2, p, ug, y_four, y_attn, *consts, *aliased)


def _plan_kernel(route_ref, dest_ref, cnt_out_ref, cnt_ref, start_ref, carry_ref):
    ph, t = pl.program_id(0), pl.program_id(1)
    tm = route_ref.shape[1]
    rt = route_ref[...]
    e1 = rt[0:1, :].astype(jnp.int32)
    e2 = rt[1:2, :].astype(jnp.int32)
    sub = lax.broadcasted_iota(jnp.int32, (N_EXPERTS, tm), 0)
    hit1, hit2 = sub == e1, sub == e2
    onehot = jnp.where(hit1 | hit2, 1.0, 0.0)
    tile_cnt = jnp.sum(onehot, axis=1, keepdims=True)

    @pl.when((ph == 0) & (t == 0))
    def _():
        cnt_ref[...] = jnp.zeros_like(cnt_ref)

    @pl.when(ph == 0)
    def _():
        cnt_ref[...] += tile_cnt

    @pl.when((ph == 1) & (t == 0))
    def _():
        cnt = cnt_ref[...]
        padded = jnp.floor((cnt + (MOE_BLOCK - 1.0)) * (1.0 / MOE_BLOCK)) * MOE_BLOCK
        row = lax.broadcasted_iota(jnp.int32, cnt.shape, 0)
        incl = padded
        sh = 1
        while sh < N_EXPERTS:
            incl = incl + jnp.where(row >= sh, pltpu.roll(incl, sh, axis=0), 0.0)
            sh *= 2
        start_ref[...] = incl - padded
        carry_ref[...] = jnp.zeros_like(carry_ref)
        cnt_out_ref[...] = cnt

    @pl.when(ph == 1)
    def _():
        r_i = lax.broadcasted_iota(jnp.int32, (tm, tm), 0)
        c_i = lax.broadcasted_iota(jnp.int32, (tm, tm), 1)
        before = jnp.where(r_i < c_i, 1.0, 0.0).astype(BF16)
        rank = _dot(onehot.astype(BF16), before)
        base = start_ref[:, 0:1] + carry_ref[:, 0:1] + rank
        d1 = jnp.sum(jnp.where(hit1, base, 0.0), axis=0, keepdims=True)
        d2 = jnp.sum(jnp.where(hit2, base, 0.0), axis=0, keepdims=True)
        sub8 = lax.broadcasted_iota(jnp.int32, (SUBLANES, tm), 0)
        dest_ref[...] = jnp.where(sub8 == 0, d1, d2).astype(jnp.int32)
        carry_ref[...] += tile_cnt


def _plan(route_t, tm):
    rows = route_t.shape[1]
    n_t = rows // tm
    return pl.pallas_call(
        _plan_kernel,
        grid=(2, n_t),
        in_specs=[pl.BlockSpec((SUBLANES, tm), lambda ph, t: (0, t))],
        out_specs=[pl.BlockSpec((None, SUBLANES, tm), lambda ph, t: (t * ph, 0, 0)),
                   pl.BlockSpec((N_EXPERTS, LANES), lambda ph, t: (0, 0))],
        out_shape=[jax.ShapeDtypeStruct((n_t, SUBLANES, tm), jnp.int32),
                   jax.ShapeDtypeStruct((N_EXPERTS, LANES), F32)],
        scratch_shapes=[pltpu.VMEM((N_EXPERTS, LANES), F32)] * 3,
        compiler_params=_cparams("arbitrary", "arbitrary"),
        name="moe_plan",
    )(route_t)


def _row_copy(src_ref, src_row, dst_ref, dst_row, sem):
    return pltpu.make_async_copy(
        src_ref.at[pl.ds(pl.multiple_of(src_row * ROW_WORDS, ROW_WORDS), ROW_WORDS)],
        dst_ref.at[pl.ds(pl.multiple_of(dst_row * ROW_WORDS, ROW_WORDS), ROW_WORDS)], sem)


def _dispatch_kernel(dest_ref, h2_ref, xs_ref, sem, *, tm):
    def body(r4, c):
        for u in range(DMA_UNROLL):
            r = r4 * DMA_UNROLL + u
            for k in range(2):
                _row_copy(h2_ref, r, xs_ref, dest_ref[0, 0, k * tm + r], sem).start(priority=k)
        return c

    lax.fori_loop(0, tm // DMA_UNROLL, body, 0)
    for k in range(2):
        pltpu.make_async_copy(h2_ref, xs_ref.at[pl.ds(0, tm * ROW_WORDS)], sem).wait()


def _dispatch(dest, h2_tiles, n_slots, tm):
    n_t = dest.shape[0]
    return pl.pallas_call(
        functools.partial(_dispatch_kernel, tm=tm),
        grid=(n_t,),
        in_specs=[pl.BlockSpec((1, 1, 2 * tm), lambda i: (i, 0, 0), memory_space=pltpu.SMEM),
                  pl.BlockSpec((tm * ROW_WORDS, LANES), lambda i: (i, 0))],
        out_specs=pl.BlockSpec(memory_space=pl.ANY),
        out_shape=jax.ShapeDtypeStruct((n_slots * ROW_WORDS, LANES), jnp.uint32),
        scratch_shapes=[pltpu.SemaphoreType.DMA],
        compiler_params=_cparams("arbitrary"),
        name="moe_dispatch",
    )(dest, h2_tiles)


def _expert_kernel(be_ref, bn_ref, first_ref, slot_ref, nxt_ref, nxt2_ref, _blk_ref,
                   x_ref, wg_hbm, wu_hbm, wd_hbm, y_ref,
                   wg_buf, wu_buf, wd_buf, wg_bf, wu_bf, wd_bf, sem, *, layer):
    mb = MOE_BLOCK
    half = wg_bf.shape[0] // 2

    def weight_copies(e, s):
        return [pltpu.make_async_copy(wg_hbm.at[layer, e], wg_buf.at[s], sem.at[s]),
                pltpu.make_async_copy(wu_hbm.at[layer, e], wu_buf.at[s], sem.at[s]),
                pltpu.make_async_copy(wd_hbm.at[layer, e], wd_buf.at[s], sem.at[s])]

    @pl.when(pl.program_id(0) == 0)
    def _():
        for cp in weight_copies(be_ref[0], 0):
            cp.start()

        @pl.when(nxt_ref[0] >= 0)
        def _():
            for cp in weight_copies(nxt_ref[0], 1):
                cp.start(priority=1)

    def block(i, row0):
        @pl.when(first_ref[i] == 1)
        def _():
            s = slot_ref[i]
            for cp in weight_copies(be_ref[i], s):
                cp.wait()

            @pl.when(nxt2_ref[i] >= 0)
            def _():
                s2 = jnp.where(s == 0, WEIGHT_SLOTS - 1, s - 1)
                for cp in weight_copies(nxt2_ref[i], s2):
                    cp.start(priority=1)

            wg_bf[...] = wg_buf[s].astype(BF16)
            wu_bf[...] = wu_buf[s].astype(BF16)
            wd_bf[...] = wd_buf[s].astype(BF16)

        def compute(m):
            live = lax.broadcasted_iota(jnp.int32, (m, 1), 0) < bn_ref[i]
            x_lo, x_hi = _load_rows(x_ref, row0, m)
            x_lo = jnp.where(live, x_lo, 0.0).astype(BF16)
            x_hi = jnp.where(live, x_hi, 0.0).astype(BF16)
            g = _dot(x_lo, wg_bf[0:half, :]) + _dot(x_hi, wg_bf[half:, :])
            u = _dot(x_lo, wu_bf[0:half, :]) + _dot(x_hi, wu_bf[half:, :])
            hid = (_silu(g) * u).astype(BF16)
            _store_rows(y_ref, _dot(hid, wd_bf[...]), first=row0)

        for parts in range(1, mb // MOE_PART + 1):
            @pl.when((bn_ref[i] > (parts - 1) * MOE_PART) & (bn_ref[i] <= parts * MOE_PART))
            def _(parts=parts):
                compute(parts * MOE_PART)

    for sb in range(MOE_STEP_BLOCKS):
        block(pl.program_id(0) * MOE_STEP_BLOCKS + sb, sb * mb)


def _experts(table, xs, w_gate, w_up, w_down, layer):
    n_blocks = table[0].shape[0]
    _, _, d, de = w_gate.shape
    shape = (MOE_STEP_BLOCKS * MOE_BLOCK * ROW_WORDS, LANES)
    hbm = pl.BlockSpec(memory_space=pl.ANY)
    return pl.pallas_call(
        functools.partial(_expert_kernel, layer=layer),
        grid_spec=pltpu.PrefetchScalarGridSpec(
            num_scalar_prefetch=len(table),
            grid=(n_blocks // MOE_STEP_BLOCKS,),
            in_specs=[pl.BlockSpec(shape, lambda i, *t: (t[-1][i], 0)), hbm, hbm, hbm],
            out_specs=pl.BlockSpec(shape, lambda i, *t: (t[-1][i], 0)),
            scratch_shapes=[pltpu.VMEM((WEIGHT_SLOTS, d, de), F32), pltpu.VMEM((WEIGHT_SLOTS, d, de), F32),
                            pltpu.VMEM((WEIGHT_SLOTS, de, d), F32),
                            pltpu.VMEM((d, de), BF16), pltpu.VMEM((d, de), BF16), pltpu.VMEM((de, d), BF16),
                            pltpu.SemaphoreType.DMA((WEIGHT_SLOTS,))]),
        out_shape=jax.ShapeDtypeStruct(xs.shape, jnp.uint32),
        compiler_params=_cparams("arbitrary"),
        name="moe_experts",
    )(*table, xs, w_gate, w_up, w_down)


def _combine_kernel(dest_ref, dest_next_ref, x1_ref, route_ref, mod_ref, lng_ref, lnb_ref, y_ref, o_ref,
                    buf_ref, sem, *, mod_row, rows_per_batch):
    tm, d = x1_ref.shape
    i = pl.program_id(0)
    n = pl.num_programs(0)
    slot = i % 2
    m = _mod_row(mod_ref, mod_row, tm, rows_per_batch)

    def gather(idx_ref, s):
        def body(r4, c):
            for u in range(DMA_UNROLL):
                r = r4 * DMA_UNROLL + u
                for k in range(2):
                    _row_copy(y_ref, idx_ref[0, 0, k * tm + r], buf_ref.at[s], k * tm + r,
                              sem.at[s]).start(priority=k)
            return c

        lax.fori_loop(0, tm // DMA_UNROLL, body, 0)

    @pl.when(i == 0)
    def _():
        gather(dest_ref, 0)

    @pl.when(i + 1 < n)
    def _():
        gather(dest_next_ref, 1 - slot)

    for k in range(2):
        pltpu.make_async_copy(y_ref.at[pl.ds(0, tm * ROW_WORDS)],
                              buf_ref.at[slot, pl.ds(0, tm * ROW_WORDS)], sem.at[slot]).wait()

    gate2 = m[:, 5 * d:6 * d]
    rt = route_ref[...]
    f = jnp.zeros((tm, d), F32)
    for k in range(2):
        lo, hi = _load_rows(buf_ref.at[slot], k * tm, tm)
        f = f + jnp.concatenate([lo, hi], axis=1) * rt[:, 2 + k:3 + k]
    o_ref[...] = _layer_norm(RES_ALPHA * x1_ref[...] + gate2 * f) * lng_ref[...] + lnb_ref[...]


def _combine(dest, x1, route, mod, ln_g, ln_b, y_tiles, *, row_off, rows, mod_row, rows_per_batch):
    d = x1.shape[1]
    tm = COMBINE_TILE
    steps = rows // tm
    off = row_off // tm
    kern = functools.partial(_combine_kernel, mod_row=mod_row, rows_per_batch=rows_per_batch)
    const = lambda a: pl.BlockSpec(a.shape, lambda i: (0,) * a.ndim)
    return pl.pallas_call(
        kern,
        grid=(steps,),
        in_specs=[pl.BlockSpec((1, 1, 2 * tm), lambda i: (off + i, 0, 0), memory_space=pltpu.SMEM),
                  pl.BlockSpec((1, 1, 2 * tm), lambda i: (off + jnp.minimum(i + 1, steps - 1), 0, 0),
                               memory_space=pltpu.SMEM),
                  pl.BlockSpec((tm, d), lambda i: (off + i, 0)),
                  pl.BlockSpec((tm, LANES), lambda i: (off + i, 0)),
                  const(mod), const(ln_g), const(ln_b), pl.BlockSpec(memory_space=pl.ANY)],
        out_specs=pl.BlockSpec((tm, d), lambda i: (i, 0)),
        out_shape=jax.ShapeDtypeStruct((rows, d), F32),
        scratch_shapes=[pltpu.VMEM((2, 2 * tm * ROW_WORDS, LANES), jnp.uint32), pltpu.SemaphoreType.DMA((2,))],
        compiler_params=_cparams("arbitrary"),
        name="moe_combine",
    )(dest, dest, x1, route, mod, ln_g, ln_b, y_tiles)


def _rope_tables(n_pos):
    rows = n_pos // GRID_W
    row = jnp.repeat(jnp.arange(rows), GRID_W).astype(F32)
    col = jnp.tile(jnp.arange(GRID_W), rows).astype(F32)
    n_freq = HEAD_DIM // 4
    freq = ROPE_BASE ** (-jnp.arange(n_freq, dtype=F32) / n_freq)
    ang_r, ang_c = row[:, None] * freq, col[:, None] * freq
    cos_h = jnp.concatenate([jnp.cos(ang_r)] * 2 + [jnp.cos(ang_c)] * 2, axis=1)
    sin_h = jnp.concatenate([-jnp.sin(ang_r), jnp.sin(ang_r), -jnp.sin(ang_c), jnp.sin(ang_c)], axis=1)
    return jnp.tile(cos_h, (1, 2)), jnp.tile(sin_h, (1, 2))


def _block_table(counts, n_blocks):
    cnt = counts.astype(jnp.int32)
    padded = (cnt + MOE_BLOCK - 1) // MOE_BLOCK * MOE_BLOCK
    pad_end = jnp.cumsum(padded)
    pad_start = pad_end - padded
    blk_start = jnp.arange(n_blocks, dtype=jnp.int32)[:, None] * MOE_BLOCK
    be = jnp.minimum(jnp.sum((pad_end[None, :] <= blk_start).astype(jnp.int32), axis=1), N_EXPERTS - 1)
    ids = jnp.arange(N_EXPERTS, dtype=jnp.int32)
    mine = be[:, None] == ids[None, :]
    fill = jnp.sum(jnp.where(mine, cnt[None, :] + pad_start[None, :], 0), axis=1) - blk_start[:, 0]
    bn = jnp.clip(fill, 0, MOE_BLOCK)
    prev = jnp.concatenate([jnp.full((1,), -1, jnp.int32), be[:-1]])
    first = ((bn > 0) & (be != prev)).astype(jnp.int32)
    slot = (jnp.cumsum(first) - 1) % WEIGHT_SLOTS
    later = (ids[None, :] > ids[:, None]) & (cnt[None, :] > 0)
    nxt_e = jnp.min(jnp.where(later, ids[None, :], N_EXPERTS), axis=1)
    hop = nxt_e[:, None] == ids[None, :]
    nxt2_e = jnp.sum(jnp.where(hop, nxt_e[None, :], 0), axis=1) + jnp.where(nxt_e == N_EXPERTS, N_EXPERTS, 0)
    lookup = lambda tab: jnp.sum(jnp.where(mine, jnp.where(tab >= N_EXPERTS, -1, tab)[None, :], 0), axis=1)
    last_step = jnp.maximum(jnp.sum((bn > 0).astype(jnp.int32)) - 1, 0) // MOE_STEP_BLOCKS
    step_idx = jnp.minimum(jnp.arange(n_blocks // MOE_STEP_BLOCKS, dtype=jnp.int32), last_step)
    return (be, bn, first, slot.astype(jnp.int32), lookup(nxt_e).astype(jnp.int32),
            lookup(nxt2_e).astype(jnp.int32), step_idx.astype(jnp.int32))


def _moe(route_t, h2_tiles, w_gate, w_up, w_down, layer):
    rows = route_t.shape[1]
    tm = ROW_TILE
    n_blocks = -(-(2 * rows) // MOE_BLOCK) + N_EXPERTS
    n_blocks = -(-n_blocks // MOE_STEP_BLOCKS) * MOE_STEP_BLOCKS
    dest8, counts = _plan(route_t, tm)
    dest = dest8[:, 0:2, :].reshape(rows // tm, 1, 2 * tm)
    table = _block_table(counts[:, 0], n_blocks)
    xs = _dispatch(dest, h2_tiles, n_blocks * MOE_BLOCK, tm)
    ys = _experts(table, xs, w_gate, w_up, w_down, layer)
    tc = COMBINE_TILE
    dest_c = dest8[:, 0:2, :].reshape(rows // tm, 2, tm // tc, tc).transpose(0, 2, 1, 3).reshape(rows // tc, 1, 2 * tc)
    return dest_c, ys


def kernel(x, c, ctx, c_ctx, w_ada, b_ada, w_in, w_fourier, attn_sink, w_pool, pool_scale, w_sgu, b_sgu,
           w_out, ln1_g, ln1_b, w_router_group, w_router_expert, w_exp_gate, w_exp_up, w_exp_down,
           ln2_g, ln2_b):
    b, s, d = x.shape
    n_ctx = ctx.shape[1]
    n_layers = w_in.shape[0]
    tm = ROW_TILE
    assert n_layers == DEPTH and s == FFT_R * FFT_R and s % tm == 0 and b + 1 <= SUBLANES
    assert (b * (s + n_ctx)) % tm == 0 and n_ctx % SGU_CHUNK == 0
    cond = jnp.concatenate([c, c_ctx[None, :], jnp.zeros((SUBLANES - b - 1, d), F32)], axis=0)
    mod_all = _ada(cond, w_ada, b_ada[:, None, :])
    cos_t, sin_t = _rope_tables(s)
    x2 = x.reshape(b * s, d)
    c2 = ctx.reshape(b * n_ctx, d)
    n_sgu = w_sgu.shape[1]
    for layer in range(n_layers):
        last = layer == n_layers - 1
        mod = mod_all[layer]
        w_in_l = w_in[layer].astype(BF16)
        wf = w_fourier[layer].astype(BF16)
        w_pool_bd = jax.scipy.linalg.block_diag(*[w_pool[layer, g] for g in range(w_pool.shape[1])]).astype(BF16)
        w_sgu_stack = w_sgu[layer].reshape(n_sgu * SGU_CHUNK, SGU_CHUNK).astype(BF16)
        b_sgu_exp = jnp.repeat(b_sgu[layer].T, GROUP_W // n_sgu, axis=1)
        w_router = jnp.concatenate([w_router_expert[layer].reshape(d, N_EXPERTS), w_router_group[layer]], axis=1)
        w_router = jnp.pad(w_router, ((0, 0), (0, LANES - w_router.shape[1])))
        wr_hi = w_router.astype(BF16)
        w_route = jnp.concatenate([wr_hi, (w_router - wr_hi.astype(F32)).astype(BF16)], axis=1)
        merge_consts = (mod, w_pool_bd, pool_scale[layer][None, :], w_sgu_stack, b_sgu_exp,
                        w_out[layer].astype(BF16), ln1_g[layer][None, :], ln1_b[layer][None, :], w_route)
        sink = attn_sink[layer]

        a, q, qs, k, v, p, ug = _proj(x2, mod, w_in_l, cos_t, sin_t, mod_row=None, rows_per_batch=s,
                                      rope=True, tm=tm, a_pitch=FFT_PITCH)
        if last:
            kc, vc = _proj_kv(c2, mod, w_in_l[:, 2 * GROUP_W:2 * GROUP_W + 2 * KV_W], mod_row=b)
        else:
            ac, qc, qsc, kc, vc, pc, ugc = _proj(c2, mod, w_in_l, cos_t, sin_t, mod_row=b, rows_per_batch=n_ctx,
                                                 rope=False, tm=b * n_ctx, a_pitch=FFT_R)
        y_four = _fourier(a, wf, s)
        y_attn = _attention(sink, q, qs, k, v, kc, vc, seq=s, n_ctx=n_ctx, band=True)
        total = b * s + (0 if last else b * n_ctx)
        merged = _merge(x2, p, ug, y_four, y_attn, *merge_consts, (),
                        mod_row=None, seq=s, tm=tm, row_off=0, total_rows=total)
        if not last:
            yc_four = _fourier_small(ac, wf, n_ctx)
            yc_attn = _attention(sink, qc, qsc, kc, vc, kc, vc, seq=n_ctx, n_ctx=n_ctx, band=False)
            merged = _merge(c2, pc, ugc, yc_four, yc_attn, *merge_consts, tuple(merged),
                            mod_row=b, seq=n_ctx, tm=n_ctx, row_off=b * s, total_rows=total)
        x1, h2_tiles, route, route_t = merged
        dest, ys = _moe(route_t, h2_tiles, w_exp_gate, w_exp_up, w_exp_down, layer)
        ln_g, ln_b = ln2_g[layer][None, :], ln2_b[layer][None, :]
        x2 = _combine(dest, x1, route, mod, ln_g, ln_b, ys, row_off=0, rows=b * s,
                      mod_row=None, rows_per_batch=s)
        if not last:
            c2 = _combine(dest, x1, route, mod, ln_g, ln_b, ys, row_off=b * s,
                          rows=b * n_ctx, mod_row=b, rows_per_batch=n_ctx)
    return x2.reshape(b, s, d)
```
